```python
import jax
import jax.numpy as jnp
from jax import lax
import numpy as np

D_MODEL = 1024
BATCH = 8
SEQ = 4096
DEPTH = 1

ATT_HEADS = 8
ATT_KV_HEADS = 2
ATT_GROUP = ATT_HEADS // ATT_KV_HEADS
ATT_HEAD_DIM = 64
WINDOW = 128
ATT_BLOCK = 128
ROPE_DIM = ATT_HEAD_DIM // 4
ROPE_THETA = 500000.0

RET_HEADS = 4
RET_KEY_DIM = 128
RET_VAL_DIM = 256
RET_CHUNK = 128
RET_ROT_BASE = 10000.0

D_FF = 4 * D_MODEL

NORM_EPS = 1e-6
GN_EPS = 1e-6
NEG_INF = -1e30

ATT_Q_W = ATT_HEADS * ATT_HEAD_DIM
ATT_KV_W = ATT_KV_HEADS * ATT_HEAD_DIM
RET_QK_W = RET_HEADS * RET_KEY_DIM
RET_V_W = RET_HEADS * RET_VAL_DIM
IN_SPLITS = (ATT_Q_W, ATT_KV_W, ATT_KV_W, RET_QK_W, RET_QK_W, RET_V_W, RET_V_W, D_MODEL, D_MODEL)
IN_WIDTH = sum(IN_SPLITS)
IN_OFFSETS = tuple(int(v) for v in np.cumsum(IN_SPLITS)[:-1])

kernel_name = "hybrid_swa_sink_retention_gated_block"


def rmsnorm(x, gain):
    xf = x.astype(jnp.float32)
    y = xf * lax.rsqrt(jnp.mean(xf * xf, axis=-1, keepdims=True) + NORM_EPS)
    return (y * gain.astype(jnp.float32)).astype(x.dtype)


def rope_tables(seq_len, dim, theta):
    pos = jnp.arange(seq_len, dtype=jnp.float32)
    inv_freq = theta ** (-jnp.arange(0, dim, 2, dtype=jnp.float32) / dim)
    ang = pos[:, None] * inv_freq[None, :]
    return jnp.cos(ang)[:, None, :], jnp.sin(ang)[:, None, :]


def apply_rope(x, cos, sin):
    xf = x.astype(jnp.float32)
    x1, x2 = jnp.split(xf, 2, axis=-1)
    out = jnp.concatenate([x1 * cos - x2 * sin, x2 * cos + x1 * sin], axis=-1)
    return out.astype(x.dtype)


def partial_rope(x, cos, sin):
    return jnp.concatenate([apply_rope(x[..., :ROPE_DIM], cos, sin), x[..., ROPE_DIM:]], axis=-1)


def sliding_window_sink_attention(q, k, v, sinks):
    b, s, _, dh = q.shape
    c = ATT_BLOCK
    nb = s // c
    qb = q.reshape(b, nb, c, ATT_KV_HEADS, ATT_GROUP, dh)
    kb = k.reshape(b, nb, c, ATT_KV_HEADS, dh)
    vb = v.reshape(b, nb, c, ATT_KV_HEADS, dh)

    def with_prev(t):
        prev = jnp.concatenate([jnp.zeros_like(t[:, :1]), t[:, :-1]], axis=1)
        return jnp.concatenate([prev, t], axis=2)

    kw, vw = with_prev(kb), with_prev(vb)
    scores = jnp.einsum('bnqhgd,bnkhd->bnhgqk', qb, kw,
                        preferred_element_type=jnp.float32) * (dh ** -0.5)
    qi = jnp.arange(c)[:, None] + c
    kj = jnp.arange(2 * c)[None, :]
    delta = qi - kj
    in_window = (delta >= 0) & (delta < WINDOW)
    has_prev = (jnp.arange(nb) > 0)[:, None, None] | (kj >= c)[None]
    mask = in_window[None] & has_prev
    scores = jnp.where(mask[None, :, None, None], scores, NEG_INF)
    sink = sinks.astype(jnp.float32).reshape(ATT_KV_HEADS, ATT_GROUP)[None, None, :, :, None, None]
    sink = jnp.broadcast_to(sink, scores.shape[:-1] + (1,))
    probs = jax.nn.softmax(jnp.concatenate([scores, sink], axis=-1), axis=-1)[..., :-1]
    out = jnp.einsum('bnhgqk,bnkhd->bnqhgd', probs.astype(v.dtype), vw)
    return out.reshape(b, s, ATT_HEADS * dh)


def chunkwise_retention(q, k, v):
    b, s, h, dk = q.shape
    dv = v.shape[-1]
    c = RET_CHUNK
    nc = s // c
    log_gamma = jnp.log1p(-jnp.exp2(-5.0 - jnp.arange(h, dtype=jnp.float32)))
    idx = jnp.arange(c, dtype=jnp.float32)
    diff = idx[:, None] - idx[None, :]
    intra = jnp.where(diff >= 0, jnp.exp(jnp.maximum(diff, 0.0) * log_gamma[:, None, None]), 0.0)
    q_decay = jnp.exp((idx + 1.0)[None, :] * log_gamma[:, None])[..., None]
    k_decay = jnp.exp((c - 1.0 - idx)[None, :] * log_gamma[:, None])[..., None]
    chunk_decay = jnp.exp(c * log_gamma)[:, None, None]

    def to_chunks(t):
        return t.astype(jnp.float32).reshape(b, nc, c, h, t.shape[-1]).transpose(1, 0, 3, 2, 4)

    qc, kc, vc = to_chunks(q), to_chunks(k), to_chunks(v)

    def step(state, inp):
        qi, ki, vi = inp
        att = jnp.einsum('bhqd,bhkd->bhqk', qi, ki) * intra
        inner = jnp.einsum('bhqk,bhkv->bhqv', att, vi)
        cross = jnp.einsum('bhqd,bhdv->bhqv', qi * q_decay, state)
        new_state = state * chunk_decay + jnp.einsum('bhkd,bhkv->bhdv', ki * k_decay, vi)
        return new_state, inner + cross

    state0 = jnp.zeros((b, h, dk, dv), jnp.float32)
    _, out = lax.scan(step, state0, (qc, kc, vc))
    return out.transpose(1, 0, 3, 2, 4).reshape(b, s, h, dv)


def head_groupnorm(y, gain):
    b, s, h, dv = y.shape
    mu = jnp.mean(y, axis=-1, keepdims=True)
    var = jnp.mean(jnp.square(y - mu), axis=-1, keepdims=True)
    yn = (y - mu) * lax.rsqrt(var + GN_EPS)
    return yn.reshape(b, s, h * dv) * gain.astype(jnp.float32)


def mixer_block(xn, w_in, b_gates, attn_sinks, ret_gn_gain, w_att_up, w_ret_up, w_out, att_rope, ret_rope):
    b, s, _ = xn.shape
    proj = jnp.einsum('bsd,dn->bsn', xn, w_in)
    q_a, k_a, v_a, q_r, k_r, v_r, g_r, gate_a, gate_r = jnp.split(proj, IN_OFFSETS, axis=-1)

    q_a = partial_rope(q_a.reshape(b, s, ATT_HEADS, ATT_HEAD_DIM), *att_rope)
    k_a = partial_rope(k_a.reshape(b, s, ATT_KV_HEADS, ATT_HEAD_DIM), *att_rope)
    v_a = v_a.reshape(b, s, ATT_KV_HEADS, ATT_HEAD_DIM)
    y_a = sliding_window_sink_attention(q_a, k_a, v_a, attn_sinks) @ w_att_up

    q_r = apply_rope(q_r.reshape(b, s, RET_HEADS, RET_KEY_DIM), *ret_rope)
    k_r = apply_rope(k_r.reshape(b, s, RET_HEADS, RET_KEY_DIM), *ret_rope) * (RET_KEY_DIM ** -0.5)
    ret = chunkwise_retention(q_r, k_r, v_r.reshape(b, s, RET_HEADS, RET_VAL_DIM))
    ret = head_groupnorm(ret, ret_gn_gain)
    y_r = (jax.nn.silu(g_r.astype(jnp.float32)) * ret).astype(xn.dtype) @ w_ret_up

    bg = b_gates.astype(jnp.float32)
    ga = jax.nn.sigmoid(gate_a.astype(jnp.float32) + bg[:D_MODEL])
    gr = jax.nn.sigmoid(gate_r.astype(jnp.float32) + bg[D_MODEL:])
    merged = (ga * y_a.astype(jnp.float32) + gr * y_r.astype(jnp.float32)).astype(xn.dtype)
    return merged @ w_out


def squared_relu_mlp(xn, w_ff1, w_ff2):
    hdn = jnp.square(jax.nn.relu(xn @ w_ff1))
    return hdn @ w_ff2


def _fwd_setup_inputs(seed: int = 0) -> dict:
    key = jax.random.key(seed)
    ks = jax.random.split(key, 14)
    f32 = jnp.float32

    def normal(k, shape, scale):
        return jax.random.normal(k, shape, f32) * scale

    def gain(k, shape):
        return 1.0 + 0.02 * jax.random.normal(k, shape, f32)

    return {
        'x': normal(ks[0], (BATCH, SEQ, D_MODEL), 1.0),
        'norm_mix_gain': gain(ks[1], (DEPTH, D_MODEL)),
        'w_in': normal(ks[2], (DEPTH, D_MODEL, IN_WIDTH), D_MODEL ** -0.5),
        'b_gates': normal(ks[3], (DEPTH, 2 * D_MODEL), 0.1),
        'attn_sinks': normal(ks[4], (DEPTH, ATT_HEADS), 0.5),
        'ret_gn_gain': gain(ks[5], (DEPTH, RET_V_W)),
        'w_att_up': normal(ks[6], (DEPTH, ATT_Q_W, D_MODEL), ATT_Q_W ** -0.5),
        'w_ret_up': normal(ks[7], (DEPTH, RET_V_W, D_MODEL), RET_V_W ** -0.5),
        'w_out': normal(ks[8], (DEPTH, D_MODEL, D_MODEL), D_MODEL ** -0.5),
        'norm_mlp_gain': gain(ks[9], (DEPTH, D_MODEL)),
        'w_ff1': normal(ks[10], (DEPTH, D_MODEL, D_FF), D_MODEL ** -0.5),
        'w_ff2': normal(ks[11], (DEPTH, D_FF, D_MODEL), D_FF ** -0.5),
        'norm_final_gain': gain(ks[12], (D_MODEL,)),
    }


def _fwd_reference(x, norm_mix_gain, w_in, b_gates, attn_sinks, ret_gn_gain, w_att_up, w_ret_up, w_out,
              norm_mlp_gain, w_ff1, w_ff2, norm_final_gain):
    s = x.shape[1]
    att_rope = rope_tables(s, ROPE_DIM, ROPE_THETA)
    ret_rope = rope_tables(s, RET_KEY_DIM, RET_ROT_BASE)
    h = x
    for l in range(DEPTH):
        h = h + mixer_block(rmsnorm(h, norm_mix_gain[l]), w_in[l], b_gates[l], attn_sinks[l],
                            ret_gn_gain[l], w_att_up[l], w_ret_up[l], w_out[l], att_rope, ret_rope)
        h = h + squared_relu_mlp(rmsnorm(h, norm_mlp_gain[l]), w_ff1[l], w_ff2[l])
    return rmsnorm(h, norm_final_gain)


import jax as _jax
import jax.numpy as _jnp

TWIN_FORMAT = 'train_step'
FWD_PARAMS = ['x', 'norm_mix_gain', 'w_in', 'b_gates', 'attn_sinks', 'ret_gn_gain', 'w_att_up', 'w_ret_up', 'w_out', 'norm_mlp_gain', 'w_ff1', 'w_ff2', 'norm_final_gain']
TWIN_WEIGHTS = ['norm_mix_gain', 'w_in', 'b_gates', 'attn_sinks', 'ret_gn_gain', 'w_att_up', 'w_ret_up', 'w_out', 'norm_mlp_gain', 'w_ff1', 'w_ff2', 'norm_final_gain']
TWIN_DIFF_INPUT = 'x'
TWIN_INPUTS = ['x', 'norm_mix_gain', 'w_in', 'b_gates', 'attn_sinks', 'ret_gn_gain', 'w_att_up', 'w_ret_up', 'w_out', 'norm_mlp_gain', 'w_ff1', 'w_ff2', 'norm_final_gain', 'loss_target', 'm_norm_mix_gain', 'm_w_in', 'm_b_gates', 'm_attn_sinks', 'm_ret_gn_gain', 'm_w_att_up', 'm_w_ret_up', 'm_w_out', 'm_norm_mlp_gain', 'm_w_ff1', 'm_w_ff2', 'm_norm_final_gain', 'v_norm_mix_gain', 'v_w_in', 'v_b_gates', 'v_attn_sinks', 'v_ret_gn_gain', 'v_w_att_up', 'v_w_ret_up', 'v_w_out', 'v_norm_mlp_gain', 'v_w_ff1', 'v_w_ff2', 'v_norm_final_gain']
TWIN_OUTPUTS = ['loss', 'grad_x', 'grad_norm_mix_gain', 'grad_w_in', 'grad_b_gates', 'grad_attn_sinks', 'grad_ret_gn_gain', 'grad_w_att_up', 'grad_w_ret_up', 'grad_w_out', 'grad_norm_mlp_gain', 'grad_w_ff1', 'grad_w_ff2', 'grad_norm_final_gain', 'delta_norm_mix_gain', 'delta_w_in', 'delta_b_gates', 'delta_attn_sinks', 'delta_ret_gn_gain', 'delta_w_att_up', 'delta_w_ret_up', 'delta_w_out', 'delta_norm_mlp_gain', 'delta_w_ff1', 'delta_w_ff2', 'delta_norm_final_gain', 'new_m_norm_mix_gain', 'new_m_w_in', 'new_m_b_gates', 'new_m_attn_sinks', 'new_m_ret_gn_gain', 'new_m_w_att_up', 'new_m_w_ret_up', 'new_m_w_out', 'new_m_norm_mlp_gain', 'new_m_w_ff1', 'new_m_w_ff2', 'new_m_norm_final_gain', 'new_v_norm_mix_gain', 'new_v_w_in', 'new_v_b_gates', 'new_v_attn_sinks', 'new_v_ret_gn_gain', 'new_v_w_att_up', 'new_v_w_ret_up', 'new_v_w_out', 'new_v_norm_mlp_gain', 'new_v_w_ff1', 'new_v_w_ff2', 'new_v_norm_final_gain']
TWIN_LEAF_KINDS = {'loss': 'loss', 'grad_x': 'grad_x', 'grad_norm_mix_gain': 'grad_w', 'grad_w_in': 'grad_w', 'grad_b_gates': 'grad_w', 'grad_attn_sinks': 'grad_w', 'grad_ret_gn_gain': 'grad_w', 'grad_w_att_up': 'grad_w', 'grad_w_ret_up': 'grad_w', 'grad_w_out': 'grad_w', 'grad_norm_mlp_gain': 'grad_w', 'grad_w_ff1': 'grad_w', 'grad_w_ff2': 'grad_w', 'grad_norm_final_gain': 'grad_w', 'delta_norm_mix_gain': 'delta_w', 'delta_w_in': 'delta_w', 'delta_b_gates': 'delta_w', 'delta_attn_sinks': 'delta_w', 'delta_ret_gn_gain': 'delta_w', 'delta_w_att_up': 'delta_w', 'delta_w_ret_up': 'delta_w', 'delta_w_out': 'delta_w', 'delta_norm_mlp_gain': 'delta_w', 'delta_w_ff1': 'delta_w', 'delta_w_ff2': 'delta_w', 'delta_norm_final_gain': 'delta_w', 'new_m_norm_mix_gain': 'new_m', 'new_m_w_in': 'new_m', 'new_m_b_gates': 'new_m', 'new_m_attn_sinks': 'new_m', 'new_m_ret_gn_gain': 'new_m', 'new_m_w_att_up': 'new_m', 'new_m_w_ret_up': 'new_m', 'new_m_w_out': 'new_m', 'new_m_norm_mlp_gain': 'new_m', 'new_m_w_ff1': 'new_m', 'new_m_w_ff2': 'new_m', 'new_m_norm_final_gain': 'new_m', 'new_v_norm_mix_gain': 'new_v', 'new_v_w_in': 'new_v', 'new_v_b_gates': 'new_v', 'new_v_attn_sinks': 'new_v', 'new_v_ret_gn_gain': 'new_v', 'new_v_w_att_up': 'new_v', 'new_v_w_ret_up': 'new_v', 'new_v_w_out': 'new_v', 'new_v_norm_mlp_gain': 'new_v', 'new_v_w_ff1': 'new_v', 'new_v_w_ff2': 'new_v', 'new_v_norm_final_gain': 'new_v'}


def _forward(args):
    return _fwd_reference(*[args[k] for k in FWD_PARAMS])


def _output_shape():
    out = _jax.eval_shape(lambda: _forward(_fwd_setup_inputs(0)))
    return out.shape, out.dtype

N_MICROBATCH = 1
ADAM_LR = 0.001
ADAM_B1 = 0.9
ADAM_B2 = 0.999
ADAM_EPS = 1e-08
ADAM_WD = 0.01
ADAM_STEP = 10
PER_EXAMPLE_BATCH_AXIS = {'x': 0, 'loss_target': 0}
SHARED_INPUTS = []
_WEIGHT_DTYPES = {'norm_mix_gain': _jnp.float32, 'w_in': _jnp.float32, 'b_gates': _jnp.float32, 'attn_sinks': _jnp.float32, 'ret_gn_gain': _jnp.float32, 'w_att_up': _jnp.float32, 'w_ret_up': _jnp.float32, 'w_out': _jnp.float32, 'norm_mlp_gain': _jnp.float32, 'w_ff1': _jnp.float32, 'w_ff2': _jnp.float32, 'norm_final_gain': _jnp.float32}
MOMENT_SCALE = {'norm_mix_gain': 1.297158e-01, 'w_in': 5.259960e-02, 'b_gates': 1.701215e-02, 'attn_sinks': 1.561658e-02, 'ret_gn_gain': 5.929404e-02, 'w_att_up': 2.037333e-02, 'w_ret_up': 5.972776e-02, 'w_out': 6.252361e-02, 'norm_mlp_gain': 1.571177e-01, 'w_ff1': 7.787162e-02, 'w_ff2': 1.454567e-01, 'norm_final_gain': 3.233224e+01}


def _to_microbatches(a, axis):
    t = _jnp.moveaxis(a, axis, 0)
    t = t.reshape((N_MICROBATCH, t.shape[0] // N_MICROBATCH) + t.shape[1:])
    return _jnp.moveaxis(t, 1, axis + 1)


def setup_inputs(seed: int = 0) -> dict:
    inp = _fwd_setup_inputs(seed)
    key = _jax.random.fold_in(_jax.random.key(seed), 7919)
    shape, _ = _output_shape()
    out = dict(inp)
    out["loss_target"] = _jax.random.normal(_jax.random.fold_in(key, 0), shape, _jnp.float32)
    for i, name in enumerate(TWIN_WEIGHTS):
        w = inp[name].astype(_jnp.float32)
        if MOMENT_SCALE is None:
            s = _jnp.sqrt(_jnp.mean(_jnp.square(w)) + 1e-30)
        else:
            s = MOMENT_SCALE[name]
        km, kv = _jax.random.split(_jax.random.fold_in(key, i + 1))
        out[name] = w
        out["m_" + name] = s * _jax.random.normal(km, w.shape, _jnp.float32)
        out["v_" + name] = (s * s) * _jax.random.uniform(kv, w.shape, _jnp.float32, 0.5, 1.5)
    if N_MICROBATCH > 1:
        for name, axis in PER_EXAMPLE_BATCH_AXIS.items():
            out[name] = _to_microbatches(out[name], axis)
    return {'x': out['x'], 'norm_mix_gain': out['norm_mix_gain'], 'w_in': out['w_in'], 'b_gates': out['b_gates'], 'attn_sinks': out['attn_sinks'], 'ret_gn_gain': out['ret_gn_gain'], 'w_att_up': out['w_att_up'], 'w_ret_up': out['w_ret_up'], 'w_out': out['w_out'], 'norm_mlp_gain': out['norm_mlp_gain'], 'w_ff1': out['w_ff1'], 'w_ff2': out['w_ff2'], 'norm_final_gain': out['norm_final_gain'], 'loss_target': out['loss_target'], 'm_norm_mix_gain': out['m_norm_mix_gain'], 'm_w_in': out['m_w_in'], 'm_b_gates': out['m_b_gates'], 'm_attn_sinks': out['m_attn_sinks'], 'm_ret_gn_gain': out['m_ret_gn_gain'], 'm_w_att_up': out['m_w_att_up'], 'm_w_ret_up': out['m_w_ret_up'], 'm_w_out': out['m_w_out'], 'm_norm_mlp_gain': out['m_norm_mlp_gain'], 'm_w_ff1': out['m_w_ff1'], 'm_w_ff2': out['m_w_ff2'], 'm_norm_final_gain': out['m_norm_final_gain'], 'v_norm_mix_gain': out['v_norm_mix_gain'], 'v_w_in': out['v_w_in'], 'v_b_gates': out['v_b_gates'], 'v_attn_sinks': out['v_attn_sinks'], 'v_ret_gn_gain': out['v_ret_gn_gain'], 'v_w_att_up': out['v_w_att_up'], 'v_w_ret_up': out['v_w_ret_up'], 'v_w_out': out['v_w_out'], 'v_norm_mlp_gain': out['v_norm_mlp_gain'], 'v_w_ff1': out['v_w_ff1'], 'v_w_ff2': out['v_w_ff2'], 'v_norm_final_gain': out['v_norm_final_gain']}


def _loss(weights, diff, rest, loss_target):
    with _jax.named_scope("forward"):
        args = {**rest, TWIN_DIFF_INPUT: diff, **{k: w.astype(_WEIGHT_DTYPES[k]) for k, w in weights.items()}}
        y = _forward(args)
    with _jax.named_scope("loss_head"):
        err = _jnp.square(y.astype(_jnp.float32) - loss_target)
        return 0.5 * _jnp.sum(_jnp.mean(err, axis=-1)) if err.ndim else 0.5 * err


def _adamw(w, g, m, v):
    m = ADAM_B1 * m + (1.0 - ADAM_B1) * g
    v = ADAM_B2 * v + (1.0 - ADAM_B2) * _jnp.square(g)
    m_hat = m / (1.0 - ADAM_B1 ** ADAM_STEP)
    v_hat = v / (1.0 - ADAM_B2 ** ADAM_STEP)
    delta = -ADAM_LR * (m_hat / (_jnp.sqrt(v_hat) + ADAM_EPS) + ADAM_WD * w)
    return delta, m, v


def reference(x, norm_mix_gain, w_in, b_gates, attn_sinks, ret_gn_gain, w_att_up, w_ret_up, w_out, norm_mlp_gain, w_ff1, w_ff2, norm_final_gain, loss_target, m_norm_mix_gain, m_w_in, m_b_gates, m_attn_sinks, m_ret_gn_gain, m_w_att_up, m_w_ret_up, m_w_out, m_norm_mlp_gain, m_w_ff1, m_w_ff2, m_norm_final_gain, v_norm_mix_gain, v_w_in, v_b_gates, v_attn_sinks, v_ret_gn_gain, v_w_att_up, v_w_ret_up, v_w_out, v_norm_mlp_gain, v_w_ff1, v_w_ff2, v_norm_final_gain):
    given = dict(x=x, norm_mix_gain=norm_mix_gain, w_in=w_in, b_gates=b_gates, attn_sinks=attn_sinks, ret_gn_gain=ret_gn_gain, w_att_up=w_att_up, w_ret_up=w_ret_up, w_out=w_out, norm_mlp_gain=norm_mlp_gain, w_ff1=w_ff1, w_ff2=w_ff2, norm_final_gain=norm_final_gain, loss_target=loss_target, m_norm_mix_gain=m_norm_mix_gain, m_w_in=m_w_in, m_b_gates=m_b_gates, m_attn_sinks=m_attn_sinks, m_ret_gn_gain=m_ret_gn_gain, m_w_att_up=m_w_att_up, m_w_ret_up=m_w_ret_up, m_w_out=m_w_out, m_norm_mlp_gain=m_norm_mlp_gain, m_w_ff1=m_w_ff1, m_w_ff2=m_w_ff2, m_norm_final_gain=m_norm_final_gain, v_norm_mix_gain=v_norm_mix_gain, v_w_in=v_w_in, v_b_gates=v_b_gates, v_attn_sinks=v_attn_sinks, v_ret_gn_gain=v_ret_gn_gain, v_w_att_up=v_w_att_up, v_w_ret_up=v_w_ret_up, v_w_out=v_w_out, v_norm_mlp_gain=v_norm_mlp_gain, v_w_ff1=v_w_ff1, v_w_ff2=v_w_ff2, v_norm_final_gain=v_norm_final_gain)
    weights = {n: given[n] for n in TWIN_WEIGHTS}
    shared = {n: given[n] for n in SHARED_INPUTS}
    per_example = {n: given[n] for n in ['x']}
    grad_fn = _jax.value_and_grad(_loss, argnums=(0, 1))

    def one_microbatch(ex, loss_target):
        ex = dict(ex)
        diff = ex.pop(TWIN_DIFF_INPUT)
        return grad_fn(weights, diff, {**shared, **ex}, loss_target)

    if N_MICROBATCH == 1:
        loss, (grad_w, grad_x) = one_microbatch(per_example, given["loss_target"])
    else:
        def body(carry, xs):
            loss_sum, grad_sum = carry
            l_k, (gw_k, gx_k) = one_microbatch(xs[0], xs[1])
            with _jax.named_scope("update"):
                return (loss_sum + l_k, _jax.tree.map(_jnp.add, grad_sum, gw_k)), gx_k

        init = (_jnp.zeros((), _jnp.float32), _jax.tree.map(_jnp.zeros_like, weights))
        (loss, grad_w), grad_x = _jax.lax.scan(body, init, (per_example, given["loss_target"]))
    with _jax.named_scope("update"):
        delta_w, new_m, new_v = {}, {}, {}
        for n in TWIN_WEIGHTS:
            delta_w[n], new_m[n], new_v[n] = _adamw(weights[n], grad_w[n], given["m_" + n], given["v_" + n])
    return (loss, grad_x, *[grad_w[n] for n in TWIN_WEIGHTS], *[delta_w[n] for n in TWIN_WEIGHTS],
            *[new_m[n] for n in TWIN_WEIGHTS], *[new_v[n] for n in TWIN_WEIGHTS])
```

```python
import functools
import math

import jax
import jax.numpy as jnp
from jax import lax
from jax.experimental import pallas as pl
from jax.experimental.pallas import tpu as pltpu

F32 = jnp.float32
BF = jnp.bfloat16
MESH = pl.DeviceIdType.MESH

D_MODEL = 1024
ATT_HEADS = 8
ATT_HEAD_DIM = 64
ATT_BLOCK = 128
ROPE_DIM = 16
ROPE_THETA = 500000.0
RET_HEADS = 4
RET_KEY_DIM = 128
RET_VAL_DIM = 256
RET_CHUNK = 128
RET_ROT_BASE = 10000.0
D_FF = 4096
NORM_EPS = 1e-6
GN_EPS = 1e-6
NEG_INF = -1e30
ATT_SCALE = ATT_HEAD_DIM ** -0.5
RET_SCALE = RET_KEY_DIM ** -0.5

C_QA, C_KA, C_VA, C_QR, C_KR, C_VR, C_GR, C_GATES, C_END = 0, 512, 640, 768, 1280, 1792, 2816, 3840, 5888

ADAM_LR = 0.001
ADAM_B1 = 0.9
ADAM_B2 = 0.999
ADAM_EPS = 1e-08
ADAM_WD = 0.01
ADAM_STEP = 10

N_DEV = 8
LANE = 128
VMEM_LIMIT = 56 * 1024 * 1024
TOKEN_TILE = 256
TN_TOKEN_TILE = 512

RET_LOG_GAMMA = tuple(math.log1p(-(2.0 ** (-5.0 - h))) for h in range(RET_HEADS))
RET_CHUNK_DECAY = tuple(math.exp(RET_CHUNK * lg) for lg in RET_LOG_GAMMA)

VMEM_FULL = pl.BlockSpec(memory_space=pltpu.VMEM)
SMEM_FULL = pl.BlockSpec(memory_space=pltpu.SMEM)
ANY = pl.BlockSpec(memory_space=pl.ANY)


def _params(n_axes):
    return pltpu.CompilerParams(dimension_semantics=("arbitrary",) * n_axes, vmem_limit_bytes=VMEM_LIMIT)


def _nn(a, b):
    return jnp.dot(a, b, preferred_element_type=F32)


def _nt(a, b):
    return lax.dot_general(a, b, (((1,), (1,)), ((), ())), preferred_element_type=F32)


def _tn(a, b):
    return lax.dot_general(a, b, (((0,), (0,)), ((), ())), preferred_element_type=F32)


def _sigmoid(v):
    return 1.0 / (1.0 + jnp.exp(-v))


def _rows(tile, width):
    return pl.BlockSpec((tile, width), lambda i: (i, 0))


def _slabs(v, fn):
    return jnp.concatenate([fn(v[:, LANE * j:LANE * (j + 1)]) for j in range(v.shape[1] // LANE)], axis=1)


def _rope_att(v, ca, sa, sb):
    return _slabs(v, lambda t: t * ca + pltpu.roll(t, LANE - 8, 1) * sa + pltpu.roll(t, 8, 1) * sb)


def _rope_att_t(v, ca, sa, sb):
    return _slabs(v, lambda t: t * ca + pltpu.roll(t * sa, 8, 1) + pltpu.roll(t * sb, LANE - 8, 1))


def _rope_ret(v, cr, sr):
    return _slabs(v, lambda t: t * cr + pltpu.roll(t, 64, 1) * sr)


def _rope_ret_t(v, cr, sr):
    return _slabs(v, lambda t: t * cr + pltpu.roll(t * sr, 64, 1))


def _rope_tables(seq):
    pos = jnp.arange(seq, dtype=F32)

    def cs(dim, theta):
        inv = theta ** (-jnp.arange(0, dim, 2, dtype=F32) / dim)
        ang = pos[:, None] * inv[None, :]
        return jnp.cos(ang), jnp.sin(ang)

    ca, sa_ = cs(ROPE_DIM, ROPE_THETA)
    zeros8 = jnp.zeros_like(ca)
    rest = ATT_HEAD_DIM - ROPE_DIM
    c64 = jnp.concatenate([ca, ca, jnp.ones((seq, rest), F32)], axis=1)
    sa64 = jnp.concatenate([-sa_, zeros8, jnp.zeros((seq, rest), F32)], axis=1)
    sb64 = jnp.concatenate([zeros8, sa_, jnp.zeros((seq, rest), F32)], axis=1)
    cr, sr_ = cs(RET_KEY_DIM, RET_ROT_BASE)
    return (jnp.tile(c64, (1, 2)), jnp.tile(sa64, (1, 2)), jnp.tile(sb64, (1, 2)),
            jnp.concatenate([cr, cr], axis=1), jnp.concatenate([-sr_, sr_], axis=1))


def _ret_tables():
    c = RET_CHUNK
    lg = jnp.asarray(RET_LOG_GAMMA, F32)
    idx = jnp.arange(c, dtype=F32)
    diff = idx[:, None] - idx[None, :]
    intra = jnp.where(diff >= 0, jnp.exp(jnp.maximum(diff, 0.0) * lg[:, None, None]), 0.0)
    qd = jnp.exp((idx + 1.0)[None, :] * lg[:, None])[..., None]
    kd = jnp.exp((c - 1.0 - idx)[None, :] * lg[:, None])[..., None]
    return intra, jnp.broadcast_to(qd, (RET_HEADS, c, RET_KEY_DIM)), jnp.broadcast_to(kd, (RET_HEADS, c, RET_KEY_DIM))


def _inproj_fwd(x, g1, w_in, tabs):
    seq = x.shape[0]
    tm = min(TOKEN_TILE, seq)

    def body(x_ref, g_ref, w_ref, ca_ref, sa_ref, sb_ref, cr_ref, sr_ref,
             xn_ref, qkv_ref, qkr_ref, vr_ref, gr_ref, gates_ref):
        xf = x_ref[...]
        r = lax.rsqrt(jnp.mean(xf * xf, axis=-1, keepdims=True) + NORM_EPS)
        xn = (xf * r * g_ref[...]).astype(BF)
        xn_ref[...] = xn
        pa = _nn(xn, w_ref[:, C_QA:C_QR])
        qk = _rope_att(pa[:, :C_VA], ca_ref[...], sa_ref[...], sb_ref[...])
        qkv_ref[:, :C_VA] = qk.astype(BF)
        qkv_ref[:, C_VA:] = pa[:, C_VA:].astype(BF)
        cr, sr = cr_ref[...], sr_ref[...]
        qr = _rope_ret(_nn(xn, w_ref[:, C_QR:C_KR]), cr, sr)
        qkr_ref[:, :512] = qr.astype(BF)
        kr = _rope_ret(_nn(xn, w_ref[:, C_KR:C_VR]), cr, sr) * RET_SCALE
        qkr_ref[:, 512:] = kr.astype(BF)
        vr_ref[...] = _nn(xn, w_ref[:, C_VR:C_GR]).astype(BF)
        gr_ref[...] = _nn(xn, w_ref[:, C_GR:C_GATES]).astype(BF)
        gates_ref[...] = _nn(xn, w_ref[:, C_GATES:C_END]).astype(BF)

    tab = _rows(tm, LANE)
    return pl.pallas_call(
        body, name="inproj_fwd", grid=(seq // tm,),
        in_specs=[_rows(tm, D_MODEL), VMEM_FULL, VMEM_FULL, tab, tab, tab, tab, tab],
        out_specs=[_rows(tm, D_MODEL), _rows(tm, 768), _rows(tm, 1024), _rows(tm, 1024), _rows(tm, 1024),
                   _rows(tm, 2048)],
        out_shape=[jax.ShapeDtypeStruct((seq, w), BF) for w in (D_MODEL, 768, 1024, 1024, 1024, 2048)],
        compiler_params=_params(1),
    )(x, g1, w_in, *tabs)


def _kv_variants(w):
    lo = lax.broadcasted_iota(jnp.int32, w.shape, 1) < ATT_HEAD_DIM
    w0 = jnp.where(lo, w, 0.0)
    w1 = jnp.where(lo, 0.0, w)
    return {(0, 0): w0.astype(BF), (0, 1): pltpu.roll(w0, 64, 1).astype(BF),
            (1, 1): w1.astype(BF), (1, 0): pltpu.roll(w1, 64, 1).astype(BF)}


def _att_mask(i):
    c = ATT_BLOCK
    qi = lax.broadcasted_iota(jnp.int32, (c, 2 * c), 0)
    kj = lax.broadcasted_iota(jnp.int32, (c, 2 * c), 1)
    return (kj > qi) & (kj <= qi + c) & ((kj >= c) | (i > 0))


def _att_probs(qp, k_e, allowed, sk):
    s = _nt(qp, k_e) * ATT_SCALE
    s = jnp.where(allowed, s, NEG_INF)
    m = jnp.maximum(jnp.max(s, axis=1, keepdims=True), sk)
    pe = jnp.exp(s - m)
    inv = 1.0 / (jnp.sum(pe, axis=1, keepdims=True) + jnp.exp(sk - m))
    return pe * inv, jnp.exp(sk - m) * inv


def _attn_fwd(qkv, sinks):
    seq = qkv.shape[0]
    c = ATT_BLOCK
    nb = seq // c

    def body(sink_ref, q_ref, kvc_ref, kvp_ref, o_ref):
        i = pl.program_id(0)
        kc, kp = kvc_ref[...], kvp_ref[...]
        kw = jnp.concatenate([kp[:, :LANE], kc[:, :LANE]], axis=0).astype(F32)
        vw = jnp.concatenate([kp[:, LANE:], kc[:, LANE:]], axis=0).astype(F32)
        kk, vv = _kv_variants(kw), _kv_variants(vw)
        allowed = _att_mask(i)
        for p in range(4):
            qp = q_ref[:, LANE * p:LANE * (p + 1)]
            acc = jnp.zeros((c, LANE), F32)
            for e in (0, 1):
                prob, _ = _att_probs(qp, kk[(p // 2, e)], allowed, sink_ref[2 * p + e])
                acc = acc + _nn(prob.astype(BF), vv[(p // 2, e)])
            o_ref[:, LANE * p:LANE * (p + 1)] = acc.astype(BF)

    return pl.pallas_call(
        body, name="attn_fwd", grid=(nb,),
        in_specs=[SMEM_FULL,
                  pl.BlockSpec((c, 512), lambda i: (i, 0)),
                  pl.BlockSpec((c, 256), lambda i: (i, 2)),
                  pl.BlockSpec((c, 256), lambda i: (jnp.maximum(i - 1, 0), 2))],
        out_specs=pl.BlockSpec((c, 512), lambda i: (i, 0)),
        out_shape=jax.ShapeDtypeStruct((seq, 512), BF),
        compiler_params=_params(1),
    )(sinks, qkv, qkv, qkv)


def _ret_fwd(qkr, vr, gr, gain, rtabs):
    seq = qkr.shape[0]
    c = RET_CHUNK
    nc = seq // c
    dk, dv = RET_KEY_DIM, RET_VAL_DIM

    def body(qk_ref, v_ref, g_ref, gain_ref, intra_ref, qd_ref, kd_ref, ret_ref, yrp_ref, st_ref, state):
        @pl.when(pl.program_id(0) == 0)
        def _():
            state[...] = jnp.zeros_like(state)

        for h in range(RET_HEADS):
            qh = qk_ref[:, dk * h:dk * (h + 1)]
            kh = qk_ref[:, 512 + dk * h:512 + dk * (h + 1)]
            vh = v_ref[:, dv * h:dv * (h + 1)]
            sh = state[h]
            shb = sh.astype(BF)
            st_ref[0, h] = shb
            att = _nt(qh, kh) * intra_ref[h]
            inner = _nn(att.astype(BF), vh)
            cross = _nn((qh.astype(F32) * qd_ref[h]).astype(BF), shb)
            out = inner + cross
            state[h] = sh * RET_CHUNK_DECAY[h] + _tn((kh.astype(F32) * kd_ref[h]).astype(BF), vh)
            ret_ref[:, dv * h:dv * (h + 1)] = out
            mu = jnp.mean(out, axis=-1, keepdims=True)
            d = out - mu
            var = jnp.mean(d * d, axis=-1, keepdims=True)
            y = d * lax.rsqrt(var + GN_EPS) * gain_ref[:, dv * h:dv * (h + 1)]
            g = g_ref[:, dv * h:dv * (h + 1)].astype(F32)
            yrp_ref[:, dv * h:dv * (h + 1)] = (g * _sigmoid(g) * y).astype(BF)

    return pl.pallas_call(
        body, name="ret_fwd", grid=(nc,),
        in_specs=[_rows(c, 1024), _rows(c, 1024), _rows(c, 1024), VMEM_FULL, VMEM_FULL, VMEM_FULL, VMEM_FULL],
        out_specs=[_rows(c, 1024), _rows(c, 1024), pl.BlockSpec((1, RET_HEADS, dk, dv), lambda i: (i, 0, 0, 0))],
        out_shape=[jax.ShapeDtypeStruct((seq, 1024), F32), jax.ShapeDtypeStruct((seq, 1024), BF),
                   jax.ShapeDtypeStruct((nc, RET_HEADS, dk, dv), BF)],
        scratch_shapes=[pltpu.VMEM((RET_HEADS, dk, dv), F32)],
        compiler_params=_params(1),
    )(qkr, vr, gr, gain, *rtabs)


def _mix_fwd(oa, yrp, gates, bg, x, w_att, w_ret, w_out):
    seq = x.shape[0]
    tm = min(TOKEN_TILE, seq)

    def body(oa_ref, yrp_ref, gates_ref, bg_ref, x_ref, wa_ref, wr_ref, wo_ref, ya_ref, yr_ref, mg_ref, h1_ref):
        ya = _nn(oa_ref[...], wa_ref[...])
        yr = _nn(yrp_ref[...], wr_ref[...])
        gt = _sigmoid(gates_ref[...].astype(F32) + bg_ref[...])
        merged = (gt[:, :D_MODEL] * ya + gt[:, D_MODEL:] * yr).astype(BF)
        ya_ref[...] = ya.astype(BF)
        yr_ref[...] = yr.astype(BF)
        mg_ref[...] = merged
        h1_ref[...] = x_ref[...] + _nn(merged, wo_ref[...])

    return pl.pallas_call(
        body, name="mix_fwd", grid=(seq // tm,),
        in_specs=[_rows(tm, 512), _rows(tm, 1024), _rows(tm, 2048), VMEM_FULL, _rows(tm, D_MODEL),
                  VMEM_FULL, VMEM_FULL, VMEM_FULL],
        out_specs=[_rows(tm, D_MODEL)] * 4,
        out_shape=[jax.ShapeDtypeStruct((seq, D_MODEL), BF)] * 3 + [jax.ShapeDtypeStruct((seq, D_MODEL), F32)],
        compiler_params=_params(1),
    )(oa, yrp, gates, bg, x, w_att, w_ret, w_out)


def _mlp_fwd_loss(h1, g2, g3, target, w_ff1, w_ff2):
    seq = h1.shape[0]
    tm = min(TOKEN_TILE, seq)

    def body(h1_ref, g2_ref, g3_ref, t_ref, w1_ref, w2_ref, xn2_ref, u_ref, dh2_ref, loss_ref, dg3_ref):
        @pl.when(pl.program_id(0) == 0)
        def _():
            loss_ref[...] = jnp.zeros_like(loss_ref)
            dg3_ref[...] = jnp.zeros_like(dg3_ref)

        h1v = h1_ref[...]
        r2 = lax.rsqrt(jnp.mean(h1v * h1v, axis=-1, keepdims=True) + NORM_EPS)
        xn2 = (h1v * r2 * g2_ref[...]).astype(BF)
        xn2_ref[...] = xn2
        u = _nn(xn2, w1_ref[...])
        u_ref[...] = u.astype(BF)
        hdn = jnp.square(jnp.maximum(u, 0.0)).astype(BF)
        h2 = h1v + _nn(hdn, w2_ref[...])
        r3 = lax.rsqrt(jnp.mean(h2 * h2, axis=-1, keepdims=True) + NORM_EPS)
        hn = h2 * r3
        err = hn * g3_ref[...] - t_ref[...]
        loss_ref[...] += jnp.sum(err * err) * (0.5 / D_MODEL)
        dy = err * (1.0 / D_MODEL)
        dg3_ref[...] += jnp.sum(dy * hn, axis=0, keepdims=True)
        z = dy * g3_ref[...]
        dh2_ref[...] = r3 * (z - hn * jnp.mean(z * hn, axis=-1, keepdims=True))

    return pl.pallas_call(
        body, name="mlp_fwd_loss", grid=(seq // tm,),
        in_specs=[_rows(tm, D_MODEL), VMEM_FULL, VMEM_FULL, _rows(tm, D_MODEL), VMEM_FULL, VMEM_FULL],
        out_specs=[_rows(tm, D_MODEL), _rows(tm, D_FF), _rows(tm, D_MODEL),
                   pl.BlockSpec((1, LANE), lambda i: (0, 0)), pl.BlockSpec((1, D_MODEL), lambda i: (0, 0))],
        out_shape=[jax.ShapeDtypeStruct((seq, D_MODEL), BF), jax.ShapeDtypeStruct((seq, D_FF), BF),
                   jax.ShapeDtypeStruct((seq, D_MODEL), F32), jax.ShapeDtypeStruct((1, LANE), F32),
                   jax.ShapeDtypeStruct((1, D_MODEL), F32)],
        compiler_params=_params(1),
    )(h1, g2, g3, target, w_ff1, w_ff2)


def _rms_bwd(dxn, xin, gain):
    r = lax.rsqrt(jnp.mean(xin * xin, axis=-1, keepdims=True) + NORM_EPS)
    xhat = xin * r
    z = dxn * gain
    dxin = r * (z - xhat * jnp.mean(z * xhat, axis=-1, keepdims=True))
    return dxin, jnp.sum(dxn * xhat, axis=0, keepdims=True)


def _mlp_bwd(dh2, u, h1, g2, w_ff1, w_ff2):
    seq = h1.shape[0]
    tm = min(TOKEN_TILE, seq)

    def body(dh2_ref, u_ref, h1_ref, g2_ref, w1_ref, w2_ref, du_ref, dh1_ref, dg2_ref):
        @pl.when(pl.program_id(0) == 0)
        def _():
            dg2_ref[...] = jnp.zeros_like(dg2_ref)

        dh2v = dh2_ref[...]
        dhdn = _nt(dh2v.astype(BF), w2_ref[...])
        du = (dhdn * (2.0 * jnp.maximum(u_ref[...].astype(F32), 0.0))).astype(BF)
        du_ref[...] = du
        dxn2 = _nt(du, w1_ref[...])
        dnorm, dg = _rms_bwd(dxn2, h1_ref[...], g2_ref[...])
        dh1_ref[...] = dh2v + dnorm
        dg2_ref[...] += dg

    return pl.pallas_call(
        body, name="mlp_bwd", grid=(seq // tm,),
        in_specs=[_rows(tm, D_MODEL), _rows(tm, D_FF), _rows(tm, D_MODEL), VMEM_FULL, VMEM_FULL, VMEM_FULL],
        out_specs=[_rows(tm, D_FF), _rows(tm, D_MODEL), pl.BlockSpec((1, D_MODEL), lambda i: (0, 0))],
        out_shape=[jax.ShapeDtypeStruct((seq, D_FF), BF), jax.ShapeDtypeStruct((seq, D_MODEL), F32),
                   jax.ShapeDtypeStruct((1, D_MODEL), F32)],
        compiler_params=_params(1),
    )(dh2, u, h1, g2, w_ff1, w_ff2)


def _mm_tn(a, b, name, relu_sq=False):
    seq, kdim = a.shape
    ndim = b.shape[1]
    ts = min(TN_TOKEN_TILE, seq)
    tk = min(kdim, 1024)
    tn = min(ndim, 1024)
    n_steps = seq // ts

    def body(a_ref, b_ref, o_ref):
        @pl.when(pl.program_id(2) == 0)
        def _():
            o_ref[...] = jnp.zeros_like(o_ref)

        av = a_ref[...]
        if relu_sq:
            av = jnp.square(jnp.maximum(av.astype(F32), 0.0))
        o_ref[...] += _tn(av.astype(BF), b_ref[...].astype(BF))

    return pl.pallas_call(
        body, name=name, grid=(kdim // tk, ndim // tn, n_steps),
        in_specs=[pl.BlockSpec((ts, tk), lambda k, n, s: (s, k)), pl.BlockSpec((ts, tn), lambda k, n, s: (s, n))],
        out_specs=pl.BlockSpec((tk, tn), lambda k, n, s: (k, n)),
        out_shape=jax.ShapeDtypeStruct((kdim, ndim), F32),
        compiler_params=_params(3),
    )(a, b)


def _mix_bwd(dh1, ya, yr, gates, bg, w_att, w_ret, w_out):
    seq = dh1.shape[0]
    tm = min(TOKEN_TILE, seq)

    def body(dh1_ref, ya_ref, yr_ref, gates_ref, bg_ref, wa_ref, wr_ref, wo_ref,
             dya_ref, dyr_ref, dgates_ref, doa_ref, dyrp_ref, db_ref):
        @pl.when(pl.program_id(0) == 0)
        def _():
            db_ref[...] = jnp.zeros_like(db_ref)

        dm = _nt(dh1_ref[...].astype(BF), wo_ref[...])
        gt = _sigmoid(gates_ref[...].astype(F32) + bg_ref[...])
        ga, gr = gt[:, :D_MODEL], gt[:, D_MODEL:]
        dya = (dm * ga).astype(BF)
        dyr = (dm * gr).astype(BF)
        dya_ref[...] = dya
        dyr_ref[...] = dyr
        dga = dm * ya_ref[...].astype(F32) * ga * (1.0 - ga)
        dgr = dm * yr_ref[...].astype(F32) * gr * (1.0 - gr)
        dgates_ref[:, :D_MODEL] = dga.astype(BF)
        dgates_ref[:, D_MODEL:] = dgr.astype(BF)
        db_ref[:, :D_MODEL] += jnp.sum(dga, axis=0, keepdims=True)
        db_ref[:, D_MODEL:] += jnp.sum(dgr, axis=0, keepdims=True)
        doa_ref[...] = _nt(dya, wa_ref[...]).astype(BF)
        dyrp_ref[...] = _nt(dyr, wr_ref[...]).astype(BF)

    return pl.pallas_call(
        body, name="mix_bwd", grid=(seq // tm,),
        in_specs=[_rows(tm, D_MODEL), _rows(tm, D_MODEL), _rows(tm, D_MODEL), _rows(tm, 2048), VMEM_FULL,
                  VMEM_FULL, VMEM_FULL, VMEM_FULL],
        out_specs=[_rows(tm, D_MODEL), _rows(tm, D_MODEL), _rows(tm, 2048), _rows(tm, 512), _rows(tm, 1024),
                   pl.BlockSpec((1, 2048), lambda i: (0, 0))],
        out_shape=[jax.ShapeDtypeStruct((seq, D_MODEL), BF), jax.ShapeDtypeStruct((seq, D_MODEL), BF),
                   jax.ShapeDtypeStruct((seq, 2048), BF), jax.ShapeDtypeStruct((seq, 512), BF),
                   jax.ShapeDtypeStruct((seq, 1024), BF), jax.ShapeDtypeStruct((1, 2048), F32)],
        compiler_params=_params(1),
    )(dh1, ya, yr, gates, bg, w_att, w_ret, w_out)


def _ret_bwd(qkr, vr, gr, ret, dyrp, states, gain, rtabs, cr, sr):
    seq = qkr.shape[0]
    c = RET_CHUNK
    nc = seq // c
    dk, dv = RET_KEY_DIM, RET_VAL_DIM

    def body(qk_ref, v_ref, g_ref, ret_ref, dyp_ref, st_ref, gain_ref, intra_ref, qd_ref, kd_ref, cr_ref, sr_ref,
             dall_ref, dgain_ref, dstate):
        @pl.when(pl.program_id(0) == 0)
        def _():
            dstate[...] = jnp.zeros_like(dstate)
            dgain_ref[...] = jnp.zeros_like(dgain_ref)

        crv, srv = cr_ref[...], sr_ref[...]
        for h in range(RET_HEADS):
            vs = slice(dv * h, dv * (h + 1))
            qh = qk_ref[:, dk * h:dk * (h + 1)]
            kh = qk_ref[:, 512 + dk * h:512 + dk * (h + 1)]
            vh = v_ref[:, vs]
            out = ret_ref[:, vs]
            g = g_ref[:, vs].astype(F32)
            dyp = dyp_ref[:, vs].astype(F32)
            gain_h = gain_ref[:, vs]
            mu = jnp.mean(out, axis=-1, keepdims=True)
            d = out - mu
            rstd = lax.rsqrt(jnp.mean(d * d, axis=-1, keepdims=True) + GN_EPS)
            yn = d * rstd
            sg = _sigmoid(g)
            dg = dyp * (yn * gain_h) * (sg * (1.0 + g * (1.0 - sg)))
            dy = dyp * (g * sg)
            dgain_ref[:, vs] += jnp.sum(dy * yn, axis=0, keepdims=True)
            dyn = dy * gain_h
            dout = rstd * (dyn - jnp.mean(dyn, axis=-1, keepdims=True)
                           - yn * jnp.mean(dyn * yn, axis=-1, keepdims=True))
            doutb = dout.astype(BF)
            sc = st_ref[0, h]
            dsp = dstate[h]
            dspb = dsp.astype(BF)
            intra, qdv, kdv = intra_ref[h], qd_ref[h], kd_ref[h]
            att = _nt(qh, kh) * intra
            dab = (_nt(doutb, vh) * intra).astype(BF)
            qdec = (qh.astype(F32) * qdv).astype(BF)
            kdec = (kh.astype(F32) * kdv).astype(BF)
            dq = _nn(dab, kh) + _nt(doutb, sc) * qdv
            dkk = _tn(dab, qh) + _nt(vh, dspb) * kdv
            dvv = _tn(att.astype(BF), doutb) + _nn(kdec, dspb)
            dstate[h] = dsp * RET_CHUNK_DECAY[h] + _tn(qdec, doutb)
            dall_ref[:, dk * h:dk * (h + 1)] = _rope_ret_t(dq, crv, srv).astype(BF)
            dall_ref[:, 512 + dk * h:512 + dk * (h + 1)] = (_rope_ret_t(dkk, crv, srv) * RET_SCALE).astype(BF)
            dall_ref[:, 1024 + dv * h:1024 + dv * (h + 1)] = dvv.astype(BF)
            dall_ref[:, 2048 + dv * h:2048 + dv * (h + 1)] = dg.astype(BF)

    def rev(width):
        return pl.BlockSpec((c, width), lambda i: (nc - 1 - i, 0))

    return pl.pallas_call(
        body, name="ret_bwd", grid=(nc,),
        in_specs=[rev(1024), rev(1024), rev(1024), rev(1024), rev(1024),
                  pl.BlockSpec((1, RET_HEADS, dk, dv), lambda i: (nc - 1 - i, 0, 0, 0)),
                  VMEM_FULL, VMEM_FULL, VMEM_FULL, VMEM_FULL, rev(LANE), rev(LANE)],
        out_specs=[rev(3072), pl.BlockSpec((1, 1024), lambda i: (0, 0))],
        out_shape=[jax.ShapeDtypeStruct((seq, 3072), BF), jax.ShapeDtypeStruct((1, 1024), F32)],
        scratch_shapes=[pltpu.VMEM((RET_HEADS, dk, dv), F32)],
        compiler_params=_params(1),
    )(qkr, vr, gr, ret, dyrp, states, gain, *rtabs, cr, sr)


def _attn_bwd(qkv, doa, sinks, ca, sa, sb):
    seq = qkv.shape[0]
    c = ATT_BLOCK
    nb = seq // c

    def body(sink_ref, q_ref, kvc_ref, kvp_ref, do_ref, cac_ref, sac_ref, sbc_ref, cap_ref, sap_ref, sbp_ref,
             dq_ref, dkv_ref, dsink_ref, carry):
        i = pl.program_id(0)

        @pl.when(i == 0)
        def _():
            carry[...] = jnp.zeros_like(carry)
            dsink_ref[...] = jnp.zeros_like(dsink_ref)

        def flush(total):
            dk_pre = _rope_att_t(total[:, :LANE], cap_ref[...], sap_ref[...], sbp_ref[...])
            dkv_ref[:, :LANE] = dk_pre.astype(BF)
            dkv_ref[:, LANE:] = total[:, LANE:].astype(BF)

        @pl.when(i < nb)
        def _():
            kc, kp = kvc_ref[...], kvp_ref[...]
            kw = jnp.concatenate([kp[:, :LANE], kc[:, :LANE]], axis=0).astype(F32)
            vw = jnp.concatenate([kp[:, LANE:], kc[:, LANE:]], axis=0).astype(F32)
            kk, vv = _kv_variants(kw), _kv_variants(vw)
            allowed = _att_mask(i)
            lane1 = lax.broadcasted_iota(jnp.int32, (1, LANE), 1)
            gk = {key: jnp.zeros((2 * c, LANE), F32) for key in kk}
            gv = {key: jnp.zeros((2 * c, LANE), F32) for key in kk}
            dsink = jnp.zeros((1, LANE), F32)
            dqs = []
            for p in range(4):
                qp = q_ref[:, LANE * p:LANE * (p + 1)]
                dop = do_ref[:, LANE * p:LANE * (p + 1)]
                dqp = jnp.zeros((c, LANE), F32)
                for e in (0, 1):
                    key = (p // 2, e)
                    prob, psink = _att_probs(qp, kk[key], allowed, sink_ref[2 * p + e])
                    dprob = _nt(dop, vv[key])
                    drow = jnp.sum(dprob * prob, axis=1, keepdims=True)
                    ds = (prob * (dprob - drow) * ATT_SCALE).astype(BF)
                    dqp = dqp + _nn(ds, kk[key])
                    gk[key] = gk[key] + _tn(ds, qp)
                    gv[key] = gv[key] + _tn(prob.astype(BF), dop)
                    dsink = dsink + jnp.where(lane1 == 2 * p + e, -jnp.sum(psink * drow), 0.0)
                dqs.append(dqp)
            dq = jnp.concatenate(dqs, axis=1)
            dq_ref[...] = _rope_att_t(dq, cac_ref[...], sac_ref[...], sbc_ref[...]).astype(BF)
            dsink_ref[...] += dsink
            lo = lax.broadcasted_iota(jnp.int32, (2 * c, LANE), 1) < ATT_HEAD_DIM

            def fold(gg):
                return jnp.where(lo, gg[(0, 0)] + pltpu.roll(gg[(0, 1)], 64, 1),
                                 gg[(1, 1)] + pltpu.roll(gg[(1, 0)], 64, 1))

            dkw, dvw = fold(gk), fold(gv)
            flush(carry[...] + jnp.concatenate([dkw[:c], dvw[:c]], axis=1))
            carry[...] = jnp.concatenate([dkw[c:], dvw[c:]], axis=1)

        @pl.when(i == nb)
        def _():
            flush(carry[...])

    cur = lambda i: (jnp.minimum(i, nb - 1), 0)
    prev = lambda i: (jnp.maximum(i - 1, 0), 0)
    return pl.pallas_call(
        body, name="attn_bwd", grid=(nb + 1,),
        in_specs=[SMEM_FULL,
                  pl.BlockSpec((c, 512), cur),
                  pl.BlockSpec((c, 256), lambda i: (jnp.minimum(i, nb - 1), 2)),
                  pl.BlockSpec((c, 256), lambda i: (jnp.clip(i - 1, 0, nb - 1), 2)),
                  pl.BlockSpec((c, 512), cur),
                  pl.BlockSpec((c, LANE), cur), pl.BlockSpec((c, LANE), cur), pl.BlockSpec((c, LANE), cur),
                  pl.BlockSpec((c, LANE), prev), pl.BlockSpec((c, LANE), prev), pl.BlockSpec((c, LANE), prev)],
        out_specs=[pl.BlockSpec((c, 512), cur), pl.BlockSpec((c, 256), prev),
                   pl.BlockSpec((1, LANE), lambda i: (0, 0))],
        out_shape=[jax.ShapeDtypeStruct((seq, 512), BF), jax.ShapeDtypeStruct((seq, 256), BF),
                   jax.ShapeDtypeStruct((1, LANE), F32)],
        scratch_shapes=[pltpu.VMEM((c, 256), F32)],
        compiler_params=_params(1),
    )(sinks, qkv, qkv, qkv, doa, ca, sa, sb, ca, sa, sb)


def _inproj_bwd(dqa, dkva, dret, dgates, dh1, x, g1, w_in):
    seq = x.shape[0]
    tm = min(TOKEN_TILE, seq)

    def body(dqa_ref, dkva_ref, dret_ref, dgates_ref, dh1_ref, x_ref, g_ref, w_ref, dx_ref, dg1_ref):
        @pl.when(pl.program_id(0) == 0)
        def _():
            dg1_ref[...] = jnp.zeros_like(dg1_ref)

        dxn = (_nt(dqa_ref[...], w_ref[:, C_QA:C_KA]) + _nt(dkva_ref[...], w_ref[:, C_KA:C_QR])
               + _nt(dret_ref[...], w_ref[:, C_QR:C_GATES]) + _nt(dgates_ref[...], w_ref[:, C_GATES:C_END]))
        dnorm, dg = _rms_bwd(dxn, x_ref[...], g_ref[...])
        dx_ref[...] = dh1_ref[...] + dnorm
        dg1_ref[...] += dg

    return pl.pallas_call(
        body, name="inproj_bwd", grid=(seq // tm,),
        in_specs=[_rows(tm, 512), _rows(tm, 256), _rows(tm, 3072), _rows(tm, 2048), _rows(tm, D_MODEL),
                  _rows(tm, D_MODEL), VMEM_FULL, VMEM_FULL],
        out_specs=[_rows(tm, D_MODEL), pl.BlockSpec((1, D_MODEL), lambda i: (0, 0))],
        out_shape=[jax.ShapeDtypeStruct((seq, D_MODEL), F32), jax.ShapeDtypeStruct((1, D_MODEL), F32)],
        compiler_params=_params(1),
    )(dqa, dkva, dret, dgates, dh1, x, g1, w_in)


def _local_step(x, target, g1, bg, sinks, gain, g2, g3, w_in, w_att, w_ret, w_out, w_ff1, w_ff2):
    seq = x.shape[0]
    ca, sa, sb, cr, sr = _rope_tables(seq)
    rtabs = _ret_tables()

    xn1, qkv, qkr, vr, gr, gates = _inproj_fwd(x, g1, w_in, (ca, sa, sb, cr, sr))
    oa = _attn_fwd(qkv, sinks)
    ret, yrp, states = _ret_fwd(qkr, vr, gr, gain, rtabs)
    ya, yr, merged, h1 = _mix_fwd(oa, yrp, gates, bg, x, w_att, w_ret, w_out)
    xn2, u, dh2, loss, dg3 = _mlp_fwd_loss(h1, g2, g3, target, w_ff1, w_ff2)

    du, dh1, dg2 = _mlp_bwd(dh2, u, h1, g2, w_ff1, w_ff2)
    d_ff2 = _mm_tn(u, dh2, "dw_ff2", relu_sq=True)
    d_ff1 = _mm_tn(xn2, du, "dw_ff1")
    dya, dyr, dgates, doa, dyrp, db = _mix_bwd(dh1, ya, yr, gates, bg, w_att, w_ret, w_out)
    d_out = _mm_tn(merged, dh1, "dw_out")
    d_att = _mm_tn(oa, dya, "dw_att_up")
    d_ret = _mm_tn(yrp, dyr, "dw_ret_up")
    dret, dgain = _ret_bwd(qkr, vr, gr, ret, dyrp, states, gain, rtabs, cr, sr)
    dqa, dkva, dsink = _attn_bwd(qkv, doa, sinks, ca, sa, sb)
    dx, dg1 = _inproj_bwd(dqa, dkva, dret, dgates, dh1, x, g1, w_in)
    d_in = jnp.concatenate([_mm_tn(xn1, dqa, "dw_in_qa"), _mm_tn(xn1, dkva, "dw_in_kva"),
                            _mm_tn(xn1, dret, "dw_in_ret"), _mm_tn(xn1, dgates, "dw_in_gates")], axis=1)
    grads = dict(w_in=d_in, w_att_up=d_att, w_ret_up=d_ret, w_out=d_out, w_ff1=d_ff1, w_ff2=d_ff2)
    small = dict(norm_mix_gain=dg1, b_gates=db, attn_sinks=dsink, ret_gn_gain=dgain, norm_mlp_gain=dg2,
                 norm_final_gain=dg3)
    return loss, dx, grads, small


def _coords():
    return lax.axis_index("x"), lax.axis_index("y"), lax.axis_index("c")


def _flip(v, bit):
    return 1 - v if bit else v


def _allgather(shards):
    n = len(shards)

    def body(*refs):
        ins, outs = refs[:n], refs[n:2 * n]
        send_sems, recv_sems, local_sems = refs[2 * n:]
        x, y, c = _coords()
        me, sib = (x, y, c), (x, y, 1 - c)
        chips = [(1 - x, y), (x, 1 - y), (1 - x, 1 - y)]

        def slot(a, blk):
            return outs[a].at[4 * blk[0] + 2 * blk[1] + blk[2]]

        def copy(a, k, blk, to, src=None):
            return pltpu.make_async_remote_copy(
                src_ref=slot(a, blk) if src is None else src, dst_ref=slot(a, blk),
                send_sem=send_sems.at[a, k], recv_sem=recv_sems.at[a, k], device_id=to, device_id_type=MESH)

        mine = [pltpu.make_async_copy(ins[a], slot(a, me), local_sems.at[a]) for a in range(n)]
        for cp in mine:
            cp.start()
        first = []
        for a in range(n):
            first.append(copy(a, 0, me, sib, src=ins[a]))
            first += [copy(a, 1 + j, me, (*chip, c), src=ins[a]) for j, chip in enumerate(chips)]
        for cp in first:
            cp.start()
        passed = []
        for j, chip in enumerate(chips):
            for a in range(n):
                copy(a, 1 + j, (*chip, c), me).wait_recv()
                fwd = copy(a, 4 + j, (*chip, c), sib)
                fwd.start()
                passed.append(fwd)
        for a in range(n):
            copy(a, 0, sib, me).wait_recv()
            for j, chip in enumerate(chips):
                copy(a, 4 + j, (*chip, 1 - c), me).wait_recv()
        for cp in first + passed:
            cp.wait_send()
        for cp in mine:
            cp.wait()

    return pl.pallas_call(
        body, name="allgather_weights",
        in_specs=[ANY] * n, out_specs=[ANY] * n,
        out_shape=[jax.ShapeDtypeStruct((N_DEV,) + s.shape, s.dtype) for s in shards],
        scratch_shapes=[pltpu.SemaphoreType.DMA((n, 7)), pltpu.SemaphoreType.DMA((n, 7)),
                        pltpu.SemaphoreType.DMA((n,))],
    )(*shards)


def _pair_exchange(grads, small):
    n = len(grads)

    def body(*refs):
        g, small_ref = refs[:n], refs[n]
        r1, small_all = refs[n + 1:2 * n + 1], refs[2 * n + 1]
        send_sems, recv_sems, ssend, srecv, lsem = refs[2 * n + 2:]
        x, y, c = _coords()
        sib = (x, y, 1 - c)
        started = []
        for a in range(n):
            for j in range(4):
                cp = pltpu.make_async_remote_copy(
                    src_ref=g[a].at[2 * j + (1 - c)], dst_ref=r1[a].at[j],
                    send_sem=send_sems.at[a, j], recv_sem=recv_sems.at[a, j], device_id=sib, device_id_type=MESH)
                cp.start()
                started.append(cp)
        me_idx = 4 * x + 2 * y + c
        own = pltpu.make_async_copy(small_ref, small_all.at[me_idx], lsem)
        own.start()
        small_cps = []
        for r in range(1, N_DEV):
            px, py, pc = _flip(x, r & 4), _flip(y, r & 2), _flip(c, r & 1)
            cp = pltpu.make_async_remote_copy(
                src_ref=small_ref, dst_ref=small_all.at[me_idx],
                send_sem=ssend.at[r - 1], recv_sem=srecv.at[r - 1], device_id=(px, py, pc), device_id_type=MESH)
            cp.start()
            small_cps.append((cp, 4 * px + 2 * py + pc))
        for cp in started:
            cp.wait_recv()
        for r, (cp, peer_idx) in enumerate(small_cps):
            pltpu.make_async_remote_copy(
                src_ref=small_ref, dst_ref=small_all.at[peer_idx],
                send_sem=ssend.at[r], recv_sem=srecv.at[r], device_id=(x, y, c), device_id_type=MESH).wait_recv()
        for cp in started:
            cp.wait_send()
        for cp, _ in small_cps:
            cp.wait_send()
        own.wait()

    return pl.pallas_call(
        body, name="pair_exchange",
        in_specs=[ANY] * (n + 1), out_specs=[ANY] * (n + 1),
        out_shape=[jax.ShapeDtypeStruct((4,) + g.shape[1:], g.dtype) for g in grads]
        + [jax.ShapeDtypeStruct((N_DEV,) + small.shape, small.dtype)],
        scratch_shapes=[pltpu.SemaphoreType.DMA((n, 4)), pltpu.SemaphoreType.DMA((n, 4)),
                        pltpu.SemaphoreType.DMA((N_DEV - 1,)), pltpu.SemaphoreType.DMA((N_DEV - 1,)),
                        pltpu.SemaphoreType.DMA],
    )(*grads, small)


def _pair_sum(grads, r1s, c_arr):
    n = len(grads)
    q = 4

    def body(c_ref, *refs):
        g, r, t = refs[:n], refs[n:2 * n], refs[2 * n:]
        for a in range(n):
            t[a][...] = (g[a][...] + r[a][...]).astype(t[a].dtype)

    def blk(arr):
        return (1, arr.shape[1] // q, arr.shape[2])

    grid_spec = pltpu.PrefetchScalarGridSpec(
        num_scalar_prefetch=1, grid=(4, q),
        in_specs=[pl.BlockSpec(blk(g), lambda j, s, c_ref: (2 * j + c_ref[0], s, 0)) for g in grads]
        + [pl.BlockSpec(blk(r), lambda j, s, c_ref: (j, s, 0)) for r in r1s],
        out_specs=[pl.BlockSpec(blk(r), lambda j, s, c_ref: (j, s, 0)) for r in r1s])
    return pl.pallas_call(
        body, name="pair_sum", grid_spec=grid_spec,
        out_shape=[jax.ShapeDtypeStruct(r.shape, RS_PAYLOAD) for r in r1s],
        compiler_params=_params(2),
    )(c_arr, *grads, *r1s)


def _chip_exchange(ts):
    n = len(ts)

    def body(*refs):
        t, r2 = refs[:n], refs[n:2 * n]
        send_sems, recv_sems = refs[2 * n:]
        x, y, c = _coords()
        started = []
        for a in range(n):
            for r in range(1, 4):
                tx, ty = _flip(x, r & 2), _flip(y, r & 1)
                cp = pltpu.make_async_remote_copy(
                    src_ref=t[a].at[2 * tx + ty], dst_ref=r2[a].at[r - 1],
                    send_sem=send_sems.at[a, r - 1], recv_sem=recv_sems.at[a, r - 1],
                    device_id=(tx, ty, c), device_id_type=MESH)
                cp.start()
                started.append(cp)
        for cp in started:
            cp.wait_recv()
        for cp in started:
            cp.wait_send()

    return pl.pallas_call(
        body, name="chip_exchange",
        in_specs=[ANY] * n, out_specs=[ANY] * n,
        out_shape=[jax.ShapeDtypeStruct((3,) + t.shape[1:], t.dtype) for t in ts],
        scratch_shapes=[pltpu.SemaphoreType.DMA((n, 3)), pltpu.SemaphoreType.DMA((n, 3))],
    )(*ts)


def _adamw(w, g, m, v):
    m = ADAM_B1 * m + (1.0 - ADAM_B1) * g
    v = ADAM_B2 * v + (1.0 - ADAM_B2) * jnp.square(g)
    m_hat = m / (1.0 - ADAM_B1 ** ADAM_STEP)
    v_hat = v / (1.0 - ADAM_B2 ** ADAM_STEP)
    delta = -ADAM_LR * (m_hat / (jnp.sqrt(v_hat) + ADAM_EPS) + ADAM_WD * w)
    return delta, m, v


def _adam_big(ts, r2s, ws, ms, vs, chip_arr):
    n = len(ts)
    q = 8

    def body(j_ref, *refs):
        t, r2, w, m, v = (refs[k * n:(k + 1) * n] for k in range(5))
        outs = refs[5 * n:]
        for a in range(n):
            g = ((t[a][0].astype(F32) + r2[a][0].astype(F32)) + r2[a][1].astype(F32)) + r2[a][2].astype(F32)
            delta, nm, nv = _adamw(w[a][...], g, m[a][...], v[a][...])
            outs[4 * a][...] = g
            outs[4 * a + 1][...] = delta
            outs[4 * a + 2][...] = nm
            outs[4 * a + 3][...] = nv

    def rows(arr):
        return arr.shape[0] // q

    in_specs = ([pl.BlockSpec((1, rows(w), w.shape[1]), lambda s, j_ref: (j_ref[0], s, 0)) for w in ws]
                + [pl.BlockSpec((3, rows(w), w.shape[1]), lambda s, j_ref: (0, s, 0)) for w in ws]
                + [pl.BlockSpec((rows(w), w.shape[1]), lambda s, j_ref: (s, 0)) for w in ws] * 3)
    out_specs = [pl.BlockSpec((rows(w), w.shape[1]), lambda s, j_ref: (s, 0)) for w in ws for _ in range(4)]
    grid_spec = pltpu.PrefetchScalarGridSpec(num_scalar_prefetch=1, grid=(q,), in_specs=in_specs,
                                             out_specs=out_specs)
    return pl.pallas_call(
        body, name="adam_big", grid_spec=grid_spec,
        out_shape=[jax.ShapeDtypeStruct(w.shape, F32) for w in ws for _ in range(4)],
        compiler_params=_params(1),
    )(chip_arr, *ts, *r2s, *ws, *ms, *vs)


def _adam_small(small_all, w, m, v):
    def body(all_ref, w_ref, m_ref, v_ref, g_ref, d_ref, nm_ref, nv_ref):
        g = all_ref[0]
        for k in range(1, N_DEV):
            g = g + all_ref[k]
        delta, nm, nv = _adamw(w_ref[...], g, m_ref[...], v_ref[...])
        g_ref[...] = g
        d_ref[...] = delta
        nm_ref[...] = nm
        nv_ref[...] = nv

    return pl.pallas_call(
        body, name="adam_small",
        in_specs=[VMEM_FULL] * 4, out_specs=[VMEM_FULL] * 4,
        out_shape=[jax.ShapeDtypeStruct(w.shape, F32)] * 4,
    )(small_all, w, m, v)


RS_PAYLOAD = F32

SMALL_ROWS = ("norm_mix_gain", "b_gates_a", "b_gates_r", "ret_gn_gain", "norm_mlp_gain", "norm_final_gain",
              "attn_sinks")


def _pack_rows(rows):
    rid = lax.broadcasted_iota(jnp.int32, (8, D_MODEL), 0)
    out = jnp.zeros((8, D_MODEL), F32)
    for i, r in enumerate(rows):
        out = jnp.where(rid == i, jnp.broadcast_to(r, (8, D_MODEL)), out)
    return out


def _small_rows(norm_mix_gain, b_gates, attn_sinks, ret_gn_gain, norm_mlp_gain, norm_final_gain):
    return [norm_mix_gain, b_gates[:, :D_MODEL], b_gates[:, D_MODEL:], ret_gn_gain, norm_mlp_gain,
            norm_final_gain.reshape(1, D_MODEL), jnp.pad(attn_sinks, ((0, 0), (0, D_MODEL - ATT_HEADS)))]


def _unpack_small(p):
    return dict(norm_mix_gain=p[0:1], b_gates=jnp.concatenate([p[1:2], p[2:3]], axis=1), ret_gn_gain=p[3:4],
                norm_mlp_gain=p[4:5], norm_final_gain=p[5], attn_sinks=p[6:7, :ATT_HEADS])


WEIGHTS = ("norm_mix_gain", "w_in", "b_gates", "attn_sinks", "ret_gn_gain", "w_att_up", "w_ret_up", "w_out",
           "norm_mlp_gain", "w_ff1", "w_ff2", "norm_final_gain")
BIG = ("w_in", "w_att_up", "w_ret_up", "w_out", "w_ff1", "w_ff2")
COLUMN_SHARDED = ("w_in", "w_att_up", "w_ff1")


def kernel(x, norm_mix_gain, w_in, b_gates, attn_sinks, ret_gn_gain, w_att_up, w_ret_up, w_out, norm_mlp_gain, w_ff1, w_ff2, norm_final_gain, loss_target, m_norm_mix_gain, m_w_in, m_b_gates, m_attn_sinks, m_ret_gn_gain, m_w_att_up, m_w_ret_up, m_w_out, m_norm_mlp_gain, m_w_ff1, m_w_ff2, m_norm_final_gain, v_norm_mix_gain, v_w_in, v_b_gates, v_attn_sinks, v_ret_gn_gain, v_w_att_up, v_w_ret_up, v_w_out, v_norm_mlp_gain, v_w_ff1, v_w_ff2, v_norm_final_gain):
    w = dict(norm_mix_gain=norm_mix_gain, w_in=w_in, b_gates=b_gates, attn_sinks=attn_sinks, ret_gn_gain=ret_gn_gain,
             w_att_up=w_att_up, w_ret_up=w_ret_up, w_out=w_out, norm_mlp_gain=norm_mlp_gain, w_ff1=w_ff1,
             w_ff2=w_ff2, norm_final_gain=norm_final_gain)
    m = dict(norm_mix_gain=m_norm_mix_gain, w_in=m_w_in, b_gates=m_b_gates, attn_sinks=m_attn_sinks,
             ret_gn_gain=m_ret_gn_gain, w_att_up=m_w_att_up, w_ret_up=m_w_ret_up, w_out=m_w_out,
             norm_mlp_gain=m_norm_mlp_gain, w_ff1=m_w_ff1, w_ff2=m_w_ff2, norm_final_gain=m_norm_final_gain)
    v = dict(norm_mix_gain=v_norm_mix_gain, w_in=v_w_in, b_gates=v_b_gates, attn_sinks=v_attn_sinks,
             ret_gn_gain=v_ret_gn_gain, w_att_up=v_w_att_up, w_ret_up=v_w_ret_up, w_out=v_w_out,
             norm_mlp_gain=v_norm_mlp_gain, w_ff1=v_w_ff1, w_ff2=v_w_ff2, norm_final_gain=v_norm_final_gain)

    gathered = _allgather([w[k][0].astype(BF) for k in BIG])
    full = {}
    for k, g in zip(BIG, gathered):
        if k in COLUMN_SHARDED:
            full[k] = jnp.transpose(g, (1, 0, 2)).reshape(g.shape[1], N_DEV * g.shape[2])
        else:
            full[k] = g.reshape(N_DEV * g.shape[1], g.shape[2])

    loss_p, dx, grads, small = _local_step(
        x[0], loss_target[0], norm_mix_gain, b_gates, attn_sinks[0], ret_gn_gain, norm_mlp_gain,
        norm_final_gain.reshape(1, D_MODEL), full["w_in"], full["w_att_up"], full["w_ret_up"], full["w_out"],
        full["w_ff1"], full["w_ff2"])

    stacks = []
    for k in BIG:
        g = grads[k]
        if k in COLUMN_SHARDED:
            stacks.append(jnp.transpose(g.reshape(g.shape[0], N_DEV, g.shape[1] // N_DEV), (1, 0, 2)))
        else:
            stacks.append(g.reshape(N_DEV, g.shape[0] // N_DEV, g.shape[1]))
    lane = lax.broadcasted_iota(jnp.int32, (1, D_MODEL), 1)
    sink_row = jnp.where(lane < ATT_HEADS, jnp.pad(small["attn_sinks"], ((0, 0), (0, D_MODEL - LANE))),
                         jnp.where(lane == ATT_HEADS, jnp.pad(loss_p, ((0, 0), (0, D_MODEL - LANE)), mode="edge"),
                                   0.0))
    small_pack = _pack_rows([small["norm_mix_gain"], small["b_gates"][:, :D_MODEL], small["b_gates"][:, D_MODEL:],
                             small["ret_gn_gain"], small["norm_mlp_gain"], small["norm_final_gain"], sink_row])
    xi, yi, ci = _coords()
    *r1s, small_all = _pair_exchange(stacks, small_pack)
    ts = _pair_sum(stacks, r1s, jnp.reshape(ci, (1,)).astype(jnp.int32))
    r2s = _chip_exchange(ts)
    big_out = _adam_big(ts, r2s, [w[k][0] for k in BIG], [m[k][0] for k in BIG], [v[k][0] for k in BIG],
                        jnp.reshape(2 * xi + yi, (1,)).astype(jnp.int32))
    sm_g, sm_d, sm_m, sm_v = _adam_small(small_all, _pack_rows(_small_rows(*[w[k] for k in WEIGHTS if k not in BIG])),
                                         _pack_rows(_small_rows(*[m[k] for k in WEIGHTS if k not in BIG])),
                                         _pack_rows(_small_rows(*[v[k] for k in WEIGHTS if k not in BIG])))

    loss = sm_g[6, ATT_HEADS]
    kinds = []
    for idx, packed in enumerate((sm_g, sm_d, sm_m, sm_v)):
        out = _unpack_small(packed)
        for a, k in enumerate(BIG):
            out[k] = big_out[4 * a + idx][None]
        kinds.append(out)
    return (loss, dx[None], *[kinds[0][k] for k in WEIGHTS], *[kinds[1][k] for k in WEIGHTS],
            *[kinds[2][k] for k in WEIGHTS], *[kinds[3][k] for k in WEIGHTS])
```

```python
import functools
import math

import jax
import jax.numpy as jnp
from jax import lax
from jax.experimental import pallas as pl
from jax.experimental.pallas import tpu as pltpu

F32 = jnp.float32
BF = jnp.bfloat16
MESH = pl.DeviceIdType.MESH

D_MODEL = 1024
ATT_HEADS = 8
ATT_HEAD_DIM = 64
ATT_BLOCK = 128
ROPE_DIM = 16
ROPE_THETA = 500000.0
RET_HEADS = 4
RET_KEY_DIM = 128
RET_VAL_DIM = 256
RET_CHUNK = 128
RET_ROT_BASE = 10000.0
D_FF = 4096
NORM_EPS = 1e-6
GN_EPS = 1e-6
NEG_INF = -1e30
ATT_SCALE = ATT_HEAD_DIM ** -0.5
RET_SCALE = RET_KEY_DIM ** -0.5

C_QA, C_KA, C_VA, C_QR, C_KR, C_VR, C_GR, C_GATES, C_END = 0, 512, 640, 768, 1280, 1792, 2816, 3840, 5888

ADAM_LR = 0.001
ADAM_B1 = 0.9
ADAM_B2 = 0.999
ADAM_EPS = 1e-08
ADAM_WD = 0.01
ADAM_STEP = 10

N_DEV = 8
LANE = 128
VMEM_LIMIT = 56 * 1024 * 1024
TOKEN_TILE = 256
TN_TOKEN_TILE = 512

RET_LOG_GAMMA = tuple(math.log1p(-(2.0 ** (-5.0 - h))) for h in range(RET_HEADS))
RET_CHUNK_DECAY = tuple(math.exp(RET_CHUNK * lg) for lg in RET_LOG_GAMMA)

VMEM_FULL = pl.BlockSpec(memory_space=pltpu.VMEM)
SMEM_FULL = pl.BlockSpec(memory_space=pltpu.SMEM)
ANY = pl.BlockSpec(memory_space=pl.ANY)


def _params(n_axes):
    return pltpu.CompilerParams(dimension_semantics=("arbitrary",) * n_axes, vmem_limit_bytes=VMEM_LIMIT)


def _nn(a, b):
    return jnp.dot(a, b, preferred_element_type=F32)


def _nt(a, b):
    return lax.dot_general(a, b, (((1,), (1,)), ((), ())), preferred_element_type=F32)


def _tn(a, b):
    return lax.dot_general(a, b, (((0,), (0,)), ((), ())), preferred_element_type=F32)


def _sigmoid(v):
    return 1.0 / (1.0 + jnp.exp(-v))


def _rows(tile, width):
    return pl.BlockSpec((tile, width), lambda i: (i, 0))


class _Comm:
    def __init__(self, inputs, out_shapes, scratch, start, finish):
        self.inputs, self.out_shapes, self.scratch, self.start, self.finish = inputs, out_shapes, scratch, start, finish


def _call(body, args, comm=None, *, name, grid, in_specs, out_specs, out_shape, scratch_shapes=()):
    params = _params(len(grid))
    if comm is None:
        return pl.pallas_call(body, name=name, grid=grid, in_specs=in_specs, out_specs=out_specs, out_shape=out_shape,
                              scratch_shapes=scratch_shapes, compiler_params=params)(*args), ()
    single = not isinstance(out_specs, (list, tuple))
    out_specs_l = [out_specs] if single else list(out_specs)
    out_shape_l = [out_shape] if single else list(out_shape)
    n_in, n_out, n_scr = len(in_specs), len(out_specs_l), len(scratch_shapes)
    n_cin, n_cout = len(comm.inputs), len(comm.out_shapes)

    def hosted(*refs):
        it = iter(refs)
        ins, cin, outs, cout, scr = ([next(it) for _ in range(k)] for k in (n_in, n_cin, n_out, n_cout, n_scr))
        sems = list(it)
        ids = [pl.program_id(k) for k in range(len(grid))]
        first = functools.reduce(jnp.logical_and, [i == 0 for i in ids])
        last = functools.reduce(jnp.logical_and, [i == g - 1 for i, g in zip(ids, grid)])

        @pl.when(first)
        def _():
            comm.start(cin, cout, sems)

        body(*ins, *outs, *scr)

        @pl.when(last)
        def _():
            comm.finish(cin, cout, sems)

    res = pl.pallas_call(
        hosted, name=name, grid=grid, in_specs=list(in_specs) + [ANY] * n_cin,
        out_specs=out_specs_l + [ANY] * n_cout, out_shape=out_shape_l + list(comm.out_shapes),
        scratch_shapes=list(scratch_shapes) + list(comm.scratch), compiler_params=params)(*args, *comm.inputs)
    return (res[0] if single else res[:n_out]), res[n_out:]


def _comm_alone(comm, name):
    n_cin, n_cout = len(comm.inputs), len(comm.out_shapes)

    def body(*refs):
        cin, cout, sems = refs[:n_cin], refs[n_cin:n_cin + n_cout], refs[n_cin + n_cout:]
        comm.start(cin, cout, sems)
        comm.finish(cin, cout, sems)

    return pl.pallas_call(body, name=name, in_specs=[ANY] * n_cin, out_specs=[ANY] * n_cout,
                          out_shape=list(comm.out_shapes), scratch_shapes=list(comm.scratch))(*comm.inputs)


def _slabs(v, fn):
    return jnp.concatenate([fn(v[:, LANE * j:LANE * (j + 1)]) for j in range(v.shape[1] // LANE)], axis=1)


def _rope_att(v, ca, sa, sb):
    return _slabs(v, lambda t: t * ca + pltpu.roll(t, LANE - 8, 1) * sa + pltpu.roll(t, 8, 1) * sb)


def _rope_att_t(v, ca, sa, sb):
    return _slabs(v, lambda t: t * ca + pltpu.roll(t * sa, 8, 1) + pltpu.roll(t * sb, LANE - 8, 1))


def _rope_ret(v, cr, sr):
    return _slabs(v, lambda t: t * cr + pltpu.roll(t, 64, 1) * sr)


def _rope_ret_t(v, cr, sr):
    return _slabs(v, lambda t: t * cr + pltpu.roll(t * sr, 64, 1))


def _rope_tables(seq):
    pos = jnp.arange(seq, dtype=F32)

    def cs(dim, theta):
        inv = theta ** (-jnp.arange(0, dim, 2, dtype=F32) / dim)
        ang = pos[:, None] * inv[None, :]
        return jnp.cos(ang), jnp.sin(ang)

    ca, sa_ = cs(ROPE_DIM, ROPE_THETA)
    zeros8 = jnp.zeros_like(ca)
    rest = ATT_HEAD_DIM - ROPE_DIM
    c64 = jnp.concatenate([ca, ca, jnp.ones((seq, rest), F32)], axis=1)
    sa64 = jnp.concatenate([-sa_, zeros8, jnp.zeros((seq, rest), F32)], axis=1)
    sb64 = jnp.concatenate([zeros8, sa_, jnp.zeros((seq, rest), F32)], axis=1)
    cr, sr_ = cs(RET_KEY_DIM, RET_ROT_BASE)
    return (jnp.tile(c64, (1, 2)), jnp.tile(sa64, (1, 2)), jnp.tile(sb64, (1, 2)),
            jnp.concatenate([cr, cr], axis=1), jnp.concatenate([-sr_, sr_], axis=1))


def _ret_tables():
    c = RET_CHUNK
    lg = jnp.asarray(RET_LOG_GAMMA, F32)
    idx = jnp.arange(c, dtype=F32)
    diff = idx[:, None] - idx[None, :]
    intra = jnp.where(diff >= 0, jnp.exp(jnp.maximum(diff, 0.0) * lg[:, None, None]), 0.0)
    qd = jnp.exp((idx + 1.0)[None, :] * lg[:, None])[..., None]
    kd = jnp.exp((c - 1.0 - idx)[None, :] * lg[:, None])[..., None]
    return intra, jnp.broadcast_to(qd, (RET_HEADS, c, RET_KEY_DIM)), jnp.broadcast_to(kd, (RET_HEADS, c, RET_KEY_DIM))


def _inproj_fwd(x, g1, w_in, tabs, comm=None):
    seq = x.shape[0]
    tm = min(TOKEN_TILE, seq)

    def body(x_ref, g_ref, w_ref, ca_ref, sa_ref, sb_ref, cr_ref, sr_ref,
             xn_ref, qkv_ref, qkr_ref, vr_ref, gr_ref, gates_ref):
        xf = x_ref[...]
        r = lax.rsqrt(jnp.mean(xf * xf, axis=-1, keepdims=True) + NORM_EPS)
        xn = (xf * r * g_ref[...]).astype(BF)
        xn_ref[...] = xn
        pa = _nn(xn, w_ref[:, C_QA:C_QR])
        qk = _rope_att(pa[:, :C_VA], ca_ref[...], sa_ref[...], sb_ref[...])
        qkv_ref[:, :C_VA] = qk.astype(BF)
        qkv_ref[:, C_VA:] = pa[:, C_VA:].astype(BF)
        cr, sr = cr_ref[...], sr_ref[...]
        qr = _rope_ret(_nn(xn, w_ref[:, C_QR:C_KR]), cr, sr)
        qkr_ref[:, :512] = qr.astype(BF)
        kr = _rope_ret(_nn(xn, w_ref[:, C_KR:C_VR]), cr, sr) * RET_SCALE
        qkr_ref[:, 512:] = kr.astype(BF)
        vr_ref[...] = _nn(xn, w_ref[:, C_VR:C_GR]).astype(BF)
        gr_ref[...] = _nn(xn, w_ref[:, C_GR:C_GATES]).astype(BF)
        gates_ref[...] = _nn(xn, w_ref[:, C_GATES:C_END]).astype(BF)

    tab = _rows(tm, LANE)
    return _call(
        body, (x, g1, w_in, *tabs), comm, name="inproj_fwd", grid=(seq // tm,),
        in_specs=[_rows(tm, D_MODEL), VMEM_FULL, VMEM_FULL, tab, tab, tab, tab, tab],
        out_specs=[_rows(tm, D_MODEL), _rows(tm, 768), _rows(tm, 1024), _rows(tm, 1024), _rows(tm, 1024),
                   _rows(tm, 2048)],
        out_shape=[jax.ShapeDtypeStruct((seq, w), BF) for w in (D_MODEL, 768, 1024, 1024, 1024, 2048)])


def _kv_variants(w):
    lo = lax.broadcasted_iota(jnp.int32, w.shape, 1) < ATT_HEAD_DIM
    w0 = jnp.where(lo, w, 0.0)
    w1 = jnp.where(lo, 0.0, w)
    return {(0, 0): w0.astype(BF), (0, 1): pltpu.roll(w0, 64, 1).astype(BF),
            (1, 1): w1.astype(BF), (1, 0): pltpu.roll(w1, 64, 1).astype(BF)}


def _att_mask(i):
    c = ATT_BLOCK
    qi = lax.broadcasted_iota(jnp.int32, (c, 2 * c), 0)
    kj = lax.broadcasted_iota(jnp.int32, (c, 2 * c), 1)
    return (kj > qi) & (kj <= qi + c) & ((kj >= c) | (i > 0))


def _att_probs(qp, k_e, allowed, sk):
    s = _nt(qp, k_e) * ATT_SCALE
    s = jnp.where(allowed, s, NEG_INF)
    m = jnp.maximum(jnp.max(s, axis=1, keepdims=True), sk)
    pe = jnp.exp(s - m)
    inv = 1.0 / (jnp.sum(pe, axis=1, keepdims=True) + jnp.exp(sk - m))
    return pe * inv, jnp.exp(sk - m) * inv


def _attn_fwd(qkv, sinks, comm=None):
    seq = qkv.shape[0]
    c = ATT_BLOCK
    nb = seq // c

    def body(sink_ref, q_ref, kvc_ref, kvp_ref, o_ref):
        i = pl.program_id(0)
        kc, kp = kvc_ref[...], kvp_ref[...]
        kw = jnp.concatenate([kp[:, :LANE], kc[:, :LANE]], axis=0).astype(F32)
        vw = jnp.concatenate([kp[:, LANE:], kc[:, LANE:]], axis=0).astype(F32)
        kk, vv = _kv_variants(kw), _kv_variants(vw)
        allowed = _att_mask(i)
        for p in range(4):
            qp = q_ref[:, LANE * p:LANE * (p + 1)]
            acc = jnp.zeros((c, LANE), F32)
            for e in (0, 1):
                prob, _ = _att_probs(qp, kk[(p // 2, e)], allowed, sink_ref[2 * p + e])
                acc = acc + _nn(prob.astype(BF), vv[(p // 2, e)])
            o_ref[:, LANE * p:LANE * (p + 1)] = acc.astype(BF)

    return _call(
        body, (sinks, qkv, qkv, qkv), comm, name="attn_fwd", grid=(nb,),
        in_specs=[SMEM_FULL,
                  pl.BlockSpec((c, 512), lambda i: (i, 0)),
                  pl.BlockSpec((c, 256), lambda i: (i, 2)),
                  pl.BlockSpec((c, 256), lambda i: (jnp.maximum(i - 1, 0), 2))],
        out_specs=pl.BlockSpec((c, 512), lambda i: (i, 0)),
        out_shape=jax.ShapeDtypeStruct((seq, 512), BF))


def _ret_fwd(qkr, vr, gr, gain, rtabs, comm=None):
    seq = qkr.shape[0]
    c = RET_CHUNK
    nc = seq // c
    dk, dv = RET_KEY_DIM, RET_VAL_DIM

    def body(qk_ref, v_ref, g_ref, gain_ref, intra_ref, qd_ref, kd_ref, ret_ref, yrp_ref, st_ref, state):
        @pl.when(pl.program_id(0) == 0)
        def _():
            state[...] = jnp.zeros_like(state)

        for h in range(RET_HEADS):
            qh = qk_ref[:, dk * h:dk * (h + 1)]
            kh = qk_ref[:, 512 + dk * h:512 + dk * (h + 1)]
            vh = v_ref[:, dv * h:dv * (h + 1)]
            sh = state[h]
            shb = sh.astype(BF)
            st_ref[0, h] = shb
            att = _nt(qh, kh) * intra_ref[h]
            inner = _nn(att.astype(BF), vh)
            cross = _nn((qh.astype(F32) * qd_ref[h]).astype(BF), shb)
            out = inner + cross
            state[h] = sh * RET_CHUNK_DECAY[h] + _tn((kh.astype(F32) * kd_ref[h]).astype(BF), vh)
            ret_ref[:, dv * h:dv * (h + 1)] = out
            mu = jnp.mean(out, axis=-1, keepdims=True)
            d = out - mu
            var = jnp.mean(d * d, axis=-1, keepdims=True)
            y = d * lax.rsqrt(var + GN_EPS) * gain_ref[:, dv * h:dv * (h + 1)]
            g = g_ref[:, dv * h:dv * (h + 1)].astype(F32)
            yrp_ref[:, dv * h:dv * (h + 1)] = (g * _sigmoid(g) * y).astype(BF)

    return _call(
        body, (qkr, vr, gr, gain, *rtabs), comm, name="ret_fwd", grid=(nc,),
        in_specs=[_rows(c, 1024), _rows(c, 1024), _rows(c, 1024), VMEM_FULL, VMEM_FULL, VMEM_FULL, VMEM_FULL],
        out_specs=[_rows(c, 1024), _rows(c, 1024), pl.BlockSpec((1, RET_HEADS, dk, dv), lambda i: (i, 0, 0, 0))],
        out_shape=[jax.ShapeDtypeStruct((seq, 1024), F32), jax.ShapeDtypeStruct((seq, 1024), BF),
                   jax.ShapeDtypeStruct((nc, RET_HEADS, dk, dv), BF)],
        scratch_shapes=[pltpu.VMEM((RET_HEADS, dk, dv), F32)])


def _mix_fwd(oa, yrp, gates, bg, x, w_att, w_ret, w_out):
    seq = x.shape[0]
    tm = min(TOKEN_TILE, seq)

    def body(oa_ref, yrp_ref, gates_ref, bg_ref, x_ref, wa_ref, wr_ref, wo_ref, ya_ref, yr_ref, mg_ref, h1_ref):
        ya = _nn(oa_ref[...], wa_ref[...])
        yr = _nn(yrp_ref[...], wr_ref[...])
        gt = _sigmoid(gates_ref[...].astype(F32) + bg_ref[...])
        merged = (gt[:, :D_MODEL] * ya + gt[:, D_MODEL:] * yr).astype(BF)
        ya_ref[...] = ya.astype(BF)
        yr_ref[...] = yr.astype(BF)
        mg_ref[...] = merged
        h1_ref[...] = x_ref[...] + _nn(merged, wo_ref[...])

    return pl.pallas_call(
        body, name="mix_fwd", grid=(seq // tm,),
        in_specs=[_rows(tm, 512), _rows(tm, 1024), _rows(tm, 2048), VMEM_FULL, _rows(tm, D_MODEL),
                  VMEM_FULL, VMEM_FULL, VMEM_FULL],
        out_specs=[_rows(tm, D_MODEL)] * 4,
        out_shape=[jax.ShapeDtypeStruct((seq, D_MODEL), BF)] * 3 + [jax.ShapeDtypeStruct((seq, D_MODEL), F32)],
        compiler_params=_params(1),
    )(oa, yrp, gates, bg, x, w_att, w_ret, w_out)


def _mlp_fwd_loss(h1, g2, g3, target, w_ff1, w_ff2):
    seq = h1.shape[0]
    tm = min(TOKEN_TILE, seq)

    def body(h1_ref, g2_ref, g3_ref, t_ref, w1_ref, w2_ref, xn2_ref, u_ref, dh2_ref, loss_ref, dg3_ref):
        @pl.when(pl.program_id(0) == 0)
        def _():
            loss_ref[...] = jnp.zeros_like(loss_ref)
            dg3_ref[...] = jnp.zeros_like(dg3_ref)

        h1v = h1_ref[...]
        r2 = lax.rsqrt(jnp.mean(h1v * h1v, axis=-1, keepdims=True) + NORM_EPS)
        xn2 = (h1v * r2 * g2_ref[...]).astype(BF)
        xn2_ref[...] = xn2
        u = _nn(xn2, w1_ref[...])
        u_ref[...] = u.astype(BF)
        hdn = jnp.square(jnp.maximum(u, 0.0)).astype(BF)
        h2 = h1v + _nn(hdn, w2_ref[...])
        r3 = lax.rsqrt(jnp.mean(h2 * h2, axis=-1, keepdims=True) + NORM_EPS)
        hn = h2 * r3
        err = hn * g3_ref[...] - t_ref[...]
        loss_ref[...] += jnp.sum(err * err) * (0.5 / D_MODEL)
        dy = err * (1.0 / D_MODEL)
        dg3_ref[...] += jnp.sum(dy * hn, axis=0, keepdims=True)
        z = dy * g3_ref[...]
        dh2_ref[...] = r3 * (z - hn * jnp.mean(z * hn, axis=-1, keepdims=True))

    return pl.pallas_call(
        body, name="mlp_fwd_loss", grid=(seq // tm,),
        in_specs=[_rows(tm, D_MODEL), VMEM_FULL, VMEM_FULL, _rows(tm, D_MODEL), VMEM_FULL, VMEM_FULL],
        out_specs=[_rows(tm, D_MODEL), _rows(tm, D_FF), _rows(tm, D_MODEL),
                   pl.BlockSpec((1, LANE), lambda i: (0, 0)), pl.BlockSpec((1, D_MODEL), lambda i: (0, 0))],
        out_shape=[jax.ShapeDtypeStruct((seq, D_MODEL), BF), jax.ShapeDtypeStruct((seq, D_FF), BF),
                   jax.ShapeDtypeStruct((seq, D_MODEL), F32), jax.ShapeDtypeStruct((1, LANE), F32),
                   jax.ShapeDtypeStruct((1, D_MODEL), F32)],
        compiler_params=_params(1),
    )(h1, g2, g3, target, w_ff1, w_ff2)


def _rms_bwd(dxn, xin, gain):
    r = lax.rsqrt(jnp.mean(xin * xin, axis=-1, keepdims=True) + NORM_EPS)
    xhat = xin * r
    z = dxn * gain
    dxin = r * (z - xhat * jnp.mean(z * xhat, axis=-1, keepdims=True))
    return dxin, jnp.sum(dxn * xhat, axis=0, keepdims=True)


def _mlp_bwd(dh2, u, h1, g2, w_ff1, w_ff2):
    seq = h1.shape[0]
    tm = min(TOKEN_TILE, seq)

    def body(dh2_ref, u_ref, h1_ref, g2_ref, w1_ref, w2_ref, du_ref, dh1_ref, dg2_ref):
        @pl.when(pl.program_id(0) == 0)
        def _():
            dg2_ref[...] = jnp.zeros_like(dg2_ref)

        dh2v = dh2_ref[...]
        dhdn = _nt(dh2v.astype(BF), w2_ref[...])
        du = (dhdn * (2.0 * jnp.maximum(u_ref[...].astype(F32), 0.0))).astype(BF)
        du_ref[...] = du
        dxn2 = _nt(du, w1_ref[...])
        dnorm, dg = _rms_bwd(dxn2, h1_ref[...], g2_ref[...])
        dh1_ref[...] = dh2v + dnorm
        dg2_ref[...] += dg

    return pl.pallas_call(
        body, name="mlp_bwd", grid=(seq // tm,),
        in_specs=[_rows(tm, D_MODEL), _rows(tm, D_FF), _rows(tm, D_MODEL), VMEM_FULL, VMEM_FULL, VMEM_FULL],
        out_specs=[_rows(tm, D_FF), _rows(tm, D_MODEL), pl.BlockSpec((1, D_MODEL), lambda i: (0, 0))],
        out_shape=[jax.ShapeDtypeStruct((seq, D_FF), BF), jax.ShapeDtypeStruct((seq, D_MODEL), F32),
                   jax.ShapeDtypeStruct((1, D_MODEL), F32)],
        compiler_params=_params(1),
    )(dh2, u, h1, g2, w_ff1, w_ff2)


def _mm_tn(a, b, name, relu_sq=False, comm=None):
    seq, kdim = a.shape
    ndim = b.shape[1]
    ts = min(TN_TOKEN_TILE, seq)
    tk = min(kdim, 1024)
    tn = min(ndim, 1024)
    n_steps = seq // ts

    def body(a_ref, b_ref, o_ref):
        @pl.when(pl.program_id(2) == 0)
        def _():
            o_ref[...] = jnp.zeros_like(o_ref)

        av = a_ref[...]
        if relu_sq:
            av = jnp.square(jnp.maximum(av.astype(F32), 0.0))
        o_ref[...] += _tn(av.astype(BF), b_ref[...].astype(BF))

    res = _call(
        body, (a, b), comm, name=name, grid=(kdim // tk, ndim // tn, n_steps),
        in_specs=[pl.BlockSpec((ts, tk), lambda k, n, s: (s, k)), pl.BlockSpec((ts, tn), lambda k, n, s: (s, n))],
        out_specs=pl.BlockSpec((tk, tn), lambda k, n, s: (k, n)),
        out_shape=jax.ShapeDtypeStruct((kdim, ndim), F32))
    return res[0] if comm is None else res


def _mix_bwd(dh1, ya, yr, gates, bg, w_att, w_ret, w_out, comm=None):
    seq = dh1.shape[0]
    tm = min(TOKEN_TILE, seq)

    def body(dh1_ref, ya_ref, yr_ref, gates_ref, bg_ref, wa_ref, wr_ref, wo_ref,
             dya_ref, dyr_ref, dgates_ref, doa_ref, dyrp_ref, db_ref):
        @pl.when(pl.program_id(0) == 0)
        def _():
            db_ref[...] = jnp.zeros_like(db_ref)

        dm = _nt(dh1_ref[...].astype(BF), wo_ref[...])
        gt = _sigmoid(gates_ref[...].astype(F32) + bg_ref[...])
        ga, gr = gt[:, :D_MODEL], gt[:, D_MODEL:]
        dya = (dm * ga).astype(BF)
        dyr = (dm * gr).astype(BF)
        dya_ref[...] = dya
        dyr_ref[...] = dyr
        dga = dm * ya_ref[...].astype(F32) * ga * (1.0 - ga)
        dgr = dm * yr_ref[...].astype(F32) * gr * (1.0 - gr)
        dgates_ref[:, :D_MODEL] = dga.astype(BF)
        dgates_ref[:, D_MODEL:] = dgr.astype(BF)
        db_ref[:, :D_MODEL] += jnp.sum(dga, axis=0, keepdims=True)
        db_ref[:, D_MODEL:] += jnp.sum(dgr, axis=0, keepdims=True)
        doa_ref[...] = _nt(dya, wa_ref[...]).astype(BF)
        dyrp_ref[...] = _nt(dyr, wr_ref[...]).astype(BF)

    return _call(
        body, (dh1, ya, yr, gates, bg, w_att, w_ret, w_out), comm, name="mix_bwd", grid=(seq // tm,),
        in_specs=[_rows(tm, D_MODEL), _rows(tm, D_MODEL), _rows(tm, D_MODEL), _rows(tm, 2048), VMEM_FULL,
                  VMEM_FULL, VMEM_FULL, VMEM_FULL],
        out_specs=[_rows(tm, D_MODEL), _rows(tm, D_MODEL), _rows(tm, 2048), _rows(tm, 512), _rows(tm, 1024),
                   pl.BlockSpec((1, 2048), lambda i: (0, 0))],
        out_shape=[jax.ShapeDtypeStruct((seq, D_MODEL), BF), jax.ShapeDtypeStruct((seq, D_MODEL), BF),
                   jax.ShapeDtypeStruct((seq, 2048), BF), jax.ShapeDtypeStruct((seq, 512), BF),
                   jax.ShapeDtypeStruct((seq, 1024), BF), jax.ShapeDtypeStruct((1, 2048), F32)])


def _ret_bwd(qkr, vr, gr, ret, dyrp, states, gain, rtabs, cr, sr, comm=None):
    seq = qkr.shape[0]
    c = RET_CHUNK
    nc = seq // c
    dk, dv = RET_KEY_DIM, RET_VAL_DIM

    def body(qk_ref, v_ref, g_ref, ret_ref, dyp_ref, st_ref, gain_ref, intra_ref, qd_ref, kd_ref, cr_ref, sr_ref,
             dall_ref, dgain_ref, dstate):
        @pl.when(pl.program_id(0) == 0)
        def _():
            dstate[...] = jnp.zeros_like(dstate)
            dgain_ref[...] = jnp.zeros_like(dgain_ref)

        crv, srv = cr_ref[...], sr_ref[...]
        for h in range(RET_HEADS):
            vs = slice(dv * h, dv * (h + 1))
            qh = qk_ref[:, dk * h:dk * (h + 1)]
            kh = qk_ref[:, 512 + dk * h:512 + dk * (h + 1)]
            vh = v_ref[:, vs]
            out = ret_ref[:, vs]
            g = g_ref[:, vs].astype(F32)
            dyp = dyp_ref[:, vs].astype(F32)
            gain_h = gain_ref[:, vs]
            mu = jnp.mean(out, axis=-1, keepdims=True)
            d = out - mu
            rstd = lax.rsqrt(jnp.mean(d * d, axis=-1, keepdims=True) + GN_EPS)
            yn = d * rstd
            sg = _sigmoid(g)
            dg = dyp * (yn * gain_h) * (sg * (1.0 + g * (1.0 - sg)))
            dy = dyp * (g * sg)
            dgain_ref[:, vs] += jnp.sum(dy * yn, axis=0, keepdims=True)
            dyn = dy * gain_h
            dout = rstd * (dyn - jnp.mean(dyn, axis=-1, keepdims=True)
                           - yn * jnp.mean(dyn * yn, axis=-1, keepdims=True))
            doutb = dout.astype(BF)
            sc = st_ref[0, h]
            dsp = dstate[h]
            dspb = dsp.astype(BF)
            intra, qdv, kdv = intra_ref[h], qd_ref[h], kd_ref[h]
            att = _nt(qh, kh) * intra
            dab = (_nt(doutb, vh) * intra).astype(BF)
            qdec = (qh.astype(F32) * qdv).astype(BF)
            kdec = (kh.astype(F32) * kdv).astype(BF)
            dq = _nn(dab, kh) + _nt(doutb, sc) * qdv
            dkk = _tn(dab, qh) + _nt(vh, dspb) * kdv
            dvv = _tn(att.astype(BF), doutb) + _nn(kdec, dspb)
            dstate[h] = dsp * RET_CHUNK_DECAY[h] + _tn(qdec, doutb)
            dall_ref[:, dk * h:dk * (h + 1)] = _rope_ret_t(dq, crv, srv).astype(BF)
            dall_ref[:, 512 + dk * h:512 + dk * (h + 1)] = (_rope_ret_t(dkk, crv, srv) * RET_SCALE).astype(BF)
            dall_ref[:, 1024 + dv * h:1024 + dv * (h + 1)] = dvv.astype(BF)
            dall_ref[:, 2048 + dv * h:2048 + dv * (h + 1)] = dg.astype(BF)

    def rev(width):
        return pl.BlockSpec((c, width), lambda i: (nc - 1 - i, 0))

    return _call(
        body, (qkr, vr, gr, ret, dyrp, states, gain, *rtabs, cr, sr), comm, name="ret_bwd", grid=(nc,),
        in_specs=[rev(1024), rev(1024), rev(1024), rev(1024), rev(1024),
                  pl.BlockSpec((1, RET_HEADS, dk, dv), lambda i: (nc - 1 - i, 0, 0, 0)),
                  VMEM_FULL, VMEM_FULL, VMEM_FULL, VMEM_FULL, rev(LANE), rev(LANE)],
        out_specs=[rev(3072), pl.BlockSpec((1, 1024), lambda i: (0, 0))],
        out_shape=[jax.ShapeDtypeStruct((seq, 3072), BF), jax.ShapeDtypeStruct((1, 1024), F32)],
        scratch_shapes=[pltpu.VMEM((RET_HEADS, dk, dv), F32)])


def _attn_bwd(qkv, doa, sinks, ca, sa, sb, comm=None):
    seq = qkv.shape[0]
    c = ATT_BLOCK
    nb = seq // c

    def body(sink_ref, q_ref, kvc_ref, kvp_ref, do_ref, cac_ref, sac_ref, sbc_ref, cap_ref, sap_ref, sbp_ref,
             dq_ref, dkv_ref, dsink_ref, carry):
        i = pl.program_id(0)

        @pl.when(i == 0)
        def _():
            carry[...] = jnp.zeros_like(carry)
            dsink_ref[...] = jnp.zeros_like(dsink_ref)

        def flush(total):
            dk_pre = _rope_att_t(total[:, :LANE], cap_ref[...], sap_ref[...], sbp_ref[...])
            dkv_ref[:, :LANE] = dk_pre.astype(BF)
            dkv_ref[:, LANE:] = total[:, LANE:].astype(BF)

        @pl.when(i < nb)
        def _():
            kc, kp = kvc_ref[...], kvp_ref[...]
            kw = jnp.concatenate([kp[:, :LANE], kc[:, :LANE]], axis=0).astype(F32)
            vw = jnp.concatenate([kp[:, LANE:], kc[:, LANE:]], axis=0).astype(F32)
            kk, vv = _kv_variants(kw), _kv_variants(vw)
            allowed = _att_mask(i)
            lane1 = lax.broadcasted_iota(jnp.int32, (1, LANE), 1)
            gk = {key: jnp.zeros((2 * c, LANE), F32) for key in kk}
            gv = {key: jnp.zeros((2 * c, LANE), F32) for key in kk}
            dsink = jnp.zeros((1, LANE), F32)
            dqs = []
            for p in range(4):
                qp = q_ref[:, LANE * p:LANE * (p + 1)]
                dop = do_ref[:, LANE * p:LANE * (p + 1)]
                dqp = jnp.zeros((c, LANE), F32)
                for e in (0, 1):
                    key = (p // 2, e)
                    prob, psink = _att_probs(qp, kk[key], allowed, sink_ref[2 * p + e])
                    dprob = _nt(dop, vv[key])
                    drow = jnp.sum(dprob * prob, axis=1, keepdims=True)
                    ds = (prob * (dprob - drow) * ATT_SCALE).astype(BF)
                    dqp = dqp + _nn(ds, kk[key])
                    gk[key] = gk[key] + _tn(ds, qp)
                    gv[key] = gv[key] + _tn(prob.astype(BF), dop)
                    dsink = dsink + jnp.where(lane1 == 2 * p + e, -jnp.sum(psink * drow), 0.0)
                dqs.append(dqp)
            dq = jnp.concatenate(dqs, axis=1)
            dq_ref[...] = _rope_att_t(dq, cac_ref[...], sac_ref[...], sbc_ref[...]).astype(BF)
            dsink_ref[...] += dsink
            lo = lax.broadcasted_iota(jnp.int32, (2 * c, LANE), 1) < ATT_HEAD_DIM

            def fold(gg):
                return jnp.where(lo, gg[(0, 0)] + pltpu.roll(gg[(0, 1)], 64, 1),
                                 gg[(1, 1)] + pltpu.roll(gg[(1, 0)], 64, 1))

            dkw, dvw = fold(gk), fold(gv)
            flush(carry[...] + jnp.concatenate([dkw[:c], dvw[:c]], axis=1))
            carry[...] = jnp.concatenate([dkw[c:], dvw[c:]], axis=1)

        @pl.when(i == nb)
        def _():
            flush(carry[...])

    cur = lambda i: (jnp.minimum(i, nb - 1), 0)
    prev = lambda i: (jnp.maximum(i - 1, 0), 0)
    return _call(
        body, (sinks, qkv, qkv, qkv, doa, ca, sa, sb, ca, sa, sb), comm, name="attn_bwd", grid=(nb + 1,),
        in_specs=[SMEM_FULL,
                  pl.BlockSpec((c, 512), cur),
                  pl.BlockSpec((c, 256), lambda i: (jnp.minimum(i, nb - 1), 2)),
                  pl.BlockSpec((c, 256), lambda i: (jnp.clip(i - 1, 0, nb - 1), 2)),
                  pl.BlockSpec((c, 512), cur),
                  pl.BlockSpec((c, LANE), cur), pl.BlockSpec((c, LANE), cur), pl.BlockSpec((c, LANE), cur),
                  pl.BlockSpec((c, LANE), prev), pl.BlockSpec((c, LANE), prev), pl.BlockSpec((c, LANE), prev)],
        out_specs=[pl.BlockSpec((c, 512), cur), pl.BlockSpec((c, 256), prev),
                   pl.BlockSpec((1, LANE), lambda i: (0, 0))],
        out_shape=[jax.ShapeDtypeStruct((seq, 512), BF), jax.ShapeDtypeStruct((seq, 256), BF),
                   jax.ShapeDtypeStruct((1, LANE), F32)],
        scratch_shapes=[pltpu.VMEM((c, 256), F32)])


def _inproj_bwd(dqa, dkva, dret, dgates, dh1, x, g1, w_in, comm=None):
    seq = x.shape[0]
    tm = min(TOKEN_TILE, seq)

    def body(dqa_ref, dkva_ref, dret_ref, dgates_ref, dh1_ref, x_ref, g_ref, w_ref, dx_ref, dg1_ref):
        @pl.when(pl.program_id(0) == 0)
        def _():
            dg1_ref[...] = jnp.zeros_like(dg1_ref)

        dxn = (_nt(dqa_ref[...], w_ref[:, C_QA:C_KA]) + _nt(dkva_ref[...], w_ref[:, C_KA:C_QR])
               + _nt(dret_ref[...], w_ref[:, C_QR:C_GATES]) + _nt(dgates_ref[...], w_ref[:, C_GATES:C_END]))
        dnorm, dg = _rms_bwd(dxn, x_ref[...], g_ref[...])
        dx_ref[...] = dh1_ref[...] + dnorm
        dg1_ref[...] += dg

    return _call(
        body, (dqa, dkva, dret, dgates, dh1, x, g1, w_in), comm, name="inproj_bwd", grid=(seq // tm,),
        in_specs=[_rows(tm, 512), _rows(tm, 256), _rows(tm, 3072), _rows(tm, 2048), _rows(tm, D_MODEL),
                  _rows(tm, D_MODEL), VMEM_FULL, VMEM_FULL],
        out_specs=[_rows(tm, D_MODEL), pl.BlockSpec((1, D_MODEL), lambda i: (0, 0))],
        out_shape=[jax.ShapeDtypeStruct((seq, D_MODEL), F32), jax.ShapeDtypeStruct((1, D_MODEL), F32)])


def _local_step(x, target, g1, bg, sinks, gain, g2, g3, w_in, shards):
    seq = x.shape[0]
    ca, sa, sb, cr, sr = _rope_tables(seq)
    rtabs = _ret_tables()
    scatter = _Scatter

    comm, unpack = _gather(shards, ("w_att_up", "w_ret_up", "w_out"))
    (xn1, qkv, qkr, vr, gr, gates), got = _inproj_fwd(x, g1, w_in, (ca, sa, sb, cr, sr), comm)
    w_att, w_ret, w_out = unpack(got)
    comm, unpack = _gather(shards, ("w_ff1",))
    oa, got = _attn_fwd(qkv, sinks, comm)
    (w_ff1,) = unpack(got)
    comm, unpack = _gather(shards, ("w_ff2",))
    (ret, yrp, states), got = _ret_fwd(qkr, vr, gr, gain, rtabs, comm)
    (w_ff2,) = unpack(got)
    ya, yr, merged, h1 = _mix_fwd(oa, yrp, gates, bg, x, w_att, w_ret, w_out)
    xn2, u, dh2, loss, dg3 = _mlp_fwd_loss(h1, g2, g3, target, w_ff1, w_ff2)

    du, dh1, dg2 = _mlp_bwd(dh2, u, h1, g2, w_ff1, w_ff2)
    ff = scatter(dict(w_ff1=_mm_tn(xn2, du, "dw_ff1"), w_ff2=_mm_tn(u, dh2, "dw_ff2", relu_sq=True)), "ff")
    (dya, dyr, dgates, doa, dyrp, db), r1 = _mix_bwd(dh1, ya, yr, gates, bg, w_att, w_ret, w_out, ff.pair_comm())
    ff.pair_done(r1)
    mid = scatter(dict(w_att_up=_mm_tn(oa, dya, "dw_att_up"), w_ret_up=_mm_tn(yrp, dyr, "dw_ret_up"),
                       w_out=_mm_tn(merged, dh1, "dw_out")), "mid")
    (dqa, dkva, dsink), r2 = _attn_bwd(qkv, doa, sinks, ca, sa, sb, ff.chip_comm())
    ff.chip_done(r2)
    (dret, dgain), r1 = _ret_bwd(qkr, vr, gr, ret, dyrp, states, gain, rtabs, cr, sr, mid.pair_comm())
    mid.pair_done(r1)
    d_ret_cols, r2 = _mm_tn(xn1, dret, "dw_in_ret", comm=mid.chip_comm())
    mid.chip_done(r2)
    d_in = jnp.concatenate([_mm_tn(xn1, dqa, "dw_in_qa"), _mm_tn(xn1, dkva, "dw_in_kva"), d_ret_cols,
                            _mm_tn(xn1, dgates, "dw_in_gates")], axis=1)
    win = scatter(dict(w_in=d_in), "in")
    win.pair_done(_comm_alone(win.pair_comm(), "pair_exchange_in"))
    (dx, dg1), r2 = _inproj_bwd(dqa, dkva, dret, dgates, dh1, x, g1, w_in, win.chip_comm())
    win.chip_done(r2)
    small = dict(norm_mix_gain=dg1, b_gates=db, attn_sinks=dsink, ret_gn_gain=dgain, norm_mlp_gain=dg2,
                 norm_final_gain=dg3)
    return loss, dx, small, (win, mid, ff)


def _coords():
    return lax.axis_index("x"), lax.axis_index("y"), lax.axis_index("c")


def _flip(v, bit):
    return 1 - v if bit else v


def _gather_comm(shards):
    n = len(shards)

    def parts(ins, outs, sems):
        send_sems, recv_sems, local_sems = sems
        x, y, c = _coords()
        me, sib = (x, y, c), (x, y, 1 - c)
        chips = [(1 - x, y), (x, 1 - y), (1 - x, 1 - y)]

        def slot(a, blk):
            return outs[a].at[4 * blk[0] + 2 * blk[1] + blk[2]]

        def copy(a, k, blk, to, src=None):
            return pltpu.make_async_remote_copy(
                src_ref=slot(a, blk) if src is None else src, dst_ref=slot(a, blk),
                send_sem=send_sems.at[a, k], recv_sem=recv_sems.at[a, k], device_id=to, device_id_type=MESH)

        mine = [pltpu.make_async_copy(ins[a], slot(a, me), local_sems.at[a]) for a in range(n)]
        first = []
        for a in range(n):
            first.append(copy(a, 0, me, sib, src=ins[a]))
            first += [copy(a, 1 + j, me, (*chip, c), src=ins[a]) for j, chip in enumerate(chips)]
        return me, sib, chips, c, copy, mine, first

    def start(ins, outs, sems):
        *_, mine, first = parts(ins, outs, sems)
        for cp in mine + first:
            cp.start()

    def finish(ins, outs, sems):
        me, sib, chips, c, copy, mine, first = parts(ins, outs, sems)
        passed = []
        for j, chip in enumerate(chips):
            for a in range(n):
                copy(a, 1 + j, (*chip, c), me).wait_recv()
                fwd = copy(a, 4 + j, (*chip, c), sib)
                fwd.start()
                passed.append(fwd)
        for a in range(n):
            copy(a, 0, sib, me).wait_recv()
            for j, chip in enumerate(chips):
                copy(a, 4 + j, (*chip, 1 - c), me).wait_recv()
        for cp in first + passed:
            cp.wait_send()
        for cp in mine:
            cp.wait()

    return _Comm(list(shards), [jax.ShapeDtypeStruct((N_DEV,) + s.shape, s.dtype) for s in shards],
                 [pltpu.SemaphoreType.DMA((n, 7)), pltpu.SemaphoreType.DMA((n, 7)), pltpu.SemaphoreType.DMA((n,))],
                 start, finish)


COLUMN_SHARDED = ("w_in", "w_att_up", "w_ff1")


def _gather(shards, names):
    def unpack(got):
        full = []
        for k, g in zip(names, got):
            if k in COLUMN_SHARDED:
                full.append(jnp.transpose(g, (1, 0, 2)).reshape(g.shape[1], N_DEV * g.shape[2]))
            else:
                full.append(g.reshape(N_DEV * g.shape[1], g.shape[2]))
        return full

    return _gather_comm([shards[k] for k in names]), unpack


def _pair_comm(grads):
    n = len(grads)

    def copies(g, r1, sems):
        send_sems, recv_sems = sems
        x, y, c = _coords()
        return [pltpu.make_async_remote_copy(
            src_ref=g[a].at[2 * j + (1 - c)], dst_ref=r1[a].at[j], send_sem=send_sems.at[a, j],
            recv_sem=recv_sems.at[a, j], device_id=(x, y, 1 - c), device_id_type=MESH)
            for a in range(n) for j in range(4)]

    def start(g, r1, sems):
        for cp in copies(g, r1, sems):
            cp.start()

    def finish(g, r1, sems):
        for cp in copies(g, r1, sems):
            cp.wait_recv()
        for cp in copies(g, r1, sems):
            cp.wait_send()

    return _Comm(list(grads), [jax.ShapeDtypeStruct((4,) + g.shape[1:], g.dtype) for g in grads],
                 [pltpu.SemaphoreType.DMA((n, 4)), pltpu.SemaphoreType.DMA((n, 4))], start, finish)


def _small_comm(small):
    def parts(ins, outs, sems):
        (small_ref,), (small_all,) = ins, outs
        ssend, srecv, lsem = sems
        x, y, c = _coords()
        me_idx = 4 * x + 2 * y + c
        own = pltpu.make_async_copy(small_ref, small_all.at[me_idx], lsem)
        sends, recvs = [], []
        for r in range(1, N_DEV):
            px, py, pc = _flip(x, r & 4), _flip(y, r & 2), _flip(c, r & 1)
            sends.append(pltpu.make_async_remote_copy(
                src_ref=small_ref, dst_ref=small_all.at[me_idx], send_sem=ssend.at[r - 1], recv_sem=srecv.at[r - 1],
                device_id=(px, py, pc), device_id_type=MESH))
            recvs.append(pltpu.make_async_remote_copy(
                src_ref=small_ref, dst_ref=small_all.at[4 * px + 2 * py + pc], send_sem=ssend.at[r - 1],
                recv_sem=srecv.at[r - 1], device_id=(px, py, pc), device_id_type=MESH))
        return own, sends, recvs

    def start(ins, outs, sems):
        own, sends, _ = parts(ins, outs, sems)
        own.start()
        for cp in sends:
            cp.start()

    def finish(ins, outs, sems):
        own, sends, recvs = parts(ins, outs, sems)
        for cp in recvs:
            cp.wait_recv()
        for cp in sends:
            cp.wait_send()
        own.wait()

    return _Comm([small], [jax.ShapeDtypeStruct((N_DEV,) + small.shape, small.dtype)],
                 [pltpu.SemaphoreType.DMA((N_DEV - 1,)), pltpu.SemaphoreType.DMA((N_DEV - 1,)),
                  pltpu.SemaphoreType.DMA], start, finish)


def _pair_sum(grads, r1s, c_arr, tag):
    n = len(grads)
    q = 4

    def body(c_ref, *refs):
        g, r, t = refs[:n], refs[n:2 * n], refs[2 * n:]
        for a in range(n):
            t[a][...] = (g[a][...] + r[a][...]).astype(t[a].dtype)

    def blk(arr):
        return (1, arr.shape[1] // q, arr.shape[2])

    grid_spec = pltpu.PrefetchScalarGridSpec(
        num_scalar_prefetch=1, grid=(4, q),
        in_specs=[pl.BlockSpec(blk(g), lambda j, s, c_ref: (2 * j + c_ref[0], s, 0)) for g in grads]
        + [pl.BlockSpec(blk(r), lambda j, s, c_ref: (j, s, 0)) for r in r1s],
        out_specs=[pl.BlockSpec(blk(r), lambda j, s, c_ref: (j, s, 0)) for r in r1s])
    return pl.pallas_call(
        body, name="pair_sum_" + tag, grid_spec=grid_spec,
        out_shape=[jax.ShapeDtypeStruct(r.shape, RS_PAYLOAD) for r in r1s],
        compiler_params=_params(2),
    )(c_arr, *grads, *r1s)


def _chip_comm(ts):
    n = len(ts)

    def copies(t, r2, sems):
        send_sems, recv_sems = sems
        x, y, c = _coords()
        out = []
        for a in range(n):
            for r in range(1, 4):
                tx, ty = _flip(x, r & 2), _flip(y, r & 1)
                out.append(pltpu.make_async_remote_copy(
                    src_ref=t[a].at[2 * tx + ty], dst_ref=r2[a].at[r - 1], send_sem=send_sems.at[a, r - 1],
                    recv_sem=recv_sems.at[a, r - 1], device_id=(tx, ty, c), device_id_type=MESH))
        return out

    def start(t, r2, sems):
        for cp in copies(t, r2, sems):
            cp.start()

    def finish(t, r2, sems):
        for cp in copies(t, r2, sems):
            cp.wait_recv()
        for cp in copies(t, r2, sems):
            cp.wait_send()

    return _Comm(list(ts), [jax.ShapeDtypeStruct((3,) + t.shape[1:], t.dtype) for t in ts],
                 [pltpu.SemaphoreType.DMA((n, 3)), pltpu.SemaphoreType.DMA((n, 3))], start, finish)


class _Scatter:
    def __init__(self, grads, tag):
        self.names, self.tag = tuple(grads), tag
        self.stacks = []
        for k, g in grads.items():
            if k in COLUMN_SHARDED:
                self.stacks.append(jnp.transpose(g.reshape(g.shape[0], N_DEV, g.shape[1] // N_DEV), (1, 0, 2)))
            else:
                self.stacks.append(g.reshape(N_DEV, g.shape[0] // N_DEV, g.shape[1]))

    def pair_comm(self):
        return _pair_comm(self.stacks)

    def pair_done(self, r1s):
        self.r1s = list(r1s)
        c_arr = jnp.reshape(lax.axis_index("c"), (1,)).astype(jnp.int32)
        self.ts = _pair_sum(self.stacks, self.r1s, c_arr, self.tag)

    def chip_comm(self):
        return _chip_comm(self.ts)

    def chip_done(self, r2s):
        self.r2s = list(r2s)


def _adamw(w, g, m, v):
    m = ADAM_B1 * m + (1.0 - ADAM_B1) * g
    v = ADAM_B2 * v + (1.0 - ADAM_B2) * jnp.square(g)
    m_hat = m / (1.0 - ADAM_B1 ** ADAM_STEP)
    v_hat = v / (1.0 - ADAM_B2 ** ADAM_STEP)
    delta = -ADAM_LR * (m_hat / (jnp.sqrt(v_hat) + ADAM_EPS) + ADAM_WD * w)
    return delta, m, v


def _adam_big(stacks, r1s, r2s, ws, ms, vs, idx_arr):
    n = len(ws)
    q = 8

    def body(idx_ref, *refs):
        g0, r1, r2, w, m, v = (refs[k * n:(k + 1) * n] for k in range(6))
        outs = refs[6 * n:]
        for a in range(n):
            g = (((g0[a][0] + r1[a][0]) + r2[a][0].astype(F32)) + r2[a][1].astype(F32)) + r2[a][2].astype(F32)
            delta, nm, nv = _adamw(w[a][...], g, m[a][...], v[a][...])
            outs[4 * a][...] = g
            outs[4 * a + 1][...] = delta
            outs[4 * a + 2][...] = nm
            outs[4 * a + 3][...] = nv

    def rows(arr):
        return arr.shape[0] // q

    in_specs = ([pl.BlockSpec((1, rows(w), w.shape[1]), lambda s, idx_ref: (idx_ref[0], s, 0)) for w in ws]
                + [pl.BlockSpec((1, rows(w), w.shape[1]), lambda s, idx_ref: (idx_ref[1], s, 0)) for w in ws]
                + [pl.BlockSpec((3, rows(w), w.shape[1]), lambda s, idx_ref: (0, s, 0)) for w in ws]
                + [pl.BlockSpec((rows(w), w.shape[1]), lambda s, idx_ref: (s, 0)) for w in ws] * 3)
    out_specs = [pl.BlockSpec((rows(w), w.shape[1]), lambda s, idx_ref: (s, 0)) for w in ws for _ in range(4)]
    grid_spec = pltpu.PrefetchScalarGridSpec(num_scalar_prefetch=1, grid=(q,), in_specs=in_specs,
                                             out_specs=out_specs)
    return pl.pallas_call(
        body, name="adam_big", grid_spec=grid_spec,
        out_shape=[jax.ShapeDtypeStruct(w.shape, F32) for w in ws for _ in range(4)],
        compiler_params=_params(1),
    )(idx_arr, *stacks, *r1s, *r2s, *ws, *ms, *vs)


def _adam_small(small_all, w, m, v):
    def body(all_ref, w_ref, m_ref, v_ref, g_ref, d_ref, nm_ref, nv_ref):
        g = all_ref[0]
        for k in range(1, N_DEV):
            g = g + all_ref[k]
        delta, nm, nv = _adamw(w_ref[...], g, m_ref[...], v_ref[...])
        g_ref[...] = g
        d_ref[...] = delta
        nm_ref[...] = nm
        nv_ref[...] = nv

    return pl.pallas_call(
        body, name="adam_small",
        in_specs=[VMEM_FULL] * 4, out_specs=[VMEM_FULL] * 4,
        out_shape=[jax.ShapeDtypeStruct(w.shape, F32)] * 4,
    )(small_all, w, m, v)


RS_PAYLOAD = BF


def _pack_rows(rows):
    rid = lax.broadcasted_iota(jnp.int32, (8, D_MODEL), 0)
    out = jnp.zeros((8, D_MODEL), F32)
    for i, r in enumerate(rows):
        out = jnp.where(rid == i, jnp.broadcast_to(r, (8, D_MODEL)), out)
    return out


def _small_rows(norm_mix_gain, b_gates, attn_sinks, ret_gn_gain, norm_mlp_gain, norm_final_gain):
    return [norm_mix_gain, b_gates[:, :D_MODEL], b_gates[:, D_MODEL:], ret_gn_gain, norm_mlp_gain,
            norm_final_gain.reshape(1, D_MODEL), jnp.pad(attn_sinks, ((0, 0), (0, D_MODEL - ATT_HEADS)))]


def _unpack_small(p):
    return dict(norm_mix_gain=p[0:1], b_gates=jnp.concatenate([p[1:2], p[2:3]], axis=1), ret_gn_gain=p[3:4],
                norm_mlp_gain=p[4:5], norm_final_gain=p[5], attn_sinks=p[6:7, :ATT_HEADS])


WEIGHTS = ("norm_mix_gain", "w_in", "b_gates", "attn_sinks", "ret_gn_gain", "w_att_up", "w_ret_up", "w_out",
           "norm_mlp_gain", "w_ff1", "w_ff2", "norm_final_gain")
BIG = ("w_in", "w_att_up", "w_ret_up", "w_out", "w_ff1", "w_ff2")


def kernel(x, norm_mix_gain, w_in, b_gates, attn_sinks, ret_gn_gain, w_att_up, w_ret_up, w_out, norm_mlp_gain, w_ff1, w_ff2, norm_final_gain, loss_target, m_norm_mix_gain, m_w_in, m_b_gates, m_attn_sinks, m_ret_gn_gain, m_w_att_up, m_w_ret_up, m_w_out, m_norm_mlp_gain, m_w_ff1, m_w_ff2, m_norm_final_gain, v_norm_mix_gain, v_w_in, v_b_gates, v_attn_sinks, v_ret_gn_gain, v_w_att_up, v_w_ret_up, v_w_out, v_norm_mlp_gain, v_w_ff1, v_w_ff2, v_norm_final_gain):
    w = dict(norm_mix_gain=norm_mix_gain, w_in=w_in, b_gates=b_gates, attn_sinks=attn_sinks, ret_gn_gain=ret_gn_gain,
             w_att_up=w_att_up, w_ret_up=w_ret_up, w_out=w_out, norm_mlp_gain=norm_mlp_gain, w_ff1=w_ff1,
             w_ff2=w_ff2, norm_final_gain=norm_final_gain)
    m = dict(norm_mix_gain=m_norm_mix_gain, w_in=m_w_in, b_gates=m_b_gates, attn_sinks=m_attn_sinks,
             ret_gn_gain=m_ret_gn_gain, w_att_up=m_w_att_up, w_ret_up=m_w_ret_up, w_out=m_w_out,
             norm_mlp_gain=m_norm_mlp_gain, w_ff1=m_w_ff1, w_ff2=m_w_ff2, norm_final_gain=m_norm_final_gain)
    v = dict(norm_mix_gain=v_norm_mix_gain, w_in=v_w_in, b_gates=v_b_gates, attn_sinks=v_attn_sinks,
             ret_gn_gain=v_ret_gn_gain, w_att_up=v_w_att_up, w_ret_up=v_w_ret_up, w_out=v_w_out,
             norm_mlp_gain=v_norm_mlp_gain, w_ff1=v_w_ff1, w_ff2=v_w_ff2, norm_final_gain=v_norm_final_gain)

    shards = {k: w[k][0].astype(BF) for k in BIG}
    comm, unpack = _gather(shards, ("w_in",))
    (w_in_full,) = unpack(_comm_alone(comm, "allgather_w_in"))

    loss_p, dx, small, groups = _local_step(
        x[0], loss_target[0], norm_mix_gain, b_gates, attn_sinks[0], ret_gn_gain, norm_mlp_gain,
        norm_final_gain.reshape(1, D_MODEL), w_in_full, shards)

    by_name = {}
    for grp in groups:
        for k, stack, r1, r2 in zip(grp.names, grp.stacks, grp.r1s, grp.r2s):
            by_name[k] = (stack, r1, r2)
    lane = lax.broadcasted_iota(jnp.int32, (1, D_MODEL), 1)
    sink_row = jnp.where(lane < ATT_HEADS, jnp.pad(small["attn_sinks"], ((0, 0), (0, D_MODEL - LANE))),
                         jnp.where(lane == ATT_HEADS, jnp.pad(loss_p, ((0, 0), (0, D_MODEL - LANE)), mode="edge"),
                                   0.0))
    small_pack = _pack_rows([small["norm_mix_gain"], small["b_gates"][:, :D_MODEL], small["b_gates"][:, D_MODEL:],
                             small["ret_gn_gain"], small["norm_mlp_gain"], small["norm_final_gain"], sink_row])
    xi, yi, ci = _coords()
    (small_all,) = _comm_alone(_small_comm(small_pack), "small_exchange")
    big_out = _adam_big(*[[by_name[k][i] for k in BIG] for i in range(3)],
                        [w[k][0] for k in BIG], [m[k][0] for k in BIG], [v[k][0] for k in BIG],
                        jnp.stack([4 * xi + 2 * yi + ci, 2 * xi + yi]).astype(jnp.int32))
    sm_g, sm_d, sm_m, sm_v = _adam_small(small_all, _pack_rows(_small_rows(*[w[k] for k in WEIGHTS if k not in BIG])),
                                         _pack_rows(_small_rows(*[m[k] for k in WEIGHTS if k not in BIG])),
                                         _pack_rows(_small_rows(*[v[k] for k in WEIGHTS if k not in BIG])))

    loss = sm_g[6, ATT_HEADS]
    kinds = []
    for idx, packed in enumerate((sm_g, sm_d, sm_m, sm_v)):
        out = _unpack_small(packed)
        for a, k in enumerate(BIG):
            out[k] = big_out[4 * a + idx][None]
        kinds.append(out)
    return (loss, dx[None], *[kinds[0][k] for k in WEIGHTS], *[kinds[1][k] for k in WEIGHTS],
            *[kinds[2][k] for k in WEIGHTS], *[kinds[3][k] for k in WEIGHTS])
```

```python
import functools
import math

import jax
import jax.numpy as jnp
from jax import lax
from jax.experimental import pallas as pl
from jax.experimental.pallas import tpu as pltpu

F32 = jnp.float32
BF = jnp.bfloat16
MESH = pl.DeviceIdType.MESH

D_MODEL = 1024
ATT_HEADS = 8
ATT_HEAD_DIM = 64
ATT_BLOCK = 128
ROPE_DIM = 16
ROPE_THETA = 500000.0
RET_HEADS = 4
RET_KEY_DIM = 128
RET_VAL_DIM = 256
RET_CHUNK = 128
RET_ROT_BASE = 10000.0
D_FF = 4096
NORM_EPS = 1e-6
GN_EPS = 1e-6
NEG_INF = -1e30
ATT_SCALE = ATT_HEAD_DIM ** -0.5
RET_SCALE = RET_KEY_DIM ** -0.5

C_QA, C_KA, C_VA, C_QR, C_KR, C_VR, C_GR, C_GATES, C_END = 0, 512, 640, 768, 1280, 1792, 2816, 3840, 5888

ADAM_LR = 0.001
ADAM_B1 = 0.9
ADAM_B2 = 0.999
ADAM_EPS = 1e-08
ADAM_WD = 0.01
ADAM_STEP = 10

N_DEV = 8
LANE = 128
VMEM_LIMIT = 56 * 1024 * 1024
TOKEN_TILE = 256
TN_TOKEN_TILE = 1024

RET_LOG_GAMMA = tuple(math.log1p(-(2.0 ** (-5.0 - h))) for h in range(RET_HEADS))
RET_CHUNK_DECAY = tuple(math.exp(RET_CHUNK * lg) for lg in RET_LOG_GAMMA)

VMEM_FULL = pl.BlockSpec(memory_space=pltpu.VMEM)
SMEM_FULL = pl.BlockSpec(memory_space=pltpu.SMEM)
ANY = pl.BlockSpec(memory_space=pl.ANY)


def _params(n_axes):
    return pltpu.CompilerParams(dimension_semantics=("arbitrary",) * n_axes, vmem_limit_bytes=VMEM_LIMIT)


def _nn(a, b):
    return jnp.dot(a, b, preferred_element_type=F32)


def _nt(a, b):
    return lax.dot_general(a, b, (((1,), (1,)), ((), ())), preferred_element_type=F32)


def _tn(a, b):
    return lax.dot_general(a, b, (((0,), (0,)), ((), ())), preferred_element_type=F32)


def _sigmoid(v):
    return 1.0 / (1.0 + jnp.exp(-v))


def _rows(tile, width):
    return pl.BlockSpec((tile, width), lambda i: (i, 0))


class _Comm:
    def __init__(self, inputs, out_shapes, scratch, start, finish):
        self.inputs, self.out_shapes, self.scratch, self.start, self.finish = inputs, out_shapes, scratch, start, finish


def _call(body, args, comm=None, *, name, grid, in_specs, out_specs, out_shape, scratch_shapes=()):
    params = _params(len(grid))
    if comm is None:
        return pl.pallas_call(body, name=name, grid=grid, in_specs=in_specs, out_specs=out_specs, out_shape=out_shape,
                              scratch_shapes=scratch_shapes, compiler_params=params)(*args), ()
    single = not isinstance(out_specs, (list, tuple))
    out_specs_l = [out_specs] if single else list(out_specs)
    out_shape_l = [out_shape] if single else list(out_shape)
    n_in, n_out, n_scr = len(in_specs), len(out_specs_l), len(scratch_shapes)
    n_cin, n_cout = len(comm.inputs), len(comm.out_shapes)

    def hosted(*refs):
        it = iter(refs)
        ins, cin, outs, cout, scr = ([next(it) for _ in range(k)] for k in (n_in, n_cin, n_out, n_cout, n_scr))
        sems = list(it)
        ids = [pl.program_id(k) for k in range(len(grid))]
        first = functools.reduce(jnp.logical_and, [i == 0 for i in ids])
        last = functools.reduce(jnp.logical_and, [i == g - 1 for i, g in zip(ids, grid)])

        @pl.when(first)
        def _():
            comm.start(cin, cout, sems)

        body(*ins, *outs, *scr)

        @pl.when(last)
        def _():
            comm.finish(cin, cout, sems)

    res = pl.pallas_call(
        hosted, name=name, grid=grid, in_specs=list(in_specs) + [ANY] * n_cin,
        out_specs=out_specs_l + [ANY] * n_cout, out_shape=out_shape_l + list(comm.out_shapes),
        scratch_shapes=list(scratch_shapes) + list(comm.scratch), compiler_params=params)(*args, *comm.inputs)
    return (res[0] if single else res[:n_out]), res[n_out:]


def _comm_alone(comm, name):
    n_cin, n_cout = len(comm.inputs), len(comm.out_shapes)

    def body(*refs):
        cin, cout, sems = refs[:n_cin], refs[n_cin:n_cin + n_cout], refs[n_cin + n_cout:]
        comm.start(cin, cout, sems)
        comm.finish(cin, cout, sems)

    return pl.pallas_call(body, name=name, in_specs=[ANY] * n_cin, out_specs=[ANY] * n_cout,
                          out_shape=list(comm.out_shapes), scratch_shapes=list(comm.scratch))(*comm.inputs)


def _slabs(v, fn):
    return jnp.concatenate([fn(v[:, LANE * j:LANE * (j + 1)]) for j in range(v.shape[1] // LANE)], axis=1)


def _rope_att(v, ca, sa, sb):
    return _slabs(v, lambda t: t * ca + pltpu.roll(t, LANE - 8, 1) * sa + pltpu.roll(t, 8, 1) * sb)


def _rope_att_t(v, ca, sa, sb):
    return _slabs(v, lambda t: t * ca + pltpu.roll(t * sa, 8, 1) + pltpu.roll(t * sb, LANE - 8, 1))


def _rope_ret(v, cr, sr):
    return _slabs(v, lambda t: t * cr + pltpu.roll(t, 64, 1) * sr)


def _rope_ret_t(v, cr, sr):
    return _slabs(v, lambda t: t * cr + pltpu.roll(t * sr, 64, 1))


def _rope_tables(seq):
    pos = jnp.arange(seq, dtype=F32)

    def cs(dim, theta):
        inv = theta ** (-jnp.arange(0, dim, 2, dtype=F32) / dim)
        ang = pos[:, None] * inv[None, :]
        return jnp.cos(ang), jnp.sin(ang)

    ca, sa_ = cs(ROPE_DIM, ROPE_THETA)
    zeros8 = jnp.zeros_like(ca)
    rest = ATT_HEAD_DIM - ROPE_DIM
    c64 = jnp.concatenate([ca, ca, jnp.ones((seq, rest), F32)], axis=1)
    sa64 = jnp.concatenate([-sa_, zeros8, jnp.zeros((seq, rest), F32)], axis=1)
    sb64 = jnp.concatenate([zeros8, sa_, jnp.zeros((seq, rest), F32)], axis=1)
    cr, sr_ = cs(RET_KEY_DIM, RET_ROT_BASE)
    return (jnp.tile(c64, (1, 2)), jnp.tile(sa64, (1, 2)), jnp.tile(sb64, (1, 2)),
            jnp.concatenate([cr, cr], axis=1), jnp.concatenate([-sr_, sr_], axis=1))


def _ret_tables():
    c = RET_CHUNK
    lg = jnp.asarray(RET_LOG_GAMMA, F32)
    idx = jnp.arange(c, dtype=F32)
    diff = idx[:, None] - idx[None, :]
    intra = jnp.where(diff >= 0, jnp.exp(jnp.maximum(diff, 0.0) * lg[:, None, None]), 0.0)
    qd = jnp.exp((idx + 1.0)[None, :] * lg[:, None])[..., None]
    kd = jnp.exp((c - 1.0 - idx)[None, :] * lg[:, None])[..., None]
    return intra, jnp.broadcast_to(qd, (RET_HEADS, c, RET_KEY_DIM)), jnp.broadcast_to(kd, (RET_HEADS, c, RET_KEY_DIM))


def _inproj_fwd(x, g1, w_in, tabs, comm=None):
    seq = x.shape[0]
    tm = min(TOKEN_TILE, seq)

    def body(x_ref, g_ref, w_ref, ca_ref, sa_ref, sb_ref, cr_ref, sr_ref,
             xn_ref, qkv_ref, qkr_ref, vr_ref, gr_ref, gates_ref):
        xf = x_ref[...]
        r = lax.rsqrt(jnp.mean(xf * xf, axis=-1, keepdims=True) + NORM_EPS)
        xn = (xf * r * g_ref[...]).astype(BF)
        xn_ref[...] = xn
        pa = _nt(xn, w_ref[C_QA:C_QR, :])
        qk = _rope_att(pa[:, :C_VA], ca_ref[...], sa_ref[...], sb_ref[...])
        qkv_ref[:, :C_VA] = qk.astype(BF)
        qkv_ref[:, C_VA:] = pa[:, C_VA:].astype(BF)
        cr, sr = cr_ref[...], sr_ref[...]
        qr = _rope_ret(_nt(xn, w_ref[C_QR:C_KR, :]), cr, sr)
        qkr_ref[:, :512] = qr.astype(BF)
        kr = _rope_ret(_nt(xn, w_ref[C_KR:C_VR, :]), cr, sr) * RET_SCALE
        qkr_ref[:, 512:] = kr.astype(BF)
        vr_ref[...] = _nt(xn, w_ref[C_VR:C_GR, :]).astype(BF)
        gr_ref[...] = _nt(xn, w_ref[C_GR:C_GATES, :]).astype(BF)
        gates_ref[...] = _nt(xn, w_ref[C_GATES:C_END, :]).astype(BF)

    tab = _rows(tm, LANE)
    return _call(
        body, (x, g1, w_in, *tabs), comm, name="inproj_fwd", grid=(seq // tm,),
        in_specs=[_rows(tm, D_MODEL), VMEM_FULL, VMEM_FULL, tab, tab, tab, tab, tab],
        out_specs=[_rows(tm, D_MODEL), _rows(tm, 768), _rows(tm, 1024), _rows(tm, 1024), _rows(tm, 1024),
                   _rows(tm, 2048)],
        out_shape=[jax.ShapeDtypeStruct((seq, w), BF) for w in (D_MODEL, 768, 1024, 1024, 1024, 2048)])


def _kv_variants(w):
    lo = lax.broadcasted_iota(jnp.int32, w.shape, 1) < ATT_HEAD_DIM
    w0 = jnp.where(lo, w, 0.0)
    w1 = jnp.where(lo, 0.0, w)
    return {(0, 0): w0.astype(BF), (0, 1): pltpu.roll(w0, 64, 1).astype(BF),
            (1, 1): w1.astype(BF), (1, 0): pltpu.roll(w1, 64, 1).astype(BF)}


def _att_mask(i):
    c = ATT_BLOCK
    qi = lax.broadcasted_iota(jnp.int32, (c, 2 * c), 0)
    kj = lax.broadcasted_iota(jnp.int32, (c, 2 * c), 1)
    return (kj > qi) & (kj <= qi + c) & ((kj >= c) | (i > 0))


def _att_probs(qp, k_e, allowed, sk):
    s = _nt(qp, k_e) * ATT_SCALE
    s = jnp.where(allowed, s, NEG_INF)
    m = jnp.maximum(jnp.max(s, axis=1, keepdims=True), sk)
    pe = jnp.exp(s - m)
    inv = 1.0 / (jnp.sum(pe, axis=1, keepdims=True) + jnp.exp(sk - m))
    return pe * inv, jnp.exp(sk - m) * inv


def _attn_fwd(qkv, sinks, comm=None):
    seq = qkv.shape[0]
    c = ATT_BLOCK
    nb = seq // c

    def body(sink_ref, q_ref, kvc_ref, kvp_ref, o_ref):
        i = pl.program_id(0)
        kc, kp = kvc_ref[...], kvp_ref[...]
        kw = jnp.concatenate([kp[:, :LANE], kc[:, :LANE]], axis=0).astype(F32)
        vw = jnp.concatenate([kp[:, LANE:], kc[:, LANE:]], axis=0).astype(F32)
        kk, vv = _kv_variants(kw), _kv_variants(vw)
        allowed = _att_mask(i)
        for p in range(4):
            qp = q_ref[:, LANE * p:LANE * (p + 1)]
            acc = jnp.zeros((c, LANE), F32)
            for e in (0, 1):
                prob, _ = _att_probs(qp, kk[(p // 2, e)], allowed, sink_ref[2 * p + e])
                acc = acc + _nn(prob.astype(BF), vv[(p // 2, e)])
            o_ref[:, LANE * p:LANE * (p + 1)] = acc.astype(BF)

    return _call(
        body, (sinks, qkv, qkv, qkv), comm, name="attn_fwd", grid=(nb,),
        in_specs=[SMEM_FULL,
                  pl.BlockSpec((c, 512), lambda i: (i, 0)),
                  pl.BlockSpec((c, 256), lambda i: (i, 2)),
                  pl.BlockSpec((c, 256), lambda i: (jnp.maximum(i - 1, 0), 2))],
        out_specs=pl.BlockSpec((c, 512), lambda i: (i, 0)),
        out_shape=jax.ShapeDtypeStruct((seq, 512), BF))


def _ret_fwd(qkr, vr, gr, gain, rtabs, comm=None):
    seq = qkr.shape[0]
    c = RET_CHUNK
    nc = seq // c
    dk, dv = RET_KEY_DIM, RET_VAL_DIM

    def body(qk_ref, v_ref, g_ref, gain_ref, intra_ref, qd_ref, kd_ref, ret_ref, yrp_ref, st_ref, state):
        @pl.when(pl.program_id(0) == 0)
        def _():
            state[...] = jnp.zeros_like(state)

        for h in range(RET_HEADS):
            qh = qk_ref[:, dk * h:dk * (h + 1)]
            kh = qk_ref[:, 512 + dk * h:512 + dk * (h + 1)]
            vh = v_ref[:, dv * h:dv * (h + 1)]
            sh = state[h]
            shb = sh.astype(BF)
            st_ref[0, h] = shb
            att = _nt(qh, kh) * intra_ref[h]
            inner = _nn(att.astype(BF), vh)
            cross = _nn((qh.astype(F32) * qd_ref[h]).astype(BF), shb)
            out = inner + cross
            state[h] = sh * RET_CHUNK_DECAY[h] + _tn((kh.astype(F32) * kd_ref[h]).astype(BF), vh)
            ret_ref[:, dv * h:dv * (h + 1)] = out
            mu = jnp.mean(out, axis=-1, keepdims=True)
            d = out - mu
            var = jnp.mean(d * d, axis=-1, keepdims=True)
            y = d * lax.rsqrt(var + GN_EPS) * gain_ref[:, dv * h:dv * (h + 1)]
            g = g_ref[:, dv * h:dv * (h + 1)].astype(F32)
            yrp_ref[:, dv * h:dv * (h + 1)] = (g * _sigmoid(g) * y).astype(BF)

    return _call(
        body, (qkr, vr, gr, gain, *rtabs), comm, name="ret_fwd", grid=(nc,),
        in_specs=[_rows(c, 1024), _rows(c, 1024), _rows(c, 1024), VMEM_FULL, VMEM_FULL, VMEM_FULL, VMEM_FULL],
        out_specs=[_rows(c, 1024), _rows(c, 1024), pl.BlockSpec((1, RET_HEADS, dk, dv), lambda i: (i, 0, 0, 0))],
        out_shape=[jax.ShapeDtypeStruct((seq, 1024), F32), jax.ShapeDtypeStruct((seq, 1024), BF),
                   jax.ShapeDtypeStruct((nc, RET_HEADS, dk, dv), BF)],
        scratch_shapes=[pltpu.VMEM((RET_HEADS, dk, dv), F32)])


def _mix_fwd(oa, yrp, gates, bg, x, w_att, w_ret, w_out):
    seq = x.shape[0]
    tm = min(TOKEN_TILE, seq)

    def body(oa_ref, yrp_ref, gates_ref, bg_ref, x_ref, wa_ref, wr_ref, wo_ref, ya_ref, yr_ref, mg_ref, h1_ref):
        ya = _nt(oa_ref[...], wa_ref[...])
        yr = _nn(yrp_ref[...], wr_ref[...])
        gt = _sigmoid(gates_ref[...].astype(F32) + bg_ref[...])
        merged = (gt[:, :D_MODEL] * ya + gt[:, D_MODEL:] * yr).astype(BF)
        ya_ref[...] = ya.astype(BF)
        yr_ref[...] = yr.astype(BF)
        mg_ref[...] = merged
        h1_ref[...] = x_ref[...] + _nn(merged, wo_ref[...])

    return pl.pallas_call(
        body, name="mix_fwd", grid=(seq // tm,),
        in_specs=[_rows(tm, 512), _rows(tm, 1024), _rows(tm, 2048), VMEM_FULL, _rows(tm, D_MODEL),
                  VMEM_FULL, VMEM_FULL, VMEM_FULL],
        out_specs=[_rows(tm, D_MODEL)] * 4,
        out_shape=[jax.ShapeDtypeStruct((seq, D_MODEL), BF)] * 3 + [jax.ShapeDtypeStruct((seq, D_MODEL), F32)],
        compiler_params=_params(1),
    )(oa, yrp, gates, bg, x, w_att, w_ret, w_out)


def _mlp_fwd_loss(h1, g2, g3, target, w_ff1, w_ff2):
    seq = h1.shape[0]
    tm = min(TOKEN_TILE, seq)

    def body(h1_ref, g2_ref, g3_ref, t_ref, w1_ref, w2_ref, xn2_ref, u_ref, dh2_ref, loss_ref, dg3_ref):
        @pl.when(pl.program_id(0) == 0)
        def _():
            loss_ref[...] = jnp.zeros_like(loss_ref)
            dg3_ref[...] = jnp.zeros_like(dg3_ref)

        h1v = h1_ref[...]
        r2 = lax.rsqrt(jnp.mean(h1v * h1v, axis=-1, keepdims=True) + NORM_EPS)
        xn2 = (h1v * r2 * g2_ref[...]).astype(BF)
        xn2_ref[...] = xn2
        u = _nt(xn2, w1_ref[...])
        u_ref[...] = u.astype(BF)
        hdn = jnp.square(jnp.maximum(u, 0.0)).astype(BF)
        h2 = h1v + _nn(hdn, w2_ref[...])
        r3 = lax.rsqrt(jnp.mean(h2 * h2, axis=-1, keepdims=True) + NORM_EPS)
        hn = h2 * r3
        err = hn * g3_ref[...] - t_ref[...]
        loss_ref[...] += jnp.sum(err * err) * (0.5 / D_MODEL)
        dy = err * (1.0 / D_MODEL)
        dg3_ref[...] += jnp.sum(dy * hn, axis=0, keepdims=True)
        z = dy * g3_ref[...]
        dh2_ref[...] = r3 * (z - hn * jnp.mean(z * hn, axis=-1, keepdims=True))

    return pl.pallas_call(
        body, name="mlp_fwd_loss", grid=(seq // tm,),
        in_specs=[_rows(tm, D_MODEL), VMEM_FULL, VMEM_FULL, _rows(tm, D_MODEL), VMEM_FULL, VMEM_FULL],
        out_specs=[_rows(tm, D_MODEL), _rows(tm, D_FF), _rows(tm, D_MODEL),
                   pl.BlockSpec((1, LANE), lambda i: (0, 0)), pl.BlockSpec((1, D_MODEL), lambda i: (0, 0))],
        out_shape=[jax.ShapeDtypeStruct((seq, D_MODEL), BF), jax.ShapeDtypeStruct((seq, D_FF), BF),
                   jax.ShapeDtypeStruct((seq, D_MODEL), F32), jax.ShapeDtypeStruct((1, LANE), F32),
                   jax.ShapeDtypeStruct((1, D_MODEL), F32)],
        compiler_params=_params(1),
    )(h1, g2, g3, target, w_ff1, w_ff2)


def _rms_bwd(dxn, xin, gain):
    r = lax.rsqrt(jnp.mean(xin * xin, axis=-1, keepdims=True) + NORM_EPS)
    xhat = xin * r
    z = dxn * gain
    dxin = r * (z - xhat * jnp.mean(z * xhat, axis=-1, keepdims=True))
    return dxin, jnp.sum(dxn * xhat, axis=0, keepdims=True)


def _mlp_bwd(dh2, u, h1, g2, w_ff1, w_ff2):
    seq = h1.shape[0]
    tm = min(TOKEN_TILE, seq)

    def body(dh2_ref, u_ref, h1_ref, g2_ref, w1_ref, w2_ref, du_ref, dh1_ref, dg2_ref):
        @pl.when(pl.program_id(0) == 0)
        def _():
            dg2_ref[...] = jnp.zeros_like(dg2_ref)

        dh2v = dh2_ref[...]
        dhdn = _nt(dh2v.astype(BF), w2_ref[...])
        du = (dhdn * (2.0 * jnp.maximum(u_ref[...].astype(F32), 0.0))).astype(BF)
        du_ref[...] = du
        dxn2 = _nn(du, w1_ref[...])
        dnorm, dg = _rms_bwd(dxn2, h1_ref[...], g2_ref[...])
        dh1_ref[...] = dh2v + dnorm
        dg2_ref[...] += dg

    return pl.pallas_call(
        body, name="mlp_bwd", grid=(seq // tm,),
        in_specs=[_rows(tm, D_MODEL), _rows(tm, D_FF), _rows(tm, D_MODEL), VMEM_FULL, VMEM_FULL, VMEM_FULL],
        out_specs=[_rows(tm, D_FF), _rows(tm, D_MODEL), pl.BlockSpec((1, D_MODEL), lambda i: (0, 0))],
        out_shape=[jax.ShapeDtypeStruct((seq, D_FF), BF), jax.ShapeDtypeStruct((seq, D_MODEL), F32),
                   jax.ShapeDtypeStruct((1, D_MODEL), F32)],
        compiler_params=_params(1),
    )(dh2, u, h1, g2, w_ff1, w_ff2)


def _mm_tn(a, b, name, relu_sq=False, comm=None):
    seq, kdim = a.shape
    ndim = b.shape[1]
    ts = min(TN_TOKEN_TILE, seq)
    tk = min(kdim, 1024)
    tn = min(ndim, 1024)
    n_steps = seq // ts

    def body(a_ref, b_ref, o_ref):
        @pl.when(pl.program_id(2) == 0)
        def _():
            o_ref[...] = jnp.zeros_like(o_ref)

        av = a_ref[...]
        if relu_sq:
            av = jnp.square(jnp.maximum(av.astype(F32), 0.0))
        o_ref[...] += _tn(av.astype(BF), b_ref[...].astype(BF))

    res = _call(
        body, (a, b), comm, name=name, grid=(kdim // tk, ndim // tn, n_steps),
        in_specs=[pl.BlockSpec((ts, tk), lambda k, n, s: (s, k)), pl.BlockSpec((ts, tn), lambda k, n, s: (s, n))],
        out_specs=pl.BlockSpec((tk, tn), lambda k, n, s: (k, n)),
        out_shape=jax.ShapeDtypeStruct((kdim, ndim), F32))
    return res[0] if comm is None else res


def _dw_in_t(pieces, xn1, comm=None):
    seq = xn1.shape[0]
    ts = min(2 * TN_TOKEN_TILE, seq)
    tr = 256
    n_blk = [p.shape[1] // tr for p in pieces]
    offs = [sum(n_blk[:i]) for i in range(len(pieces) + 1)]
    n_p = len(pieces)

    def body(*refs):
        piece_refs, xn_ref, o_ref = refs[:n_p], refs[n_p], refs[n_p + 1]
        r, s = pl.program_id(0), pl.program_id(1)

        @pl.when(s == 0)
        def _():
            o_ref[...] = jnp.zeros_like(o_ref)

        for p in range(n_p):
            @pl.when((r >= offs[p]) & (r < offs[p + 1]))
            def _(p=p):
                o_ref[...] += _tn(piece_refs[p][...], xn_ref[pl.ds(pl.multiple_of(s * ts, ts), ts), :])

    def piece_spec(p):
        def index(r, s):
            inside = (r >= offs[p]) & (r < offs[p + 1])
            return jnp.where(inside, s, 0), jnp.clip(r - offs[p], 0, n_blk[p] - 1)
        return pl.BlockSpec((ts, tr), index)

    return _call(
        body, (*pieces, xn1), comm, name="dw_in", grid=(offs[-1], seq // ts),
        in_specs=[piece_spec(p) for p in range(n_p)] + [VMEM_FULL],
        out_specs=pl.BlockSpec((tr, D_MODEL), lambda r, s: (r, 0)),
        out_shape=jax.ShapeDtypeStruct((offs[-1] * tr, D_MODEL), F32))


def _mix_bwd(dh1, ya, yr, gates, bg, w_att, w_ret, w_out, comm=None):
    seq = dh1.shape[0]
    tm = min(TOKEN_TILE, seq)

    def body(dh1_ref, ya_ref, yr_ref, gates_ref, bg_ref, wa_ref, wr_ref, wo_ref,
             dya_ref, dyr_ref, dgates_ref, doa_ref, dyrp_ref, db_ref):
        @pl.when(pl.program_id(0) == 0)
        def _():
            db_ref[...] = jnp.zeros_like(db_ref)

        dm = _nt(dh1_ref[...].astype(BF), wo_ref[...])
        gt = _sigmoid(gates_ref[...].astype(F32) + bg_ref[...])
        ga, gr = gt[:, :D_MODEL], gt[:, D_MODEL:]
        dya = (dm * ga).astype(BF)
        dyr = (dm * gr).astype(BF)
        dya_ref[...] = dya
        dyr_ref[...] = dyr
        dga = dm * ya_ref[...].astype(F32) * ga * (1.0 - ga)
        dgr = dm * yr_ref[...].astype(F32) * gr * (1.0 - gr)
        dgates_ref[:, :D_MODEL] = dga.astype(BF)
        dgates_ref[:, D_MODEL:] = dgr.astype(BF)
        db_ref[:, :D_MODEL] += jnp.sum(dga, axis=0, keepdims=True)
        db_ref[:, D_MODEL:] += jnp.sum(dgr, axis=0, keepdims=True)
        doa_ref[...] = _nn(dya, wa_ref[...]).astype(BF)
        dyrp_ref[...] = _nt(dyr, wr_ref[...]).astype(BF)

    return _call(
        body, (dh1, ya, yr, gates, bg, w_att, w_ret, w_out), comm, name="mix_bwd", grid=(seq // tm,),
        in_specs=[_rows(tm, D_MODEL), _rows(tm, D_MODEL), _rows(tm, D_MODEL), _rows(tm, 2048), VMEM_FULL,
                  VMEM_FULL, VMEM_FULL, VMEM_FULL],
        out_specs=[_rows(tm, D_MODEL), _rows(tm, D_MODEL), _rows(tm, 2048), _rows(tm, 512), _rows(tm, 1024),
                   pl.BlockSpec((1, 2048), lambda i: (0, 0))],
        out_shape=[jax.ShapeDtypeStruct((seq, D_MODEL), BF), jax.ShapeDtypeStruct((seq, D_MODEL), BF),
                   jax.ShapeDtypeStruct((seq, 2048), BF), jax.ShapeDtypeStruct((seq, 512), BF),
                   jax.ShapeDtypeStruct((seq, 1024), BF), jax.ShapeDtypeStruct((1, 2048), F32)])


def _ret_bwd(qkr, vr, gr, ret, dyrp, states, gain, rtabs, cr, sr, comm=None):
    seq = qkr.shape[0]
    c = RET_CHUNK
    nc = seq // c
    dk, dv = RET_KEY_DIM, RET_VAL_DIM

    def body(qk_ref, v_ref, g_ref, ret_ref, dyp_ref, st_ref, gain_ref, intra_ref, qd_ref, kd_ref, cr_ref, sr_ref,
             dall_ref, dgain_ref, dstate):
        @pl.when(pl.program_id(0) == 0)
        def _():
            dstate[...] = jnp.zeros_like(dstate)
            dgain_ref[...] = jnp.zeros_like(dgain_ref)

        crv, srv = cr_ref[...], sr_ref[...]
        for h in range(RET_HEADS):
            vs = slice(dv * h, dv * (h + 1))
            qh = qk_ref[:, dk * h:dk * (h + 1)]
            kh = qk_ref[:, 512 + dk * h:512 + dk * (h + 1)]
            vh = v_ref[:, vs]
            out = ret_ref[:, vs]
            g = g_ref[:, vs].astype(F32)
            dyp = dyp_ref[:, vs].astype(F32)
            gain_h = gain_ref[:, vs]
            mu = jnp.mean(out, axis=-1, keepdims=True)
            d = out - mu
            rstd = lax.rsqrt(jnp.mean(d * d, axis=-1, keepdims=True) + GN_EPS)
            yn = d * rstd
            sg = _sigmoid(g)
            dg = dyp * (yn * gain_h) * (sg * (1.0 + g * (1.0 - sg)))
            dy = dyp * (g * sg)
            dgain_ref[:, vs] += jnp.sum(dy * yn, axis=0, keepdims=True)
            dyn = dy * gain_h
            dout = rstd * (dyn - jnp.mean(dyn, axis=-1, keepdims=True)
                           - yn * jnp.mean(dyn * yn, axis=-1, keepdims=True))
            doutb = dout.astype(BF)
            sc = st_ref[0, h]
            dsp = dstate[h]
            dspb = dsp.astype(BF)
            intra, qdv, kdv = intra_ref[h], qd_ref[h], kd_ref[h]
            att = _nt(qh, kh) * intra
            dab = (_nt(doutb, vh) * intra).astype(BF)
            qdec = (qh.astype(F32) * qdv).astype(BF)
            kdec = (kh.astype(F32) * kdv).astype(BF)
            dq = _nn(dab, kh) + _nt(doutb, sc) * qdv
            dkk = _tn(dab, qh) + _nt(vh, dspb) * kdv
            dvv = _tn(att.astype(BF), doutb) + _nn(kdec, dspb)
            dstate[h] = dsp * RET_CHUNK_DECAY[h] + _tn(qdec, doutb)
            dall_ref[:, dk * h:dk * (h + 1)] = _rope_ret_t(dq, crv, srv).astype(BF)
            dall_ref[:, 512 + dk * h:512 + dk * (h + 1)] = (_rope_ret_t(dkk, crv, srv) * RET_SCALE).astype(BF)
            dall_ref[:, 1024 + dv * h:1024 + dv * (h + 1)] = dvv.astype(BF)
            dall_ref[:, 2048 + dv * h:2048 + dv * (h + 1)] = dg.astype(BF)

    def rev(width):
        return pl.BlockSpec((c, width), lambda i: (nc - 1 - i, 0))

    return _call(
        body, (qkr, vr, gr, ret, dyrp, states, gain, *rtabs, cr, sr), comm, name="ret_bwd", grid=(nc,),
        in_specs=[rev(1024), rev(1024), rev(1024), rev(1024), rev(1024),
                  pl.BlockSpec((1, RET_HEADS, dk, dv), lambda i: (nc - 1 - i, 0, 0, 0)),
                  VMEM_FULL, VMEM_FULL, VMEM_FULL, VMEM_FULL, rev(LANE), rev(LANE)],
        out_specs=[rev(3072), pl.BlockSpec((1, 1024), lambda i: (0, 0))],
        out_shape=[jax.ShapeDtypeStruct((seq, 3072), BF), jax.ShapeDtypeStruct((1, 1024), F32)],
        scratch_shapes=[pltpu.VMEM((RET_HEADS, dk, dv), F32)])


def _attn_bwd(qkv, doa, sinks, ca, sa, sb, comm=None):
    seq = qkv.shape[0]
    c = ATT_BLOCK
    nb = seq // c

    def body(sink_ref, q_ref, kvc_ref, kvp_ref, do_ref, cac_ref, sac_ref, sbc_ref, cap_ref, sap_ref, sbp_ref,
             dq_ref, dkv_ref, dsink_ref, carry):
        i = pl.program_id(0)

        @pl.when(i == 0)
        def _():
            carry[...] = jnp.zeros_like(carry)
            dsink_ref[...] = jnp.zeros_like(dsink_ref)

        def flush(total):
            dk_pre = _rope_att_t(total[:, :LANE], cap_ref[...], sap_ref[...], sbp_ref[...])
            dkv_ref[:, :LANE] = dk_pre.astype(BF)
            dkv_ref[:, LANE:] = total[:, LANE:].astype(BF)

        @pl.when(i < nb)
        def _():
            kc, kp = kvc_ref[...], kvp_ref[...]
            kw = jnp.concatenate([kp[:, :LANE], kc[:, :LANE]], axis=0).astype(F32)
            vw = jnp.concatenate([kp[:, LANE:], kc[:, LANE:]], axis=0).astype(F32)
            kk, vv = _kv_variants(kw), _kv_variants(vw)
            allowed = _att_mask(i)
            lane1 = lax.broadcasted_iota(jnp.int32, (1, LANE), 1)
            gk = {key: jnp.zeros((2 * c, LANE), F32) for key in kk}
            gv = {key: jnp.zeros((2 * c, LANE), F32) for key in kk}
            dsink = jnp.zeros((1, LANE), F32)
            dqs = []
            for p in range(4):
                qp = q_ref[:, LANE * p:LANE * (p + 1)]
                dop = do_ref[:, LANE * p:LANE * (p + 1)]
                dqp = jnp.zeros((c, LANE), F32)
                for e in (0, 1):
                    key = (p // 2, e)
                    prob, psink = _att_probs(qp, kk[key], allowed, sink_ref[2 * p + e])
                    dprob = _nt(dop, vv[key])
                    drow = jnp.sum(dprob * prob, axis=1, keepdims=True)
                    ds = (prob * (dprob - drow) * ATT_SCALE).astype(BF)
                    dqp = dqp + _nn(ds, kk[key])
                    gk[key] = gk[key] + _tn(ds, qp)
                    gv[key] = gv[key] + _tn(prob.astype(BF), dop)
                    dsink = dsink + jnp.where(lane1 == 2 * p + e, -jnp.sum(psink * drow), 0.0)
                dqs.append(dqp)
            dq = jnp.concatenate(dqs, axis=1)
            dq_ref[...] = _rope_att_t(dq, cac_ref[...], sac_ref[...], sbc_ref[...]).astype(BF)
            dsink_ref[...] += dsink
            lo = lax.broadcasted_iota(jnp.int32, (2 * c, LANE), 1) < ATT_HEAD_DIM

            def fold(gg):
                return jnp.where(lo, gg[(0, 0)] + pltpu.roll(gg[(0, 1)], 64, 1),
                                 gg[(1, 1)] + pltpu.roll(gg[(1, 0)], 64, 1))

            dkw, dvw = fold(gk), fold(gv)
            flush(carry[...] + jnp.concatenate([dkw[:c], dvw[:c]], axis=1))
            carry[...] = jnp.concatenate([dkw[c:], dvw[c:]], axis=1)

        @pl.when(i == nb)
        def _():
            flush(carry[...])

    cur = lambda i: (jnp.minimum(i, nb - 1), 0)
    prev = lambda i: (jnp.maximum(i - 1, 0), 0)
    return _call(
        body, (sinks, qkv, qkv, qkv, doa, ca, sa, sb, ca, sa, sb), comm, name="attn_bwd", grid=(nb + 1,),
        in_specs=[SMEM_FULL,
                  pl.BlockSpec((c, 512), cur),
                  pl.BlockSpec((c, 256), lambda i: (jnp.minimum(i, nb - 1), 2)),
                  pl.BlockSpec((c, 256), lambda i: (jnp.clip(i - 1, 0, nb - 1), 2)),
                  pl.BlockSpec((c, 512), cur),
                  pl.BlockSpec((c, LANE), cur), pl.BlockSpec((c, LANE), cur), pl.BlockSpec((c, LANE), cur),
                  pl.BlockSpec((c, LANE), prev), pl.BlockSpec((c, LANE), prev), pl.BlockSpec((c, LANE), prev)],
        out_specs=[pl.BlockSpec((c, 512), cur), pl.BlockSpec((c, 256), prev),
                   pl.BlockSpec((1, LANE), lambda i: (0, 0))],
        out_shape=[jax.ShapeDtypeStruct((seq, 512), BF), jax.ShapeDtypeStruct((seq, 256), BF),
                   jax.ShapeDtypeStruct((1, LANE), F32)],
        scratch_shapes=[pltpu.VMEM((c, 256), F32)])


def _inproj_bwd(dqa, dkva, dret, dgates, dh1, x, g1, w_in, comm=None):
    seq = x.shape[0]
    tm = min(TOKEN_TILE, seq)

    def body(dqa_ref, dkva_ref, dret_ref, dgates_ref, dh1_ref, x_ref, g_ref, w_ref, dx_ref, dg1_ref):
        @pl.when(pl.program_id(0) == 0)
        def _():
            dg1_ref[...] = jnp.zeros_like(dg1_ref)

        dxn = (_nn(dqa_ref[...], w_ref[C_QA:C_KA, :]) + _nn(dkva_ref[...], w_ref[C_KA:C_QR, :])
               + _nn(dret_ref[...], w_ref[C_QR:C_GATES, :]) + _nn(dgates_ref[...], w_ref[C_GATES:C_END, :]))
        dnorm, dg = _rms_bwd(dxn, x_ref[...], g_ref[...])
        dx_ref[...] = dh1_ref[...] + dnorm
        dg1_ref[...] += dg

    return _call(
        body, (dqa, dkva, dret, dgates, dh1, x, g1, w_in), comm, name="inproj_bwd", grid=(seq // tm,),
        in_specs=[_rows(tm, 512), _rows(tm, 256), _rows(tm, 3072), _rows(tm, 2048), _rows(tm, D_MODEL),
                  _rows(tm, D_MODEL), VMEM_FULL, VMEM_FULL],
        out_specs=[_rows(tm, D_MODEL), pl.BlockSpec((1, D_MODEL), lambda i: (0, 0))],
        out_shape=[jax.ShapeDtypeStruct((seq, D_MODEL), F32), jax.ShapeDtypeStruct((1, D_MODEL), F32)])


def _local_step(x, target, g1, bg, sinks, gain, g2, g3, w_in, shards):
    seq = x.shape[0]
    ca, sa, sb, cr, sr = _rope_tables(seq)
    rtabs = _ret_tables()
    scatter = _Scatter

    comm, unpack = _gather(shards, ("w_att_up", "w_ret_up", "w_out"))
    (xn1, qkv, qkr, vr, gr, gates), got = _inproj_fwd(x, g1, w_in, (ca, sa, sb, cr, sr), comm)
    w_att, w_ret, w_out = unpack(got)
    comm, unpack = _gather(shards, ("w_ff1",))
    oa, got = _attn_fwd(qkv, sinks, comm)
    (w_ff1,) = unpack(got)
    comm, unpack = _gather(shards, ("w_ff2",))
    (ret, yrp, states), got = _ret_fwd(qkr, vr, gr, gain, rtabs, comm)
    (w_ff2,) = unpack(got)
    ya, yr, merged, h1 = _mix_fwd(oa, yrp, gates, bg, x, w_att, w_ret, w_out)
    xn2, u, dh2, loss, dg3 = _mlp_fwd_loss(h1, g2, g3, target, w_ff1, w_ff2)

    du, dh1, dg2 = _mlp_bwd(dh2, u, h1, g2, w_ff1, w_ff2)
    ff = scatter(dict(w_ff1=_mm_tn(du, xn2, "dw_ff1"), w_ff2=_mm_tn(u, dh2, "dw_ff2", relu_sq=True)), "ff")
    (dya, dyr, dgates, doa, dyrp, db), r1 = _mix_bwd(dh1, ya, yr, gates, bg, w_att, w_ret, w_out, ff.pair_comm())
    ff.pair_done(r1)
    mid = scatter(dict(w_att_up=_mm_tn(dya, oa, "dw_att_up"), w_ret_up=_mm_tn(yrp, dyr, "dw_ret_up"),
                       w_out=_mm_tn(merged, dh1, "dw_out")), "mid")
    (dqa, dkva, dsink), r2 = _attn_bwd(qkv, doa, sinks, ca, sa, sb, ff.chip_comm())
    ff.chip_done(r2)
    (dret, dgain), r1 = _ret_bwd(qkr, vr, gr, ret, dyrp, states, gain, rtabs, cr, sr, mid.pair_comm())
    mid.pair_done(r1)
    d_in, r2 = _dw_in_t((dqa, dkva, dret, dgates), xn1, mid.chip_comm())
    mid.chip_done(r2)
    win = scatter(dict(w_in=d_in), "in")
    win.pair_done(_comm_alone(win.pair_comm(), "pair_exchange_in"))
    (dx, dg1), r2 = _inproj_bwd(dqa, dkva, dret, dgates, dh1, x, g1, w_in, win.chip_comm())
    win.chip_done(r2)
    small = dict(norm_mix_gain=dg1, b_gates=db, attn_sinks=dsink, ret_gn_gain=dgain, norm_mlp_gain=dg2,
                 norm_final_gain=dg3)
    return loss, dx, small, (win, mid, ff)


def _coords():
    return lax.axis_index("x"), lax.axis_index("y"), lax.axis_index("c")


def _flip(v, bit):
    return 1 - v if bit else v


def _gather_comm(shards):
    n = len(shards)

    def parts(ins, outs, sems):
        send_sems, recv_sems, local_sems = sems
        x, y, c = _coords()
        me, sib = (x, y, c), (x, y, 1 - c)
        chips = [(1 - x, y), (x, 1 - y), (1 - x, 1 - y)]

        def slot(a, blk):
            return outs[a].at[4 * blk[0] + 2 * blk[1] + blk[2]]

        def copy(a, k, blk, to, src=None):
            return pltpu.make_async_remote_copy(
                src_ref=slot(a, blk) if src is None else src, dst_ref=slot(a, blk),
                send_sem=send_sems.at[a, k], recv_sem=recv_sems.at[a, k], device_id=to, device_id_type=MESH)

        mine = [pltpu.make_async_copy(ins[a], slot(a, me), local_sems.at[a]) for a in range(n)]
        first = []
        for a in range(n):
            first.append(copy(a, 0, me, sib, src=ins[a]))
            first += [copy(a, 1 + j, me, (*chip, c), src=ins[a]) for j, chip in enumerate(chips)]
        return me, sib, chips, c, copy, mine, first

    def start(ins, outs, sems):
        *_, mine, first = parts(ins, outs, sems)
        for cp in mine + first:
            cp.start()

    def finish(ins, outs, sems):
        me, sib, chips, c, copy, mine, first = parts(ins, outs, sems)
        passed = []
        for j, chip in enumerate(chips):
            for a in range(n):
                copy(a, 1 + j, (*chip, c), me).wait_recv()
                fwd = copy(a, 4 + j, (*chip, c), sib)
                fwd.start()
                passed.append(fwd)
        for a in range(n):
            copy(a, 0, sib, me).wait_recv()
            for j, chip in enumerate(chips):
                copy(a, 4 + j, (*chip, 1 - c), me).wait_recv()
        for cp in first + passed:
            cp.wait_send()
        for cp in mine:
            cp.wait()

    return _Comm(list(shards), [jax.ShapeDtypeStruct((N_DEV,) + s.shape, s.dtype) for s in shards],
                 [pltpu.SemaphoreType.DMA((n, 7)), pltpu.SemaphoreType.DMA((n, 7)), pltpu.SemaphoreType.DMA((n,))],
                 start, finish)


COLUMN_SHARDED = ("w_in", "w_att_up", "w_ff1")


def _gather(shards, names):
    def unpack(got):
        return [g.reshape(N_DEV * g.shape[1], g.shape[2]) for g in got]

    return _gather_comm([shards[k] for k in names]), unpack


def _pair_comm(grads):
    n = len(grads)

    def copies(g, r1, sems):
        send_sems, recv_sems = sems
        x, y, c = _coords()
        return [pltpu.make_async_remote_copy(
            src_ref=g[a].at[2 * j + (1 - c)], dst_ref=r1[a].at[j], send_sem=send_sems.at[a, j],
            recv_sem=recv_sems.at[a, j], device_id=(x, y, 1 - c), device_id_type=MESH)
            for a in range(n) for j in range(4)]

    def start(g, r1, sems):
        for cp in copies(g, r1, sems):
            cp.start()

    def finish(g, r1, sems):
        for cp in copies(g, r1, sems):
            cp.wait_recv()
        for cp in copies(g, r1, sems):
            cp.wait_send()

    return _Comm(list(grads), [jax.ShapeDtypeStruct((4,) + g.shape[1:], g.dtype) for g in grads],
                 [pltpu.SemaphoreType.DMA((n, 4)), pltpu.SemaphoreType.DMA((n, 4))], start, finish)


def _small_comm(small):
    def parts(ins, outs, sems):
        (small_ref,), (small_all,) = ins, outs
        ssend, srecv, lsem = sems
        x, y, c = _coords()
        me_idx = 4 * x + 2 * y + c
        own = pltpu.make_async_copy(small_ref, small_all.at[me_idx], lsem)
        sends, recvs = [], []
        for r in range(1, N_DEV):
            px, py, pc = _flip(x, r & 4), _flip(y, r & 2), _flip(c, r & 1)
            sends.append(pltpu.make_async_remote_copy(
                src_ref=small_ref, dst_ref=small_all.at[me_idx], send_sem=ssend.at[r - 1], recv_sem=srecv.at[r - 1],
                device_id=(px, py, pc), device_id_type=MESH))
            recvs.append(pltpu.make_async_remote_copy(
                src_ref=small_ref, dst_ref=small_all.at[4 * px + 2 * py + pc], send_sem=ssend.at[r - 1],
                recv_sem=srecv.at[r - 1], device_id=(px, py, pc), device_id_type=MESH))
        return own, sends, recvs

    def start(ins, outs, sems):
        own, sends, _ = parts(ins, outs, sems)
        own.start()
        for cp in sends:
            cp.start()

    def finish(ins, outs, sems):
        own, sends, recvs = parts(ins, outs, sems)
        for cp in recvs:
            cp.wait_recv()
        for cp in sends:
            cp.wait_send()
        own.wait()

    return _Comm([small], [jax.ShapeDtypeStruct((N_DEV,) + small.shape, small.dtype)],
                 [pltpu.SemaphoreType.DMA((N_DEV - 1,)), pltpu.SemaphoreType.DMA((N_DEV - 1,)),
                  pltpu.SemaphoreType.DMA], start, finish)


def _pair_sum(grads, r1s, c_arr, tag):
    n = len(grads)
    q = 4

    def body(c_ref, *refs):
        g, r, t = refs[:n], refs[n:2 * n], refs[2 * n:]
        for a in range(n):
            t[a][...] = (g[a][...] + r[a][...]).astype(t[a].dtype)

    def blk(arr):
        return (1, arr.shape[1] // q, arr.shape[2])

    grid_spec = pltpu.PrefetchScalarGridSpec(
        num_scalar_prefetch=1, grid=(4, q),
        in_specs=[pl.BlockSpec(blk(g), lambda j, s, c_ref: (2 * j + c_ref[0], s, 0)) for g in grads]
        + [pl.BlockSpec(blk(r), lambda j, s, c_ref: (j, s, 0)) for r in r1s],
        out_specs=[pl.BlockSpec(blk(r), lambda j, s, c_ref: (j, s, 0)) for r in r1s])
    return pl.pallas_call(
        body, name="pair_sum_" + tag, grid_spec=grid_spec,
        out_shape=[jax.ShapeDtypeStruct(r.shape, RS_PAYLOAD) for r in r1s],
        compiler_params=_params(2),
    )(c_arr, *grads, *r1s)


def _chip_comm(ts):
    n = len(ts)

    def copies(t, r2, sems):
        send_sems, recv_sems = sems
        x, y, c = _coords()
        out = []
        for a in range(n):
            for r in range(1, 4):
                tx, ty = _flip(x, r & 2), _flip(y, r & 1)
                out.append(pltpu.make_async_remote_copy(
                    src_ref=t[a].at[2 * tx + ty], dst_ref=r2[a].at[r - 1], send_sem=send_sems.at[a, r - 1],
                    recv_sem=recv_sems.at[a, r - 1], device_id=(tx, ty, c), device_id_type=MESH))
        return out

    def start(t, r2, sems):
        for cp in copies(t, r2, sems):
            cp.start()

    def finish(t, r2, sems):
        for cp in copies(t, r2, sems):
            cp.wait_recv()
        for cp in copies(t, r2, sems):
            cp.wait_send()

    return _Comm(list(ts), [jax.ShapeDtypeStruct((3,) + t.shape[1:], t.dtype) for t in ts],
                 [pltpu.SemaphoreType.DMA((n, 3)), pltpu.SemaphoreType.DMA((n, 3))], start, finish)


class _Scatter:
    def __init__(self, grads, tag):
        self.names, self.tag = tuple(grads), tag
        self.stacks = [g.reshape(N_DEV, g.shape[0] // N_DEV, g.shape[1]) for g in grads.values()]

    def pair_comm(self):
        return _pair_comm(self.stacks)

    def pair_done(self, r1s):
        self.r1s = list(r1s)
        c_arr = jnp.reshape(lax.axis_index("c"), (1,)).astype(jnp.int32)
        self.ts = _pair_sum(self.stacks, self.r1s, c_arr, self.tag)

    def chip_comm(self):
        return _chip_comm(self.ts)

    def chip_done(self, r2s):
        self.r2s = list(r2s)


def _adamw(w, g, m, v):
    m = ADAM_B1 * m + (1.0 - ADAM_B1) * g
    v = ADAM_B2 * v + (1.0 - ADAM_B2) * jnp.square(g)
    m_hat = m / (1.0 - ADAM_B1 ** ADAM_STEP)
    v_hat = v / (1.0 - ADAM_B2 ** ADAM_STEP)
    delta = -ADAM_LR * (m_hat / (jnp.sqrt(v_hat) + ADAM_EPS) + ADAM_WD * w)
    return delta, m, v


ADAM_STEPS = 4


def _piece_specs(stacks):
    def rows(s):
        return s.shape[1] // ADAM_STEPS
    return ([pl.BlockSpec((1, rows(s), s.shape[2]), lambda i, idx_ref: (idx_ref[0], i, 0)) for s in stacks]
            + [pl.BlockSpec((1, rows(s), s.shape[2]), lambda i, idx_ref: (idx_ref[1], i, 0)) for s in stacks]
            + [pl.BlockSpec((3, rows(s), s.shape[2]), lambda i, idx_ref: (0, i, 0)) for s in stacks])


def _piece_sum(g0, r1, r2):
    return (((g0[0] + r1[0]) + r2[0].astype(F32)) + r2[1].astype(F32)) + r2[2].astype(F32)


def _shard_sum(stacks, r1s, r2s, idx_arr):
    n = len(stacks)

    def body(idx_ref, *refs):
        g0, r1, r2, outs = (refs[k * n:(k + 1) * n] for k in range(4))
        for a in range(n):
            outs[a][...] = _piece_sum(g0[a], r1[a], r2[a])

    out_specs = [pl.BlockSpec((s.shape[1] // ADAM_STEPS, s.shape[2]), lambda i, idx_ref: (i, 0)) for s in stacks]
    grid_spec = pltpu.PrefetchScalarGridSpec(num_scalar_prefetch=1, grid=(ADAM_STEPS,),
                                             in_specs=_piece_specs(stacks), out_specs=out_specs)
    return pl.pallas_call(
        body, name="shard_sum", grid_spec=grid_spec,
        out_shape=[jax.ShapeDtypeStruct(s.shape[1:], F32) for s in stacks],
        compiler_params=_params(1),
    )(idx_arr, *stacks, *r1s, *r2s)


def _adam_big(pieces, summed, ws, ms, vs, idx_arr):
    stacks, r1s, r2s = pieces
    n_p, n = len(stacks), len(ws)

    def body(idx_ref, *refs):
        it = iter(refs)
        g0, r1, r2, gs, w, m, v = ([next(it) for _ in range(k)] for k in (n_p, n_p, n_p, n - n_p, n, n, n))
        outs = list(it)
        for a in range(n):
            g = _piece_sum(g0[a], r1[a], r2[a]) if a < n_p else gs[a - n_p][...]
            delta, nm, nv = _adamw(w[a][...], g, m[a][...], v[a][...])
            outs[4 * a][...] = g
            outs[4 * a + 1][...] = delta
            outs[4 * a + 2][...] = nm
            outs[4 * a + 3][...] = nv

    def shard_spec(w):
        return pl.BlockSpec((w.shape[0] // ADAM_STEPS, w.shape[1]), lambda i, idx_ref: (i, 0))

    in_specs = _piece_specs(stacks) + [shard_spec(w) for w in ws[n_p:]] + [shard_spec(w) for w in ws] * 3
    out_specs = [shard_spec(w) for w in ws for _ in range(4)]
    grid_spec = pltpu.PrefetchScalarGridSpec(num_scalar_prefetch=1, grid=(ADAM_STEPS,), in_specs=in_specs,
                                             out_specs=out_specs)
    return pl.pallas_call(
        body, name="adam_big", grid_spec=grid_spec,
        out_shape=[jax.ShapeDtypeStruct(w.shape, F32) for w in ws for _ in range(4)],
        compiler_params=_params(1),
    )(idx_arr, *stacks, *r1s, *r2s, *summed, *ws, *ms, *vs)


def _adam_small(small_all, w, m, v):
    def body(all_ref, w_ref, m_ref, v_ref, g_ref, d_ref, nm_ref, nv_ref):
        g = all_ref[0]
        for k in range(1, N_DEV):
            g = g + all_ref[k]
        delta, nm, nv = _adamw(w_ref[...], g, m_ref[...], v_ref[...])
        g_ref[...] = g
        d_ref[...] = delta
        nm_ref[...] = nm
        nv_ref[...] = nv

    return pl.pallas_call(
        body, name="adam_small",
        in_specs=[VMEM_FULL] * 4, out_specs=[VMEM_FULL] * 4,
        out_shape=[jax.ShapeDtypeStruct(w.shape, F32)] * 4,
    )(small_all, w, m, v)


RS_PAYLOAD = BF


def _pack_rows(rows):
    rid = lax.broadcasted_iota(jnp.int32, (8, D_MODEL), 0)
    out = jnp.zeros((8, D_MODEL), F32)
    for i, r in enumerate(rows):
        out = jnp.where(rid == i, jnp.broadcast_to(r, (8, D_MODEL)), out)
    return out


def _small_rows(norm_mix_gain, b_gates, attn_sinks, ret_gn_gain, norm_mlp_gain, norm_final_gain):
    return [norm_mix_gain, b_gates[:, :D_MODEL], b_gates[:, D_MODEL:], ret_gn_gain, norm_mlp_gain,
            norm_final_gain.reshape(1, D_MODEL), jnp.pad(attn_sinks, ((0, 0), (0, D_MODEL - ATT_HEADS)))]


def _unpack_small(p):
    return dict(norm_mix_gain=p[0:1], b_gates=jnp.concatenate([p[1:2], p[2:3]], axis=1), ret_gn_gain=p[3:4],
                norm_mlp_gain=p[4:5], norm_final_gain=p[5], attn_sinks=p[6:7, :ATT_HEADS])


WEIGHTS = ("norm_mix_gain", "w_in", "b_gates", "attn_sinks", "ret_gn_gain", "w_att_up", "w_ret_up", "w_out",
           "norm_mlp_gain", "w_ff1", "w_ff2", "norm_final_gain")
BIG = ("w_in", "w_att_up", "w_ret_up", "w_out", "w_ff1", "w_ff2")


def kernel(x, norm_mix_gain, w_in, b_gates, attn_sinks, ret_gn_gain, w_att_up, w_ret_up, w_out, norm_mlp_gain, w_ff1, w_ff2, norm_final_gain, loss_target, m_norm_mix_gain, m_w_in, m_b_gates, m_attn_sinks, m_ret_gn_gain, m_w_att_up, m_w_ret_up, m_w_out, m_norm_mlp_gain, m_w_ff1, m_w_ff2, m_norm_final_gain, v_norm_mix_gain, v_w_in, v_b_gates, v_attn_sinks, v_ret_gn_gain, v_w_att_up, v_w_ret_up, v_w_out, v_norm_mlp_gain, v_w_ff1, v_w_ff2, v_norm_final_gain):
    w = dict(norm_mix_gain=norm_mix_gain, w_in=w_in, b_gates=b_gates, attn_sinks=attn_sinks, ret_gn_gain=ret_gn_gain,
             w_att_up=w_att_up, w_ret_up=w_ret_up, w_out=w_out, norm_mlp_gain=norm_mlp_gain, w_ff1=w_ff1,
             w_ff2=w_ff2, norm_final_gain=norm_final_gain)
    m = dict(norm_mix_gain=m_norm_mix_gain, w_in=m_w_in, b_gates=m_b_gates, attn_sinks=m_attn_sinks,
             ret_gn_gain=m_ret_gn_gain, w_att_up=m_w_att_up, w_ret_up=m_w_ret_up, w_out=m_w_out,
             norm_mlp_gain=m_norm_mlp_gain, w_ff1=m_w_ff1, w_ff2=m_w_ff2, norm_final_gain=m_norm_final_gain)
    v = dict(norm_mix_gain=v_norm_mix_gain, w_in=v_w_in, b_gates=v_b_gates, attn_sinks=v_attn_sinks,
             ret_gn_gain=v_ret_gn_gain, w_att_up=v_w_att_up, w_ret_up=v_w_ret_up, w_out=v_w_out,
             norm_mlp_gain=v_norm_mlp_gain, w_ff1=v_w_ff1, w_ff2=v_w_ff2, norm_final_gain=v_norm_final_gain)

    shards = {k: (w[k][0].T if k in COLUMN_SHARDED else w[k][0]).astype(BF) for k in BIG}
    comm, unpack = _gather(shards, ("w_in",))
    (w_in_full,) = unpack(_comm_alone(comm, "allgather_w_in"))

    loss_p, dx, small, groups = _local_step(
        x[0], loss_target[0], norm_mix_gain, b_gates, attn_sinks[0], ret_gn_gain, norm_mlp_gain,
        norm_final_gain.reshape(1, D_MODEL), w_in_full, shards)

    by_name = {}
    for grp in groups:
        for k, stack, r1, r2 in zip(grp.names, grp.stacks, grp.r1s, grp.r2s):
            by_name[k] = (stack, r1, r2)
    lane = lax.broadcasted_iota(jnp.int32, (1, D_MODEL), 1)
    sink_row = jnp.where(lane < ATT_HEADS, jnp.pad(small["attn_sinks"], ((0, 0), (0, D_MODEL - LANE))),
                         jnp.where(lane == ATT_HEADS, jnp.pad(loss_p, ((0, 0), (0, D_MODEL - LANE)), mode="edge"),
                                   0.0))
    small_pack = _pack_rows([small["norm_mix_gain"], small["b_gates"][:, :D_MODEL], small["b_gates"][:, D_MODEL:],
                             small["ret_gn_gain"], small["norm_mlp_gain"], small["norm_final_gain"], sink_row])
    xi, yi, ci = _coords()
    (small_all,) = _comm_alone(_small_comm(small_pack), "small_exchange")
    idx_arr = jnp.stack([4 * xi + 2 * yi + ci, 2 * xi + yi]).astype(jnp.int32)
    in_pieces = ["w_ret_up", "w_out", "w_ff2", "w_in"]
    in_sum = ["w_att_up", "w_ff1"]
    order = in_pieces + in_sum
    summed_t = _shard_sum(*[[by_name[k][i] for k in in_sum] for i in range(3)], idx_arr)

    def shard(tree, k):
        return tree[k][0].T if k == "w_in" else tree[k][0]

    adam_out = _adam_big([[by_name[k][i] for k in in_pieces] for i in range(3)], [g.T for g in summed_t],
                         [shard(w, k) for k in order], [shard(m, k) for k in order], [shard(v, k) for k in order],
                         idx_arr)
    big_out = [adam_out[4 * order.index(k) + i].T if k == "w_in" else adam_out[4 * order.index(k) + i]
               for k in BIG for i in range(4)]
    sm_g, sm_d, sm_m, sm_v = _adam_small(small_all, _pack_rows(_small_rows(*[w[k] for k in WEIGHTS if k not in BIG])),
                                         _pack_rows(_small_rows(*[m[k] for k in WEIGHTS if k not in BIG])),
                                         _pack_rows(_small_rows(*[v[k] for k in WEIGHTS if k not in BIG])))

    loss = sm_g[6, ATT_HEADS]
    kinds = []
    for idx, packed in enumerate((sm_g, sm_d, sm_m, sm_v)):
        out = _unpack_small(packed)
        for a, k in enumerate(BIG):
            out[k] = big_out[4 * a + idx][None]
        kinds.append(out)
    return (loss, dx[None], *[kinds[0][k] for k in WEIGHTS], *[kinds[1][k] for k in WEIGHTS],
            *[kinds[2][k] for k in WEIGHTS], *[kinds[3][k] for k in WEIGHTS])
```

```python
import functools
import math

import jax
import jax.numpy as jnp
from jax import lax
from jax.experimental import pallas as pl
from jax.experimental.pallas import tpu as pltpu

F32 = jnp.float32
BF = jnp.bfloat16
MESH = pl.DeviceIdType.MESH

D_MODEL = 1024
ATT_HEADS = 8
ATT_HEAD_DIM = 64
ATT_BLOCK = 128
ROPE_DIM = 16
ROPE_THETA = 500000.0
RET_HEADS = 4
RET_KEY_DIM = 128
RET_VAL_DIM = 256
RET_CHUNK = 128
RET_ROT_BASE = 10000.0
D_FF = 4096
NORM_EPS = 1e-6
GN_EPS = 1e-6
NEG_INF = -1e30
ATT_SCALE = ATT_HEAD_DIM ** -0.5
RET_SCALE = RET_KEY_DIM ** -0.5

C_QA, C_KA, C_VA, C_QR, C_KR, C_VR, C_GR, C_GATES, C_END = 0, 512, 640, 768, 1280, 1792, 2816, 3840, 5888

ADAM_LR = 0.001
ADAM_B1 = 0.9
ADAM_B2 = 0.999
ADAM_EPS = 1e-08
ADAM_WD = 0.01
ADAM_STEP = 10

N_DEV = 8
LANE = 128
VMEM_LIMIT = 56 * 1024 * 1024
TOKEN_TILE = 256
TN_TOKEN_TILE = 1024

RET_LOG_GAMMA = tuple(math.log1p(-(2.0 ** (-5.0 - h))) for h in range(RET_HEADS))
RET_CHUNK_DECAY = tuple(math.exp(RET_CHUNK * lg) for lg in RET_LOG_GAMMA)

VMEM_FULL = pl.BlockSpec(memory_space=pltpu.VMEM)
SMEM_FULL = pl.BlockSpec(memory_space=pltpu.SMEM)
ANY = pl.BlockSpec(memory_space=pl.ANY)


def _params(n_axes):
    return pltpu.CompilerParams(dimension_semantics=("arbitrary",) * n_axes, vmem_limit_bytes=VMEM_LIMIT)


def _nn(a, b):
    return jnp.dot(a, b, preferred_element_type=F32)


def _nt(a, b):
    return lax.dot_general(a, b, (((1,), (1,)), ((), ())), preferred_element_type=F32)


def _tn(a, b):
    return lax.dot_general(a, b, (((0,), (0,)), ((), ())), preferred_element_type=F32)


def _sigmoid(v):
    return 1.0 / (1.0 + jnp.exp(-v))


def _rows(tile, width):
    return pl.BlockSpec((tile, width), lambda i: (i, 0))


def _cols(height, tile):
    return pl.BlockSpec((height, tile), lambda i: (0, i))


class _Comm:
    def __init__(self, inputs, out_shapes, scratch, start, finish):
        self.inputs, self.out_shapes, self.scratch, self.start, self.finish = inputs, out_shapes, scratch, start, finish


def _call(body, args, comm=None, *, name, grid, in_specs, out_specs, out_shape, scratch_shapes=()):
    params = _params(len(grid))
    if comm is None:
        return pl.pallas_call(body, name=name, grid=grid, in_specs=in_specs, out_specs=out_specs, out_shape=out_shape,
                              scratch_shapes=scratch_shapes, compiler_params=params)(*args), ()
    single = not isinstance(out_specs, (list, tuple))
    out_specs_l = [out_specs] if single else list(out_specs)
    out_shape_l = [out_shape] if single else list(out_shape)
    n_in, n_out, n_scr = len(in_specs), len(out_specs_l), len(scratch_shapes)
    n_cin, n_cout = len(comm.inputs), len(comm.out_shapes)

    def hosted(*refs):
        it = iter(refs)
        ins, cin, outs, cout, scr = ([next(it) for _ in range(k)] for k in (n_in, n_cin, n_out, n_cout, n_scr))
        sems = list(it)
        ids = [pl.program_id(k) for k in range(len(grid))]
        first = functools.reduce(jnp.logical_and, [i == 0 for i in ids])
        last = functools.reduce(jnp.logical_and, [i == g - 1 for i, g in zip(ids, grid)])

        @pl.when(first)
        def _():
            comm.start(cin, cout, sems)

        body(*ins, *outs, *scr)

        @pl.when(last)
        def _():
            comm.finish(cin, cout, sems)

    res = pl.pallas_call(
        hosted, name=name, grid=grid, in_specs=list(in_specs) + [ANY] * n_cin,
        out_specs=out_specs_l + [ANY] * n_cout, out_shape=out_shape_l + list(comm.out_shapes),
        scratch_shapes=list(scratch_shapes) + list(comm.scratch), compiler_params=params)(*args, *comm.inputs)
    return (res[0] if single else res[:n_out]), res[n_out:]


def _comm_alone(comm, name):
    n_cin, n_cout = len(comm.inputs), len(comm.out_shapes)

    def body(*refs):
        cin, cout, sems = refs[:n_cin], refs[n_cin:n_cin + n_cout], refs[n_cin + n_cout:]
        comm.start(cin, cout, sems)
        comm.finish(cin, cout, sems)

    return pl.pallas_call(body, name=name, in_specs=[ANY] * n_cin, out_specs=[ANY] * n_cout,
                          out_shape=list(comm.out_shapes), scratch_shapes=list(comm.scratch))(*comm.inputs)


def _slabs(v, fn):
    return jnp.concatenate([fn(v[:, LANE * j:LANE * (j + 1)]) for j in range(v.shape[1] // LANE)], axis=1)


def _rope_att(v, ca, sa, sb):
    return _slabs(v, lambda t: t * ca + pltpu.roll(t, LANE - 8, 1) * sa + pltpu.roll(t, 8, 1) * sb)


def _rope_att_t(v, ca, sa, sb):
    return _slabs(v, lambda t: t * ca + pltpu.roll(t * sa, 8, 1) + pltpu.roll(t * sb, LANE - 8, 1))


def _rope_att_rows(v, ct, st, sign):
    parts = []
    for h in range(v.shape[0] // ATT_HEAD_DIM):
        r0 = ATT_HEAD_DIM * h
        x1, x2 = v[r0:r0 + 8], v[r0 + 8:r0 + 16]
        parts += [x1 * ct - sign * (x2 * st), x2 * ct + sign * (x1 * st), v[r0 + 16:r0 + ATT_HEAD_DIM]]
    return jnp.concatenate(parts, axis=0)


def _rope_ret(v, cr, sr):
    return _slabs(v, lambda t: t * cr + pltpu.roll(t, 64, 1) * sr)


def _rope_ret_t(v, cr, sr):
    return _slabs(v, lambda t: t * cr + pltpu.roll(t * sr, 64, 1))


def _rope_tables(seq):
    pos = jnp.arange(seq, dtype=F32)

    def cs(dim, theta):
        inv = theta ** (-jnp.arange(0, dim, 2, dtype=F32) / dim)
        ang = pos[:, None] * inv[None, :]
        return jnp.cos(ang), jnp.sin(ang)

    ca, sa_ = cs(ROPE_DIM, ROPE_THETA)
    zeros8 = jnp.zeros_like(ca)
    rest = ATT_HEAD_DIM - ROPE_DIM
    c64 = jnp.concatenate([ca, ca, jnp.ones((seq, rest), F32)], axis=1)
    sa64 = jnp.concatenate([-sa_, zeros8, jnp.zeros((seq, rest), F32)], axis=1)
    sb64 = jnp.concatenate([zeros8, sa_, jnp.zeros((seq, rest), F32)], axis=1)
    cr, sr_ = cs(RET_KEY_DIM, RET_ROT_BASE)
    return (jnp.tile(c64, (1, 2)), jnp.tile(sa64, (1, 2)), jnp.tile(sb64, (1, 2)),
            jnp.concatenate([cr, cr], axis=1), jnp.concatenate([-sr_, sr_], axis=1), ca.T, sa_.T)


def _ret_tables():
    c = RET_CHUNK
    lg = jnp.asarray(RET_LOG_GAMMA, F32)
    idx = jnp.arange(c, dtype=F32)
    diff = idx[:, None] - idx[None, :]
    intra = jnp.where(diff >= 0, jnp.exp(jnp.maximum(diff, 0.0) * lg[:, None, None]), 0.0)
    qd = jnp.exp((idx + 1.0)[None, :] * lg[:, None])[..., None]
    kd = jnp.exp((c - 1.0 - idx)[None, :] * lg[:, None])[..., None]
    return intra, jnp.broadcast_to(qd, (RET_HEADS, c, RET_KEY_DIM)), jnp.broadcast_to(kd, (RET_HEADS, c, RET_KEY_DIM))


def _inproj_fwd(x, g1, w_in, tabs, comm=None):
    seq = x.shape[0]
    tm = min(TOKEN_TILE, seq)

    def body(x_ref, g_ref, w_ref, ca_ref, sa_ref, sb_ref, cr_ref, sr_ref, ct_ref, st_ref,
             xn_ref, qt_ref, kv_ref, kvt_ref, qkr_ref, vr_ref, gr_ref, gates_ref):
        xf = x_ref[...]
        r = lax.rsqrt(jnp.mean(xf * xf, axis=-1, keepdims=True) + NORM_EPS)
        xn = (xf * r * g_ref[...]).astype(BF)
        xn_ref[...] = xn
        ct, st = ct_ref[...], st_ref[...]
        qt_ref[...] = _rope_att_rows(_nt(w_ref[C_QA:C_KA, :], xn), ct, st, 1.0).astype(BF)
        kvt = _nt(w_ref[C_KA:C_QR, :], xn)
        kvt_ref[:LANE, :] = _rope_att_rows(kvt[:LANE], ct, st, 1.0).astype(BF)
        kvt_ref[LANE:, :] = kvt[LANE:].astype(BF)
        kvn = _nt(xn, w_ref[C_KA:C_QR, :])
        kv_ref[:, :LANE] = _rope_att(kvn[:, :LANE], ca_ref[...], sa_ref[...], sb_ref[...]).astype(BF)
        kv_ref[:, LANE:] = kvn[:, LANE:].astype(BF)
        cr, sr = cr_ref[...], sr_ref[...]
        qr = _rope_ret(_nt(xn, w_ref[C_QR:C_KR, :]), cr, sr)
        qkr_ref[:, :512] = qr.astype(BF)
        kr = _rope_ret(_nt(xn, w_ref[C_KR:C_VR, :]), cr, sr) * RET_SCALE
        qkr_ref[:, 512:] = kr.astype(BF)
        vr_ref[...] = _nt(xn, w_ref[C_VR:C_GR, :]).astype(BF)
        gr_ref[...] = _nt(xn, w_ref[C_GR:C_GATES, :]).astype(BF)
        gates_ref[...] = _nt(xn, w_ref[C_GATES:C_END, :]).astype(BF)

    tab = _rows(tm, LANE)
    tab_t = _cols(8, tm)
    return _call(
        body, (x, g1, w_in, *tabs), comm, name="inproj_fwd", grid=(seq // tm,),
        in_specs=[_rows(tm, D_MODEL), VMEM_FULL, VMEM_FULL, tab, tab, tab, tab, tab, tab_t, tab_t],
        out_specs=[_rows(tm, D_MODEL), _cols(512, tm), _rows(tm, 256), _cols(256, tm), _rows(tm, 1024),
                   _rows(tm, 1024), _rows(tm, 1024), _rows(tm, 2048)],
        out_shape=[jax.ShapeDtypeStruct(s, BF) for s in ((seq, D_MODEL), (512, seq), (seq, 256), (256, seq),
                                                         (seq, 1024), (seq, 1024), (seq, 1024), (seq, 2048))])


ATT_Q_PER_KV = ATT_HEADS // 2


def _att_group(ref, kv):
    d = ATT_HEAD_DIM
    g = jnp.concatenate([ref[d * (ATT_Q_PER_KV * kv + j):d * (ATT_Q_PER_KV * kv + j + 1), :]
                         for j in range(ATT_Q_PER_KV)], axis=1)
    z = jnp.zeros_like(g)
    return jnp.concatenate([g, z] if kv == 0 else [z, g], axis=0)


def _att_probs(kwin, qpad, sink_ref, kv, i):
    c = ATT_BLOCK
    kj = lax.broadcasted_iota(jnp.int32, (2 * c, c), 0)
    qi = lax.broadcasted_iota(jnp.int32, (2 * c, c), 1)
    allowed = (kj > qi) & (kj <= qi + c) & ((kj >= c) | (i > 0))
    allowed = jnp.concatenate([allowed] * ATT_Q_PER_KV, axis=1)
    s = jnp.where(allowed, _nn(kwin, qpad) * ATT_SCALE, NEG_INF)
    sk = jnp.concatenate([jnp.full((1, c), sink_ref[ATT_Q_PER_KV * kv + j], F32) for j in range(ATT_Q_PER_KV)], axis=1)
    m = jnp.maximum(jnp.max(s, axis=0, keepdims=True), sk)
    pe = jnp.exp(s - m)
    psink = jnp.exp(sk - m)
    inv = 1.0 / (jnp.sum(pe, axis=0, keepdims=True) + psink)
    return pe * inv, psink * inv


def _attn_fwd(qt, kv, kvt, sinks, comm=None):
    seq = kv.shape[0]
    c = ATT_BLOCK
    d = ATT_HEAD_DIM
    nb = seq // c

    def body(sink_ref, qt_ref, kvc_ref, kvp_ref, kvtc_ref, kvtp_ref, ot_ref):
        i = pl.program_id(0)
        kwin = jnp.concatenate([kvp_ref[:, :LANE], kvc_ref[:, :LANE]], axis=0)
        vtwin = jnp.concatenate([kvtp_ref[LANE:, :], kvtc_ref[LANE:, :]], axis=1)
        for g in (0, 1):
            prob, _ = _att_probs(kwin, _att_group(qt_ref, g), sink_ref, g, i)
            out = _nn(vtwin[d * g:d * (g + 1), :], prob.astype(BF))
            for j in range(ATT_Q_PER_KV):
                h = ATT_Q_PER_KV * g + j
                ot_ref[d * h:d * (h + 1), :] = out[:, c * j:c * (j + 1)].astype(BF)

    prev = lambda i: jnp.maximum(i - 1, 0)
    return _call(
        body, (sinks, qt, kv, kv, kvt, kvt), comm, name="attn_fwd", grid=(nb,),
        in_specs=[SMEM_FULL, _cols(512, c), _rows(c, 256), pl.BlockSpec((c, 256), lambda i: (prev(i), 0)),
                  _cols(256, c), pl.BlockSpec((256, c), lambda i: (0, prev(i)))],
        out_specs=_cols(512, c),
        out_shape=jax.ShapeDtypeStruct((512, seq), BF))


def _ret_fwd(qkr, vr, gr, gain, rtabs, comm=None):
    seq = qkr.shape[0]
    c = RET_CHUNK
    nc = seq // c
    dk, dv = RET_KEY_DIM, RET_VAL_DIM

    def body(qk_ref, v_ref, g_ref, gain_ref, intra_ref, qd_ref, kd_ref, ret_ref, yrp_ref, st_ref, state):
        @pl.when(pl.program_id(0) == 0)
        def _():
            state[...] = jnp.zeros_like(state)

        for h in range(RET_HEADS):
            qh = qk_ref[:, dk * h:dk * (h + 1)]
            kh = qk_ref[:, 512 + dk * h:512 + dk * (h + 1)]
            vh = v_ref[:, dv * h:dv * (h + 1)]
            sh = state[h]
            shb = sh.astype(BF)
            st_ref[0, h] = shb
            att = _nt(qh, kh) * intra_ref[h]
            inner = _nn(att.astype(BF), vh)
            cross = _nn((qh.astype(F32) * qd_ref[h]).astype(BF), shb)
            out = inner + cross
            state[h] = sh * RET_CHUNK_DECAY[h] + _tn((kh.astype(F32) * kd_ref[h]).astype(BF), vh)
            ret_ref[:, dv * h:dv * (h + 1)] = out
            mu = jnp.mean(out, axis=-1, keepdims=True)
            d = out - mu
            var = jnp.mean(d * d, axis=-1, keepdims=True)
            y = d * lax.rsqrt(var + GN_EPS) * gain_ref[:, dv * h:dv * (h + 1)]
            g = g_ref[:, dv * h:dv * (h + 1)].astype(F32)
            yrp_ref[:, dv * h:dv * (h + 1)] = (g * _sigmoid(g) * y).astype(BF)

    return _call(
        body, (qkr, vr, gr, gain, *rtabs), comm, name="ret_fwd", grid=(nc,),
        in_specs=[_rows(c, 1024), _rows(c, 1024), _rows(c, 1024), VMEM_FULL, VMEM_FULL, VMEM_FULL, VMEM_FULL],
        out_specs=[_rows(c, 1024), _rows(c, 1024), pl.BlockSpec((1, RET_HEADS, dk, dv), lambda i: (i, 0, 0, 0))],
        out_shape=[jax.ShapeDtypeStruct((seq, 1024), F32), jax.ShapeDtypeStruct((seq, 1024), BF),
                   jax.ShapeDtypeStruct((nc, RET_HEADS, dk, dv), BF)],
        scratch_shapes=[pltpu.VMEM((RET_HEADS, dk, dv), F32)])


def _mix_fwd(oat, yrp, gates, bg, x, w_att, w_ret, w_out):
    seq = x.shape[0]
    tm = min(TOKEN_TILE, seq)

    def body(oat_ref, yrp_ref, gates_ref, bg_ref, x_ref, wa_ref, wr_ref, wo_ref, ya_ref, yr_ref, mg_ref, h1_ref):
        ya = _tn(oat_ref[...], wa_ref[...])
        yr = _nn(yrp_ref[...], wr_ref[...])
        gt = _sigmoid(gates_ref[...].astype(F32) + bg_ref[...])
        merged = (gt[:, :D_MODEL] * ya + gt[:, D_MODEL:] * yr).astype(BF)
        ya_ref[...] = ya.astype(BF)
        yr_ref[...] = yr.astype(BF)
        mg_ref[...] = merged
        h1_ref[...] = x_ref[...] + _nn(merged, wo_ref[...])

    return pl.pallas_call(
        body, name="mix_fwd", grid=(seq // tm,),
        in_specs=[_cols(512, tm), _rows(tm, 1024), _rows(tm, 2048), VMEM_FULL, _rows(tm, D_MODEL),
                  VMEM_FULL, VMEM_FULL, VMEM_FULL],
        out_specs=[_rows(tm, D_MODEL)] * 4,
        out_shape=[jax.ShapeDtypeStruct((seq, D_MODEL), BF)] * 3 + [jax.ShapeDtypeStruct((seq, D_MODEL), F32)],
        compiler_params=_params(1),
    )(oat, yrp, gates, bg, x, w_att, w_ret, w_out)


def _mlp_fwd_loss(h1, g2, g3, target, w_ff1, w_ff2):
    seq = h1.shape[0]
    tm = min(TOKEN_TILE, seq)

    def body(h1_ref, g2_ref, g3_ref, t_ref, w1_ref, w2_ref, xn2_ref, u_ref, dh2_ref, loss_ref, dg3_ref):
        @pl.when(pl.program_id(0) == 0)
        def _():
            loss_ref[...] = jnp.zeros_like(loss_ref)
            dg3_ref[...] = jnp.zeros_like(dg3_ref)

        h1v = h1_ref[...]
        r2 = lax.rsqrt(jnp.mean(h1v * h1v, axis=-1, keepdims=True) + NORM_EPS)
        xn2 = (h1v * r2 * g2_ref[...]).astype(BF)
        xn2_ref[...] = xn2
        u = _nt(xn2, w1_ref[...])
        u_ref[...] = u.astype(BF)
        hdn = jnp.square(jnp.maximum(u, 0.0)).astype(BF)
        h2 = h1v + _nn(hdn, w2_ref[...])
        r3 = lax.rsqrt(jnp.mean(h2 * h2, axis=-1, keepdims=True) + NORM_EPS)
        hn = h2 * r3
        err = hn * g3_ref[...] - t_ref[...]
        loss_ref[...] += jnp.sum(err * err) * (0.5 / D_MODEL)
        dy = err * (1.0 / D_MODEL)
        dg3_ref[...] += jnp.sum(dy * hn, axis=0, keepdims=True)
        z = dy * g3_ref[...]
        dh2_ref[...] = r3 * (z - hn * jnp.mean(z * hn, axis=-1, keepdims=True))

    return pl.pallas_call(
        body, name="mlp_fwd_loss", grid=(seq // tm,),
        in_specs=[_rows(tm, D_MODEL), VMEM_FULL, VMEM_FULL, _rows(tm, D_MODEL), VMEM_FULL, VMEM_FULL],
        out_specs=[_rows(tm, D_MODEL), _rows(tm, D_FF), _rows(tm, D_MODEL),
                   pl.BlockSpec((1, LANE), lambda i: (0, 0)), pl.BlockSpec((1, D_MODEL), lambda i: (0, 0))],
        out_shape=[jax.ShapeDtypeStruct((seq, D_MODEL), BF), jax.ShapeDtypeStruct((seq, D_FF), BF),
                   jax.ShapeDtypeStruct((seq, D_MODEL), F32), jax.ShapeDtypeStruct((1, LANE), F32),
                   jax.ShapeDtypeStruct((1, D_MODEL), F32)],
        compiler_params=_params(1),
    )(h1, g2, g3, target, w_ff1, w_ff2)


def _rms_bwd(dxn, xin, gain):
    r = lax.rsqrt(jnp.mean(xin * xin, axis=-1, keepdims=True) + NORM_EPS)
    xhat = xin * r
    z = dxn * gain
    dxin = r * (z - xhat * jnp.mean(z * xhat, axis=-1, keepdims=True))
    return dxin, jnp.sum(dxn * xhat, axis=0, keepdims=True)


def _mlp_bwd(dh2, u, h1, g2, w_ff1, w_ff2):
    seq = h1.shape[0]
    tm = min(TOKEN_TILE, seq)

    def body(dh2_ref, u_ref, h1_ref, g2_ref, w1_ref, w2_ref, du_ref, dh1_ref, dg2_ref):
        @pl.when(pl.program_id(0) == 0)
        def _():
            dg2_ref[...] = jnp.zeros_like(dg2_ref)

        dh2v = dh2_ref[...]
        dhdn = _nt(dh2v.astype(BF), w2_ref[...])
        du = (dhdn * (2.0 * jnp.maximum(u_ref[...].astype(F32), 0.0))).astype(BF)
        du_ref[...] = du
        dxn2 = _nn(du, w1_ref[...])
        dnorm, dg = _rms_bwd(dxn2, h1_ref[...], g2_ref[...])
        dh1_ref[...] = dh2v + dnorm
        dg2_ref[...] += dg

    return pl.pallas_call(
        body, name="mlp_bwd", grid=(seq // tm,),
        in_specs=[_rows(tm, D_MODEL), _rows(tm, D_FF), _rows(tm, D_MODEL), VMEM_FULL, VMEM_FULL, VMEM_FULL],
        out_specs=[_rows(tm, D_FF), _rows(tm, D_MODEL), pl.BlockSpec((1, D_MODEL), lambda i: (0, 0))],
        out_shape=[jax.ShapeDtypeStruct((seq, D_FF), BF), jax.ShapeDtypeStruct((seq, D_MODEL), F32),
                   jax.ShapeDtypeStruct((1, D_MODEL), F32)],
        compiler_params=_params(1),
    )(dh2, u, h1, g2, w_ff1, w_ff2)


def _mm_tn(a, b, name, relu_sq=False, a_is_t=False, comm=None):
    kdim, seq = a.shape if a_is_t else a.shape[::-1]
    ndim = b.shape[1]
    ts = min(TN_TOKEN_TILE, seq)
    tk = min(kdim, 1024)
    tn = min(ndim, 1024)
    n_steps = seq // ts

    def body(a_ref, b_ref, o_ref):
        @pl.when(pl.program_id(2) == 0)
        def _():
            o_ref[...] = jnp.zeros_like(o_ref)

        av = a_ref[...]
        if relu_sq:
            av = jnp.square(jnp.maximum(av.astype(F32), 0.0))
        mm = _nn if a_is_t else _tn
        o_ref[...] += mm(av.astype(BF), b_ref[...].astype(BF))

    a_spec = (pl.BlockSpec((tk, ts), lambda k, n, s: (k, s)) if a_is_t
              else pl.BlockSpec((ts, tk), lambda k, n, s: (s, k)))
    res = _call(
        body, (a, b), comm, name=name, grid=(kdim // tk, ndim // tn, n_steps),
        in_specs=[a_spec, pl.BlockSpec((ts, tn), lambda k, n, s: (s, n))],
        out_specs=pl.BlockSpec((tk, tn), lambda k, n, s: (k, n)),
        out_shape=jax.ShapeDtypeStruct((kdim, ndim), F32))
    return res[0] if comm is None else res


def _dw_in_t(pieces, is_t, xn1, comm=None):
    seq = xn1.shape[0]
    ts = min(2 * TN_TOKEN_TILE, seq)
    tr = 256
    n_blk = [(p.shape[0] if t else p.shape[1]) // tr for p, t in zip(pieces, is_t)]
    offs = [sum(n_blk[:i]) for i in range(len(pieces) + 1)]
    n_p = len(pieces)

    def body(*refs):
        piece_refs, xn_ref, o_ref = refs[:n_p], refs[n_p], refs[n_p + 1]
        r, s = pl.program_id(0), pl.program_id(1)

        @pl.when(s == 0)
        def _():
            o_ref[...] = jnp.zeros_like(o_ref)

        for p in range(n_p):
            @pl.when((r >= offs[p]) & (r < offs[p + 1]))
            def _(p=p):
                mm = _nn if is_t[p] else _tn
                o_ref[...] += mm(piece_refs[p][...], xn_ref[pl.ds(pl.multiple_of(s * ts, ts), ts), :])

    def piece_spec(p):
        def index(r, s):
            inside = (r >= offs[p]) & (r < offs[p + 1])
            tok, blk = jnp.where(inside, s, 0), jnp.clip(r - offs[p], 0, n_blk[p] - 1)
            return (blk, tok) if is_t[p] else (tok, blk)
        return pl.BlockSpec((tr, ts) if is_t[p] else (ts, tr), index)

    return _call(
        body, (*pieces, xn1), comm, name="dw_in", grid=(offs[-1], seq // ts),
        in_specs=[piece_spec(p) for p in range(n_p)] + [VMEM_FULL],
        out_specs=pl.BlockSpec((tr, D_MODEL), lambda r, s: (r, 0)),
        out_shape=jax.ShapeDtypeStruct((offs[-1] * tr, D_MODEL), F32))


def _mix_bwd(dh1, ya, yr, gates, bg, w_att, w_ret, w_out, comm=None):
    seq = dh1.shape[0]
    tm = min(TOKEN_TILE, seq)

    def body(dh1_ref, ya_ref, yr_ref, gates_ref, bg_ref, wa_ref, wr_ref, wo_ref,
             dya_ref, dyr_ref, dgates_ref, doa_ref, dyrp_ref, db_ref):
        @pl.when(pl.program_id(0) == 0)
        def _():
            db_ref[...] = jnp.zeros_like(db_ref)

        dm = _nt(dh1_ref[...].astype(BF), wo_ref[...])
        gt = _sigmoid(gates_ref[...].astype(F32) + bg_ref[...])
        ga, gr = gt[:, :D_MODEL], gt[:, D_MODEL:]
        dya = (dm * ga).astype(BF)
        dyr = (dm * gr).astype(BF)
        dya_ref[...] = dya
        dyr_ref[...] = dyr
        dga = dm * ya_ref[...].astype(F32) * ga * (1.0 - ga)
        dgr = dm * yr_ref[...].astype(F32) * gr * (1.0 - gr)
        dgates_ref[:, :D_MODEL] = dga.astype(BF)
        dgates_ref[:, D_MODEL:] = dgr.astype(BF)
        db_ref[:, :D_MODEL] += jnp.sum(dga, axis=0, keepdims=True)
        db_ref[:, D_MODEL:] += jnp.sum(dgr, axis=0, keepdims=True)
        doa_ref[...] = _nt(wa_ref[...], dya).astype(BF)
        dyrp_ref[...] = _nt(dyr, wr_ref[...]).astype(BF)

    return _call(
        body, (dh1, ya, yr, gates, bg, w_att, w_ret, w_out), comm, name="mix_bwd", grid=(seq // tm,),
        in_specs=[_rows(tm, D_MODEL), _rows(tm, D_MODEL), _rows(tm, D_MODEL), _rows(tm, 2048), VMEM_FULL,
                  VMEM_FULL, VMEM_FULL, VMEM_FULL],
        out_specs=[_rows(tm, D_MODEL), _rows(tm, D_MODEL), _rows(tm, 2048), _cols(512, tm), _rows(tm, 1024),
                   pl.BlockSpec((1, 2048), lambda i: (0, 0))],
        out_shape=[jax.ShapeDtypeStruct((seq, D_MODEL), BF), jax.ShapeDtypeStruct((seq, D_MODEL), BF),
                   jax.ShapeDtypeStruct((seq, 2048), BF), jax.ShapeDtypeStruct((512, seq), BF),
                   jax.ShapeDtypeStruct((seq, 1024), BF), jax.ShapeDtypeStruct((1, 2048), F32)])


def _ret_bwd(qkr, vr, gr, ret, dyrp, states, gain, rtabs, cr, sr, comm=None):
    seq = qkr.shape[0]
    c = RET_CHUNK
    nc = seq // c
    dk, dv = RET_KEY_DIM, RET_VAL_DIM

    def body(qk_ref, v_ref, g_ref, ret_ref, dyp_ref, st_ref, gain_ref, intra_ref, qd_ref, kd_ref, cr_ref, sr_ref,
             dall_ref, dgain_ref, dstate):
        @pl.when(pl.program_id(0) == 0)
        def _():
            dstate[...] = jnp.zeros_like(dstate)
            dgain_ref[...] = jnp.zeros_like(dgain_ref)

        crv, srv = cr_ref[...], sr_ref[...]
        for h in range(RET_HEADS):
            vs = slice(dv * h, dv * (h + 1))
            qh = qk_ref[:, dk * h:dk * (h + 1)]
            kh = qk_ref[:, 512 + dk * h:512 + dk * (h + 1)]
            vh = v_ref[:, vs]
            out = ret_ref[:, vs]
            g = g_ref[:, vs].astype(F32)
            dyp = dyp_ref[:, vs].astype(F32)
            gain_h = gain_ref[:, vs]
            mu = jnp.mean(out, axis=-1, keepdims=True)
            d = out - mu
            rstd = lax.rsqrt(jnp.mean(d * d, axis=-1, keepdims=True) + GN_EPS)
            yn = d * rstd
            sg = _sigmoid(g)
            dg = dyp * (yn * gain_h) * (sg * (1.0 + g * (1.0 - sg)))
            dy = dyp * (g * sg)
            dgain_ref[:, vs] += jnp.sum(dy * yn, axis=0, keepdims=True)
            dyn = dy * gain_h
            dout = rstd * (dyn - jnp.mean(dyn, axis=-1, keepdims=True)
                           - yn * jnp.mean(dyn * yn, axis=-1, keepdims=True))
            doutb = dout.astype(BF)
            sc = st_ref[0, h]
            dsp = dstate[h]
            dspb = dsp.astype(BF)
            intra, qdv, kdv = intra_ref[h], qd_ref[h], kd_ref[h]
            att = _nt(qh, kh) * intra
            dab = (_nt(doutb, vh) * intra).astype(BF)
            qdec = (qh.astype(F32) * qdv).astype(BF)
            kdec = (kh.astype(F32) * kdv).astype(BF)
            dq = _nn(dab, kh) + _nt(doutb, sc) * qdv
            dkk = _tn(dab, qh) + _nt(vh, dspb) * kdv
            dvv = _tn(att.astype(BF), doutb) + _nn(kdec, dspb)
            dstate[h] = dsp * RET_CHUNK_DECAY[h] + _tn(qdec, doutb)
            dall_ref[:, dk * h:dk * (h + 1)] = _rope_ret_t(dq, crv, srv).astype(BF)
            dall_ref[:, 512 + dk * h:512 + dk * (h + 1)] = (_rope_ret_t(dkk, crv, srv) * RET_SCALE).astype(BF)
            dall_ref[:, 1024 + dv * h:1024 + dv * (h + 1)] = dvv.astype(BF)
            dall_ref[:, 2048 + dv * h:2048 + dv * (h + 1)] = dg.astype(BF)

    def rev(width):
        return pl.BlockSpec((c, width), lambda i: (nc - 1 - i, 0))

    return _call(
        body, (qkr, vr, gr, ret, dyrp, states, gain, *rtabs, cr, sr), comm, name="ret_bwd", grid=(nc,),
        in_specs=[rev(1024), rev(1024), rev(1024), rev(1024), rev(1024),
                  pl.BlockSpec((1, RET_HEADS, dk, dv), lambda i: (nc - 1 - i, 0, 0, 0)),
                  VMEM_FULL, VMEM_FULL, VMEM_FULL, VMEM_FULL, rev(LANE), rev(LANE)],
        out_specs=[rev(3072), pl.BlockSpec((1, 1024), lambda i: (0, 0))],
        out_shape=[jax.ShapeDtypeStruct((seq, 3072), BF), jax.ShapeDtypeStruct((1, 1024), F32)],
        scratch_shapes=[pltpu.VMEM((RET_HEADS, dk, dv), F32)])


def _attn_bwd(qt, kv, kvt, dot, sinks, ca, sa, sb, ct, st, comm=None):
    seq = kv.shape[0]
    c = ATT_BLOCK
    d = ATT_HEAD_DIM
    nb = seq // c

    def body(sink_ref, qt_ref, kvc_ref, kvp_ref, kvtc_ref, kvtp_ref, dot_ref, cap_ref, sap_ref, sbp_ref,
             ctc_ref, stc_ref, dqt_ref, dkv_ref, dsink_ref, carry):
        i = pl.program_id(0)

        @pl.when(i == 0)
        def _():
            carry[...] = jnp.zeros_like(carry)
            dsink_ref[...] = jnp.zeros_like(dsink_ref)

        def flush(total):
            dk_pre = _rope_att_t(total[:, :LANE], cap_ref[...], sap_ref[...], sbp_ref[...])
            dkv_ref[:, :LANE] = dk_pre.astype(BF)
            dkv_ref[:, LANE:] = total[:, LANE:].astype(BF)

        @pl.when(i < nb)
        def _():
            kwin = jnp.concatenate([kvp_ref[:, :LANE], kvc_ref[:, :LANE]], axis=0)
            vwin = jnp.concatenate([kvp_ref[:, LANE:], kvc_ref[:, LANE:]], axis=0)
            ktwin = jnp.concatenate([kvtp_ref[:LANE, :], kvtc_ref[:LANE, :]], axis=1)
            ctc, stc = ctc_ref[...], stc_ref[...]
            lane1 = lax.broadcasted_iota(jnp.int32, (1, LANE), 1)
            dkw = jnp.zeros((2 * c, LANE), F32)
            dvw = jnp.zeros((2 * c, LANE), F32)
            dsink = jnp.zeros((1, LANE), F32)
            for g in (0, 1):
                qpad, dopad = _att_group(qt_ref, g), _att_group(dot_ref, g)
                prob, psink = _att_probs(kwin, qpad, sink_ref, g, i)
                dprob = _nn(vwin, dopad)
                drow = jnp.sum(dprob * prob, axis=0, keepdims=True)
                ds = (prob * (dprob - drow) * ATT_SCALE).astype(BF)
                dqg = _nn(ktwin[d * g:d * (g + 1), :], ds)
                dkw = dkw + _nt(ds, qpad)
                dvw = dvw + _nt(prob.astype(BF), dopad)
                dsink_lanes = psink * drow
                for j in range(ATT_Q_PER_KV):
                    h = ATT_Q_PER_KV * g + j
                    dqt_ref[d * h:d * (h + 1), :] = _rope_att_rows(dqg[:, c * j:c * (j + 1)], ctc, stc, -1.0).astype(BF)
                    dsink = dsink + jnp.where(lane1 == h, -jnp.sum(dsink_lanes[:, c * j:c * (j + 1)]), 0.0)
            dsink_ref[...] += dsink
            flush(carry[...] + jnp.concatenate([dkw[:c], dvw[:c]], axis=1))
            carry[...] = jnp.concatenate([dkw[c:], dvw[c:]], axis=1)

        @pl.when(i == nb)
        def _():
            flush(carry[...])

    cur = lambda i: jnp.minimum(i, nb - 1)
    prev = lambda i: jnp.clip(i - 1, 0, nb - 1)
    rows_cur = lambda w: pl.BlockSpec((c, w), lambda i: (cur(i), 0))
    rows_prev = lambda w: pl.BlockSpec((c, w), lambda i: (prev(i), 0))
    cols_cur = lambda h: pl.BlockSpec((h, c), lambda i: (0, cur(i)))
    cols_prev = lambda h: pl.BlockSpec((h, c), lambda i: (0, prev(i)))
    return _call(
        body, (sinks, qt, kv, kv, kvt, kvt, dot, ca, sa, sb, ct, st), comm, name="attn_bwd", grid=(nb + 1,),
        in_specs=[SMEM_FULL, cols_cur(512), rows_cur(256), rows_prev(256), cols_cur(256), cols_prev(256),
                  cols_cur(512), rows_prev(LANE), rows_prev(LANE), rows_prev(LANE), cols_cur(8), cols_cur(8)],
        out_specs=[cols_cur(512), rows_prev(256), pl.BlockSpec((1, LANE), lambda i: (0, 0))],
        out_shape=[jax.ShapeDtypeStruct((512, seq), BF), jax.ShapeDtypeStruct((seq, 256), BF),
                   jax.ShapeDtypeStruct((1, LANE), F32)],
        scratch_shapes=[pltpu.VMEM((c, 256), F32)])


def _inproj_bwd(dqa, dkva, dret, dgates, dh1, x, g1, w_in, comm=None):
    seq = x.shape[0]
    tm = min(TOKEN_TILE, seq)

    def body(dqa_ref, dkva_ref, dret_ref, dgates_ref, dh1_ref, x_ref, g_ref, w_ref, dx_ref, dg1_ref):
        @pl.when(pl.program_id(0) == 0)
        def _():
            dg1_ref[...] = jnp.zeros_like(dg1_ref)

        dxn = (_tn(dqa_ref[...], w_ref[C_QA:C_KA, :]) + _nn(dkva_ref[...], w_ref[C_KA:C_QR, :])
               + _nn(dret_ref[...], w_ref[C_QR:C_GATES, :]) + _nn(dgates_ref[...], w_ref[C_GATES:C_END, :]))
        dnorm, dg = _rms_bwd(dxn, x_ref[...], g_ref[...])
        dx_ref[...] = dh1_ref[...] + dnorm
        dg1_ref[...] += dg

    return _call(
        body, (dqa, dkva, dret, dgates, dh1, x, g1, w_in), comm, name="inproj_bwd", grid=(seq // tm,),
        in_specs=[_cols(512, tm), _rows(tm, 256), _rows(tm, 3072), _rows(tm, 2048), _rows(tm, D_MODEL),
                  _rows(tm, D_MODEL), VMEM_FULL, VMEM_FULL],
        out_specs=[_rows(tm, D_MODEL), pl.BlockSpec((1, D_MODEL), lambda i: (0, 0))],
        out_shape=[jax.ShapeDtypeStruct((seq, D_MODEL), F32), jax.ShapeDtypeStruct((1, D_MODEL), F32)])


def _local_step(x, target, g1, bg, sinks, gain, g2, g3, w_in, shards):
    seq = x.shape[0]
    ca, sa, sb, cr, sr, ct, st = _rope_tables(seq)
    rtabs = _ret_tables()
    scatter = _Scatter

    comm, unpack = _gather(shards, ("w_att_up", "w_ret_up", "w_out"))
    (xn1, qt, kv, kvt, qkr, vr, gr, gates), got = _inproj_fwd(x, g1, w_in, (ca, sa, sb, cr, sr, ct, st), comm)
    w_att, w_ret, w_out = unpack(got)
    comm, unpack = _gather(shards, ("w_ff1",))
    oa, got = _attn_fwd(qt, kv, kvt, sinks, comm)
    (w_ff1,) = unpack(got)
    comm, unpack = _gather(shards, ("w_ff2",))
    (ret, yrp, states), got = _ret_fwd(qkr, vr, gr, gain, rtabs, comm)
    (w_ff2,) = unpack(got)
    ya, yr, merged, h1 = _mix_fwd(oa, yrp, gates, bg, x, w_att, w_ret, w_out)
    xn2, u, dh2, loss, dg3 = _mlp_fwd_loss(h1, g2, g3, target, w_ff1, w_ff2)

    du, dh1, dg2 = _mlp_bwd(dh2, u, h1, g2, w_ff1, w_ff2)
    ff = scatter(dict(w_ff1=_mm_tn(du, xn2, "dw_ff1"), w_ff2=_mm_tn(u, dh2, "dw_ff2", relu_sq=True)), "ff")
    (dya, dyr, dgates, doa, dyrp, db), r1 = _mix_bwd(dh1, ya, yr, gates, bg, w_att, w_ret, w_out, ff.pair_comm())
    ff.pair_done(r1)
    mid = scatter(dict(w_att_up=_mm_tn(oa, dya, "dw_att_up", a_is_t=True), w_ret_up=_mm_tn(yrp, dyr, "dw_ret_up"),
                       w_out=_mm_tn(merged, dh1, "dw_out")), "mid")
    (dqa, dkva, dsink), r2 = _attn_bwd(qt, kv, kvt, doa, sinks, ca, sa, sb, ct, st, ff.chip_comm())
    ff.chip_done(r2)
    (dret, dgain), r1 = _ret_bwd(qkr, vr, gr, ret, dyrp, states, gain, rtabs, cr, sr, mid.pair_comm())
    mid.pair_done(r1)
    d_in, r2 = _dw_in_t((dqa, dkva, dret, dgates), (True, False, False, False), xn1, mid.chip_comm())
    mid.chip_done(r2)
    win = scatter(dict(w_in=d_in), "in")
    win.pair_done(_comm_alone(win.pair_comm(), "pair_exchange_in"))
    (dx, dg1), r2 = _inproj_bwd(dqa, dkva, dret, dgates, dh1, x, g1, w_in, win.chip_comm())
    win.chip_done(r2)
    small = dict(norm_mix_gain=dg1, b_gates=db, attn_sinks=dsink, ret_gn_gain=dgain, norm_mlp_gain=dg2,
                 norm_final_gain=dg3)
    return loss, dx, small, (win, mid, ff)


def _coords():
    return lax.axis_index("x"), lax.axis_index("y"), lax.axis_index("c")


def _flip(v, bit):
    return 1 - v if bit else v


def _gather_comm(shards):
    n = len(shards)

    def parts(ins, outs, sems):
        send_sems, recv_sems, local_sems = sems
        x, y, c = _coords()
        me, sib = (x, y, c), (x, y, 1 - c)
        chips = [(1 - x, y), (x, 1 - y), (1 - x, 1 - y)]

        def slot(a, blk):
            return outs[a].at[4 * blk[0] + 2 * blk[1] + blk[2]]

        def copy(a, k, blk, to, src=None):
            return pltpu.make_async_remote_copy(
                src_ref=slot(a, blk) if src is None else src, dst_ref=slot(a, blk),
                send_sem=send_sems.at[a, k], recv_sem=recv_sems.at[a, k], device_id=to, device_id_type=MESH)

        mine = [pltpu.make_async_copy(ins[a], slot(a, me), local_sems.at[a]) for a in range(n)]
        first = []
        for a in range(n):
            first.append(copy(a, 0, me, sib, src=ins[a]))
            first += [copy(a, 1 + j, me, (*chip, c), src=ins[a]) for j, chip in enumerate(chips)]
        return me, sib, chips, c, copy, mine, first

    def start(ins, outs, sems):
        *_, mine, first = parts(ins, outs, sems)
        for cp in mine + first:
            cp.start()

    def finish(ins, outs, sems):
        me, sib, chips, c, copy, mine, first = parts(ins, outs, sems)
        passed = []
        for j, chip in enumerate(chips):
            for a in range(n):
                copy(a, 1 + j, (*chip, c), me).wait_recv()
                fwd = copy(a, 4 + j, (*chip, c), sib)
                fwd.start()
                passed.append(fwd)
        for a in range(n):
            copy(a, 0, sib, me).wait_recv()
            for j, chip in enumerate(chips):
                copy(a, 4 + j, (*chip, 1 - c), me).wait_recv()
        for cp in first + passed:
            cp.wait_send()
        for cp in mine:
            cp.wait()

    return _Comm(list(shards), [jax.ShapeDtypeStruct((N_DEV,) + s.shape, s.dtype) for s in shards],
                 [pltpu.SemaphoreType.DMA((n, 7)), pltpu.SemaphoreType.DMA((n, 7)), pltpu.SemaphoreType.DMA((n,))],
                 start, finish)


COLUMN_SHARDED = ("w_in", "w_ff1")
COLUMN_RELAID = ("w_att_up",)


def _gather(shards, names):
    def unpack(got):
        return [jnp.transpose(g, (1, 0, 2)).reshape(g.shape[1], N_DEV * g.shape[2]) if k in COLUMN_RELAID
                else g.reshape(N_DEV * g.shape[1], g.shape[2]) for k, g in zip(names, got)]

    return _gather_comm([shards[k] for k in names]), unpack


def _pair_comm(grads):
    n = len(grads)

    def copies(g, r1, sems):
        send_sems, recv_sems = sems
        x, y, c = _coords()
        return [pltpu.make_async_remote_copy(
            src_ref=g[a].at[2 * j + (1 - c)], dst_ref=r1[a].at[j], send_sem=send_sems.at[a, j],
            recv_sem=recv_sems.at[a, j], device_id=(x, y, 1 - c), device_id_type=MESH)
            for a in range(n) for j in range(4)]

    def start(g, r1, sems):
        for cp in copies(g, r1, sems):
            cp.start()

    def finish(g, r1, sems):
        for cp in copies(g, r1, sems):
            cp.wait_recv()
        for cp in copies(g, r1, sems):
            cp.wait_send()

    return _Comm(list(grads), [jax.ShapeDtypeStruct((4,) + g.shape[1:], g.dtype) for g in grads],
                 [pltpu.SemaphoreType.DMA((n, 4)), pltpu.SemaphoreType.DMA((n, 4))], start, finish)


def _small_comm(small):
    def parts(ins, outs, sems):
        (small_ref,), (small_all,) = ins, outs
        ssend, srecv, lsem = sems
        x, y, c = _coords()
        me_idx = 4 * x + 2 * y + c
        own = pltpu.make_async_copy(small_ref, small_all.at[me_idx], lsem)
        sends, recvs = [], []
        for r in range(1, N_DEV):
            px, py, pc = _flip(x, r & 4), _flip(y, r & 2), _flip(c, r & 1)
            sends.append(pltpu.make_async_remote_copy(
                src_ref=small_ref, dst_ref=small_all.at[me_idx], send_sem=ssend.at[r - 1], recv_sem=srecv.at[r - 1],
                device_id=(px, py, pc), device_id_type=MESH))
            recvs.append(pltpu.make_async_remote_copy(
                src_ref=small_ref, dst_ref=small_all.at[4 * px + 2 * py + pc], send_sem=ssend.at[r - 1],
                recv_sem=srecv.at[r - 1], device_id=(px, py, pc), device_id_type=MESH))
        return own, sends, recvs

    def start(ins, outs, sems):
        own, sends, _ = parts(ins, outs, sems)
        own.start()
        for cp in sends:
            cp.start()

    def finish(ins, outs, sems):
        own, sends, recvs = parts(ins, outs, sems)
        for cp in recvs:
            cp.wait_recv()
        for cp in sends:
            cp.wait_send()
        own.wait()

    return _Comm([small], [jax.ShapeDtypeStruct((N_DEV,) + small.shape, small.dtype)],
                 [pltpu.SemaphoreType.DMA((N_DEV - 1,)), pltpu.SemaphoreType.DMA((N_DEV - 1,)),
                  pltpu.SemaphoreType.DMA], start, finish)


def _pair_sum(grads, r1s, c_arr, tag):
    n = len(grads)
    q = 4

    def body(c_ref, *refs):
        g, r, t = refs[:n], refs[n:2 * n], refs[2 * n:]
        for a in range(n):
            t[a][...] = (g[a][...] + r[a][...]).astype(t[a].dtype)

    def blk(arr):
        return (1, arr.shape[1] // q, arr.shape[2])

    grid_spec = pltpu.PrefetchScalarGridSpec(
        num_scalar_prefetch=1, grid=(4, q),
        in_specs=[pl.BlockSpec(blk(g), lambda j, s, c_ref: (2 * j + c_ref[0], s, 0)) for g in grads]
        + [pl.BlockSpec(blk(r), lambda j, s, c_ref: (j, s, 0)) for r in r1s],
        out_specs=[pl.BlockSpec(blk(r), lambda j, s, c_ref: (j, s, 0)) for r in r1s])
    return pl.pallas_call(
        body, name="pair_sum_" + tag, grid_spec=grid_spec,
        out_shape=[jax.ShapeDtypeStruct(r.shape, RS_PAYLOAD) for r in r1s],
        compiler_params=_params(2),
    )(c_arr, *grads, *r1s)


def _chip_comm(ts):
    n = len(ts)

    def copies(t, r2, sems):
        send_sems, recv_sems = sems
        x, y, c = _coords()
        out = []
        for a in range(n):
            for r in range(1, 4):
                tx, ty = _flip(x, r & 2), _flip(y, r & 1)
                out.append(pltpu.make_async_remote_copy(
                    src_ref=t[a].at[2 * tx + ty], dst_ref=r2[a].at[r - 1], send_sem=send_sems.at[a, r - 1],
                    recv_sem=recv_sems.at[a, r - 1], device_id=(tx, ty, c), device_id_type=MESH))
        return out

    def start(t, r2, sems):
        for cp in copies(t, r2, sems):
            cp.start()

    def finish(t, r2, sems):
        for cp in copies(t, r2, sems):
            cp.wait_recv()
        for cp in copies(t, r2, sems):
            cp.wait_send()

    return _Comm(list(ts), [jax.ShapeDtypeStruct((3,) + t.shape[1:], t.dtype) for t in ts],
                 [pltpu.SemaphoreType.DMA((n, 3)), pltpu.SemaphoreType.DMA((n, 3))], start, finish)


class _Scatter:
    def __init__(self, grads, tag):
        self.names, self.tag = tuple(grads), tag
        self.stacks = [jnp.transpose(g.reshape(g.shape[0], N_DEV, g.shape[1] // N_DEV), (1, 0, 2))
                       if k in COLUMN_RELAID else g.reshape(N_DEV, g.shape[0] // N_DEV, g.shape[1])
                       for k, g in grads.items()]

    def pair_comm(self):
        return _pair_comm(self.stacks)

    def pair_done(self, r1s):
        self.r1s = list(r1s)
        c_arr = jnp.reshape(lax.axis_index("c"), (1,)).astype(jnp.int32)
        self.ts = _pair_sum(self.stacks, self.r1s, c_arr, self.tag)

    def chip_comm(self):
        return _chip_comm(self.ts)

    def chip_done(self, r2s):
        self.r2s = list(r2s)


def _adamw(w, g, m, v):
    m = ADAM_B1 * m + (1.0 - ADAM_B1) * g
    v = ADAM_B2 * v + (1.0 - ADAM_B2) * jnp.square(g)
    m_hat = m / (1.0 - ADAM_B1 ** ADAM_STEP)
    v_hat = v / (1.0 - ADAM_B2 ** ADAM_STEP)
    delta = -ADAM_LR * (m_hat / (jnp.sqrt(v_hat) + ADAM_EPS) + ADAM_WD * w)
    return delta, m, v


ADAM_STEPS = 4


def _piece_specs(stacks):
    def rows(s):
        return s.shape[1] // ADAM_STEPS
    return ([pl.BlockSpec((1, rows(s), s.shape[2]), lambda i, idx_ref: (idx_ref[0], i, 0)) for s in stacks]
            + [pl.BlockSpec((1, rows(s), s.shape[2]), lambda i, idx_ref: (idx_ref[1], i, 0)) for s in stacks]
            + [pl.BlockSpec((3, rows(s), s.shape[2]), lambda i, idx_ref: (0, i, 0)) for s in stacks])


def _piece_sum(g0, r1, r2):
    return (((g0[0] + r1[0]) + r2[0].astype(F32)) + r2[1].astype(F32)) + r2[2].astype(F32)


def _shard_sum(stacks, r1s, r2s, idx_arr):
    n = len(stacks)

    def body(idx_ref, *refs):
        g0, r1, r2, outs = (refs[k * n:(k + 1) * n] for k in range(4))
        for a in range(n):
            outs[a][...] = _piece_sum(g0[a], r1[a], r2[a])

    out_specs = [pl.BlockSpec((s.shape[1] // ADAM_STEPS, s.shape[2]), lambda i, idx_ref: (i, 0)) for s in stacks]
    grid_spec = pltpu.PrefetchScalarGridSpec(num_scalar_prefetch=1, grid=(ADAM_STEPS,),
                                             in_specs=_piece_specs(stacks), out_specs=out_specs)
    return pl.pallas_call(
        body, name="shard_sum", grid_spec=grid_spec,
        out_shape=[jax.ShapeDtypeStruct(s.shape[1:], F32) for s in stacks],
        compiler_params=_params(1),
    )(idx_arr, *stacks, *r1s, *r2s)


def _adam_big(pieces, summed, ws, ms, vs, idx_arr):
    stacks, r1s, r2s = pieces
    n_p, n = len(stacks), len(ws)

    def body(idx_ref, *refs):
        it = iter(refs)
        g0, r1, r2, gs, w, m, v = ([next(it) for _ in range(k)] for k in (n_p, n_p, n_p, n - n_p, n, n, n))
        outs = list(it)
        for a in range(n):
            g = _piece_sum(g0[a], r1[a], r2[a]) if a < n_p else gs[a - n_p][...]
            delta, nm, nv = _adamw(w[a][...], g, m[a][...], v[a][...])
            outs[4 * a][...] = g
            outs[4 * a + 1][...] = delta
            outs[4 * a + 2][...] = nm
            outs[4 * a + 3][...] = nv

    def shard_spec(w):
        return pl.BlockSpec((w.shape[0] // ADAM_STEPS, w.shape[1]), lambda i, idx_ref: (i, 0))

    in_specs = _piece_specs(stacks) + [shard_spec(w) for w in ws[n_p:]] + [shard_spec(w) for w in ws] * 3
    out_specs = [shard_spec(w) for w in ws for _ in range(4)]
    grid_spec = pltpu.PrefetchScalarGridSpec(num_scalar_prefetch=1, grid=(ADAM_STEPS,), in_specs=in_specs,
                                             out_specs=out_specs)
    return pl.pallas_call(
        body, name="adam_big", grid_spec=grid_spec,
        out_shape=[jax.ShapeDtypeStruct(w.shape, F32) for w in ws for _ in range(4)],
        compiler_params=_params(1),
    )(idx_arr, *stacks, *r1s, *r2s, *summed, *ws, *ms, *vs)


def _adam_small(small_all, w, m, v):
    def body(all_ref, w_ref, m_ref, v_ref, g_ref, d_ref, nm_ref, nv_ref):
        g = all_ref[0]
        for k in range(1, N_DEV):
            g = g + all_ref[k]
        delta, nm, nv = _adamw(w_ref[...], g, m_ref[...], v_ref[...])
        g_ref[...] = g
        d_ref[...] = delta
        nm_ref[...] = nm
        nv_ref[...] = nv

    return pl.pallas_call(
        body, name="adam_small",
        in_specs=[VMEM_FULL] * 4, out_specs=[VMEM_FULL] * 4,
        out_shape=[jax.ShapeDtypeStruct(w.shape, F32)] * 4,
    )(small_all, w, m, v)


RS_PAYLOAD = BF


def _pack_rows(rows):
    rid = lax.broadcasted_iota(jnp.int32, (8, D_MODEL), 0)
    out = jnp.zeros((8, D_MODEL), F32)
    for i, r in enumerate(rows):
        out = jnp.where(rid == i, jnp.broadcast_to(r, (8, D_MODEL)), out)
    return out


def _small_rows(norm_mix_gain, b_gates, attn_sinks, ret_gn_gain, norm_mlp_gain, norm_final_gain):
    return [norm_mix_gain, b_gates[:, :D_MODEL], b_gates[:, D_MODEL:], ret_gn_gain, norm_mlp_gain,
            norm_final_gain.reshape(1, D_MODEL), jnp.pad(attn_sinks, ((0, 0), (0, D_MODEL - ATT_HEADS)))]


def _unpack_small(p):
    return dict(norm_mix_gain=p[0:1], b_gates=jnp.concatenate([p[1:2], p[2:3]], axis=1), ret_gn_gain=p[3:4],
                norm_mlp_gain=p[4:5], norm_final_gain=p[5], attn_sinks=p[6:7, :ATT_HEADS])


WEIGHTS = ("norm_mix_gain", "w_in", "b_gates", "attn_sinks", "ret_gn_gain", "w_att_up", "w_ret_up", "w_out",
           "norm_mlp_gain", "w_ff1", "w_ff2", "norm_final_gain")
BIG = ("w_in", "w_att_up", "w_ret_up", "w_out", "w_ff1", "w_ff2")


def kernel(x, norm_mix_gain, w_in, b_gates, attn_sinks, ret_gn_gain, w_att_up, w_ret_up, w_out, norm_mlp_gain, w_ff1, w_ff2, norm_final_gain, loss_target, m_norm_mix_gain, m_w_in, m_b_gates, m_attn_sinks, m_ret_gn_gain, m_w_att_up, m_w_ret_up, m_w_out, m_norm_mlp_gain, m_w_ff1, m_w_ff2, m_norm_final_gain, v_norm_mix_gain, v_w_in, v_b_gates, v_attn_sinks, v_ret_gn_gain, v_w_att_up, v_w_ret_up, v_w_out, v_norm_mlp_gain, v_w_ff1, v_w_ff2, v_norm_final_gain):
    w = dict(norm_mix_gain=norm_mix_gain, w_in=w_in, b_gates=b_gates, attn_sinks=attn_sinks, ret_gn_gain=ret_gn_gain,
             w_att_up=w_att_up, w_ret_up=w_ret_up, w_out=w_out, norm_mlp_gain=norm_mlp_gain, w_ff1=w_ff1,
             w_ff2=w_ff2, norm_final_gain=norm_final_gain)
    m = dict(norm_mix_gain=m_norm_mix_gain, w_in=m_w_in, b_gates=m_b_gates, attn_sinks=m_attn_sinks,
             ret_gn_gain=m_ret_gn_gain, w_att_up=m_w_att_up, w_ret_up=m_w_ret_up, w_out=m_w_out,
             norm_mlp_gain=m_norm_mlp_gain, w_ff1=m_w_ff1, w_ff2=m_w_ff2, norm_final_gain=m_norm_final_gain)
    v = dict(norm_mix_gain=v_norm_mix_gain, w_in=v_w_in, b_gates=v_b_gates, attn_sinks=v_attn_sinks,
             ret_gn_gain=v_ret_gn_gain, w_att_up=v_w_att_up, w_ret_up=v_w_ret_up, w_out=v_w_out,
             norm_mlp_gain=v_norm_mlp_gain, w_ff1=v_w_ff1, w_ff2=v_w_ff2, norm_final_gain=v_norm_final_gain)

    shards = {k: (w[k][0].T if k in COLUMN_SHARDED else w[k][0]).astype(BF) for k in BIG}
    comm, unpack = _gather(shards, ("w_in",))
    (w_in_full,) = unpack(_comm_alone(comm, "allgather_w_in"))

    loss_p, dx, small, groups = _local_step(
        x[0], loss_target[0], norm_mix_gain, b_gates, attn_sinks[0], ret_gn_gain, norm_mlp_gain,
        norm_final_gain.reshape(1, D_MODEL), w_in_full, shards)

    by_name = {}
    for grp in groups:
        for k, stack, r1, r2 in zip(grp.names, grp.stacks, grp.r1s, grp.r2s):
            by_name[k] = (stack, r1, r2)
    lane = lax.broadcasted_iota(jnp.int32, (1, D_MODEL), 1)
    sink_row = jnp.where(lane < ATT_HEADS, jnp.pad(small["attn_sinks"], ((0, 0), (0, D_MODEL - LANE))),
                         jnp.where(lane == ATT_HEADS, jnp.pad(loss_p, ((0, 0), (0, D_MODEL - LANE)), mode="edge"),
                                   0.0))
    small_pack = _pack_rows([small["norm_mix_gain"], small["b_gates"][:, :D_MODEL], small["b_gates"][:, D_MODEL:],
                             small["ret_gn_gain"], small["norm_mlp_gain"], small["norm_final_gain"], sink_row])
    xi, yi, ci = _coords()
    (small_all,) = _comm_alone(_small_comm(small_pack), "small_exchange")
    idx_arr = jnp.stack([4 * xi + 2 * yi + ci, 2 * xi + yi]).astype(jnp.int32)
    in_pieces = ["w_att_up", "w_ret_up", "w_out", "w_ff2", "w_in"]
    in_sum = ["w_ff1"]
    order = in_pieces + in_sum
    summed_t = _shard_sum(*[[by_name[k][i] for k in in_sum] for i in range(3)], idx_arr)

    def shard(tree, k):
        return tree[k][0].T if k == "w_in" else tree[k][0]

    adam_out = _adam_big([[by_name[k][i] for k in in_pieces] for i in range(3)], [g.T for g in summed_t],
                         [shard(w, k) for k in order], [shard(m, k) for k in order], [shard(v, k) for k in order],
                         idx_arr)
    big_out = [adam_out[4 * order.index(k) + i].T if k == "w_in" else adam_out[4 * order.index(k) + i]
               for k in BIG for i in range(4)]
    sm_g, sm_d, sm_m, sm_v = _adam_small(small_all, _pack_rows(_small_rows(*[w[k] for k in WEIGHTS if k not in BIG])),
                                         _pack_rows(_small_rows(*[m[k] for k in WEIGHTS if k not in BIG])),
                                         _pack_rows(_small_rows(*[v[k] for k in WEIGHTS if k not in BIG])))

    loss = sm_g[6, ATT_HEADS]
    kinds = []
    for idx, packed in enumerate((sm_g, sm_d, sm_m, sm_v)):
        out = _unpack_small(packed)
        for a, k in enumerate(BIG):
            out[k] = big_out[4 * a + idx][None]
        kinds.append(out)
    return (loss, dx[None], *[kinds[0][k] for k in WEIGHTS], *[kinds[1][k] for k in WEIGHTS],
            *[kinds[2][k] for k in WEIGHTS], *[kinds[3][k] for k in WEIGHTS])
```

```python
import functools
import math

import jax
import jax.numpy as jnp
from jax import lax
from jax.experimental import pallas as pl
from jax.experimental.pallas import tpu as pltpu

F32 = jnp.float32
BF = jnp.bfloat16
MESH = pl.DeviceIdType.MESH

D_MODEL = 1024
ATT_HEADS = 8
ATT_HEAD_DIM = 64
ATT_BLOCK = 128
ROPE_DIM = 16
ROPE_THETA = 500000.0
RET_HEADS = 4
RET_KEY_DIM = 128
RET_VAL_DIM = 256
RET_CHUNK = 128
RET_ROT_BASE = 10000.0
D_FF = 4096
NORM_EPS = 1e-6
GN_EPS = 1e-6
NEG_INF = -1e30
ATT_SCALE = ATT_HEAD_DIM ** -0.5
RET_SCALE = RET_KEY_DIM ** -0.5

C_QA, C_KA, C_VA, C_QR, C_KR, C_VR, C_GR, C_GATES, C_END = 0, 512, 640, 768, 1280, 1792, 2816, 3840, 5888

ADAM_LR = 0.001
ADAM_B1 = 0.9
ADAM_B2 = 0.999
ADAM_EPS = 1e-08
ADAM_WD = 0.01
ADAM_STEP = 10

N_DEV = 8
LANE = 128
VMEM_LIMIT = 56 * 1024 * 1024
TOKEN_TILE = 256
TN_TOKEN_TILE = 1024

RET_LOG_GAMMA = tuple(math.log1p(-(2.0 ** (-5.0 - h))) for h in range(RET_HEADS))
RET_CHUNK_DECAY = tuple(math.exp(RET_CHUNK * lg) for lg in RET_LOG_GAMMA)

VMEM_FULL = pl.BlockSpec(memory_space=pltpu.VMEM)
SMEM_FULL = pl.BlockSpec(memory_space=pltpu.SMEM)
ANY = pl.BlockSpec(memory_space=pl.ANY)


def _params(n_axes):
    return pltpu.CompilerParams(dimension_semantics=("arbitrary",) * n_axes, vmem_limit_bytes=VMEM_LIMIT)


def _nn(a, b):
    return jnp.dot(a, b, preferred_element_type=F32)


def _nt(a, b):
    return lax.dot_general(a, b, (((1,), (1,)), ((), ())), preferred_element_type=F32)


def _tn(a, b):
    return lax.dot_general(a, b, (((0,), (0,)), ((), ())), preferred_element_type=F32)


def _sigmoid(v):
    return 1.0 / (1.0 + jnp.exp(-v))


def _rows(tile, width):
    return pl.BlockSpec((tile, width), lambda i: (i, 0))


def _cols(height, tile):
    return pl.BlockSpec((height, tile), lambda i: (0, i))


class _Comm:
    def __init__(self, inputs, out_shapes, scratch, start, finish):
        self.inputs, self.out_shapes, self.scratch, self.start, self.finish = inputs, out_shapes, scratch, start, finish


def _join(a, b):
    na_in, na_out, na_sem = len(a.inputs), len(a.out_shapes), len(a.scratch)

    def split(ins, outs, sems):
        return (ins[:na_in], outs[:na_out], sems[:na_sem]), (ins[na_in:], outs[na_out:], sems[na_sem:])

    def start(ins, outs, sems):
        pa, pb = split(ins, outs, sems)
        a.start(*pa)
        b.start(*pb)

    def finish(ins, outs, sems):
        pa, pb = split(ins, outs, sems)
        a.finish(*pa)
        b.finish(*pb)

    return _Comm(list(a.inputs) + list(b.inputs), list(a.out_shapes) + list(b.out_shapes),
                 list(a.scratch) + list(b.scratch), start, finish)


def _call(body, args, comm=None, *, name, grid, in_specs, out_specs, out_shape, scratch_shapes=()):
    params = _params(len(grid))
    if comm is None:
        return pl.pallas_call(body, name=name, grid=grid, in_specs=in_specs, out_specs=out_specs, out_shape=out_shape,
                              scratch_shapes=scratch_shapes, compiler_params=params)(*args), ()
    single = not isinstance(out_specs, (list, tuple))
    out_specs_l = [out_specs] if single else list(out_specs)
    out_shape_l = [out_shape] if single else list(out_shape)
    n_in, n_out, n_scr = len(in_specs), len(out_specs_l), len(scratch_shapes)
    n_cin, n_cout = len(comm.inputs), len(comm.out_shapes)

    def hosted(*refs):
        it = iter(refs)
        ins, cin, outs, cout, scr = ([next(it) for _ in range(k)] for k in (n_in, n_cin, n_out, n_cout, n_scr))
        sems = list(it)
        ids = [pl.program_id(k) for k in range(len(grid))]
        first = functools.reduce(jnp.logical_and, [i == 0 for i in ids])
        last = functools.reduce(jnp.logical_and, [i == g - 1 for i, g in zip(ids, grid)])

        @pl.when(first)
        def _():
            comm.start(cin, cout, sems)

        body(*ins, *outs, *scr)

        @pl.when(last)
        def _():
            comm.finish(cin, cout, sems)

    res = pl.pallas_call(
        hosted, name=name, grid=grid, in_specs=list(in_specs) + [ANY] * n_cin,
        out_specs=out_specs_l + [ANY] * n_cout, out_shape=out_shape_l + list(comm.out_shapes),
        scratch_shapes=list(scratch_shapes) + list(comm.scratch), compiler_params=params)(*args, *comm.inputs)
    return (res[0] if single else res[:n_out]), res[n_out:]


def _comm_alone(comm, name):
    n_cin, n_cout = len(comm.inputs), len(comm.out_shapes)

    def body(*refs):
        cin, cout, sems = refs[:n_cin], refs[n_cin:n_cin + n_cout], refs[n_cin + n_cout:]
        comm.start(cin, cout, sems)
        comm.finish(cin, cout, sems)

    return pl.pallas_call(body, name=name, in_specs=[ANY] * n_cin, out_specs=[ANY] * n_cout,
                          out_shape=list(comm.out_shapes), scratch_shapes=list(comm.scratch))(*comm.inputs)


def _slabs(v, fn):
    return jnp.concatenate([fn(v[:, LANE * j:LANE * (j + 1)]) for j in range(v.shape[1] // LANE)], axis=1)


def _rope_att(v, ca, sa, sb):
    return _slabs(v, lambda t: t * ca + pltpu.roll(t, LANE - 8, 1) * sa + pltpu.roll(t, 8, 1) * sb)


def _rope_att_t(v, ca, sa, sb):
    return _slabs(v, lambda t: t * ca + pltpu.roll(t * sa, 8, 1) + pltpu.roll(t * sb, LANE - 8, 1))


def _rope_att_rows(v, ct, st, sign):
    parts = []
    for h in range(v.shape[0] // ATT_HEAD_DIM):
        r0 = ATT_HEAD_DIM * h
        x1, x2 = v[r0:r0 + 8], v[r0 + 8:r0 + 16]
        parts += [x1 * ct - sign * (x2 * st), x2 * ct + sign * (x1 * st), v[r0 + 16:r0 + ATT_HEAD_DIM]]
    return jnp.concatenate(parts, axis=0)


def _rope_ret(v, cr, sr):
    return _slabs(v, lambda t: t * cr + pltpu.roll(t, 64, 1) * sr)


def _rope_ret_t(v, cr, sr):
    return _slabs(v, lambda t: t * cr + pltpu.roll(t * sr, 64, 1))


def _rope_tables(seq):
    pos = jnp.arange(seq, dtype=F32)

    def cs(dim, theta):
        inv = theta ** (-jnp.arange(0, dim, 2, dtype=F32) / dim)
        ang = pos[:, None] * inv[None, :]
        return jnp.cos(ang), jnp.sin(ang)

    ca, sa_ = cs(ROPE_DIM, ROPE_THETA)
    zeros8 = jnp.zeros_like(ca)
    rest = ATT_HEAD_DIM - ROPE_DIM
    c64 = jnp.concatenate([ca, ca, jnp.ones((seq, rest), F32)], axis=1)
    sa64 = jnp.concatenate([-sa_, zeros8, jnp.zeros((seq, rest), F32)], axis=1)
    sb64 = jnp.concatenate([zeros8, sa_, jnp.zeros((seq, rest), F32)], axis=1)
    cr, sr_ = cs(RET_KEY_DIM, RET_ROT_BASE)
    return (jnp.tile(c64, (1, 2)), jnp.tile(sa64, (1, 2)), jnp.tile(sb64, (1, 2)),
            jnp.concatenate([cr, cr], axis=1), jnp.concatenate([-sr_, sr_], axis=1), ca.T, sa_.T)


def _ret_tables():
    c = RET_CHUNK
    lg = jnp.asarray(RET_LOG_GAMMA, F32)
    idx = jnp.arange(c, dtype=F32)
    diff = idx[:, None] - idx[None, :]
    intra = jnp.where(diff >= 0, jnp.exp(jnp.maximum(diff, 0.0) * lg[:, None, None]), 0.0)
    qd = jnp.exp((idx + 1.0)[None, :] * lg[:, None])[..., None]
    kd = jnp.exp((c - 1.0 - idx)[None, :] * lg[:, None])[..., None]
    return intra, jnp.broadcast_to(qd, (RET_HEADS, c, RET_KEY_DIM)), jnp.broadcast_to(kd, (RET_HEADS, c, RET_KEY_DIM))


def _inproj_fwd(x, g1, w_in, tabs, comm=None):
    seq = x.shape[0]
    tm = min(TOKEN_TILE, seq)

    def body(x_ref, g_ref, w_ref, ca_ref, sa_ref, sb_ref, cr_ref, sr_ref, ct_ref, st_ref,
             xn_ref, qt_ref, kv_ref, kvt_ref, qkr_ref, vr_ref, gr_ref, gates_ref):
        xf = x_ref[...]
        r = lax.rsqrt(jnp.mean(xf * xf, axis=-1, keepdims=True) + NORM_EPS)
        xn = (xf * r * g_ref[...]).astype(BF)
        xn_ref[...] = xn
        ct, st = ct_ref[...], st_ref[...]
        qt_ref[...] = _rope_att_rows(_nt(w_ref[C_QA:C_KA, :], xn), ct, st, 1.0).astype(BF)
        kvt = _nt(w_ref[C_KA:C_QR, :], xn)
        kvt_ref[:LANE, :] = _rope_att_rows(kvt[:LANE], ct, st, 1.0).astype(BF)
        kvt_ref[LANE:, :] = kvt[LANE:].astype(BF)
        kvn = _nt(xn, w_ref[C_KA:C_QR, :])
        kv_ref[:, :LANE] = _rope_att(kvn[:, :LANE], ca_ref[...], sa_ref[...], sb_ref[...]).astype(BF)
        kv_ref[:, LANE:] = kvn[:, LANE:].astype(BF)
        cr, sr = cr_ref[...], sr_ref[...]
        qr = _rope_ret(_nt(xn, w_ref[C_QR:C_KR, :]), cr, sr)
        qkr_ref[:, :512] = qr.astype(BF)
        kr = _rope_ret(_nt(xn, w_ref[C_KR:C_VR, :]), cr, sr) * RET_SCALE
        qkr_ref[:, 512:] = kr.astype(BF)
        vr_ref[...] = _nt(xn, w_ref[C_VR:C_GR, :]).astype(BF)
        gr_ref[...] = _nt(xn, w_ref[C_GR:C_GATES, :]).astype(BF)
        gates_ref[...] = _nt(xn, w_ref[C_GATES:C_END, :]).astype(BF)

    tab = _rows(tm, LANE)
    tab_t = _cols(8, tm)
    return _call(
        body, (x, g1, w_in, *tabs), comm, name="inproj_fwd", grid=(seq // tm,),
        in_specs=[_rows(tm, D_MODEL), VMEM_FULL, VMEM_FULL, tab, tab, tab, tab, tab, tab_t, tab_t],
        out_specs=[_rows(tm, D_MODEL), _cols(512, tm), _rows(tm, 256), _cols(256, tm), _rows(tm, 1024),
                   _rows(tm, 1024), _rows(tm, 1024), _rows(tm, 2048)],
        out_shape=[jax.ShapeDtypeStruct(s, BF) for s in ((seq, D_MODEL), (512, seq), (seq, 256), (256, seq),
                                                         (seq, 1024), (seq, 1024), (seq, 1024), (seq, 2048))])


ATT_Q_PER_KV = ATT_HEADS // 2


def _att_group(ref, kv):
    d = ATT_HEAD_DIM
    g = jnp.concatenate([ref[d * (ATT_Q_PER_KV * kv + j):d * (ATT_Q_PER_KV * kv + j + 1), :]
                         for j in range(ATT_Q_PER_KV)], axis=1)
    z = jnp.zeros_like(g)
    return jnp.concatenate([g, z] if kv == 0 else [z, g], axis=0)


def _att_probs(kwin, qpad, sink_ref, kv, i):
    c = ATT_BLOCK
    kj = lax.broadcasted_iota(jnp.int32, (2 * c, c), 0)
    qi = lax.broadcasted_iota(jnp.int32, (2 * c, c), 1)
    allowed = (kj > qi) & (kj <= qi + c) & ((kj >= c) | (i > 0))
    allowed = jnp.concatenate([allowed] * ATT_Q_PER_KV, axis=1)
    s = jnp.where(allowed, _nn(kwin, qpad) * ATT_SCALE, NEG_INF)
    sk = jnp.concatenate([jnp.full((1, c), sink_ref[ATT_Q_PER_KV * kv + j], F32) for j in range(ATT_Q_PER_KV)], axis=1)
    m = jnp.maximum(jnp.max(s, axis=0, keepdims=True), sk)
    pe = jnp.exp(s - m)
    psink = jnp.exp(sk - m)
    inv = 1.0 / (jnp.sum(pe, axis=0, keepdims=True) + psink)
    return pe * inv, psink * inv


def _attn_fwd(qt, kv, kvt, sinks, comm=None):
    seq = kv.shape[0]
    c = ATT_BLOCK
    d = ATT_HEAD_DIM
    nb = seq // c

    def body(sink_ref, qt_ref, kvc_ref, kvp_ref, kvtc_ref, kvtp_ref, ot_ref):
        i = pl.program_id(0)
        kwin = jnp.concatenate([kvp_ref[:, :LANE], kvc_ref[:, :LANE]], axis=0)
        vtwin = jnp.concatenate([kvtp_ref[LANE:, :], kvtc_ref[LANE:, :]], axis=1)
        for g in (0, 1):
            prob, _ = _att_probs(kwin, _att_group(qt_ref, g), sink_ref, g, i)
            out = _nn(vtwin[d * g:d * (g + 1), :], prob.astype(BF))
            for j in range(ATT_Q_PER_KV):
                h = ATT_Q_PER_KV * g + j
                ot_ref[d * h:d * (h + 1), :] = out[:, c * j:c * (j + 1)].astype(BF)

    prev = lambda i: jnp.maximum(i - 1, 0)
    return _call(
        body, (sinks, qt, kv, kv, kvt, kvt), comm, name="attn_fwd", grid=(nb,),
        in_specs=[SMEM_FULL, _cols(512, c), _rows(c, 256), pl.BlockSpec((c, 256), lambda i: (prev(i), 0)),
                  _cols(256, c), pl.BlockSpec((256, c), lambda i: (0, prev(i)))],
        out_specs=_cols(512, c),
        out_shape=jax.ShapeDtypeStruct((512, seq), BF))


def _ret_fwd(qkr, vr, gr, gain, rtabs, comm=None):
    seq = qkr.shape[0]
    c = RET_CHUNK
    nc = seq // c
    dk, dv = RET_KEY_DIM, RET_VAL_DIM

    def body(qk_ref, v_ref, g_ref, gain_ref, intra_ref, qd_ref, kd_ref, ret_ref, yrp_ref, st_ref, state):
        @pl.when(pl.program_id(0) == 0)
        def _():
            state[...] = jnp.zeros_like(state)

        for h in range(RET_HEADS):
            qh = qk_ref[:, dk * h:dk * (h + 1)]
            kh = qk_ref[:, 512 + dk * h:512 + dk * (h + 1)]
            vh = v_ref[:, dv * h:dv * (h + 1)]
            sh = state[h]
            shb = sh.astype(BF)
            st_ref[0, h] = shb
            att = _nt(qh, kh) * intra_ref[h]
            inner = _nn(att.astype(BF), vh)
            cross = _nn((qh.astype(F32) * qd_ref[h]).astype(BF), shb)
            out = inner + cross
            state[h] = sh * RET_CHUNK_DECAY[h] + _tn((kh.astype(F32) * kd_ref[h]).astype(BF), vh)
            ret_ref[:, dv * h:dv * (h + 1)] = out
            mu = jnp.mean(out, axis=-1, keepdims=True)
            d = out - mu
            var = jnp.mean(d * d, axis=-1, keepdims=True)
            y = d * lax.rsqrt(var + GN_EPS) * gain_ref[:, dv * h:dv * (h + 1)]
            g = g_ref[:, dv * h:dv * (h + 1)].astype(F32)
            yrp_ref[:, dv * h:dv * (h + 1)] = (g * _sigmoid(g) * y).astype(BF)

    return _call(
        body, (qkr, vr, gr, gain, *rtabs), comm, name="ret_fwd", grid=(nc,),
        in_specs=[_rows(c, 1024), _rows(c, 1024), _rows(c, 1024), VMEM_FULL, VMEM_FULL, VMEM_FULL, VMEM_FULL],
        out_specs=[_rows(c, 1024), _rows(c, 1024), pl.BlockSpec((1, RET_HEADS, dk, dv), lambda i: (i, 0, 0, 0))],
        out_shape=[jax.ShapeDtypeStruct((seq, 1024), F32), jax.ShapeDtypeStruct((seq, 1024), BF),
                   jax.ShapeDtypeStruct((nc, RET_HEADS, dk, dv), BF)],
        scratch_shapes=[pltpu.VMEM((RET_HEADS, dk, dv), F32)])


def _mix_fwd(oat, yrp, gates, bg, x, w_att, w_ret, w_out):
    seq = x.shape[0]
    tm = min(TOKEN_TILE, seq)

    def body(oat_ref, yrp_ref, gates_ref, bg_ref, x_ref, wa_ref, wr_ref, wo_ref, ya_ref, yr_ref, mg_ref, h1_ref):
        ya = _tn(oat_ref[...], wa_ref[...])
        yr = _nn(yrp_ref[...], wr_ref[...])
        gt = _sigmoid(gates_ref[...].astype(F32) + bg_ref[...])
        merged = (gt[:, :D_MODEL] * ya + gt[:, D_MODEL:] * yr).astype(BF)
        ya_ref[...] = ya.astype(BF)
        yr_ref[...] = yr.astype(BF)
        mg_ref[...] = merged
        h1_ref[...] = x_ref[...] + _nn(merged, wo_ref[...])

    return pl.pallas_call(
        body, name="mix_fwd", grid=(seq // tm,),
        in_specs=[_cols(512, tm), _rows(tm, 1024), _rows(tm, 2048), VMEM_FULL, _rows(tm, D_MODEL),
                  VMEM_FULL, VMEM_FULL, VMEM_FULL],
        out_specs=[_rows(tm, D_MODEL)] * 4,
        out_shape=[jax.ShapeDtypeStruct((seq, D_MODEL), BF)] * 3 + [jax.ShapeDtypeStruct((seq, D_MODEL), F32)],
        compiler_params=_params(1),
    )(oat, yrp, gates, bg, x, w_att, w_ret, w_out)


def _mlp_fwd_loss(h1, g2, g3, target, w_ff1, w_ff2):
    seq = h1.shape[0]
    tm = min(TOKEN_TILE, seq)

    def body(h1_ref, g2_ref, g3_ref, t_ref, w1_ref, w2_ref, xn2_ref, u_ref, dh2_ref, loss_ref, dg3_ref):
        @pl.when(pl.program_id(0) == 0)
        def _():
            loss_ref[...] = jnp.zeros_like(loss_ref)
            dg3_ref[...] = jnp.zeros_like(dg3_ref)

        h1v = h1_ref[...]
        r2 = lax.rsqrt(jnp.mean(h1v * h1v, axis=-1, keepdims=True) + NORM_EPS)
        xn2 = (h1v * r2 * g2_ref[...]).astype(BF)
        xn2_ref[...] = xn2
        u = _nt(xn2, w1_ref[...])
        u_ref[...] = u.astype(BF)
        hdn = jnp.square(jnp.maximum(u, 0.0)).astype(BF)
        h2 = h1v + _nn(hdn, w2_ref[...])
        r3 = lax.rsqrt(jnp.mean(h2 * h2, axis=-1, keepdims=True) + NORM_EPS)
        hn = h2 * r3
        err = hn * g3_ref[...] - t_ref[...]
        loss_ref[...] += jnp.sum(err * err) * (0.5 / D_MODEL)
        dy = err * (1.0 / D_MODEL)
        dg3_ref[...] += jnp.sum(dy * hn, axis=0, keepdims=True)
        z = dy * g3_ref[...]
        dh2_ref[...] = r3 * (z - hn * jnp.mean(z * hn, axis=-1, keepdims=True))

    return pl.pallas_call(
        body, name="mlp_fwd_loss", grid=(seq // tm,),
        in_specs=[_rows(tm, D_MODEL), VMEM_FULL, VMEM_FULL, _rows(tm, D_MODEL), VMEM_FULL, VMEM_FULL],
        out_specs=[_rows(tm, D_MODEL), _rows(tm, D_FF), _rows(tm, D_MODEL),
                   pl.BlockSpec((1, LANE), lambda i: (0, 0)), pl.BlockSpec((1, D_MODEL), lambda i: (0, 0))],
        out_shape=[jax.ShapeDtypeStruct((seq, D_MODEL), BF), jax.ShapeDtypeStruct((seq, D_FF), BF),
                   jax.ShapeDtypeStruct((seq, D_MODEL), F32), jax.ShapeDtypeStruct((1, LANE), F32),
                   jax.ShapeDtypeStruct((1, D_MODEL), F32)],
        compiler_params=_params(1),
    )(h1, g2, g3, target, w_ff1, w_ff2)


def _rms_bwd(dxn, xin, gain):
    r = lax.rsqrt(jnp.mean(xin * xin, axis=-1, keepdims=True) + NORM_EPS)
    xhat = xin * r
    z = dxn * gain
    dxin = r * (z - xhat * jnp.mean(z * xhat, axis=-1, keepdims=True))
    return dxin, jnp.sum(dxn * xhat, axis=0, keepdims=True)


def _mlp_bwd(dh2, u, h1, g2, w_ff1, w_ff2):
    seq = h1.shape[0]
    tm = min(TOKEN_TILE, seq)

    def body(dh2_ref, u_ref, h1_ref, g2_ref, w1_ref, w2_ref, du_ref, dh1_ref, dg2_ref):
        @pl.when(pl.program_id(0) == 0)
        def _():
            dg2_ref[...] = jnp.zeros_like(dg2_ref)

        dh2v = dh2_ref[...]
        dhdn = _nt(dh2v.astype(BF), w2_ref[...])
        du = (dhdn * (2.0 * jnp.maximum(u_ref[...].astype(F32), 0.0))).astype(BF)
        du_ref[...] = du
        dxn2 = _nn(du, w1_ref[...])
        dnorm, dg = _rms_bwd(dxn2, h1_ref[...], g2_ref[...])
        dh1_ref[...] = dh2v + dnorm
        dg2_ref[...] += dg

    return pl.pallas_call(
        body, name="mlp_bwd", grid=(seq // tm,),
        in_specs=[_rows(tm, D_MODEL), _rows(tm, D_FF), _rows(tm, D_MODEL), VMEM_FULL, VMEM_FULL, VMEM_FULL],
        out_specs=[_rows(tm, D_FF), _rows(tm, D_MODEL), pl.BlockSpec((1, D_MODEL), lambda i: (0, 0))],
        out_shape=[jax.ShapeDtypeStruct((seq, D_FF), BF), jax.ShapeDtypeStruct((seq, D_MODEL), F32),
                   jax.ShapeDtypeStruct((1, D_MODEL), F32)],
        compiler_params=_params(1),
    )(dh2, u, h1, g2, w_ff1, w_ff2)


def _mm_tn(a, b, name, relu_sq=False, a_is_t=False, comm=None):
    kdim, seq = a.shape if a_is_t else a.shape[::-1]
    ndim = b.shape[1]
    ts = min(TN_TOKEN_TILE, seq)
    tk = min(kdim, 1024)
    tn = min(ndim, 1024)
    n_steps = seq // ts

    def body(a_ref, b_ref, o_ref):
        @pl.when(pl.program_id(2) == 0)
        def _():
            o_ref[...] = jnp.zeros_like(o_ref)

        av = a_ref[...]
        if relu_sq:
            av = jnp.square(jnp.maximum(av.astype(F32), 0.0))
        mm = _nn if a_is_t else _tn
        o_ref[...] += mm(av.astype(BF), b_ref[...].astype(BF))

    a_spec = (pl.BlockSpec((tk, ts), lambda k, n, s: (k, s)) if a_is_t
              else pl.BlockSpec((ts, tk), lambda k, n, s: (s, k)))
    res = _call(
        body, (a, b), comm, name=name, grid=(kdim // tk, ndim // tn, n_steps),
        in_specs=[a_spec, pl.BlockSpec((ts, tn), lambda k, n, s: (s, n))],
        out_specs=pl.BlockSpec((tk, tn), lambda k, n, s: (k, n)),
        out_shape=jax.ShapeDtypeStruct((kdim, ndim), F32))
    return res[0] if comm is None else res


def _dw_in_t(pieces, is_t, xn1, comm=None):
    seq = xn1.shape[0]
    ts = min(2 * TN_TOKEN_TILE, seq)
    tr = 256
    n_blk = [(p.shape[0] if t else p.shape[1]) // tr for p, t in zip(pieces, is_t)]
    offs = [sum(n_blk[:i]) for i in range(len(pieces) + 1)]
    n_p = len(pieces)

    def body(*refs):
        piece_refs, xn_ref, o_ref = refs[:n_p], refs[n_p], refs[n_p + 1]
        r, s = pl.program_id(0), pl.program_id(1)

        @pl.when(s == 0)
        def _():
            o_ref[...] = jnp.zeros_like(o_ref)

        for p in range(n_p):
            @pl.when((r >= offs[p]) & (r < offs[p + 1]))
            def _(p=p):
                mm = _nn if is_t[p] else _tn
                o_ref[...] += mm(piece_refs[p][...], xn_ref[pl.ds(pl.multiple_of(s * ts, ts), ts), :])

    def piece_spec(p):
        def index(r, s):
            inside = (r >= offs[p]) & (r < offs[p + 1])
            tok, blk = jnp.where(inside, s, 0), jnp.clip(r - offs[p], 0, n_blk[p] - 1)
            return (blk, tok) if is_t[p] else (tok, blk)
        return pl.BlockSpec((tr, ts) if is_t[p] else (ts, tr), index)

    return _call(
        body, (*pieces, xn1), comm, name="dw_in", grid=(offs[-1], seq // ts),
        in_specs=[piece_spec(p) for p in range(n_p)] + [VMEM_FULL],
        out_specs=pl.BlockSpec((tr, D_MODEL), lambda r, s: (r, 0)),
        out_shape=jax.ShapeDtypeStruct((offs[-1] * tr, D_MODEL), F32))


def _mix_bwd(dh1, ya, yr, gates, bg, w_att, w_ret, w_out, comm=None):
    seq = dh1.shape[0]
    tm = min(TOKEN_TILE, seq)

    def body(dh1_ref, ya_ref, yr_ref, gates_ref, bg_ref, wa_ref, wr_ref, wo_ref,
             dya_ref, dyr_ref, dgates_ref, doa_ref, dyrp_ref, db_ref):
        @pl.when(pl.program_id(0) == 0)
        def _():
            db_ref[...] = jnp.zeros_like(db_ref)

        dm = _nt(dh1_ref[...].astype(BF), wo_ref[...])
        gt = _sigmoid(gates_ref[...].astype(F32) + bg_ref[...])
        ga, gr = gt[:, :D_MODEL], gt[:, D_MODEL:]
        dya = (dm * ga).astype(BF)
        dyr = (dm * gr).astype(BF)
        dya_ref[...] = dya
        dyr_ref[...] = dyr
        dga = dm * ya_ref[...].astype(F32) * ga * (1.0 - ga)
        dgr = dm * yr_ref[...].astype(F32) * gr * (1.0 - gr)
        dgates_ref[:, :D_MODEL] = dga.astype(BF)
        dgates_ref[:, D_MODEL:] = dgr.astype(BF)
        db_ref[:, :D_MODEL] += jnp.sum(dga, axis=0, keepdims=True)
        db_ref[:, D_MODEL:] += jnp.sum(dgr, axis=0, keepdims=True)
        doa_ref[...] = _nt(wa_ref[...], dya).astype(BF)
        dyrp_ref[...] = _nt(dyr, wr_ref[...]).astype(BF)

    return _call(
        body, (dh1, ya, yr, gates, bg, w_att, w_ret, w_out), comm, name="mix_bwd", grid=(seq // tm,),
        in_specs=[_rows(tm, D_MODEL), _rows(tm, D_MODEL), _rows(tm, D_MODEL), _rows(tm, 2048), VMEM_FULL,
                  VMEM_FULL, VMEM_FULL, VMEM_FULL],
        out_specs=[_rows(tm, D_MODEL), _rows(tm, D_MODEL), _rows(tm, 2048), _cols(512, tm), _rows(tm, 1024),
                   pl.BlockSpec((1, 2048), lambda i: (0, 0))],
        out_shape=[jax.ShapeDtypeStruct((seq, D_MODEL), BF), jax.ShapeDtypeStruct((seq, D_MODEL), BF),
                   jax.ShapeDtypeStruct((seq, 2048), BF), jax.ShapeDtypeStruct((512, seq), BF),
                   jax.ShapeDtypeStruct((seq, 1024), BF), jax.ShapeDtypeStruct((1, 2048), F32)])


def _ret_bwd(qkr, vr, gr, ret, dyrp, states, gain, rtabs, cr, sr, comm=None):
    seq = qkr.shape[0]
    c = RET_CHUNK
    nc = seq // c
    dk, dv = RET_KEY_DIM, RET_VAL_DIM

    def body(qk_ref, v_ref, g_ref, ret_ref, dyp_ref, st_ref, gain_ref, intra_ref, qd_ref, kd_ref, cr_ref, sr_ref,
             dall_ref, dgain_ref, dstate):
        @pl.when(pl.program_id(0) == 0)
        def _():
            dstate[...] = jnp.zeros_like(dstate)
            dgain_ref[...] = jnp.zeros_like(dgain_ref)

        crv, srv = cr_ref[...], sr_ref[...]
        for h in range(RET_HEADS):
            vs = slice(dv * h, dv * (h + 1))
            qh = qk_ref[:, dk * h:dk * (h + 1)]
            kh = qk_ref[:, 512 + dk * h:512 + dk * (h + 1)]
            vh = v_ref[:, vs]
            out = ret_ref[:, vs]
            g = g_ref[:, vs].astype(F32)
            dyp = dyp_ref[:, vs].astype(F32)
            gain_h = gain_ref[:, vs]
            mu = jnp.mean(out, axis=-1, keepdims=True)
            d = out - mu
            rstd = lax.rsqrt(jnp.mean(d * d, axis=-1, keepdims=True) + GN_EPS)
            yn = d * rstd
            sg = _sigmoid(g)
            dg = dyp * (yn * gain_h) * (sg * (1.0 + g * (1.0 - sg)))
            dy = dyp * (g * sg)
            dgain_ref[:, vs] += jnp.sum(dy * yn, axis=0, keepdims=True)
            dyn = dy * gain_h
            dout = rstd * (dyn - jnp.mean(dyn, axis=-1, keepdims=True)
                           - yn * jnp.mean(dyn * yn, axis=-1, keepdims=True))
            doutb = dout.astype(BF)
            sc = st_ref[0, h]
            dsp = dstate[h]
            dspb = dsp.astype(BF)
            intra, qdv, kdv = intra_ref[h], qd_ref[h], kd_ref[h]
            att = _nt(qh, kh) * intra
            dab = (_nt(doutb, vh) * intra).astype(BF)
            qdec = (qh.astype(F32) * qdv).astype(BF)
            kdec = (kh.astype(F32) * kdv).astype(BF)
            dq = _nn(dab, kh) + _nt(doutb, sc) * qdv
            dkk = _tn(dab, qh) + _nt(vh, dspb) * kdv
            dvv = _tn(att.astype(BF), doutb) + _nn(kdec, dspb)
            dstate[h] = dsp * RET_CHUNK_DECAY[h] + _tn(qdec, doutb)
            dall_ref[:, dk * h:dk * (h + 1)] = _rope_ret_t(dq, crv, srv).astype(BF)
            dall_ref[:, 512 + dk * h:512 + dk * (h + 1)] = (_rope_ret_t(dkk, crv, srv) * RET_SCALE).astype(BF)
            dall_ref[:, 1024 + dv * h:1024 + dv * (h + 1)] = dvv.astype(BF)
            dall_ref[:, 2048 + dv * h:2048 + dv * (h + 1)] = dg.astype(BF)

    def rev(width):
        return pl.BlockSpec((c, width), lambda i: (nc - 1 - i, 0))

    return _call(
        body, (qkr, vr, gr, ret, dyrp, states, gain, *rtabs, cr, sr), comm, name="ret_bwd", grid=(nc,),
        in_specs=[rev(1024), rev(1024), rev(1024), rev(1024), rev(1024),
                  pl.BlockSpec((1, RET_HEADS, dk, dv), lambda i: (nc - 1 - i, 0, 0, 0)),
                  VMEM_FULL, VMEM_FULL, VMEM_FULL, VMEM_FULL, rev(LANE), rev(LANE)],
        out_specs=[rev(3072), pl.BlockSpec((1, 1024), lambda i: (0, 0))],
        out_shape=[jax.ShapeDtypeStruct((seq, 3072), BF), jax.ShapeDtypeStruct((1, 1024), F32)],
        scratch_shapes=[pltpu.VMEM((RET_HEADS, dk, dv), F32)])


def _attn_bwd(qt, kv, kvt, dot, sinks, ca, sa, sb, ct, st, comm=None):
    seq = kv.shape[0]
    c = ATT_BLOCK
    d = ATT_HEAD_DIM
    nb = seq // c

    def body(sink_ref, qt_ref, kvc_ref, kvp_ref, kvtc_ref, kvtp_ref, dot_ref, cap_ref, sap_ref, sbp_ref,
             ctc_ref, stc_ref, dqt_ref, dkv_ref, dsink_ref, carry):
        i = pl.program_id(0)

        @pl.when(i == 0)
        def _():
            carry[...] = jnp.zeros_like(carry)
            dsink_ref[...] = jnp.zeros_like(dsink_ref)

        def flush(total):
            dk_pre = _rope_att_t(total[:, :LANE], cap_ref[...], sap_ref[...], sbp_ref[...])
            dkv_ref[:, :LANE] = dk_pre.astype(BF)
            dkv_ref[:, LANE:] = total[:, LANE:].astype(BF)

        @pl.when(i < nb)
        def _():
            kwin = jnp.concatenate([kvp_ref[:, :LANE], kvc_ref[:, :LANE]], axis=0)
            vwin = jnp.concatenate([kvp_ref[:, LANE:], kvc_ref[:, LANE:]], axis=0)
            ktwin = jnp.concatenate([kvtp_ref[:LANE, :], kvtc_ref[:LANE, :]], axis=1)
            ctc, stc = ctc_ref[...], stc_ref[...]
            lane1 = lax.broadcasted_iota(jnp.int32, (1, LANE), 1)
            dkw = jnp.zeros((2 * c, LANE), F32)
            dvw = jnp.zeros((2 * c, LANE), F32)
            dsink = jnp.zeros((1, LANE), F32)
            for g in (0, 1):
                qpad, dopad = _att_group(qt_ref, g), _att_group(dot_ref, g)
                prob, psink = _att_probs(kwin, qpad, sink_ref, g, i)
                dprob = _nn(vwin, dopad)
                drow = jnp.sum(dprob * prob, axis=0, keepdims=True)
                ds = (prob * (dprob - drow) * ATT_SCALE).astype(BF)
                dqg = _nn(ktwin[d * g:d * (g + 1), :], ds)
                dkw = dkw + _nt(ds, qpad)
                dvw = dvw + _nt(prob.astype(BF), dopad)
                dsink_lanes = psink * drow
                for j in range(ATT_Q_PER_KV):
                    h = ATT_Q_PER_KV * g + j
                    dqt_ref[d * h:d * (h + 1), :] = _rope_att_rows(dqg[:, c * j:c * (j + 1)], ctc, stc, -1.0).astype(BF)
                    dsink = dsink + jnp.where(lane1 == h, -jnp.sum(dsink_lanes[:, c * j:c * (j + 1)]), 0.0)
            dsink_ref[...] += dsink
            flush(carry[...] + jnp.concatenate([dkw[:c], dvw[:c]], axis=1))
            carry[...] = jnp.concatenate([dkw[c:], dvw[c:]], axis=1)

        @pl.when(i == nb)
        def _():
            flush(carry[...])

    cur = lambda i: jnp.minimum(i, nb - 1)
    prev = lambda i: jnp.clip(i - 1, 0, nb - 1)
    rows_cur = lambda w: pl.BlockSpec((c, w), lambda i: (cur(i), 0))
    rows_prev = lambda w: pl.BlockSpec((c, w), lambda i: (prev(i), 0))
    cols_cur = lambda h: pl.BlockSpec((h, c), lambda i: (0, cur(i)))
    cols_prev = lambda h: pl.BlockSpec((h, c), lambda i: (0, prev(i)))
    return _call(
        body, (sinks, qt, kv, kv, kvt, kvt, dot, ca, sa, sb, ct, st), comm, name="attn_bwd", grid=(nb + 1,),
        in_specs=[SMEM_FULL, cols_cur(512), rows_cur(256), rows_prev(256), cols_cur(256), cols_prev(256),
                  cols_cur(512), rows_prev(LANE), rows_prev(LANE), rows_prev(LANE), cols_cur(8), cols_cur(8)],
        out_specs=[cols_cur(512), rows_prev(256), pl.BlockSpec((1, LANE), lambda i: (0, 0))],
        out_shape=[jax.ShapeDtypeStruct((512, seq), BF), jax.ShapeDtypeStruct((seq, 256), BF),
                   jax.ShapeDtypeStruct((1, LANE), F32)],
        scratch_shapes=[pltpu.VMEM((c, 256), F32)])


def _inproj_bwd(dqa, dkva, dret, dgates, dh1, x, g1, w_in, comm=None):
    seq = x.shape[0]
    tm = min(TOKEN_TILE, seq)

    def body(dqa_ref, dkva_ref, dret_ref, dgates_ref, dh1_ref, x_ref, g_ref, w_ref, dx_ref, dg1_ref):
        @pl.when(pl.program_id(0) == 0)
        def _():
            dg1_ref[...] = jnp.zeros_like(dg1_ref)

        dxn = (_tn(dqa_ref[...], w_ref[C_QA:C_KA, :]) + _nn(dkva_ref[...], w_ref[C_KA:C_QR, :])
               + _nn(dret_ref[...], w_ref[C_QR:C_GATES, :]) + _nn(dgates_ref[...], w_ref[C_GATES:C_END, :]))
        dnorm, dg = _rms_bwd(dxn, x_ref[...], g_ref[...])
        dx_ref[...] = dh1_ref[...] + dnorm
        dg1_ref[...] += dg

    return _call(
        body, (dqa, dkva, dret, dgates, dh1, x, g1, w_in), comm, name="inproj_bwd", grid=(seq // tm,),
        in_specs=[_cols(512, tm), _rows(tm, 256), _rows(tm, 3072), _rows(tm, 2048), _rows(tm, D_MODEL),
                  _rows(tm, D_MODEL), VMEM_FULL, VMEM_FULL],
        out_specs=[_rows(tm, D_MODEL), pl.BlockSpec((1, D_MODEL), lambda i: (0, 0))],
        out_shape=[jax.ShapeDtypeStruct((seq, D_MODEL), F32), jax.ShapeDtypeStruct((1, D_MODEL), F32)])


def _local_step(x, target, g1, bg, sinks, gain, g2, g3, w_in, shards):
    seq = x.shape[0]
    ca, sa, sb, cr, sr, ct, st = _rope_tables(seq)
    rtabs = _ret_tables()
    scatter = _Scatter

    comm_mid, unpack_mid = _gather(shards, ("w_att_up", "w_ret_up", "w_out"))
    comm_ff1, unpack_ff1 = _gather(shards, ("w_ff1",))
    (xn1, qt, kv, kvt, qkr, vr, gr, gates), got = _inproj_fwd(x, g1, w_in, (ca, sa, sb, cr, sr, ct, st),
                                                              _join(comm_mid, comm_ff1))
    w_att, w_ret, w_out = unpack_mid(got[:3])
    (w_ff1,) = unpack_ff1(got[3:])
    comm, unpack = _gather(shards, ("w_ff2",))
    oa, got = _attn_fwd(qt, kv, kvt, sinks, comm)
    (w_ff2,) = unpack(got)
    (ret, yrp, states), _ = _ret_fwd(qkr, vr, gr, gain, rtabs)
    ya, yr, merged, h1 = _mix_fwd(oa, yrp, gates, bg, x, w_att, w_ret, w_out)
    xn2, u, dh2, loss, dg3 = _mlp_fwd_loss(h1, g2, g3, target, w_ff1, w_ff2)

    du, dh1, dg2 = _mlp_bwd(dh2, u, h1, g2, w_ff1, w_ff2)
    ff2 = scatter(dict(w_ff2=_mm_tn(u, dh2, "dw_ff2", relu_sq=True)), "ff2")
    d_ff1, r1 = _mm_tn(du, xn2, "dw_ff1", comm=ff2.pair_comm())
    ff2.pair_done(r1)
    ff1 = scatter(dict(w_ff1=d_ff1), "ff1")
    (dya, dyr, dgates, doa, dyrp, db), got = _mix_bwd(dh1, ya, yr, gates, bg, w_att, w_ret, w_out,
                                                      _join(ff1.pair_comm(), ff2.chip_comm()))
    ff1.pair_done(got[:1])
    ff2.chip_done(got[1:])
    mid = scatter(dict(w_att_up=_mm_tn(oa, dya, "dw_att_up", a_is_t=True), w_ret_up=_mm_tn(yrp, dyr, "dw_ret_up"),
                       w_out=_mm_tn(merged, dh1, "dw_out")), "mid")
    (dqa, dkva, dsink), r2 = _attn_bwd(qt, kv, kvt, doa, sinks, ca, sa, sb, ct, st, ff1.chip_comm())
    ff1.chip_done(r2)
    (dret, dgain), r1 = _ret_bwd(qkr, vr, gr, ret, dyrp, states, gain, rtabs, cr, sr, mid.pair_comm())
    mid.pair_done(r1)
    d_in, r2 = _dw_in_t((dqa, dkva, dret, dgates), (True, False, False, False), xn1, mid.chip_comm())
    mid.chip_done(r2)
    win = scatter(dict(w_in=d_in), "in")
    win.pair_done(_comm_alone(win.pair_comm(), "pair_exchange_in"))
    (dx, dg1), r2 = _inproj_bwd(dqa, dkva, dret, dgates, dh1, x, g1, w_in, win.chip_comm())
    win.chip_done(r2)
    small = dict(norm_mix_gain=dg1, b_gates=db, attn_sinks=dsink, ret_gn_gain=dgain, norm_mlp_gain=dg2,
                 norm_final_gain=dg3)
    return loss, dx, small, (win, mid, ff1, ff2)


def _coords():
    return lax.axis_index("x"), lax.axis_index("y"), lax.axis_index("c")


def _flip(v, bit):
    return 1 - v if bit else v


def _gather_comm(shards):
    n = len(shards)

    def parts(ins, outs, sems):
        send_sems, recv_sems, local_sems = sems
        x, y, c = _coords()
        me, sib = (x, y, c), (x, y, 1 - c)
        chips = [(1 - x, y), (x, 1 - y), (1 - x, 1 - y)]

        def slot(a, blk):
            return outs[a].at[4 * blk[0] + 2 * blk[1] + blk[2]]

        def copy(a, k, blk, to, src=None):
            return pltpu.make_async_remote_copy(
                src_ref=slot(a, blk) if src is None else src, dst_ref=slot(a, blk),
                send_sem=send_sems.at[a, k], recv_sem=recv_sems.at[a, k], device_id=to, device_id_type=MESH)

        mine = [pltpu.make_async_copy(ins[a], slot(a, me), local_sems.at[a]) for a in range(n)]
        first = []
        for a in range(n):
            first.append(copy(a, 0, me, sib, src=ins[a]))
            first += [copy(a, 1 + j, me, (*chip, c), src=ins[a]) for j, chip in enumerate(chips)]
        return me, sib, chips, c, copy, mine, first

    def start(ins, outs, sems):
        *_, mine, first = parts(ins, outs, sems)
        for cp in mine + first:
            cp.start()

    def finish(ins, outs, sems):
        me, sib, chips, c, copy, mine, first = parts(ins, outs, sems)
        passed = []
        for j, chip in enumerate(chips):
            for a in range(n):
                copy(a, 1 + j, (*chip, c), me).wait_recv()
                fwd = copy(a, 4 + j, (*chip, c), sib)
                fwd.start()
                passed.append(fwd)
        for a in range(n):
            copy(a, 0, sib, me).wait_recv()
            for j, chip in enumerate(chips):
                copy(a, 4 + j, (*chip, 1 - c), me).wait_recv()
        for cp in first + passed:
            cp.wait_send()
        for cp in mine:
            cp.wait()

    return _Comm(list(shards), [jax.ShapeDtypeStruct((N_DEV,) + s.shape, s.dtype) for s in shards],
                 [pltpu.SemaphoreType.DMA((n, 7)), pltpu.SemaphoreType.DMA((n, 7)), pltpu.SemaphoreType.DMA((n,))],
                 start, finish)


COLUMN_SHARDED = ("w_in", "w_ff1")
COLUMN_RELAID = ("w_att_up",)


def _gather(shards, names):
    def unpack(got):
        return [jnp.transpose(g, (1, 0, 2)).reshape(g.shape[1], N_DEV * g.shape[2]) if k in COLUMN_RELAID
                else g.reshape(N_DEV * g.shape[1], g.shape[2]) for k, g in zip(names, got)]

    return _gather_comm([shards[k] for k in names]), unpack


def _pair_comm(grads):
    n = len(grads)

    def copies(g, r1, sems):
        send_sems, recv_sems = sems
        x, y, c = _coords()
        return [pltpu.make_async_remote_copy(
            src_ref=g[a].at[2 * j + (1 - c)], dst_ref=r1[a].at[j], send_sem=send_sems.at[a, j],
            recv_sem=recv_sems.at[a, j], device_id=(x, y, 1 - c), device_id_type=MESH)
            for a in range(n) for j in range(4)]

    def start(g, r1, sems):
        for cp in copies(g, r1, sems):
            cp.start()

    def finish(g, r1, sems):
        for cp in copies(g, r1, sems):
            cp.wait_recv()
        for cp in copies(g, r1, sems):
            cp.wait_send()

    return _Comm(list(grads), [jax.ShapeDtypeStruct((4,) + g.shape[1:], g.dtype) for g in grads],
                 [pltpu.SemaphoreType.DMA((n, 4)), pltpu.SemaphoreType.DMA((n, 4))], start, finish)


def _small_comm(small):
    def parts(ins, outs, sems):
        (small_ref,), (small_all,) = ins, outs
        ssend, srecv, lsem = sems
        x, y, c = _coords()
        me_idx = 4 * x + 2 * y + c
        own = pltpu.make_async_copy(small_ref, small_all.at[me_idx], lsem)
        sends, recvs = [], []
        for r in range(1, N_DEV):
            px, py, pc = _flip(x, r & 4), _flip(y, r & 2), _flip(c, r & 1)
            sends.append(pltpu.make_async_remote_copy(
                src_ref=small_ref, dst_ref=small_all.at[me_idx], send_sem=ssend.at[r - 1], recv_sem=srecv.at[r - 1],
                device_id=(px, py, pc), device_id_type=MESH))
            recvs.append(pltpu.make_async_remote_copy(
                src_ref=small_ref, dst_ref=small_all.at[4 * px + 2 * py + pc], send_sem=ssend.at[r - 1],
                recv_sem=srecv.at[r - 1], device_id=(px, py, pc), device_id_type=MESH))
        return own, sends, recvs

    def start(ins, outs, sems):
        own, sends, _ = parts(ins, outs, sems)
        own.start()
        for cp in sends:
            cp.start()

    def finish(ins, outs, sems):
        own, sends, recvs = parts(ins, outs, sems)
        for cp in recvs:
            cp.wait_recv()
        for cp in sends:
            cp.wait_send()
        own.wait()

    return _Comm([small], [jax.ShapeDtypeStruct((N_DEV,) + small.shape, small.dtype)],
                 [pltpu.SemaphoreType.DMA((N_DEV - 1,)), pltpu.SemaphoreType.DMA((N_DEV - 1,)),
                  pltpu.SemaphoreType.DMA], start, finish)


def _pair_sum(grads, r1s, c_arr, tag):
    n = len(grads)
    q = 4

    def body(c_ref, *refs):
        g, r, t = refs[:n], refs[n:2 * n], refs[2 * n:]
        for a in range(n):
            t[a][...] = (g[a][...] + r[a][...]).astype(t[a].dtype)

    def blk(arr):
        return (1, arr.shape[1] // q, arr.shape[2])

    grid_spec = pltpu.PrefetchScalarGridSpec(
        num_scalar_prefetch=1, grid=(4, q),
        in_specs=[pl.BlockSpec(blk(g), lambda j, s, c_ref: (2 * j + c_ref[0], s, 0)) for g in grads]
        + [pl.BlockSpec(blk(r), lambda j, s, c_ref: (j, s, 0)) for r in r1s],
        out_specs=[pl.BlockSpec(blk(r), lambda j, s, c_ref: (j, s, 0)) for r in r1s])
    return pl.pallas_call(
        body, name="pair_sum_" + tag, grid_spec=grid_spec,
        out_shape=[jax.ShapeDtypeStruct(r.shape, RS_PAYLOAD) for r in r1s],
        compiler_params=_params(2),
    )(c_arr, *grads, *r1s)


def _chip_comm(ts):
    n = len(ts)

    def copies(t, r2, sems):
        send_sems, recv_sems = sems
        x, y, c = _coords()
        out = []
        for a in range(n):
            for r in range(1, 4):
                tx, ty = _flip(x, r & 2), _flip(y, r & 1)
                out.append(pltpu.make_async_remote_copy(
                    src_ref=t[a].at[2 * tx + ty], dst_ref=r2[a].at[r - 1], send_sem=send_sems.at[a, r - 1],
                    recv_sem=recv_sems.at[a, r - 1], device_id=(tx, ty, c), device_id_type=MESH))
        return out

    def start(t, r2, sems):
        for cp in copies(t, r2, sems):
            cp.start()

    def finish(t, r2, sems):
        for cp in copies(t, r2, sems):
            cp.wait_recv()
        for cp in copies(t, r2, sems):
            cp.wait_send()

    return _Comm(list(ts), [jax.ShapeDtypeStruct((3,) + t.shape[1:], t.dtype) for t in ts],
                 [pltpu.SemaphoreType.DMA((n, 3)), pltpu.SemaphoreType.DMA((n, 3))], start, finish)


class _Scatter:
    def __init__(self, grads, tag):
        self.names, self.tag = tuple(grads), tag
        self.stacks = [jnp.transpose(g.reshape(g.shape[0], N_DEV, g.shape[1] // N_DEV), (1, 0, 2))
                       if k in COLUMN_RELAID else g.reshape(N_DEV, g.shape[0] // N_DEV, g.shape[1])
                       for k, g in grads.items()]

    def pair_comm(self):
        return _pair_comm(self.stacks)

    def pair_done(self, r1s):
        self.r1s = list(r1s)
        c_arr = jnp.reshape(lax.axis_index("c"), (1,)).astype(jnp.int32)
        self.ts = _pair_sum(self.stacks, self.r1s, c_arr, self.tag)

    def chip_comm(self):
        return _chip_comm(self.ts)

    def chip_done(self, r2s):
        self.r2s = list(r2s)


def _adamw(w, g, m, v):
    m = ADAM_B1 * m + (1.0 - ADAM_B1) * g
    v = ADAM_B2 * v + (1.0 - ADAM_B2) * jnp.square(g)
    m_hat = m / (1.0 - ADAM_B1 ** ADAM_STEP)
    v_hat = v / (1.0 - ADAM_B2 ** ADAM_STEP)
    delta = -ADAM_LR * (m_hat / (jnp.sqrt(v_hat) + ADAM_EPS) + ADAM_WD * w)
    return delta, m, v


ADAM_STEPS = 4


def _piece_specs(stacks):
    def rows(s):
        return s.shape[1] // ADAM_STEPS
    return ([pl.BlockSpec((1, rows(s), s.shape[2]), lambda i, idx_ref: (idx_ref[0], i, 0)) for s in stacks]
            + [pl.BlockSpec((1, rows(s), s.shape[2]), lambda i, idx_ref: (idx_ref[1], i, 0)) for s in stacks]
            + [pl.BlockSpec((3, rows(s), s.shape[2]), lambda i, idx_ref: (0, i, 0)) for s in stacks])


def _piece_sum(g0, r1, r2):
    return (((g0[0] + r1[0]) + r2[0].astype(F32)) + r2[1].astype(F32)) + r2[2].astype(F32)


def _shard_sum(stacks, r1s, r2s, idx_arr):
    n = len(stacks)

    def body(idx_ref, *refs):
        g0, r1, r2, outs = (refs[k * n:(k + 1) * n] for k in range(4))
        for a in range(n):
            outs[a][...] = _piece_sum(g0[a], r1[a], r2[a])

    out_specs = [pl.BlockSpec((s.shape[1] // ADAM_STEPS, s.shape[2]), lambda i, idx_ref: (i, 0)) for s in stacks]
    grid_spec = pltpu.PrefetchScalarGridSpec(num_scalar_prefetch=1, grid=(ADAM_STEPS,),
                                             in_specs=_piece_specs(stacks), out_specs=out_specs)
    return pl.pallas_call(
        body, name="shard_sum", grid_spec=grid_spec,
        out_shape=[jax.ShapeDtypeStruct(s.shape[1:], F32) for s in stacks],
        compiler_params=_params(1),
    )(idx_arr, *stacks, *r1s, *r2s)


def _adam_big(pieces, summed, ws, ms, vs, idx_arr):
    stacks, r1s, r2s = pieces
    n_p, n = len(stacks), len(ws)

    def body(idx_ref, *refs):
        it = iter(refs)
        g0, r1, r2, gs, w, m, v = ([next(it) for _ in range(k)] for k in (n_p, n_p, n_p, n - n_p, n, n, n))
        outs = list(it)
        for a in range(n):
            g = _piece_sum(g0[a], r1[a], r2[a]) if a < n_p else gs[a - n_p][...]
            delta, nm, nv = _adamw(w[a][...], g, m[a][...], v[a][...])
            outs[4 * a][...] = g
            outs[4 * a + 1][...] = delta
            outs[4 * a + 2][...] = nm
            outs[4 * a + 3][...] = nv

    def shard_spec(w):
        return pl.BlockSpec((w.shape[0] // ADAM_STEPS, w.shape[1]), lambda i, idx_ref: (i, 0))

    in_specs = _piece_specs(stacks) + [shard_spec(w) for w in ws[n_p:]] + [shard_spec(w) for w in ws] * 3
    out_specs = [shard_spec(w) for w in ws for _ in range(4)]
    grid_spec = pltpu.PrefetchScalarGridSpec(num_scalar_prefetch=1, grid=(ADAM_STEPS,), in_specs=in_specs,
                                             out_specs=out_specs)
    return pl.pallas_call(
        body, name="adam_big", grid_spec=grid_spec,
        out_shape=[jax.ShapeDtypeStruct(w.shape, F32) for w in ws for _ in range(4)],
        compiler_params=_params(1),
    )(idx_arr, *stacks, *r1s, *r2s, *summed, *ws, *ms, *vs)


def _adam_small(small_all, w, m, v):
    def body(all_ref, w_ref, m_ref, v_ref, g_ref, d_ref, nm_ref, nv_ref):
        g = all_ref[0]
        for k in range(1, N_DEV):
            g = g + all_ref[k]
        delta, nm, nv = _adamw(w_ref[...], g, m_ref[...], v_ref[...])
        g_ref[...] = g
        d_ref[...] = delta
        nm_ref[...] = nm
        nv_ref[...] = nv

    return pl.pallas_call(
        body, name="adam_small",
        in_specs=[VMEM_FULL] * 4, out_specs=[VMEM_FULL] * 4,
        out_shape=[jax.ShapeDtypeStruct(w.shape, F32)] * 4,
    )(small_all, w, m, v)


RS_PAYLOAD = BF


def _pack_rows(rows):
    rid = lax.broadcasted_iota(jnp.int32, (8, D_MODEL), 0)
    out = jnp.zeros((8, D_MODEL), F32)
    for i, r in enumerate(rows):
        out = jnp.where(rid == i, jnp.broadcast_to(r, (8, D_MODEL)), out)
    return out


def _small_rows(norm_mix_gain, b_gates, attn_sinks, ret_gn_gain, norm_mlp_gain, norm_final_gain):
    return [norm_mix_gain, b_gates[:, :D_MODEL], b_gates[:, D_MODEL:], ret_gn_gain, norm_mlp_gain,
            norm_final_gain.reshape(1, D_MODEL), jnp.pad(attn_sinks, ((0, 0), (0, D_MODEL - ATT_HEADS)))]


def _unpack_small(p):
    return dict(norm_mix_gain=p[0:1], b_gates=jnp.concatenate([p[1:2], p[2:3]], axis=1), ret_gn_gain=p[3:4],
                norm_mlp_gain=p[4:5], norm_final_gain=p[5], attn_sinks=p[6:7, :ATT_HEADS])


WEIGHTS = ("norm_mix_gain", "w_in", "b_gates", "attn_sinks", "ret_gn_gain", "w_att_up", "w_ret_up", "w_out",
           "norm_mlp_gain", "w_ff1", "w_ff2", "norm_final_gain")
BIG = ("w_in", "w_att_up", "w_ret_up", "w_out", "w_ff1", "w_ff2")


def kernel(x, norm_mix_gain, w_in, b_gates, attn_sinks, ret_gn_gain, w_att_up, w_ret_up, w_out, norm_mlp_gain, w_ff1, w_ff2, norm_final_gain, loss_target, m_norm_mix_gain, m_w_in, m_b_gates, m_attn_sinks, m_ret_gn_gain, m_w_att_up, m_w_ret_up, m_w_out, m_norm_mlp_gain, m_w_ff1, m_w_ff2, m_norm_final_gain, v_norm_mix_gain, v_w_in, v_b_gates, v_attn_sinks, v_ret_gn_gain, v_w_att_up, v_w_ret_up, v_w_out, v_norm_mlp_gain, v_w_ff1, v_w_ff2, v_norm_final_gain):
    w = dict(norm_mix_gain=norm_mix_gain, w_in=w_in, b_gates=b_gates, attn_sinks=attn_sinks, ret_gn_gain=ret_gn_gain,
             w_att_up=w_att_up, w_ret_up=w_ret_up, w_out=w_out, norm_mlp_gain=norm_mlp_gain, w_ff1=w_ff1,
             w_ff2=w_ff2, norm_final_gain=norm_final_gain)
    m = dict(norm_mix_gain=m_norm_mix_gain, w_in=m_w_in, b_gates=m_b_gates, attn_sinks=m_attn_sinks,
             ret_gn_gain=m_ret_gn_gain, w_att_up=m_w_att_up, w_ret_up=m_w_ret_up, w_out=m_w_out,
             norm_mlp_gain=m_norm_mlp_gain, w_ff1=m_w_ff1, w_ff2=m_w_ff2, norm_final_gain=m_norm_final_gain)
    v = dict(norm_mix_gain=v_norm_mix_gain, w_in=v_w_in, b_gates=v_b_gates, attn_sinks=v_attn_sinks,
             ret_gn_gain=v_ret_gn_gain, w_att_up=v_w_att_up, w_ret_up=v_w_ret_up, w_out=v_w_out,
             norm_mlp_gain=v_norm_mlp_gain, w_ff1=v_w_ff1, w_ff2=v_w_ff2, norm_final_gain=v_norm_final_gain)

    shards = {k: (w[k][0].T if k in COLUMN_SHARDED else w[k][0]).astype(BF) for k in BIG}
    comm, unpack = _gather(shards, ("w_in",))
    (w_in_full,) = unpack(_comm_alone(comm, "allgather_w_in"))

    loss_p, dx, small, groups = _local_step(
        x[0], loss_target[0], norm_mix_gain, b_gates, attn_sinks[0], ret_gn_gain, norm_mlp_gain,
        norm_final_gain.reshape(1, D_MODEL), w_in_full, shards)

    by_name = {}
    for grp in groups:
        for k, stack, r1, r2 in zip(grp.names, grp.stacks, grp.r1s, grp.r2s):
            by_name[k] = (stack, r1, r2)
    lane = lax.broadcasted_iota(jnp.int32, (1, D_MODEL), 1)
    sink_row = jnp.where(lane < ATT_HEADS, jnp.pad(small["attn_sinks"], ((0, 0), (0, D_MODEL - LANE))),
                         jnp.where(lane == ATT_HEADS, jnp.pad(loss_p, ((0, 0), (0, D_MODEL - LANE)), mode="edge"),
                                   0.0))
    small_pack = _pack_rows([small["norm_mix_gain"], small["b_gates"][:, :D_MODEL], small["b_gates"][:, D_MODEL:],
                             small["ret_gn_gain"], small["norm_mlp_gain"], small["norm_final_gain"], sink_row])
    xi, yi, ci = _coords()
    (small_all,) = _comm_alone(_small_comm(small_pack), "small_exchange")
    idx_arr = jnp.stack([4 * xi + 2 * yi + ci, 2 * xi + yi]).astype(jnp.int32)
    in_pieces = ["w_att_up", "w_ret_up", "w_out", "w_ff2", "w_in"]
    in_sum = ["w_ff1"]
    order = in_pieces + in_sum
    summed_t = _shard_sum(*[[by_name[k][i] for k in in_sum] for i in range(3)], idx_arr)

    def shard(tree, k):
        return tree[k][0].T if k == "w_in" else tree[k][0]

    adam_out = _adam_big([[by_name[k][i] for k in in_pieces] for i in range(3)], [g.T for g in summed_t],
                         [shard(w, k) for k in order], [shard(m, k) for k in order], [shard(v, k) for k in order],
                         idx_arr)
    big_out = [adam_out[4 * order.index(k) + i].T if k == "w_in" else adam_out[4 * order.index(k) + i]
               for k in BIG for i in range(4)]
    sm_g, sm_d, sm_m, sm_v = _adam_small(small_all, _pack_rows(_small_rows(*[w[k] for k in WEIGHTS if k not in BIG])),
                                         _pack_rows(_small_rows(*[m[k] for k in WEIGHTS if k not in BIG])),
                                         _pack_rows(_small_rows(*[v[k] for k in WEIGHTS if k not in BIG])))

    loss = sm_g[6, ATT_HEADS]
    kinds = []
    for idx, packed in enumerate((sm_g, sm_d, sm_m, sm_v)):
        out = _unpack_small(packed)
        for a, k in enumerate(BIG):
            out[k] = big_out[4 * a + idx][None]
        kinds.append(out)
    return (loss, dx[None], *[kinds[0][k] for k in WEIGHTS], *[kinds[1][k] for k in WEIGHTS],
            *[kinds[2][k] for k in WEIGHTS], *[kinds[3][k] for k in WEIGHTS])
```

```python
import functools
import math

import jax
import jax.numpy as jnp
from jax import lax
from jax.experimental import pallas as pl
from jax.experimental.pallas import tpu as pltpu

F32 = jnp.float32
BF = jnp.bfloat16
MESH = pl.DeviceIdType.MESH

D_MODEL = 1024
ATT_HEADS = 8
ATT_HEAD_DIM = 64
ATT_BLOCK = 128
ROPE_DIM = 16
ROPE_THETA = 500000.0
RET_HEADS = 4
RET_KEY_DIM = 128
RET_VAL_DIM = 256
RET_CHUNK = 128
RET_ROT_BASE = 10000.0
D_FF = 4096
NORM_EPS = 1e-6
GN_EPS = 1e-6
NEG_INF = -1e30
ATT_SCALE = ATT_HEAD_DIM ** -0.5
RET_SCALE = RET_KEY_DIM ** -0.5

C_QA, C_KA, C_VA, C_QR, C_KR, C_VR, C_GR, C_GATES, C_END = 0, 512, 640, 768, 1280, 1792, 2816, 3840, 5888

ADAM_LR = 0.001
ADAM_B1 = 0.9
ADAM_B2 = 0.999
ADAM_EPS = 1e-08
ADAM_WD = 0.01
ADAM_STEP = 10

N_DEV = 8
LANE = 128
VMEM_LIMIT = 56 * 1024 * 1024
TOKEN_TILE = 256
TN_TOKEN_TILE = 1024
FF_CHUNKS = (0, 1024, 2048, 3072, 4096)
IN_CHUNKS_FWD = (C_QA, C_KA, C_QR, C_KR, C_VR, C_GR, C_GATES, C_END)
IN_CHUNKS_BWD = (C_QA, C_KA, C_QR, C_GATES, C_END)

RET_LOG_GAMMA = tuple(math.log1p(-(2.0 ** (-5.0 - h))) for h in range(RET_HEADS))
RET_CHUNK_DECAY = tuple(math.exp(RET_CHUNK * lg) for lg in RET_LOG_GAMMA)

VMEM_FULL = pl.BlockSpec(memory_space=pltpu.VMEM)
SMEM_FULL = pl.BlockSpec(memory_space=pltpu.SMEM)
ANY = pl.BlockSpec(memory_space=pl.ANY)


def _params(n_axes):
    return pltpu.CompilerParams(dimension_semantics=("arbitrary",) * n_axes, vmem_limit_bytes=VMEM_LIMIT)


def _nn(a, b):
    return jnp.dot(a, b, preferred_element_type=F32)


def _nt(a, b):
    return lax.dot_general(a, b, (((1,), (1,)), ((), ())), preferred_element_type=F32)


def _tn(a, b):
    return lax.dot_general(a, b, (((0,), (0,)), ((), ())), preferred_element_type=F32)


def _sigmoid(v):
    return 1.0 / (1.0 + jnp.exp(-v))


def _rows(tile, width):
    return pl.BlockSpec((tile, width), lambda i: (i, 0))


def _cols(height, tile):
    return pl.BlockSpec((height, tile), lambda i: (0, i))


class _Resident:
    def __init__(self, hbm_ref, vmem_ref, sems, bounds):
        self.hbm, self.vmem, self.sems, self.bounds = hbm_ref, vmem_ref, sems, bounds

    @staticmethod
    def scratch(w, bounds):
        return [pltpu.VMEM(w.shape, w.dtype), pltpu.SemaphoreType.DMA((len(bounds) - 1,))]

    @staticmethod
    def load(*weights):
        @pl.when(pl.program_id(0) == 0)
        def _():
            copies = [w._copy(c) for w in weights for c in range(len(w.bounds) - 1)]
            for cp in copies:
                cp.start()
            for cp in copies:
                cp.wait()

    def _rows(self, c):
        return pl.ds(self.bounds[c], self.bounds[c + 1] - self.bounds[c])

    def _copy(self, c):
        return pltpu.make_async_copy(self.hbm.at[self._rows(c)], self.vmem.at[self._rows(c)], self.sems.at[c])

    def chunk(self, c):
        return self.vmem.at[self._rows(c)]


class _Comm:
    def __init__(self, inputs, out_shapes, scratch, start, finish):
        self.inputs, self.out_shapes, self.scratch, self.start, self.finish = inputs, out_shapes, scratch, start, finish


def _join(a, b):
    na_in, na_out, na_sem = len(a.inputs), len(a.out_shapes), len(a.scratch)

    def split(ins, outs, sems):
        return (ins[:na_in], outs[:na_out], sems[:na_sem]), (ins[na_in:], outs[na_out:], sems[na_sem:])

    def start(ins, outs, sems):
        pa, pb = split(ins, outs, sems)
        a.start(*pa)
        b.start(*pb)

    def finish(ins, outs, sems):
        pa, pb = split(ins, outs, sems)
        a.finish(*pa)
        b.finish(*pb)

    return _Comm(list(a.inputs) + list(b.inputs), list(a.out_shapes) + list(b.out_shapes),
                 list(a.scratch) + list(b.scratch), start, finish)


def _call(body, args, comm=None, *, name, grid, in_specs, out_specs, out_shape, scratch_shapes=()):
    params = _params(len(grid))
    if comm is None:
        return pl.pallas_call(body, name=name, grid=grid, in_specs=in_specs, out_specs=out_specs, out_shape=out_shape,
                              scratch_shapes=scratch_shapes, compiler_params=params)(*args), ()
    single = not isinstance(out_specs, (list, tuple))
    out_specs_l = [out_specs] if single else list(out_specs)
    out_shape_l = [out_shape] if single else list(out_shape)
    n_in, n_out, n_scr = len(in_specs), len(out_specs_l), len(scratch_shapes)
    n_cin, n_cout = len(comm.inputs), len(comm.out_shapes)

    def hosted(*refs):
        it = iter(refs)
        ins, cin, outs, cout, scr = ([next(it) for _ in range(k)] for k in (n_in, n_cin, n_out, n_cout, n_scr))
        sems = list(it)
        ids = [pl.program_id(k) for k in range(len(grid))]
        first = functools.reduce(jnp.logical_and, [i == 0 for i in ids])
        last = functools.reduce(jnp.logical_and, [i == g - 1 for i, g in zip(ids, grid)])

        @pl.when(first)
        def _():
            comm.start(cin, cout, sems)

        body(*ins, *outs, *scr)

        @pl.when(last)
        def _():
            comm.finish(cin, cout, sems)

    res = pl.pallas_call(
        hosted, name=name, grid=grid, in_specs=list(in_specs) + [ANY] * n_cin,
        out_specs=out_specs_l + [ANY] * n_cout, out_shape=out_shape_l + list(comm.out_shapes),
        scratch_shapes=list(scratch_shapes) + list(comm.scratch), compiler_params=params)(*args, *comm.inputs)
    return (res[0] if single else res[:n_out]), res[n_out:]


def _comm_alone(comm, name):
    n_cin, n_cout = len(comm.inputs), len(comm.out_shapes)

    def body(*refs):
        cin, cout, sems = refs[:n_cin], refs[n_cin:n_cin + n_cout], refs[n_cin + n_cout:]
        comm.start(cin, cout, sems)
        comm.finish(cin, cout, sems)

    return pl.pallas_call(body, name=name, in_specs=[ANY] * n_cin, out_specs=[ANY] * n_cout,
                          out_shape=list(comm.out_shapes), scratch_shapes=list(comm.scratch))(*comm.inputs)


def _slabs(v, fn):
    return jnp.concatenate([fn(v[:, LANE * j:LANE * (j + 1)]) for j in range(v.shape[1] // LANE)], axis=1)


def _rope_att(v, ca, sa, sb):
    return _slabs(v, lambda t: t * ca + pltpu.roll(t, LANE - 8, 1) * sa + pltpu.roll(t, 8, 1) * sb)


def _rope_att_t(v, ca, sa, sb):
    return _slabs(v, lambda t: t * ca + pltpu.roll(t * sa, 8, 1) + pltpu.roll(t * sb, LANE - 8, 1))


def _rope_att_rows(v, ct, st, sign):
    parts = []
    for h in range(v.shape[0] // ATT_HEAD_DIM):
        r0 = ATT_HEAD_DIM * h
        x1, x2 = v[r0:r0 + 8], v[r0 + 8:r0 + 16]
        parts += [x1 * ct - sign * (x2 * st), x2 * ct + sign * (x1 * st), v[r0 + 16:r0 + ATT_HEAD_DIM]]
    return jnp.concatenate(parts, axis=0)


def _rope_ret(v, cr, sr):
    return _slabs(v, lambda t: t * cr + pltpu.roll(t, 64, 1) * sr)


def _rope_ret_t(v, cr, sr):
    return _slabs(v, lambda t: t * cr + pltpu.roll(t * sr, 64, 1))


def _rope_tables(seq):
    pos = jnp.arange(seq, dtype=F32)

    def cs(dim, theta):
        inv = theta ** (-jnp.arange(0, dim, 2, dtype=F32) / dim)
        ang = pos[:, None] * inv[None, :]
        return jnp.cos(ang), jnp.sin(ang)

    ca, sa_ = cs(ROPE_DIM, ROPE_THETA)
    zeros8 = jnp.zeros_like(ca)
    rest = ATT_HEAD_DIM - ROPE_DIM
    c64 = jnp.concatenate([ca, ca, jnp.ones((seq, rest), F32)], axis=1)
    sa64 = jnp.concatenate([-sa_, zeros8, jnp.zeros((seq, rest), F32)], axis=1)
    sb64 = jnp.concatenate([zeros8, sa_, jnp.zeros((seq, rest), F32)], axis=1)
    cr, sr_ = cs(RET_KEY_DIM, RET_ROT_BASE)
    return (jnp.tile(c64, (1, 2)), jnp.tile(sa64, (1, 2)), jnp.tile(sb64, (1, 2)),
            jnp.concatenate([cr, cr], axis=1), jnp.concatenate([-sr_, sr_], axis=1), ca.T, sa_.T)


def _ret_tables():
    c = RET_CHUNK
    lg = jnp.asarray(RET_LOG_GAMMA, F32)
    idx = jnp.arange(c, dtype=F32)
    diff = idx[:, None] - idx[None, :]
    intra = jnp.where(diff >= 0, jnp.exp(jnp.maximum(diff, 0.0) * lg[:, None, None]), 0.0)
    qd = jnp.exp((idx + 1.0)[None, :] * lg[:, None])[..., None]
    kd = jnp.exp((c - 1.0 - idx)[None, :] * lg[:, None])[..., None]
    return intra, jnp.broadcast_to(qd, (RET_HEADS, c, RET_KEY_DIM)), jnp.broadcast_to(kd, (RET_HEADS, c, RET_KEY_DIM))


def _inproj_fwd(x, g1, w_in, tabs, comm=None):
    seq = x.shape[0]
    tm = min(TOKEN_TILE, seq)

    def body(x_ref, g_ref, w_hbm, ca_ref, sa_ref, sb_ref, cr_ref, sr_ref, ct_ref, st_ref,
             xn_ref, qt_ref, kv_ref, kvt_ref, qkr_ref, vr_ref, gr_ref, gates_ref, w_vmem, w_sems):
        w = _Resident(w_hbm, w_vmem, w_sems, IN_CHUNKS_FWD)
        _Resident.load(w)
        xf = x_ref[...]
        r = lax.rsqrt(jnp.mean(xf * xf, axis=-1, keepdims=True) + NORM_EPS)
        xn = (xf * r * g_ref[...]).astype(BF)
        xn_ref[...] = xn
        ct, st = ct_ref[...], st_ref[...]
        qt_ref[...] = _rope_att_rows(_nt(w.chunk(0)[...], xn), ct, st, 1.0).astype(BF)
        w_kv = w.chunk(1)
        kvt = _nt(w_kv[...], xn)
        kvt_ref[:LANE, :] = _rope_att_rows(kvt[:LANE], ct, st, 1.0).astype(BF)
        kvt_ref[LANE:, :] = kvt[LANE:].astype(BF)
        kvn = _nt(xn, w_kv[...])
        kv_ref[:, :LANE] = _rope_att(kvn[:, :LANE], ca_ref[...], sa_ref[...], sb_ref[...]).astype(BF)
        kv_ref[:, LANE:] = kvn[:, LANE:].astype(BF)
        cr, sr = cr_ref[...], sr_ref[...]
        qr = _rope_ret(_nt(xn, w.chunk(2)[...]), cr, sr)
        qkr_ref[:, :512] = qr.astype(BF)
        kr = _rope_ret(_nt(xn, w.chunk(3)[...]), cr, sr) * RET_SCALE
        qkr_ref[:, 512:] = kr.astype(BF)
        vr_ref[...] = _nt(xn, w.chunk(4)[...]).astype(BF)
        gr_ref[...] = _nt(xn, w.chunk(5)[...]).astype(BF)
        gates_ref[...] = _nt(xn, w.chunk(6)[...]).astype(BF)

    tab = _rows(tm, LANE)
    tab_t = _cols(8, tm)
    return _call(
        body, (x, g1, w_in, *tabs), comm, name="inproj_fwd", grid=(seq // tm,),
        in_specs=[_rows(tm, D_MODEL), VMEM_FULL, ANY, tab, tab, tab, tab, tab, tab_t, tab_t],
        out_specs=[_rows(tm, D_MODEL), _cols(512, tm), _rows(tm, 256), _cols(256, tm), _rows(tm, 1024),
                   _rows(tm, 1024), _rows(tm, 1024), _rows(tm, 2048)],
        out_shape=[jax.ShapeDtypeStruct(s, BF) for s in ((seq, D_MODEL), (512, seq), (seq, 256), (256, seq),
                                                         (seq, 1024), (seq, 1024), (seq, 1024), (seq, 2048))],
        scratch_shapes=_Resident.scratch(w_in, IN_CHUNKS_FWD))


ATT_Q_PER_KV = ATT_HEADS // 2


def _att_group(ref, kv):
    d = ATT_HEAD_DIM
    g = jnp.concatenate([ref[d * (ATT_Q_PER_KV * kv + j):d * (ATT_Q_PER_KV * kv + j + 1), :]
                         for j in range(ATT_Q_PER_KV)], axis=1)
    z = jnp.zeros_like(g)
    return jnp.concatenate([g, z] if kv == 0 else [z, g], axis=0)


def _att_probs(kwin, qpad, sink_ref, kv, i):
    c = ATT_BLOCK
    kj = lax.broadcasted_iota(jnp.int32, (2 * c, c), 0)
    qi = lax.broadcasted_iota(jnp.int32, (2 * c, c), 1)
    allowed = (kj > qi) & (kj <= qi + c) & ((kj >= c) | (i > 0))
    allowed = jnp.concatenate([allowed] * ATT_Q_PER_KV, axis=1)
    s = jnp.where(allowed, _nn(kwin, qpad) * ATT_SCALE, NEG_INF)
    sk = jnp.concatenate([jnp.full((1, c), sink_ref[ATT_Q_PER_KV * kv + j], F32) for j in range(ATT_Q_PER_KV)], axis=1)
    m = jnp.maximum(jnp.max(s, axis=0, keepdims=True), sk)
    pe = jnp.exp(s - m)
    psink = jnp.exp(sk - m)
    inv = 1.0 / (jnp.sum(pe, axis=0, keepdims=True) + psink)
    return pe * inv, psink * inv


def _attn_fwd(qt, kv, kvt, sinks, comm=None):
    seq = kv.shape[0]
    c = ATT_BLOCK
    d = ATT_HEAD_DIM
    nb = seq // c

    def body(sink_ref, qt_ref, kvc_ref, kvp_ref, kvtc_ref, kvtp_ref, ot_ref):
        i = pl.program_id(0)
        kwin = jnp.concatenate([kvp_ref[:, :LANE], kvc_ref[:, :LANE]], axis=0)
        vtwin = jnp.concatenate([kvtp_ref[LANE:, :], kvtc_ref[LANE:, :]], axis=1)
        for g in (0, 1):
            prob, _ = _att_probs(kwin, _att_group(qt_ref, g), sink_ref, g, i)
            out = _nn(vtwin[d * g:d * (g + 1), :], prob.astype(BF))
            for j in range(ATT_Q_PER_KV):
                h = ATT_Q_PER_KV * g + j
                ot_ref[d * h:d * (h + 1), :] = out[:, c * j:c * (j + 1)].astype(BF)

    prev = lambda i: jnp.maximum(i - 1, 0)
    return _call(
        body, (sinks, qt, kv, kv, kvt, kvt), comm, name="attn_fwd", grid=(nb,),
        in_specs=[SMEM_FULL, _cols(512, c), _rows(c, 256), pl.BlockSpec((c, 256), lambda i: (prev(i), 0)),
                  _cols(256, c), pl.BlockSpec((256, c), lambda i: (0, prev(i)))],
        out_specs=_cols(512, c),
        out_shape=jax.ShapeDtypeStruct((512, seq), BF))


def _ret_fwd(qkr, vr, gr, gain, rtabs, comm=None):
    seq = qkr.shape[0]
    c = RET_CHUNK
    nc = seq // c
    dk, dv = RET_KEY_DIM, RET_VAL_DIM

    def body(qk_ref, v_ref, g_ref, gain_ref, intra_ref, qd_ref, kd_ref, ret_ref, yrp_ref, st_ref, state):
        @pl.when(pl.program_id(0) == 0)
        def _():
            state[...] = jnp.zeros_like(state)

        for h in range(RET_HEADS):
            qh = qk_ref[:, dk * h:dk * (h + 1)]
            kh = qk_ref[:, 512 + dk * h:512 + dk * (h + 1)]
            vh = v_ref[:, dv * h:dv * (h + 1)]
            sh = state[h]
            shb = sh.astype(BF)
            st_ref[0, h] = shb
            att = _nt(qh, kh) * intra_ref[h]
            inner = _nn(att.astype(BF), vh)
            cross = _nn((qh.astype(F32) * qd_ref[h]).astype(BF), shb)
            out = inner + cross
            state[h] = sh * RET_CHUNK_DECAY[h] + _tn((kh.astype(F32) * kd_ref[h]).astype(BF), vh)
            ret_ref[:, dv * h:dv * (h + 1)] = out
            mu = jnp.mean(out, axis=-1, keepdims=True)
            d = out - mu
            var = jnp.mean(d * d, axis=-1, keepdims=True)
            y = d * lax.rsqrt(var + GN_EPS) * gain_ref[:, dv * h:dv * (h + 1)]
            g = g_ref[:, dv * h:dv * (h + 1)].astype(F32)
            yrp_ref[:, dv * h:dv * (h + 1)] = (g * _sigmoid(g) * y).astype(BF)

    return _call(
        body, (qkr, vr, gr, gain, *rtabs), comm, name="ret_fwd", grid=(nc,),
        in_specs=[_rows(c, 1024), _rows(c, 1024), _rows(c, 1024), VMEM_FULL, VMEM_FULL, VMEM_FULL, VMEM_FULL],
        out_specs=[_rows(c, 1024), _rows(c, 1024), pl.BlockSpec((1, RET_HEADS, dk, dv), lambda i: (i, 0, 0, 0))],
        out_shape=[jax.ShapeDtypeStruct((seq, 1024), F32), jax.ShapeDtypeStruct((seq, 1024), BF),
                   jax.ShapeDtypeStruct((nc, RET_HEADS, dk, dv), BF)],
        scratch_shapes=[pltpu.VMEM((RET_HEADS, dk, dv), F32)])


def _mix_fwd(oat, yrp, gates, bg, x, w_att, w_ret, w_out):
    seq = x.shape[0]
    tm = min(TOKEN_TILE, seq)

    def body(oat_ref, yrp_ref, gates_ref, bg_ref, x_ref, wa_ref, wr_ref, wo_ref, ya_ref, yr_ref, mg_ref, h1_ref):
        ya = _tn(oat_ref[...], wa_ref[...])
        yr = _nn(yrp_ref[...], wr_ref[...])
        gt = _sigmoid(gates_ref[...].astype(F32) + bg_ref[...])
        merged = (gt[:, :D_MODEL] * ya + gt[:, D_MODEL:] * yr).astype(BF)
        ya_ref[...] = ya.astype(BF)
        yr_ref[...] = yr.astype(BF)
        mg_ref[...] = merged
        h1_ref[...] = x_ref[...] + _nn(merged, wo_ref[...])

    return pl.pallas_call(
        body, name="mix_fwd", grid=(seq // tm,),
        in_specs=[_cols(512, tm), _rows(tm, 1024), _rows(tm, 2048), VMEM_FULL, _rows(tm, D_MODEL),
                  VMEM_FULL, VMEM_FULL, VMEM_FULL],
        out_specs=[_rows(tm, D_MODEL)] * 4,
        out_shape=[jax.ShapeDtypeStruct((seq, D_MODEL), BF)] * 3 + [jax.ShapeDtypeStruct((seq, D_MODEL), F32)],
        compiler_params=_params(1),
    )(oat, yrp, gates, bg, x, w_att, w_ret, w_out)


def _mlp_fwd_loss(h1, g2, g3, target, w_ff1, w_ff2):
    seq = h1.shape[0]
    tm = min(TOKEN_TILE, seq)

    def body(h1_ref, g2_ref, g3_ref, t_ref, w1_hbm, w2_hbm, xn2_ref, u_ref, dh2_ref, loss_ref, dg3_ref,
             w1_vmem, w1_sems, w2_vmem, w2_sems):
        w1 = _Resident(w1_hbm, w1_vmem, w1_sems, FF_CHUNKS)
        w2 = _Resident(w2_hbm, w2_vmem, w2_sems, FF_CHUNKS)
        _Resident.load(w1, w2)

        @pl.when(pl.program_id(0) == 0)
        def _():
            loss_ref[...] = jnp.zeros_like(loss_ref)
            dg3_ref[...] = jnp.zeros_like(dg3_ref)

        h1v = h1_ref[...]
        r2 = lax.rsqrt(jnp.mean(h1v * h1v, axis=-1, keepdims=True) + NORM_EPS)
        xn2 = (h1v * r2 * g2_ref[...]).astype(BF)
        xn2_ref[...] = xn2
        h2 = h1v
        for c in range(len(FF_CHUNKS) - 1):
            cols = slice(FF_CHUNKS[c], FF_CHUNKS[c + 1])
            u = _nt(xn2, w1.chunk(c)[...])
            u_ref[:, cols] = u.astype(BF)
            h2 = h2 + _nn(jnp.square(jnp.maximum(u, 0.0)).astype(BF), w2.chunk(c)[...])
        r3 = lax.rsqrt(jnp.mean(h2 * h2, axis=-1, keepdims=True) + NORM_EPS)
        hn = h2 * r3
        err = hn * g3_ref[...] - t_ref[...]
        loss_ref[...] += jnp.sum(err * err) * (0.5 / D_MODEL)
        dy = err * (1.0 / D_MODEL)
        dg3_ref[...] += jnp.sum(dy * hn, axis=0, keepdims=True)
        z = dy * g3_ref[...]
        dh2_ref[...] = r3 * (z - hn * jnp.mean(z * hn, axis=-1, keepdims=True))

    return pl.pallas_call(
        body, name="mlp_fwd_loss", grid=(seq // tm,),
        in_specs=[_rows(tm, D_MODEL), VMEM_FULL, VMEM_FULL, _rows(tm, D_MODEL), ANY, ANY],
        out_specs=[_rows(tm, D_MODEL), _rows(tm, D_FF), _rows(tm, D_MODEL),
                   pl.BlockSpec((1, LANE), lambda i: (0, 0)), pl.BlockSpec((1, D_MODEL), lambda i: (0, 0))],
        out_shape=[jax.ShapeDtypeStruct((seq, D_MODEL), BF), jax.ShapeDtypeStruct((seq, D_FF), BF),
                   jax.ShapeDtypeStruct((seq, D_MODEL), F32), jax.ShapeDtypeStruct((1, LANE), F32),
                   jax.ShapeDtypeStruct((1, D_MODEL), F32)],
        scratch_shapes=_Resident.scratch(w_ff1, FF_CHUNKS) + _Resident.scratch(w_ff2, FF_CHUNKS),
        compiler_params=_params(1),
    )(h1, g2, g3, target, w_ff1, w_ff2)


def _rms_bwd(dxn, xin, gain):
    r = lax.rsqrt(jnp.mean(xin * xin, axis=-1, keepdims=True) + NORM_EPS)
    xhat = xin * r
    z = dxn * gain
    dxin = r * (z - xhat * jnp.mean(z * xhat, axis=-1, keepdims=True))
    return dxin, jnp.sum(dxn * xhat, axis=0, keepdims=True)


def _mlp_bwd(dh2, u, h1, g2, w_ff1, w_ff2):
    seq = h1.shape[0]
    tm = min(TOKEN_TILE, seq)

    def body(dh2_ref, u_ref, h1_ref, g2_ref, w1_hbm, w2_hbm, du_ref, dh1_ref, dg2_ref,
             w1_vmem, w1_sems, w2_vmem, w2_sems):
        w2 = _Resident(w2_hbm, w2_vmem, w2_sems, FF_CHUNKS)
        w1 = _Resident(w1_hbm, w1_vmem, w1_sems, FF_CHUNKS)
        _Resident.load(w2, w1)

        @pl.when(pl.program_id(0) == 0)
        def _():
            dg2_ref[...] = jnp.zeros_like(dg2_ref)

        dh2v = dh2_ref[...]
        dh2b = dh2v.astype(BF)
        dxn2 = jnp.zeros_like(dh2v)
        for c in range(len(FF_CHUNKS) - 1):
            cols = slice(FF_CHUNKS[c], FF_CHUNKS[c + 1])
            dhdn = _nt(dh2b, w2.chunk(c)[...])
            du = (dhdn * (2.0 * jnp.maximum(u_ref[:, cols].astype(F32), 0.0))).astype(BF)
            du_ref[:, cols] = du
            dxn2 = dxn2 + _nn(du, w1.chunk(c)[...])
        dnorm, dg = _rms_bwd(dxn2, h1_ref[...], g2_ref[...])
        dh1_ref[...] = dh2v + dnorm
        dg2_ref[...] += dg

    return pl.pallas_call(
        body, name="mlp_bwd", grid=(seq // tm,),
        in_specs=[_rows(tm, D_MODEL), _rows(tm, D_FF), _rows(tm, D_MODEL), VMEM_FULL, ANY, ANY],
        out_specs=[_rows(tm, D_FF), _rows(tm, D_MODEL), pl.BlockSpec((1, D_MODEL), lambda i: (0, 0))],
        out_shape=[jax.ShapeDtypeStruct((seq, D_FF), BF), jax.ShapeDtypeStruct((seq, D_MODEL), F32),
                   jax.ShapeDtypeStruct((1, D_MODEL), F32)],
        scratch_shapes=_Resident.scratch(w_ff1, FF_CHUNKS) + _Resident.scratch(w_ff2, FF_CHUNKS),
        compiler_params=_params(1),
    )(dh2, u, h1, g2, w_ff1, w_ff2)


def _mm_tn(a, b, name, relu_sq=False, a_is_t=False, comm=None):
    kdim, seq = a.shape if a_is_t else a.shape[::-1]
    ndim = b.shape[1]
    ts = min(TN_TOKEN_TILE, seq)
    tk = min(kdim, 1024)
    tn = min(ndim, 1024)
    n_steps = seq // ts

    def body(a_ref, b_ref, o_ref):
        @pl.when(pl.program_id(2) == 0)
        def _():
            o_ref[...] = jnp.zeros_like(o_ref)

        av = a_ref[...]
        if relu_sq:
            av = jnp.square(jnp.maximum(av.astype(F32), 0.0))
        mm = _nn if a_is_t else _tn
        o_ref[...] += mm(av.astype(BF), b_ref[...].astype(BF))

    a_spec = (pl.BlockSpec((tk, ts), lambda k, n, s: (k, s)) if a_is_t
              else pl.BlockSpec((ts, tk), lambda k, n, s: (s, k)))
    res = _call(
        body, (a, b), comm, name=name, grid=(kdim // tk, ndim // tn, n_steps),
        in_specs=[a_spec, pl.BlockSpec((ts, tn), lambda k, n, s: (s, n))],
        out_specs=pl.BlockSpec((tk, tn), lambda k, n, s: (k, n)),
        out_shape=jax.ShapeDtypeStruct((kdim, ndim), F32))
    return res[0] if comm is None else res


def _dw_in_t(pieces, is_t, xn1, comm=None):
    seq = xn1.shape[0]
    ts = min(2 * TN_TOKEN_TILE, seq)
    tr = 256
    n_blk = [(p.shape[0] if t else p.shape[1]) // tr for p, t in zip(pieces, is_t)]
    offs = [sum(n_blk[:i]) for i in range(len(pieces) + 1)]
    n_p = len(pieces)

    def body(*refs):
        piece_refs, xn_ref, o_ref = refs[:n_p], refs[n_p], refs[n_p + 1]
        r, s = pl.program_id(0), pl.program_id(1)

        @pl.when(s == 0)
        def _():
            o_ref[...] = jnp.zeros_like(o_ref)

        for p in range(n_p):
            @pl.when((r >= offs[p]) & (r < offs[p + 1]))
            def _(p=p):
                mm = _nn if is_t[p] else _tn
                o_ref[...] += mm(piece_refs[p][...], xn_ref[pl.ds(pl.multiple_of(s * ts, ts), ts), :])

    def piece_spec(p):
        def index(r, s):
            inside = (r >= offs[p]) & (r < offs[p + 1])
            tok, blk = jnp.where(inside, s, 0), jnp.clip(r - offs[p], 0, n_blk[p] - 1)
            return (blk, tok) if is_t[p] else (tok, blk)
        return pl.BlockSpec((tr, ts) if is_t[p] else (ts, tr), index)

    return _call(
        body, (*pieces, xn1), comm, name="dw_in", grid=(offs[-1], seq // ts),
        in_specs=[piece_spec(p) for p in range(n_p)] + [VMEM_FULL],
        out_specs=pl.BlockSpec((tr, D_MODEL), lambda r, s: (r, 0)),
        out_shape=jax.ShapeDtypeStruct((offs[-1] * tr, D_MODEL), F32))


def _mix_bwd(dh1, ya, yr, gates, bg, w_att, w_ret, w_out, comm=None):
    seq = dh1.shape[0]
    tm = min(TOKEN_TILE, seq)

    def body(dh1_ref, ya_ref, yr_ref, gates_ref, bg_ref, wa_ref, wr_ref, wo_ref,
             dya_ref, dyr_ref, dgates_ref, doa_ref, dyrp_ref, db_ref):
        @pl.when(pl.program_id(0) == 0)
        def _():
            db_ref[...] = jnp.zeros_like(db_ref)

        dm = _nt(dh1_ref[...].astype(BF), wo_ref[...])
        gt = _sigmoid(gates_ref[...].astype(F32) + bg_ref[...])
        ga, gr = gt[:, :D_MODEL], gt[:, D_MODEL:]
        dya = (dm * ga).astype(BF)
        dyr = (dm * gr).astype(BF)
        dya_ref[...] = dya
        dyr_ref[...] = dyr
        dga = dm * ya_ref[...].astype(F32) * ga * (1.0 - ga)
        dgr = dm * yr_ref[...].astype(F32) * gr * (1.0 - gr)
        dgates_ref[:, :D_MODEL] = dga.astype(BF)
        dgates_ref[:, D_MODEL:] = dgr.astype(BF)
        db_ref[:, :D_MODEL] += jnp.sum(dga, axis=0, keepdims=True)
        db_ref[:, D_MODEL:] += jnp.sum(dgr, axis=0, keepdims=True)
        doa_ref[...] = _nt(wa_ref[...], dya).astype(BF)
        dyrp_ref[...] = _nt(dyr, wr_ref[...]).astype(BF)

    return _call(
        body, (dh1, ya, yr, gates, bg, w_att, w_ret, w_out), comm, name="mix_bwd", grid=(seq // tm,),
        in_specs=[_rows(tm, D_MODEL), _rows(tm, D_MODEL), _rows(tm, D_MODEL), _rows(tm, 2048), VMEM_FULL,
                  VMEM_FULL, VMEM_FULL, VMEM_FULL],
        out_specs=[_rows(tm, D_MODEL), _rows(tm, D_MODEL), _rows(tm, 2048), _cols(512, tm), _rows(tm, 1024),
                   pl.BlockSpec((1, 2048), lambda i: (0, 0))],
        out_shape=[jax.ShapeDtypeStruct((seq, D_MODEL), BF), jax.ShapeDtypeStruct((seq, D_MODEL), BF),
                   jax.ShapeDtypeStruct((seq, 2048), BF), jax.ShapeDtypeStruct((512, seq), BF),
                   jax.ShapeDtypeStruct((seq, 1024), BF), jax.ShapeDtypeStruct((1, 2048), F32)])


def _ret_bwd(qkr, vr, gr, ret, dyrp, states, gain, rtabs, cr, sr, comm=None):
    seq = qkr.shape[0]
    c = RET_CHUNK
    nc = seq // c
    dk, dv = RET_KEY_DIM, RET_VAL_DIM

    def body(qk_ref, v_ref, g_ref, ret_ref, dyp_ref, st_ref, gain_ref, intra_ref, qd_ref, kd_ref, cr_ref, sr_ref,
             dall_ref, dgain_ref, dstate):
        @pl.when(pl.program_id(0) == 0)
        def _():
            dstate[...] = jnp.zeros_like(dstate)
            dgain_ref[...] = jnp.zeros_like(dgain_ref)

        crv, srv = cr_ref[...], sr_ref[...]
        for h in range(RET_HEADS):
            vs = slice(dv * h, dv * (h + 1))
            qh = qk_ref[:, dk * h:dk * (h + 1)]
            kh = qk_ref[:, 512 + dk * h:512 + dk * (h + 1)]
            vh = v_ref[:, vs]
            out = ret_ref[:, vs]
            g = g_ref[:, vs].astype(F32)
            dyp = dyp_ref[:, vs].astype(F32)
            gain_h = gain_ref[:, vs]
            mu = jnp.mean(out, axis=-1, keepdims=True)
            d = out - mu
            rstd = lax.rsqrt(jnp.mean(d * d, axis=-1, keepdims=True) + GN_EPS)
            yn = d * rstd
            sg = _sigmoid(g)
            dg = dyp * (yn * gain_h) * (sg * (1.0 + g * (1.0 - sg)))
            dy = dyp * (g * sg)
            dgain_ref[:, vs] += jnp.sum(dy * yn, axis=0, keepdims=True)
            dyn = dy * gain_h
            dout = rstd * (dyn - jnp.mean(dyn, axis=-1, keepdims=True)
                           - yn * jnp.mean(dyn * yn, axis=-1, keepdims=True))
            doutb = dout.astype(BF)
            sc = st_ref[0, h]
            dsp = dstate[h]
            dspb = dsp.astype(BF)
            intra, qdv, kdv = intra_ref[h], qd_ref[h], kd_ref[h]
            att = _nt(qh, kh) * intra
            dab = (_nt(doutb, vh) * intra).astype(BF)
            qdec = (qh.astype(F32) * qdv).astype(BF)
            kdec = (kh.astype(F32) * kdv).astype(BF)
            dq = _nn(dab, kh) + _nt(doutb, sc) * qdv
            dkk = _tn(dab, qh) + _nt(vh, dspb) * kdv
            dvv = _tn(att.astype(BF), doutb) + _nn(kdec, dspb)
            dstate[h] = dsp * RET_CHUNK_DECAY[h] + _tn(qdec, doutb)
            dall_ref[:, dk * h:dk * (h + 1)] = _rope_ret_t(dq, crv, srv).astype(BF)
            dall_ref[:, 512 + dk * h:512 + dk * (h + 1)] = (_rope_ret_t(dkk, crv, srv) * RET_SCALE).astype(BF)
            dall_ref[:, 1024 + dv * h:1024 + dv * (h + 1)] = dvv.astype(BF)
            dall_ref[:, 2048 + dv * h:2048 + dv * (h + 1)] = dg.astype(BF)

    def rev(width):
        return pl.BlockSpec((c, width), lambda i: (nc - 1 - i, 0))

    return _call(
        body, (qkr, vr, gr, ret, dyrp, states, gain, *rtabs, cr, sr), comm, name="ret_bwd", grid=(nc,),
        in_specs=[rev(1024), rev(1024), rev(1024), rev(1024), rev(1024),
                  pl.BlockSpec((1, RET_HEADS, dk, dv), lambda i: (nc - 1 - i, 0, 0, 0)),
                  VMEM_FULL, VMEM_FULL, VMEM_FULL, VMEM_FULL, rev(LANE), rev(LANE)],
        out_specs=[rev(3072), pl.BlockSpec((1, 1024), lambda i: (0, 0))],
        out_shape=[jax.ShapeDtypeStruct((seq, 3072), BF), jax.ShapeDtypeStruct((1, 1024), F32)],
        scratch_shapes=[pltpu.VMEM((RET_HEADS, dk, dv), F32)])


def _attn_bwd(qt, kv, kvt, dot, sinks, ca, sa, sb, ct, st, comm=None):
    seq = kv.shape[0]
    c = ATT_BLOCK
    d = ATT_HEAD_DIM
    nb = seq // c

    def body(sink_ref, qt_ref, kvc_ref, kvp_ref, kvtc_ref, kvtp_ref, dot_ref, cap_ref, sap_ref, sbp_ref,
             ctc_ref, stc_ref, dqt_ref, dkv_ref, dsink_ref, carry):
        i = pl.program_id(0)

        @pl.when(i == 0)
        def _():
            carry[...] = jnp.zeros_like(carry)
            dsink_ref[...] = jnp.zeros_like(dsink_ref)

        def flush(total):
            dk_pre = _rope_att_t(total[:, :LANE], cap_ref[...], sap_ref[...], sbp_ref[...])
            dkv_ref[:, :LANE] = dk_pre.astype(BF)
            dkv_ref[:, LANE:] = total[:, LANE:].astype(BF)

        @pl.when(i < nb)
        def _():
            kwin = jnp.concatenate([kvp_ref[:, :LANE], kvc_ref[:, :LANE]], axis=0)
            vwin = jnp.concatenate([kvp_ref[:, LANE:], kvc_ref[:, LANE:]], axis=0)
            ktwin = jnp.concatenate([kvtp_ref[:LANE, :], kvtc_ref[:LANE, :]], axis=1)
            ctc, stc = ctc_ref[...], stc_ref[...]
            lane1 = lax.broadcasted_iota(jnp.int32, (1, LANE), 1)
            dkw = jnp.zeros((2 * c, LANE), F32)
            dvw = jnp.zeros((2 * c, LANE), F32)
            dsink = jnp.zeros((1, LANE), F32)
            for g in (0, 1):
                qpad, dopad = _att_group(qt_ref, g), _att_group(dot_ref, g)
                prob, psink = _att_probs(kwin, qpad, sink_ref, g, i)
                dprob = _nn(vwin, dopad)
                drow = jnp.sum(dprob * prob, axis=0, keepdims=True)
                ds = (prob * (dprob - drow) * ATT_SCALE).astype(BF)
                dqg = _nn(ktwin[d * g:d * (g + 1), :], ds)
                dkw = dkw + _nt(ds, qpad)
                dvw = dvw + _nt(prob.astype(BF), dopad)
                dsink_lanes = psink * drow
                for j in range(ATT_Q_PER_KV):
                    h = ATT_Q_PER_KV * g + j
                    dqt_ref[d * h:d * (h + 1), :] = _rope_att_rows(dqg[:, c * j:c * (j + 1)], ctc, stc, -1.0).astype(BF)
                    dsink = dsink + jnp.where(lane1 == h, -jnp.sum(dsink_lanes[:, c * j:c * (j + 1)]), 0.0)
            dsink_ref[...] += dsink
            flush(carry[...] + jnp.concatenate([dkw[:c], dvw[:c]], axis=1))
            carry[...] = jnp.concatenate([dkw[c:], dvw[c:]], axis=1)

        @pl.when(i == nb)
        def _():
            flush(carry[...])

    cur = lambda i: jnp.minimum(i, nb - 1)
    prev = lambda i: jnp.clip(i - 1, 0, nb - 1)
    rows_cur = lambda w: pl.BlockSpec((c, w), lambda i: (cur(i), 0))
    rows_prev = lambda w: pl.BlockSpec((c, w), lambda i: (prev(i), 0))
    cols_cur = lambda h: pl.BlockSpec((h, c), lambda i: (0, cur(i)))
    cols_prev = lambda h: pl.BlockSpec((h, c), lambda i: (0, prev(i)))
    return _call(
        body, (sinks, qt, kv, kv, kvt, kvt, dot, ca, sa, sb, ct, st), comm, name="attn_bwd", grid=(nb + 1,),
        in_specs=[SMEM_FULL, cols_cur(512), rows_cur(256), rows_prev(256), cols_cur(256), cols_prev(256),
                  cols_cur(512), rows_prev(LANE), rows_prev(LANE), rows_prev(LANE), cols_cur(8), cols_cur(8)],
        out_specs=[cols_cur(512), rows_prev(256), pl.BlockSpec((1, LANE), lambda i: (0, 0))],
        out_shape=[jax.ShapeDtypeStruct((512, seq), BF), jax.ShapeDtypeStruct((seq, 256), BF),
                   jax.ShapeDtypeStruct((1, LANE), F32)],
        scratch_shapes=[pltpu.VMEM((c, 256), F32)])


def _inproj_bwd(dqa, dkva, dret, dgates, dh1, x, g1, w_in, comm=None):
    seq = x.shape[0]
    tm = min(TOKEN_TILE, seq)

    def body(dqa_ref, dkva_ref, dret_ref, dgates_ref, dh1_ref, x_ref, g_ref, w_hbm, dx_ref, dg1_ref, w_vmem, w_sems):
        w = _Resident(w_hbm, w_vmem, w_sems, IN_CHUNKS_BWD)
        _Resident.load(w)

        @pl.when(pl.program_id(0) == 0)
        def _():
            dg1_ref[...] = jnp.zeros_like(dg1_ref)

        dxn = _tn(dqa_ref[...], w.chunk(0)[...])
        dxn = dxn + _nn(dkva_ref[...], w.chunk(1)[...])
        dxn = dxn + _nn(dret_ref[...], w.chunk(2)[...])
        dxn = dxn + _nn(dgates_ref[...], w.chunk(3)[...])
        dnorm, dg = _rms_bwd(dxn, x_ref[...], g_ref[...])
        dx_ref[...] = dh1_ref[...] + dnorm
        dg1_ref[...] += dg

    return _call(
        body, (dqa, dkva, dret, dgates, dh1, x, g1, w_in), comm, name="inproj_bwd", grid=(seq // tm,),
        in_specs=[_cols(512, tm), _rows(tm, 256), _rows(tm, 3072), _rows(tm, 2048), _rows(tm, D_MODEL),
                  _rows(tm, D_MODEL), VMEM_FULL, ANY],
        out_specs=[_rows(tm, D_MODEL), pl.BlockSpec((1, D_MODEL), lambda i: (0, 0))],
        out_shape=[jax.ShapeDtypeStruct((seq, D_MODEL), F32), jax.ShapeDtypeStruct((1, D_MODEL), F32)],
        scratch_shapes=_Resident.scratch(w_in, IN_CHUNKS_BWD))


def _local_step(x, target, g1, bg, sinks, gain, g2, g3, w_in, shards):
    seq = x.shape[0]
    ca, sa, sb, cr, sr, ct, st = _rope_tables(seq)
    rtabs = _ret_tables()
    scatter = _Scatter

    comm_mid, unpack_mid = _gather(shards, ("w_att_up", "w_ret_up", "w_out"))
    comm_ff1, unpack_ff1 = _gather(shards, ("w_ff1",))
    (xn1, qt, kv, kvt, qkr, vr, gr, gates), got = _inproj_fwd(x, g1, w_in, (ca, sa, sb, cr, sr, ct, st),
                                                              _join(comm_mid, comm_ff1))
    w_att, w_ret, w_out = unpack_mid(got[:3])
    (w_ff1,) = unpack_ff1(got[3:])
    comm, unpack = _gather(shards, ("w_ff2",))
    oa, got = _attn_fwd(qt, kv, kvt, sinks, comm)
    (w_ff2,) = unpack(got)
    (ret, yrp, states), _ = _ret_fwd(qkr, vr, gr, gain, rtabs)
    ya, yr, merged, h1 = _mix_fwd(oa, yrp, gates, bg, x, w_att, w_ret, w_out)
    xn2, u, dh2, loss, dg3 = _mlp_fwd_loss(h1, g2, g3, target, w_ff1, w_ff2)

    du, dh1, dg2 = _mlp_bwd(dh2, u, h1, g2, w_ff1, w_ff2)
    ff2 = scatter(dict(w_ff2=_mm_tn(u, dh2, "dw_ff2", relu_sq=True)), "ff2")
    d_ff1, r1 = _mm_tn(du, xn2, "dw_ff1", comm=ff2.pair_comm())
    ff2.pair_done(r1)
    ff1 = scatter(dict(w_ff1=d_ff1), "ff1")
    (dya, dyr, dgates, doa, dyrp, db), got = _mix_bwd(dh1, ya, yr, gates, bg, w_att, w_ret, w_out,
                                                      _join(ff1.pair_comm(), ff2.chip_comm()))
    ff1.pair_done(got[:1])
    ff2.chip_done(got[1:])
    mid = scatter(dict(w_att_up=_mm_tn(oa, dya, "dw_att_up", a_is_t=True), w_ret_up=_mm_tn(yrp, dyr, "dw_ret_up"),
                       w_out=_mm_tn(merged, dh1, "dw_out")), "mid")
    (dqa, dkva, dsink), r2 = _attn_bwd(qt, kv, kvt, doa, sinks, ca, sa, sb, ct, st, ff1.chip_comm())
    ff1.chip_done(r2)
    (dret, dgain), r1 = _ret_bwd(qkr, vr, gr, ret, dyrp, states, gain, rtabs, cr, sr, mid.pair_comm())
    mid.pair_done(r1)
    d_in, r2 = _dw_in_t((dqa, dkva, dret, dgates), (True, False, False, False), xn1, mid.chip_comm())
    mid.chip_done(r2)
    win = scatter(dict(w_in=d_in), "in")
    win.pair_done(_comm_alone(win.pair_comm(), "pair_exchange_in"))
    (dx, dg1), r2 = _inproj_bwd(dqa, dkva, dret, dgates, dh1, x, g1, w_in, win.chip_comm())
    win.chip_done(r2)
    small = dict(norm_mix_gain=dg1, b_gates=db, attn_sinks=dsink, ret_gn_gain=dgain, norm_mlp_gain=dg2,
                 norm_final_gain=dg3)
    return loss, dx, small, (win, mid, ff1, ff2)


def _coords():
    return lax.axis_index("x"), lax.axis_index("y"), lax.axis_index("c")


def _flip(v, bit):
    return 1 - v if bit else v


def _gather_comm(shards):
    n = len(shards)

    def parts(ins, outs, sems):
        send_sems, recv_sems, local_sems = sems
        x, y, c = _coords()
        me, sib = (x, y, c), (x, y, 1 - c)
        chips = [(1 - x, y), (x, 1 - y), (1 - x, 1 - y)]

        def slot(a, blk):
            return outs[a].at[4 * blk[0] + 2 * blk[1] + blk[2]]

        def copy(a, k, blk, to, src=None):
            return pltpu.make_async_remote_copy(
                src_ref=slot(a, blk) if src is None else src, dst_ref=slot(a, blk),
                send_sem=send_sems.at[a, k], recv_sem=recv_sems.at[a, k], device_id=to, device_id_type=MESH)

        mine = [pltpu.make_async_copy(ins[a], slot(a, me), local_sems.at[a]) for a in range(n)]
        first = []
        for a in range(n):
            first.append(copy(a, 0, me, sib, src=ins[a]))
            first += [copy(a, 1 + j, me, (*chip, c), src=ins[a]) for j, chip in enumerate(chips)]
        return me, sib, chips, c, copy, mine, first

    def start(ins, outs, sems):
        *_, mine, first = parts(ins, outs, sems)
        for cp in mine + first:
            cp.start()

    def finish(ins, outs, sems):
        me, sib, chips, c, copy, mine, first = parts(ins, outs, sems)
        passed = []
        for j, chip in enumerate(chips):
            for a in range(n):
                copy(a, 1 + j, (*chip, c), me).wait_recv()
                fwd = copy(a, 4 + j, (*chip, c), sib)
                fwd.start()
                passed.append(fwd)
        for a in range(n):
            copy(a, 0, sib, me).wait_recv()
            for j, chip in enumerate(chips):
                copy(a, 4 + j, (*chip, 1 - c), me).wait_recv()
        for cp in first + passed:
            cp.wait_send()
        for cp in mine:
            cp.wait()

    return _Comm(list(shards), [jax.ShapeDtypeStruct((N_DEV,) + s.shape, s.dtype) for s in shards],
                 [pltpu.SemaphoreType.DMA((n, 7)), pltpu.SemaphoreType.DMA((n, 7)), pltpu.SemaphoreType.DMA((n,))],
                 start, finish)


COLUMN_SHARDED = ("w_in", "w_ff1")
COLUMN_RELAID = ("w_att_up",)


def _gather(shards, names):
    def unpack(got):
        return [jnp.transpose(g, (1, 0, 2)).reshape(g.shape[1], N_DEV * g.shape[2]) if k in COLUMN_RELAID
                else g.reshape(N_DEV * g.shape[1], g.shape[2]) for k, g in zip(names, got)]

    return _gather_comm([shards[k] for k in names]), unpack


def _pair_comm(grads):
    n = len(grads)

    def copies(g, r1, sems):
        send_sems, recv_sems = sems
        x, y, c = _coords()
        return [pltpu.make_async_remote_copy(
            src_ref=g[a].at[2 * j + (1 - c)], dst_ref=r1[a].at[j], send_sem=send_sems.at[a, j],
            recv_sem=recv_sems.at[a, j], device_id=(x, y, 1 - c), device_id_type=MESH)
            for a in range(n) for j in range(4)]

    def start(g, r1, sems):
        for cp in copies(g, r1, sems):
            cp.start()

    def finish(g, r1, sems):
        for cp in copies(g, r1, sems):
            cp.wait_recv()
        for cp in copies(g, r1, sems):
            cp.wait_send()

    return _Comm(list(grads), [jax.ShapeDtypeStruct((4,) + g.shape[1:], g.dtype) for g in grads],
                 [pltpu.SemaphoreType.DMA((n, 4)), pltpu.SemaphoreType.DMA((n, 4))], start, finish)


def _small_comm(small):
    def parts(ins, outs, sems):
        (small_ref,), (small_all,) = ins, outs
        ssend, srecv, lsem = sems
        x, y, c = _coords()
        me_idx = 4 * x + 2 * y + c
        own = pltpu.make_async_copy(small_ref, small_all.at[me_idx], lsem)
        sends, recvs = [], []
        for r in range(1, N_DEV):
            px, py, pc = _flip(x, r & 4), _flip(y, r & 2), _flip(c, r & 1)
            sends.append(pltpu.make_async_remote_copy(
                src_ref=small_ref, dst_ref=small_all.at[me_idx], send_sem=ssend.at[r - 1], recv_sem=srecv.at[r - 1],
                device_id=(px, py, pc), device_id_type=MESH))
            recvs.append(pltpu.make_async_remote_copy(
                src_ref=small_ref, dst_ref=small_all.at[4 * px + 2 * py + pc], send_sem=ssend.at[r - 1],
                recv_sem=srecv.at[r - 1], device_id=(px, py, pc), device_id_type=MESH))
        return own, sends, recvs

    def start(ins, outs, sems):
        own, sends, _ = parts(ins, outs, sems)
        own.start()
        for cp in sends:
            cp.start()

    def finish(ins, outs, sems):
        own, sends, recvs = parts(ins, outs, sems)
        for cp in recvs:
            cp.wait_recv()
        for cp in sends:
            cp.wait_send()
        own.wait()

    return _Comm([small], [jax.ShapeDtypeStruct((N_DEV,) + small.shape, small.dtype)],
                 [pltpu.SemaphoreType.DMA((N_DEV - 1,)), pltpu.SemaphoreType.DMA((N_DEV - 1,)),
                  pltpu.SemaphoreType.DMA], start, finish)


def _pair_sum(grads, r1s, c_arr, tag):
    n = len(grads)
    q = 4

    def body(c_ref, *refs):
        g, r, t = refs[:n], refs[n:2 * n], refs[2 * n:]
        for a in range(n):
            t[a][...] = (g[a][...] + r[a][...]).astype(t[a].dtype)

    def blk(arr):
        return (1, arr.shape[1] // q, arr.shape[2])

    grid_spec = pltpu.PrefetchScalarGridSpec(
        num_scalar_prefetch=1, grid=(4, q),
        in_specs=[pl.BlockSpec(blk(g), lambda j, s, c_ref: (2 * j + c_ref[0], s, 0)) for g in grads]
        + [pl.BlockSpec(blk(r), lambda j, s, c_ref: (j, s, 0)) for r in r1s],
        out_specs=[pl.BlockSpec(blk(r), lambda j, s, c_ref: (j, s, 0)) for r in r1s])
    return pl.pallas_call(
        body, name="pair_sum_" + tag, grid_spec=grid_spec,
        out_shape=[jax.ShapeDtypeStruct(r.shape, RS_PAYLOAD) for r in r1s],
        compiler_params=_params(2),
    )(c_arr, *grads, *r1s)


def _chip_comm(ts):
    n = len(ts)

    def copies(t, r2, sems):
        send_sems, recv_sems = sems
        x, y, c = _coords()
        out = []
        for a in range(n):
            for r in range(1, 4):
                tx, ty = _flip(x, r & 2), _flip(y, r & 1)
                out.append(pltpu.make_async_remote_copy(
                    src_ref=t[a].at[2 * tx + ty], dst_ref=r2[a].at[r - 1], send_sem=send_sems.at[a, r - 1],
                    recv_sem=recv_sems.at[a, r - 1], device_id=(tx, ty, c), device_id_type=MESH))
        return out

    def start(t, r2, sems):
        for cp in copies(t, r2, sems):
            cp.start()

    def finish(t, r2, sems):
        for cp in copies(t, r2, sems):
            cp.wait_recv()
        for cp in copies(t, r2, sems):
            cp.wait_send()

    return _Comm(list(ts), [jax.ShapeDtypeStruct((3,) + t.shape[1:], t.dtype) for t in ts],
                 [pltpu.SemaphoreType.DMA((n, 3)), pltpu.SemaphoreType.DMA((n, 3))], start, finish)


class _Scatter:
    def __init__(self, grads, tag):
        self.names, self.tag = tuple(grads), tag
        self.stacks = [jnp.transpose(g.reshape(g.shape[0], N_DEV, g.shape[1] // N_DEV), (1, 0, 2))
                       if k in COLUMN_RELAID else g.reshape(N_DEV, g.shape[0] // N_DEV, g.shape[1])
                       for k, g in grads.items()]

    def pair_comm(self):
        return _pair_comm(self.stacks)

    def pair_done(self, r1s):
        self.r1s = list(r1s)
        c_arr = jnp.reshape(lax.axis_index("c"), (1,)).astype(jnp.int32)
        self.ts = _pair_sum(self.stacks, self.r1s, c_arr, self.tag)

    def chip_comm(self):
        return _chip_comm(self.ts)

    def chip_done(self, r2s):
        self.r2s = list(r2s)


def _adamw(w, g, m, v):
    m = ADAM_B1 * m + (1.0 - ADAM_B1) * g
    v = ADAM_B2 * v + (1.0 - ADAM_B2) * jnp.square(g)
    m_hat = m / (1.0 - ADAM_B1 ** ADAM_STEP)
    v_hat = v / (1.0 - ADAM_B2 ** ADAM_STEP)
    delta = -ADAM_LR * (m_hat / (jnp.sqrt(v_hat) + ADAM_EPS) + ADAM_WD * w)
    return delta, m, v


ADAM_STEPS = 4


def _piece_specs(stacks):
    def rows(s):
        return s.shape[1] // ADAM_STEPS
    return ([pl.BlockSpec((1, rows(s), s.shape[2]), lambda i, idx_ref: (idx_ref[0], i, 0)) for s in stacks]
            + [pl.BlockSpec((1, rows(s), s.shape[2]), lambda i, idx_ref: (idx_ref[1], i, 0)) for s in stacks]
            + [pl.BlockSpec((3, rows(s), s.shape[2]), lambda i, idx_ref: (0, i, 0)) for s in stacks])


def _piece_sum(g0, r1, r2):
    return (((g0[0] + r1[0]) + r2[0].astype(F32)) + r2[1].astype(F32)) + r2[2].astype(F32)


def _shard_sum(stacks, r1s, r2s, idx_arr):
    n = len(stacks)

    def body(idx_ref, *refs):
        g0, r1, r2, outs = (refs[k * n:(k + 1) * n] for k in range(4))
        for a in range(n):
            outs[a][...] = _piece_sum(g0[a], r1[a], r2[a])

    out_specs = [pl.BlockSpec((s.shape[1] // ADAM_STEPS, s.shape[2]), lambda i, idx_ref: (i, 0)) for s in stacks]
    grid_spec = pltpu.PrefetchScalarGridSpec(num_scalar_prefetch=1, grid=(ADAM_STEPS,),
                                             in_specs=_piece_specs(stacks), out_specs=out_specs)
    return pl.pallas_call(
        body, name="shard_sum", grid_spec=grid_spec,
        out_shape=[jax.ShapeDtypeStruct(s.shape[1:], F32) for s in stacks],
        compiler_params=_params(1),
    )(idx_arr, *stacks, *r1s, *r2s)


def _adam_big(pieces, summed, ws, ms, vs, idx_arr):
    stacks, r1s, r2s = pieces
    n_p, n = len(stacks), len(ws)

    def body(idx_ref, *refs):
        it = iter(refs)
        g0, r1, r2, gs, w, m, v = ([next(it) for _ in range(k)] for k in (n_p, n_p, n_p, n - n_p, n, n, n))
        outs = list(it)
        for a in range(n):
            g = _piece_sum(g0[a], r1[a], r2[a]) if a < n_p else gs[a - n_p][...]
            delta, nm, nv = _adamw(w[a][...], g, m[a][...], v[a][...])
            outs[4 * a][...] = g
            outs[4 * a + 1][...] = delta
            outs[4 * a + 2][...] = nm
            outs[4 * a + 3][...] = nv

    def shard_spec(w):
        return pl.BlockSpec((w.shape[0] // ADAM_STEPS, w.shape[1]), lambda i, idx_ref: (i, 0))

    in_specs = _piece_specs(stacks) + [shard_spec(w) for w in ws[n_p:]] + [shard_spec(w) for w in ws] * 3
    out_specs = [shard_spec(w) for w in ws for _ in range(4)]
    grid_spec = pltpu.PrefetchScalarGridSpec(num_scalar_prefetch=1, grid=(ADAM_STEPS,), in_specs=in_specs,
                                             out_specs=out_specs)
    return pl.pallas_call(
        body, name="adam_big", grid_spec=grid_spec,
        out_shape=[jax.ShapeDtypeStruct(w.shape, F32) for w in ws for _ in range(4)],
        compiler_params=_params(1),
    )(idx_arr, *stacks, *r1s, *r2s, *summed, *ws, *ms, *vs)


def _adam_small(small_all, w, m, v):
    def body(all_ref, w_ref, m_ref, v_ref, g_ref, d_ref, nm_ref, nv_ref):
        g = all_ref[0]
        for k in range(1, N_DEV):
            g = g + all_ref[k]
        delta, nm, nv = _adamw(w_ref[...], g, m_ref[...], v_ref[...])
        g_ref[...] = g
        d_ref[...] = delta
        nm_ref[...] = nm
        nv_ref[...] = nv

    return pl.pallas_call(
        body, name="adam_small",
        in_specs=[VMEM_FULL] * 4, out_specs=[VMEM_FULL] * 4,
        out_shape=[jax.ShapeDtypeStruct(w.shape, F32)] * 4,
    )(small_all, w, m, v)


RS_PAYLOAD = BF


def _pack_rows(rows):
    rid = lax.broadcasted_iota(jnp.int32, (8, D_MODEL), 0)
    out = jnp.zeros((8, D_MODEL), F32)
    for i, r in enumerate(rows):
        out = jnp.where(rid == i, jnp.broadcast_to(r, (8, D_MODEL)), out)
    return out


def _small_rows(norm_mix_gain, b_gates, attn_sinks, ret_gn_gain, norm_mlp_gain, norm_final_gain):
    return [norm_mix_gain, b_gates[:, :D_MODEL], b_gates[:, D_MODEL:], ret_gn_gain, norm_mlp_gain,
            norm_final_gain.reshape(1, D_MODEL), jnp.pad(attn_sinks, ((0, 0), (0, D_MODEL - ATT_HEADS)))]


def _unpack_small(p):
    return dict(norm_mix_gain=p[0:1], b_gates=jnp.concatenate([p[1:2], p[2:3]], axis=1), ret_gn_gain=p[3:4],
                norm_mlp_gain=p[4:5], norm_final_gain=p[5], attn_sinks=p[6:7, :ATT_HEADS])


WEIGHTS = ("norm_mix_gain", "w_in", "b_gates", "attn_sinks", "ret_gn_gain", "w_att_up", "w_ret_up", "w_out",
           "norm_mlp_gain", "w_ff1", "w_ff2", "norm_final_gain")
BIG = ("w_in", "w_att_up", "w_ret_up", "w_out", "w_ff1", "w_ff2")


def kernel(x, norm_mix_gain, w_in, b_gates, attn_sinks, ret_gn_gain, w_att_up, w_ret_up, w_out, norm_mlp_gain, w_ff1, w_ff2, norm_final_gain, loss_target, m_norm_mix_gain, m_w_in, m_b_gates, m_attn_sinks, m_ret_gn_gain, m_w_att_up, m_w_ret_up, m_w_out, m_norm_mlp_gain, m_w_ff1, m_w_ff2, m_norm_final_gain, v_norm_mix_gain, v_w_in, v_b_gates, v_attn_sinks, v_ret_gn_gain, v_w_att_up, v_w_ret_up, v_w_out, v_norm_mlp_gain, v_w_ff1, v_w_ff2, v_norm_final_gain):
    w = dict(norm_mix_gain=norm_mix_gain, w_in=w_in, b_gates=b_gates, attn_sinks=attn_sinks, ret_gn_gain=ret_gn_gain,
             w_att_up=w_att_up, w_ret_up=w_ret_up, w_out=w_out, norm_mlp_gain=norm_mlp_gain, w_ff1=w_ff1,
             w_ff2=w_ff2, norm_final_gain=norm_final_gain)
    m = dict(norm_mix_gain=m_norm_mix_gain, w_in=m_w_in, b_gates=m_b_gates, attn_sinks=m_attn_sinks,
             ret_gn_gain=m_ret_gn_gain, w_att_up=m_w_att_up, w_ret_up=m_w_ret_up, w_out=m_w_out,
             norm_mlp_gain=m_norm_mlp_gain, w_ff1=m_w_ff1, w_ff2=m_w_ff2, norm_final_gain=m_norm_final_gain)
    v = dict(norm_mix_gain=v_norm_mix_gain, w_in=v_w_in, b_gates=v_b_gates, attn_sinks=v_attn_sinks,
             ret_gn_gain=v_ret_gn_gain, w_att_up=v_w_att_up, w_ret_up=v_w_ret_up, w_out=v_w_out,
             norm_mlp_gain=v_norm_mlp_gain, w_ff1=v_w_ff1, w_ff2=v_w_ff2, norm_final_gain=v_norm_final_gain)

    shards = {k: (w[k][0].T if k in COLUMN_SHARDED else w[k][0]).astype(BF) for k in BIG}
    comm, unpack = _gather(shards, ("w_in",))
    (w_in_full,) = unpack(_comm_alone(comm, "allgather_w_in"))

    loss_p, dx, small, groups = _local_step(
        x[0], loss_target[0], norm_mix_gain, b_gates, attn_sinks[0], ret_gn_gain, norm_mlp_gain,
        norm_final_gain.reshape(1, D_MODEL), w_in_full, shards)

    by_name = {}
    for grp in groups:
        for k, stack, r1, r2 in zip(grp.names, grp.stacks, grp.r1s, grp.r2s):
            by_name[k] = (stack, r1, r2)
    lane = lax.broadcasted_iota(jnp.int32, (1, D_MODEL), 1)
    sink_row = jnp.where(lane < ATT_HEADS, jnp.pad(small["attn_sinks"], ((0, 0), (0, D_MODEL - LANE))),
                         jnp.where(lane == ATT_HEADS, jnp.pad(loss_p, ((0, 0), (0, D_MODEL - LANE)), mode="edge"),
                                   0.0))
    small_pack = _pack_rows([small["norm_mix_gain"], small["b_gates"][:, :D_MODEL], small["b_gates"][:, D_MODEL:],
                             small["ret_gn_gain"], small["norm_mlp_gain"], small["norm_final_gain"], sink_row])
    xi, yi, ci = _coords()
    (small_all,) = _comm_alone(_small_comm(small_pack), "small_exchange")
    idx_arr = jnp.stack([4 * xi + 2 * yi + ci, 2 * xi + yi]).astype(jnp.int32)
    in_pieces = ["w_att_up", "w_ret_up", "w_out", "w_ff2", "w_in"]
    in_sum = ["w_ff1"]
    order = in_pieces + in_sum
    summed_t = _shard_sum(*[[by_name[k][i] for k in in_sum] for i in range(3)], idx_arr)

    def shard(tree, k):
        return tree[k][0].T if k == "w_in" else tree[k][0]

    adam_out = _adam_big([[by_name[k][i] for k in in_pieces] for i in range(3)], [g.T for g in summed_t],
                         [shard(w, k) for k in order], [shard(m, k) for k in order], [shard(v, k) for k in order],
                         idx_arr)
    big_out = [adam_out[4 * order.index(k) + i].T if k == "w_in" else adam_out[4 * order.index(k) + i]
               for k in BIG for i in range(4)]
    sm_g, sm_d, sm_m, sm_v = _adam_small(small_all, _pack_rows(_small_rows(*[w[k] for k in WEIGHTS if k not in BIG])),
                                         _pack_rows(_small_rows(*[m[k] for k in WEIGHTS if k not in BIG])),
                                         _pack_rows(_small_rows(*[v[k] for k in WEIGHTS if k not in BIG])))

    loss = sm_g[6, ATT_HEADS]
    kinds = []
    for idx, packed in enumerate((sm_g, sm_d, sm_m, sm_v)):
        out = _unpack_small(packed)
        for a, k in enumerate(BIG):
            out[k] = big_out[4 * a + idx][None]
        kinds.append(out)
    return (loss, dx[None], *[kinds[0][k] for k in WEIGHTS], *[kinds[1][k] for k in WEIGHTS],
            *[kinds[2][k] for k in WEIGHTS], *[kinds[3][k] for k in WEIGHTS])
```

```python
import functools
import math

import jax
import jax.numpy as jnp
from jax import lax
from jax.experimental import pallas as pl
from jax.experimental.pallas import tpu as pltpu

F32 = jnp.float32
BF = jnp.bfloat16
MESH = pl.DeviceIdType.MESH

D_MODEL = 1024
ATT_HEADS = 8
ATT_HEAD_DIM = 64
ATT_BLOCK = 128
ROPE_DIM = 16
ROPE_THETA = 500000.0
RET_HEADS = 4
RET_KEY_DIM = 128
RET_VAL_DIM = 256
RET_CHUNK = 128
RET_ROT_BASE = 10000.0
D_FF = 4096
NORM_EPS = 1e-6
GN_EPS = 1e-6
NEG_INF = -1e30
ATT_SCALE = ATT_HEAD_DIM ** -0.5
RET_SCALE = RET_KEY_DIM ** -0.5

C_QA, C_KA, C_VA, C_QR, C_KR, C_VR, C_GR, C_GATES, C_END = 0, 512, 640, 768, 1280, 1792, 2816, 3840, 5888

ADAM_LR = 0.001
ADAM_B1 = 0.9
ADAM_B2 = 0.999
ADAM_EPS = 1e-08
ADAM_WD = 0.01
ADAM_STEP = 10

N_DEV = 8
LANE = 128
VMEM_LIMIT = 56 * 1024 * 1024
TOKEN_TILE = 256
TN_TOKEN_TILE = 1024
FF_CHUNKS = (0, 1024, 2048, 3072, 4096)
IN_CHUNKS_FWD = (C_QA, C_KA, C_QR, C_KR, C_VR, C_GR, C_GATES, C_END)
IN_CHUNKS_BWD = (C_QA, C_KA, C_QR, C_GATES, C_END)

RET_LOG_GAMMA = tuple(math.log1p(-(2.0 ** (-5.0 - h))) for h in range(RET_HEADS))
RET_CHUNK_DECAY = tuple(math.exp(RET_CHUNK * lg) for lg in RET_LOG_GAMMA)

VMEM_FULL = pl.BlockSpec(memory_space=pltpu.VMEM)
SMEM_FULL = pl.BlockSpec(memory_space=pltpu.SMEM)
ANY = pl.BlockSpec(memory_space=pl.ANY)


def _params(n_axes):
    return pltpu.CompilerParams(dimension_semantics=("arbitrary",) * n_axes, vmem_limit_bytes=VMEM_LIMIT)


def _nn(a, b):
    return jnp.dot(a, b, preferred_element_type=F32)


def _nt(a, b):
    return lax.dot_general(a, b, (((1,), (1,)), ((), ())), preferred_element_type=F32)


def _tn(a, b):
    return lax.dot_general(a, b, (((0,), (0,)), ((), ())), preferred_element_type=F32)


def _sigmoid(v):
    return 1.0 / (1.0 + jnp.exp(-v))


def _rows(tile, width):
    return pl.BlockSpec((tile, width), lambda i: (i, 0))


def _cols(height, tile):
    return pl.BlockSpec((height, tile), lambda i: (0, i))


class _Resident:
    def __init__(self, hbm_ref, vmem_ref, sems, bounds):
        self.hbm, self.vmem, self.sems, self.bounds = hbm_ref, vmem_ref, sems, bounds

    @staticmethod
    def scratch(w, bounds):
        return [pltpu.VMEM(w.shape, w.dtype), pltpu.SemaphoreType.DMA((len(bounds) - 1,))]

    @staticmethod
    def load(*weights):
        @pl.when(pl.program_id(0) == 0)
        def _():
            copies = [w._copy(c) for w in weights for c in range(len(w.bounds) - 1)]
            for cp in copies:
                cp.start()
            for cp in copies:
                cp.wait()

    def _rows(self, c):
        return pl.ds(self.bounds[c], self.bounds[c + 1] - self.bounds[c])

    def _copy(self, c):
        return pltpu.make_async_copy(self.hbm.at[self._rows(c)], self.vmem.at[self._rows(c)], self.sems.at[c])

    def chunk(self, c):
        return self.vmem.at[self._rows(c)]


class _Comm:
    def __init__(self, inputs, out_shapes, scratch, start, finish):
        self.inputs, self.out_shapes, self.scratch, self.start, self.finish = inputs, out_shapes, scratch, start, finish


def _join(a, b):
    na_in, na_out, na_sem = len(a.inputs), len(a.out_shapes), len(a.scratch)

    def split(ins, outs, sems):
        return (ins[:na_in], outs[:na_out], sems[:na_sem]), (ins[na_in:], outs[na_out:], sems[na_sem:])

    def start(ins, outs, sems):
        pa, pb = split(ins, outs, sems)
        a.start(*pa)
        b.start(*pb)

    def finish(ins, outs, sems):
        pa, pb = split(ins, outs, sems)
        a.finish(*pa)
        b.finish(*pb)

    return _Comm(list(a.inputs) + list(b.inputs), list(a.out_shapes) + list(b.out_shapes),
                 list(a.scratch) + list(b.scratch), start, finish)


def _call(body, args, comm=None, *, name, grid, in_specs, out_specs, out_shape, scratch_shapes=()):
    params = _params(len(grid))
    if comm is None:
        return pl.pallas_call(body, name=name, grid=grid, in_specs=in_specs, out_specs=out_specs, out_shape=out_shape,
                              scratch_shapes=scratch_shapes, compiler_params=params)(*args), ()
    single = not isinstance(out_specs, (list, tuple))
    out_specs_l = [out_specs] if single else list(out_specs)
    out_shape_l = [out_shape] if single else list(out_shape)
    n_in, n_out, n_scr = len(in_specs), len(out_specs_l), len(scratch_shapes)
    n_cin, n_cout = len(comm.inputs), len(comm.out_shapes)

    def hosted(*refs):
        it = iter(refs)
        ins, cin, outs, cout, scr = ([next(it) for _ in range(k)] for k in (n_in, n_cin, n_out, n_cout, n_scr))
        sems = list(it)
        ids = [pl.program_id(k) for k in range(len(grid))]
        first = functools.reduce(jnp.logical_and, [i == 0 for i in ids])
        last = functools.reduce(jnp.logical_and, [i == g - 1 for i, g in zip(ids, grid)])

        @pl.when(first)
        def _():
            comm.start(cin, cout, sems)

        body(*ins, *outs, *scr)

        @pl.when(last)
        def _():
            comm.finish(cin, cout, sems)

    res = pl.pallas_call(
        hosted, name=name, grid=grid, in_specs=list(in_specs) + [ANY] * n_cin,
        out_specs=out_specs_l + [ANY] * n_cout, out_shape=out_shape_l + list(comm.out_shapes),
        scratch_shapes=list(scratch_shapes) + list(comm.scratch), compiler_params=params)(*args, *comm.inputs)
    return (res[0] if single else res[:n_out]), res[n_out:]


def _comm_alone(comm, name):
    n_cin, n_cout = len(comm.inputs), len(comm.out_shapes)

    def body(*refs):
        cin, cout, sems = refs[:n_cin], refs[n_cin:n_cin + n_cout], refs[n_cin + n_cout:]
        comm.start(cin, cout, sems)
        comm.finish(cin, cout, sems)

    return pl.pallas_call(body, name=name, in_specs=[ANY] * n_cin, out_specs=[ANY] * n_cout,
                          out_shape=list(comm.out_shapes), scratch_shapes=list(comm.scratch))(*comm.inputs)


def _slabs(v, fn):
    return jnp.concatenate([fn(v[:, LANE * j:LANE * (j + 1)]) for j in range(v.shape[1] // LANE)], axis=1)


def _rope_att(v, ca, sa, sb):
    return _slabs(v, lambda t: t * ca + pltpu.roll(t, LANE - 8, 1) * sa + pltpu.roll(t, 8, 1) * sb)


def _rope_att_t(v, ca, sa, sb):
    return _slabs(v, lambda t: t * ca + pltpu.roll(t * sa, 8, 1) + pltpu.roll(t * sb, LANE - 8, 1))


def _rope_att_rows(v, ct, st, sign):
    parts = []
    for h in range(v.shape[0] // ATT_HEAD_DIM):
        r0 = ATT_HEAD_DIM * h
        x1, x2 = v[r0:r0 + 8], v[r0 + 8:r0 + 16]
        parts += [x1 * ct - sign * (x2 * st), x2 * ct + sign * (x1 * st), v[r0 + 16:r0 + ATT_HEAD_DIM]]
    return jnp.concatenate(parts, axis=0)


def _rope_ret(v, cr, sr):
    return _slabs(v, lambda t: t * cr + pltpu.roll(t, 64, 1) * sr)


def _rope_ret_t(v, cr, sr):
    return _slabs(v, lambda t: t * cr + pltpu.roll(t * sr, 64, 1))


def _rope_tables(seq):
    pos = jnp.arange(seq, dtype=F32)

    def cs(dim, theta):
        inv = theta ** (-jnp.arange(0, dim, 2, dtype=F32) / dim)
        ang = pos[:, None] * inv[None, :]
        return jnp.cos(ang), jnp.sin(ang)

    ca, sa_ = cs(ROPE_DIM, ROPE_THETA)
    zeros8 = jnp.zeros_like(ca)
    rest = ATT_HEAD_DIM - ROPE_DIM
    c64 = jnp.concatenate([ca, ca, jnp.ones((seq, rest), F32)], axis=1)
    sa64 = jnp.concatenate([-sa_, zeros8, jnp.zeros((seq, rest), F32)], axis=1)
    sb64 = jnp.concatenate([zeros8, sa_, jnp.zeros((seq, rest), F32)], axis=1)
    cr, sr_ = cs(RET_KEY_DIM, RET_ROT_BASE)
    return (jnp.tile(c64, (1, 2)), jnp.tile(sa64, (1, 2)), jnp.tile(sb64, (1, 2)),
            jnp.concatenate([cr, cr], axis=1), jnp.concatenate([-sr_, sr_], axis=1), ca.T, sa_.T)


def _ret_tables():
    c = RET_CHUNK
    lg = jnp.asarray(RET_LOG_GAMMA, F32)
    idx = jnp.arange(c, dtype=F32)
    diff = idx[:, None] - idx[None, :]
    intra = jnp.where(diff >= 0, jnp.exp(jnp.maximum(diff, 0.0) * lg[:, None, None]), 0.0)
    qd = jnp.exp((idx + 1.0)[None, :] * lg[:, None])[..., None]
    kd = jnp.exp((c - 1.0 - idx)[None, :] * lg[:, None])[..., None]
    return intra, jnp.broadcast_to(qd, (RET_HEADS, c, RET_KEY_DIM)), jnp.broadcast_to(kd, (RET_HEADS, c, RET_KEY_DIM))


def _inproj_fwd(x, g1, w_in, tabs, comm=None):
    seq = x.shape[0]
    tm = min(TOKEN_TILE, seq)

    def body(x_ref, g_ref, w_hbm, ca_ref, sa_ref, sb_ref, cr_ref, sr_ref, ct_ref, st_ref,
             xn_ref, qt_ref, kv_ref, kvt_ref, qkr_ref, vr_ref, gr_ref, gates_ref, w_vmem, w_sems):
        w = _Resident(w_hbm, w_vmem, w_sems, IN_CHUNKS_FWD)
        _Resident.load(w)
        xf = x_ref[...]
        r = lax.rsqrt(jnp.mean(xf * xf, axis=-1, keepdims=True) + NORM_EPS)
        xn = (xf * r * g_ref[...]).astype(BF)
        xn_ref[...] = xn
        ct, st = ct_ref[...], st_ref[...]
        qt_ref[...] = _rope_att_rows(_nt(w.chunk(0)[...], xn), ct, st, 1.0).astype(BF)
        w_kv = w.chunk(1)
        kvt = _nt(w_kv[...], xn)
        kvt_ref[:LANE, :] = _rope_att_rows(kvt[:LANE], ct, st, 1.0).astype(BF)
        kvt_ref[LANE:, :] = kvt[LANE:].astype(BF)
        kvn = _nt(xn, w_kv[...])
        kv_ref[:, :LANE] = _rope_att(kvn[:, :LANE], ca_ref[...], sa_ref[...], sb_ref[...]).astype(BF)
        kv_ref[:, LANE:] = kvn[:, LANE:].astype(BF)
        cr, sr = cr_ref[...], sr_ref[...]
        qr = _rope_ret(_nt(xn, w.chunk(2)[...]), cr, sr)
        qkr_ref[:, :512] = qr.astype(BF)
        kr = _rope_ret(_nt(xn, w.chunk(3)[...]), cr, sr) * RET_SCALE
        qkr_ref[:, 512:] = kr.astype(BF)
        vr_ref[...] = _nt(xn, w.chunk(4)[...]).astype(BF)
        gr_ref[...] = _nt(xn, w.chunk(5)[...]).astype(BF)
        gates_ref[...] = _nt(xn, w.chunk(6)[...]).astype(BF)

    tab = _rows(tm, LANE)
    tab_t = _cols(8, tm)
    return _call(
        body, (x, g1, w_in, *tabs), comm, name="inproj_fwd", grid=(seq // tm,),
        in_specs=[_rows(tm, D_MODEL), VMEM_FULL, ANY, tab, tab, tab, tab, tab, tab_t, tab_t],
        out_specs=[_rows(tm, D_MODEL), _cols(512, tm), _rows(tm, 256), _cols(256, tm), _rows(tm, 1024),
                   _rows(tm, 1024), _rows(tm, 1024), _rows(tm, 2048)],
        out_shape=[jax.ShapeDtypeStruct(s, BF) for s in ((seq, D_MODEL), (512, seq), (seq, 256), (256, seq),
                                                         (seq, 1024), (seq, 1024), (seq, 1024), (seq, 2048))],
        scratch_shapes=_Resident.scratch(w_in, IN_CHUNKS_FWD))


ATT_Q_PER_KV = ATT_HEADS // 2


def _att_group(ref, kv):
    d = ATT_HEAD_DIM
    g = jnp.concatenate([ref[d * (ATT_Q_PER_KV * kv + j):d * (ATT_Q_PER_KV * kv + j + 1), :]
                         for j in range(ATT_Q_PER_KV)], axis=1)
    z = jnp.zeros_like(g)
    return jnp.concatenate([g, z] if kv == 0 else [z, g], axis=0)


def _att_probs(kwin, qpad, sink_ref, kv, i):
    c = ATT_BLOCK
    kj = lax.broadcasted_iota(jnp.int32, (2 * c, c), 0)
    qi = lax.broadcasted_iota(jnp.int32, (2 * c, c), 1)
    allowed = (kj > qi) & (kj <= qi + c) & ((kj >= c) | (i > 0))
    allowed = jnp.concatenate([allowed] * ATT_Q_PER_KV, axis=1)
    s = jnp.where(allowed, _nn(kwin, qpad) * ATT_SCALE, NEG_INF)
    sk = jnp.concatenate([jnp.full((1, c), sink_ref[ATT_Q_PER_KV * kv + j], F32) for j in range(ATT_Q_PER_KV)], axis=1)
    m = jnp.maximum(jnp.max(s, axis=0, keepdims=True), sk)
    pe = jnp.exp(s - m)
    psink = jnp.exp(sk - m)
    inv = 1.0 / (jnp.sum(pe, axis=0, keepdims=True) + psink)
    return pe * inv, psink * inv


def _attn_fwd(qt, kv, kvt, sinks, comm=None):
    seq = kv.shape[0]
    c = ATT_BLOCK
    d = ATT_HEAD_DIM
    nb = seq // c

    def body(sink_ref, qt_ref, kvc_ref, kvp_ref, kvtc_ref, kvtp_ref, ot_ref):
        i = pl.program_id(0)
        kwin = jnp.concatenate([kvp_ref[:, :LANE], kvc_ref[:, :LANE]], axis=0)
        vtwin = jnp.concatenate([kvtp_ref[LANE:, :], kvtc_ref[LANE:, :]], axis=1)
        for g in (0, 1):
            prob, _ = _att_probs(kwin, _att_group(qt_ref, g), sink_ref, g, i)
            out = _nn(vtwin[d * g:d * (g + 1), :], prob.astype(BF))
            for j in range(ATT_Q_PER_KV):
                h = ATT_Q_PER_KV * g + j
                ot_ref[d * h:d * (h + 1), :] = out[:, c * j:c * (j + 1)].astype(BF)

    prev = lambda i: jnp.maximum(i - 1, 0)
    return _call(
        body, (sinks, qt, kv, kv, kvt, kvt), comm, name="attn_fwd", grid=(nb,),
        in_specs=[SMEM_FULL, _cols(512, c), _rows(c, 256), pl.BlockSpec((c, 256), lambda i: (prev(i), 0)),
                  _cols(256, c), pl.BlockSpec((256, c), lambda i: (0, prev(i)))],
        out_specs=_cols(512, c),
        out_shape=jax.ShapeDtypeStruct((512, seq), BF))


def _ret_fwd(qkr, vr, gr, gain, rtabs, comm=None):
    seq = qkr.shape[0]
    c = RET_CHUNK
    nc = seq // c
    dk, dv = RET_KEY_DIM, RET_VAL_DIM

    def body(qk_ref, v_ref, g_ref, gain_ref, intra_ref, qd_ref, kd_ref, ret_ref, yrp_ref, st_ref, state):
        @pl.when(pl.program_id(0) == 0)
        def _():
            state[...] = jnp.zeros_like(state)

        for h in range(RET_HEADS):
            qh = qk_ref[:, dk * h:dk * (h + 1)]
            kh = qk_ref[:, 512 + dk * h:512 + dk * (h + 1)]
            vh = v_ref[:, dv * h:dv * (h + 1)]
            sh = state[h]
            shb = sh.astype(BF)
            st_ref[0, h] = shb
            att = _nt(qh, kh) * intra_ref[h]
            inner = _nn(att.astype(BF), vh)
            cross = _nn((qh.astype(F32) * qd_ref[h]).astype(BF), shb)
            out = inner + cross
            state[h] = sh * RET_CHUNK_DECAY[h] + _tn((kh.astype(F32) * kd_ref[h]).astype(BF), vh)
            ret_ref[:, dv * h:dv * (h + 1)] = out
            mu = jnp.mean(out, axis=-1, keepdims=True)
            d = out - mu
            var = jnp.mean(d * d, axis=-1, keepdims=True)
            y = d * lax.rsqrt(var + GN_EPS) * gain_ref[:, dv * h:dv * (h + 1)]
            g = g_ref[:, dv * h:dv * (h + 1)].astype(F32)
            yrp_ref[:, dv * h:dv * (h + 1)] = (g * _sigmoid(g) * y).astype(BF)

    return _call(
        body, (qkr, vr, gr, gain, *rtabs), comm, name="ret_fwd", grid=(nc,),
        in_specs=[_rows(c, 1024), _rows(c, 1024), _rows(c, 1024), VMEM_FULL, VMEM_FULL, VMEM_FULL, VMEM_FULL],
        out_specs=[_rows(c, 1024), _rows(c, 1024), pl.BlockSpec((1, RET_HEADS, dk, dv), lambda i: (i, 0, 0, 0))],
        out_shape=[jax.ShapeDtypeStruct((seq, 1024), F32), jax.ShapeDtypeStruct((seq, 1024), BF),
                   jax.ShapeDtypeStruct((nc, RET_HEADS, dk, dv), BF)],
        scratch_shapes=[pltpu.VMEM((RET_HEADS, dk, dv), F32)])


def _mix_fwd(oat, yrp, gates, bg, x, w_att, w_ret, w_out):
    seq = x.shape[0]
    tm = min(TOKEN_TILE, seq)

    def body(oat_ref, yrp_ref, gates_ref, bg_ref, x_ref, wa_ref, wr_ref, wo_ref, ya_ref, yr_ref, mg_ref, h1_ref):
        ya = _tn(oat_ref[...], wa_ref[...])
        yr = _nn(yrp_ref[...], wr_ref[...])
        gt = _sigmoid(gates_ref[...].astype(F32) + bg_ref[...])
        merged = (gt[:, :D_MODEL] * ya + gt[:, D_MODEL:] * yr).astype(BF)
        ya_ref[...] = ya.astype(BF)
        yr_ref[...] = yr.astype(BF)
        mg_ref[...] = merged
        h1_ref[...] = x_ref[...] + _nn(merged, wo_ref[...])

    return pl.pallas_call(
        body, name="mix_fwd", grid=(seq // tm,),
        in_specs=[_cols(512, tm), _rows(tm, 1024), _rows(tm, 2048), VMEM_FULL, _rows(tm, D_MODEL),
                  VMEM_FULL, VMEM_FULL, VMEM_FULL],
        out_specs=[_rows(tm, D_MODEL)] * 4,
        out_shape=[jax.ShapeDtypeStruct((seq, D_MODEL), BF)] * 3 + [jax.ShapeDtypeStruct((seq, D_MODEL), F32)],
        compiler_params=_params(1),
    )(oat, yrp, gates, bg, x, w_att, w_ret, w_out)


def _mlp_fwd_loss(h1, g2, g3, target, w_ff1, w_ff2):
    seq = h1.shape[0]
    tm = min(TOKEN_TILE, seq)

    def body(h1_ref, g2_ref, g3_ref, t_ref, w1_hbm, w2_hbm, xn2_ref, u_ref, dh2_ref, loss_ref, dg3_ref,
             w1_vmem, w1_sems, w2_vmem, w2_sems):
        w1 = _Resident(w1_hbm, w1_vmem, w1_sems, FF_CHUNKS)
        w2 = _Resident(w2_hbm, w2_vmem, w2_sems, FF_CHUNKS)
        _Resident.load(w1, w2)

        @pl.when(pl.program_id(0) == 0)
        def _():
            loss_ref[...] = jnp.zeros_like(loss_ref)
            dg3_ref[...] = jnp.zeros_like(dg3_ref)

        h1v = h1_ref[...]
        r2 = lax.rsqrt(jnp.mean(h1v * h1v, axis=-1, keepdims=True) + NORM_EPS)
        xn2 = (h1v * r2 * g2_ref[...]).astype(BF)
        xn2_ref[...] = xn2
        h2 = h1v
        for c in range(len(FF_CHUNKS) - 1):
            cols = slice(FF_CHUNKS[c], FF_CHUNKS[c + 1])
            u = _nt(xn2, w1.chunk(c)[...])
            u_ref[:, cols] = u.astype(BF)
            h2 = h2 + _nn(jnp.square(jnp.maximum(u, 0.0)).astype(BF), w2.chunk(c)[...])
        r3 = lax.rsqrt(jnp.mean(h2 * h2, axis=-1, keepdims=True) + NORM_EPS)
        hn = h2 * r3
        err = hn * g3_ref[...] - t_ref[...]
        loss_ref[...] += jnp.sum(err * err) * (0.5 / D_MODEL)
        dy = err * (1.0 / D_MODEL)
        dg3_ref[...] += jnp.sum(dy * hn, axis=0, keepdims=True)
        z = dy * g3_ref[...]
        dh2_ref[...] = r3 * (z - hn * jnp.mean(z * hn, axis=-1, keepdims=True))

    return pl.pallas_call(
        body, name="mlp_fwd_loss", grid=(seq // tm,),
        in_specs=[_rows(tm, D_MODEL), VMEM_FULL, VMEM_FULL, _rows(tm, D_MODEL), ANY, ANY],
        out_specs=[_rows(tm, D_MODEL), _rows(tm, D_FF), _rows(tm, D_MODEL),
                   pl.BlockSpec((1, LANE), lambda i: (0, 0)), pl.BlockSpec((1, D_MODEL), lambda i: (0, 0))],
        out_shape=[jax.ShapeDtypeStruct((seq, D_MODEL), BF), jax.ShapeDtypeStruct((seq, D_FF), BF),
                   jax.ShapeDtypeStruct((seq, D_MODEL), F32), jax.ShapeDtypeStruct((1, LANE), F32),
                   jax.ShapeDtypeStruct((1, D_MODEL), F32)],
        scratch_shapes=_Resident.scratch(w_ff1, FF_CHUNKS) + _Resident.scratch(w_ff2, FF_CHUNKS),
        compiler_params=_params(1),
    )(h1, g2, g3, target, w_ff1, w_ff2)


def _rms_bwd(dxn, xin, gain):
    r = lax.rsqrt(jnp.mean(xin * xin, axis=-1, keepdims=True) + NORM_EPS)
    xhat = xin * r
    z = dxn * gain
    dxin = r * (z - xhat * jnp.mean(z * xhat, axis=-1, keepdims=True))
    return dxin, jnp.sum(dxn * xhat, axis=0, keepdims=True)


def _mlp_bwd(dh2, u, h1, g2, w_ff1, w_ff2):
    seq = h1.shape[0]
    tm = min(TOKEN_TILE, seq)

    def body(dh2_ref, u_ref, h1_ref, g2_ref, w1_hbm, w2_hbm, du_ref, dh1_ref, dg2_ref,
             w1_vmem, w1_sems, w2_vmem, w2_sems):
        w2 = _Resident(w2_hbm, w2_vmem, w2_sems, FF_CHUNKS)
        w1 = _Resident(w1_hbm, w1_vmem, w1_sems, FF_CHUNKS)
        _Resident.load(w2, w1)

        @pl.when(pl.program_id(0) == 0)
        def _():
            dg2_ref[...] = jnp.zeros_like(dg2_ref)

        dh2v = dh2_ref[...]
        dh2b = dh2v.astype(BF)
        dxn2 = jnp.zeros_like(dh2v)
        for c in range(len(FF_CHUNKS) - 1):
            cols = slice(FF_CHUNKS[c], FF_CHUNKS[c + 1])
            dhdn = _nt(dh2b, w2.chunk(c)[...])
            du = (dhdn * (2.0 * jnp.maximum(u_ref[:, cols].astype(F32), 0.0))).astype(BF)
            du_ref[:, cols] = du
            dxn2 = dxn2 + _nn(du, w1.chunk(c)[...])
        dnorm, dg = _rms_bwd(dxn2, h1_ref[...], g2_ref[...])
        dh1_ref[...] = dh2v + dnorm
        dg2_ref[...] += dg

    return pl.pallas_call(
        body, name="mlp_bwd", grid=(seq // tm,),
        in_specs=[_rows(tm, D_MODEL), _rows(tm, D_FF), _rows(tm, D_MODEL), VMEM_FULL, ANY, ANY],
        out_specs=[_rows(tm, D_FF), _rows(tm, D_MODEL), pl.BlockSpec((1, D_MODEL), lambda i: (0, 0))],
        out_shape=[jax.ShapeDtypeStruct((seq, D_FF), BF), jax.ShapeDtypeStruct((seq, D_MODEL), F32),
                   jax.ShapeDtypeStruct((1, D_MODEL), F32)],
        scratch_shapes=_Resident.scratch(w_ff1, FF_CHUNKS) + _Resident.scratch(w_ff2, FF_CHUNKS),
        compiler_params=_params(1),
    )(dh2, u, h1, g2, w_ff1, w_ff2)


def _mm_tn(a, b, name, relu_sq=False, a_is_t=False, comm=None):
    kdim, seq = a.shape if a_is_t else a.shape[::-1]
    ndim = b.shape[1]
    ts = min(TN_TOKEN_TILE, seq)
    tk = min(kdim, 1024)
    tn = min(ndim, 1024)
    n_steps = seq // ts

    def body(a_ref, b_ref, o_ref):
        @pl.when(pl.program_id(2) == 0)
        def _():
            o_ref[...] = jnp.zeros_like(o_ref)

        av = a_ref[...]
        if relu_sq:
            av = jnp.square(jnp.maximum(av.astype(F32), 0.0))
        mm = _nn if a_is_t else _tn
        o_ref[...] += mm(av.astype(BF), b_ref[...].astype(BF))

    a_spec = (pl.BlockSpec((tk, ts), lambda k, n, s: (k, s)) if a_is_t
              else pl.BlockSpec((ts, tk), lambda k, n, s: (s, k)))
    res = _call(
        body, (a, b), comm, name=name, grid=(kdim // tk, ndim // tn, n_steps),
        in_specs=[a_spec, pl.BlockSpec((ts, tn), lambda k, n, s: (s, n))],
        out_specs=pl.BlockSpec((tk, tn), lambda k, n, s: (k, n)),
        out_shape=jax.ShapeDtypeStruct((kdim, ndim), F32))
    return res[0] if comm is None else res


def _dw_in_t(pieces, is_t, xn1, comm=None):
    seq = xn1.shape[0]
    ts = min(2 * TN_TOKEN_TILE, seq)
    tr = 256
    n_blk = [(p.shape[0] if t else p.shape[1]) // tr for p, t in zip(pieces, is_t)]
    offs = [sum(n_blk[:i]) for i in range(len(pieces) + 1)]
    n_p = len(pieces)

    def body(*refs):
        piece_refs, xn_ref, o_ref = refs[:n_p], refs[n_p], refs[n_p + 1]
        r, s = pl.program_id(0), pl.program_id(1)

        @pl.when(s == 0)
        def _():
            o_ref[...] = jnp.zeros_like(o_ref)

        for p in range(n_p):
            @pl.when((r >= offs[p]) & (r < offs[p + 1]))
            def _(p=p):
                mm = _nn if is_t[p] else _tn
                o_ref[...] += mm(piece_refs[p][...], xn_ref[pl.ds(pl.multiple_of(s * ts, ts), ts), :])

    def piece_spec(p):
        def index(r, s):
            inside = (r >= offs[p]) & (r < offs[p + 1])
            tok, blk = jnp.where(inside, s, 0), jnp.clip(r - offs[p], 0, n_blk[p] - 1)
            return (blk, tok) if is_t[p] else (tok, blk)
        return pl.BlockSpec((tr, ts) if is_t[p] else (ts, tr), index)

    return _call(
        body, (*pieces, xn1), comm, name="dw_in", grid=(offs[-1], seq // ts),
        in_specs=[piece_spec(p) for p in range(n_p)] + [VMEM_FULL],
        out_specs=pl.BlockSpec((tr, D_MODEL), lambda r, s: (r, 0)),
        out_shape=jax.ShapeDtypeStruct((offs[-1] * tr, D_MODEL), F32))


def _mix_bwd(dh1, ya, yr, gates, bg, w_att, w_ret, w_out, comm=None):
    seq = dh1.shape[0]
    tm = min(TOKEN_TILE, seq)

    def body(dh1_ref, ya_ref, yr_ref, gates_ref, bg_ref, wa_ref, wr_ref, wo_ref,
             dya_ref, dyr_ref, dgates_ref, doa_ref, dyrp_ref, db_ref):
        @pl.when(pl.program_id(0) == 0)
        def _():
            db_ref[...] = jnp.zeros_like(db_ref)

        dm = _nt(dh1_ref[...].astype(BF), wo_ref[...])
        gt = _sigmoid(gates_ref[...].astype(F32) + bg_ref[...])
        ga, gr = gt[:, :D_MODEL], gt[:, D_MODEL:]
        dya = (dm * ga).astype(BF)
        dyr = (dm * gr).astype(BF)
        dya_ref[...] = dya
        dyr_ref[...] = dyr
        dga = dm * ya_ref[...].astype(F32) * ga * (1.0 - ga)
        dgr = dm * yr_ref[...].astype(F32) * gr * (1.0 - gr)
        dgates_ref[:, :D_MODEL] = dga.astype(BF)
        dgates_ref[:, D_MODEL:] = dgr.astype(BF)
        db_ref[:, :D_MODEL] += jnp.sum(dga, axis=0, keepdims=True)
        db_ref[:, D_MODEL:] += jnp.sum(dgr, axis=0, keepdims=True)
        doa_ref[...] = _nt(wa_ref[...], dya).astype(BF)
        dyrp_ref[...] = _nt(dyr, wr_ref[...]).astype(BF)

    return _call(
        body, (dh1, ya, yr, gates, bg, w_att, w_ret, w_out), comm, name="mix_bwd", grid=(seq // tm,),
        in_specs=[_rows(tm, D_MODEL), _rows(tm, D_MODEL), _rows(tm, D_MODEL), _rows(tm, 2048), VMEM_FULL,
                  VMEM_FULL, VMEM_FULL, VMEM_FULL],
        out_specs=[_rows(tm, D_MODEL), _rows(tm, D_MODEL), _rows(tm, 2048), _cols(512, tm), _rows(tm, 1024),
                   pl.BlockSpec((1, 2048), lambda i: (0, 0))],
        out_shape=[jax.ShapeDtypeStruct((seq, D_MODEL), BF), jax.ShapeDtypeStruct((seq, D_MODEL), BF),
                   jax.ShapeDtypeStruct((seq, 2048), BF), jax.ShapeDtypeStruct((512, seq), BF),
                   jax.ShapeDtypeStruct((seq, 1024), BF), jax.ShapeDtypeStruct((1, 2048), F32)])


def _ret_bwd(qkr, vr, gr, ret, dyrp, states, gain, rtabs, cr, sr, comm=None):
    seq = qkr.shape[0]
    c = RET_CHUNK
    nc = seq // c
    dk, dv = RET_KEY_DIM, RET_VAL_DIM

    def body(qk_ref, v_ref, g_ref, ret_ref, dyp_ref, st_ref, gain_ref, intra_ref, qd_ref, kd_ref, cr_ref, sr_ref,
             dall_ref, dgain_ref, dstate):
        @pl.when(pl.program_id(0) == 0)
        def _():
            dstate[...] = jnp.zeros_like(dstate)
            dgain_ref[...] = jnp.zeros_like(dgain_ref)

        crv, srv = cr_ref[...], sr_ref[...]
        for h in range(RET_HEADS):
            vs = slice(dv * h, dv * (h + 1))
            qh = qk_ref[:, dk * h:dk * (h + 1)]
            kh = qk_ref[:, 512 + dk * h:512 + dk * (h + 1)]
            vh = v_ref[:, vs]
            out = ret_ref[:, vs]
            g = g_ref[:, vs].astype(F32)
            dyp = dyp_ref[:, vs].astype(F32)
            gain_h = gain_ref[:, vs]
            mu = jnp.mean(out, axis=-1, keepdims=True)
            d = out - mu
            rstd = lax.rsqrt(jnp.mean(d * d, axis=-1, keepdims=True) + GN_EPS)
            yn = d * rstd
            sg = _sigmoid(g)
            dg = dyp * (yn * gain_h) * (sg * (1.0 + g * (1.0 - sg)))
            dy = dyp * (g * sg)
            dgain_ref[:, vs] += jnp.sum(dy * yn, axis=0, keepdims=True)
            dyn = dy * gain_h
            dout = rstd * (dyn - jnp.mean(dyn, axis=-1, keepdims=True)
                           - yn * jnp.mean(dyn * yn, axis=-1, keepdims=True))
            doutb = dout.astype(BF)
            sc = st_ref[0, h]
            dsp = dstate[h]
            dspb = dsp.astype(BF)
            intra, qdv, kdv = intra_ref[h], qd_ref[h], kd_ref[h]
            att = _nt(qh, kh) * intra
            dab = (_nt(doutb, vh) * intra).astype(BF)
            qdec = (qh.astype(F32) * qdv).astype(BF)
            kdec = (kh.astype(F32) * kdv).astype(BF)
            dq = _nn(dab, kh) + _nt(doutb, sc) * qdv
            dkk = _tn(dab, qh) + _nt(vh, dspb) * kdv
            dvv = _tn(att.astype(BF), doutb) + _nn(kdec, dspb)
            dstate[h] = dsp * RET_CHUNK_DECAY[h] + _tn(qdec, doutb)
            dall_ref[:, dk * h:dk * (h + 1)] = _rope_ret_t(dq, crv, srv).astype(BF)
            dall_ref[:, 512 + dk * h:512 + dk * (h + 1)] = (_rope_ret_t(dkk, crv, srv) * RET_SCALE).astype(BF)
            dall_ref[:, 1024 + dv * h:1024 + dv * (h + 1)] = dvv.astype(BF)
            dall_ref[:, 2048 + dv * h:2048 + dv * (h + 1)] = dg.astype(BF)

    def rev(width):
        return pl.BlockSpec((c, width), lambda i: (nc - 1 - i, 0))

    return _call(
        body, (qkr, vr, gr, ret, dyrp, states, gain, *rtabs, cr, sr), comm, name="ret_bwd", grid=(nc,),
        in_specs=[rev(1024), rev(1024), rev(1024), rev(1024), rev(1024),
                  pl.BlockSpec((1, RET_HEADS, dk, dv), lambda i: (nc - 1 - i, 0, 0, 0)),
                  VMEM_FULL, VMEM_FULL, VMEM_FULL, VMEM_FULL, rev(LANE), rev(LANE)],
        out_specs=[rev(3072), pl.BlockSpec((1, 1024), lambda i: (0, 0))],
        out_shape=[jax.ShapeDtypeStruct((seq, 3072), BF), jax.ShapeDtypeStruct((1, 1024), F32)],
        scratch_shapes=[pltpu.VMEM((RET_HEADS, dk, dv), F32)])


def _attn_bwd(qt, kv, kvt, dot, sinks, ca, sa, sb, ct, st, comm=None):
    seq = kv.shape[0]
    c = ATT_BLOCK
    d = ATT_HEAD_DIM
    nb = seq // c

    def body(sink_ref, qt_ref, kvc_ref, kvp_ref, kvtc_ref, kvtp_ref, dot_ref, cap_ref, sap_ref, sbp_ref,
             ctc_ref, stc_ref, dqt_ref, dkv_ref, dsink_ref, carry):
        i = pl.program_id(0)

        @pl.when(i == 0)
        def _():
            carry[...] = jnp.zeros_like(carry)
            dsink_ref[...] = jnp.zeros_like(dsink_ref)

        def flush(total):
            dk_pre = _rope_att_t(total[:, :LANE], cap_ref[...], sap_ref[...], sbp_ref[...])
            dkv_ref[:, :LANE] = dk_pre.astype(BF)
            dkv_ref[:, LANE:] = total[:, LANE:].astype(BF)

        @pl.when(i < nb)
        def _():
            kwin = jnp.concatenate([kvp_ref[:, :LANE], kvc_ref[:, :LANE]], axis=0)
            vwin = jnp.concatenate([kvp_ref[:, LANE:], kvc_ref[:, LANE:]], axis=0)
            ktwin = jnp.concatenate([kvtp_ref[:LANE, :], kvtc_ref[:LANE, :]], axis=1)
            ctc, stc = ctc_ref[...], stc_ref[...]
            lane1 = lax.broadcasted_iota(jnp.int32, (1, LANE), 1)
            dkw = jnp.zeros((2 * c, LANE), F32)
            dvw = jnp.zeros((2 * c, LANE), F32)
            dsink = jnp.zeros((1, LANE), F32)
            for g in (0, 1):
                qpad, dopad = _att_group(qt_ref, g), _att_group(dot_ref, g)
                prob, psink = _att_probs(kwin, qpad, sink_ref, g, i)
                dprob = _nn(vwin, dopad)
                drow = jnp.sum(dprob * prob, axis=0, keepdims=True)
                ds = (prob * (dprob - drow) * ATT_SCALE).astype(BF)
                dqg = _nn(ktwin[d * g:d * (g + 1), :], ds)
                dkw = dkw + _nt(ds, qpad)
                dvw = dvw + _nt(prob.astype(BF), dopad)
                dsink_lanes = psink * drow
                for j in range(ATT_Q_PER_KV):
                    h = ATT_Q_PER_KV * g + j
                    dqt_ref[d * h:d * (h + 1), :] = _rope_att_rows(dqg[:, c * j:c * (j + 1)], ctc, stc, -1.0).astype(BF)
                    dsink = dsink + jnp.where(lane1 == h, -jnp.sum(dsink_lanes[:, c * j:c * (j + 1)]), 0.0)
            dsink_ref[...] += dsink
            flush(carry[...] + jnp.concatenate([dkw[:c], dvw[:c]], axis=1))
            carry[...] = jnp.concatenate([dkw[c:], dvw[c:]], axis=1)

        @pl.when(i == nb)
        def _():
            flush(carry[...])

    cur = lambda i: jnp.minimum(i, nb - 1)
    prev = lambda i: jnp.clip(i - 1, 0, nb - 1)
    rows_cur = lambda w: pl.BlockSpec((c, w), lambda i: (cur(i), 0))
    rows_prev = lambda w: pl.BlockSpec((c, w), lambda i: (prev(i), 0))
    cols_cur = lambda h: pl.BlockSpec((h, c), lambda i: (0, cur(i)))
    cols_prev = lambda h: pl.BlockSpec((h, c), lambda i: (0, prev(i)))
    return _call(
        body, (sinks, qt, kv, kv, kvt, kvt, dot, ca, sa, sb, ct, st), comm, name="attn_bwd", grid=(nb + 1,),
        in_specs=[SMEM_FULL, cols_cur(512), rows_cur(256), rows_prev(256), cols_cur(256), cols_prev(256),
                  cols_cur(512), rows_prev(LANE), rows_prev(LANE), rows_prev(LANE), cols_cur(8), cols_cur(8)],
        out_specs=[cols_cur(512), rows_prev(256), pl.BlockSpec((1, LANE), lambda i: (0, 0))],
        out_shape=[jax.ShapeDtypeStruct((512, seq), BF), jax.ShapeDtypeStruct((seq, 256), BF),
                   jax.ShapeDtypeStruct((1, LANE), F32)],
        scratch_shapes=[pltpu.VMEM((c, 256), F32)])


def _inproj_bwd(dqa, dkva, dret, dgates, dh1, x, g1, w_in, comm=None):
    seq = x.shape[0]
    tm = min(TOKEN_TILE, seq)

    def body(dqa_ref, dkva_ref, dret_ref, dgates_ref, dh1_ref, x_ref, g_ref, w_hbm, dx_ref, dg1_ref, w_vmem, w_sems):
        w = _Resident(w_hbm, w_vmem, w_sems, IN_CHUNKS_BWD)
        _Resident.load(w)

        @pl.when(pl.program_id(0) == 0)
        def _():
            dg1_ref[...] = jnp.zeros_like(dg1_ref)

        dxn = _tn(dqa_ref[...], w.chunk(0)[...])
        dxn = dxn + _nn(dkva_ref[...], w.chunk(1)[...])
        dxn = dxn + _nn(dret_ref[...], w.chunk(2)[...])
        dxn = dxn + _nn(dgates_ref[...], w.chunk(3)[...])
        dnorm, dg = _rms_bwd(dxn, x_ref[...], g_ref[...])
        dx_ref[...] = dh1_ref[...] + dnorm
        dg1_ref[...] += dg

    return _call(
        body, (dqa, dkva, dret, dgates, dh1, x, g1, w_in), comm, name="inproj_bwd", grid=(seq // tm,),
        in_specs=[_cols(512, tm), _rows(tm, 256), _rows(tm, 3072), _rows(tm, 2048), _rows(tm, D_MODEL),
                  _rows(tm, D_MODEL), VMEM_FULL, ANY],
        out_specs=[_rows(tm, D_MODEL), pl.BlockSpec((1, D_MODEL), lambda i: (0, 0))],
        out_shape=[jax.ShapeDtypeStruct((seq, D_MODEL), F32), jax.ShapeDtypeStruct((1, D_MODEL), F32)],
        scratch_shapes=_Resident.scratch(w_in, IN_CHUNKS_BWD))


def _local_step(x, target, g1, bg, sinks, gain, g2, g3, w_in, shards):
    seq = x.shape[0]
    ca, sa, sb, cr, sr, ct, st = _rope_tables(seq)
    rtabs = _ret_tables()
    scatter = _Scatter

    comm_mid, unpack_mid = _gather(shards, ("w_att_up", "w_ret_up", "w_out"))
    comm_ff1, unpack_ff1 = _gather(shards, ("w_ff1",))
    (xn1, qt, kv, kvt, qkr, vr, gr, gates), got = _inproj_fwd(x, g1, w_in, (ca, sa, sb, cr, sr, ct, st),
                                                              _join(comm_mid, comm_ff1))
    w_att, w_ret, w_out = unpack_mid(got[:3])
    (w_ff1,) = unpack_ff1(got[3:])
    comm, unpack = _gather(shards, ("w_ff2",))
    oa, got = _attn_fwd(qt, kv, kvt, sinks, comm)
    (w_ff2,) = unpack(got)
    (ret, yrp, states), _ = _ret_fwd(qkr, vr, gr, gain, rtabs)
    ya, yr, merged, h1 = _mix_fwd(oa, yrp, gates, bg, x, w_att, w_ret, w_out)
    xn2, u, dh2, loss, dg3 = _mlp_fwd_loss(h1, g2, g3, target, w_ff1, w_ff2)

    du, dh1, dg2 = _mlp_bwd(dh2, u, h1, g2, w_ff1, w_ff2)
    ff2 = scatter(dict(w_ff2=_mm_tn(u, dh2, "dw_ff2", relu_sq=True)), "ff2")
    d_ff1, r1 = _mm_tn(du, xn2, "dw_ff1", comm=ff2.pair_comm())
    ff2.pair_done(r1)
    ff1 = scatter(dict(w_ff1=d_ff1), "ff1")
    (dya, dyr, dgates, doa, dyrp, db), got = _mix_bwd(dh1, ya, yr, gates, bg, w_att, w_ret, w_out,
                                                      _join(ff1.pair_comm(), ff2.chip_comm()))
    ff1.pair_done(got[:1])
    ff2.chip_done(got[1:])
    mid = scatter(dict(w_att_up=_mm_tn(oa, dya, "dw_att_up", a_is_t=True), w_ret_up=_mm_tn(yrp, dyr, "dw_ret_up"),
                       w_out=_mm_tn(merged, dh1, "dw_out")), "mid")
    (dqa, dkva, dsink), r2 = _attn_bwd(qt, kv, kvt, doa, sinks, ca, sa, sb, ct, st, ff1.chip_comm())
    ff1.chip_done(r2)
    (dret, dgain), r1 = _ret_bwd(qkr, vr, gr, ret, dyrp, states, gain, rtabs, cr, sr, mid.pair_comm())
    mid.pair_done(r1)
    d_in, r2 = _dw_in_t((dqa, dkva, dret, dgates), (True, False, False, False), xn1, mid.chip_comm())
    mid.chip_done(r2)
    win = scatter(dict(w_in=d_in), "in")
    win.pair_done(_comm_alone(win.pair_comm(), "pair_exchange_in"))
    (dx, dg1), r2 = _inproj_bwd(dqa, dkva, dret, dgates, dh1, x, g1, w_in, win.chip_comm())
    win.chip_done(r2)
    small = dict(norm_mix_gain=dg1, b_gates=db, attn_sinks=dsink, ret_gn_gain=dgain, norm_mlp_gain=dg2,
                 norm_final_gain=dg3)
    return loss, dx, small, (win, mid, ff1, ff2)


def _coords():
    return lax.axis_index("x"), lax.axis_index("y"), lax.axis_index("c")


def _flip(v, bit):
    return 1 - v if bit else v


def _xor(a, b):
    return a + b - 2 * a * b


def _gather_relay_comm(shards):
    n = len(shards)

    def parts(ins, outs, sems):
        send_sems, recv_sems, local_sems = sems
        x, y, c = _coords()
        me, sib = (x, y, c), (x, y, 1 - c)

        def ring(core):
            return ((_xor(x, core), _xor(y, 1 - core), core), (_xor(x, 1 - core), _xor(y, core), core),
                    (1 - x, 1 - y, core))

        def slot(a, blk):
            return outs[a].at[4 * blk[0] + 2 * blk[1] + blk[2]]

        def copy(a, k, blk, to, src=None):
            return pltpu.make_async_remote_copy(
                src_ref=slot(a, blk) if src is None else src, dst_ref=slot(a, blk),
                send_sem=send_sems.at[a, k], recv_sem=recv_sems.at[a, k], device_id=to, device_id_type=MESH)

        mine = [pltpu.make_async_copy(ins[a], slot(a, me), local_sems.at[a]) for a in range(n)]
        return me, sib, ring, copy, mine

    def first_copies(ins, outs, sems):
        me, sib, ring, copy, mine = parts(ins, outs, sems)
        src, dst, _ = ring(me[2])
        return mine, [copy(a, k, me, to, src=ins[a]) for a in range(n) for k, to in ((1, src), (2, dst), (0, sib))]

    def start(ins, outs, sems):
        mine, first = first_copies(ins, outs, sems)
        for cp in mine + first:
            cp.start()

    def finish(ins, outs, sems):
        me, sib, ring, copy, _ = parts(ins, outs, sems)
        mine, first = first_copies(ins, outs, sems)
        src, dst, diag = ring(me[2])
        passed = []
        for k, blk, onward in ((1, src, ((3, dst), (4, sib))), (2, dst, ((5, sib),)), (3, diag, ((6, sib),))):
            for a in range(n):
                copy(a, k, blk, me).wait_recv()
                for k2, to in onward:
                    fwd = copy(a, k2, blk, to)
                    fwd.start()
                    passed.append(fwd)
        s_src, s_dst, s_diag = ring(sib[2])
        for a in range(n):
            for k, blk in ((0, sib), (4, s_src), (5, s_dst), (6, s_diag)):
                copy(a, k, blk, me).wait_recv()
        for cp in first + passed:
            cp.wait_send()
        for cp in mine:
            cp.wait()

    return _Comm(list(shards), [jax.ShapeDtypeStruct((N_DEV,) + s.shape, s.dtype) for s in shards],
                 [pltpu.SemaphoreType.DMA((n, 7)), pltpu.SemaphoreType.DMA((n, 7)), pltpu.SemaphoreType.DMA((n,))],
                 start, finish)


def _gather_comm(shards):
    n = len(shards)

    def parts(ins, outs, sems):
        send_sems, recv_sems, local_sems = sems
        x, y, c = _coords()
        me, sib = (x, y, c), (x, y, 1 - c)
        chips = [(1 - x, y), (x, 1 - y), (1 - x, 1 - y)]

        def slot(a, blk):
            return outs[a].at[4 * blk[0] + 2 * blk[1] + blk[2]]

        def copy(a, k, blk, to, src=None):
            return pltpu.make_async_remote_copy(
                src_ref=slot(a, blk) if src is None else src, dst_ref=slot(a, blk),
                send_sem=send_sems.at[a, k], recv_sem=recv_sems.at[a, k], device_id=to, device_id_type=MESH)

        mine = [pltpu.make_async_copy(ins[a], slot(a, me), local_sems.at[a]) for a in range(n)]
        first = []
        for a in range(n):
            first.append(copy(a, 0, me, sib, src=ins[a]))
            first += [copy(a, 1 + j, me, (*chip, c), src=ins[a]) for j, chip in enumerate(chips)]
        return me, sib, chips, c, copy, mine, first

    def start(ins, outs, sems):
        *_, mine, first = parts(ins, outs, sems)
        for cp in mine + first:
            cp.start()

    def finish(ins, outs, sems):
        me, sib, chips, c, copy, mine, first = parts(ins, outs, sems)
        passed = []
        for j, chip in enumerate(chips):
            for a in range(n):
                copy(a, 1 + j, (*chip, c), me).wait_recv()
                fwd = copy(a, 4 + j, (*chip, c), sib)
                fwd.start()
                passed.append(fwd)
        for a in range(n):
            copy(a, 0, sib, me).wait_recv()
            for j, chip in enumerate(chips):
                copy(a, 4 + j, (*chip, 1 - c), me).wait_recv()
        for cp in first + passed:
            cp.wait_send()
        for cp in mine:
            cp.wait()

    return _Comm(list(shards), [jax.ShapeDtypeStruct((N_DEV,) + s.shape, s.dtype) for s in shards],
                 [pltpu.SemaphoreType.DMA((n, 7)), pltpu.SemaphoreType.DMA((n, 7)), pltpu.SemaphoreType.DMA((n,))],
                 start, finish)


COLUMN_SHARDED = ("w_in", "w_ff1")
COLUMN_RELAID = ("w_att_up",)


def _gather(shards, names, relay=False):
    def unpack(got):
        return [jnp.transpose(g, (1, 0, 2)).reshape(g.shape[1], N_DEV * g.shape[2]) if k in COLUMN_RELAID
                else g.reshape(N_DEV * g.shape[1], g.shape[2]) for k, g in zip(names, got)]

    return (_gather_relay_comm if relay else _gather_comm)([shards[k] for k in names]), unpack


def _pair_comm(grads):
    n = len(grads)

    def copies(g, r1, sems):
        send_sems, recv_sems = sems
        x, y, c = _coords()
        return [pltpu.make_async_remote_copy(
            src_ref=g[a].at[2 * j + (1 - c)], dst_ref=r1[a].at[j], send_sem=send_sems.at[a, j],
            recv_sem=recv_sems.at[a, j], device_id=(x, y, 1 - c), device_id_type=MESH)
            for a in range(n) for j in range(4)]

    def start(g, r1, sems):
        for cp in copies(g, r1, sems):
            cp.start()

    def finish(g, r1, sems):
        for cp in copies(g, r1, sems):
            cp.wait_recv()
        for cp in copies(g, r1, sems):
            cp.wait_send()

    return _Comm(list(grads), [jax.ShapeDtypeStruct((4,) + g.shape[1:], g.dtype) for g in grads],
                 [pltpu.SemaphoreType.DMA((n, 4)), pltpu.SemaphoreType.DMA((n, 4))], start, finish)


def _small_comm(small):
    def parts(ins, outs, sems):
        (small_ref,), (small_all,) = ins, outs
        ssend, srecv, lsem = sems
        x, y, c = _coords()
        me_idx = 4 * x + 2 * y + c
        own = pltpu.make_async_copy(small_ref, small_all.at[me_idx], lsem)
        sends, recvs = [], []
        for r in range(1, N_DEV):
            px, py, pc = _flip(x, r & 4), _flip(y, r & 2), _flip(c, r & 1)
            sends.append(pltpu.make_async_remote_copy(
                src_ref=small_ref, dst_ref=small_all.at[me_idx], send_sem=ssend.at[r - 1], recv_sem=srecv.at[r - 1],
                device_id=(px, py, pc), device_id_type=MESH))
            recvs.append(pltpu.make_async_remote_copy(
                src_ref=small_ref, dst_ref=small_all.at[4 * px + 2 * py + pc], send_sem=ssend.at[r - 1],
                recv_sem=srecv.at[r - 1], device_id=(px, py, pc), device_id_type=MESH))
        return own, sends, recvs

    def start(ins, outs, sems):
        own, sends, _ = parts(ins, outs, sems)
        own.start()
        for cp in sends:
            cp.start()

    def finish(ins, outs, sems):
        own, sends, recvs = parts(ins, outs, sems)
        for cp in recvs:
            cp.wait_recv()
        for cp in sends:
            cp.wait_send()
        own.wait()

    return _Comm([small], [jax.ShapeDtypeStruct((N_DEV,) + small.shape, small.dtype)],
                 [pltpu.SemaphoreType.DMA((N_DEV - 1,)), pltpu.SemaphoreType.DMA((N_DEV - 1,)),
                  pltpu.SemaphoreType.DMA], start, finish)


def _pair_sum(grads, r1s, c_arr, tag):
    n = len(grads)
    q = 4

    def body(c_ref, *refs):
        g, r, t = refs[:n], refs[n:2 * n], refs[2 * n:]
        for a in range(n):
            t[a][...] = (g[a][...] + r[a][...]).astype(t[a].dtype)

    def blk(arr):
        return (1, arr.shape[1] // q, arr.shape[2])

    grid_spec = pltpu.PrefetchScalarGridSpec(
        num_scalar_prefetch=1, grid=(4, q),
        in_specs=[pl.BlockSpec(blk(g), lambda j, s, c_ref: (2 * j + c_ref[0], s, 0)) for g in grads]
        + [pl.BlockSpec(blk(r), lambda j, s, c_ref: (j, s, 0)) for r in r1s],
        out_specs=[pl.BlockSpec(blk(r), lambda j, s, c_ref: (j, s, 0)) for r in r1s])
    return pl.pallas_call(
        body, name="pair_sum_" + tag, grid_spec=grid_spec,
        out_shape=[jax.ShapeDtypeStruct(r.shape, RS_PAYLOAD) for r in r1s],
        compiler_params=_params(2),
    )(c_arr, *grads, *r1s)


def _chip_comm(ts):
    n = len(ts)

    def copies(t, r2, sems):
        send_sems, recv_sems = sems
        x, y, c = _coords()
        out = []
        for a in range(n):
            for r in range(1, 4):
                tx, ty = _flip(x, r & 2), _flip(y, r & 1)
                out.append(pltpu.make_async_remote_copy(
                    src_ref=t[a].at[2 * tx + ty], dst_ref=r2[a].at[r - 1], send_sem=send_sems.at[a, r - 1],
                    recv_sem=recv_sems.at[a, r - 1], device_id=(tx, ty, c), device_id_type=MESH))
        return out

    def start(t, r2, sems):
        for cp in copies(t, r2, sems):
            cp.start()

    def finish(t, r2, sems):
        for cp in copies(t, r2, sems):
            cp.wait_recv()
        for cp in copies(t, r2, sems):
            cp.wait_send()

    return _Comm(list(ts), [jax.ShapeDtypeStruct((3,) + t.shape[1:], t.dtype) for t in ts],
                 [pltpu.SemaphoreType.DMA((n, 3)), pltpu.SemaphoreType.DMA((n, 3))], start, finish)


class _Scatter:
    def __init__(self, grads, tag):
        self.names, self.tag = tuple(grads), tag
        self.stacks = [jnp.transpose(g.reshape(g.shape[0], N_DEV, g.shape[1] // N_DEV), (1, 0, 2))
                       if k in COLUMN_RELAID else g.reshape(N_DEV, g.shape[0] // N_DEV, g.shape[1])
                       for k, g in grads.items()]

    def pair_comm(self):
        return _pair_comm(self.stacks)

    def pair_done(self, r1s):
        self.r1s = list(r1s)
        c_arr = jnp.reshape(lax.axis_index("c"), (1,)).astype(jnp.int32)
        self.ts = _pair_sum(self.stacks, self.r1s, c_arr, self.tag)

    def chip_comm(self):
        return _chip_comm(self.ts)

    def chip_done(self, r2s):
        self.r2s = list(r2s)


def _adamw(w, g, m, v):
    m = ADAM_B1 * m + (1.0 - ADAM_B1) * g
    v = ADAM_B2 * v + (1.0 - ADAM_B2) * jnp.square(g)
    m_hat = m / (1.0 - ADAM_B1 ** ADAM_STEP)
    v_hat = v / (1.0 - ADAM_B2 ** ADAM_STEP)
    delta = -ADAM_LR * (m_hat / (jnp.sqrt(v_hat) + ADAM_EPS) + ADAM_WD * w)
    return delta, m, v


ADAM_STEPS = 4


def _piece_specs(stacks):
    def rows(s):
        return s.shape[1] // ADAM_STEPS
    return ([pl.BlockSpec((1, rows(s), s.shape[2]), lambda i, idx_ref: (idx_ref[0], i, 0)) for s in stacks]
            + [pl.BlockSpec((1, rows(s), s.shape[2]), lambda i, idx_ref: (idx_ref[1], i, 0)) for s in stacks]
            + [pl.BlockSpec((3, rows(s), s.shape[2]), lambda i, idx_ref: (0, i, 0)) for s in stacks])


def _piece_sum(g0, r1, r2):
    return (((g0[0] + r1[0]) + r2[0].astype(F32)) + r2[1].astype(F32)) + r2[2].astype(F32)


def _shard_sum(stacks, r1s, r2s, idx_arr):
    n = len(stacks)

    def body(idx_ref, *refs):
        g0, r1, r2, outs = (refs[k * n:(k + 1) * n] for k in range(4))
        for a in range(n):
            outs[a][...] = _piece_sum(g0[a], r1[a], r2[a])

    out_specs = [pl.BlockSpec((s.shape[1] // ADAM_STEPS, s.shape[2]), lambda i, idx_ref: (i, 0)) for s in stacks]
    grid_spec = pltpu.PrefetchScalarGridSpec(num_scalar_prefetch=1, grid=(ADAM_STEPS,),
                                             in_specs=_piece_specs(stacks), out_specs=out_specs)
    return pl.pallas_call(
        body, name="shard_sum", grid_spec=grid_spec,
        out_shape=[jax.ShapeDtypeStruct(s.shape[1:], F32) for s in stacks],
        compiler_params=_params(1),
    )(idx_arr, *stacks, *r1s, *r2s)


def _adam_big(pieces, summed, ws, ms, vs, idx_arr):
    stacks, r1s, r2s = pieces
    n_p, n = len(stacks), len(ws)

    def body(idx_ref, *refs):
        it = iter(refs)
        g0, r1, r2, gs, w, m, v = ([next(it) for _ in range(k)] for k in (n_p, n_p, n_p, n - n_p, n, n, n))
        outs = list(it)
        for a in range(n):
            g = _piece_sum(g0[a], r1[a], r2[a]) if a < n_p else gs[a - n_p][...]
            delta, nm, nv = _adamw(w[a][...], g, m[a][...], v[a][...])
            outs[4 * a][...] = g
            outs[4 * a + 1][...] = delta
            outs[4 * a + 2][...] = nm
            outs[4 * a + 3][...] = nv

    def shard_spec(w):
        return pl.BlockSpec((w.shape[0] // ADAM_STEPS, w.shape[1]), lambda i, idx_ref: (i, 0))

    in_specs = _piece_specs(stacks) + [shard_spec(w) for w in ws[n_p:]] + [shard_spec(w) for w in ws] * 3
    out_specs = [shard_spec(w) for w in ws for _ in range(4)]
    grid_spec = pltpu.PrefetchScalarGridSpec(num_scalar_prefetch=1, grid=(ADAM_STEPS,), in_specs=in_specs,
                                             out_specs=out_specs)
    return pl.pallas_call(
        body, name="adam_big", grid_spec=grid_spec,
        out_shape=[jax.ShapeDtypeStruct(w.shape, F32) for w in ws for _ in range(4)],
        compiler_params=_params(1),
    )(idx_arr, *stacks, *r1s, *r2s, *summed, *ws, *ms, *vs)


def _adam_small(small_all, w, m, v):
    def body(all_ref, w_ref, m_ref, v_ref, g_ref, d_ref, nm_ref, nv_ref):
        g = all_ref[0]
        for k in range(1, N_DEV):
            g = g + all_ref[k]
        delta, nm, nv = _adamw(w_ref[...], g, m_ref[...], v_ref[...])
        g_ref[...] = g
        d_ref[...] = delta
        nm_ref[...] = nm
        nv_ref[...] = nv

    return pl.pallas_call(
        body, name="adam_small",
        in_specs=[VMEM_FULL] * 4, out_specs=[VMEM_FULL] * 4,
        out_shape=[jax.ShapeDtypeStruct(w.shape, F32)] * 4,
    )(small_all, w, m, v)


RS_PAYLOAD = BF


def _pack_rows(rows):
    rid = lax.broadcasted_iota(jnp.int32, (8, D_MODEL), 0)
    out = jnp.zeros((8, D_MODEL), F32)
    for i, r in enumerate(rows):
        out = jnp.where(rid == i, jnp.broadcast_to(r, (8, D_MODEL)), out)
    return out


def _small_rows(norm_mix_gain, b_gates, attn_sinks, ret_gn_gain, norm_mlp_gain, norm_final_gain):
    return [norm_mix_gain, b_gates[:, :D_MODEL], b_gates[:, D_MODEL:], ret_gn_gain, norm_mlp_gain,
            norm_final_gain.reshape(1, D_MODEL), jnp.pad(attn_sinks, ((0, 0), (0, D_MODEL - ATT_HEADS)))]


def _unpack_small(p):
    return dict(norm_mix_gain=p[0:1], b_gates=jnp.concatenate([p[1:2], p[2:3]], axis=1), ret_gn_gain=p[3:4],
                norm_mlp_gain=p[4:5], norm_final_gain=p[5], attn_sinks=p[6:7, :ATT_HEADS])


WEIGHTS = ("norm_mix_gain", "w_in", "b_gates", "attn_sinks", "ret_gn_gain", "w_att_up", "w_ret_up", "w_out",
           "norm_mlp_gain", "w_ff1", "w_ff2", "norm_final_gain")
BIG = ("w_in", "w_att_up", "w_ret_up", "w_out", "w_ff1", "w_ff2")


def kernel(x, norm_mix_gain, w_in, b_gates, attn_sinks, ret_gn_gain, w_att_up, w_ret_up, w_out, norm_mlp_gain, w_ff1, w_ff2, norm_final_gain, loss_target, m_norm_mix_gain, m_w_in, m_b_gates, m_attn_sinks, m_ret_gn_gain, m_w_att_up, m_w_ret_up, m_w_out, m_norm_mlp_gain, m_w_ff1, m_w_ff2, m_norm_final_gain, v_norm_mix_gain, v_w_in, v_b_gates, v_attn_sinks, v_ret_gn_gain, v_w_att_up, v_w_ret_up, v_w_out, v_norm_mlp_gain, v_w_ff1, v_w_ff2, v_norm_final_gain):
    w = dict(norm_mix_gain=norm_mix_gain, w_in=w_in, b_gates=b_gates, attn_sinks=attn_sinks, ret_gn_gain=ret_gn_gain,
             w_att_up=w_att_up, w_ret_up=w_ret_up, w_out=w_out, norm_mlp_gain=norm_mlp_gain, w_ff1=w_ff1,
             w_ff2=w_ff2, norm_final_gain=norm_final_gain)
    m = dict(norm_mix_gain=m_norm_mix_gain, w_in=m_w_in, b_gates=m_b_gates, attn_sinks=m_attn_sinks,
             ret_gn_gain=m_ret_gn_gain, w_att_up=m_w_att_up, w_ret_up=m_w_ret_up, w_out=m_w_out,
             norm_mlp_gain=m_norm_mlp_gain, w_ff1=m_w_ff1, w_ff2=m_w_ff2, norm_final_gain=m_norm_final_gain)
    v = dict(norm_mix_gain=v_norm_mix_gain, w_in=v_w_in, b_gates=v_b_gates, attn_sinks=v_attn_sinks,
             ret_gn_gain=v_ret_gn_gain, w_att_up=v_w_att_up, w_ret_up=v_w_ret_up, w_out=v_w_out,
             norm_mlp_gain=v_norm_mlp_gain, w_ff1=v_w_ff1, w_ff2=v_w_ff2, norm_final_gain=v_norm_final_gain)

    shards = {k: (w[k][0].T if k in COLUMN_SHARDED else w[k][0]).astype(BF) for k in BIG}
    comm, unpack = _gather(shards, ("w_in",), relay=True)
    (w_in_full,) = unpack(_comm_alone(comm, "allgather_w_in"))

    loss_p, dx, small, groups = _local_step(
        x[0], loss_target[0], norm_mix_gain, b_gates, attn_sinks[0], ret_gn_gain, norm_mlp_gain,
        norm_final_gain.reshape(1, D_MODEL), w_in_full, shards)

    by_name = {}
    for grp in groups:
        for k, stack, r1, r2 in zip(grp.names, grp.stacks, grp.r1s, grp.r2s):
            by_name[k] = (stack, r1, r2)
    lane = lax.broadcasted_iota(jnp.int32, (1, D_MODEL), 1)
    sink_row = jnp.where(lane < ATT_HEADS, jnp.pad(small["attn_sinks"], ((0, 0), (0, D_MODEL - LANE))),
                         jnp.where(lane == ATT_HEADS, jnp.pad(loss_p, ((0, 0), (0, D_MODEL - LANE)), mode="edge"),
                                   0.0))
    small_pack = _pack_rows([small["norm_mix_gain"], small["b_gates"][:, :D_MODEL], small["b_gates"][:, D_MODEL:],
                             small["ret_gn_gain"], small["norm_mlp_gain"], small["norm_final_gain"], sink_row])
    xi, yi, ci = _coords()
    (small_all,) = _comm_alone(_small_comm(small_pack), "small_exchange")
    idx_arr = jnp.stack([4 * xi + 2 * yi + ci, 2 * xi + yi]).astype(jnp.int32)
    in_pieces = ["w_att_up", "w_ret_up", "w_out", "w_ff2", "w_in"]
    in_sum = ["w_ff1"]
    order = in_pieces + in_sum
    summed_t = _shard_sum(*[[by_name[k][i] for k in in_sum] for i in range(3)], idx_arr)

    def shard(tree, k):
        return tree[k][0].T if k == "w_in" else tree[k][0]

    adam_out = _adam_big([[by_name[k][i] for k in in_pieces] for i in range(3)], [g.T for g in summed_t],
                         [shard(w, k) for k in order], [shard(m, k) for k in order], [shard(v, k) for k in order],
                         idx_arr)
    big_out = [adam_out[4 * order.index(k) + i].T if k == "w_in" else adam_out[4 * order.index(k) + i]
               for k in BIG for i in range(4)]
    sm_g, sm_d, sm_m, sm_v = _adam_small(small_all, _pack_rows(_small_rows(*[w[k] for k in WEIGHTS if k not in BIG])),
                                         _pack_rows(_small_rows(*[m[k] for k in WEIGHTS if k not in BIG])),
                                         _pack_rows(_small_rows(*[v[k] for k in WEIGHTS if k not in BIG])))

    loss = sm_g[6, ATT_HEADS]
    kinds = []
    for idx, packed in enumerate((sm_g, sm_d, sm_m, sm_v)):
        out = _unpack_small(packed)
        for a, k in enumerate(BIG):
            out[k] = big_out[4 * a + idx][None]
        kinds.append(out)
    return (loss, dx[None], *[kinds[0][k] for k in WEIGHTS], *[kinds[1][k] for k in WEIGHTS],
            *[kinds[2][k] for k in WEIGHTS], *[kinds[3][k] for k in WEIGHTS])
```

```python
import functools
import math

import jax
import jax.numpy as jnp
from jax import lax
from jax.experimental import pallas as pl
from jax.experimental.pallas import tpu as pltpu

F32 = jnp.float32
BF = jnp.bfloat16
MESH = pl.DeviceIdType.MESH

D_MODEL = 1024
ATT_HEADS = 8
ATT_HEAD_DIM = 64
ATT_BLOCK = 128
ROPE_DIM = 16
ROPE_THETA = 500000.0
RET_HEADS = 4
RET_KEY_DIM = 128
RET_VAL_DIM = 256
RET_CHUNK = 128
RET_ROT_BASE = 10000.0
D_FF = 4096
NORM_EPS = 1e-6
GN_EPS = 1e-6
NEG_INF = -1e30
ATT_SCALE = ATT_HEAD_DIM ** -0.5
RET_SCALE = RET_KEY_DIM ** -0.5

C_QA, C_KA, C_VA, C_QR, C_KR, C_VR, C_GR, C_GATES, C_END = 0, 512, 640, 768, 1280, 1792, 2816, 3840, 5888

ADAM_LR = 0.001
ADAM_B1 = 0.9
ADAM_B2 = 0.999
ADAM_EPS = 1e-08
ADAM_WD = 0.01
ADAM_STEP = 10

N_DEV = 8
LANE = 128
VMEM_LIMIT = 56 * 1024 * 1024
TOKEN_TILE = 256
TN_TOKEN_TILE = 1024
FF_CHUNKS = (0, 1024, 2048, 3072, 4096)
IN_CHUNKS_FWD = (C_QA, C_KA, C_QR, C_KR, C_VR, C_GR, C_GATES, C_END)
IN_CHUNKS_BWD = (C_QA, C_KA, C_QR, C_GATES, C_END)

RET_LOG_GAMMA = tuple(math.log1p(-(2.0 ** (-5.0 - h))) for h in range(RET_HEADS))
RET_CHUNK_DECAY = tuple(math.exp(RET_CHUNK * lg) for lg in RET_LOG_GAMMA)

VMEM_FULL = pl.BlockSpec(memory_space=pltpu.VMEM)
SMEM_FULL = pl.BlockSpec(memory_space=pltpu.SMEM)
ANY = pl.BlockSpec(memory_space=pl.ANY)


def _params(n_axes):
    return pltpu.CompilerParams(dimension_semantics=("arbitrary",) * n_axes, vmem_limit_bytes=VMEM_LIMIT)


def _nn(a, b):
    return jnp.dot(a, b, preferred_element_type=F32)


def _nt(a, b):
    return lax.dot_general(a, b, (((1,), (1,)), ((), ())), preferred_element_type=F32)


def _tn(a, b):
    return lax.dot_general(a, b, (((0,), (0,)), ((), ())), preferred_element_type=F32)


def _sigmoid(v):
    return 1.0 / (1.0 + jnp.exp(-v))


def _rows(tile, width):
    return pl.BlockSpec((tile, width), lambda i: (i, 0))


def _cols(height, tile):
    return pl.BlockSpec((height, tile), lambda i: (0, i))


class _Resident:
    def __init__(self, hbm_ref, vmem_ref, sems, bounds):
        self.hbm, self.vmem, self.sems, self.bounds = hbm_ref, vmem_ref, sems, bounds

    @staticmethod
    def scratch(w, bounds):
        return [pltpu.VMEM(w.shape, w.dtype), pltpu.SemaphoreType.DMA((len(bounds) - 1,))]

    @staticmethod
    def load(*weights):
        @pl.when(pl.program_id(0) == 0)
        def _():
            copies = [w._copy(c) for w in weights for c in range(len(w.bounds) - 1)]
            for cp in copies:
                cp.start()
            for cp in copies:
                cp.wait()

    def _rows(self, c):
        return pl.ds(self.bounds[c], self.bounds[c + 1] - self.bounds[c])

    def _copy(self, c):
        return pltpu.make_async_copy(self.hbm.at[self._rows(c)], self.vmem.at[self._rows(c)], self.sems.at[c])

    def chunk(self, c):
        return self.vmem.at[self._rows(c)]


class _Comm:
    def __init__(self, inputs, out_shapes, scratch, start, finish, relay=None):
        self.inputs, self.out_shapes, self.scratch = inputs, out_shapes, scratch
        self.start, self.finish, self.relay = start, finish, relay


RELAY_AT = 0.5


def _join(a, b):
    na_in, na_out, na_sem = len(a.inputs), len(a.out_shapes), len(a.scratch)

    def both(name):
        def run(ins, outs, sems):
            for part, args in ((a, (ins[:na_in], outs[:na_out], sems[:na_sem])),
                               (b, (ins[na_in:], outs[na_out:], sems[na_sem:]))):
                if getattr(part, name) is not None:
                    getattr(part, name)(*args)
        return run

    return _Comm(list(a.inputs) + list(b.inputs), list(a.out_shapes) + list(b.out_shapes),
                 list(a.scratch) + list(b.scratch), both("start"), both("finish"),
                 both("relay") if a.relay or b.relay else None)


def _call(body, args, comm=None, *, name, grid, in_specs, out_specs, out_shape, scratch_shapes=()):
    params = _params(len(grid))
    if comm is None:
        return pl.pallas_call(body, name=name, grid=grid, in_specs=in_specs, out_specs=out_specs, out_shape=out_shape,
                              scratch_shapes=scratch_shapes, compiler_params=params)(*args), ()
    single = not isinstance(out_specs, (list, tuple))
    out_specs_l = [out_specs] if single else list(out_specs)
    out_shape_l = [out_shape] if single else list(out_shape)
    n_in, n_out, n_scr = len(in_specs), len(out_specs_l), len(scratch_shapes)
    n_cin, n_cout = len(comm.inputs), len(comm.out_shapes)

    def hosted(*refs):
        it = iter(refs)
        ins, cin, outs, cout, scr = ([next(it) for _ in range(k)] for k in (n_in, n_cin, n_out, n_cout, n_scr))
        sems = list(it)
        ids = [pl.program_id(k) for k in range(len(grid))]
        first = functools.reduce(jnp.logical_and, [i == 0 for i in ids])
        last = functools.reduce(jnp.logical_and, [i == g - 1 for i, g in zip(ids, grid)])

        @pl.when(first)
        def _():
            comm.start(cin, cout, sems)

        if comm.relay is not None:
            at = [int(grid[0] * RELAY_AT)] + [0] * (len(grid) - 1)

            @pl.when(functools.reduce(jnp.logical_and, [i == v for i, v in zip(ids, at)]))
            def _():
                comm.relay(cin, cout, sems)

        body(*ins, *outs, *scr)

        @pl.when(last)
        def _():
            comm.finish(cin, cout, sems)

    res = pl.pallas_call(
        hosted, name=name, grid=grid, in_specs=list(in_specs) + [ANY] * n_cin,
        out_specs=out_specs_l + [ANY] * n_cout, out_shape=out_shape_l + list(comm.out_shapes),
        scratch_shapes=list(scratch_shapes) + list(comm.scratch), compiler_params=params)(*args, *comm.inputs)
    return (res[0] if single else res[:n_out]), res[n_out:]


def _comm_alone(comm, name):
    n_cin, n_cout = len(comm.inputs), len(comm.out_shapes)

    def body(*refs):
        cin, cout, sems = refs[:n_cin], refs[n_cin:n_cin + n_cout], refs[n_cin + n_cout:]
        comm.start(cin, cout, sems)
        if comm.relay is not None:
            comm.relay(cin, cout, sems)
        comm.finish(cin, cout, sems)

    return pl.pallas_call(body, name=name, in_specs=[ANY] * n_cin, out_specs=[ANY] * n_cout,
                          out_shape=list(comm.out_shapes), scratch_shapes=list(comm.scratch))(*comm.inputs)


def _slabs(v, fn):
    return jnp.concatenate([fn(v[:, LANE * j:LANE * (j + 1)]) for j in range(v.shape[1] // LANE)], axis=1)


def _rope_att(v, ca, sa, sb):
    return _slabs(v, lambda t: t * ca + pltpu.roll(t, LANE - 8, 1) * sa + pltpu.roll(t, 8, 1) * sb)


def _rope_att_t(v, ca, sa, sb):
    return _slabs(v, lambda t: t * ca + pltpu.roll(t * sa, 8, 1) + pltpu.roll(t * sb, LANE - 8, 1))


def _rope_att_rows(v, ct, st, sign):
    parts = []
    for h in range(v.shape[0] // ATT_HEAD_DIM):
        r0 = ATT_HEAD_DIM * h
        x1, x2 = v[r0:r0 + 8], v[r0 + 8:r0 + 16]
        parts += [x1 * ct - sign * (x2 * st), x2 * ct + sign * (x1 * st), v[r0 + 16:r0 + ATT_HEAD_DIM]]
    return jnp.concatenate(parts, axis=0)


def _rope_ret(v, cr, sr):
    return _slabs(v, lambda t: t * cr + pltpu.roll(t, 64, 1) * sr)


def _rope_ret_t(v, cr, sr):
    return _slabs(v, lambda t: t * cr + pltpu.roll(t * sr, 64, 1))


def _rope_tables(seq):
    pos = jnp.arange(seq, dtype=F32)

    def cs(dim, theta):
        inv = theta ** (-jnp.arange(0, dim, 2, dtype=F32) / dim)
        ang = pos[:, None] * inv[None, :]
        return jnp.cos(ang), jnp.sin(ang)

    ca, sa_ = cs(ROPE_DIM, ROPE_THETA)
    zeros8 = jnp.zeros_like(ca)
    rest = ATT_HEAD_DIM - ROPE_DIM
    c64 = jnp.concatenate([ca, ca, jnp.ones((seq, rest), F32)], axis=1)
    sa64 = jnp.concatenate([-sa_, zeros8, jnp.zeros((seq, rest), F32)], axis=1)
    sb64 = jnp.concatenate([zeros8, sa_, jnp.zeros((seq, rest), F32)], axis=1)
    cr, sr_ = cs(RET_KEY_DIM, RET_ROT_BASE)
    return (jnp.tile(c64, (1, 2)), jnp.tile(sa64, (1, 2)), jnp.tile(sb64, (1, 2)),
            jnp.concatenate([cr, cr], axis=1), jnp.concatenate([-sr_, sr_], axis=1), ca.T, sa_.T)


def _ret_tables():
    c = RET_CHUNK
    lg = jnp.asarray(RET_LOG_GAMMA, F32)
    idx = jnp.arange(c, dtype=F32)
    diff = idx[:, None] - idx[None, :]
    intra = jnp.where(diff >= 0, jnp.exp(jnp.maximum(diff, 0.0) * lg[:, None, None]), 0.0)
    qd = jnp.exp((idx + 1.0)[None, :] * lg[:, None])[..., None]
    kd = jnp.exp((c - 1.0 - idx)[None, :] * lg[:, None])[..., None]
    return intra, jnp.broadcast_to(qd, (RET_HEADS, c, RET_KEY_DIM)), jnp.broadcast_to(kd, (RET_HEADS, c, RET_KEY_DIM))


def _inproj_fwd(x, g1, w_in, tabs, comm=None):
    seq = x.shape[0]
    tm = min(TOKEN_TILE, seq)

    def body(x_ref, g_ref, w_hbm, ca_ref, sa_ref, sb_ref, cr_ref, sr_ref, ct_ref, st_ref,
             xn_ref, qt_ref, kv_ref, kvt_ref, qkr_ref, vr_ref, gr_ref, gates_ref, w_vmem, w_sems):
        w = _Resident(w_hbm, w_vmem, w_sems, IN_CHUNKS_FWD)
        _Resident.load(w)
        xf = x_ref[...]
        r = lax.rsqrt(jnp.mean(xf * xf, axis=-1, keepdims=True) + NORM_EPS)
        xn = (xf * r * g_ref[...]).astype(BF)
        xn_ref[...] = xn
        ct, st = ct_ref[...], st_ref[...]
        qt_ref[...] = _rope_att_rows(_nt(w.chunk(0)[...], xn), ct, st, 1.0).astype(BF)
        w_kv = w.chunk(1)
        kvt = _nt(w_kv[...], xn)
        kvt_ref[:LANE, :] = _rope_att_rows(kvt[:LANE], ct, st, 1.0).astype(BF)
        kvt_ref[LANE:, :] = kvt[LANE:].astype(BF)
        kvn = _nt(xn, w_kv[...])
        kv_ref[:, :LANE] = _rope_att(kvn[:, :LANE], ca_ref[...], sa_ref[...], sb_ref[...]).astype(BF)
        kv_ref[:, LANE:] = kvn[:, LANE:].astype(BF)
        cr, sr = cr_ref[...], sr_ref[...]
        qr = _rope_ret(_nt(xn, w.chunk(2)[...]), cr, sr)
        qkr_ref[:, :512] = qr.astype(BF)
        kr = _rope_ret(_nt(xn, w.chunk(3)[...]), cr, sr) * RET_SCALE
        qkr_ref[:, 512:] = kr.astype(BF)
        vr_ref[...] = _nt(xn, w.chunk(4)[...]).astype(BF)
        gr_ref[...] = _nt(xn, w.chunk(5)[...]).astype(BF)
        gates_ref[...] = _nt(xn, w.chunk(6)[...]).astype(BF)

    tab = _rows(tm, LANE)
    tab_t = _cols(8, tm)
    return _call(
        body, (x, g1, w_in, *tabs), comm, name="inproj_fwd", grid=(seq // tm,),
        in_specs=[_rows(tm, D_MODEL), VMEM_FULL, ANY, tab, tab, tab, tab, tab, tab_t, tab_t],
        out_specs=[_rows(tm, D_MODEL), _cols(512, tm), _rows(tm, 256), _cols(256, tm), _rows(tm, 1024),
                   _rows(tm, 1024), _rows(tm, 1024), _rows(tm, 2048)],
        out_shape=[jax.ShapeDtypeStruct(s, BF) for s in ((seq, D_MODEL), (512, seq), (seq, 256), (256, seq),
                                                         (seq, 1024), (seq, 1024), (seq, 1024), (seq, 2048))],
        scratch_shapes=_Resident.scratch(w_in, IN_CHUNKS_FWD))


ATT_Q_PER_KV = ATT_HEADS // 2


def _att_group(ref, kv):
    d = ATT_HEAD_DIM
    g = jnp.concatenate([ref[d * (ATT_Q_PER_KV * kv + j):d * (ATT_Q_PER_KV * kv + j + 1), :]
                         for j in range(ATT_Q_PER_KV)], axis=1)
    z = jnp.zeros_like(g)
    return jnp.concatenate([g, z] if kv == 0 else [z, g], axis=0)


def _att_probs(kwin, qpad, sink_ref, kv, i):
    c = ATT_BLOCK
    kj = lax.broadcasted_iota(jnp.int32, (2 * c, c), 0)
    qi = lax.broadcasted_iota(jnp.int32, (2 * c, c), 1)
    allowed = (kj > qi) & (kj <= qi + c) & ((kj >= c) | (i > 0))
    allowed = jnp.concatenate([allowed] * ATT_Q_PER_KV, axis=1)
    s = jnp.where(allowed, _nn(kwin, qpad) * ATT_SCALE, NEG_INF)
    sk = jnp.concatenate([jnp.full((1, c), sink_ref[ATT_Q_PER_KV * kv + j], F32) for j in range(ATT_Q_PER_KV)], axis=1)
    m = jnp.maximum(jnp.max(s, axis=0, keepdims=True), sk)
    pe = jnp.exp(s - m)
    psink = jnp.exp(sk - m)
    inv = 1.0 / (jnp.sum(pe, axis=0, keepdims=True) + psink)
    return pe * inv, psink * inv


def _attn_fwd(qt, kv, kvt, sinks, comm=None):
    seq = kv.shape[0]
    c = ATT_BLOCK
    d = ATT_HEAD_DIM
    nb = seq // c

    def body(sink_ref, qt_ref, kvc_ref, kvp_ref, kvtc_ref, kvtp_ref, ot_ref):
        i = pl.program_id(0)
        kwin = jnp.concatenate([kvp_ref[:, :LANE], kvc_ref[:, :LANE]], axis=0)
        vtwin = jnp.concatenate([kvtp_ref[LANE:, :], kvtc_ref[LANE:, :]], axis=1)
        for g in (0, 1):
            prob, _ = _att_probs(kwin, _att_group(qt_ref, g), sink_ref, g, i)
            out = _nn(vtwin[d * g:d * (g + 1), :], prob.astype(BF))
            for j in range(ATT_Q_PER_KV):
                h = ATT_Q_PER_KV * g + j
                ot_ref[d * h:d * (h + 1), :] = out[:, c * j:c * (j + 1)].astype(BF)

    prev = lambda i: jnp.maximum(i - 1, 0)
    return _call(
        body, (sinks, qt, kv, kv, kvt, kvt), comm, name="attn_fwd", grid=(nb,),
        in_specs=[SMEM_FULL, _cols(512, c), _rows(c, 256), pl.BlockSpec((c, 256), lambda i: (prev(i), 0)),
                  _cols(256, c), pl.BlockSpec((256, c), lambda i: (0, prev(i)))],
        out_specs=_cols(512, c),
        out_shape=jax.ShapeDtypeStruct((512, seq), BF))


def _ret_fwd(qkr, vr, gr, gain, rtabs, comm=None):
    seq = qkr.shape[0]
    c = RET_CHUNK
    nc = seq // c
    dk, dv = RET_KEY_DIM, RET_VAL_DIM

    def body(qk_ref, v_ref, g_ref, gain_ref, intra_ref, qd_ref, kd_ref, ret_ref, yrp_ref, st_ref, state):
        @pl.when(pl.program_id(0) == 0)
        def _():
            state[...] = jnp.zeros_like(state)

        for h in range(RET_HEADS):
            qh = qk_ref[:, dk * h:dk * (h + 1)]
            kh = qk_ref[:, 512 + dk * h:512 + dk * (h + 1)]
            vh = v_ref[:, dv * h:dv * (h + 1)]
            sh = state[h]
            shb = sh.astype(BF)
            st_ref[0, h] = shb
            att = _nt(qh, kh) * intra_ref[h]
            inner = _nn(att.astype(BF), vh)
            cross = _nn((qh.astype(F32) * qd_ref[h]).astype(BF), shb)
            out = inner + cross
            state[h] = sh * RET_CHUNK_DECAY[h] + _tn((kh.astype(F32) * kd_ref[h]).astype(BF), vh)
            ret_ref[:, dv * h:dv * (h + 1)] = out
            mu = jnp.mean(out, axis=-1, keepdims=True)
            d = out - mu
            var = jnp.mean(d * d, axis=-1, keepdims=True)
            y = d * lax.rsqrt(var + GN_EPS) * gain_ref[:, dv * h:dv * (h + 1)]
            g = g_ref[:, dv * h:dv * (h + 1)].astype(F32)
            yrp_ref[:, dv * h:dv * (h + 1)] = (g * _sigmoid(g) * y).astype(BF)

    return _call(
        body, (qkr, vr, gr, gain, *rtabs), comm, name="ret_fwd", grid=(nc,),
        in_specs=[_rows(c, 1024), _rows(c, 1024), _rows(c, 1024), VMEM_FULL, VMEM_FULL, VMEM_FULL, VMEM_FULL],
        out_specs=[_rows(c, 1024), _rows(c, 1024), pl.BlockSpec((1, RET_HEADS, dk, dv), lambda i: (i, 0, 0, 0))],
        out_shape=[jax.ShapeDtypeStruct((seq, 1024), F32), jax.ShapeDtypeStruct((seq, 1024), BF),
                   jax.ShapeDtypeStruct((nc, RET_HEADS, dk, dv), BF)],
        scratch_shapes=[pltpu.VMEM((RET_HEADS, dk, dv), F32)])


def _mix_fwd(oat, yrp, gates, bg, x, w_att, w_ret, w_out):
    seq = x.shape[0]
    tm = min(TOKEN_TILE, seq)

    def body(oat_ref, yrp_ref, gates_ref, bg_ref, x_ref, wa_ref, wr_ref, wo_ref, ya_ref, yr_ref, mg_ref, h1_ref):
        ya = _tn(oat_ref[...], wa_ref[...])
        yr = _nn(yrp_ref[...], wr_ref[...])
        gt = _sigmoid(gates_ref[...].astype(F32) + bg_ref[...])
        merged = (gt[:, :D_MODEL] * ya + gt[:, D_MODEL:] * yr).astype(BF)
        ya_ref[...] = ya.astype(BF)
        yr_ref[...] = yr.astype(BF)
        mg_ref[...] = merged
        h1_ref[...] = x_ref[...] + _nn(merged, wo_ref[...])

    return pl.pallas_call(
        body, name="mix_fwd", grid=(seq // tm,),
        in_specs=[_cols(512, tm), _rows(tm, 1024), _rows(tm, 2048), VMEM_FULL, _rows(tm, D_MODEL),
                  VMEM_FULL, VMEM_FULL, VMEM_FULL],
        out_specs=[_rows(tm, D_MODEL)] * 4,
        out_shape=[jax.ShapeDtypeStruct((seq, D_MODEL), BF)] * 3 + [jax.ShapeDtypeStruct((seq, D_MODEL), F32)],
        compiler_params=_params(1),
    )(oat, yrp, gates, bg, x, w_att, w_ret, w_out)


def _mlp_fwd_loss(h1, g2, g3, target, w_ff1, w_ff2):
    seq = h1.shape[0]
    tm = min(TOKEN_TILE, seq)

    def body(h1_ref, g2_ref, g3_ref, t_ref, w1_hbm, w2_hbm, xn2_ref, u_ref, dh2_ref, loss_ref, dg3_ref,
             w1_vmem, w1_sems, w2_vmem, w2_sems):
        w1 = _Resident(w1_hbm, w1_vmem, w1_sems, FF_CHUNKS)
        w2 = _Resident(w2_hbm, w2_vmem, w2_sems, FF_CHUNKS)
        _Resident.load(w1, w2)

        @pl.when(pl.program_id(0) == 0)
        def _():
            loss_ref[...] = jnp.zeros_like(loss_ref)
            dg3_ref[...] = jnp.zeros_like(dg3_ref)

        h1v = h1_ref[...]
        r2 = lax.rsqrt(jnp.mean(h1v * h1v, axis=-1, keepdims=True) + NORM_EPS)
        xn2 = (h1v * r2 * g2_ref[...]).astype(BF)
        xn2_ref[...] = xn2
        h2 = h1v
        for c in range(len(FF_CHUNKS) - 1):
            cols = slice(FF_CHUNKS[c], FF_CHUNKS[c + 1])
            u = _nt(xn2, w1.chunk(c)[...])
            u_ref[:, cols] = u.astype(BF)
            h2 = h2 + _nn(jnp.square(jnp.maximum(u, 0.0)).astype(BF), w2.chunk(c)[...])
        r3 = lax.rsqrt(jnp.mean(h2 * h2, axis=-1, keepdims=True) + NORM_EPS)
        hn = h2 * r3
        err = hn * g3_ref[...] - t_ref[...]
        loss_ref[...] += jnp.sum(err * err) * (0.5 / D_MODEL)
        dy = err * (1.0 / D_MODEL)
        dg3_ref[...] += jnp.sum(dy * hn, axis=0, keepdims=True)
        z = dy * g3_ref[...]
        dh2_ref[...] = r3 * (z - hn * jnp.mean(z * hn, axis=-1, keepdims=True))

    return pl.pallas_call(
        body, name="mlp_fwd_loss", grid=(seq // tm,),
        in_specs=[_rows(tm, D_MODEL), VMEM_FULL, VMEM_FULL, _rows(tm, D_MODEL), ANY, ANY],
        out_specs=[_rows(tm, D_MODEL), _rows(tm, D_FF), _rows(tm, D_MODEL),
                   pl.BlockSpec((1, LANE), lambda i: (0, 0)), pl.BlockSpec((1, D_MODEL), lambda i: (0, 0))],
        out_shape=[jax.ShapeDtypeStruct((seq, D_MODEL), BF), jax.ShapeDtypeStruct((seq, D_FF), BF),
                   jax.ShapeDtypeStruct((seq, D_MODEL), F32), jax.ShapeDtypeStruct((1, LANE), F32),
                   jax.ShapeDtypeStruct((1, D_MODEL), F32)],
        scratch_shapes=_Resident.scratch(w_ff1, FF_CHUNKS) + _Resident.scratch(w_ff2, FF_CHUNKS),
        compiler_params=_params(1),
    )(h1, g2, g3, target, w_ff1, w_ff2)


def _rms_bwd(dxn, xin, gain):
    r = lax.rsqrt(jnp.mean(xin * xin, axis=-1, keepdims=True) + NORM_EPS)
    xhat = xin * r
    z = dxn * gain
    dxin = r * (z - xhat * jnp.mean(z * xhat, axis=-1, keepdims=True))
    return dxin, jnp.sum(dxn * xhat, axis=0, keepdims=True)


def _mlp_bwd(dh2, u, h1, g2, w_ff1, w_ff2):
    seq = h1.shape[0]
    tm = min(TOKEN_TILE, seq)

    def body(dh2_ref, u_ref, h1_ref, g2_ref, w1_hbm, w2_hbm, du_ref, dh1_ref, dg2_ref,
             w1_vmem, w1_sems, w2_vmem, w2_sems):
        w2 = _Resident(w2_hbm, w2_vmem, w2_sems, FF_CHUNKS)
        w1 = _Resident(w1_hbm, w1_vmem, w1_sems, FF_CHUNKS)
        _Resident.load(w2, w1)

        @pl.when(pl.program_id(0) == 0)
        def _():
            dg2_ref[...] = jnp.zeros_like(dg2_ref)

        dh2v = dh2_ref[...]
        dh2b = dh2v.astype(BF)
        dxn2 = jnp.zeros_like(dh2v)
        for c in range(len(FF_CHUNKS) - 1):
            cols = slice(FF_CHUNKS[c], FF_CHUNKS[c + 1])
            dhdn = _nt(dh2b, w2.chunk(c)[...])
            du = (dhdn * (2.0 * jnp.maximum(u_ref[:, cols].astype(F32), 0.0))).astype(BF)
            du_ref[:, cols] = du
            dxn2 = dxn2 + _nn(du, w1.chunk(c)[...])
        dnorm, dg = _rms_bwd(dxn2, h1_ref[...], g2_ref[...])
        dh1_ref[...] = dh2v + dnorm
        dg2_ref[...] += dg

    return pl.pallas_call(
        body, name="mlp_bwd", grid=(seq // tm,),
        in_specs=[_rows(tm, D_MODEL), _rows(tm, D_FF), _rows(tm, D_MODEL), VMEM_FULL, ANY, ANY],
        out_specs=[_rows(tm, D_FF), _rows(tm, D_MODEL), pl.BlockSpec((1, D_MODEL), lambda i: (0, 0))],
        out_shape=[jax.ShapeDtypeStruct((seq, D_FF), BF), jax.ShapeDtypeStruct((seq, D_MODEL), F32),
                   jax.ShapeDtypeStruct((1, D_MODEL), F32)],
        scratch_shapes=_Resident.scratch(w_ff1, FF_CHUNKS) + _Resident.scratch(w_ff2, FF_CHUNKS),
        compiler_params=_params(1),
    )(dh2, u, h1, g2, w_ff1, w_ff2)


def _mm_tn(a, b, name, relu_sq=False, a_is_t=False, comm=None):
    kdim, seq = a.shape if a_is_t else a.shape[::-1]
    ndim = b.shape[1]
    ts = min(TN_TOKEN_TILE, seq)
    tk = min(kdim, 1024)
    tn = min(ndim, 1024)
    n_steps = seq // ts

    def body(a_ref, b_ref, o_ref):
        @pl.when(pl.program_id(2) == 0)
        def _():
            o_ref[...] = jnp.zeros_like(o_ref)

        av = a_ref[...]
        if relu_sq:
            av = jnp.square(jnp.maximum(av.astype(F32), 0.0))
        mm = _nn if a_is_t else _tn
        o_ref[...] += mm(av.astype(BF), b_ref[...].astype(BF))

    a_spec = (pl.BlockSpec((tk, ts), lambda k, n, s: (k, s)) if a_is_t
              else pl.BlockSpec((ts, tk), lambda k, n, s: (s, k)))
    res = _call(
        body, (a, b), comm, name=name, grid=(kdim // tk, ndim // tn, n_steps),
        in_specs=[a_spec, pl.BlockSpec((ts, tn), lambda k, n, s: (s, n))],
        out_specs=pl.BlockSpec((tk, tn), lambda k, n, s: (k, n)),
        out_shape=jax.ShapeDtypeStruct((kdim, ndim), F32))
    return res[0] if comm is None else res


def _dw_in_t(pieces, is_t, xn1, comm=None):
    seq = xn1.shape[0]
    ts = min(2 * TN_TOKEN_TILE, seq)
    tr = 256
    n_blk = [(p.shape[0] if t else p.shape[1]) // tr for p, t in zip(pieces, is_t)]
    offs = [sum(n_blk[:i]) for i in range(len(pieces) + 1)]
    n_p = len(pieces)

    def body(*refs):
        piece_refs, xn_ref, o_ref = refs[:n_p], refs[n_p], refs[n_p + 1]
        r, s = pl.program_id(0), pl.program_id(1)

        @pl.when(s == 0)
        def _():
            o_ref[...] = jnp.zeros_like(o_ref)

        for p in range(n_p):
            @pl.when((r >= offs[p]) & (r < offs[p + 1]))
            def _(p=p):
                mm = _nn if is_t[p] else _tn
                o_ref[...] += mm(piece_refs[p][...], xn_ref[pl.ds(pl.multiple_of(s * ts, ts), ts), :])

    def piece_spec(p):
        def index(r, s):
            inside = (r >= offs[p]) & (r < offs[p + 1])
            tok, blk = jnp.where(inside, s, 0), jnp.clip(r - offs[p], 0, n_blk[p] - 1)
            return (blk, tok) if is_t[p] else (tok, blk)
        return pl.BlockSpec((tr, ts) if is_t[p] else (ts, tr), index)

    return _call(
        body, (*pieces, xn1), comm, name="dw_in", grid=(offs[-1], seq // ts),
        in_specs=[piece_spec(p) for p in range(n_p)] + [VMEM_FULL],
        out_specs=pl.BlockSpec((tr, D_MODEL), lambda r, s: (r, 0)),
        out_shape=jax.ShapeDtypeStruct((offs[-1] * tr, D_MODEL), F32))


def _mix_bwd(dh1, ya, yr, gates, bg, w_att, w_ret, w_out, comm=None):
    seq = dh1.shape[0]
    tm = min(TOKEN_TILE, seq)

    def body(dh1_ref, ya_ref, yr_ref, gates_ref, bg_ref, wa_ref, wr_ref, wo_ref,
             dya_ref, dyr_ref, dgates_ref, doa_ref, dyrp_ref, db_ref):
        @pl.when(pl.program_id(0) == 0)
        def _():
            db_ref[...] = jnp.zeros_like(db_ref)

        dm = _nt(dh1_ref[...].astype(BF), wo_ref[...])
        gt = _sigmoid(gates_ref[...].astype(F32) + bg_ref[...])
        ga, gr = gt[:, :D_MODEL], gt[:, D_MODEL:]
        dya = (dm * ga).astype(BF)
        dyr = (dm * gr).astype(BF)
        dya_ref[...] = dya
        dyr_ref[...] = dyr
        dga = dm * ya_ref[...].astype(F32) * ga * (1.0 - ga)
        dgr = dm * yr_ref[...].astype(F32) * gr * (1.0 - gr)
        dgates_ref[:, :D_MODEL] = dga.astype(BF)
        dgates_ref[:, D_MODEL:] = dgr.astype(BF)
        db_ref[:, :D_MODEL] += jnp.sum(dga, axis=0, keepdims=True)
        db_ref[:, D_MODEL:] += jnp.sum(dgr, axis=0, keepdims=True)
        doa_ref[...] = _nt(wa_ref[...], dya).astype(BF)
        dyrp_ref[...] = _nt(dyr, wr_ref[...]).astype(BF)

    return _call(
        body, (dh1, ya, yr, gates, bg, w_att, w_ret, w_out), comm, name="mix_bwd", grid=(seq // tm,),
        in_specs=[_rows(tm, D_MODEL), _rows(tm, D_MODEL), _rows(tm, D_MODEL), _rows(tm, 2048), VMEM_FULL,
                  VMEM_FULL, VMEM_FULL, VMEM_FULL],
        out_specs=[_rows(tm, D_MODEL), _rows(tm, D_MODEL), _rows(tm, 2048), _cols(512, tm), _rows(tm, 1024),
                   pl.BlockSpec((1, 2048), lambda i: (0, 0))],
        out_shape=[jax.ShapeDtypeStruct((seq, D_MODEL), BF), jax.ShapeDtypeStruct((seq, D_MODEL), BF),
                   jax.ShapeDtypeStruct((seq, 2048), BF), jax.ShapeDtypeStruct((512, seq), BF),
                   jax.ShapeDtypeStruct((seq, 1024), BF), jax.ShapeDtypeStruct((1, 2048), F32)])


def _ret_bwd(qkr, vr, gr, ret, dyrp, states, gain, rtabs, cr, sr, comm=None):
    seq = qkr.shape[0]
    c = RET_CHUNK
    nc = seq // c
    dk, dv = RET_KEY_DIM, RET_VAL_DIM

    def body(qk_ref, v_ref, g_ref, ret_ref, dyp_ref, st_ref, gain_ref, intra_ref, qd_ref, kd_ref, cr_ref, sr_ref,
             dall_ref, dgain_ref, dstate):
        @pl.when(pl.program_id(0) == 0)
        def _():
            dstate[...] = jnp.zeros_like(dstate)
            dgain_ref[...] = jnp.zeros_like(dgain_ref)

        crv, srv = cr_ref[...], sr_ref[...]
        for h in range(RET_HEADS):
            vs = slice(dv * h, dv * (h + 1))
            qh = qk_ref[:, dk * h:dk * (h + 1)]
            kh = qk_ref[:, 512 + dk * h:512 + dk * (h + 1)]
            vh = v_ref[:, vs]
            out = ret_ref[:, vs]
            g = g_ref[:, vs].astype(F32)
            dyp = dyp_ref[:, vs].astype(F32)
            gain_h = gain_ref[:, vs]
            mu = jnp.mean(out, axis=-1, keepdims=True)
            d = out - mu
            rstd = lax.rsqrt(jnp.mean(d * d, axis=-1, keepdims=True) + GN_EPS)
            yn = d * rstd
            sg = _sigmoid(g)
            dg = dyp * (yn * gain_h) * (sg * (1.0 + g * (1.0 - sg)))
            dy = dyp * (g * sg)
            dgain_ref[:, vs] += jnp.sum(dy * yn, axis=0, keepdims=True)
            dyn = dy * gain_h
            dout = rstd * (dyn - jnp.mean(dyn, axis=-1, keepdims=True)
                           - yn * jnp.mean(dyn * yn, axis=-1, keepdims=True))
            doutb = dout.astype(BF)
            sc = st_ref[0, h]
            dsp = dstate[h]
            dspb = dsp.astype(BF)
            intra, qdv, kdv = intra_ref[h], qd_ref[h], kd_ref[h]
            att = _nt(qh, kh) * intra
            dab = (_nt(doutb, vh) * intra).astype(BF)
            qdec = (qh.astype(F32) * qdv).astype(BF)
            kdec = (kh.astype(F32) * kdv).astype(BF)
            dq = _nn(dab, kh) + _nt(doutb, sc) * qdv
            dkk = _tn(dab, qh) + _nt(vh, dspb) * kdv
            dvv = _tn(att.astype(BF), doutb) + _nn(kdec, dspb)
            dstate[h] = dsp * RET_CHUNK_DECAY[h] + _tn(qdec, doutb)
            dall_ref[:, dk * h:dk * (h + 1)] = _rope_ret_t(dq, crv, srv).astype(BF)
            dall_ref[:, 512 + dk * h:512 + dk * (h + 1)] = (_rope_ret_t(dkk, crv, srv) * RET_SCALE).astype(BF)
            dall_ref[:, 1024 + dv * h:1024 + dv * (h + 1)] = dvv.astype(BF)
            dall_ref[:, 2048 + dv * h:2048 + dv * (h + 1)] = dg.astype(BF)

    def rev(width):
        return pl.BlockSpec((c, width), lambda i: (nc - 1 - i, 0))

    return _call(
        body, (qkr, vr, gr, ret, dyrp, states, gain, *rtabs, cr, sr), comm, name="ret_bwd", grid=(nc,),
        in_specs=[rev(1024), rev(1024), rev(1024), rev(1024), rev(1024),
                  pl.BlockSpec((1, RET_HEADS, dk, dv), lambda i: (nc - 1 - i, 0, 0, 0)),
                  VMEM_FULL, VMEM_FULL, VMEM_FULL, VMEM_FULL, rev(LANE), rev(LANE)],
        out_specs=[rev(3072), pl.BlockSpec((1, 1024), lambda i: (0, 0))],
        out_shape=[jax.ShapeDtypeStruct((seq, 3072), BF), jax.ShapeDtypeStruct((1, 1024), F32)],
        scratch_shapes=[pltpu.VMEM((RET_HEADS, dk, dv), F32)])


def _attn_bwd(qt, kv, kvt, dot, sinks, ca, sa, sb, ct, st, comm=None):
    seq = kv.shape[0]
    c = ATT_BLOCK
    d = ATT_HEAD_DIM
    nb = seq // c

    def body(sink_ref, qt_ref, kvc_ref, kvp_ref, kvtc_ref, kvtp_ref, dot_ref, cap_ref, sap_ref, sbp_ref,
             ctc_ref, stc_ref, dqt_ref, dkv_ref, dsink_ref, carry):
        i = pl.program_id(0)

        @pl.when(i == 0)
        def _():
            carry[...] = jnp.zeros_like(carry)
            dsink_ref[...] = jnp.zeros_like(dsink_ref)

        def flush(total):
            dk_pre = _rope_att_t(total[:, :LANE], cap_ref[...], sap_ref[...], sbp_ref[...])
            dkv_ref[:, :LANE] = dk_pre.astype(BF)
            dkv_ref[:, LANE:] = total[:, LANE:].astype(BF)

        @pl.when(i < nb)
        def _():
            kwin = jnp.concatenate([kvp_ref[:, :LANE], kvc_ref[:, :LANE]], axis=0)
            vwin = jnp.concatenate([kvp_ref[:, LANE:], kvc_ref[:, LANE:]], axis=0)
            ktwin = jnp.concatenate([kvtp_ref[:LANE, :], kvtc_ref[:LANE, :]], axis=1)
            ctc, stc = ctc_ref[...], stc_ref[...]
            lane1 = lax.broadcasted_iota(jnp.int32, (1, LANE), 1)
            dkw = jnp.zeros((2 * c, LANE), F32)
            dvw = jnp.zeros((2 * c, LANE), F32)
            dsink = jnp.zeros((1, LANE), F32)
            for g in (0, 1):
                qpad, dopad = _att_group(qt_ref, g), _att_group(dot_ref, g)
                prob, psink = _att_probs(kwin, qpad, sink_ref, g, i)
                dprob = _nn(vwin, dopad)
                drow = jnp.sum(dprob * prob, axis=0, keepdims=True)
                ds = (prob * (dprob - drow) * ATT_SCALE).astype(BF)
                dqg = _nn(ktwin[d * g:d * (g + 1), :], ds)
                dkw = dkw + _nt(ds, qpad)
                dvw = dvw + _nt(prob.astype(BF), dopad)
                dsink_lanes = psink * drow
                for j in range(ATT_Q_PER_KV):
                    h = ATT_Q_PER_KV * g + j
                    dqt_ref[d * h:d * (h + 1), :] = _rope_att_rows(dqg[:, c * j:c * (j + 1)], ctc, stc, -1.0).astype(BF)
                    dsink = dsink + jnp.where(lane1 == h, -jnp.sum(dsink_lanes[:, c * j:c * (j + 1)]), 0.0)
            dsink_ref[...] += dsink
            flush(carry[...] + jnp.concatenate([dkw[:c], dvw[:c]], axis=1))
            carry[...] = jnp.concatenate([dkw[c:], dvw[c:]], axis=1)

        @pl.when(i == nb)
        def _():
            flush(carry[...])

    cur = lambda i: jnp.minimum(i, nb - 1)
    prev = lambda i: jnp.clip(i - 1, 0, nb - 1)
    rows_cur = lambda w: pl.BlockSpec((c, w), lambda i: (cur(i), 0))
    rows_prev = lambda w: pl.BlockSpec((c, w), lambda i: (prev(i), 0))
    cols_cur = lambda h: pl.BlockSpec((h, c), lambda i: (0, cur(i)))
    cols_prev = lambda h: pl.BlockSpec((h, c), lambda i: (0, prev(i)))
    return _call(
        body, (sinks, qt, kv, kv, kvt, kvt, dot, ca, sa, sb, ct, st), comm, name="attn_bwd", grid=(nb + 1,),
        in_specs=[SMEM_FULL, cols_cur(512), rows_cur(256), rows_prev(256), cols_cur(256), cols_prev(256),
                  cols_cur(512), rows_prev(LANE), rows_prev(LANE), rows_prev(LANE), cols_cur(8), cols_cur(8)],
        out_specs=[cols_cur(512), rows_prev(256), pl.BlockSpec((1, LANE), lambda i: (0, 0))],
        out_shape=[jax.ShapeDtypeStruct((512, seq), BF), jax.ShapeDtypeStruct((seq, 256), BF),
                   jax.ShapeDtypeStruct((1, LANE), F32)],
        scratch_shapes=[pltpu.VMEM((c, 256), F32)])


def _inproj_bwd(dqa, dkva, dret, dgates, dh1, x, g1, w_in, comm=None):
    seq = x.shape[0]
    tm = min(TOKEN_TILE, seq)

    def body(dqa_ref, dkva_ref, dret_ref, dgates_ref, dh1_ref, x_ref, g_ref, w_hbm, dx_ref, dg1_ref, w_vmem, w_sems):
        w = _Resident(w_hbm, w_vmem, w_sems, IN_CHUNKS_BWD)
        _Resident.load(w)

        @pl.when(pl.program_id(0) == 0)
        def _():
            dg1_ref[...] = jnp.zeros_like(dg1_ref)

        dxn = _tn(dqa_ref[...], w.chunk(0)[...])
        dxn = dxn + _nn(dkva_ref[...], w.chunk(1)[...])
        dxn = dxn + _nn(dret_ref[...], w.chunk(2)[...])
        dxn = dxn + _nn(dgates_ref[...], w.chunk(3)[...])
        dnorm, dg = _rms_bwd(dxn, x_ref[...], g_ref[...])
        dx_ref[...] = dh1_ref[...] + dnorm
        dg1_ref[...] += dg

    return _call(
        body, (dqa, dkva, dret, dgates, dh1, x, g1, w_in), comm, name="inproj_bwd", grid=(seq // tm,),
        in_specs=[_cols(512, tm), _rows(tm, 256), _rows(tm, 3072), _rows(tm, 2048), _rows(tm, D_MODEL),
                  _rows(tm, D_MODEL), VMEM_FULL, ANY],
        out_specs=[_rows(tm, D_MODEL), pl.BlockSpec((1, D_MODEL), lambda i: (0, 0))],
        out_shape=[jax.ShapeDtypeStruct((seq, D_MODEL), F32), jax.ShapeDtypeStruct((1, D_MODEL), F32)],
        scratch_shapes=_Resident.scratch(w_in, IN_CHUNKS_BWD))


def _local_step(x, target, g1, bg, sinks, gain, g2, g3, w_in, shards):
    seq = x.shape[0]
    ca, sa, sb, cr, sr, ct, st = _rope_tables(seq)
    rtabs = _ret_tables()
    scatter = _Scatter

    comm_mid, unpack_mid = _gather(shards, ("w_att_up", "w_ret_up", "w_out"), relay=True)
    comm_ff1, unpack_ff1 = _gather(shards, ("w_ff1",), relay=True)
    (xn1, qt, kv, kvt, qkr, vr, gr, gates), got = _inproj_fwd(x, g1, w_in, (ca, sa, sb, cr, sr, ct, st),
                                                              _join(comm_mid, comm_ff1))
    w_att, w_ret, w_out = unpack_mid(got[:3])
    (w_ff1,) = unpack_ff1(got[3:])
    comm, unpack = _gather(shards, ("w_ff2",), relay=True)
    oa, got = _attn_fwd(qt, kv, kvt, sinks, comm)
    (w_ff2,) = unpack(got)
    (ret, yrp, states), _ = _ret_fwd(qkr, vr, gr, gain, rtabs)
    ya, yr, merged, h1 = _mix_fwd(oa, yrp, gates, bg, x, w_att, w_ret, w_out)
    xn2, u, dh2, loss, dg3 = _mlp_fwd_loss(h1, g2, g3, target, w_ff1, w_ff2)

    du, dh1, dg2 = _mlp_bwd(dh2, u, h1, g2, w_ff1, w_ff2)
    ff2 = scatter(dict(w_ff2=_mm_tn(u, dh2, "dw_ff2", relu_sq=True)), "ff2")
    d_ff1, r1 = _mm_tn(du, xn2, "dw_ff1", comm=ff2.pair_comm())
    ff2.pair_done(r1)
    ff1 = scatter(dict(w_ff1=d_ff1), "ff1")
    (dya, dyr, dgates, doa, dyrp, db), got = _mix_bwd(dh1, ya, yr, gates, bg, w_att, w_ret, w_out,
                                                      _join(ff1.pair_comm(), ff2.chip_comm()))
    ff1.pair_done(got[:1])
    ff2.chip_done(got[1:])
    mid = scatter(dict(w_att_up=_mm_tn(oa, dya, "dw_att_up", a_is_t=True), w_ret_up=_mm_tn(yrp, dyr, "dw_ret_up"),
                       w_out=_mm_tn(merged, dh1, "dw_out")), "mid")
    (dqa, dkva, dsink), r2 = _attn_bwd(qt, kv, kvt, doa, sinks, ca, sa, sb, ct, st, ff1.chip_comm())
    ff1.chip_done(r2)
    (dret, dgain), r1 = _ret_bwd(qkr, vr, gr, ret, dyrp, states, gain, rtabs, cr, sr, mid.pair_comm())
    mid.pair_done(r1)
    d_in, r2 = _dw_in_t((dqa, dkva, dret, dgates), (True, False, False, False), xn1, mid.chip_comm())
    mid.chip_done(r2)
    win = scatter(dict(w_in=d_in), "in")
    win.pair_done(_comm_alone(win.pair_comm(), "pair_exchange_in"))
    (dx, dg1), r2 = _inproj_bwd(dqa, dkva, dret, dgates, dh1, x, g1, w_in, win.chip_comm())
    win.chip_done(r2)
    small = dict(norm_mix_gain=dg1, b_gates=db, attn_sinks=dsink, ret_gn_gain=dgain, norm_mlp_gain=dg2,
                 norm_final_gain=dg3)
    return loss, dx, small, (win, mid, ff1, ff2)


def _coords():
    return lax.axis_index("x"), lax.axis_index("y"), lax.axis_index("c")


def _flip(v, bit):
    return 1 - v if bit else v


def _xor(a, b):
    return a + b - 2 * a * b


def _gather_relay_comm(shards):
    n = len(shards)

    def parts(ins, outs, sems):
        send_sems, recv_sems, local_sems = sems
        x, y, c = _coords()
        me, sib = (x, y, c), (x, y, 1 - c)

        def ring(core):
            return ((_xor(x, core), _xor(y, 1 - core), core), (_xor(x, 1 - core), _xor(y, core), core),
                    (1 - x, 1 - y, core))

        def slot(a, blk):
            return outs[a].at[4 * blk[0] + 2 * blk[1] + blk[2]]

        def copy(a, k, blk, to, src=None):
            return pltpu.make_async_remote_copy(
                src_ref=slot(a, blk) if src is None else src, dst_ref=slot(a, blk),
                send_sem=send_sems.at[a, k], recv_sem=recv_sems.at[a, k], device_id=to, device_id_type=MESH)

        mine = [pltpu.make_async_copy(ins[a], slot(a, me), local_sems.at[a]) for a in range(n)]
        return me, sib, ring, copy, mine

    def first_copies(ins, outs, sems):
        me, sib, ring, copy, mine = parts(ins, outs, sems)
        src, dst, _ = ring(me[2])
        return mine, [copy(a, k, me, to, src=ins[a]) for a in range(n) for k, to in ((1, src), (2, dst), (0, sib))]

    def start(ins, outs, sems):
        mine, first = first_copies(ins, outs, sems)
        for cp in mine + first:
            cp.start()

    def relay(ins, outs, sems):
        me, sib, ring, copy, _ = parts(ins, outs, sems)
        src, dst, _ = ring(me[2])
        for a in range(n):
            copy(a, 1, src, me).wait_recv()
            copy(a, 3, src, dst).start()
            copy(a, 4, src, sib).start()

    def finish(ins, outs, sems):
        me, sib, ring, copy, _ = parts(ins, outs, sems)
        mine, first = first_copies(ins, outs, sems)
        src, dst, diag = ring(me[2])
        passed = [copy(a, k, src, to) for a in range(n) for k, to in ((3, dst), (4, sib))]
        for k, blk in ((2, dst), (3, diag)):
            for a in range(n):
                copy(a, k, blk, me).wait_recv()
                fwd = copy(a, k + 3, blk, sib)
                fwd.start()
                passed.append(fwd)
        s_src, s_dst, s_diag = ring(sib[2])
        for a in range(n):
            for k, blk in ((0, sib), (4, s_src), (5, s_dst), (6, s_diag)):
                copy(a, k, blk, me).wait_recv()
        for cp in first + passed:
            cp.wait_send()
        for cp in mine:
            cp.wait()

    return _Comm(list(shards), [jax.ShapeDtypeStruct((N_DEV,) + s.shape, s.dtype) for s in shards],
                 [pltpu.SemaphoreType.DMA((n, 7)), pltpu.SemaphoreType.DMA((n, 7)), pltpu.SemaphoreType.DMA((n,))],
                 start, finish, relay)


def _gather_comm(shards):
    n = len(shards)

    def parts(ins, outs, sems):
        send_sems, recv_sems, local_sems = sems
        x, y, c = _coords()
        me, sib = (x, y, c), (x, y, 1 - c)
        chips = [(1 - x, y), (x, 1 - y), (1 - x, 1 - y)]

        def slot(a, blk):
            return outs[a].at[4 * blk[0] + 2 * blk[1] + blk[2]]

        def copy(a, k, blk, to, src=None):
            return pltpu.make_async_remote_copy(
                src_ref=slot(a, blk) if src is None else src, dst_ref=slot(a, blk),
                send_sem=send_sems.at[a, k], recv_sem=recv_sems.at[a, k], device_id=to, device_id_type=MESH)

        mine = [pltpu.make_async_copy(ins[a], slot(a, me), local_sems.at[a]) for a in range(n)]
        first = []
        for a in range(n):
            first.append(copy(a, 0, me, sib, src=ins[a]))
            first += [copy(a, 1 + j, me, (*chip, c), src=ins[a]) for j, chip in enumerate(chips)]
        return me, sib, chips, c, copy, mine, first

    def start(ins, outs, sems):
        *_, mine, first = parts(ins, outs, sems)
        for cp in mine + first:
            cp.start()

    def finish(ins, outs, sems):
        me, sib, chips, c, copy, mine, first = parts(ins, outs, sems)
        passed = []
        for j, chip in enumerate(chips):
            for a in range(n):
                copy(a, 1 + j, (*chip, c), me).wait_recv()
                fwd = copy(a, 4 + j, (*chip, c), sib)
                fwd.start()
                passed.append(fwd)
        for a in range(n):
            copy(a, 0, sib, me).wait_recv()
            for j, chip in enumerate(chips):
                copy(a, 4 + j, (*chip, 1 - c), me).wait_recv()
        for cp in first + passed:
            cp.wait_send()
        for cp in mine:
            cp.wait()

    return _Comm(list(shards), [jax.ShapeDtypeStruct((N_DEV,) + s.shape, s.dtype) for s in shards],
                 [pltpu.SemaphoreType.DMA((n, 7)), pltpu.SemaphoreType.DMA((n, 7)), pltpu.SemaphoreType.DMA((n,))],
                 start, finish)


COLUMN_SHARDED = ("w_in", "w_ff1")
COLUMN_RELAID = ("w_att_up",)


def _gather(shards, names, relay=False):
    def unpack(got):
        return [jnp.transpose(g, (1, 0, 2)).reshape(g.shape[1], N_DEV * g.shape[2]) if k in COLUMN_RELAID
                else g.reshape(N_DEV * g.shape[1], g.shape[2]) for k, g in zip(names, got)]

    return (_gather_relay_comm if relay else _gather_comm)([shards[k] for k in names]), unpack


def _pair_comm(grads):
    n = len(grads)

    def copies(g, r1, sems):
        send_sems, recv_sems = sems
        x, y, c = _coords()
        return [pltpu.make_async_remote_copy(
            src_ref=g[a].at[2 * j + (1 - c)], dst_ref=r1[a].at[j], send_sem=send_sems.at[a, j],
            recv_sem=recv_sems.at[a, j], device_id=(x, y, 1 - c), device_id_type=MESH)
            for a in range(n) for j in range(4)]

    def start(g, r1, sems):
        for cp in copies(g, r1, sems):
            cp.start()

    def finish(g, r1, sems):
        for cp in copies(g, r1, sems):
            cp.wait_recv()
        for cp in copies(g, r1, sems):
            cp.wait_send()

    return _Comm(list(grads), [jax.ShapeDtypeStruct((4,) + g.shape[1:], g.dtype) for g in grads],
                 [pltpu.SemaphoreType.DMA((n, 4)), pltpu.SemaphoreType.DMA((n, 4))], start, finish)


def _small_comm(small):
    def parts(ins, outs, sems):
        (small_ref,), (small_all,) = ins, outs
        ssend, srecv, lsem = sems
        x, y, c = _coords()
        me_idx = 4 * x + 2 * y + c
        own = pltpu.make_async_copy(small_ref, small_all.at[me_idx], lsem)
        sends, recvs = [], []
        for r in range(1, N_DEV):
            px, py, pc = _flip(x, r & 4), _flip(y, r & 2), _flip(c, r & 1)
            sends.append(pltpu.make_async_remote_copy(
                src_ref=small_ref, dst_ref=small_all.at[me_idx], send_sem=ssend.at[r - 1], recv_sem=srecv.at[r - 1],
                device_id=(px, py, pc), device_id_type=MESH))
            recvs.append(pltpu.make_async_remote_copy(
                src_ref=small_ref, dst_ref=small_all.at[4 * px + 2 * py + pc], send_sem=ssend.at[r - 1],
                recv_sem=srecv.at[r - 1], device_id=(px, py, pc), device_id_type=MESH))
        return own, sends, recvs

    def start(ins, outs, sems):
        own, sends, _ = parts(ins, outs, sems)
        own.start()
        for cp in sends:
            cp.start()

    def finish(ins, outs, sems):
        own, sends, recvs = parts(ins, outs, sems)
        for cp in recvs:
            cp.wait_recv()
        for cp in sends:
            cp.wait_send()
        own.wait()

    return _Comm([small], [jax.ShapeDtypeStruct((N_DEV,) + small.shape, small.dtype)],
                 [pltpu.SemaphoreType.DMA((N_DEV - 1,)), pltpu.SemaphoreType.DMA((N_DEV - 1,)),
                  pltpu.SemaphoreType.DMA], start, finish)


def _pair_sum(grads, r1s, c_arr, tag):
    n = len(grads)
    q = 1

    def body(c_ref, *refs):
        g, r, t = refs[:n], refs[n:2 * n], refs[2 * n:]
        for a in range(n):
            t[a][...] = (g[a][...] + r[a][...]).astype(t[a].dtype)

    def blk(arr):
        return (1, arr.shape[1] // q, arr.shape[2])

    grid_spec = pltpu.PrefetchScalarGridSpec(
        num_scalar_prefetch=1, grid=(4, q),
        in_specs=[pl.BlockSpec(blk(g), lambda j, s, c_ref: (2 * j + c_ref[0], s, 0)) for g in grads]
        + [pl.BlockSpec(blk(r), lambda j, s, c_ref: (j, s, 0)) for r in r1s],
        out_specs=[pl.BlockSpec(blk(r), lambda j, s, c_ref: (j, s, 0)) for r in r1s])
    return pl.pallas_call(
        body, name="pair_sum_" + tag, grid_spec=grid_spec,
        out_shape=[jax.ShapeDtypeStruct(r.shape, RS_PAYLOAD) for r in r1s],
        compiler_params=_params(2),
    )(c_arr, *grads, *r1s)


def _chip_comm(ts):
    n = len(ts)

    def copies(t, r2, sems):
        send_sems, recv_sems = sems
        x, y, c = _coords()
        out = []
        for a in range(n):
            for r in range(1, 4):
                tx, ty = _flip(x, r & 2), _flip(y, r & 1)
                out.append(pltpu.make_async_remote_copy(
                    src_ref=t[a].at[2 * tx + ty], dst_ref=r2[a].at[r - 1], send_sem=send_sems.at[a, r - 1],
                    recv_sem=recv_sems.at[a, r - 1], device_id=(tx, ty, c), device_id_type=MESH))
        return out

    def start(t, r2, sems):
        for cp in copies(t, r2, sems):
            cp.start()

    def finish(t, r2, sems):
        for cp in copies(t, r2, sems):
            cp.wait_recv()
        for cp in copies(t, r2, sems):
            cp.wait_send()

    return _Comm(list(ts), [jax.ShapeDtypeStruct((3,) + t.shape[1:], t.dtype) for t in ts],
                 [pltpu.SemaphoreType.DMA((n, 3)), pltpu.SemaphoreType.DMA((n, 3))], start, finish)


class _Scatter:
    def __init__(self, grads, tag):
        self.names, self.tag = tuple(grads), tag
        self.stacks = [jnp.transpose(g.reshape(g.shape[0], N_DEV, g.shape[1] // N_DEV), (1, 0, 2))
                       if k in COLUMN_RELAID else g.reshape(N_DEV, g.shape[0] // N_DEV, g.shape[1])
                       for k, g in grads.items()]

    def pair_comm(self):
        return _pair_comm(self.stacks)

    def pair_done(self, r1s):
        self.r1s = list(r1s)
        c_arr = jnp.reshape(lax.axis_index("c"), (1,)).astype(jnp.int32)
        self.ts = _pair_sum(self.stacks, self.r1s, c_arr, self.tag)

    def chip_comm(self):
        return _chip_comm(self.ts)

    def chip_done(self, r2s):
        self.r2s = list(r2s)


def _adamw(w, g, m, v):
    m = ADAM_B1 * m + (1.0 - ADAM_B1) * g
    v = ADAM_B2 * v + (1.0 - ADAM_B2) * jnp.square(g)
    m_hat = m / (1.0 - ADAM_B1 ** ADAM_STEP)
    v_hat = v / (1.0 - ADAM_B2 ** ADAM_STEP)
    delta = -ADAM_LR * (m_hat / (jnp.sqrt(v_hat) + ADAM_EPS) + ADAM_WD * w)
    return delta, m, v


ADAM_STEPS = 4


def _piece_specs(stacks):
    def rows(s):
        return s.shape[1] // ADAM_STEPS
    return ([pl.BlockSpec((1, rows(s), s.shape[2]), lambda i, idx_ref: (idx_ref[0], i, 0)) for s in stacks]
            + [pl.BlockSpec((1, rows(s), s.shape[2]), lambda i, idx_ref: (idx_ref[1], i, 0)) for s in stacks]
            + [pl.BlockSpec((3, rows(s), s.shape[2]), lambda i, idx_ref: (0, i, 0)) for s in stacks])


def _piece_sum(g0, r1, r2):
    return (((g0[0] + r1[0]) + r2[0].astype(F32)) + r2[1].astype(F32)) + r2[2].astype(F32)


def _shard_sum(stacks, r1s, r2s, idx_arr):
    n = len(stacks)

    def body(idx_ref, *refs):
        g0, r1, r2, outs = (refs[k * n:(k + 1) * n] for k in range(4))
        for a in range(n):
            outs[a][...] = _piece_sum(g0[a], r1[a], r2[a])

    out_specs = [pl.BlockSpec((s.shape[1] // ADAM_STEPS, s.shape[2]), lambda i, idx_ref: (i, 0)) for s in stacks]
    grid_spec = pltpu.PrefetchScalarGridSpec(num_scalar_prefetch=1, grid=(ADAM_STEPS,),
                                             in_specs=_piece_specs(stacks), out_specs=out_specs)
    return pl.pallas_call(
        body, name="shard_sum", grid_spec=grid_spec,
        out_shape=[jax.ShapeDtypeStruct(s.shape[1:], F32) for s in stacks],
        compiler_params=_params(1),
    )(idx_arr, *stacks, *r1s, *r2s)


def _adam_big(pieces, summed, ws, ms, vs, idx_arr):
    stacks, r1s, r2s = pieces
    n_p, n = len(stacks), len(ws)

    def body(idx_ref, *refs):
        it = iter(refs)
        g0, r1, r2, gs, w, m, v = ([next(it) for _ in range(k)] for k in (n_p, n_p, n_p, n - n_p, n, n, n))
        outs = list(it)
        for a in range(n):
            g = _piece_sum(g0[a], r1[a], r2[a]) if a < n_p else gs[a - n_p][...]
            delta, nm, nv = _adamw(w[a][...], g, m[a][...], v[a][...])
            outs[4 * a][...] = g
            outs[4 * a + 1][...] = delta
            outs[4 * a + 2][...] = nm
            outs[4 * a + 3][...] = nv

    def shard_spec(w):
        return pl.BlockSpec((w.shape[0] // ADAM_STEPS, w.shape[1]), lambda i, idx_ref: (i, 0))

    in_specs = _piece_specs(stacks) + [shard_spec(w) for w in ws[n_p:]] + [shard_spec(w) for w in ws] * 3
    out_specs = [shard_spec(w) for w in ws for _ in range(4)]
    grid_spec = pltpu.PrefetchScalarGridSpec(num_scalar_prefetch=1, grid=(ADAM_STEPS,), in_specs=in_specs,
                                             out_specs=out_specs)
    return pl.pallas_call(
        body, name="adam_big", grid_spec=grid_spec,
        out_shape=[jax.ShapeDtypeStruct(w.shape, F32) for w in ws for _ in range(4)],
        compiler_params=_params(1),
    )(idx_arr, *stacks, *r1s, *r2s, *summed, *ws, *ms, *vs)


def _adam_small(small_all, w, m, v):
    def body(all_ref, w_ref, m_ref, v_ref, g_ref, d_ref, nm_ref, nv_ref):
        g = all_ref[0]
        for k in range(1, N_DEV):
            g = g + all_ref[k]
        delta, nm, nv = _adamw(w_ref[...], g, m_ref[...], v_ref[...])
        g_ref[...] = g
        d_ref[...] = delta
        nm_ref[...] = nm
        nv_ref[...] = nv

    return pl.pallas_call(
        body, name="adam_small",
        in_specs=[VMEM_FULL] * 4, out_specs=[VMEM_FULL] * 4,
        out_shape=[jax.ShapeDtypeStruct(w.shape, F32)] * 4,
    )(small_all, w, m, v)


RS_PAYLOAD = BF


def _pack_rows(rows):
    rid = lax.broadcasted_iota(jnp.int32, (8, D_MODEL), 0)
    out = jnp.zeros((8, D_MODEL), F32)
    for i, r in enumerate(rows):
        out = jnp.where(rid == i, jnp.broadcast_to(r, (8, D_MODEL)), out)
    return out


def _small_rows(norm_mix_gain, b_gates, attn_sinks, ret_gn_gain, norm_mlp_gain, norm_final_gain):
    return [norm_mix_gain, b_gates[:, :D_MODEL], b_gates[:, D_MODEL:], ret_gn_gain, norm_mlp_gain,
            norm_final_gain.reshape(1, D_MODEL), jnp.pad(attn_sinks, ((0, 0), (0, D_MODEL - ATT_HEADS)))]


def _unpack_small(p):
    return dict(norm_mix_gain=p[0:1], b_gates=jnp.concatenate([p[1:2], p[2:3]], axis=1), ret_gn_gain=p[3:4],
                norm_mlp_gain=p[4:5], norm_final_gain=p[5], attn_sinks=p[6:7, :ATT_HEADS])


WEIGHTS = ("norm_mix_gain", "w_in", "b_gates", "attn_sinks", "ret_gn_gain", "w_att_up", "w_ret_up", "w_out",
           "norm_mlp_gain", "w_ff1", "w_ff2", "norm_final_gain")
BIG = ("w_in", "w_att_up", "w_ret_up", "w_out", "w_ff1", "w_ff2")


def kernel(x, norm_mix_gain, w_in, b_gates, attn_sinks, ret_gn_gain, w_att_up, w_ret_up, w_out, norm_mlp_gain, w_ff1, w_ff2, norm_final_gain, loss_target, m_norm_mix_gain, m_w_in, m_b_gates, m_attn_sinks, m_ret_gn_gain, m_w_att_up, m_w_ret_up, m_w_out, m_norm_mlp_gain, m_w_ff1, m_w_ff2, m_norm_final_gain, v_norm_mix_gain, v_w_in, v_b_gates, v_attn_sinks, v_ret_gn_gain, v_w_att_up, v_w_ret_up, v_w_out, v_norm_mlp_gain, v_w_ff1, v_w_ff2, v_norm_final_gain):
    w = dict(norm_mix_gain=norm_mix_gain, w_in=w_in, b_gates=b_gates, attn_sinks=attn_sinks, ret_gn_gain=ret_gn_gain,
             w_att_up=w_att_up, w_ret_up=w_ret_up, w_out=w_out, norm_mlp_gain=norm_mlp_gain, w_ff1=w_ff1,
             w_ff2=w_ff2, norm_final_gain=norm_final_gain)
    m = dict(norm_mix_gain=m_norm_mix_gain, w_in=m_w_in, b_gates=m_b_gates, attn_sinks=m_attn_sinks,
             ret_gn_gain=m_ret_gn_gain, w_att_up=m_w_att_up, w_ret_up=m_w_ret_up, w_out=m_w_out,
             norm_mlp_gain=m_norm_mlp_gain, w_ff1=m_w_ff1, w_ff2=m_w_ff2, norm_final_gain=m_norm_final_gain)
    v = dict(norm_mix_gain=v_norm_mix_gain, w_in=v_w_in, b_gates=v_b_gates, attn_sinks=v_attn_sinks,
             ret_gn_gain=v_ret_gn_gain, w_att_up=v_w_att_up, w_ret_up=v_w_ret_up, w_out=v_w_out,
             norm_mlp_gain=v_norm_mlp_gain, w_ff1=v_w_ff1, w_ff2=v_w_ff2, norm_final_gain=v_norm_final_gain)

    shards = {k: (w[k][0].T if k in COLUMN_SHARDED else w[k][0]).astype(BF) for k in BIG}
    comm, unpack = _gather(shards, ("w_in",), relay=True)
    (w_in_full,) = unpack(_comm_alone(comm, "allgather_w_in"))

    loss_p, dx, small, groups = _local_step(
        x[0], loss_target[0], norm_mix_gain, b_gates, attn_sinks[0], ret_gn_gain, norm_mlp_gain,
        norm_final_gain.reshape(1, D_MODEL), w_in_full, shards)

    by_name = {}
    for grp in groups:
        for k, stack, r1, r2 in zip(grp.names, grp.stacks, grp.r1s, grp.r2s):
            by_name[k] = (stack, r1, r2)
    lane = lax.broadcasted_iota(jnp.int32, (1, D_MODEL), 1)
    sink_row = jnp.where(lane < ATT_HEADS, jnp.pad(small["attn_sinks"], ((0, 0), (0, D_MODEL - LANE))),
                         jnp.where(lane == ATT_HEADS, jnp.pad(loss_p, ((0, 0), (0, D_MODEL - LANE)), mode="edge"),
                                   0.0))
    small_pack = _pack_rows([small["norm_mix_gain"], small["b_gates"][:, :D_MODEL], small["b_gates"][:, D_MODEL:],
                             small["ret_gn_gain"], small["norm_mlp_gain"], small["norm_final_gain"], sink_row])
    xi, yi, ci = _coords()
    (small_all,) = _comm_alone(_small_comm(small_pack), "small_exchange")
    idx_arr = jnp.stack([4 * xi + 2 * yi + ci, 2 * xi + yi]).astype(jnp.int32)
    in_pieces = ["w_att_up", "w_ret_up", "w_out", "w_ff2", "w_in"]
    in_sum = ["w_ff1"]
    order = in_pieces + in_sum
    summed_t = _shard_sum(*[[by_name[k][i] for k in in_sum] for i in range(3)], idx_arr)

    def shard(tree, k):
        return tree[k][0].T if k == "w_in" else tree[k][0]

    adam_out = _adam_big([[by_name[k][i] for k in in_pieces] for i in range(3)], [g.T for g in summed_t],
                         [shard(w, k) for k in order], [shard(m, k) for k in order], [shard(v, k) for k in order],
                         idx_arr)
    big_out = [adam_out[4 * order.index(k) + i].T if k == "w_in" else adam_out[4 * order.index(k) + i]
               for k in BIG for i in range(4)]
    sm_g, sm_d, sm_m, sm_v = _adam_small(small_all, _pack_rows(_small_rows(*[w[k] for k in WEIGHTS if k not in BIG])),
                                         _pack_rows(_small_rows(*[m[k] for k in WEIGHTS if k not in BIG])),
                                         _pack_rows(_small_rows(*[v[k] for k in WEIGHTS if k not in BIG])))

    loss = sm_g[6, ATT_HEADS]
    kinds = []
    for idx, packed in enumerate((sm_g, sm_d, sm_m, sm_v)):
        out = _unpack_small(packed)
        for a, k in enumerate(BIG):
            out[k] = big_out[4 * a + idx][None]
        kinds.append(out)
    return (loss, dx[None], *[kinds[0][k] for k in WEIGHTS], *[kinds[1][k] for k in WEIGHTS],
            *[kinds[2][k] for k in WEIGHTS], *[kinds[3][k] for k in WEIGHTS])
```

```python
import functools
import math

import jax
import jax.numpy as jnp
from jax import lax
from jax.experimental import pallas as pl
from jax.experimental.pallas import tpu as pltpu

F32 = jnp.float32
BF = jnp.bfloat16
MESH = pl.DeviceIdType.MESH

D_MODEL = 1024
ATT_HEADS = 8
ATT_HEAD_DIM = 64
ATT_BLOCK = 128
ROPE_DIM = 16
ROPE_THETA = 500000.0
RET_HEADS = 4
RET_KEY_DIM = 128
RET_VAL_DIM = 256
RET_CHUNK = 128
RET_ROT_BASE = 10000.0
D_FF = 4096
NORM_EPS = 1e-6
GN_EPS = 1e-6
NEG_INF = -1e30
ATT_SCALE = ATT_HEAD_DIM ** -0.5
RET_SCALE = RET_KEY_DIM ** -0.5

C_QA, C_KA, C_VA, C_QR, C_KR, C_VR, C_GR, C_GATES, C_END = 0, 512, 640, 768, 1280, 1792, 2816, 3840, 5888

ADAM_LR = 0.001
ADAM_B1 = 0.9
ADAM_B2 = 0.999
ADAM_EPS = 1e-08
ADAM_WD = 0.01
ADAM_STEP = 10

N_DEV = 8
LANE = 128
VMEM_LIMIT = 56 * 1024 * 1024
TOKEN_TILE = 256
TN_TOKEN_TILE = 1024
FF_CHUNKS = (0, 1024, 2048, 3072, 4096)
IN_CHUNKS_FWD = (C_QA, C_KA, C_QR, C_KR, C_VR, C_GR, C_GATES, C_END)
IN_CHUNKS_BWD = (C_QA, C_KA, C_QR, C_GATES, C_END)

RET_LOG_GAMMA = tuple(math.log1p(-(2.0 ** (-5.0 - h))) for h in range(RET_HEADS))
RET_CHUNK_DECAY = tuple(math.exp(RET_CHUNK * lg) for lg in RET_LOG_GAMMA)

VMEM_FULL = pl.BlockSpec(memory_space=pltpu.VMEM)
SMEM_FULL = pl.BlockSpec(memory_space=pltpu.SMEM)
ANY = pl.BlockSpec(memory_space=pl.ANY)


def _params(n_axes):
    return pltpu.CompilerParams(dimension_semantics=("arbitrary",) * n_axes, vmem_limit_bytes=VMEM_LIMIT)


def _nn(a, b):
    return jnp.dot(a, b, preferred_element_type=F32)


def _nt(a, b):
    return lax.dot_general(a, b, (((1,), (1,)), ((), ())), preferred_element_type=F32)


def _tn(a, b):
    return lax.dot_general(a, b, (((0,), (0,)), ((), ())), preferred_element_type=F32)


def _sigmoid(v):
    return 1.0 / (1.0 + jnp.exp(-v))


def _rows(tile, width):
    return pl.BlockSpec((tile, width), lambda i: (i, 0))


def _cols(height, tile):
    return pl.BlockSpec((height, tile), lambda i: (0, i))


class _Resident:
    def __init__(self, hbm_ref, vmem_ref, sems, bounds):
        self.hbm, self.vmem, self.sems, self.bounds = hbm_ref, vmem_ref, sems, bounds

    @staticmethod
    def scratch(w, bounds):
        return [pltpu.VMEM(w.shape, w.dtype), pltpu.SemaphoreType.DMA((len(bounds) - 1,))]

    @staticmethod
    def load(*weights):
        @pl.when(pl.program_id(0) == 0)
        def _():
            copies = [w._copy(c) for w in weights for c in range(len(w.bounds) - 1)]
            for cp in copies:
                cp.start()
            for cp in copies:
                cp.wait()

    def _rows(self, c):
        return pl.ds(self.bounds[c], self.bounds[c + 1] - self.bounds[c])

    def _copy(self, c):
        return pltpu.make_async_copy(self.hbm.at[self._rows(c)], self.vmem.at[self._rows(c)], self.sems.at[c])

    def chunk(self, c):
        return self.vmem.at[self._rows(c)]


class _Comm:
    def __init__(self, inputs, out_shapes, scratch, start, finish, relay=None):
        self.inputs, self.out_shapes, self.scratch = inputs, out_shapes, scratch
        self.start, self.finish, self.relay = start, finish, relay


RELAY_AT = 0.5


def _join(a, b):
    na_in, na_out, na_sem = len(a.inputs), len(a.out_shapes), len(a.scratch)

    def both(name):
        def run(ins, outs, sems):
            for part, args in ((a, (ins[:na_in], outs[:na_out], sems[:na_sem])),
                               (b, (ins[na_in:], outs[na_out:], sems[na_sem:]))):
                if getattr(part, name) is not None:
                    getattr(part, name)(*args)
        return run

    return _Comm(list(a.inputs) + list(b.inputs), list(a.out_shapes) + list(b.out_shapes),
                 list(a.scratch) + list(b.scratch), both("start"), both("finish"),
                 both("relay") if a.relay or b.relay else None)


def _call(body, args, comm=None, *, name, grid, in_specs, out_specs, out_shape, scratch_shapes=()):
    params = _params(len(grid))
    if comm is None:
        return pl.pallas_call(body, name=name, grid=grid, in_specs=in_specs, out_specs=out_specs, out_shape=out_shape,
                              scratch_shapes=scratch_shapes, compiler_params=params)(*args), ()
    single = not isinstance(out_specs, (list, tuple))
    out_specs_l = [out_specs] if single else list(out_specs)
    out_shape_l = [out_shape] if single else list(out_shape)
    n_in, n_out, n_scr = len(in_specs), len(out_specs_l), len(scratch_shapes)
    n_cin, n_cout = len(comm.inputs), len(comm.out_shapes)

    def hosted(*refs):
        it = iter(refs)
        ins, cin, outs, cout, scr = ([next(it) for _ in range(k)] for k in (n_in, n_cin, n_out, n_cout, n_scr))
        sems = list(it)
        ids = [pl.program_id(k) for k in range(len(grid))]
        first = functools.reduce(jnp.logical_and, [i == 0 for i in ids])
        last = functools.reduce(jnp.logical_and, [i == g - 1 for i, g in zip(ids, grid)])

        @pl.when(first)
        def _():
            comm.start(cin, cout, sems)

        if comm.relay is not None:
            at = [int(grid[0] * RELAY_AT)] + [0] * (len(grid) - 1)

            @pl.when(functools.reduce(jnp.logical_and, [i == v for i, v in zip(ids, at)]))
            def _():
                comm.relay(cin, cout, sems)

        body(*ins, *outs, *scr)

        @pl.when(last)
        def _():
            comm.finish(cin, cout, sems)

    res = pl.pallas_call(
        hosted, name=name, grid=grid, in_specs=list(in_specs) + [ANY] * n_cin,
        out_specs=out_specs_l + [ANY] * n_cout, out_shape=out_shape_l + list(comm.out_shapes),
        scratch_shapes=list(scratch_shapes) + list(comm.scratch), compiler_params=params)(*args, *comm.inputs)
    return (res[0] if single else res[:n_out]), res[n_out:]


def _comm_alone(comm, name):
    n_cin, n_cout = len(comm.inputs), len(comm.out_shapes)

    def body(*refs):
        cin, cout, sems = refs[:n_cin], refs[n_cin:n_cin + n_cout], refs[n_cin + n_cout:]
        comm.start(cin, cout, sems)
        if comm.relay is not None:
            comm.relay(cin, cout, sems)
        comm.finish(cin, cout, sems)

    return pl.pallas_call(body, name=name, in_specs=[ANY] * n_cin, out_specs=[ANY] * n_cout,
                          out_shape=list(comm.out_shapes), scratch_shapes=list(comm.scratch))(*comm.inputs)


def _slabs(v, fn):
    return jnp.concatenate([fn(v[:, LANE * j:LANE * (j + 1)]) for j in range(v.shape[1] // LANE)], axis=1)


def _rope_att(v, ca, sa, sb):
    return _slabs(v, lambda t: t * ca + pltpu.roll(t, LANE - 8, 1) * sa + pltpu.roll(t, 8, 1) * sb)


def _rope_att_t(v, ca, sa, sb):
    return _slabs(v, lambda t: t * ca + pltpu.roll(t * sa, 8, 1) + pltpu.roll(t * sb, LANE - 8, 1))


def _rope_att_rows(v, ct, st, sign):
    parts = []
    for h in range(v.shape[0] // ATT_HEAD_DIM):
        r0 = ATT_HEAD_DIM * h
        x1, x2 = v[r0:r0 + 8], v[r0 + 8:r0 + 16]
        parts += [x1 * ct - sign * (x2 * st), x2 * ct + sign * (x1 * st), v[r0 + 16:r0 + ATT_HEAD_DIM]]
    return jnp.concatenate(parts, axis=0)


def _rope_ret(v, cr, sr):
    return _slabs(v, lambda t: t * cr + pltpu.roll(t, 64, 1) * sr)


def _rope_ret_t(v, cr, sr):
    return _slabs(v, lambda t: t * cr + pltpu.roll(t * sr, 64, 1))


def _rope_tables(seq):
    pos = jnp.arange(seq, dtype=F32)

    def cs(dim, theta):
        inv = theta ** (-jnp.arange(0, dim, 2, dtype=F32) / dim)
        ang = pos[:, None] * inv[None, :]
        return jnp.cos(ang), jnp.sin(ang)

    ca, sa_ = cs(ROPE_DIM, ROPE_THETA)
    zeros8 = jnp.zeros_like(ca)
    rest = ATT_HEAD_DIM - ROPE_DIM
    c64 = jnp.concatenate([ca, ca, jnp.ones((seq, rest), F32)], axis=1)
    sa64 = jnp.concatenate([-sa_, zeros8, jnp.zeros((seq, rest), F32)], axis=1)
    sb64 = jnp.concatenate([zeros8, sa_, jnp.zeros((seq, rest), F32)], axis=1)
    cr, sr_ = cs(RET_KEY_DIM, RET_ROT_BASE)
    return (jnp.tile(c64, (1, 2)), jnp.tile(sa64, (1, 2)), jnp.tile(sb64, (1, 2)),
            jnp.concatenate([cr, cr], axis=1), jnp.concatenate([-sr_, sr_], axis=1), ca.T, sa_.T)


def _ret_tables():
    c = RET_CHUNK
    lg = jnp.asarray(RET_LOG_GAMMA, F32)
    idx = jnp.arange(c, dtype=F32)
    diff = idx[:, None] - idx[None, :]
    intra = jnp.where(diff >= 0, jnp.exp(jnp.maximum(diff, 0.0) * lg[:, None, None]), 0.0)
    qd = jnp.exp((idx + 1.0)[None, :] * lg[:, None])[..., None]
    kd = jnp.exp((c - 1.0 - idx)[None, :] * lg[:, None])[..., None]
    return intra, jnp.broadcast_to(qd, (RET_HEADS, c, RET_KEY_DIM)), jnp.broadcast_to(kd, (RET_HEADS, c, RET_KEY_DIM))


def _inproj_fwd(x, g1, w_in, tabs, comm=None):
    seq = x.shape[0]
    tm = min(TOKEN_TILE, seq)

    def body(x_ref, g_ref, w_hbm, ca_ref, sa_ref, sb_ref, cr_ref, sr_ref, ct_ref, st_ref,
             xn_ref, qt_ref, kv_ref, kvt_ref, qkr_ref, vr_ref, gr_ref, gates_ref, w_vmem, w_sems):
        w = _Resident(w_hbm, w_vmem, w_sems, IN_CHUNKS_FWD)
        _Resident.load(w)
        xf = x_ref[...]
        r = lax.rsqrt(jnp.mean(xf * xf, axis=-1, keepdims=True) + NORM_EPS)
        xn = (xf * r * g_ref[...]).astype(BF)
        xn_ref[...] = xn
        ct, st = ct_ref[...], st_ref[...]
        qt_ref[...] = _rope_att_rows(_nt(w.chunk(0)[...], xn), ct, st, 1.0).astype(BF)
        w_kv = w.chunk(1)
        kvt = _nt(w_kv[...], xn)
        kvt_ref[:LANE, :] = _rope_att_rows(kvt[:LANE], ct, st, 1.0).astype(BF)
        kvt_ref[LANE:, :] = kvt[LANE:].astype(BF)
        kvn = _nt(xn, w_kv[...])
        kv_ref[:, :LANE] = _rope_att(kvn[:, :LANE], ca_ref[...], sa_ref[...], sb_ref[...]).astype(BF)
        kv_ref[:, LANE:] = kvn[:, LANE:].astype(BF)
        cr, sr = cr_ref[...], sr_ref[...]
        qr = _rope_ret(_nt(xn, w.chunk(2)[...]), cr, sr)
        qkr_ref[:, :512] = qr.astype(BF)
        kr = _rope_ret(_nt(xn, w.chunk(3)[...]), cr, sr) * RET_SCALE
        qkr_ref[:, 512:] = kr.astype(BF)
        vr_ref[...] = _nt(xn, w.chunk(4)[...]).astype(BF)
        gr_ref[...] = _nt(xn, w.chunk(5)[...]).astype(BF)
        gates_ref[...] = _nt(xn, w.chunk(6)[...]).astype(BF)

    tab = _rows(tm, LANE)
    tab_t = _cols(8, tm)
    return _call(
        body, (x, g1, w_in, *tabs), comm, name="inproj_fwd", grid=(seq // tm,),
        in_specs=[_rows(tm, D_MODEL), VMEM_FULL, ANY, tab, tab, tab, tab, tab, tab_t, tab_t],
        out_specs=[_rows(tm, D_MODEL), _cols(512, tm), _rows(tm, 256), _cols(256, tm), _rows(tm, 1024),
                   _rows(tm, 1024), _rows(tm, 1024), _rows(tm, 2048)],
        out_shape=[jax.ShapeDtypeStruct(s, BF) for s in ((seq, D_MODEL), (512, seq), (seq, 256), (256, seq),
                                                         (seq, 1024), (seq, 1024), (seq, 1024), (seq, 2048))],
        scratch_shapes=_Resident.scratch(w_in, IN_CHUNKS_FWD))


ATT_Q_PER_KV = ATT_HEADS // 2


def _att_group(ref, kv):
    d = ATT_HEAD_DIM
    g = jnp.concatenate([ref[d * (ATT_Q_PER_KV * kv + j):d * (ATT_Q_PER_KV * kv + j + 1), :]
                         for j in range(ATT_Q_PER_KV)], axis=1)
    z = jnp.zeros_like(g)
    return jnp.concatenate([g, z] if kv == 0 else [z, g], axis=0)


def _att_probs(kwin, qpad, sink_ref, kv, i):
    c = ATT_BLOCK
    kj = lax.broadcasted_iota(jnp.int32, (2 * c, c), 0)
    qi = lax.broadcasted_iota(jnp.int32, (2 * c, c), 1)
    allowed = (kj > qi) & (kj <= qi + c) & ((kj >= c) | (i > 0))
    allowed = jnp.concatenate([allowed] * ATT_Q_PER_KV, axis=1)
    s = jnp.where(allowed, _nn(kwin, qpad) * ATT_SCALE, NEG_INF)
    sk = jnp.concatenate([jnp.full((1, c), sink_ref[ATT_Q_PER_KV * kv + j], F32) for j in range(ATT_Q_PER_KV)], axis=1)
    m = jnp.maximum(jnp.max(s, axis=0, keepdims=True), sk)
    pe = jnp.exp(s - m)
    psink = jnp.exp(sk - m)
    inv = 1.0 / (jnp.sum(pe, axis=0, keepdims=True) + psink)
    return pe * inv, psink * inv


def _seq_fwd(qt, kv, kvt, sinks, qkr, vr, gr, gain, rtabs, comm=None):
    seq = kv.shape[0]
    c = ATT_BLOCK
    assert c == RET_CHUNK
    d = ATT_HEAD_DIM
    nb = seq // c
    dk, dv = RET_KEY_DIM, RET_VAL_DIM

    def body(sink_ref, qt_ref, kvc_ref, kvp_ref, kvtc_ref, kvtp_ref, qk_ref, v_ref, g_ref, gain_ref, intra_ref,
             qd_ref, kd_ref, ot_ref, ret_ref, yrp_ref, st_ref, state):
        i = pl.program_id(0)
        kwin = jnp.concatenate([kvp_ref[:, :LANE], kvc_ref[:, :LANE]], axis=0)
        vtwin = jnp.concatenate([kvtp_ref[LANE:, :], kvtc_ref[LANE:, :]], axis=1)
        for g in (0, 1):
            prob, _ = _att_probs(kwin, _att_group(qt_ref, g), sink_ref, g, i)
            out = _nn(vtwin[d * g:d * (g + 1), :], prob.astype(BF))
            for j in range(ATT_Q_PER_KV):
                h = ATT_Q_PER_KV * g + j
                ot_ref[d * h:d * (h + 1), :] = out[:, c * j:c * (j + 1)].astype(BF)

        @pl.when(i == 0)
        def _():
            state[...] = jnp.zeros_like(state)

        for h in range(RET_HEADS):
            qh = qk_ref[:, dk * h:dk * (h + 1)]
            kh = qk_ref[:, 512 + dk * h:512 + dk * (h + 1)]
            vh = v_ref[:, dv * h:dv * (h + 1)]
            sh = state[h]
            shb = sh.astype(BF)
            st_ref[0, h] = shb
            att = _nt(qh, kh) * intra_ref[h]
            inner = _nn(att.astype(BF), vh)
            cross = _nn((qh.astype(F32) * qd_ref[h]).astype(BF), shb)
            out = inner + cross
            state[h] = sh * RET_CHUNK_DECAY[h] + _tn((kh.astype(F32) * kd_ref[h]).astype(BF), vh)
            ret_ref[:, dv * h:dv * (h + 1)] = out
            mu = jnp.mean(out, axis=-1, keepdims=True)
            dev = out - mu
            var = jnp.mean(dev * dev, axis=-1, keepdims=True)
            y = dev * lax.rsqrt(var + GN_EPS) * gain_ref[:, dv * h:dv * (h + 1)]
            g = g_ref[:, dv * h:dv * (h + 1)].astype(F32)
            yrp_ref[:, dv * h:dv * (h + 1)] = (g * _sigmoid(g) * y).astype(BF)

    prev = lambda i: jnp.maximum(i - 1, 0)
    return _call(
        body, (sinks, qt, kv, kv, kvt, kvt, qkr, vr, gr, gain, *rtabs), comm, name="seq_fwd", grid=(nb,),
        in_specs=[SMEM_FULL, _cols(512, c), _rows(c, 256), pl.BlockSpec((c, 256), lambda i: (prev(i), 0)),
                  _cols(256, c), pl.BlockSpec((256, c), lambda i: (0, prev(i))),
                  _rows(c, 1024), _rows(c, 1024), _rows(c, 1024), VMEM_FULL, VMEM_FULL, VMEM_FULL, VMEM_FULL],
        out_specs=[_cols(512, c), _rows(c, 1024), _rows(c, 1024),
                   pl.BlockSpec((1, RET_HEADS, dk, dv), lambda i: (i, 0, 0, 0))],
        out_shape=[jax.ShapeDtypeStruct((512, seq), BF), jax.ShapeDtypeStruct((seq, 1024), F32),
                   jax.ShapeDtypeStruct((seq, 1024), BF), jax.ShapeDtypeStruct((nb, RET_HEADS, dk, dv), BF)],
        scratch_shapes=[pltpu.VMEM((RET_HEADS, dk, dv), F32)])


def _mix_fwd(oat, yrp, gates, bg, x, w_att, w_ret, w_out):
    seq = x.shape[0]
    tm = min(TOKEN_TILE, seq)

    def body(oat_ref, yrp_ref, gates_ref, bg_ref, x_ref, wa_ref, wr_ref, wo_ref, ya_ref, yr_ref, mg_ref, h1_ref):
        ya = _tn(oat_ref[...], wa_ref[...])
        yr = _nn(yrp_ref[...], wr_ref[...])
        gt = _sigmoid(gates_ref[...].astype(F32) + bg_ref[...])
        merged = (gt[:, :D_MODEL] * ya + gt[:, D_MODEL:] * yr).astype(BF)
        ya_ref[...] = ya.astype(BF)
        yr_ref[...] = yr.astype(BF)
        mg_ref[...] = merged
        h1_ref[...] = x_ref[...] + _nn(merged, wo_ref[...])

    return pl.pallas_call(
        body, name="mix_fwd", grid=(seq // tm,),
        in_specs=[_cols(512, tm), _rows(tm, 1024), _rows(tm, 2048), VMEM_FULL, _rows(tm, D_MODEL),
                  VMEM_FULL, VMEM_FULL, VMEM_FULL],
        out_specs=[_rows(tm, D_MODEL)] * 4,
        out_shape=[jax.ShapeDtypeStruct((seq, D_MODEL), BF)] * 3 + [jax.ShapeDtypeStruct((seq, D_MODEL), F32)],
        compiler_params=_params(1),
    )(oat, yrp, gates, bg, x, w_att, w_ret, w_out)


def _mlp_loss_step(h1, g2, g3, target, w_ff1, w_ff2):
    seq = h1.shape[0]
    tm = min(TOKEN_TILE, seq)
    n_chunks = len(FF_CHUNKS) - 1

    def body(h1_ref, g2_ref, g3_ref, t_ref, w1_hbm, w2_hbm,
             xn2_ref, hdn_ref, dh2_ref, du_ref, dh1_ref, loss_ref, dg3_ref, dg2_ref,
             w1_vmem, w1_sems, w2_vmem, w2_sems, relu_u):
        w1 = _Resident(w1_hbm, w1_vmem, w1_sems, FF_CHUNKS)
        w2 = _Resident(w2_hbm, w2_vmem, w2_sems, FF_CHUNKS)
        _Resident.load(w1, w2)

        @pl.when(pl.program_id(0) == 0)
        def _():
            loss_ref[...] = jnp.zeros_like(loss_ref)
            dg3_ref[...] = jnp.zeros_like(dg3_ref)
            dg2_ref[...] = jnp.zeros_like(dg2_ref)

        h1v = h1_ref[...]
        r2 = lax.rsqrt(jnp.mean(h1v * h1v, axis=-1, keepdims=True) + NORM_EPS)
        xn2 = (h1v * r2 * g2_ref[...]).astype(BF)
        xn2_ref[...] = xn2
        h2 = h1v
        for c in range(n_chunks):
            cols = slice(FF_CHUNKS[c], FF_CHUNKS[c + 1])
            a = jnp.maximum(_nt(xn2, w1.chunk(c)[...]), 0.0)
            relu_u[:, cols] = a.astype(BF)
            hdn = jnp.square(a).astype(BF)
            hdn_ref[:, cols] = hdn
            h2 = h2 + _nn(hdn, w2.chunk(c)[...])
        r3 = lax.rsqrt(jnp.mean(h2 * h2, axis=-1, keepdims=True) + NORM_EPS)
        hn = h2 * r3
        err = hn * g3_ref[...] - t_ref[...]
        loss_ref[...] += jnp.sum(err * err) * (0.5 / D_MODEL)
        dy = err * (1.0 / D_MODEL)
        dg3_ref[...] += jnp.sum(dy * hn, axis=0, keepdims=True)
        z = dy * g3_ref[...]
        dh2 = r3 * (z - hn * jnp.mean(z * hn, axis=-1, keepdims=True))
        dh2b = dh2.astype(BF)
        dh2_ref[...] = dh2b
        dxn2 = jnp.zeros_like(dh2)
        for c in range(n_chunks):
            cols = slice(FF_CHUNKS[c], FF_CHUNKS[c + 1])
            du = (_nt(dh2b, w2.chunk(c)[...]) * (2.0 * relu_u[:, cols].astype(F32))).astype(BF)
            du_ref[:, cols] = du
            dxn2 = dxn2 + _nn(du, w1.chunk(c)[...])
        dnorm, dg = _rms_bwd(dxn2, h1v, g2_ref[...])
        dh1_ref[...] = dh2 + dnorm
        dg2_ref[...] += dg

    gain_acc = pl.BlockSpec((1, D_MODEL), lambda i: (0, 0))
    return pl.pallas_call(
        body, name="mlp_loss_step", grid=(seq // tm,),
        in_specs=[_rows(tm, D_MODEL), VMEM_FULL, VMEM_FULL, _rows(tm, D_MODEL), ANY, ANY],
        out_specs=[_rows(tm, D_MODEL), _rows(tm, D_FF), _rows(tm, D_MODEL), _rows(tm, D_FF), _rows(tm, D_MODEL),
                   pl.BlockSpec((1, LANE), lambda i: (0, 0)), gain_acc, gain_acc],
        out_shape=[jax.ShapeDtypeStruct((seq, D_MODEL), BF), jax.ShapeDtypeStruct((seq, D_FF), BF),
                   jax.ShapeDtypeStruct((seq, D_MODEL), BF), jax.ShapeDtypeStruct((seq, D_FF), BF),
                   jax.ShapeDtypeStruct((seq, D_MODEL), F32), jax.ShapeDtypeStruct((1, LANE), F32),
                   jax.ShapeDtypeStruct((1, D_MODEL), F32), jax.ShapeDtypeStruct((1, D_MODEL), F32)],
        scratch_shapes=(_Resident.scratch(w_ff1, FF_CHUNKS) + _Resident.scratch(w_ff2, FF_CHUNKS)
                        + [pltpu.VMEM((tm, D_FF), BF)]),
        compiler_params=_params(1),
    )(h1, g2, g3, target, w_ff1, w_ff2)


def _rms_bwd(dxn, xin, gain):
    r = lax.rsqrt(jnp.mean(xin * xin, axis=-1, keepdims=True) + NORM_EPS)
    xhat = xin * r
    z = dxn * gain
    dxin = r * (z - xhat * jnp.mean(z * xhat, axis=-1, keepdims=True))
    return dxin, jnp.sum(dxn * xhat, axis=0, keepdims=True)


def _mm_tn(a, b, name, a_is_t=False, comm=None):
    kdim, seq = a.shape if a_is_t else a.shape[::-1]
    ndim = b.shape[1]
    ts = min(TN_TOKEN_TILE, seq)
    tk = min(kdim, 1024)
    tn = min(ndim, 1024)
    n_steps = seq // ts

    def body(a_ref, b_ref, o_ref):
        @pl.when(pl.program_id(2) == 0)
        def _():
            o_ref[...] = jnp.zeros_like(o_ref)

        mm = _nn if a_is_t else _tn
        o_ref[...] += mm(a_ref[...].astype(BF), b_ref[...].astype(BF))

    a_spec = (pl.BlockSpec((tk, ts), lambda k, n, s: (k, s)) if a_is_t
              else pl.BlockSpec((ts, tk), lambda k, n, s: (s, k)))
    res = _call(
        body, (a, b), comm, name=name, grid=(kdim // tk, ndim // tn, n_steps),
        in_specs=[a_spec, pl.BlockSpec((ts, tn), lambda k, n, s: (s, n))],
        out_specs=pl.BlockSpec((tk, tn), lambda k, n, s: (k, n)),
        out_shape=jax.ShapeDtypeStruct((kdim, ndim), F32))
    return res[0] if comm is None else res


def _dw_in_t(pieces, is_t, xn1, comm=None):
    seq = xn1.shape[0]
    ts = min(2 * TN_TOKEN_TILE, seq)
    tr = 256
    n_blk = [(p.shape[0] if t else p.shape[1]) // tr for p, t in zip(pieces, is_t)]
    offs = [sum(n_blk[:i]) for i in range(len(pieces) + 1)]
    n_p = len(pieces)

    def body(*refs):
        piece_refs, xn_ref, o_ref = refs[:n_p], refs[n_p], refs[n_p + 1]
        r, s = pl.program_id(0), pl.program_id(1)

        @pl.when(s == 0)
        def _():
            o_ref[...] = jnp.zeros_like(o_ref)

        for p in range(n_p):
            @pl.when((r >= offs[p]) & (r < offs[p + 1]))
            def _(p=p):
                mm = _nn if is_t[p] else _tn
                o_ref[...] += mm(piece_refs[p][...], xn_ref[pl.ds(pl.multiple_of(s * ts, ts), ts), :])

    def piece_spec(p):
        def index(r, s):
            inside = (r >= offs[p]) & (r < offs[p + 1])
            tok, blk = jnp.where(inside, s, 0), jnp.clip(r - offs[p], 0, n_blk[p] - 1)
            return (blk, tok) if is_t[p] else (tok, blk)
        return pl.BlockSpec((tr, ts) if is_t[p] else (ts, tr), index)

    return _call(
        body, (*pieces, xn1), comm, name="dw_in", grid=(offs[-1], seq // ts),
        in_specs=[piece_spec(p) for p in range(n_p)] + [VMEM_FULL],
        out_specs=pl.BlockSpec((tr, D_MODEL), lambda r, s: (r, 0)),
        out_shape=jax.ShapeDtypeStruct((offs[-1] * tr, D_MODEL), F32))


def _mix_bwd(dh1, ya, yr, gates, bg, w_att, w_ret, w_out, comm=None):
    seq = dh1.shape[0]
    tm = min(TOKEN_TILE, seq)

    def body(dh1_ref, ya_ref, yr_ref, gates_ref, bg_ref, wa_ref, wr_ref, wo_ref,
             dya_ref, dyr_ref, dgates_ref, doa_ref, dyrp_ref, db_ref):
        @pl.when(pl.program_id(0) == 0)
        def _():
            db_ref[...] = jnp.zeros_like(db_ref)

        dm = _nt(dh1_ref[...].astype(BF), wo_ref[...])
        gt = _sigmoid(gates_ref[...].astype(F32) + bg_ref[...])
        ga, gr = gt[:, :D_MODEL], gt[:, D_MODEL:]
        dya = (dm * ga).astype(BF)
        dyr = (dm * gr).astype(BF)
        dya_ref[...] = dya
        dyr_ref[...] = dyr
        dga = dm * ya_ref[...].astype(F32) * ga * (1.0 - ga)
        dgr = dm * yr_ref[...].astype(F32) * gr * (1.0 - gr)
        dgates_ref[:, :D_MODEL] = dga.astype(BF)
        dgates_ref[:, D_MODEL:] = dgr.astype(BF)
        db_ref[:, :D_MODEL] += jnp.sum(dga, axis=0, keepdims=True)
        db_ref[:, D_MODEL:] += jnp.sum(dgr, axis=0, keepdims=True)
        doa_ref[...] = _nt(wa_ref[...], dya).astype(BF)
        dyrp_ref[...] = _nt(dyr, wr_ref[...]).astype(BF)

    return _call(
        body, (dh1, ya, yr, gates, bg, w_att, w_ret, w_out), comm, name="mix_bwd", grid=(seq // tm,),
        in_specs=[_rows(tm, D_MODEL), _rows(tm, D_MODEL), _rows(tm, D_MODEL), _rows(tm, 2048), VMEM_FULL,
                  VMEM_FULL, VMEM_FULL, VMEM_FULL],
        out_specs=[_rows(tm, D_MODEL), _rows(tm, D_MODEL), _rows(tm, 2048), _cols(512, tm), _rows(tm, 1024),
                   pl.BlockSpec((1, 2048), lambda i: (0, 0))],
        out_shape=[jax.ShapeDtypeStruct((seq, D_MODEL), BF), jax.ShapeDtypeStruct((seq, D_MODEL), BF),
                   jax.ShapeDtypeStruct((seq, 2048), BF), jax.ShapeDtypeStruct((512, seq), BF),
                   jax.ShapeDtypeStruct((seq, 1024), BF), jax.ShapeDtypeStruct((1, 2048), F32)])


def _ret_bwd(qkr, vr, gr, ret, dyrp, states, gain, rtabs, cr, sr, comm=None):
    seq = qkr.shape[0]
    c = RET_CHUNK
    nc = seq // c
    dk, dv = RET_KEY_DIM, RET_VAL_DIM

    def body(qk_ref, v_ref, g_ref, ret_ref, dyp_ref, st_ref, gain_ref, intra_ref, qd_ref, kd_ref, cr_ref, sr_ref,
             dall_ref, dgain_ref, dstate):
        @pl.when(pl.program_id(0) == 0)
        def _():
            dstate[...] = jnp.zeros_like(dstate)
            dgain_ref[...] = jnp.zeros_like(dgain_ref)

        crv, srv = cr_ref[...], sr_ref[...]
        for h in range(RET_HEADS):
            vs = slice(dv * h, dv * (h + 1))
            qh = qk_ref[:, dk * h:dk * (h + 1)]
            kh = qk_ref[:, 512 + dk * h:512 + dk * (h + 1)]
            vh = v_ref[:, vs]
            out = ret_ref[:, vs]
            g = g_ref[:, vs].astype(F32)
            dyp = dyp_ref[:, vs].astype(F32)
            gain_h = gain_ref[:, vs]
            mu = jnp.mean(out, axis=-1, keepdims=True)
            d = out - mu
            rstd = lax.rsqrt(jnp.mean(d * d, axis=-1, keepdims=True) + GN_EPS)
            yn = d * rstd
            sg = _sigmoid(g)
            dg = dyp * (yn * gain_h) * (sg * (1.0 + g * (1.0 - sg)))
            dy = dyp * (g * sg)
            dgain_ref[:, vs] += jnp.sum(dy * yn, axis=0, keepdims=True)
            dyn = dy * gain_h
            dout = rstd * (dyn - jnp.mean(dyn, axis=-1, keepdims=True)
                           - yn * jnp.mean(dyn * yn, axis=-1, keepdims=True))
            doutb = dout.astype(BF)
            sc = st_ref[0, h]
            dsp = dstate[h]
            dspb = dsp.astype(BF)
            intra, qdv, kdv = intra_ref[h], qd_ref[h], kd_ref[h]
            att = _nt(qh, kh) * intra
            dab = (_nt(doutb, vh) * intra).astype(BF)
            qdec = (qh.astype(F32) * qdv).astype(BF)
            kdec = (kh.astype(F32) * kdv).astype(BF)
            dq = _nn(dab, kh) + _nt(doutb, sc) * qdv
            dkk = _tn(dab, qh) + _nt(vh, dspb) * kdv
            dvv = _tn(att.astype(BF), doutb) + _nn(kdec, dspb)
            dstate[h] = dsp * RET_CHUNK_DECAY[h] + _tn(qdec, doutb)
            dall_ref[:, dk * h:dk * (h + 1)] = _rope_ret_t(dq, crv, srv).astype(BF)
            dall_ref[:, 512 + dk * h:512 + dk * (h + 1)] = (_rope_ret_t(dkk, crv, srv) * RET_SCALE).astype(BF)
            dall_ref[:, 1024 + dv * h:1024 + dv * (h + 1)] = dvv.astype(BF)
            dall_ref[:, 2048 + dv * h:2048 + dv * (h + 1)] = dg.astype(BF)

    def rev(width):
        return pl.BlockSpec((c, width), lambda i: (nc - 1 - i, 0))

    return _call(
        body, (qkr, vr, gr, ret, dyrp, states, gain, *rtabs, cr, sr), comm, name="ret_bwd", grid=(nc,),
        in_specs=[rev(1024), rev(1024), rev(1024), rev(1024), rev(1024),
                  pl.BlockSpec((1, RET_HEADS, dk, dv), lambda i: (nc - 1 - i, 0, 0, 0)),
                  VMEM_FULL, VMEM_FULL, VMEM_FULL, VMEM_FULL, rev(LANE), rev(LANE)],
        out_specs=[rev(3072), pl.BlockSpec((1, 1024), lambda i: (0, 0))],
        out_shape=[jax.ShapeDtypeStruct((seq, 3072), BF), jax.ShapeDtypeStruct((1, 1024), F32)],
        scratch_shapes=[pltpu.VMEM((RET_HEADS, dk, dv), F32)])


def _attn_bwd(qt, kv, kvt, dot, sinks, ca, sa, sb, ct, st, comm=None):
    seq = kv.shape[0]
    c = ATT_BLOCK
    d = ATT_HEAD_DIM
    nb = seq // c

    def body(sink_ref, qt_ref, kvc_ref, kvp_ref, kvtc_ref, kvtp_ref, dot_ref, cap_ref, sap_ref, sbp_ref,
             ctc_ref, stc_ref, dqt_ref, dkv_ref, dsink_ref, carry):
        i = pl.program_id(0)

        @pl.when(i == 0)
        def _():
            carry[...] = jnp.zeros_like(carry)
            dsink_ref[...] = jnp.zeros_like(dsink_ref)

        def flush(total):
            dk_pre = _rope_att_t(total[:, :LANE], cap_ref[...], sap_ref[...], sbp_ref[...])
            dkv_ref[:, :LANE] = dk_pre.astype(BF)
            dkv_ref[:, LANE:] = total[:, LANE:].astype(BF)

        @pl.when(i < nb)
        def _():
            kwin = jnp.concatenate([kvp_ref[:, :LANE], kvc_ref[:, :LANE]], axis=0)
            vwin = jnp.concatenate([kvp_ref[:, LANE:], kvc_ref[:, LANE:]], axis=0)
            ktwin = jnp.concatenate([kvtp_ref[:LANE, :], kvtc_ref[:LANE, :]], axis=1)
            ctc, stc = ctc_ref[...], stc_ref[...]
            lane1 = lax.broadcasted_iota(jnp.int32, (1, LANE), 1)
            dkw = jnp.zeros((2 * c, LANE), F32)
            dvw = jnp.zeros((2 * c, LANE), F32)
            dsink = jnp.zeros((1, LANE), F32)
            for g in (0, 1):
                qpad, dopad = _att_group(qt_ref, g), _att_group(dot_ref, g)
                prob, psink = _att_probs(kwin, qpad, sink_ref, g, i)
                dprob = _nn(vwin, dopad)
                drow = jnp.sum(dprob * prob, axis=0, keepdims=True)
                ds = (prob * (dprob - drow) * ATT_SCALE).astype(BF)
                dqg = _nn(ktwin[d * g:d * (g + 1), :], ds)
                dkw = dkw + _nt(ds, qpad)
                dvw = dvw + _nt(prob.astype(BF), dopad)
                dsink_lanes = psink * drow
                for j in range(ATT_Q_PER_KV):
                    h = ATT_Q_PER_KV * g + j
                    dqt_ref[d * h:d * (h + 1), :] = _rope_att_rows(dqg[:, c * j:c * (j + 1)], ctc, stc, -1.0).astype(BF)
                    dsink = dsink + jnp.where(lane1 == h, -jnp.sum(dsink_lanes[:, c * j:c * (j + 1)]), 0.0)
            dsink_ref[...] += dsink
            flush(carry[...] + jnp.concatenate([dkw[:c], dvw[:c]], axis=1))
            carry[...] = jnp.concatenate([dkw[c:], dvw[c:]], axis=1)

        @pl.when(i == nb)
        def _():
            flush(carry[...])

    cur = lambda i: jnp.minimum(i, nb - 1)
    prev = lambda i: jnp.clip(i - 1, 0, nb - 1)
    rows_cur = lambda w: pl.BlockSpec((c, w), lambda i: (cur(i), 0))
    rows_prev = lambda w: pl.BlockSpec((c, w), lambda i: (prev(i), 0))
    cols_cur = lambda h: pl.BlockSpec((h, c), lambda i: (0, cur(i)))
    cols_prev = lambda h: pl.BlockSpec((h, c), lambda i: (0, prev(i)))
    return _call(
        body, (sinks, qt, kv, kv, kvt, kvt, dot, ca, sa, sb, ct, st), comm, name="attn_bwd", grid=(nb + 1,),
        in_specs=[SMEM_FULL, cols_cur(512), rows_cur(256), rows_prev(256), cols_cur(256), cols_prev(256),
                  cols_cur(512), rows_prev(LANE), rows_prev(LANE), rows_prev(LANE), cols_cur(8), cols_cur(8)],
        out_specs=[cols_cur(512), rows_prev(256), pl.BlockSpec((1, LANE), lambda i: (0, 0))],
        out_shape=[jax.ShapeDtypeStruct((512, seq), BF), jax.ShapeDtypeStruct((seq, 256), BF),
                   jax.ShapeDtypeStruct((1, LANE), F32)],
        scratch_shapes=[pltpu.VMEM((c, 256), F32)])


def _inproj_bwd(dqa, dkva, dret, dgates, dh1, x, g1, w_in, comm=None):
    seq = x.shape[0]
    tm = min(TOKEN_TILE, seq)

    def body(dqa_ref, dkva_ref, dret_ref, dgates_ref, dh1_ref, x_ref, g_ref, w_hbm, dx_ref, dg1_ref, w_vmem, w_sems):
        w = _Resident(w_hbm, w_vmem, w_sems, IN_CHUNKS_BWD)
        _Resident.load(w)

        @pl.when(pl.program_id(0) == 0)
        def _():
            dg1_ref[...] = jnp.zeros_like(dg1_ref)

        dxn = _tn(dqa_ref[...], w.chunk(0)[...])
        dxn = dxn + _nn(dkva_ref[...], w.chunk(1)[...])
        dxn = dxn + _nn(dret_ref[...], w.chunk(2)[...])
        dxn = dxn + _nn(dgates_ref[...], w.chunk(3)[...])
        dnorm, dg = _rms_bwd(dxn, x_ref[...], g_ref[...])
        dx_ref[...] = dh1_ref[...] + dnorm
        dg1_ref[...] += dg

    return _call(
        body, (dqa, dkva, dret, dgates, dh1, x, g1, w_in), comm, name="inproj_bwd", grid=(seq // tm,),
        in_specs=[_cols(512, tm), _rows(tm, 256), _rows(tm, 3072), _rows(tm, 2048), _rows(tm, D_MODEL),
                  _rows(tm, D_MODEL), VMEM_FULL, ANY],
        out_specs=[_rows(tm, D_MODEL), pl.BlockSpec((1, D_MODEL), lambda i: (0, 0))],
        out_shape=[jax.ShapeDtypeStruct((seq, D_MODEL), F32), jax.ShapeDtypeStruct((1, D_MODEL), F32)],
        scratch_shapes=_Resident.scratch(w_in, IN_CHUNKS_BWD))


def _local_step(x, target, g1, bg, sinks, gain, g2, g3, w_in, shards):
    seq = x.shape[0]
    ca, sa, sb, cr, sr, ct, st = _rope_tables(seq)
    rtabs = _ret_tables()
    scatter = _Scatter

    comm_mid, unpack_mid = _gather(shards, ("w_att_up", "w_ret_up", "w_out"), relay=True)
    comm_ff1, unpack_ff1 = _gather(shards, ("w_ff1",), relay=True)
    (xn1, qt, kv, kvt, qkr, vr, gr, gates), got = _inproj_fwd(x, g1, w_in, (ca, sa, sb, cr, sr, ct, st),
                                                              _join(comm_mid, comm_ff1))
    w_att, w_ret, w_out = unpack_mid(got[:3])
    (w_ff1,) = unpack_ff1(got[3:])
    comm, unpack = _gather(shards, ("w_ff2",), relay=True)
    (oa, ret, yrp, states), got = _seq_fwd(qt, kv, kvt, sinks, qkr, vr, gr, gain, rtabs, comm)
    (w_ff2,) = unpack(got)
    ya, yr, merged, h1 = _mix_fwd(oa, yrp, gates, bg, x, w_att, w_ret, w_out)
    xn2, hdn, dh2, du, dh1, loss, dg3, dg2 = _mlp_loss_step(h1, g2, g3, target, w_ff1, w_ff2)

    ff2 = scatter(dict(w_ff2=_mm_tn(hdn, dh2, "dw_ff2")), "ff2")
    d_ff1, r1 = _mm_tn(du, xn2, "dw_ff1", comm=ff2.pair_comm())
    ff2.pair_done(r1)
    ff1 = scatter(dict(w_ff1=d_ff1), "ff1")
    (dya, dyr, dgates, doa, dyrp, db), got = _mix_bwd(dh1, ya, yr, gates, bg, w_att, w_ret, w_out,
                                                      _join(ff1.pair_comm(), ff2.chip_comm()))
    ff1.pair_done(got[:1])
    ff2.chip_done(got[1:])
    mid = scatter(dict(w_att_up=_mm_tn(oa, dya, "dw_att_up", a_is_t=True), w_ret_up=_mm_tn(yrp, dyr, "dw_ret_up"),
                       w_out=_mm_tn(merged, dh1, "dw_out")), "mid")
    (dqa, dkva, dsink), r2 = _attn_bwd(qt, kv, kvt, doa, sinks, ca, sa, sb, ct, st, ff1.chip_comm())
    ff1.chip_done(r2)
    (dret, dgain), r1 = _ret_bwd(qkr, vr, gr, ret, dyrp, states, gain, rtabs, cr, sr, mid.pair_comm())
    mid.pair_done(r1)
    d_in, r2 = _dw_in_t((dqa, dkva, dret, dgates), (True, False, False, False), xn1, mid.chip_comm())
    mid.chip_done(r2)
    win = scatter(dict(w_in=d_in), "in")
    win.pair_done(_comm_alone(win.pair_comm(), "pair_exchange_in"))
    (dx, dg1), r2 = _inproj_bwd(dqa, dkva, dret, dgates, dh1, x, g1, w_in, win.chip_comm())
    win.chip_done(r2)
    small = dict(norm_mix_gain=dg1, b_gates=db, attn_sinks=dsink, ret_gn_gain=dgain, norm_mlp_gain=dg2,
                 norm_final_gain=dg3)
    return loss, dx, small, (win, mid, ff1, ff2)


def _coords():
    return lax.axis_index("x"), lax.axis_index("y"), lax.axis_index("c")


def _flip(v, bit):
    return 1 - v if bit else v


def _xor(a, b):
    return a + b - 2 * a * b


def _gather_relay_comm(shards):
    n = len(shards)

    def parts(ins, outs, sems):
        send_sems, recv_sems, local_sems = sems
        x, y, c = _coords()
        me, sib = (x, y, c), (x, y, 1 - c)

        def ring(core):
            return ((_xor(x, core), _xor(y, 1 - core), core), (_xor(x, 1 - core), _xor(y, core), core),
                    (1 - x, 1 - y, core))

        def slot(a, blk):
            return outs[a].at[4 * blk[0] + 2 * blk[1] + blk[2]]

        def copy(a, k, blk, to, src=None):
            return pltpu.make_async_remote_copy(
                src_ref=slot(a, blk) if src is None else src, dst_ref=slot(a, blk),
                send_sem=send_sems.at[a, k], recv_sem=recv_sems.at[a, k], device_id=to, device_id_type=MESH)

        mine = [pltpu.make_async_copy(ins[a], slot(a, me), local_sems.at[a]) for a in range(n)]
        return me, sib, ring, copy, mine

    def first_copies(ins, outs, sems):
        me, sib, ring, copy, mine = parts(ins, outs, sems)
        src, dst, _ = ring(me[2])
        return mine, [copy(a, k, me, to, src=ins[a]) for a in range(n) for k, to in ((1, src), (2, dst), (0, sib))]

    def start(ins, outs, sems):
        mine, first = first_copies(ins, outs, sems)
        for cp in mine + first:
            cp.start()

    def relay(ins, outs, sems):
        me, sib, ring, copy, _ = parts(ins, outs, sems)
        src, dst, _ = ring(me[2])
        for a in range(n):
            copy(a, 1, src, me).wait_recv()
            copy(a, 3, src, dst).start()
            copy(a, 4, src, sib).start()

    def finish(ins, outs, sems):
        me, sib, ring, copy, _ = parts(ins, outs, sems)
        mine, first = first_copies(ins, outs, sems)
        src, dst, diag = ring(me[2])
        passed = [copy(a, k, src, to) for a in range(n) for k, to in ((3, dst), (4, sib))]
        for k, blk in ((2, dst), (3, diag)):
            for a in range(n):
                copy(a, k, blk, me).wait_recv()
                fwd = copy(a, k + 3, blk, sib)
                fwd.start()
                passed.append(fwd)
        s_src, s_dst, s_diag = ring(sib[2])
        for a in range(n):
            for k, blk in ((0, sib), (4, s_src), (5, s_dst), (6, s_diag)):
                copy(a, k, blk, me).wait_recv()
        for cp in first + passed:
            cp.wait_send()
        for cp in mine:
            cp.wait()

    return _Comm(list(shards), [jax.ShapeDtypeStruct((N_DEV,) + s.shape, s.dtype) for s in shards],
                 [pltpu.SemaphoreType.DMA((n, 7)), pltpu.SemaphoreType.DMA((n, 7)), pltpu.SemaphoreType.DMA((n,))],
                 start, finish, relay)


def _gather_comm(shards):
    n = len(shards)

    def parts(ins, outs, sems):
        send_sems, recv_sems, local_sems = sems
        x, y, c = _coords()
        me, sib = (x, y, c), (x, y, 1 - c)
        chips = [(1 - x, y), (x, 1 - y), (1 - x, 1 - y)]

        def slot(a, blk):
            return outs[a].at[4 * blk[0] + 2 * blk[1] + blk[2]]

        def copy(a, k, blk, to, src=None):
            return pltpu.make_async_remote_copy(
                src_ref=slot(a, blk) if src is None else src, dst_ref=slot(a, blk),
                send_sem=send_sems.at[a, k], recv_sem=recv_sems.at[a, k], device_id=to, device_id_type=MESH)

        mine = [pltpu.make_async_copy(ins[a], slot(a, me), local_sems.at[a]) for a in range(n)]
        first = []
        for a in range(n):
            first.append(copy(a, 0, me, sib, src=ins[a]))
            first += [copy(a, 1 + j, me, (*chip, c), src=ins[a]) for j, chip in enumerate(chips)]
        return me, sib, chips, c, copy, mine, first

    def start(ins, outs, sems):
        *_, mine, first = parts(ins, outs, sems)
        for cp in mine + first:
            cp.start()

    def finish(ins, outs, sems):
        me, sib, chips, c, copy, mine, first = parts(ins, outs, sems)
        passed = []
        for j, chip in enumerate(chips):
            for a in range(n):
                copy(a, 1 + j, (*chip, c), me).wait_recv()
                fwd = copy(a, 4 + j, (*chip, c), sib)
                fwd.start()
                passed.append(fwd)
        for a in range(n):
            copy(a, 0, sib, me).wait_recv()
            for j, chip in enumerate(chips):
                copy(a, 4 + j, (*chip, 1 - c), me).wait_recv()
        for cp in first + passed:
            cp.wait_send()
        for cp in mine:
            cp.wait()

    return _Comm(list(shards), [jax.ShapeDtypeStruct((N_DEV,) + s.shape, s.dtype) for s in shards],
                 [pltpu.SemaphoreType.DMA((n, 7)), pltpu.SemaphoreType.DMA((n, 7)), pltpu.SemaphoreType.DMA((n,))],
                 start, finish)


COLUMN_SHARDED = ("w_in", "w_ff1")
COLUMN_RELAID = ("w_att_up",)


def _gather(shards, names, relay=False):
    def unpack(got):
        return [jnp.transpose(g, (1, 0, 2)).reshape(g.shape[1], N_DEV * g.shape[2]) if k in COLUMN_RELAID
                else g.reshape(N_DEV * g.shape[1], g.shape[2]) for k, g in zip(names, got)]

    return (_gather_relay_comm if relay else _gather_comm)([shards[k] for k in names]), unpack


def _pair_comm(grads):
    n = len(grads)

    def copies(g, r1, sems):
        send_sems, recv_sems = sems
        x, y, c = _coords()
        return [pltpu.make_async_remote_copy(
            src_ref=g[a].at[2 * j + (1 - c)], dst_ref=r1[a].at[j], send_sem=send_sems.at[a, j],
            recv_sem=recv_sems.at[a, j], device_id=(x, y, 1 - c), device_id_type=MESH)
            for a in range(n) for j in range(4)]

    def start(g, r1, sems):
        for cp in copies(g, r1, sems):
            cp.start()

    def finish(g, r1, sems):
        for cp in copies(g, r1, sems):
            cp.wait_recv()
        for cp in copies(g, r1, sems):
            cp.wait_send()

    return _Comm(list(grads), [jax.ShapeDtypeStruct((4,) + g.shape[1:], g.dtype) for g in grads],
                 [pltpu.SemaphoreType.DMA((n, 4)), pltpu.SemaphoreType.DMA((n, 4))], start, finish)


def _small_comm(small):
    def parts(ins, outs, sems):
        (small_ref,), (small_all,) = ins, outs
        ssend, srecv, lsem = sems
        x, y, c = _coords()
        me_idx = 4 * x + 2 * y + c
        own = pltpu.make_async_copy(small_ref, small_all.at[me_idx], lsem)
        sends, recvs = [], []
        for r in range(1, N_DEV):
            px, py, pc = _flip(x, r & 4), _flip(y, r & 2), _flip(c, r & 1)
            sends.append(pltpu.make_async_remote_copy(
                src_ref=small_ref, dst_ref=small_all.at[me_idx], send_sem=ssend.at[r - 1], recv_sem=srecv.at[r - 1],
                device_id=(px, py, pc), device_id_type=MESH))
            recvs.append(pltpu.make_async_remote_copy(
                src_ref=small_ref, dst_ref=small_all.at[4 * px + 2 * py + pc], send_sem=ssend.at[r - 1],
                recv_sem=srecv.at[r - 1], device_id=(px, py, pc), device_id_type=MESH))
        return own, sends, recvs

    def start(ins, outs, sems):
        own, sends, _ = parts(ins, outs, sems)
        own.start()
        for cp in sends:
            cp.start()

    def finish(ins, outs, sems):
        own, sends, recvs = parts(ins, outs, sems)
        for cp in recvs:
            cp.wait_recv()
        for cp in sends:
            cp.wait_send()
        own.wait()

    return _Comm([small], [jax.ShapeDtypeStruct((N_DEV,) + small.shape, small.dtype)],
                 [pltpu.SemaphoreType.DMA((N_DEV - 1,)), pltpu.SemaphoreType.DMA((N_DEV - 1,)),
                  pltpu.SemaphoreType.DMA], start, finish)


def _pair_sum(grads, r1s, c_arr, tag):
    n = len(grads)
    q = 1

    def body(c_ref, *refs):
        g, r, t = refs[:n], refs[n:2 * n], refs[2 * n:]
        for a in range(n):
            t[a][...] = (g[a][...] + r[a][...]).astype(t[a].dtype)

    def blk(arr):
        return (1, arr.shape[1] // q, arr.shape[2])

    grid_spec = pltpu.PrefetchScalarGridSpec(
        num_scalar_prefetch=1, grid=(4, q),
        in_specs=[pl.BlockSpec(blk(g), lambda j, s, c_ref: (2 * j + c_ref[0], s, 0)) for g in grads]
        + [pl.BlockSpec(blk(r), lambda j, s, c_ref: (j, s, 0)) for r in r1s],
        out_specs=[pl.BlockSpec(blk(r), lambda j, s, c_ref: (j, s, 0)) for r in r1s])
    return pl.pallas_call(
        body, name="pair_sum_" + tag, grid_spec=grid_spec,
        out_shape=[jax.ShapeDtypeStruct(r.shape, RS_PAYLOAD) for r in r1s],
        compiler_params=_params(2),
    )(c_arr, *grads, *r1s)


def _chip_comm(ts):
    n = len(ts)

    def copies(t, r2, sems):
        send_sems, recv_sems = sems
        x, y, c = _coords()
        out = []
        for a in range(n):
            for r in range(1, 4):
                tx, ty = _flip(x, r & 2), _flip(y, r & 1)
                out.append(pltpu.make_async_remote_copy(
                    src_ref=t[a].at[2 * tx + ty], dst_ref=r2[a].at[r - 1], send_sem=send_sems.at[a, r - 1],
                    recv_sem=recv_sems.at[a, r - 1], device_id=(tx, ty, c), device_id_type=MESH))
        return out

    def start(t, r2, sems):
        for cp in copies(t, r2, sems):
            cp.start()

    def finish(t, r2, sems):
        for cp in copies(t, r2, sems):
            cp.wait_recv()
        for cp in copies(t, r2, sems):
            cp.wait_send()

    return _Comm(list(ts), [jax.ShapeDtypeStruct((3,) + t.shape[1:], t.dtype) for t in ts],
                 [pltpu.SemaphoreType.DMA((n, 3)), pltpu.SemaphoreType.DMA((n, 3))], start, finish)


class _Scatter:
    def __init__(self, grads, tag):
        self.names, self.tag = tuple(grads), tag
        self.stacks = [jnp.transpose(g.reshape(g.shape[0], N_DEV, g.shape[1] // N_DEV), (1, 0, 2))
                       if k in COLUMN_RELAID else g.reshape(N_DEV, g.shape[0] // N_DEV, g.shape[1])
                       for k, g in grads.items()]

    def pair_comm(self):
        return _pair_comm(self.stacks)

    def pair_done(self, r1s):
        self.r1s = list(r1s)
        c_arr = jnp.reshape(lax.axis_index("c"), (1,)).astype(jnp.int32)
        self.ts = _pair_sum(self.stacks, self.r1s, c_arr, self.tag)

    def chip_comm(self):
        return _chip_comm(self.ts)

    def chip_done(self, r2s):
        self.r2s = list(r2s)


def _adamw(w, g, m, v):
    m = ADAM_B1 * m + (1.0 - ADAM_B1) * g
    v = ADAM_B2 * v + (1.0 - ADAM_B2) * jnp.square(g)
    m_hat = m / (1.0 - ADAM_B1 ** ADAM_STEP)
    v_hat = v / (1.0 - ADAM_B2 ** ADAM_STEP)
    delta = -ADAM_LR * (m_hat / (jnp.sqrt(v_hat) + ADAM_EPS) + ADAM_WD * w)
    return delta, m, v


ADAM_STEPS = 4


def _piece_specs(stacks):
    def rows(s):
        return s.shape[1] // ADAM_STEPS
    return ([pl.BlockSpec((1, rows(s), s.shape[2]), lambda i, idx_ref: (idx_ref[0], i, 0)) for s in stacks]
            + [pl.BlockSpec((1, rows(s), s.shape[2]), lambda i, idx_ref: (idx_ref[1], i, 0)) for s in stacks]
            + [pl.BlockSpec((3, rows(s), s.shape[2]), lambda i, idx_ref: (0, i, 0)) for s in stacks])


def _piece_sum(g0, r1, r2):
    return (((g0[0] + r1[0]) + r2[0].astype(F32)) + r2[1].astype(F32)) + r2[2].astype(F32)


def _shard_sum(stacks, r1s, r2s, idx_arr):
    n = len(stacks)

    def body(idx_ref, *refs):
        g0, r1, r2, outs = (refs[k * n:(k + 1) * n] for k in range(4))
        for a in range(n):
            outs[a][...] = _piece_sum(g0[a], r1[a], r2[a])

    out_specs = [pl.BlockSpec((s.shape[1] // ADAM_STEPS, s.shape[2]), lambda i, idx_ref: (i, 0)) for s in stacks]
    grid_spec = pltpu.PrefetchScalarGridSpec(num_scalar_prefetch=1, grid=(ADAM_STEPS,),
                                             in_specs=_piece_specs(stacks), out_specs=out_specs)
    return pl.pallas_call(
        body, name="shard_sum", grid_spec=grid_spec,
        out_shape=[jax.ShapeDtypeStruct(s.shape[1:], F32) for s in stacks],
        compiler_params=_params(1),
    )(idx_arr, *stacks, *r1s, *r2s)


def _adam_big(pieces, summed, ws, ms, vs, idx_arr):
    stacks, r1s, r2s = pieces
    n_p, n = len(stacks), len(ws)

    def body(idx_ref, *refs):
        it = iter(refs)
        g0, r1, r2, gs, w, m, v = ([next(it) for _ in range(k)] for k in (n_p, n_p, n_p, n - n_p, n, n, n))
        outs = list(it)
        for a in range(n):
            g = _piece_sum(g0[a], r1[a], r2[a]) if a < n_p else gs[a - n_p][...]
            delta, nm, nv = _adamw(w[a][...], g, m[a][...], v[a][...])
            outs[4 * a][...] = g
            outs[4 * a + 1][...] = delta
            outs[4 * a + 2][...] = nm
            outs[4 * a + 3][...] = nv

    def shard_spec(w):
        return pl.BlockSpec((w.shape[0] // ADAM_STEPS, w.shape[1]), lambda i, idx_ref: (i, 0))

    in_specs = _piece_specs(stacks) + [shard_spec(w) for w in ws[n_p:]] + [shard_spec(w) for w in ws] * 3
    out_specs = [shard_spec(w) for w in ws for _ in range(4)]
    grid_spec = pltpu.PrefetchScalarGridSpec(num_scalar_prefetch=1, grid=(ADAM_STEPS,), in_specs=in_specs,
                                             out_specs=out_specs)
    return pl.pallas_call(
        body, name="adam_big", grid_spec=grid_spec,
        out_shape=[jax.ShapeDtypeStruct(w.shape, F32) for w in ws for _ in range(4)],
        compiler_params=_params(1),
    )(idx_arr, *stacks, *r1s, *r2s, *summed, *ws, *ms, *vs)


def _adam_small(small_all, w, m, v):
    def body(all_ref, w_ref, m_ref, v_ref, g_ref, d_ref, nm_ref, nv_ref):
        g = all_ref[0]
        for k in range(1, N_DEV):
            g = g + all_ref[k]
        delta, nm, nv = _adamw(w_ref[...], g, m_ref[...], v_ref[...])
        g_ref[...] = g
        d_ref[...] = delta
        nm_ref[...] = nm
        nv_ref[...] = nv

    return pl.pallas_call(
        body, name="adam_small",
        in_specs=[VMEM_FULL] * 4, out_specs=[VMEM_FULL] * 4,
        out_shape=[jax.ShapeDtypeStruct(w.shape, F32)] * 4,
    )(small_all, w, m, v)


RS_PAYLOAD = BF


def _pack_rows(rows):
    rid = lax.broadcasted_iota(jnp.int32, (8, D_MODEL), 0)
    out = jnp.zeros((8, D_MODEL), F32)
    for i, r in enumerate(rows):
        out = jnp.where(rid == i, jnp.broadcast_to(r, (8, D_MODEL)), out)
    return out


def _small_rows(norm_mix_gain, b_gates, attn_sinks, ret_gn_gain, norm_mlp_gain, norm_final_gain):
    return [norm_mix_gain, b_gates[:, :D_MODEL], b_gates[:, D_MODEL:], ret_gn_gain, norm_mlp_gain,
            norm_final_gain.reshape(1, D_MODEL), jnp.pad(attn_sinks, ((0, 0), (0, D_MODEL - ATT_HEADS)))]


def _unpack_small(p):
    return dict(norm_mix_gain=p[0:1], b_gates=jnp.concatenate([p[1:2], p[2:3]], axis=1), ret_gn_gain=p[3:4],
                norm_mlp_gain=p[4:5], norm_final_gain=p[5], attn_sinks=p[6:7, :ATT_HEADS])


WEIGHTS = ("norm_mix_gain", "w_in", "b_gates", "attn_sinks", "ret_gn_gain", "w_att_up", "w_ret_up", "w_out",
           "norm_mlp_gain", "w_ff1", "w_ff2", "norm_final_gain")
BIG = ("w_in", "w_att_up", "w_ret_up", "w_out", "w_ff1", "w_ff2")


def kernel(x, norm_mix_gain, w_in, b_gates, attn_sinks, ret_gn_gain, w_att_up, w_ret_up, w_out, norm_mlp_gain, w_ff1, w_ff2, norm_final_gain, loss_target, m_norm_mix_gain, m_w_in, m_b_gates, m_attn_sinks, m_ret_gn_gain, m_w_att_up, m_w_ret_up, m_w_out, m_norm_mlp_gain, m_w_ff1, m_w_ff2, m_norm_final_gain, v_norm_mix_gain, v_w_in, v_b_gates, v_attn_sinks, v_ret_gn_gain, v_w_att_up, v_w_ret_up, v_w_out, v_norm_mlp_gain, v_w_ff1, v_w_ff2, v_norm_final_gain):
    w = dict(norm_mix_gain=norm_mix_gain, w_in=w_in, b_gates=b_gates, attn_sinks=attn_sinks, ret_gn_gain=ret_gn_gain,
             w_att_up=w_att_up, w_ret_up=w_ret_up, w_out=w_out, norm_mlp_gain=norm_mlp_gain, w_ff1=w_ff1,
             w_ff2=w_ff2, norm_final_gain=norm_final_gain)
    m = dict(norm_mix_gain=m_norm_mix_gain, w_in=m_w_in, b_gates=m_b_gates, attn_sinks=m_attn_sinks,
             ret_gn_gain=m_ret_gn_gain, w_att_up=m_w_att_up, w_ret_up=m_w_ret_up, w_out=m_w_out,
             norm_mlp_gain=m_norm_mlp_gain, w_ff1=m_w_ff1, w_ff2=m_w_ff2, norm_final_gain=m_norm_final_gain)
    v = dict(norm_mix_gain=v_norm_mix_gain, w_in=v_w_in, b_gates=v_b_gates, attn_sinks=v_attn_sinks,
             ret_gn_gain=v_ret_gn_gain, w_att_up=v_w_att_up, w_ret_up=v_w_ret_up, w_out=v_w_out,
             norm_mlp_gain=v_norm_mlp_gain, w_ff1=v_w_ff1, w_ff2=v_w_ff2, norm_final_gain=v_norm_final_gain)

    shards = {k: (w[k][0].T if k in COLUMN_SHARDED else w[k][0]).astype(BF) for k in BIG}
    comm, unpack = _gather(shards, ("w_in",), relay=True)
    (w_in_full,) = unpack(_comm_alone(comm, "allgather_w_in"))

    loss_p, dx, small, groups = _local_step(
        x[0], loss_target[0], norm_mix_gain, b_gates, attn_sinks[0], ret_gn_gain, norm_mlp_gain,
        norm_final_gain.reshape(1, D_MODEL), w_in_full, shards)

    by_name = {}
    for grp in groups:
        for k, stack, r1, r2 in zip(grp.names, grp.stacks, grp.r1s, grp.r2s):
            by_name[k] = (stack, r1, r2)
    lane = lax.broadcasted_iota(jnp.int32, (1, D_MODEL), 1)
    sink_row = jnp.where(lane < ATT_HEADS, jnp.pad(small["attn_sinks"], ((0, 0), (0, D_MODEL - LANE))),
                         jnp.where(lane == ATT_HEADS, jnp.pad(loss_p, ((0, 0), (0, D_MODEL - LANE)), mode="edge"),
                                   0.0))
    small_pack = _pack_rows([small["norm_mix_gain"], small["b_gates"][:, :D_MODEL], small["b_gates"][:, D_MODEL:],
                             small["ret_gn_gain"], small["norm_mlp_gain"], small["norm_final_gain"], sink_row])
    xi, yi, ci = _coords()
    (small_all,) = _comm_alone(_small_comm(small_pack), "small_exchange")
    idx_arr = jnp.stack([4 * xi + 2 * yi + ci, 2 * xi + yi]).astype(jnp.int32)
    in_pieces = ["w_att_up", "w_ret_up", "w_out", "w_ff2", "w_in"]
    in_sum = ["w_ff1"]
    order = in_pieces + in_sum
    summed_t = _shard_sum(*[[by_name[k][i] for k in in_sum] for i in range(3)], idx_arr)

    def shard(tree, k):
        return tree[k][0].T if k == "w_in" else tree[k][0]

    adam_out = _adam_big([[by_name[k][i] for k in in_pieces] for i in range(3)], [g.T for g in summed_t],
                         [shard(w, k) for k in order], [shard(m, k) for k in order], [shard(v, k) for k in order],
                         idx_arr)
    big_out = [adam_out[4 * order.index(k) + i].T if k == "w_in" else adam_out[4 * order.index(k) + i]
               for k in BIG for i in range(4)]
    sm_g, sm_d, sm_m, sm_v = _adam_small(small_all, _pack_rows(_small_rows(*[w[k] for k in WEIGHTS if k not in BIG])),
                                         _pack_rows(_small_rows(*[m[k] for k in WEIGHTS if k not in BIG])),
                                         _pack_rows(_small_rows(*[v[k] for k in WEIGHTS if k not in BIG])))

    loss = sm_g[6, ATT_HEADS]
    kinds = []
    for idx, packed in enumerate((sm_g, sm_d, sm_m, sm_v)):
        out = _unpack_small(packed)
        for a, k in enumerate(BIG):
            out[k] = big_out[4 * a + idx][None]
        kinds.append(out)
    return (loss, dx[None], *[kinds[0][k] for k in WEIGHTS], *[kinds[1][k] for k in WEIGHTS],
            *[kinds[2][k] for k in WEIGHTS], *[kinds[3][k] for k in WEIGHTS])
```

```python
import functools
import math

import jax
import jax.numpy as jnp
from jax import lax
from jax.experimental import pallas as pl
from jax.experimental.pallas import tpu as pltpu

F32 = jnp.float32
BF = jnp.bfloat16
MESH = pl.DeviceIdType.MESH

D_MODEL = 1024
ATT_HEADS = 8
ATT_HEAD_DIM = 64
ATT_BLOCK = 128
ROPE_DIM = 16
ROPE_THETA = 500000.0
RET_HEADS = 4
RET_KEY_DIM = 128
RET_VAL_DIM = 256
RET_CHUNK = 128
RET_ROT_BASE = 10000.0
D_FF = 4096
NORM_EPS = 1e-6
GN_EPS = 1e-6
NEG_INF = -1e30
ATT_SCALE = ATT_HEAD_DIM ** -0.5
RET_SCALE = RET_KEY_DIM ** -0.5

C_QA, C_KA, C_VA, C_QR, C_KR, C_VR, C_GR, C_GATES, C_END = 0, 512, 640, 768, 1280, 1792, 2816, 3840, 5888

ADAM_LR = 0.001
ADAM_B1 = 0.9
ADAM_B2 = 0.999
ADAM_EPS = 1e-08
ADAM_WD = 0.01
ADAM_STEP = 10

N_DEV = 8
LANE = 128
VMEM_LIMIT = 56 * 1024 * 1024
TOKEN_TILE = 256
TN_TOKEN_TILE = 2048
FF_CHUNKS = (0, 1024, 2048, 3072, 4096)
IN_CHUNKS_FWD = (C_QA, C_KA, C_QR, C_KR, C_VR, C_GR, C_GATES, C_END)
IN_CHUNKS_BWD = (C_QA, C_KA, C_QR, C_GATES, C_END)

RET_LOG_GAMMA = tuple(math.log1p(-(2.0 ** (-5.0 - h))) for h in range(RET_HEADS))
RET_CHUNK_DECAY = tuple(math.exp(RET_CHUNK * lg) for lg in RET_LOG_GAMMA)

VMEM_FULL = pl.BlockSpec(memory_space=pltpu.VMEM)
SMEM_FULL = pl.BlockSpec(memory_space=pltpu.SMEM)
ANY = pl.BlockSpec(memory_space=pl.ANY)


def _params(n_axes):
    return pltpu.CompilerParams(dimension_semantics=("arbitrary",) * n_axes, vmem_limit_bytes=VMEM_LIMIT)


def _nn(a, b):
    return jnp.dot(a, b, preferred_element_type=F32)


def _nt(a, b):
    return lax.dot_general(a, b, (((1,), (1,)), ((), ())), preferred_element_type=F32)


def _tn(a, b):
    return lax.dot_general(a, b, (((0,), (0,)), ((), ())), preferred_element_type=F32)


def _sigmoid(v):
    return 1.0 / (1.0 + jnp.exp(-v))


def _rows(tile, width):
    return pl.BlockSpec((tile, width), lambda i: (i, 0))


def _cols(height, tile):
    return pl.BlockSpec((height, tile), lambda i: (0, i))


class _Resident:
    def __init__(self, hbm_ref, vmem_ref, sems, bounds):
        self.hbm, self.vmem, self.sems, self.bounds = hbm_ref, vmem_ref, sems, bounds

    @staticmethod
    def scratch(w, bounds):
        return [pltpu.VMEM(w.shape, w.dtype), pltpu.SemaphoreType.DMA((len(bounds) - 1,))]

    @staticmethod
    def load(*weights):
        @pl.when(pl.program_id(0) == 0)
        def _():
            copies = [w._copy(c) for w in weights for c in range(len(w.bounds) - 1)]
            for cp in copies:
                cp.start()
            for cp in copies:
                cp.wait()

    def _rows(self, c):
        return pl.ds(self.bounds[c], self.bounds[c + 1] - self.bounds[c])

    def _copy(self, c):
        return pltpu.make_async_copy(self.hbm.at[self._rows(c)], self.vmem.at[self._rows(c)], self.sems.at[c])

    def chunk(self, c):
        return self.vmem.at[self._rows(c)]


class _Comm:
    def __init__(self, inputs, out_shapes, scratch, start, finish, relay=None):
        self.inputs, self.out_shapes, self.scratch = inputs, out_shapes, scratch
        self.start, self.finish, self.relay = start, finish, relay


RELAY_AT = 0.5


def _join(a, b):
    na_in, na_out, na_sem = len(a.inputs), len(a.out_shapes), len(a.scratch)

    def both(name):
        def run(ins, outs, sems):
            for part, args in ((a, (ins[:na_in], outs[:na_out], sems[:na_sem])),
                               (b, (ins[na_in:], outs[na_out:], sems[na_sem:]))):
                if getattr(part, name) is not None:
                    getattr(part, name)(*args)
        return run

    return _Comm(list(a.inputs) + list(b.inputs), list(a.out_shapes) + list(b.out_shapes),
                 list(a.scratch) + list(b.scratch), both("start"), both("finish"),
                 both("relay") if a.relay or b.relay else None)


def _call(body, args, comm=None, *, name, grid, in_specs, out_specs, out_shape, scratch_shapes=()):
    params = _params(len(grid))
    if comm is None:
        return pl.pallas_call(body, name=name, grid=grid, in_specs=in_specs, out_specs=out_specs, out_shape=out_shape,
                              scratch_shapes=scratch_shapes, compiler_params=params)(*args), ()
    single = not isinstance(out_specs, (list, tuple))
    out_specs_l = [out_specs] if single else list(out_specs)
    out_shape_l = [out_shape] if single else list(out_shape)
    n_in, n_out, n_scr = len(in_specs), len(out_specs_l), len(scratch_shapes)
    n_cin, n_cout = len(comm.inputs), len(comm.out_shapes)

    def hosted(*refs):
        it = iter(refs)
        ins, cin, outs, cout, scr = ([next(it) for _ in range(k)] for k in (n_in, n_cin, n_out, n_cout, n_scr))
        sems = list(it)
        ids = [pl.program_id(k) for k in range(len(grid))]
        first = functools.reduce(jnp.logical_and, [i == 0 for i in ids])
        last = functools.reduce(jnp.logical_and, [i == g - 1 for i, g in zip(ids, grid)])

        @pl.when(first)
        def _():
            comm.start(cin, cout, sems)

        if comm.relay is not None:
            at = [int(grid[0] * RELAY_AT)] + [0] * (len(grid) - 1)

            @pl.when(functools.reduce(jnp.logical_and, [i == v for i, v in zip(ids, at)]))
            def _():
                comm.relay(cin, cout, sems)

        body(*ins, *outs, *scr)

        @pl.when(last)
        def _():
            comm.finish(cin, cout, sems)

    res = pl.pallas_call(
        hosted, name=name, grid=grid, in_specs=list(in_specs) + [ANY] * n_cin,
        out_specs=out_specs_l + [ANY] * n_cout, out_shape=out_shape_l + list(comm.out_shapes),
        scratch_shapes=list(scratch_shapes) + list(comm.scratch), compiler_params=params)(*args, *comm.inputs)
    return (res[0] if single else res[:n_out]), res[n_out:]


def _comm_alone(comm, name):
    n_cin, n_cout = len(comm.inputs), len(comm.out_shapes)

    def body(*refs):
        cin, cout, sems = refs[:n_cin], refs[n_cin:n_cin + n_cout], refs[n_cin + n_cout:]
        comm.start(cin, cout, sems)
        if comm.relay is not None:
            comm.relay(cin, cout, sems)
        comm.finish(cin, cout, sems)

    return pl.pallas_call(body, name=name, in_specs=[ANY] * n_cin, out_specs=[ANY] * n_cout,
                          out_shape=list(comm.out_shapes), scratch_shapes=list(comm.scratch))(*comm.inputs)


def _slabs(v, fn):
    return jnp.concatenate([fn(v[:, LANE * j:LANE * (j + 1)]) for j in range(v.shape[1] // LANE)], axis=1)


def _rope_att(v, ca, sa, sb):
    return _slabs(v, lambda t: t * ca + pltpu.roll(t, LANE - 8, 1) * sa + pltpu.roll(t, 8, 1) * sb)


def _rope_att_t(v, ca, sa, sb):
    return _slabs(v, lambda t: t * ca + pltpu.roll(t * sa, 8, 1) + pltpu.roll(t * sb, LANE - 8, 1))


def _rope_att_rows(v, ct, st, sign):
    parts = []
    for h in range(v.shape[0] // ATT_HEAD_DIM):
        r0 = ATT_HEAD_DIM * h
        x1, x2 = v[r0:r0 + 8], v[r0 + 8:r0 + 16]
        parts += [x1 * ct - sign * (x2 * st), x2 * ct + sign * (x1 * st), v[r0 + 16:r0 + ATT_HEAD_DIM]]
    return jnp.concatenate(parts, axis=0)


def _rope_ret(v, cr, sr):
    return _slabs(v, lambda t: t * cr + pltpu.roll(t, 64, 1) * sr)


def _rope_ret_t(v, cr, sr):
    return _slabs(v, lambda t: t * cr + pltpu.roll(t * sr, 64, 1))


def _rope_tables(seq):
    pos = jnp.arange(seq, dtype=F32)

    def cs(dim, theta):
        inv = theta ** (-jnp.arange(0, dim, 2, dtype=F32) / dim)
        ang = pos[:, None] * inv[None, :]
        return jnp.cos(ang), jnp.sin(ang)

    ca, sa_ = cs(ROPE_DIM, ROPE_THETA)
    zeros8 = jnp.zeros_like(ca)
    rest = ATT_HEAD_DIM - ROPE_DIM
    c64 = jnp.concatenate([ca, ca, jnp.ones((seq, rest), F32)], axis=1)
    sa64 = jnp.concatenate([-sa_, zeros8, jnp.zeros((seq, rest), F32)], axis=1)
    sb64 = jnp.concatenate([zeros8, sa_, jnp.zeros((seq, rest), F32)], axis=1)
    cr, sr_ = cs(RET_KEY_DIM, RET_ROT_BASE)
    return (jnp.tile(c64, (1, 2)), jnp.tile(sa64, (1, 2)), jnp.tile(sb64, (1, 2)),
            jnp.concatenate([cr, cr], axis=1), jnp.concatenate([-sr_, sr_], axis=1), ca.T, sa_.T)


def _ret_tables():
    c = RET_CHUNK
    lg = jnp.asarray(RET_LOG_GAMMA, F32)
    idx = jnp.arange(c, dtype=F32)
    diff = idx[:, None] - idx[None, :]
    intra = jnp.where(diff >= 0, jnp.exp(jnp.maximum(diff, 0.0) * lg[:, None, None]), 0.0)
    qd = jnp.exp((idx + 1.0)[None, :] * lg[:, None])[..., None]
    kd = jnp.exp((c - 1.0 - idx)[None, :] * lg[:, None])[..., None]
    return intra, jnp.broadcast_to(qd, (RET_HEADS, c, RET_KEY_DIM)), jnp.broadcast_to(kd, (RET_HEADS, c, RET_KEY_DIM))


def _inproj_fwd(x, g1, w_in, tabs, comm=None):
    seq = x.shape[0]
    tm = min(TOKEN_TILE, seq)

    def body(x_ref, g_ref, w_hbm, ca_ref, sa_ref, sb_ref, cr_ref, sr_ref, ct_ref, st_ref,
             xn_ref, qt_ref, kv_ref, kvt_ref, qkr_ref, vr_ref, gr_ref, gates_ref, w_vmem, w_sems):
        w = _Resident(w_hbm, w_vmem, w_sems, IN_CHUNKS_FWD)
        _Resident.load(w)
        xf = x_ref[...]
        r = lax.rsqrt(jnp.mean(xf * xf, axis=-1, keepdims=True) + NORM_EPS)
        xn = (xf * r * g_ref[...]).astype(BF)
        xn_ref[...] = xn
        ct, st = ct_ref[...], st_ref[...]
        qt_ref[...] = _rope_att_rows(_nt(w.chunk(0)[...], xn), ct, st, 1.0).astype(BF)
        w_kv = w.chunk(1)
        kvt = _nt(w_kv[...], xn)
        kvt_ref[:LANE, :] = _rope_att_rows(kvt[:LANE], ct, st, 1.0).astype(BF)
        kvt_ref[LANE:, :] = kvt[LANE:].astype(BF)
        kvn = _nt(xn, w_kv[...])
        kv_ref[:, :LANE] = _rope_att(kvn[:, :LANE], ca_ref[...], sa_ref[...], sb_ref[...]).astype(BF)
        kv_ref[:, LANE:] = kvn[:, LANE:].astype(BF)
        cr, sr = cr_ref[...], sr_ref[...]
        qr = _rope_ret(_nt(xn, w.chunk(2)[...]), cr, sr)
        qkr_ref[:, :512] = qr.astype(BF)
        kr = _rope_ret(_nt(xn, w.chunk(3)[...]), cr, sr) * RET_SCALE
        qkr_ref[:, 512:] = kr.astype(BF)
        vr_ref[...] = _nt(xn, w.chunk(4)[...]).astype(BF)
        gr_ref[...] = _nt(xn, w.chunk(5)[...]).astype(BF)
        gates_ref[...] = _nt(xn, w.chunk(6)[...]).astype(BF)

    tab = _rows(tm, LANE)
    tab_t = _cols(8, tm)
    return _call(
        body, (x, g1, w_in, *tabs), comm, name="inproj_fwd", grid=(seq // tm,),
        in_specs=[_rows(tm, D_MODEL), VMEM_FULL, ANY, tab, tab, tab, tab, tab, tab_t, tab_t],
        out_specs=[_rows(tm, D_MODEL), _cols(512, tm), _rows(tm, 256), _cols(256, tm), _rows(tm, 1024),
                   _rows(tm, 1024), _rows(tm, 1024), _rows(tm, 2048)],
        out_shape=[jax.ShapeDtypeStruct(s, BF) for s in ((seq, D_MODEL), (512, seq), (seq, 256), (256, seq),
                                                         (seq, 1024), (seq, 1024), (seq, 1024), (seq, 2048))],
        scratch_shapes=_Resident.scratch(w_in, IN_CHUNKS_FWD))


ATT_Q_PER_KV = ATT_HEADS // 2


def _att_group(ref, kv):
    d = ATT_HEAD_DIM
    g = jnp.concatenate([ref[d * (ATT_Q_PER_KV * kv + j):d * (ATT_Q_PER_KV * kv + j + 1), :]
                         for j in range(ATT_Q_PER_KV)], axis=1)
    z = jnp.zeros_like(g)
    return jnp.concatenate([g, z] if kv == 0 else [z, g], axis=0)


def _att_probs(kwin, qpad, sink_ref, kv, i):
    c = ATT_BLOCK
    kj = lax.broadcasted_iota(jnp.int32, (2 * c, c), 0)
    qi = lax.broadcasted_iota(jnp.int32, (2 * c, c), 1)
    allowed = (kj > qi) & (kj <= qi + c) & ((kj >= c) | (i > 0))
    allowed = jnp.concatenate([allowed] * ATT_Q_PER_KV, axis=1)
    s = jnp.where(allowed, _nn(kwin, qpad) * ATT_SCALE, NEG_INF)
    sk = jnp.concatenate([jnp.full((1, c), sink_ref[ATT_Q_PER_KV * kv + j], F32) for j in range(ATT_Q_PER_KV)], axis=1)
    m = jnp.maximum(jnp.max(s, axis=0, keepdims=True), sk)
    pe = jnp.exp(s - m)
    psink = jnp.exp(sk - m)
    inv = 1.0 / (jnp.sum(pe, axis=0, keepdims=True) + psink)
    return pe * inv, psink * inv


def _seq_fwd(qt, kv, kvt, sinks, qkr, vr, gr, gain, rtabs, comm=None):
    seq = kv.shape[0]
    c = ATT_BLOCK
    assert c == RET_CHUNK
    d = ATT_HEAD_DIM
    nb = seq // c
    dk, dv = RET_KEY_DIM, RET_VAL_DIM

    def body(sink_ref, qt_ref, kvc_ref, kvp_ref, kvtc_ref, kvtp_ref, qk_ref, v_ref, g_ref, gain_ref, intra_ref,
             qd_ref, kd_ref, ot_ref, ret_ref, yrp_ref, st_ref, state):
        i = pl.program_id(0)
        kwin = jnp.concatenate([kvp_ref[:, :LANE], kvc_ref[:, :LANE]], axis=0)
        vtwin = jnp.concatenate([kvtp_ref[LANE:, :], kvtc_ref[LANE:, :]], axis=1)
        for g in (0, 1):
            prob, _ = _att_probs(kwin, _att_group(qt_ref, g), sink_ref, g, i)
            out = _nn(vtwin[d * g:d * (g + 1), :], prob.astype(BF))
            for j in range(ATT_Q_PER_KV):
                h = ATT_Q_PER_KV * g + j
                ot_ref[d * h:d * (h + 1), :] = out[:, c * j:c * (j + 1)].astype(BF)

        @pl.when(i == 0)
        def _():
            state[...] = jnp.zeros_like(state)

        for h in range(RET_HEADS):
            qh = qk_ref[:, dk * h:dk * (h + 1)]
            kh = qk_ref[:, 512 + dk * h:512 + dk * (h + 1)]
            vh = v_ref[:, dv * h:dv * (h + 1)]
            sh = state[h]
            shb = sh.astype(BF)
            st_ref[0, h] = shb
            att = _nt(qh, kh) * intra_ref[h]
            inner = _nn(att.astype(BF), vh)
            cross = _nn((qh.astype(F32) * qd_ref[h]).astype(BF), shb)
            out = inner + cross
            state[h] = sh * RET_CHUNK_DECAY[h] + _tn((kh.astype(F32) * kd_ref[h]).astype(BF), vh)
            ret_ref[:, dv * h:dv * (h + 1)] = out
            mu = jnp.mean(out, axis=-1, keepdims=True)
            dev = out - mu
            var = jnp.mean(dev * dev, axis=-1, keepdims=True)
            y = dev * lax.rsqrt(var + GN_EPS) * gain_ref[:, dv * h:dv * (h + 1)]
            g = g_ref[:, dv * h:dv * (h + 1)].astype(F32)
            yrp_ref[:, dv * h:dv * (h + 1)] = (g * _sigmoid(g) * y).astype(BF)

    prev = lambda i: jnp.maximum(i - 1, 0)
    return _call(
        body, (sinks, qt, kv, kv, kvt, kvt, qkr, vr, gr, gain, *rtabs), comm, name="seq_fwd", grid=(nb,),
        in_specs=[SMEM_FULL, _cols(512, c), _rows(c, 256), pl.BlockSpec((c, 256), lambda i: (prev(i), 0)),
                  _cols(256, c), pl.BlockSpec((256, c), lambda i: (0, prev(i))),
                  _rows(c, 1024), _rows(c, 1024), _rows(c, 1024), VMEM_FULL, VMEM_FULL, VMEM_FULL, VMEM_FULL],
        out_specs=[_cols(512, c), _rows(c, 1024), _rows(c, 1024),
                   pl.BlockSpec((1, RET_HEADS, dk, dv), lambda i: (i, 0, 0, 0))],
        out_shape=[jax.ShapeDtypeStruct((512, seq), BF), jax.ShapeDtypeStruct((seq, 1024), F32),
                   jax.ShapeDtypeStruct((seq, 1024), BF), jax.ShapeDtypeStruct((nb, RET_HEADS, dk, dv), BF)],
        scratch_shapes=[pltpu.VMEM((RET_HEADS, dk, dv), F32)])


def _mix_fwd(oat, yrp, gates, bg, x, w_att, w_ret, w_out):
    seq = x.shape[0]
    tm = min(TOKEN_TILE, seq)

    def body(oat_ref, yrp_ref, gates_ref, bg_ref, x_ref, wa_ref, wr_ref, wo_ref, ya_ref, yr_ref, mg_ref, h1_ref):
        ya = _tn(oat_ref[...], wa_ref[...])
        yr = _nn(yrp_ref[...], wr_ref[...])
        gt = _sigmoid(gates_ref[...].astype(F32) + bg_ref[...])
        merged = (gt[:, :D_MODEL] * ya + gt[:, D_MODEL:] * yr).astype(BF)
        ya_ref[...] = ya.astype(BF)
        yr_ref[...] = yr.astype(BF)
        mg_ref[...] = merged
        h1_ref[...] = x_ref[...] + _nn(merged, wo_ref[...])

    return pl.pallas_call(
        body, name="mix_fwd", grid=(seq // tm,),
        in_specs=[_cols(512, tm), _rows(tm, 1024), _rows(tm, 2048), VMEM_FULL, _rows(tm, D_MODEL),
                  VMEM_FULL, VMEM_FULL, VMEM_FULL],
        out_specs=[_rows(tm, D_MODEL)] * 4,
        out_shape=[jax.ShapeDtypeStruct((seq, D_MODEL), BF)] * 3 + [jax.ShapeDtypeStruct((seq, D_MODEL), F32)],
        compiler_params=_params(1),
    )(oat, yrp, gates, bg, x, w_att, w_ret, w_out)


def _mlp_loss_step(h1, g2, g3, target, w_ff1, w_ff2):
    seq = h1.shape[0]
    tm = min(TOKEN_TILE, seq)
    n_chunks = len(FF_CHUNKS) - 1

    def body(h1_ref, g2_ref, g3_ref, t_ref, w1_hbm, w2_hbm,
             xn2_ref, hdn_ref, dh2_ref, du_ref, dh1_ref, loss_ref, dg3_ref, dg2_ref,
             w1_vmem, w1_sems, w2_vmem, w2_sems, relu_u):
        w1 = _Resident(w1_hbm, w1_vmem, w1_sems, FF_CHUNKS)
        w2 = _Resident(w2_hbm, w2_vmem, w2_sems, FF_CHUNKS)
        _Resident.load(w1, w2)

        @pl.when(pl.program_id(0) == 0)
        def _():
            loss_ref[...] = jnp.zeros_like(loss_ref)
            dg3_ref[...] = jnp.zeros_like(dg3_ref)
            dg2_ref[...] = jnp.zeros_like(dg2_ref)

        h1v = h1_ref[...]
        r2 = lax.rsqrt(jnp.mean(h1v * h1v, axis=-1, keepdims=True) + NORM_EPS)
        xn2 = (h1v * r2 * g2_ref[...]).astype(BF)
        xn2_ref[...] = xn2
        h2 = h1v
        for c in range(n_chunks):
            cols = slice(FF_CHUNKS[c], FF_CHUNKS[c + 1])
            a = jnp.maximum(_nt(xn2, w1.chunk(c)[...]), 0.0)
            relu_u[:, cols] = a.astype(BF)
            hdn = jnp.square(a).astype(BF)
            hdn_ref[:, cols] = hdn
            h2 = h2 + _nn(hdn, w2.chunk(c)[...])
        r3 = lax.rsqrt(jnp.mean(h2 * h2, axis=-1, keepdims=True) + NORM_EPS)
        hn = h2 * r3
        err = hn * g3_ref[...] - t_ref[...]
        loss_ref[...] += jnp.sum(err * err) * (0.5 / D_MODEL)
        dy = err * (1.0 / D_MODEL)
        dg3_ref[...] += jnp.sum(dy * hn, axis=0, keepdims=True)
        z = dy * g3_ref[...]
        dh2 = r3 * (z - hn * jnp.mean(z * hn, axis=-1, keepdims=True))
        dh2b = dh2.astype(BF)
        dh2_ref[...] = dh2b
        dxn2 = jnp.zeros_like(dh2)
        for c in range(n_chunks):
            cols = slice(FF_CHUNKS[c], FF_CHUNKS[c + 1])
            du = (_nt(dh2b, w2.chunk(c)[...]) * (2.0 * relu_u[:, cols].astype(F32))).astype(BF)
            du_ref[:, cols] = du
            dxn2 = dxn2 + _nn(du, w1.chunk(c)[...])
        dnorm, dg = _rms_bwd(dxn2, h1v, g2_ref[...])
        dh1_ref[...] = dh2 + dnorm
        dg2_ref[...] += dg

    gain_acc = pl.BlockSpec((1, D_MODEL), lambda i: (0, 0))
    return pl.pallas_call(
        body, name="mlp_loss_step", grid=(seq // tm,),
        in_specs=[_rows(tm, D_MODEL), VMEM_FULL, VMEM_FULL, _rows(tm, D_MODEL), ANY, ANY],
        out_specs=[_rows(tm, D_MODEL), _rows(tm, D_FF), _rows(tm, D_MODEL), _rows(tm, D_FF), _rows(tm, D_MODEL),
                   pl.BlockSpec((1, LANE), lambda i: (0, 0)), gain_acc, gain_acc],
        out_shape=[jax.ShapeDtypeStruct((seq, D_MODEL), BF), jax.ShapeDtypeStruct((seq, D_FF), BF),
                   jax.ShapeDtypeStruct((seq, D_MODEL), BF), jax.ShapeDtypeStruct((seq, D_FF), BF),
                   jax.ShapeDtypeStruct((seq, D_MODEL), F32), jax.ShapeDtypeStruct((1, LANE), F32),
                   jax.ShapeDtypeStruct((1, D_MODEL), F32), jax.ShapeDtypeStruct((1, D_MODEL), F32)],
        scratch_shapes=(_Resident.scratch(w_ff1, FF_CHUNKS) + _Resident.scratch(w_ff2, FF_CHUNKS)
                        + [pltpu.VMEM((tm, D_FF), BF)]),
        compiler_params=_params(1),
    )(h1, g2, g3, target, w_ff1, w_ff2)


def _rms_bwd(dxn, xin, gain):
    r = lax.rsqrt(jnp.mean(xin * xin, axis=-1, keepdims=True) + NORM_EPS)
    xhat = xin * r
    z = dxn * gain
    dxin = r * (z - xhat * jnp.mean(z * xhat, axis=-1, keepdims=True))
    return dxin, jnp.sum(dxn * xhat, axis=0, keepdims=True)


def _mm_tn(a, b, name, a_is_t=False, comm=None):
    kdim, seq = a.shape if a_is_t else a.shape[::-1]
    ndim = b.shape[1]
    ts = min(TN_TOKEN_TILE, seq)
    tk = min(kdim, 1024)
    tn = min(ndim, 1024)
    n_steps = seq // ts

    def body(a_ref, b_ref, o_ref):
        @pl.when(pl.program_id(2) == 0)
        def _():
            o_ref[...] = jnp.zeros_like(o_ref)

        mm = _nn if a_is_t else _tn
        o_ref[...] += mm(a_ref[...].astype(BF), b_ref[...].astype(BF))

    a_spec = (pl.BlockSpec((tk, ts), lambda k, n, s: (k, s)) if a_is_t
              else pl.BlockSpec((ts, tk), lambda k, n, s: (s, k)))
    res = _call(
        body, (a, b), comm, name=name, grid=(kdim // tk, ndim // tn, n_steps),
        in_specs=[a_spec, pl.BlockSpec((ts, tn), lambda k, n, s: (s, n))],
        out_specs=pl.BlockSpec((tk, tn), lambda k, n, s: (k, n)),
        out_shape=jax.ShapeDtypeStruct((kdim, ndim), F32))
    return res[0] if comm is None else res


def _dw_in_t(pieces, is_t, xn1, comm=None):
    seq = xn1.shape[0]
    ts = min(TN_TOKEN_TILE, seq)
    tr = 256
    n_blk = [(p.shape[0] if t else p.shape[1]) // tr for p, t in zip(pieces, is_t)]
    offs = [sum(n_blk[:i]) for i in range(len(pieces) + 1)]
    n_p = len(pieces)

    def body(*refs):
        piece_refs, xn_ref, o_ref = refs[:n_p], refs[n_p], refs[n_p + 1]
        r, s = pl.program_id(0), pl.program_id(1)

        @pl.when(s == 0)
        def _():
            o_ref[...] = jnp.zeros_like(o_ref)

        for p in range(n_p):
            @pl.when((r >= offs[p]) & (r < offs[p + 1]))
            def _(p=p):
                mm = _nn if is_t[p] else _tn
                o_ref[...] += mm(piece_refs[p][...], xn_ref[pl.ds(pl.multiple_of(s * ts, ts), ts), :])

    def piece_spec(p):
        def index(r, s):
            inside = (r >= offs[p]) & (r < offs[p + 1])
            tok, blk = jnp.where(inside, s, 0), jnp.clip(r - offs[p], 0, n_blk[p] - 1)
            return (blk, tok) if is_t[p] else (tok, blk)
        return pl.BlockSpec((tr, ts) if is_t[p] else (ts, tr), index)

    return _call(
        body, (*pieces, xn1), comm, name="dw_in", grid=(offs[-1], seq // ts),
        in_specs=[piece_spec(p) for p in range(n_p)] + [VMEM_FULL],
        out_specs=pl.BlockSpec((tr, D_MODEL), lambda r, s: (r, 0)),
        out_shape=jax.ShapeDtypeStruct((offs[-1] * tr, D_MODEL), F32))


def _mix_bwd(dh1, ya, yr, gates, bg, w_att, w_ret, w_out, comm=None):
    seq = dh1.shape[0]
    tm = min(TOKEN_TILE, seq)

    def body(dh1_ref, ya_ref, yr_ref, gates_ref, bg_ref, wa_ref, wr_ref, wo_ref,
             dya_ref, dyr_ref, dgates_ref, doa_ref, dyrp_ref, db_ref):
        @pl.when(pl.program_id(0) == 0)
        def _():
            db_ref[...] = jnp.zeros_like(db_ref)

        dm = _nt(dh1_ref[...].astype(BF), wo_ref[...])
        gt = _sigmoid(gates_ref[...].astype(F32) + bg_ref[...])
        ga, gr = gt[:, :D_MODEL], gt[:, D_MODEL:]
        dya = (dm * ga).astype(BF)
        dyr = (dm * gr).astype(BF)
        dya_ref[...] = dya
        dyr_ref[...] = dyr
        dga = dm * ya_ref[...].astype(F32) * ga * (1.0 - ga)
        dgr = dm * yr_ref[...].astype(F32) * gr * (1.0 - gr)
        dgates_ref[:, :D_MODEL] = dga.astype(BF)
        dgates_ref[:, D_MODEL:] = dgr.astype(BF)
        db_ref[:, :D_MODEL] += jnp.sum(dga, axis=0, keepdims=True)
        db_ref[:, D_MODEL:] += jnp.sum(dgr, axis=0, keepdims=True)
        doa_ref[...] = _nt(wa_ref[...], dya).astype(BF)
        dyrp_ref[...] = _nt(dyr, wr_ref[...]).astype(BF)

    return _call(
        body, (dh1, ya, yr, gates, bg, w_att, w_ret, w_out), comm, name="mix_bwd", grid=(seq // tm,),
        in_specs=[_rows(tm, D_MODEL), _rows(tm, D_MODEL), _rows(tm, D_MODEL), _rows(tm, 2048), VMEM_FULL,
                  VMEM_FULL, VMEM_FULL, VMEM_FULL],
        out_specs=[_rows(tm, D_MODEL), _rows(tm, D_MODEL), _rows(tm, 2048), _cols(512, tm), _rows(tm, 1024),
                   pl.BlockSpec((1, 2048), lambda i: (0, 0))],
        out_shape=[jax.ShapeDtypeStruct((seq, D_MODEL), BF), jax.ShapeDtypeStruct((seq, D_MODEL), BF),
                   jax.ShapeDtypeStruct((seq, 2048), BF), jax.ShapeDtypeStruct((512, seq), BF),
                   jax.ShapeDtypeStruct((seq, 1024), BF), jax.ShapeDtypeStruct((1, 2048), F32)])


def _seq_bwd(qt, kv, kvt, dot, sinks, qkr, vr, gr, ret, dyrp, states, gain, rtabs, tabs, comm=None):
    seq = kv.shape[0]
    c = ATT_BLOCK
    assert c == RET_CHUNK
    d = ATT_HEAD_DIM
    nb = seq // c
    dk, dv = RET_KEY_DIM, RET_VAL_DIM
    ca, sa, sb, cr, sr, ct, st = tabs

    def body(sink_ref, qt_ref, kvc_ref, kvp_ref, kvtc_ref, kvtp_ref, dot_ref, ca_ref, sa_ref, sb_ref, ct_ref, st_ref,
             qk_ref, v_ref, g_ref, ret_ref, dyp_ref, st8_ref, gain_ref, intra_ref, qd_ref, kd_ref, cr_ref, sr_ref,
             dqt_ref, dkv_ref, dsink_ref, dall_ref, dgain_ref, carry, dstate):
        step = pl.program_id(0)
        i = nb - 1 - step

        @pl.when(step == 0)
        def _():
            carry[...] = jnp.zeros_like(carry)
            dsink_ref[...] = jnp.zeros_like(dsink_ref)
            dstate[...] = jnp.zeros_like(dstate)
            dgain_ref[...] = jnp.zeros_like(dgain_ref)

        kwin = jnp.concatenate([kvp_ref[:, :LANE], kvc_ref[:, :LANE]], axis=0)
        vwin = jnp.concatenate([kvp_ref[:, LANE:], kvc_ref[:, LANE:]], axis=0)
        ktwin = jnp.concatenate([kvtp_ref[:LANE, :], kvtc_ref[:LANE, :]], axis=1)
        ctv, stv = ct_ref[...], st_ref[...]
        lane1 = lax.broadcasted_iota(jnp.int32, (1, LANE), 1)
        dkw = jnp.zeros((2 * c, LANE), F32)
        dvw = jnp.zeros((2 * c, LANE), F32)
        dsink = jnp.zeros((1, LANE), F32)
        for g in (0, 1):
            qpad, dopad = _att_group(qt_ref, g), _att_group(dot_ref, g)
            prob, psink = _att_probs(kwin, qpad, sink_ref, g, i)
            dprob = _nn(vwin, dopad)
            drow = jnp.sum(dprob * prob, axis=0, keepdims=True)
            ds = (prob * (dprob - drow) * ATT_SCALE).astype(BF)
            dqg = _nn(ktwin[d * g:d * (g + 1), :], ds)
            dkw = dkw + _nt(ds, qpad)
            dvw = dvw + _nt(prob.astype(BF), dopad)
            dsink_lanes = psink * drow
            for j in range(ATT_Q_PER_KV):
                h = ATT_Q_PER_KV * g + j
                dqt_ref[d * h:d * (h + 1), :] = _rope_att_rows(dqg[:, c * j:c * (j + 1)], ctv, stv, -1.0).astype(BF)
                dsink = dsink + jnp.where(lane1 == h, -jnp.sum(dsink_lanes[:, c * j:c * (j + 1)]), 0.0)
        dsink_ref[...] += dsink
        total = carry[...] + jnp.concatenate([dkw[c:], dvw[c:]], axis=1)
        dkv_ref[:, :LANE] = _rope_att_t(total[:, :LANE], ca_ref[...], sa_ref[...], sb_ref[...]).astype(BF)
        dkv_ref[:, LANE:] = total[:, LANE:].astype(BF)
        carry[...] = jnp.concatenate([dkw[:c], dvw[:c]], axis=1)

        crv, srv = cr_ref[...], sr_ref[...]
        for h in range(RET_HEADS):
            vs = slice(dv * h, dv * (h + 1))
            qh = qk_ref[:, dk * h:dk * (h + 1)]
            kh = qk_ref[:, 512 + dk * h:512 + dk * (h + 1)]
            vh = v_ref[:, vs]
            out = ret_ref[:, vs]
            g = g_ref[:, vs].astype(F32)
            dyp = dyp_ref[:, vs].astype(F32)
            gain_h = gain_ref[:, vs]
            mu = jnp.mean(out, axis=-1, keepdims=True)
            dev = out - mu
            rstd = lax.rsqrt(jnp.mean(dev * dev, axis=-1, keepdims=True) + GN_EPS)
            yn = dev * rstd
            sg = _sigmoid(g)
            dg = dyp * (yn * gain_h) * (sg * (1.0 + g * (1.0 - sg)))
            dy = dyp * (g * sg)
            dgain_ref[:, vs] += jnp.sum(dy * yn, axis=0, keepdims=True)
            dyn = dy * gain_h
            dout = rstd * (dyn - jnp.mean(dyn, axis=-1, keepdims=True)
                           - yn * jnp.mean(dyn * yn, axis=-1, keepdims=True))
            doutb = dout.astype(BF)
            sc = st8_ref[0, h]
            dsp = dstate[h]
            dspb = dsp.astype(BF)
            intra, qdv, kdv = intra_ref[h], qd_ref[h], kd_ref[h]
            att = _nt(qh, kh) * intra
            dab = (_nt(doutb, vh) * intra).astype(BF)
            qdec = (qh.astype(F32) * qdv).astype(BF)
            kdec = (kh.astype(F32) * kdv).astype(BF)
            dq = _nn(dab, kh) + _nt(doutb, sc) * qdv
            dkk = _tn(dab, qh) + _nt(vh, dspb) * kdv
            dvv = _tn(att.astype(BF), doutb) + _nn(kdec, dspb)
            dstate[h] = dsp * RET_CHUNK_DECAY[h] + _tn(qdec, doutb)
            dall_ref[:, dk * h:dk * (h + 1)] = _rope_ret_t(dq, crv, srv).astype(BF)
            dall_ref[:, 512 + dk * h:512 + dk * (h + 1)] = (_rope_ret_t(dkk, crv, srv) * RET_SCALE).astype(BF)
            dall_ref[:, 1024 + dv * h:1024 + dv * (h + 1)] = dvv.astype(BF)
            dall_ref[:, 2048 + dv * h:2048 + dv * (h + 1)] = dg.astype(BF)

    cur = lambda s: nb - 1 - s
    prev = lambda s: jnp.maximum(nb - 2 - s, 0)
    rows = lambda w: pl.BlockSpec((c, w), lambda s: (cur(s), 0))
    cols = lambda h: pl.BlockSpec((h, c), lambda s: (0, cur(s)))
    acc = lambda w: pl.BlockSpec((1, w), lambda s: (0, 0))
    return _call(
        body, (sinks, qt, kv, kv, kvt, kvt, dot, ca, sa, sb, ct, st, qkr, vr, gr, ret, dyrp, states, gain, *rtabs,
               cr, sr), comm, name="seq_bwd", grid=(nb,),
        in_specs=[SMEM_FULL, cols(512), rows(256), pl.BlockSpec((c, 256), lambda s: (prev(s), 0)), cols(256),
                  pl.BlockSpec((256, c), lambda s: (0, prev(s))), cols(512), rows(LANE), rows(LANE), rows(LANE),
                  cols(8), cols(8), rows(1024), rows(1024), rows(1024), rows(1024), rows(1024),
                  pl.BlockSpec((1, RET_HEADS, dk, dv), lambda s: (cur(s), 0, 0, 0)),
                  VMEM_FULL, VMEM_FULL, VMEM_FULL, VMEM_FULL, rows(LANE), rows(LANE)],
        out_specs=[cols(512), rows(256), acc(LANE), rows(3072), acc(1024)],
        out_shape=[jax.ShapeDtypeStruct((512, seq), BF), jax.ShapeDtypeStruct((seq, 256), BF),
                   jax.ShapeDtypeStruct((1, LANE), F32), jax.ShapeDtypeStruct((seq, 3072), BF),
                   jax.ShapeDtypeStruct((1, 1024), F32)],
        scratch_shapes=[pltpu.VMEM((c, 256), F32), pltpu.VMEM((RET_HEADS, dk, dv), F32)])


def _inproj_bwd(dqa, dkva, dret, dgates, dh1, x, g1, w_in, comm=None):
    seq = x.shape[0]
    tm = min(TOKEN_TILE, seq)

    def body(dqa_ref, dkva_ref, dret_ref, dgates_ref, dh1_ref, x_ref, g_ref, w_hbm, dx_ref, dg1_ref, w_vmem, w_sems):
        w = _Resident(w_hbm, w_vmem, w_sems, IN_CHUNKS_BWD)
        _Resident.load(w)

        @pl.when(pl.program_id(0) == 0)
        def _():
            dg1_ref[...] = jnp.zeros_like(dg1_ref)

        dxn = _tn(dqa_ref[...], w.chunk(0)[...])
        dxn = dxn + _nn(dkva_ref[...], w.chunk(1)[...])
        dxn = dxn + _nn(dret_ref[...], w.chunk(2)[...])
        dxn = dxn + _nn(dgates_ref[...], w.chunk(3)[...])
        dnorm, dg = _rms_bwd(dxn, x_ref[...], g_ref[...])
        dx_ref[...] = dh1_ref[...] + dnorm
        dg1_ref[...] += dg

    return _call(
        body, (dqa, dkva, dret, dgates, dh1, x, g1, w_in), comm, name="inproj_bwd", grid=(seq // tm,),
        in_specs=[_cols(512, tm), _rows(tm, 256), _rows(tm, 3072), _rows(tm, 2048), _rows(tm, D_MODEL),
                  _rows(tm, D_MODEL), VMEM_FULL, ANY],
        out_specs=[_rows(tm, D_MODEL), pl.BlockSpec((1, D_MODEL), lambda i: (0, 0))],
        out_shape=[jax.ShapeDtypeStruct((seq, D_MODEL), F32), jax.ShapeDtypeStruct((1, D_MODEL), F32)],
        scratch_shapes=_Resident.scratch(w_in, IN_CHUNKS_BWD))


def _local_step(x, target, g1, bg, sinks, gain, g2, g3, w_in, shards):
    seq = x.shape[0]
    ca, sa, sb, cr, sr, ct, st = _rope_tables(seq)
    rtabs = _ret_tables()
    scatter = _Scatter

    comm_mid, unpack_mid = _gather(shards, ("w_att_up", "w_ret_up", "w_out"), relay=True)
    comm_ff1, unpack_ff1 = _gather(shards, ("w_ff1",), relay=True)
    (xn1, qt, kv, kvt, qkr, vr, gr, gates), got = _inproj_fwd(x, g1, w_in, (ca, sa, sb, cr, sr, ct, st),
                                                              _join(comm_mid, comm_ff1))
    w_att, w_ret, w_out = unpack_mid(got[:3])
    (w_ff1,) = unpack_ff1(got[3:])
    comm, unpack = _gather(shards, ("w_ff2",), relay=True)
    (oa, ret, yrp, states), got = _seq_fwd(qt, kv, kvt, sinks, qkr, vr, gr, gain, rtabs, comm)
    (w_ff2,) = unpack(got)
    ya, yr, merged, h1 = _mix_fwd(oa, yrp, gates, bg, x, w_att, w_ret, w_out)
    xn2, hdn, dh2, du, dh1, loss, dg3, dg2 = _mlp_loss_step(h1, g2, g3, target, w_ff1, w_ff2)

    ff2 = scatter(dict(w_ff2=_mm_tn(hdn, dh2, "dw_ff2")), "ff2")
    d_ff1, r1 = _mm_tn(du, xn2, "dw_ff1", comm=ff2.pair_comm())
    ff2.pair_done(r1)
    ff1 = scatter(dict(w_ff1=d_ff1), "ff1")
    (dya, dyr, dgates, doa, dyrp, db), got = _mix_bwd(dh1, ya, yr, gates, bg, w_att, w_ret, w_out,
                                                      _join(ff1.pair_comm(), ff2.chip_comm()))
    ff1.pair_done(got[:1])
    ff2.chip_done(got[1:])
    mid = scatter(dict(w_att_up=_mm_tn(oa, dya, "dw_att_up", a_is_t=True), w_ret_up=_mm_tn(yrp, dyr, "dw_ret_up"),
                       w_out=_mm_tn(merged, dh1, "dw_out")), "mid")
    (dqa, dkva, dsink, dret, dgain), got = _seq_bwd(
        qt, kv, kvt, doa, sinks, qkr, vr, gr, ret, dyrp, states, gain, rtabs, (ca, sa, sb, cr, sr, ct, st),
        _join(ff1.chip_comm(), mid.pair_comm()))
    ff1.chip_done(got[:1])
    mid.pair_done(got[1:])
    d_in, r2 = _dw_in_t((dqa, dkva, dret, dgates), (True, False, False, False), xn1, mid.chip_comm())
    mid.chip_done(r2)
    win = scatter(dict(w_in=d_in), "in")
    win.pair_done(_comm_alone(win.pair_comm(), "pair_exchange_in"))
    (dx, dg1), r2 = _inproj_bwd(dqa, dkva, dret, dgates, dh1, x, g1, w_in, win.chip_comm())
    win.chip_done(r2)
    small = dict(norm_mix_gain=dg1, b_gates=db, attn_sinks=dsink, ret_gn_gain=dgain, norm_mlp_gain=dg2,
                 norm_final_gain=dg3)
    return loss, dx, small, (win, mid, ff1, ff2)


def _coords():
    return lax.axis_index("x"), lax.axis_index("y"), lax.axis_index("c")


def _flip(v, bit):
    return 1 - v if bit else v


def _xor(a, b):
    return a + b - 2 * a * b


def _gather_relay_comm(shards):
    n = len(shards)

    def parts(ins, outs, sems):
        send_sems, recv_sems, local_sems = sems
        x, y, c = _coords()
        me, sib = (x, y, c), (x, y, 1 - c)

        def ring(core):
            return ((_xor(x, core), _xor(y, 1 - core), core), (_xor(x, 1 - core), _xor(y, core), core),
                    (1 - x, 1 - y, core))

        def slot(a, blk):
            return outs[a].at[4 * blk[0] + 2 * blk[1] + blk[2]]

        def copy(a, k, blk, to, src=None):
            return pltpu.make_async_remote_copy(
                src_ref=slot(a, blk) if src is None else src, dst_ref=slot(a, blk),
                send_sem=send_sems.at[a, k], recv_sem=recv_sems.at[a, k], device_id=to, device_id_type=MESH)

        mine = [pltpu.make_async_copy(ins[a], slot(a, me), local_sems.at[a]) for a in range(n)]
        return me, sib, ring, copy, mine

    def first_copies(ins, outs, sems):
        me, sib, ring, copy, mine = parts(ins, outs, sems)
        src, dst, _ = ring(me[2])
        return mine, [copy(a, k, me, to, src=ins[a]) for a in range(n) for k, to in ((1, src), (2, dst), (0, sib))]

    def start(ins, outs, sems):
        mine, first = first_copies(ins, outs, sems)
        for cp in mine + first:
            cp.start()

    def relay(ins, outs, sems):
        me, sib, ring, copy, _ = parts(ins, outs, sems)
        src, dst, _ = ring(me[2])
        for a in range(n):
            copy(a, 1, src, me).wait_recv()
            copy(a, 3, src, dst).start()
            copy(a, 4, src, sib).start()

    def finish(ins, outs, sems):
        me, sib, ring, copy, _ = parts(ins, outs, sems)
        mine, first = first_copies(ins, outs, sems)
        src, dst, diag = ring(me[2])
        passed = [copy(a, k, src, to) for a in range(n) for k, to in ((3, dst), (4, sib))]
        for k, blk in ((2, dst), (3, diag)):
            for a in range(n):
                copy(a, k, blk, me).wait_recv()
                fwd = copy(a, k + 3, blk, sib)
                fwd.start()
                passed.append(fwd)
        s_src, s_dst, s_diag = ring(sib[2])
        for a in range(n):
            for k, blk in ((0, sib), (4, s_src), (5, s_dst), (6, s_diag)):
                copy(a, k, blk, me).wait_recv()
        for cp in first + passed:
            cp.wait_send()
        for cp in mine:
            cp.wait()

    return _Comm(list(shards), [jax.ShapeDtypeStruct((N_DEV,) + s.shape, s.dtype) for s in shards],
                 [pltpu.SemaphoreType.DMA((n, 7)), pltpu.SemaphoreType.DMA((n, 7)), pltpu.SemaphoreType.DMA((n,))],
                 start, finish, relay)


def _gather_comm(shards):
    n = len(shards)

    def parts(ins, outs, sems):
        send_sems, recv_sems, local_sems = sems
        x, y, c = _coords()
        me, sib = (x, y, c), (x, y, 1 - c)
        chips = [(1 - x, y), (x, 1 - y), (1 - x, 1 - y)]

        def slot(a, blk):
            return outs[a].at[4 * blk[0] + 2 * blk[1] + blk[2]]

        def copy(a, k, blk, to, src=None):
            return pltpu.make_async_remote_copy(
                src_ref=slot(a, blk) if src is None else src, dst_ref=slot(a, blk),
                send_sem=send_sems.at[a, k], recv_sem=recv_sems.at[a, k], device_id=to, device_id_type=MESH)

        mine = [pltpu.make_async_copy(ins[a], slot(a, me), local_sems.at[a]) for a in range(n)]
        first = []
        for a in range(n):
            first.append(copy(a, 0, me, sib, src=ins[a]))
            first += [copy(a, 1 + j, me, (*chip, c), src=ins[a]) for j, chip in enumerate(chips)]
        return me, sib, chips, c, copy, mine, first

    def start(ins, outs, sems):
        *_, mine, first = parts(ins, outs, sems)
        for cp in mine + first:
            cp.start()

    def finish(ins, outs, sems):
        me, sib, chips, c, copy, mine, first = parts(ins, outs, sems)
        passed = []
        for j, chip in enumerate(chips):
            for a in range(n):
                copy(a, 1 + j, (*chip, c), me).wait_recv()
                fwd = copy(a, 4 + j, (*chip, c), sib)
                fwd.start()
                passed.append(fwd)
        for a in range(n):
            copy(a, 0, sib, me).wait_recv()
            for j, chip in enumerate(chips):
                copy(a, 4 + j, (*chip, 1 - c), me).wait_recv()
        for cp in first + passed:
            cp.wait_send()
        for cp in mine:
            cp.wait()

    return _Comm(list(shards), [jax.ShapeDtypeStruct((N_DEV,) + s.shape, s.dtype) for s in shards],
                 [pltpu.SemaphoreType.DMA((n, 7)), pltpu.SemaphoreType.DMA((n, 7)), pltpu.SemaphoreType.DMA((n,))],
                 start, finish)


COLUMN_SHARDED = ("w_in", "w_ff1")
COLUMN_RELAID = ("w_att_up",)


def _gather(shards, names, relay=False):
    def unpack(got):
        return [jnp.transpose(g, (1, 0, 2)).reshape(g.shape[1], N_DEV * g.shape[2]) if k in COLUMN_RELAID
                else g.reshape(N_DEV * g.shape[1], g.shape[2]) for k, g in zip(names, got)]

    return (_gather_relay_comm if relay else _gather_comm)([shards[k] for k in names]), unpack


def _pair_comm(grads):
    n = len(grads)

    def copies(g, r1, sems):
        send_sems, recv_sems = sems
        x, y, c = _coords()
        return [pltpu.make_async_remote_copy(
            src_ref=g[a].at[2 * j + (1 - c)], dst_ref=r1[a].at[j], send_sem=send_sems.at[a, j],
            recv_sem=recv_sems.at[a, j], device_id=(x, y, 1 - c), device_id_type=MESH)
            for a in range(n) for j in range(4)]

    def start(g, r1, sems):
        for cp in copies(g, r1, sems):
            cp.start()

    def finish(g, r1, sems):
        for cp in copies(g, r1, sems):
            cp.wait_recv()
        for cp in copies(g, r1, sems):
            cp.wait_send()

    return _Comm(list(grads), [jax.ShapeDtypeStruct((4,) + g.shape[1:], g.dtype) for g in grads],
                 [pltpu.SemaphoreType.DMA((n, 4)), pltpu.SemaphoreType.DMA((n, 4))], start, finish)


def _small_comm(small):
    def parts(ins, outs, sems):
        (small_ref,), (small_all,) = ins, outs
        ssend, srecv, lsem = sems
        x, y, c = _coords()
        me_idx = 4 * x + 2 * y + c
        own = pltpu.make_async_copy(small_ref, small_all.at[me_idx], lsem)
        sends, recvs = [], []
        for r in range(1, N_DEV):
            px, py, pc = _flip(x, r & 4), _flip(y, r & 2), _flip(c, r & 1)
            sends.append(pltpu.make_async_remote_copy(
                src_ref=small_ref, dst_ref=small_all.at[me_idx], send_sem=ssend.at[r - 1], recv_sem=srecv.at[r - 1],
                device_id=(px, py, pc), device_id_type=MESH))
            recvs.append(pltpu.make_async_remote_copy(
                src_ref=small_ref, dst_ref=small_all.at[4 * px + 2 * py + pc], send_sem=ssend.at[r - 1],
                recv_sem=srecv.at[r - 1], device_id=(px, py, pc), device_id_type=MESH))
        return own, sends, recvs

    def start(ins, outs, sems):
        own, sends, _ = parts(ins, outs, sems)
        own.start()
        for cp in sends:
            cp.start()

    def finish(ins, outs, sems):
        own, sends, recvs = parts(ins, outs, sems)
        for cp in recvs:
            cp.wait_recv()
        for cp in sends:
            cp.wait_send()
        own.wait()

    return _Comm([small], [jax.ShapeDtypeStruct((N_DEV,) + small.shape, small.dtype)],
                 [pltpu.SemaphoreType.DMA((N_DEV - 1,)), pltpu.SemaphoreType.DMA((N_DEV - 1,)),
                  pltpu.SemaphoreType.DMA], start, finish)


def _pair_sum(grads, r1s, c_arr, tag):
    n = len(grads)
    q = 1

    def body(c_ref, *refs):
        g, r, t = refs[:n], refs[n:2 * n], refs[2 * n:]
        for a in range(n):
            t[a][...] = (g[a][...] + r[a][...]).astype(t[a].dtype)

    def blk(arr):
        return (1, arr.shape[1] // q, arr.shape[2])

    grid_spec = pltpu.PrefetchScalarGridSpec(
        num_scalar_prefetch=1, grid=(4, q),
        in_specs=[pl.BlockSpec(blk(g), lambda j, s, c_ref: (2 * j + c_ref[0], s, 0)) for g in grads]
        + [pl.BlockSpec(blk(r), lambda j, s, c_ref: (j, s, 0)) for r in r1s],
        out_specs=[pl.BlockSpec(blk(r), lambda j, s, c_ref: (j, s, 0)) for r in r1s])
    return pl.pallas_call(
        body, name="pair_sum_" + tag, grid_spec=grid_spec,
        out_shape=[jax.ShapeDtypeStruct(r.shape, RS_PAYLOAD) for r in r1s],
        compiler_params=_params(2),
    )(c_arr, *grads, *r1s)


def _chip_comm(ts):
    n = len(ts)

    def copies(t, r2, sems):
        send_sems, recv_sems = sems
        x, y, c = _coords()
        out = []
        for a in range(n):
            for r in range(1, 4):
                tx, ty = _flip(x, r & 2), _flip(y, r & 1)
                out.append(pltpu.make_async_remote_copy(
                    src_ref=t[a].at[2 * tx + ty], dst_ref=r2[a].at[r - 1], send_sem=send_sems.at[a, r - 1],
                    recv_sem=recv_sems.at[a, r - 1], device_id=(tx, ty, c), device_id_type=MESH))
        return out

    def start(t, r2, sems):
        for cp in copies(t, r2, sems):
            cp.start()

    def finish(t, r2, sems):
        for cp in copies(t, r2, sems):
            cp.wait_recv()
        for cp in copies(t, r2, sems):
            cp.wait_send()

    return _Comm(list(ts), [jax.ShapeDtypeStruct((3,) + t.shape[1:], t.dtype) for t in ts],
                 [pltpu.SemaphoreType.DMA((n, 3)), pltpu.SemaphoreType.DMA((n, 3))], start, finish)


class _Scatter:
    def __init__(self, grads, tag):
        self.names, self.tag = tuple(grads), tag
        self.stacks = [jnp.transpose(g.reshape(g.shape[0], N_DEV, g.shape[1] // N_DEV), (1, 0, 2))
                       if k in COLUMN_RELAID else g.reshape(N_DEV, g.shape[0] // N_DEV, g.shape[1])
                       for k, g in grads.items()]

    def pair_comm(self):
        return _pair_comm(self.stacks)

    def pair_done(self, r1s):
        self.r1s = list(r1s)
        c_arr = jnp.reshape(lax.axis_index("c"), (1,)).astype(jnp.int32)
        self.ts = _pair_sum(self.stacks, self.r1s, c_arr, self.tag)

    def chip_comm(self):
        return _chip_comm(self.ts)

    def chip_done(self, r2s):
        self.r2s = list(r2s)


def _adamw(w, g, m, v):
    m = ADAM_B1 * m + (1.0 - ADAM_B1) * g
    v = ADAM_B2 * v + (1.0 - ADAM_B2) * jnp.square(g)
    m_hat = m / (1.0 - ADAM_B1 ** ADAM_STEP)
    v_hat = v / (1.0 - ADAM_B2 ** ADAM_STEP)
    delta = -ADAM_LR * (m_hat / (jnp.sqrt(v_hat) + ADAM_EPS) + ADAM_WD * w)
    return delta, m, v


ADAM_STEPS = 4


def _piece_specs(stacks):
    def rows(s):
        return s.shape[1] // ADAM_STEPS
    return ([pl.BlockSpec((1, rows(s), s.shape[2]), lambda i, idx_ref: (idx_ref[0], i, 0)) for s in stacks]
            + [pl.BlockSpec((1, rows(s), s.shape[2]), lambda i, idx_ref: (idx_ref[1], i, 0)) for s in stacks]
            + [pl.BlockSpec((3, rows(s), s.shape[2]), lambda i, idx_ref: (0, i, 0)) for s in stacks])


def _piece_sum(g0, r1, r2):
    return (((g0[0] + r1[0]) + r2[0].astype(F32)) + r2[1].astype(F32)) + r2[2].astype(F32)


def _shard_sum(stacks, r1s, r2s, idx_arr):
    n = len(stacks)

    def body(idx_ref, *refs):
        g0, r1, r2, outs = (refs[k * n:(k + 1) * n] for k in range(4))
        for a in range(n):
            outs[a][...] = _piece_sum(g0[a], r1[a], r2[a])

    out_specs = [pl.BlockSpec((s.shape[1] // ADAM_STEPS, s.shape[2]), lambda i, idx_ref: (i, 0)) for s in stacks]
    grid_spec = pltpu.PrefetchScalarGridSpec(num_scalar_prefetch=1, grid=(ADAM_STEPS,),
                                             in_specs=_piece_specs(stacks), out_specs=out_specs)
    return pl.pallas_call(
        body, name="shard_sum", grid_spec=grid_spec,
        out_shape=[jax.ShapeDtypeStruct(s.shape[1:], F32) for s in stacks],
        compiler_params=_params(1),
    )(idx_arr, *stacks, *r1s, *r2s)


def _adam_big(pieces, summed, ws, ms, vs, idx_arr):
    stacks, r1s, r2s = pieces
    n_p, n = len(stacks), len(ws)

    def body(idx_ref, *refs):
        it = iter(refs)
        g0, r1, r2, gs, w, m, v = ([next(it) for _ in range(k)] for k in (n_p, n_p, n_p, n - n_p, n, n, n))
        outs = list(it)
        for a in range(n):
            g = _piece_sum(g0[a], r1[a], r2[a]) if a < n_p else gs[a - n_p][...]
            delta, nm, nv = _adamw(w[a][...], g, m[a][...], v[a][...])
            outs[4 * a][...] = g
            outs[4 * a + 1][...] = delta
            outs[4 * a + 2][...] = nm
            outs[4 * a + 3][...] = nv

    def shard_spec(w):
        return pl.BlockSpec((w.shape[0] // ADAM_STEPS, w.shape[1]), lambda i, idx_ref: (i, 0))

    in_specs = _piece_specs(stacks) + [shard_spec(w) for w in ws[n_p:]] + [shard_spec(w) for w in ws] * 3
    out_specs = [shard_spec(w) for w in ws for _ in range(4)]
    grid_spec = pltpu.PrefetchScalarGridSpec(num_scalar_prefetch=1, grid=(ADAM_STEPS,), in_specs=in_specs,
                                             out_specs=out_specs)
    return pl.pallas_call(
        body, name="adam_big", grid_spec=grid_spec,
        out_shape=[jax.ShapeDtypeStruct(w.shape, F32) for w in ws for _ in range(4)],
        compiler_params=_params(1),
    )(idx_arr, *stacks, *r1s, *r2s, *summed, *ws, *ms, *vs)


def _adam_small(small_all, w, m, v):
    def body(all_ref, w_ref, m_ref, v_ref, g_ref, d_ref, nm_ref, nv_ref):
        g = all_ref[0]
        for k in range(1, N_DEV):
            g = g + all_ref[k]
        delta, nm, nv = _adamw(w_ref[...], g, m_ref[...], v_ref[...])
        g_ref[...] = g
        d_ref[...] = delta
        nm_ref[...] = nm
        nv_ref[...] = nv

    return pl.pallas_call(
        body, name="adam_small",
        in_specs=[VMEM_FULL] * 4, out_specs=[VMEM_FULL] * 4,
        out_shape=[jax.ShapeDtypeStruct(w.shape, F32)] * 4,
    )(small_all, w, m, v)


RS_PAYLOAD = BF


def _pack_rows(rows):
    rid = lax.broadcasted_iota(jnp.int32, (8, D_MODEL), 0)
    out = jnp.zeros((8, D_MODEL), F32)
    for i, r in enumerate(rows):
        out = jnp.where(rid == i, jnp.broadcast_to(r, (8, D_MODEL)), out)
    return out


def _small_rows(norm_mix_gain, b_gates, attn_sinks, ret_gn_gain, norm_mlp_gain, norm_final_gain):
    return [norm_mix_gain, b_gates[:, :D_MODEL], b_gates[:, D_MODEL:], ret_gn_gain, norm_mlp_gain,
            norm_final_gain.reshape(1, D_MODEL), jnp.pad(attn_sinks, ((0, 0), (0, D_MODEL - ATT_HEADS)))]


def _unpack_small(p):
    return dict(norm_mix_gain=p[0:1], b_gates=jnp.concatenate([p[1:2], p[2:3]], axis=1), ret_gn_gain=p[3:4],
                norm_mlp_gain=p[4:5], norm_final_gain=p[5], attn_sinks=p[6:7, :ATT_HEADS])


WEIGHTS = ("norm_mix_gain", "w_in", "b_gates", "attn_sinks", "ret_gn_gain", "w_att_up", "w_ret_up", "w_out",
           "norm_mlp_gain", "w_ff1", "w_ff2", "norm_final_gain")
BIG = ("w_in", "w_att_up", "w_ret_up", "w_out", "w_ff1", "w_ff2")


def kernel(x, norm_mix_gain, w_in, b_gates, attn_sinks, ret_gn_gain, w_att_up, w_ret_up, w_out, norm_mlp_gain, w_ff1, w_ff2, norm_final_gain, loss_target, m_norm_mix_gain, m_w_in, m_b_gates, m_attn_sinks, m_ret_gn_gain, m_w_att_up, m_w_ret_up, m_w_out, m_norm_mlp_gain, m_w_ff1, m_w_ff2, m_norm_final_gain, v_norm_mix_gain, v_w_in, v_b_gates, v_attn_sinks, v_ret_gn_gain, v_w_att_up, v_w_ret_up, v_w_out, v_norm_mlp_gain, v_w_ff1, v_w_ff2, v_norm_final_gain):
    w = dict(norm_mix_gain=norm_mix_gain, w_in=w_in, b_gates=b_gates, attn_sinks=attn_sinks, ret_gn_gain=ret_gn_gain,
             w_att_up=w_att_up, w_ret_up=w_ret_up, w_out=w_out, norm_mlp_gain=norm_mlp_gain, w_ff1=w_ff1,
             w_ff2=w_ff2, norm_final_gain=norm_final_gain)
    m = dict(norm_mix_gain=m_norm_mix_gain, w_in=m_w_in, b_gates=m_b_gates, attn_sinks=m_attn_sinks,
             ret_gn_gain=m_ret_gn_gain, w_att_up=m_w_att_up, w_ret_up=m_w_ret_up, w_out=m_w_out,
             norm_mlp_gain=m_norm_mlp_gain, w_ff1=m_w_ff1, w_ff2=m_w_ff2, norm_final_gain=m_norm_final_gain)
    v = dict(norm_mix_gain=v_norm_mix_gain, w_in=v_w_in, b_gates=v_b_gates, attn_sinks=v_attn_sinks,
             ret_gn_gain=v_ret_gn_gain, w_att_up=v_w_att_up, w_ret_up=v_w_ret_up, w_out=v_w_out,
             norm_mlp_gain=v_norm_mlp_gain, w_ff1=v_w_ff1, w_ff2=v_w_ff2, norm_final_gain=v_norm_final_gain)

    shards = {k: (w[k][0].T if k in COLUMN_SHARDED else w[k][0]).astype(BF) for k in BIG}
    comm, unpack = _gather(shards, ("w_in",), relay=True)
    (w_in_full,) = unpack(_comm_alone(comm, "allgather_w_in"))

    loss_p, dx, small, groups = _local_step(
        x[0], loss_target[0], norm_mix_gain, b_gates, attn_sinks[0], ret_gn_gain, norm_mlp_gain,
        norm_final_gain.reshape(1, D_MODEL), w_in_full, shards)

    by_name = {}
    for grp in groups:
        for k, stack, r1, r2 in zip(grp.names, grp.stacks, grp.r1s, grp.r2s):
            by_name[k] = (stack, r1, r2)
    lane = lax.broadcasted_iota(jnp.int32, (1, D_MODEL), 1)
    sink_row = jnp.where(lane < ATT_HEADS, jnp.pad(small["attn_sinks"], ((0, 0), (0, D_MODEL - LANE))),
                         jnp.where(lane == ATT_HEADS, jnp.pad(loss_p, ((0, 0), (0, D_MODEL - LANE)), mode="edge"),
                                   0.0))
    small_pack = _pack_rows([small["norm_mix_gain"], small["b_gates"][:, :D_MODEL], small["b_gates"][:, D_MODEL:],
                             small["ret_gn_gain"], small["norm_mlp_gain"], small["norm_final_gain"], sink_row])
    xi, yi, ci = _coords()
    (small_all,) = _comm_alone(_small_comm(small_pack), "small_exchange")
    idx_arr = jnp.stack([4 * xi + 2 * yi + ci, 2 * xi + yi]).astype(jnp.int32)
    in_pieces = ["w_att_up", "w_ret_up", "w_out", "w_ff2", "w_in"]
    in_sum = ["w_ff1"]
    order = in_pieces + in_sum
    summed_t = _shard_sum(*[[by_name[k][i] for k in in_sum] for i in range(3)], idx_arr)

    def shard(tree, k):
        return tree[k][0].T if k == "w_in" else tree[k][0]

    adam_out = _adam_big([[by_name[k][i] for k in in_pieces] for i in range(3)], [g.T for g in summed_t],
                         [shard(w, k) for k in order], [shard(m, k) for k in order], [shard(v, k) for k in order],
                         idx_arr)
    big_out = [adam_out[4 * order.index(k) + i].T if k == "w_in" else adam_out[4 * order.index(k) + i]
               for k in BIG for i in range(4)]
    sm_g, sm_d, sm_m, sm_v = _adam_small(small_all, _pack_rows(_small_rows(*[w[k] for k in WEIGHTS if k not in BIG])),
                                         _pack_rows(_small_rows(*[m[k] for k in WEIGHTS if k not in BIG])),
                                         _pack_rows(_small_rows(*[v[k] for k in WEIGHTS if k not in BIG])))

    loss = sm_g[6, ATT_HEADS]
    kinds = []
    for idx, packed in enumerate((sm_g, sm_d, sm_m, sm_v)):
        out = _unpack_small(packed)
        for a, k in enumerate(BIG):
            out[k] = big_out[4 * a + idx][None]
        kinds.append(out)
    return (loss, dx[None], *[kinds[0][k] for k in WEIGHTS], *[kinds[1][k] for k in WEIGHTS],
            *[kinds[2][k] for k in WEIGHTS], *[kinds[3][k] for k in WEIGHTS])
```

```python
import functools
import math

import jax
import jax.numpy as jnp
from jax import lax
from jax.experimental import pallas as pl
from jax.experimental.pallas import tpu as pltpu

F32 = jnp.float32
BF = jnp.bfloat16
MESH = pl.DeviceIdType.MESH

D_MODEL = 1024
ATT_HEADS = 8
ATT_HEAD_DIM = 64
ATT_BLOCK = 128
ROPE_DIM = 16
ROPE_THETA = 500000.0
RET_HEADS = 4
RET_KEY_DIM = 128
RET_VAL_DIM = 256
RET_CHUNK = 128
RET_ROT_BASE = 10000.0
D_FF = 4096
NORM_EPS = 1e-6
GN_EPS = 1e-6
NEG_INF = -1e30
ATT_SCALE = ATT_HEAD_DIM ** -0.5
RET_SCALE = RET_KEY_DIM ** -0.5

C_QA, C_KA, C_VA, C_QR, C_KR, C_VR, C_GR, C_GATES, C_END = 0, 512, 640, 768, 1280, 1792, 2816, 3840, 5888

ADAM_LR = 0.001
ADAM_B1 = 0.9
ADAM_B2 = 0.999
ADAM_EPS = 1e-08
ADAM_WD = 0.01
ADAM_STEP = 10

N_DEV = 8
LANE = 128
VMEM_LIMIT = 56 * 1024 * 1024
TOKEN_TILE = 512
MLP_TOKEN_TILE = 256
TN_TOKEN_TILE = 2048
FF_CHUNKS = (0, 1024, 2048, 3072, 4096)
IN_CHUNKS_FWD = (C_QA, C_KA, C_QR, C_KR, C_VR, C_GR, C_GATES, C_END)
IN_CHUNKS_BWD = (C_QA, C_KA, C_QR, C_GATES, C_END)

RET_LOG_GAMMA = tuple(math.log1p(-(2.0 ** (-5.0 - h))) for h in range(RET_HEADS))
RET_CHUNK_DECAY = tuple(math.exp(RET_CHUNK * lg) for lg in RET_LOG_GAMMA)

VMEM_FULL = pl.BlockSpec(memory_space=pltpu.VMEM)
SMEM_FULL = pl.BlockSpec(memory_space=pltpu.SMEM)
ANY = pl.BlockSpec(memory_space=pl.ANY)


def _params(n_axes):
    return pltpu.CompilerParams(dimension_semantics=("arbitrary",) * n_axes, vmem_limit_bytes=VMEM_LIMIT)


def _nn(a, b):
    return jnp.dot(a, b, preferred_element_type=F32)


def _nt(a, b):
    return lax.dot_general(a, b, (((1,), (1,)), ((), ())), preferred_element_type=F32)


def _tn(a, b):
    return lax.dot_general(a, b, (((0,), (0,)), ((), ())), preferred_element_type=F32)


def _sigmoid(v):
    return 1.0 / (1.0 + jnp.exp(-v))


def _rows(tile, width):
    return pl.BlockSpec((tile, width), lambda i: (i, 0))


def _cols(height, tile):
    return pl.BlockSpec((height, tile), lambda i: (0, i))


class _Resident:
    def __init__(self, hbm_ref, vmem_ref, sems, bounds):
        self.hbm, self.vmem, self.sems, self.bounds = hbm_ref, vmem_ref, sems, bounds

    @staticmethod
    def scratch(w, bounds):
        return [pltpu.VMEM(w.shape, w.dtype), pltpu.SemaphoreType.DMA((len(bounds) - 1,))]

    @staticmethod
    def load(*weights):
        @pl.when(pl.program_id(0) == 0)
        def _():
            copies = [w._copy(c) for w in weights for c in range(len(w.bounds) - 1)]
            for cp in copies:
                cp.start()
            for cp in copies:
                cp.wait()

    def _rows(self, c):
        return pl.ds(self.bounds[c], self.bounds[c + 1] - self.bounds[c])

    def _copy(self, c):
        return pltpu.make_async_copy(self.hbm.at[self._rows(c)], self.vmem.at[self._rows(c)], self.sems.at[c])

    def chunk(self, c):
        return self.vmem.at[self._rows(c)]


class _Comm:
    def __init__(self, inputs, out_shapes, scratch, start, finish, relay=None):
        self.inputs, self.out_shapes, self.scratch = inputs, out_shapes, scratch
        self.start, self.finish, self.relay = start, finish, relay


RELAY_AT = 0.5


def _join(a, b):
    na_in, na_out, na_sem = len(a.inputs), len(a.out_shapes), len(a.scratch)

    def both(name):
        def run(ins, outs, sems):
            for part, args in ((a, (ins[:na_in], outs[:na_out], sems[:na_sem])),
                               (b, (ins[na_in:], outs[na_out:], sems[na_sem:]))):
                if getattr(part, name) is not None:
                    getattr(part, name)(*args)
        return run

    return _Comm(list(a.inputs) + list(b.inputs), list(a.out_shapes) + list(b.out_shapes),
                 list(a.scratch) + list(b.scratch), both("start"), both("finish"),
                 both("relay") if a.relay or b.relay else None)


def _call(body, args, comm=None, *, name, grid, in_specs, out_specs, out_shape, scratch_shapes=()):
    params = _params(len(grid))
    if comm is None:
        return pl.pallas_call(body, name=name, grid=grid, in_specs=in_specs, out_specs=out_specs, out_shape=out_shape,
                              scratch_shapes=scratch_shapes, compiler_params=params)(*args), ()
    single = not isinstance(out_specs, (list, tuple))
    out_specs_l = [out_specs] if single else list(out_specs)
    out_shape_l = [out_shape] if single else list(out_shape)
    n_in, n_out, n_scr = len(in_specs), len(out_specs_l), len(scratch_shapes)
    n_cin, n_cout = len(comm.inputs), len(comm.out_shapes)

    def hosted(*refs):
        it = iter(refs)
        ins, cin, outs, cout, scr = ([next(it) for _ in range(k)] for k in (n_in, n_cin, n_out, n_cout, n_scr))
        sems = list(it)
        ids = [pl.program_id(k) for k in range(len(grid))]
        first = functools.reduce(jnp.logical_and, [i == 0 for i in ids])
        last = functools.reduce(jnp.logical_and, [i == g - 1 for i, g in zip(ids, grid)])

        @pl.when(first)
        def _():
            comm.start(cin, cout, sems)

        if comm.relay is not None:
            at = [int(grid[0] * RELAY_AT)] + [0] * (len(grid) - 1)

            @pl.when(functools.reduce(jnp.logical_and, [i == v for i, v in zip(ids, at)]))
            def _():
                comm.relay(cin, cout, sems)

        body(*ins, *outs, *scr)

        @pl.when(last)
        def _():
            comm.finish(cin, cout, sems)

    res = pl.pallas_call(
        hosted, name=name, grid=grid, in_specs=list(in_specs) + [ANY] * n_cin,
        out_specs=out_specs_l + [ANY] * n_cout, out_shape=out_shape_l + list(comm.out_shapes),
        scratch_shapes=list(scratch_shapes) + list(comm.scratch), compiler_params=params)(*args, *comm.inputs)
    return (res[0] if single else res[:n_out]), res[n_out:]


def _comm_alone(comm, name):
    n_cin, n_cout = len(comm.inputs), len(comm.out_shapes)

    def body(*refs):
        cin, cout, sems = refs[:n_cin], refs[n_cin:n_cin + n_cout], refs[n_cin + n_cout:]
        comm.start(cin, cout, sems)
        if comm.relay is not None:
            comm.relay(cin, cout, sems)
        comm.finish(cin, cout, sems)

    return pl.pallas_call(body, name=name, in_specs=[ANY] * n_cin, out_specs=[ANY] * n_cout,
                          out_shape=list(comm.out_shapes), scratch_shapes=list(comm.scratch))(*comm.inputs)


def _slabs(v, fn):
    return jnp.concatenate([fn(v[:, LANE * j:LANE * (j + 1)]) for j in range(v.shape[1] // LANE)], axis=1)


def _rope_att(v, ca, sa, sb):
    return _slabs(v, lambda t: t * ca + pltpu.roll(t, LANE - 8, 1) * sa + pltpu.roll(t, 8, 1) * sb)


def _rope_att_t(v, ca, sa, sb):
    return _slabs(v, lambda t: t * ca + pltpu.roll(t * sa, 8, 1) + pltpu.roll(t * sb, LANE - 8, 1))


def _rope_att_rows(v, ct, st, sign):
    parts = []
    for h in range(v.shape[0] // ATT_HEAD_DIM):
        r0 = ATT_HEAD_DIM * h
        x1, x2 = v[r0:r0 + 8], v[r0 + 8:r0 + 16]
        parts += [x1 * ct - sign * (x2 * st), x2 * ct + sign * (x1 * st), v[r0 + 16:r0 + ATT_HEAD_DIM]]
    return jnp.concatenate(parts, axis=0)


def _rope_ret(v, cr, sr):
    return _slabs(v, lambda t: t * cr + pltpu.roll(t, 64, 1) * sr)


def _rope_ret_t(v, cr, sr):
    return _slabs(v, lambda t: t * cr + pltpu.roll(t * sr, 64, 1))


def _rope_tables(seq):
    pos = jnp.arange(seq, dtype=F32)

    def cs(dim, theta):
        inv = theta ** (-jnp.arange(0, dim, 2, dtype=F32) / dim)
        ang = pos[:, None] * inv[None, :]
        return jnp.cos(ang), jnp.sin(ang)

    ca, sa_ = cs(ROPE_DIM, ROPE_THETA)
    zeros8 = jnp.zeros_like(ca)
    rest = ATT_HEAD_DIM - ROPE_DIM
    c64 = jnp.concatenate([ca, ca, jnp.ones((seq, rest), F32)], axis=1)
    sa64 = jnp.concatenate([-sa_, zeros8, jnp.zeros((seq, rest), F32)], axis=1)
    sb64 = jnp.concatenate([zeros8, sa_, jnp.zeros((seq, rest), F32)], axis=1)
    cr, sr_ = cs(RET_KEY_DIM, RET_ROT_BASE)
    return (jnp.tile(c64, (1, 2)), jnp.tile(sa64, (1, 2)), jnp.tile(sb64, (1, 2)),
            jnp.concatenate([cr, cr], axis=1), jnp.concatenate([-sr_, sr_], axis=1), ca.T, sa_.T)


def _ret_tables():
    c = RET_CHUNK
    lg = jnp.asarray(RET_LOG_GAMMA, F32)
    idx = jnp.arange(c, dtype=F32)
    diff = idx[:, None] - idx[None, :]
    intra = jnp.where(diff >= 0, jnp.exp(jnp.maximum(diff, 0.0) * lg[:, None, None]), 0.0)
    qd = jnp.exp((idx + 1.0)[None, :] * lg[:, None])[..., None]
    kd = jnp.exp((c - 1.0 - idx)[None, :] * lg[:, None])[..., None]
    return intra, jnp.broadcast_to(qd, (RET_HEADS, c, RET_KEY_DIM)), jnp.broadcast_to(kd, (RET_HEADS, c, RET_KEY_DIM))


def _inproj_fwd(x, g1, w_in, tabs, comm=None):
    seq = x.shape[0]
    tm = min(TOKEN_TILE, seq)

    def body(x_ref, g_ref, w_hbm, ca_ref, sa_ref, sb_ref, cr_ref, sr_ref, ct_ref, st_ref,
             xn_ref, qt_ref, kv_ref, kvt_ref, qkr_ref, vr_ref, gr_ref, gates_ref, w_vmem, w_sems):
        w = _Resident(w_hbm, w_vmem, w_sems, IN_CHUNKS_FWD)
        _Resident.load(w)
        xf = x_ref[...]
        r = lax.rsqrt(jnp.mean(xf * xf, axis=-1, keepdims=True) + NORM_EPS)
        xn = (xf * r * g_ref[...]).astype(BF)
        xn_ref[...] = xn
        ct, st = ct_ref[...], st_ref[...]
        qt_ref[...] = _rope_att_rows(_nt(w.chunk(0)[...], xn), ct, st, 1.0).astype(BF)
        w_kv = w.chunk(1)
        kvt = _nt(w_kv[...], xn)
        kvt_ref[:LANE, :] = _rope_att_rows(kvt[:LANE], ct, st, 1.0).astype(BF)
        kvt_ref[LANE:, :] = kvt[LANE:].astype(BF)
        kvn = _nt(xn, w_kv[...])
        kv_ref[:, :LANE] = _rope_att(kvn[:, :LANE], ca_ref[...], sa_ref[...], sb_ref[...]).astype(BF)
        kv_ref[:, LANE:] = kvn[:, LANE:].astype(BF)
        cr, sr = cr_ref[...], sr_ref[...]
        qr = _rope_ret(_nt(xn, w.chunk(2)[...]), cr, sr)
        qkr_ref[:, :512] = qr.astype(BF)
        kr = _rope_ret(_nt(xn, w.chunk(3)[...]), cr, sr) * RET_SCALE
        qkr_ref[:, 512:] = kr.astype(BF)
        vr_ref[...] = _nt(xn, w.chunk(4)[...]).astype(BF)
        gr_ref[...] = _nt(xn, w.chunk(5)[...]).astype(BF)
        gates_ref[...] = _nt(xn, w.chunk(6)[...]).astype(BF)

    tab = _rows(tm, LANE)
    tab_t = _cols(8, tm)
    return _call(
        body, (x, g1, w_in, *tabs), comm, name="inproj_fwd", grid=(seq // tm,),
        in_specs=[_rows(tm, D_MODEL), VMEM_FULL, ANY, tab, tab, tab, tab, tab, tab_t, tab_t],
        out_specs=[_rows(tm, D_MODEL), _cols(512, tm), _rows(tm, 256), _cols(256, tm), _rows(tm, 1024),
                   _rows(tm, 1024), _rows(tm, 1024), _rows(tm, 2048)],
        out_shape=[jax.ShapeDtypeStruct(s, BF) for s in ((seq, D_MODEL), (512, seq), (seq, 256), (256, seq),
                                                         (seq, 1024), (seq, 1024), (seq, 1024), (seq, 2048))],
        scratch_shapes=_Resident.scratch(w_in, IN_CHUNKS_FWD))


ATT_Q_PER_KV = ATT_HEADS // 2


def _att_group(ref, kv):
    d = ATT_HEAD_DIM
    g = jnp.concatenate([ref[d * (ATT_Q_PER_KV * kv + j):d * (ATT_Q_PER_KV * kv + j + 1), :]
                         for j in range(ATT_Q_PER_KV)], axis=1)
    z = jnp.zeros_like(g)
    return jnp.concatenate([g, z] if kv == 0 else [z, g], axis=0)


def _att_probs(kwin, qpad, sink_ref, kv, i):
    c = ATT_BLOCK
    kj = lax.broadcasted_iota(jnp.int32, (2 * c, c), 0)
    qi = lax.broadcasted_iota(jnp.int32, (2 * c, c), 1)
    allowed = (kj > qi) & (kj <= qi + c) & ((kj >= c) | (i > 0))
    allowed = jnp.concatenate([allowed] * ATT_Q_PER_KV, axis=1)
    s = jnp.where(allowed, _nn(kwin, qpad) * ATT_SCALE, NEG_INF)
    sk = jnp.concatenate([jnp.full((1, c), sink_ref[ATT_Q_PER_KV * kv + j], F32) for j in range(ATT_Q_PER_KV)], axis=1)
    m = jnp.maximum(jnp.max(s, axis=0, keepdims=True), sk)
    pe = jnp.exp(s - m)
    psink = jnp.exp(sk - m)
    inv = 1.0 / (jnp.sum(pe, axis=0, keepdims=True) + psink)
    return pe * inv, psink * inv


def _seq_fwd(qt, kv, kvt, sinks, qkr, vr, gr, gain, rtabs, comm=None):
    seq = kv.shape[0]
    c = ATT_BLOCK
    assert c == RET_CHUNK
    d = ATT_HEAD_DIM
    nb = seq // c
    dk, dv = RET_KEY_DIM, RET_VAL_DIM

    def body(sink_ref, qt_ref, kvc_ref, kvp_ref, kvtc_ref, kvtp_ref, qk_ref, v_ref, g_ref, gain_ref, intra_ref,
             qd_ref, kd_ref, ot_ref, ret_ref, yrp_ref, st_ref, state):
        i = pl.program_id(0)
        kwin = jnp.concatenate([kvp_ref[:, :LANE], kvc_ref[:, :LANE]], axis=0)
        vtwin = jnp.concatenate([kvtp_ref[LANE:, :], kvtc_ref[LANE:, :]], axis=1)
        for g in (0, 1):
            prob, _ = _att_probs(kwin, _att_group(qt_ref, g), sink_ref, g, i)
            out = _nn(vtwin[d * g:d * (g + 1), :], prob.astype(BF))
            for j in range(ATT_Q_PER_KV):
                h = ATT_Q_PER_KV * g + j
                ot_ref[d * h:d * (h + 1), :] = out[:, c * j:c * (j + 1)].astype(BF)

        @pl.when(i == 0)
        def _():
            state[...] = jnp.zeros_like(state)

        for h in range(RET_HEADS):
            qh = qk_ref[:, dk * h:dk * (h + 1)]
            kh = qk_ref[:, 512 + dk * h:512 + dk * (h + 1)]
            vh = v_ref[:, dv * h:dv * (h + 1)]
            sh = state[h]
            shb = sh.astype(BF)
            st_ref[0, h] = shb
            att = _nt(qh, kh) * intra_ref[h]
            inner = _nn(att.astype(BF), vh)
            cross = _nn((qh.astype(F32) * qd_ref[h]).astype(BF), shb)
            out = inner + cross
            state[h] = sh * RET_CHUNK_DECAY[h] + _tn((kh.astype(F32) * kd_ref[h]).astype(BF), vh)
            ret_ref[:, dv * h:dv * (h + 1)] = out
            mu = jnp.mean(out, axis=-1, keepdims=True)
            dev = out - mu
            var = jnp.mean(dev * dev, axis=-1, keepdims=True)
            y = dev * lax.rsqrt(var + GN_EPS) * gain_ref[:, dv * h:dv * (h + 1)]
            g = g_ref[:, dv * h:dv * (h + 1)].astype(F32)
            yrp_ref[:, dv * h:dv * (h + 1)] = (g * _sigmoid(g) * y).astype(BF)

    prev = lambda i: jnp.maximum(i - 1, 0)
    return _call(
        body, (sinks, qt, kv, kv, kvt, kvt, qkr, vr, gr, gain, *rtabs), comm, name="seq_fwd", grid=(nb,),
        in_specs=[SMEM_FULL, _cols(512, c), _rows(c, 256), pl.BlockSpec((c, 256), lambda i: (prev(i), 0)),
                  _cols(256, c), pl.BlockSpec((256, c), lambda i: (0, prev(i))),
                  _rows(c, 1024), _rows(c, 1024), _rows(c, 1024), VMEM_FULL, VMEM_FULL, VMEM_FULL, VMEM_FULL],
        out_specs=[_cols(512, c), _rows(c, 1024), _rows(c, 1024),
                   pl.BlockSpec((1, RET_HEADS, dk, dv), lambda i: (i, 0, 0, 0))],
        out_shape=[jax.ShapeDtypeStruct((512, seq), BF), jax.ShapeDtypeStruct((seq, 1024), F32),
                   jax.ShapeDtypeStruct((seq, 1024), BF), jax.ShapeDtypeStruct((nb, RET_HEADS, dk, dv), BF)],
        scratch_shapes=[pltpu.VMEM((RET_HEADS, dk, dv), F32)])


def _mix_fwd(oat, yrp, gates, bg, x, w_att, w_ret, w_out):
    seq = x.shape[0]
    tm = min(TOKEN_TILE, seq)

    def body(oat_ref, yrp_ref, gates_ref, bg_ref, x_ref, wa_ref, wr_ref, wo_ref, ya_ref, yr_ref, mg_ref, h1_ref):
        ya = _tn(oat_ref[...], wa_ref[...])
        yr = _nn(yrp_ref[...], wr_ref[...])
        gt = _sigmoid(gates_ref[...].astype(F32) + bg_ref[...])
        merged = (gt[:, :D_MODEL] * ya + gt[:, D_MODEL:] * yr).astype(BF)
        ya_ref[...] = ya.astype(BF)
        yr_ref[...] = yr.astype(BF)
        mg_ref[...] = merged
        h1_ref[...] = x_ref[...] + _nn(merged, wo_ref[...])

    return pl.pallas_call(
        body, name="mix_fwd", grid=(seq // tm,),
        in_specs=[_cols(512, tm), _rows(tm, 1024), _rows(tm, 2048), VMEM_FULL, _rows(tm, D_MODEL),
                  VMEM_FULL, VMEM_FULL, VMEM_FULL],
        out_specs=[_rows(tm, D_MODEL)] * 4,
        out_shape=[jax.ShapeDtypeStruct((seq, D_MODEL), BF)] * 3 + [jax.ShapeDtypeStruct((seq, D_MODEL), F32)],
        compiler_params=_params(1),
    )(oat, yrp, gates, bg, x, w_att, w_ret, w_out)


def _mlp_loss_step(h1, g2, g3, target, w_ff1, w_ff2):
    seq = h1.shape[0]
    tm = min(MLP_TOKEN_TILE, seq)
    n_chunks = len(FF_CHUNKS) - 1

    def body(h1_ref, g2_ref, g3_ref, t_ref, w1_hbm, w2_hbm,
             xn2_ref, hdn_ref, dh2_ref, du_ref, dh1_ref, loss_ref, dg3_ref, dg2_ref,
             w1_vmem, w1_sems, w2_vmem, w2_sems, relu_u):
        w1 = _Resident(w1_hbm, w1_vmem, w1_sems, FF_CHUNKS)
        w2 = _Resident(w2_hbm, w2_vmem, w2_sems, FF_CHUNKS)
        _Resident.load(w1, w2)

        @pl.when(pl.program_id(0) == 0)
        def _():
            loss_ref[...] = jnp.zeros_like(loss_ref)
            dg3_ref[...] = jnp.zeros_like(dg3_ref)
            dg2_ref[...] = jnp.zeros_like(dg2_ref)

        h1v = h1_ref[...]
        r2 = lax.rsqrt(jnp.mean(h1v * h1v, axis=-1, keepdims=True) + NORM_EPS)
        xn2 = (h1v * r2 * g2_ref[...]).astype(BF)
        xn2_ref[...] = xn2
        h2 = h1v
        for c in range(n_chunks):
            cols = slice(FF_CHUNKS[c], FF_CHUNKS[c + 1])
            a = jnp.maximum(_nt(xn2, w1.chunk(c)[...]), 0.0)
            relu_u[:, cols] = a.astype(BF)
            hdn = jnp.square(a).astype(BF)
            hdn_ref[:, cols] = hdn
            h2 = h2 + _nn(hdn, w2.chunk(c)[...])
        r3 = lax.rsqrt(jnp.mean(h2 * h2, axis=-1, keepdims=True) + NORM_EPS)
        hn = h2 * r3
        err = hn * g3_ref[...] - t_ref[...]
        loss_ref[...] += jnp.sum(err * err) * (0.5 / D_MODEL)
        dy = err * (1.0 / D_MODEL)
        dg3_ref[...] += jnp.sum(dy * hn, axis=0, keepdims=True)
        z = dy * g3_ref[...]
        dh2 = r3 * (z - hn * jnp.mean(z * hn, axis=-1, keepdims=True))
        dh2b = dh2.astype(BF)
        dh2_ref[...] = dh2b
        dxn2 = jnp.zeros_like(dh2)
        for c in range(n_chunks):
            cols = slice(FF_CHUNKS[c], FF_CHUNKS[c + 1])
            du = (_nt(dh2b, w2.chunk(c)[...]) * (2.0 * relu_u[:, cols].astype(F32))).astype(BF)
            du_ref[:, cols] = du
            dxn2 = dxn2 + _nn(du, w1.chunk(c)[...])
        dnorm, dg = _rms_bwd(dxn2, h1v, g2_ref[...])
        dh1_ref[...] = dh2 + dnorm
        dg2_ref[...] += dg

    gain_acc = pl.BlockSpec((1, D_MODEL), lambda i: (0, 0))
    return pl.pallas_call(
        body, name="mlp_loss_step", grid=(seq // tm,),
        in_specs=[_rows(tm, D_MODEL), VMEM_FULL, VMEM_FULL, _rows(tm, D_MODEL), ANY, ANY],
        out_specs=[_rows(tm, D_MODEL), _rows(tm, D_FF), _rows(tm, D_MODEL), _rows(tm, D_FF), _rows(tm, D_MODEL),
                   pl.BlockSpec((1, LANE), lambda i: (0, 0)), gain_acc, gain_acc],
        out_shape=[jax.ShapeDtypeStruct((seq, D_MODEL), BF), jax.ShapeDtypeStruct((seq, D_FF), BF),
                   jax.ShapeDtypeStruct((seq, D_MODEL), BF), jax.ShapeDtypeStruct((seq, D_FF), BF),
                   jax.ShapeDtypeStruct((seq, D_MODEL), F32), jax.ShapeDtypeStruct((1, LANE), F32),
                   jax.ShapeDtypeStruct((1, D_MODEL), F32), jax.ShapeDtypeStruct((1, D_MODEL), F32)],
        scratch_shapes=(_Resident.scratch(w_ff1, FF_CHUNKS) + _Resident.scratch(w_ff2, FF_CHUNKS)
                        + [pltpu.VMEM((tm, D_FF), BF)]),
        compiler_params=_params(1),
    )(h1, g2, g3, target, w_ff1, w_ff2)


def _rms_bwd(dxn, xin, gain):
    r = lax.rsqrt(jnp.mean(xin * xin, axis=-1, keepdims=True) + NORM_EPS)
    xhat = xin * r
    z = dxn * gain
    dxin = r * (z - xhat * jnp.mean(z * xhat, axis=-1, keepdims=True))
    return dxin, jnp.sum(dxn * xhat, axis=0, keepdims=True)


def _mm_tn(a, b, name, a_is_t=False, comm=None):
    kdim, seq = a.shape if a_is_t else a.shape[::-1]
    ndim = b.shape[1]
    ts = min(TN_TOKEN_TILE, seq)
    tk = min(kdim, 1024)
    tn = min(ndim, 1024)
    n_steps = seq // ts

    def body(a_ref, b_ref, o_ref):
        @pl.when(pl.program_id(2) == 0)
        def _():
            o_ref[...] = jnp.zeros_like(o_ref)

        mm = _nn if a_is_t else _tn
        o_ref[...] += mm(a_ref[...].astype(BF), b_ref[...].astype(BF))

    a_spec = (pl.BlockSpec((tk, ts), lambda k, n, s: (k, s)) if a_is_t
              else pl.BlockSpec((ts, tk), lambda k, n, s: (s, k)))
    res = _call(
        body, (a, b), comm, name=name, grid=(kdim // tk, ndim // tn, n_steps),
        in_specs=[a_spec, pl.BlockSpec((ts, tn), lambda k, n, s: (s, n))],
        out_specs=pl.BlockSpec((tk, tn), lambda k, n, s: (k, n)),
        out_shape=jax.ShapeDtypeStruct((kdim, ndim), F32))
    return res[0] if comm is None else res


def _dw_in_t(pieces, is_t, xn1, comm=None):
    seq = xn1.shape[0]
    ts = seq
    tr = 256
    n_blk = [(p.shape[0] if t else p.shape[1]) // tr for p, t in zip(pieces, is_t)]
    offs = [sum(n_blk[:i]) for i in range(len(pieces) + 1)]
    n_p = len(pieces)

    def body(*refs):
        piece_refs, xn_ref, o_ref = refs[:n_p], refs[n_p], refs[n_p + 1]
        r, s = pl.program_id(0), pl.program_id(1)

        @pl.when(s == 0)
        def _():
            o_ref[...] = jnp.zeros_like(o_ref)

        for p in range(n_p):
            @pl.when((r >= offs[p]) & (r < offs[p + 1]))
            def _(p=p):
                mm = _nn if is_t[p] else _tn
                o_ref[...] += mm(piece_refs[p][...], xn_ref[pl.ds(pl.multiple_of(s * ts, ts), ts), :])

    def piece_spec(p):
        def index(r, s):
            inside = (r >= offs[p]) & (r < offs[p + 1])
            tok, blk = jnp.where(inside, s, 0), jnp.clip(r - offs[p], 0, n_blk[p] - 1)
            return (blk, tok) if is_t[p] else (tok, blk)
        return pl.BlockSpec((tr, ts) if is_t[p] else (ts, tr), index)

    return _call(
        body, (*pieces, xn1), comm, name="dw_in", grid=(offs[-1], seq // ts),
        in_specs=[piece_spec(p) for p in range(n_p)] + [VMEM_FULL],
        out_specs=pl.BlockSpec((tr, D_MODEL), lambda r, s: (r, 0)),
        out_shape=jax.ShapeDtypeStruct((offs[-1] * tr, D_MODEL), F32))


def _mix_bwd(dh1, ya, yr, gates, bg, w_att, w_ret, w_out, comm=None):
    seq = dh1.shape[0]
    tm = min(TOKEN_TILE, seq)

    def body(dh1_ref, ya_ref, yr_ref, gates_ref, bg_ref, wa_ref, wr_ref, wo_ref,
             dya_ref, dyr_ref, dgates_ref, doa_ref, dyrp_ref, db_ref):
        @pl.when(pl.program_id(0) == 0)
        def _():
            db_ref[...] = jnp.zeros_like(db_ref)

        dm = _nt(dh1_ref[...].astype(BF), wo_ref[...])
        gt = _sigmoid(gates_ref[...].astype(F32) + bg_ref[...])
        ga, gr = gt[:, :D_MODEL], gt[:, D_MODEL:]
        dya = (dm * ga).astype(BF)
        dyr = (dm * gr).astype(BF)
        dya_ref[...] = dya
        dyr_ref[...] = dyr
        dga = dm * ya_ref[...].astype(F32) * ga * (1.0 - ga)
        dgr = dm * yr_ref[...].astype(F32) * gr * (1.0 - gr)
        dgates_ref[:, :D_MODEL] = dga.astype(BF)
        dgates_ref[:, D_MODEL:] = dgr.astype(BF)
        db_ref[:, :D_MODEL] += jnp.sum(dga, axis=0, keepdims=True)
        db_ref[:, D_MODEL:] += jnp.sum(dgr, axis=0, keepdims=True)
        doa_ref[...] = _nt(wa_ref[...], dya).astype(BF)
        dyrp_ref[...] = _nt(dyr, wr_ref[...]).astype(BF)

    return _call(
        body, (dh1, ya, yr, gates, bg, w_att, w_ret, w_out), comm, name="mix_bwd", grid=(seq // tm,),
        in_specs=[_rows(tm, D_MODEL), _rows(tm, D_MODEL), _rows(tm, D_MODEL), _rows(tm, 2048), VMEM_FULL,
                  VMEM_FULL, VMEM_FULL, VMEM_FULL],
        out_specs=[_rows(tm, D_MODEL), _rows(tm, D_MODEL), _rows(tm, 2048), _cols(512, tm), _rows(tm, 1024),
                   pl.BlockSpec((1, 2048), lambda i: (0, 0))],
        out_shape=[jax.ShapeDtypeStruct((seq, D_MODEL), BF), jax.ShapeDtypeStruct((seq, D_MODEL), BF),
                   jax.ShapeDtypeStruct((seq, 2048), BF), jax.ShapeDtypeStruct((512, seq), BF),
                   jax.ShapeDtypeStruct((seq, 1024), BF), jax.ShapeDtypeStruct((1, 2048), F32)])


def _seq_bwd(qt, kv, kvt, dot, sinks, qkr, vr, gr, ret, dyrp, states, gain, rtabs, tabs, comm=None):
    seq = kv.shape[0]
    c = ATT_BLOCK
    assert c == RET_CHUNK
    d = ATT_HEAD_DIM
    nb = seq // c
    dk, dv = RET_KEY_DIM, RET_VAL_DIM
    ca, sa, sb, cr, sr, ct, st = tabs

    def body(sink_ref, qt_ref, kvc_ref, kvp_ref, kvtc_ref, kvtp_ref, dot_ref, ca_ref, sa_ref, sb_ref, ct_ref, st_ref,
             qk_ref, v_ref, g_ref, ret_ref, dyp_ref, st8_ref, gain_ref, intra_ref, qd_ref, kd_ref, cr_ref, sr_ref,
             dqt_ref, dkv_ref, dsink_ref, dall_ref, dgain_ref, carry, dstate):
        step = pl.program_id(0)
        i = nb - 1 - step

        @pl.when(step == 0)
        def _():
            carry[...] = jnp.zeros_like(carry)
            dsink_ref[...] = jnp.zeros_like(dsink_ref)
            dstate[...] = jnp.zeros_like(dstate)
            dgain_ref[...] = jnp.zeros_like(dgain_ref)

        kwin = jnp.concatenate([kvp_ref[:, :LANE], kvc_ref[:, :LANE]], axis=0)
        vwin = jnp.concatenate([kvp_ref[:, LANE:], kvc_ref[:, LANE:]], axis=0)
        ktwin = jnp.concatenate([kvtp_ref[:LANE, :], kvtc_ref[:LANE, :]], axis=1)
        ctv, stv = ct_ref[...], st_ref[...]
        lane1 = lax.broadcasted_iota(jnp.int32, (1, LANE), 1)
        dkw = jnp.zeros((2 * c, LANE), F32)
        dvw = jnp.zeros((2 * c, LANE), F32)
        dsink = jnp.zeros((1, LANE), F32)
        for g in (0, 1):
            qpad, dopad = _att_group(qt_ref, g), _att_group(dot_ref, g)
            prob, psink = _att_probs(kwin, qpad, sink_ref, g, i)
            dprob = _nn(vwin, dopad)
            drow = jnp.sum(dprob * prob, axis=0, keepdims=True)
            ds = (prob * (dprob - drow) * ATT_SCALE).astype(BF)
            dqg = _nn(ktwin[d * g:d * (g + 1), :], ds)
            dkw = dkw + _nt(ds, qpad)
            dvw = dvw + _nt(prob.astype(BF), dopad)
            dsink_lanes = psink * drow
            for j in range(ATT_Q_PER_KV):
                h = ATT_Q_PER_KV * g + j
                dqt_ref[d * h:d * (h + 1), :] = _rope_att_rows(dqg[:, c * j:c * (j + 1)], ctv, stv, -1.0).astype(BF)
                dsink = dsink + jnp.where(lane1 == h, -jnp.sum(dsink_lanes[:, c * j:c * (j + 1)]), 0.0)
        dsink_ref[...] += dsink
        total = carry[...] + jnp.concatenate([dkw[c:], dvw[c:]], axis=1)
        dkv_ref[:, :LANE] = _rope_att_t(total[:, :LANE], ca_ref[...], sa_ref[...], sb_ref[...]).astype(BF)
        dkv_ref[:, LANE:] = total[:, LANE:].astype(BF)
        carry[...] = jnp.concatenate([dkw[:c], dvw[:c]], axis=1)

        crv, srv = cr_ref[...], sr_ref[...]
        for h in range(RET_HEADS):
            vs = slice(dv * h, dv * (h + 1))
            qh = qk_ref[:, dk * h:dk * (h + 1)]
            kh = qk_ref[:, 512 + dk * h:512 + dk * (h + 1)]
            vh = v_ref[:, vs]
            out = ret_ref[:, vs]
            g = g_ref[:, vs].astype(F32)
            dyp = dyp_ref[:, vs].astype(F32)
            gain_h = gain_ref[:, vs]
            mu = jnp.mean(out, axis=-1, keepdims=True)
            dev = out - mu
            rstd = lax.rsqrt(jnp.mean(dev * dev, axis=-1, keepdims=True) + GN_EPS)
            yn = dev * rstd
            sg = _sigmoid(g)
            dg = dyp * (yn * gain_h) * (sg * (1.0 + g * (1.0 - sg)))
            dy = dyp * (g * sg)
            dgain_ref[:, vs] += jnp.sum(dy * yn, axis=0, keepdims=True)
            dyn = dy * gain_h
            dout = rstd * (dyn - jnp.mean(dyn, axis=-1, keepdims=True)
                           - yn * jnp.mean(dyn * yn, axis=-1, keepdims=True))
            doutb = dout.astype(BF)
            sc = st8_ref[0, h]
            dsp = dstate[h]
            dspb = dsp.astype(BF)
            intra, qdv, kdv = intra_ref[h], qd_ref[h], kd_ref[h]
            att = _nt(qh, kh) * intra
            dab = (_nt(doutb, vh) * intra).astype(BF)
            qdec = (qh.astype(F32) * qdv).astype(BF)
            kdec = (kh.astype(F32) * kdv).astype(BF)
            dq = _nn(dab, kh) + _nt(doutb, sc) * qdv
            dkk = _tn(dab, qh) + _nt(vh, dspb) * kdv
            dvv = _tn(att.astype(BF), doutb) + _nn(kdec, dspb)
            dstate[h] = dsp * RET_CHUNK_DECAY[h] + _tn(qdec, doutb)
            dall_ref[:, dk * h:dk * (h + 1)] = _rope_ret_t(dq, crv, srv).astype(BF)
            dall_ref[:, 512 + dk * h:512 + dk * (h + 1)] = (_rope_ret_t(dkk, crv, srv) * RET_SCALE).astype(BF)
            dall_ref[:, 1024 + dv * h:1024 + dv * (h + 1)] = dvv.astype(BF)
            dall_ref[:, 2048 + dv * h:2048 + dv * (h + 1)] = dg.astype(BF)

    cur = lambda s: nb - 1 - s
    prev = lambda s: jnp.maximum(nb - 2 - s, 0)
    rows = lambda w: pl.BlockSpec((c, w), lambda s: (cur(s), 0))
    cols = lambda h: pl.BlockSpec((h, c), lambda s: (0, cur(s)))
    acc = lambda w: pl.BlockSpec((1, w), lambda s: (0, 0))
    return _call(
        body, (sinks, qt, kv, kv, kvt, kvt, dot, ca, sa, sb, ct, st, qkr, vr, gr, ret, dyrp, states, gain, *rtabs,
               cr, sr), comm, name="seq_bwd", grid=(nb,),
        in_specs=[SMEM_FULL, cols(512), rows(256), pl.BlockSpec((c, 256), lambda s: (prev(s), 0)), cols(256),
                  pl.BlockSpec((256, c), lambda s: (0, prev(s))), cols(512), rows(LANE), rows(LANE), rows(LANE),
                  cols(8), cols(8), rows(1024), rows(1024), rows(1024), rows(1024), rows(1024),
                  pl.BlockSpec((1, RET_HEADS, dk, dv), lambda s: (cur(s), 0, 0, 0)),
                  VMEM_FULL, VMEM_FULL, VMEM_FULL, VMEM_FULL, rows(LANE), rows(LANE)],
        out_specs=[cols(512), rows(256), acc(LANE), rows(3072), acc(1024)],
        out_shape=[jax.ShapeDtypeStruct((512, seq), BF), jax.ShapeDtypeStruct((seq, 256), BF),
                   jax.ShapeDtypeStruct((1, LANE), F32), jax.ShapeDtypeStruct((seq, 3072), BF),
                   jax.ShapeDtypeStruct((1, 1024), F32)],
        scratch_shapes=[pltpu.VMEM((c, 256), F32), pltpu.VMEM((RET_HEADS, dk, dv), F32)])


def _inproj_bwd(dqa, dkva, dret, dgates, dh1, x, g1, w_in, comm=None):
    seq = x.shape[0]
    tm = min(TOKEN_TILE, seq)

    def body(dqa_ref, dkva_ref, dret_ref, dgates_ref, dh1_ref, x_ref, g_ref, w_hbm, dx_ref, dg1_ref, w_vmem, w_sems):
        w = _Resident(w_hbm, w_vmem, w_sems, IN_CHUNKS_BWD)
        _Resident.load(w)

        @pl.when(pl.program_id(0) == 0)
        def _():
            dg1_ref[...] = jnp.zeros_like(dg1_ref)

        dxn = _tn(dqa_ref[...], w.chunk(0)[...])
        dxn = dxn + _nn(dkva_ref[...], w.chunk(1)[...])
        dxn = dxn + _nn(dret_ref[...], w.chunk(2)[...])
        dxn = dxn + _nn(dgates_ref[...], w.chunk(3)[...])
        dnorm, dg = _rms_bwd(dxn, x_ref[...], g_ref[...])
        dx_ref[...] = dh1_ref[...] + dnorm
        dg1_ref[...] += dg

    return _call(
        body, (dqa, dkva, dret, dgates, dh1, x, g1, w_in), comm, name="inproj_bwd", grid=(seq // tm,),
        in_specs=[_cols(512, tm), _rows(tm, 256), _rows(tm, 3072), _rows(tm, 2048), _rows(tm, D_MODEL),
                  _rows(tm, D_MODEL), VMEM_FULL, ANY],
        out_specs=[_rows(tm, D_MODEL), pl.BlockSpec((1, D_MODEL), lambda i: (0, 0))],
        out_shape=[jax.ShapeDtypeStruct((seq, D_MODEL), F32), jax.ShapeDtypeStruct((1, D_MODEL), F32)],
        scratch_shapes=_Resident.scratch(w_in, IN_CHUNKS_BWD))


def _local_step(x, target, g1, bg, sinks, gain, g2, g3, w_in, shards):
    seq = x.shape[0]
    ca, sa, sb, cr, sr, ct, st = _rope_tables(seq)
    rtabs = _ret_tables()
    scatter = _Scatter

    comm_mid, unpack_mid = _gather(shards, ("w_att_up", "w_ret_up", "w_out"), relay=True)
    comm_ff1, unpack_ff1 = _gather(shards, ("w_ff1",), relay=True)
    (xn1, qt, kv, kvt, qkr, vr, gr, gates), got = _inproj_fwd(x, g1, w_in, (ca, sa, sb, cr, sr, ct, st),
                                                              _join(comm_mid, comm_ff1))
    w_att, w_ret, w_out = unpack_mid(got[:3])
    (w_ff1,) = unpack_ff1(got[3:])
    comm, unpack = _gather(shards, ("w_ff2",), relay=True)
    (oa, ret, yrp, states), got = _seq_fwd(qt, kv, kvt, sinks, qkr, vr, gr, gain, rtabs, comm)
    (w_ff2,) = unpack(got)
    ya, yr, merged, h1 = _mix_fwd(oa, yrp, gates, bg, x, w_att, w_ret, w_out)
    xn2, hdn, dh2, du, dh1, loss, dg3, dg2 = _mlp_loss_step(h1, g2, g3, target, w_ff1, w_ff2)

    ff2 = scatter(dict(w_ff2=_mm_tn(hdn, dh2, "dw_ff2")), "ff2")
    d_ff1, r1 = _mm_tn(du, xn2, "dw_ff1", comm=ff2.pair_comm())
    ff2.pair_done(r1)
    ff1 = scatter(dict(w_ff1=d_ff1), "ff1")
    (dya, dyr, dgates, doa, dyrp, db), got = _mix_bwd(dh1, ya, yr, gates, bg, w_att, w_ret, w_out,
                                                      _join(ff1.pair_comm(), ff2.chip_comm()))
    ff1.pair_done(got[:1])
    ff2.chip_done(got[1:])
    mid = scatter(dict(w_att_up=_mm_tn(oa, dya, "dw_att_up", a_is_t=True), w_ret_up=_mm_tn(yrp, dyr, "dw_ret_up"),
                       w_out=_mm_tn(merged, dh1, "dw_out")), "mid")
    (dqa, dkva, dsink, dret, dgain), got = _seq_bwd(
        qt, kv, kvt, doa, sinks, qkr, vr, gr, ret, dyrp, states, gain, rtabs, (ca, sa, sb, cr, sr, ct, st),
        _join(ff1.chip_comm(), mid.pair_comm()))
    ff1.chip_done(got[:1])
    mid.pair_done(got[1:])
    d_in, r2 = _dw_in_t((dqa, dkva, dret, dgates), (True, False, False, False), xn1, mid.chip_comm())
    mid.chip_done(r2)
    win = scatter(dict(w_in=d_in), "in")
    win.pair_done(_comm_alone(win.pair_comm(), "pair_exchange_in"))
    (dx, dg1), r2 = _inproj_bwd(dqa, dkva, dret, dgates, dh1, x, g1, w_in, win.chip_comm())
    win.chip_done(r2)
    small = dict(norm_mix_gain=dg1, b_gates=db, attn_sinks=dsink, ret_gn_gain=dgain, norm_mlp_gain=dg2,
                 norm_final_gain=dg3)
    return loss, dx, small, (win, mid, ff1, ff2)


def _coords():
    return lax.axis_index("x"), lax.axis_index("y"), lax.axis_index("c")


def _flip(v, bit):
    return 1 - v if bit else v


def _xor(a, b):
    return a + b - 2 * a * b


def _gather_relay_comm(shards):
    n = len(shards)

    def parts(ins, outs, sems):
        send_sems, recv_sems, local_sems = sems
        x, y, c = _coords()
        me, sib = (x, y, c), (x, y, 1 - c)

        def ring(core):
            return ((_xor(x, core), _xor(y, 1 - core), core), (_xor(x, 1 - core), _xor(y, core), core),
                    (1 - x, 1 - y, core))

        def slot(a, blk):
            return outs[a].at[4 * blk[0] + 2 * blk[1] + blk[2]]

        def copy(a, k, blk, to, src=None):
            return pltpu.make_async_remote_copy(
                src_ref=slot(a, blk) if src is None else src, dst_ref=slot(a, blk),
                send_sem=send_sems.at[a, k], recv_sem=recv_sems.at[a, k], device_id=to, device_id_type=MESH)

        mine = [pltpu.make_async_copy(ins[a], slot(a, me), local_sems.at[a]) for a in range(n)]
        return me, sib, ring, copy, mine

    def first_copies(ins, outs, sems):
        me, sib, ring, copy, mine = parts(ins, outs, sems)
        src, dst, _ = ring(me[2])
        return mine, [copy(a, k, me, to, src=ins[a]) for a in range(n) for k, to in ((1, src), (2, dst), (0, sib))]

    def start(ins, outs, sems):
        mine, first = first_copies(ins, outs, sems)
        for cp in mine + first:
            cp.start()

    def relay(ins, outs, sems):
        me, sib, ring, copy, _ = parts(ins, outs, sems)
        src, dst, _ = ring(me[2])
        for a in range(n):
            copy(a, 1, src, me).wait_recv()
            copy(a, 3, src, dst).start()
            copy(a, 4, src, sib).start()

    def finish(ins, outs, sems):
        me, sib, ring, copy, _ = parts(ins, outs, sems)
        mine, first = first_copies(ins, outs, sems)
        src, dst, diag = ring(me[2])
        passed = [copy(a, k, src, to) for a in range(n) for k, to in ((3, dst), (4, sib))]
        for k, blk in ((2, dst), (3, diag)):
            for a in range(n):
                copy(a, k, blk, me).wait_recv()
                fwd = copy(a, k + 3, blk, sib)
                fwd.start()
                passed.append(fwd)
        s_src, s_dst, s_diag = ring(sib[2])
        for a in range(n):
            for k, blk in ((0, sib), (4, s_src), (5, s_dst), (6, s_diag)):
                copy(a, k, blk, me).wait_recv()
        for cp in first + passed:
            cp.wait_send()
        for cp in mine:
            cp.wait()

    return _Comm(list(shards), [jax.ShapeDtypeStruct((N_DEV,) + s.shape, s.dtype) for s in shards],
                 [pltpu.SemaphoreType.DMA((n, 7)), pltpu.SemaphoreType.DMA((n, 7)), pltpu.SemaphoreType.DMA((n,))],
                 start, finish, relay)


def _gather_comm(shards):
    n = len(shards)

    def parts(ins, outs, sems):
        send_sems, recv_sems, local_sems = sems
        x, y, c = _coords()
        me, sib = (x, y, c), (x, y, 1 - c)
        chips = [(1 - x, y), (x, 1 - y), (1 - x, 1 - y)]

        def slot(a, blk):
            return outs[a].at[4 * blk[0] + 2 * blk[1] + blk[2]]

        def copy(a, k, blk, to, src=None):
            return pltpu.make_async_remote_copy(
                src_ref=slot(a, blk) if src is None else src, dst_ref=slot(a, blk),
                send_sem=send_sems.at[a, k], recv_sem=recv_sems.at[a, k], device_id=to, device_id_type=MESH)

        mine = [pltpu.make_async_copy(ins[a], slot(a, me), local_sems.at[a]) for a in range(n)]
        first = []
        for a in range(n):
            first.append(copy(a, 0, me, sib, src=ins[a]))
            first += [copy(a, 1 + j, me, (*chip, c), src=ins[a]) for j, chip in enumerate(chips)]
        return me, sib, chips, c, copy, mine, first

    def start(ins, outs, sems):
        *_, mine, first = parts(ins, outs, sems)
        for cp in mine + first:
            cp.start()

    def finish(ins, outs, sems):
        me, sib, chips, c, copy, mine, first = parts(ins, outs, sems)
        passed = []
        for j, chip in enumerate(chips):
            for a in range(n):
                copy(a, 1 + j, (*chip, c), me).wait_recv()
                fwd = copy(a, 4 + j, (*chip, c), sib)
                fwd.start()
                passed.append(fwd)
        for a in range(n):
            copy(a, 0, sib, me).wait_recv()
            for j, chip in enumerate(chips):
                copy(a, 4 + j, (*chip, 1 - c), me).wait_recv()
        for cp in first + passed:
            cp.wait_send()
        for cp in mine:
            cp.wait()

    return _Comm(list(shards), [jax.ShapeDtypeStruct((N_DEV,) + s.shape, s.dtype) for s in shards],
                 [pltpu.SemaphoreType.DMA((n, 7)), pltpu.SemaphoreType.DMA((n, 7)), pltpu.SemaphoreType.DMA((n,))],
                 start, finish)


COLUMN_SHARDED = ("w_in", "w_ff1")
COLUMN_RELAID = ("w_att_up",)


def _gather(shards, names, relay=False):
    def unpack(got):
        return [jnp.transpose(g, (1, 0, 2)).reshape(g.shape[1], N_DEV * g.shape[2]) if k in COLUMN_RELAID
                else g.reshape(N_DEV * g.shape[1], g.shape[2]) for k, g in zip(names, got)]

    return (_gather_relay_comm if relay else _gather_comm)([shards[k] for k in names]), unpack


def _pair_comm(grads):
    n = len(grads)

    def copies(g, r1, sems):
        send_sems, recv_sems = sems
        x, y, c = _coords()
        return [pltpu.make_async_remote_copy(
            src_ref=g[a].at[2 * j + (1 - c)], dst_ref=r1[a].at[j], send_sem=send_sems.at[a, j],
            recv_sem=recv_sems.at[a, j], device_id=(x, y, 1 - c), device_id_type=MESH)
            for a in range(n) for j in range(4)]

    def start(g, r1, sems):
        for cp in copies(g, r1, sems):
            cp.start()

    def finish(g, r1, sems):
        for cp in copies(g, r1, sems):
            cp.wait_recv()
        for cp in copies(g, r1, sems):
            cp.wait_send()

    return _Comm(list(grads), [jax.ShapeDtypeStruct((4,) + g.shape[1:], g.dtype) for g in grads],
                 [pltpu.SemaphoreType.DMA((n, 4)), pltpu.SemaphoreType.DMA((n, 4))], start, finish)


def _small_comm(small):
    def parts(ins, outs, sems):
        (small_ref,), (small_all,) = ins, outs
        ssend, srecv, lsem = sems
        x, y, c = _coords()
        me_idx = 4 * x + 2 * y + c
        own = pltpu.make_async_copy(small_ref, small_all.at[me_idx], lsem)
        sends, recvs = [], []
        for r in range(1, N_DEV):
            px, py, pc = _flip(x, r & 4), _flip(y, r & 2), _flip(c, r & 1)
            sends.append(pltpu.make_async_remote_copy(
                src_ref=small_ref, dst_ref=small_all.at[me_idx], send_sem=ssend.at[r - 1], recv_sem=srecv.at[r - 1],
                device_id=(px, py, pc), device_id_type=MESH))
            recvs.append(pltpu.make_async_remote_copy(
                src_ref=small_ref, dst_ref=small_all.at[4 * px + 2 * py + pc], send_sem=ssend.at[r - 1],
                recv_sem=srecv.at[r - 1], device_id=(px, py, pc), device_id_type=MESH))
        return own, sends, recvs

    def start(ins, outs, sems):
        own, sends, _ = parts(ins, outs, sems)
        own.start()
        for cp in sends:
            cp.start()

    def finish(ins, outs, sems):
        own, sends, recvs = parts(ins, outs, sems)
        for cp in recvs:
            cp.wait_recv()
        for cp in sends:
            cp.wait_send()
        own.wait()

    return _Comm([small], [jax.ShapeDtypeStruct((N_DEV,) + small.shape, small.dtype)],
                 [pltpu.SemaphoreType.DMA((N_DEV - 1,)), pltpu.SemaphoreType.DMA((N_DEV - 1,)),
                  pltpu.SemaphoreType.DMA], start, finish)


def _pair_sum(grads, r1s, c_arr, tag):
    n = len(grads)
    q = 1

    def body(c_ref, *refs):
        g, r, t = refs[:n], refs[n:2 * n], refs[2 * n:]
        for a in range(n):
            t[a][...] = (g[a][...] + r[a][...]).astype(t[a].dtype)

    def blk(arr):
        return (1, arr.shape[1] // q, arr.shape[2])

    grid_spec = pltpu.PrefetchScalarGridSpec(
        num_scalar_prefetch=1, grid=(4, q),
        in_specs=[pl.BlockSpec(blk(g), lambda j, s, c_ref: (2 * j + c_ref[0], s, 0)) for g in grads]
        + [pl.BlockSpec(blk(r), lambda j, s, c_ref: (j, s, 0)) for r in r1s],
        out_specs=[pl.BlockSpec(blk(r), lambda j, s, c_ref: (j, s, 0)) for r in r1s])
    return pl.pallas_call(
        body, name="pair_sum_" + tag, grid_spec=grid_spec,
        out_shape=[jax.ShapeDtypeStruct(r.shape, RS_PAYLOAD) for r in r1s],
        compiler_params=_params(2),
    )(c_arr, *grads, *r1s)


def _chip_comm(ts):
    n = len(ts)

    def copies(t, r2, sems):
        send_sems, recv_sems = sems
        x, y, c = _coords()
        out = []
        for a in range(n):
            for r in range(1, 4):
                tx, ty = _flip(x, r & 2), _flip(y, r & 1)
                out.append(pltpu.make_async_remote_copy(
                    src_ref=t[a].at[2 * tx + ty], dst_ref=r2[a].at[r - 1], send_sem=send_sems.at[a, r - 1],
                    recv_sem=recv_sems.at[a, r - 1], device_id=(tx, ty, c), device_id_type=MESH))
        return out

    def start(t, r2, sems):
        for cp in copies(t, r2, sems):
            cp.start()

    def finish(t, r2, sems):
        for cp in copies(t, r2, sems):
            cp.wait_recv()
        for cp in copies(t, r2, sems):
            cp.wait_send()

    return _Comm(list(ts), [jax.ShapeDtypeStruct((3,) + t.shape[1:], t.dtype) for t in ts],
                 [pltpu.SemaphoreType.DMA((n, 3)), pltpu.SemaphoreType.DMA((n, 3))], start, finish)


class _Scatter:
    def __init__(self, grads, tag):
        self.names, self.tag = tuple(grads), tag
        self.stacks = [jnp.transpose(g.reshape(g.shape[0], N_DEV, g.shape[1] // N_DEV), (1, 0, 2))
                       if k in COLUMN_RELAID else g.reshape(N_DEV, g.shape[0] // N_DEV, g.shape[1])
                       for k, g in grads.items()]

    def pair_comm(self):
        return _pair_comm(self.stacks)

    def pair_done(self, r1s):
        self.r1s = list(r1s)
        c_arr = jnp.reshape(lax.axis_index("c"), (1,)).astype(jnp.int32)
        self.ts = _pair_sum(self.stacks, self.r1s, c_arr, self.tag)

    def chip_comm(self):
        return _chip_comm(self.ts)

    def chip_done(self, r2s):
        self.r2s = list(r2s)


def _adamw(w, g, m, v):
    m = ADAM_B1 * m + (1.0 - ADAM_B1) * g
    v = ADAM_B2 * v + (1.0 - ADAM_B2) * jnp.square(g)
    m_hat = m / (1.0 - ADAM_B1 ** ADAM_STEP)
    v_hat = v / (1.0 - ADAM_B2 ** ADAM_STEP)
    delta = -ADAM_LR * (m_hat / (jnp.sqrt(v_hat) + ADAM_EPS) + ADAM_WD * w)
    return delta, m, v


ADAM_STEPS = 4


def _piece_specs(stacks):
    def rows(s):
        return s.shape[1] // ADAM_STEPS
    return ([pl.BlockSpec((1, rows(s), s.shape[2]), lambda i, idx_ref: (idx_ref[0], i, 0)) for s in stacks]
            + [pl.BlockSpec((1, rows(s), s.shape[2]), lambda i, idx_ref: (idx_ref[1], i, 0)) for s in stacks]
            + [pl.BlockSpec((3, rows(s), s.shape[2]), lambda i, idx_ref: (0, i, 0)) for s in stacks])


def _piece_sum(g0, r1, r2):
    return (((g0[0] + r1[0]) + r2[0].astype(F32)) + r2[1].astype(F32)) + r2[2].astype(F32)


def _shard_sum(stacks, r1s, r2s, idx_arr):
    n = len(stacks)

    def body(idx_ref, *refs):
        g0, r1, r2, outs = (refs[k * n:(k + 1) * n] for k in range(4))
        for a in range(n):
            outs[a][...] = _piece_sum(g0[a], r1[a], r2[a])

    out_specs = [pl.BlockSpec((s.shape[1] // ADAM_STEPS, s.shape[2]), lambda i, idx_ref: (i, 0)) for s in stacks]
    grid_spec = pltpu.PrefetchScalarGridSpec(num_scalar_prefetch=1, grid=(ADAM_STEPS,),
                                             in_specs=_piece_specs(stacks), out_specs=out_specs)
    return pl.pallas_call(
        body, name="shard_sum", grid_spec=grid_spec,
        out_shape=[jax.ShapeDtypeStruct(s.shape[1:], F32) for s in stacks],
        compiler_params=_params(1),
    )(idx_arr, *stacks, *r1s, *r2s)


def _adam_big(pieces, summed, ws, ms, vs, idx_arr):
    stacks, r1s, r2s = pieces
    n_p, n = len(stacks), len(ws)

    def body(idx_ref, *refs):
        it = iter(refs)
        g0, r1, r2, gs, w, m, v = ([next(it) for _ in range(k)] for k in (n_p, n_p, n_p, n - n_p, n, n, n))
        outs = list(it)
        for a in range(n):
            g = _piece_sum(g0[a], r1[a], r2[a]) if a < n_p else gs[a - n_p][...]
            delta, nm, nv = _adamw(w[a][...], g, m[a][...], v[a][...])
            outs[4 * a][...] = g
            outs[4 * a + 1][...] = delta
            outs[4 * a + 2][...] = nm
            outs[4 * a + 3][...] = nv

    def shard_spec(w):
        return pl.BlockSpec((w.shape[0] // ADAM_STEPS, w.shape[1]), lambda i, idx_ref: (i, 0))

    in_specs = _piece_specs(stacks) + [shard_spec(w) for w in ws[n_p:]] + [shard_spec(w) for w in ws] * 3
    out_specs = [shard_spec(w) for w in ws for _ in range(4)]
    grid_spec = pltpu.PrefetchScalarGridSpec(num_scalar_prefetch=1, grid=(ADAM_STEPS,), in_specs=in_specs,
                                             out_specs=out_specs)
    return pl.pallas_call(
        body, name="adam_big", grid_spec=grid_spec,
        out_shape=[jax.ShapeDtypeStruct(w.shape, F32) for w in ws for _ in range(4)],
        compiler_params=_params(1),
    )(idx_arr, *stacks, *r1s, *r2s, *summed, *ws, *ms, *vs)


def _adam_small(small_all, w, m, v):
    def body(all_ref, w_ref, m_ref, v_ref, g_ref, d_ref, nm_ref, nv_ref):
        g = all_ref[0]
        for k in range(1, N_DEV):
            g = g + all_ref[k]
        delta, nm, nv = _adamw(w_ref[...], g, m_ref[...], v_ref[...])
        g_ref[...] = g
        d_ref[...] = delta
        nm_ref[...] = nm
        nv_ref[...] = nv

    return pl.pallas_call(
        body, name="adam_small",
        in_specs=[VMEM_FULL] * 4, out_specs=[VMEM_FULL] * 4,
        out_shape=[jax.ShapeDtypeStruct(w.shape, F32)] * 4,
    )(small_all, w, m, v)


RS_PAYLOAD = BF


def _pack_rows(rows):
    rid = lax.broadcasted_iota(jnp.int32, (8, D_MODEL), 0)
    out = jnp.zeros((8, D_MODEL), F32)
    for i, r in enumerate(rows):
        out = jnp.where(rid == i, jnp.broadcast_to(r, (8, D_MODEL)), out)
    return out


def _small_rows(norm_mix_gain, b_gates, attn_sinks, ret_gn_gain, norm_mlp_gain, norm_final_gain):
    return [norm_mix_gain, b_gates[:, :D_MODEL], b_gates[:, D_MODEL:], ret_gn_gain, norm_mlp_gain,
            norm_final_gain.reshape(1, D_MODEL), jnp.pad(attn_sinks, ((0, 0), (0, D_MODEL - ATT_HEADS)))]


def _unpack_small(p):
    return dict(norm_mix_gain=p[0:1], b_gates=jnp.concatenate([p[1:2], p[2:3]], axis=1), ret_gn_gain=p[3:4],
                norm_mlp_gain=p[4:5], norm_final_gain=p[5], attn_sinks=p[6:7, :ATT_HEADS])


WEIGHTS = ("norm_mix_gain", "w_in", "b_gates", "attn_sinks", "ret_gn_gain", "w_att_up", "w_ret_up", "w_out",
           "norm_mlp_gain", "w_ff1", "w_ff2", "norm_final_gain")
BIG = ("w_in", "w_att_up", "w_ret_up", "w_out", "w_ff1", "w_ff2")


def kernel(x, norm_mix_gain, w_in, b_gates, attn_sinks, ret_gn_gain, w_att_up, w_ret_up, w_out, norm_mlp_gain, w_ff1, w_ff2, norm_final_gain, loss_target, m_norm_mix_gain, m_w_in, m_b_gates, m_attn_sinks, m_ret_gn_gain, m_w_att_up, m_w_ret_up, m_w_out, m_norm_mlp_gain, m_w_ff1, m_w_ff2, m_norm_final_gain, v_norm_mix_gain, v_w_in, v_b_gates, v_attn_sinks, v_ret_gn_gain, v_w_att_up, v_w_ret_up, v_w_out, v_norm_mlp_gain, v_w_ff1, v_w_ff2, v_norm_final_gain):
    w = dict(norm_mix_gain=norm_mix_gain, w_in=w_in, b_gates=b_gates, attn_sinks=attn_sinks, ret_gn_gain=ret_gn_gain,
             w_att_up=w_att_up, w_ret_up=w_ret_up, w_out=w_out, norm_mlp_gain=norm_mlp_gain, w_ff1=w_ff1,
             w_ff2=w_ff2, norm_final_gain=norm_final_gain)
    m = dict(norm_mix_gain=m_norm_mix_gain, w_in=m_w_in, b_gates=m_b_gates, attn_sinks=m_attn_sinks,
             ret_gn_gain=m_ret_gn_gain, w_att_up=m_w_att_up, w_ret_up=m_w_ret_up, w_out=m_w_out,
             norm_mlp_gain=m_norm_mlp_gain, w_ff1=m_w_ff1, w_ff2=m_w_ff2, norm_final_gain=m_norm_final_gain)
    v = dict(norm_mix_gain=v_norm_mix_gain, w_in=v_w_in, b_gates=v_b_gates, attn_sinks=v_attn_sinks,
             ret_gn_gain=v_ret_gn_gain, w_att_up=v_w_att_up, w_ret_up=v_w_ret_up, w_out=v_w_out,
             norm_mlp_gain=v_norm_mlp_gain, w_ff1=v_w_ff1, w_ff2=v_w_ff2, norm_final_gain=v_norm_final_gain)

    shards = {k: (w[k][0].T if k in COLUMN_SHARDED else w[k][0]).astype(BF) for k in BIG}
    comm, unpack = _gather(shards, ("w_in",), relay=True)
    (w_in_full,) = unpack(_comm_alone(comm, "allgather_w_in"))

    loss_p, dx, small, groups = _local_step(
        x[0], loss_target[0], norm_mix_gain, b_gates, attn_sinks[0], ret_gn_gain, norm_mlp_gain,
        norm_final_gain.reshape(1, D_MODEL), w_in_full, shards)

    by_name = {}
    for grp in groups:
        for k, stack, r1, r2 in zip(grp.names, grp.stacks, grp.r1s, grp.r2s):
            by_name[k] = (stack, r1, r2)
    lane = lax.broadcasted_iota(jnp.int32, (1, D_MODEL), 1)
    sink_row = jnp.where(lane < ATT_HEADS, jnp.pad(small["attn_sinks"], ((0, 0), (0, D_MODEL - LANE))),
                         jnp.where(lane == ATT_HEADS, jnp.pad(loss_p, ((0, 0), (0, D_MODEL - LANE)), mode="edge"),
                                   0.0))
    small_pack = _pack_rows([small["norm_mix_gain"], small["b_gates"][:, :D_MODEL], small["b_gates"][:, D_MODEL:],
                             small["ret_gn_gain"], small["norm_mlp_gain"], small["norm_final_gain"], sink_row])
    xi, yi, ci = _coords()
    (small_all,) = _comm_alone(_small_comm(small_pack), "small_exchange")
    idx_arr = jnp.stack([4 * xi + 2 * yi + ci, 2 * xi + yi]).astype(jnp.int32)
    in_pieces = ["w_att_up", "w_ret_up", "w_out", "w_ff2", "w_in"]
    in_sum = ["w_ff1"]
    order = in_pieces + in_sum
    summed_t = _shard_sum(*[[by_name[k][i] for k in in_sum] for i in range(3)], idx_arr)

    def shard(tree, k):
        return tree[k][0].T if k == "w_in" else tree[k][0]

    adam_out = _adam_big([[by_name[k][i] for k in in_pieces] for i in range(3)], [g.T for g in summed_t],
                         [shard(w, k) for k in order], [shard(m, k) for k in order], [shard(v, k) for k in order],
                         idx_arr)
    big_out = [adam_out[4 * order.index(k) + i].T if k == "w_in" else adam_out[4 * order.index(k) + i]
               for k in BIG for i in range(4)]
    sm_g, sm_d, sm_m, sm_v = _adam_small(small_all, _pack_rows(_small_rows(*[w[k] for k in WEIGHTS if k not in BIG])),
                                         _pack_rows(_small_rows(*[m[k] for k in WEIGHTS if k not in BIG])),
                                         _pack_rows(_small_rows(*[v[k] for k in WEIGHTS if k not in BIG])))

    loss = sm_g[6, ATT_HEADS]
    kinds = []
    for idx, packed in enumerate((sm_g, sm_d, sm_m, sm_v)):
        out = _unpack_small(packed)
        for a, k in enumerate(BIG):
            out[k] = big_out[4 * a + idx][None]
        kinds.append(out)
    return (loss, dx[None], *[kinds[0][k] for k in WEIGHTS], *[kinds[1][k] for k in WEIGHTS],
            *[kinds[2][k] for k in WEIGHTS], *[kinds[3][k] for k in WEIGHTS])
```

```python
import functools
import math

import jax
import jax.numpy as jnp
from jax import lax
from jax.experimental import pallas as pl
from jax.experimental.pallas import tpu as pltpu

F32 = jnp.float32
BF = jnp.bfloat16
MESH = pl.DeviceIdType.MESH

D_MODEL = 1024
ATT_HEADS = 8
ATT_HEAD_DIM = 64
ATT_BLOCK = 128
ROPE_DIM = 16
ROPE_THETA = 500000.0
RET_HEADS = 4
RET_KEY_DIM = 128
RET_VAL_DIM = 256
RET_CHUNK = 128
RET_ROT_BASE = 10000.0
D_FF = 4096
NORM_EPS = 1e-6
GN_EPS = 1e-6
NEG_INF = -1e30
ATT_SCALE = ATT_HEAD_DIM ** -0.5
RET_SCALE = RET_KEY_DIM ** -0.5

C_QA, C_KA, C_VA, C_QR, C_KR, C_VR, C_GR, C_GATES, C_END = 0, 512, 640, 768, 1280, 1792, 2816, 3840, 5888

ADAM_LR = 0.001
ADAM_B1 = 0.9
ADAM_B2 = 0.999
ADAM_EPS = 1e-08
ADAM_WD = 0.01
ADAM_STEP = 10

N_DEV = 8
LANE = 128
VMEM_LIMIT = 56 * 1024 * 1024
TOKEN_TILE = 512
MLP_TOKEN_TILE = 256
TN_TOKEN_TILE = 2048
FF_CHUNKS = (0, 1024, 2048, 3072, 4096)
IN_CHUNKS_FWD = (C_QA, C_KA, C_QR, C_KR, C_VR, C_GR, C_GATES, C_END)
IN_CHUNKS_BWD = (C_QA, C_KA, C_QR, C_GATES, C_END)

RET_LOG_GAMMA = tuple(math.log1p(-(2.0 ** (-5.0 - h))) for h in range(RET_HEADS))
RET_CHUNK_DECAY = tuple(math.exp(RET_CHUNK * lg) for lg in RET_LOG_GAMMA)

VMEM_FULL = pl.BlockSpec(memory_space=pltpu.VMEM)
SMEM_FULL = pl.BlockSpec(memory_space=pltpu.SMEM)
ANY = pl.BlockSpec(memory_space=pl.ANY)


def _params(n_axes):
    return pltpu.CompilerParams(dimension_semantics=("arbitrary",) * n_axes, vmem_limit_bytes=VMEM_LIMIT)


def _nn(a, b):
    return jnp.dot(a, b, preferred_element_type=F32)


def _nt(a, b):
    return lax.dot_general(a, b, (((1,), (1,)), ((), ())), preferred_element_type=F32)


def _tn(a, b):
    return lax.dot_general(a, b, (((0,), (0,)), ((), ())), preferred_element_type=F32)


def _sigmoid(v):
    return 1.0 / (1.0 + jnp.exp(-v))


def _rows(tile, width):
    return pl.BlockSpec((tile, width), lambda i: (i, 0))


def _cols(height, tile):
    return pl.BlockSpec((height, tile), lambda i: (0, i))


class _Resident:
    def __init__(self, hbm_ref, vmem_ref, sems, bounds):
        self.hbm, self.vmem, self.sems, self.bounds = hbm_ref, vmem_ref, sems, bounds

    @staticmethod
    def scratch(w, bounds):
        return [pltpu.VMEM(w.shape, w.dtype), pltpu.SemaphoreType.DMA((len(bounds) - 1,))]

    @staticmethod
    def load(*weights):
        @pl.when(pl.program_id(0) == 0)
        def _():
            copies = [w._copy(c) for w in weights for c in range(len(w.bounds) - 1)]
            for cp in copies:
                cp.start()
            for cp in copies:
                cp.wait()

    def _rows(self, c):
        return pl.ds(self.bounds[c], self.bounds[c + 1] - self.bounds[c])

    def _copy(self, c):
        return pltpu.make_async_copy(self.hbm.at[self._rows(c)], self.vmem.at[self._rows(c)], self.sems.at[c])

    def chunk(self, c):
        return self.vmem.at[self._rows(c)]


class _Comm:
    def __init__(self, inputs, out_shapes, scratch, start, finish, relay=None):
        self.inputs, self.out_shapes, self.scratch = inputs, out_shapes, scratch
        self.start, self.finish, self.relay = start, finish, relay


RELAY_AT = 0.7


def _join(a, b):
    na_in, na_out, na_sem = len(a.inputs), len(a.out_shapes), len(a.scratch)

    def both(name):
        def run(ins, outs, sems):
            for part, args in ((a, (ins[:na_in], outs[:na_out], sems[:na_sem])),
                               (b, (ins[na_in:], outs[na_out:], sems[na_sem:]))):
                if getattr(part, name) is not None:
                    getattr(part, name)(*args)
        return run

    return _Comm(list(a.inputs) + list(b.inputs), list(a.out_shapes) + list(b.out_shapes),
                 list(a.scratch) + list(b.scratch), both("start"), both("finish"),
                 both("relay") if a.relay or b.relay else None)


def _call(body, args, comm=None, *, name, grid, in_specs, out_specs, out_shape, scratch_shapes=()):
    params = _params(len(grid))
    if comm is None:
        return pl.pallas_call(body, name=name, grid=grid, in_specs=in_specs, out_specs=out_specs, out_shape=out_shape,
                              scratch_shapes=scratch_shapes, compiler_params=params)(*args), ()
    single = not isinstance(out_specs, (list, tuple))
    out_specs_l = [out_specs] if single else list(out_specs)
    out_shape_l = [out_shape] if single else list(out_shape)
    n_in, n_out, n_scr = len(in_specs), len(out_specs_l), len(scratch_shapes)
    n_cin, n_cout = len(comm.inputs), len(comm.out_shapes)

    def hosted(*refs):
        it = iter(refs)
        ins, cin, outs, cout, scr = ([next(it) for _ in range(k)] for k in (n_in, n_cin, n_out, n_cout, n_scr))
        sems = list(it)
        ids = [pl.program_id(k) for k in range(len(grid))]
        first = functools.reduce(jnp.logical_and, [i == 0 for i in ids])
        last = functools.reduce(jnp.logical_and, [i == g - 1 for i, g in zip(ids, grid)])

        @pl.when(first)
        def _():
            comm.start(cin, cout, sems)

        if comm.relay is not None:
            at = [int(grid[0] * RELAY_AT)] + [0] * (len(grid) - 1)

            @pl.when(functools.reduce(jnp.logical_and, [i == v for i, v in zip(ids, at)]))
            def _():
                comm.relay(cin, cout, sems)

        body(*ins, *outs, *scr)

        @pl.when(last)
        def _():
            comm.finish(cin, cout, sems)

    res = pl.pallas_call(
        hosted, name=name, grid=grid, in_specs=list(in_specs) + [ANY] * n_cin,
        out_specs=out_specs_l + [ANY] * n_cout, out_shape=out_shape_l + list(comm.out_shapes),
        scratch_shapes=list(scratch_shapes) + list(comm.scratch), compiler_params=params)(*args, *comm.inputs)
    return (res[0] if single else res[:n_out]), res[n_out:]


def _comm_alone(comm, name):
    n_cin, n_cout = len(comm.inputs), len(comm.out_shapes)

    def body(*refs):
        cin, cout, sems = refs[:n_cin], refs[n_cin:n_cin + n_cout], refs[n_cin + n_cout:]
        comm.start(cin, cout, sems)
        if comm.relay is not None:
            comm.relay(cin, cout, sems)
        comm.finish(cin, cout, sems)

    return pl.pallas_call(body, name=name, in_specs=[ANY] * n_cin, out_specs=[ANY] * n_cout,
                          out_shape=list(comm.out_shapes), scratch_shapes=list(comm.scratch))(*comm.inputs)


def _slabs(v, fn):
    return jnp.concatenate([fn(v[:, LANE * j:LANE * (j + 1)]) for j in range(v.shape[1] // LANE)], axis=1)


def _rope_att(v, ca, sa, sb):
    return _slabs(v, lambda t: t * ca + pltpu.roll(t, LANE - 8, 1) * sa + pltpu.roll(t, 8, 1) * sb)


def _rope_att_t(v, ca, sa, sb):
    return _slabs(v, lambda t: t * ca + pltpu.roll(t * sa, 8, 1) + pltpu.roll(t * sb, LANE - 8, 1))


def _rope_att_rows(v, ct, st, sign):
    parts = []
    for h in range(v.shape[0] // ATT_HEAD_DIM):
        r0 = ATT_HEAD_DIM * h
        x1, x2 = v[r0:r0 + 8], v[r0 + 8:r0 + 16]
        parts += [x1 * ct - sign * (x2 * st), x2 * ct + sign * (x1 * st), v[r0 + 16:r0 + ATT_HEAD_DIM]]
    return jnp.concatenate(parts, axis=0)


def _rope_ret(v, cr, sr):
    return _slabs(v, lambda t: t * cr + pltpu.roll(t, 64, 1) * sr)


def _rope_ret_t(v, cr, sr):
    return _slabs(v, lambda t: t * cr + pltpu.roll(t * sr, 64, 1))


def _rope_tables(seq):
    pos = jnp.arange(seq, dtype=F32)

    def cs(dim, theta):
        inv = theta ** (-jnp.arange(0, dim, 2, dtype=F32) / dim)
        ang = pos[:, None] * inv[None, :]
        return jnp.cos(ang), jnp.sin(ang)

    ca, sa_ = cs(ROPE_DIM, ROPE_THETA)
    zeros8 = jnp.zeros_like(ca)
    rest = ATT_HEAD_DIM - ROPE_DIM
    c64 = jnp.concatenate([ca, ca, jnp.ones((seq, rest), F32)], axis=1)
    sa64 = jnp.concatenate([-sa_, zeros8, jnp.zeros((seq, rest), F32)], axis=1)
    sb64 = jnp.concatenate([zeros8, sa_, jnp.zeros((seq, rest), F32)], axis=1)
    cr, sr_ = cs(RET_KEY_DIM, RET_ROT_BASE)
    return (jnp.tile(c64, (1, 2)), jnp.tile(sa64, (1, 2)), jnp.tile(sb64, (1, 2)),
            jnp.concatenate([cr, cr], axis=1), jnp.concatenate([-sr_, sr_], axis=1), ca.T, sa_.T)


def _ret_tables():
    c = RET_CHUNK
    lg = jnp.asarray(RET_LOG_GAMMA, F32)
    idx = jnp.arange(c, dtype=F32)
    diff = idx[:, None] - idx[None, :]
    intra = jnp.where(diff >= 0, jnp.exp(jnp.maximum(diff, 0.0) * lg[:, None, None]), 0.0)
    qd = jnp.exp((idx + 1.0)[None, :] * lg[:, None])[..., None]
    kd = jnp.exp((c - 1.0 - idx)[None, :] * lg[:, None])[..., None]
    return intra, jnp.broadcast_to(qd, (RET_HEADS, c, RET_KEY_DIM)), jnp.broadcast_to(kd, (RET_HEADS, c, RET_KEY_DIM))


def _inproj_fwd(x, g1, w_in, tabs, comm=None):
    seq = x.shape[0]
    tm = min(TOKEN_TILE, seq)

    def body(x_ref, g_ref, w_hbm, ca_ref, sa_ref, sb_ref, cr_ref, sr_ref, ct_ref, st_ref,
             xn_ref, qt_ref, kv_ref, kvt_ref, qkr_ref, vr_ref, gr_ref, gates_ref, w_vmem, w_sems):
        w = _Resident(w_hbm, w_vmem, w_sems, IN_CHUNKS_FWD)
        _Resident.load(w)
        xf = x_ref[...]
        r = lax.rsqrt(jnp.mean(xf * xf, axis=-1, keepdims=True) + NORM_EPS)
        xn = (xf * r * g_ref[...]).astype(BF)
        xn_ref[...] = xn
        ct, st = ct_ref[...], st_ref[...]
        qt_ref[...] = _rope_att_rows(_nt(w.chunk(0)[...], xn), ct, st, 1.0).astype(BF)
        w_kv = w.chunk(1)
        kvt = _nt(w_kv[...], xn)
        kvt_ref[:LANE, :] = _rope_att_rows(kvt[:LANE], ct, st, 1.0).astype(BF)
        kvt_ref[LANE:, :] = kvt[LANE:].astype(BF)
        kvn = _nt(xn, w_kv[...])
        kv_ref[:, :LANE] = _rope_att(kvn[:, :LANE], ca_ref[...], sa_ref[...], sb_ref[...]).astype(BF)
        kv_ref[:, LANE:] = kvn[:, LANE:].astype(BF)
        cr, sr = cr_ref[...], sr_ref[...]
        qr = _rope_ret(_nt(xn, w.chunk(2)[...]), cr, sr)
        qkr_ref[:, :512] = qr.astype(BF)
        kr = _rope_ret(_nt(xn, w.chunk(3)[...]), cr, sr) * RET_SCALE
        qkr_ref[:, 512:] = kr.astype(BF)
        vr_ref[...] = _nt(xn, w.chunk(4)[...]).astype(BF)
        gr_ref[...] = _nt(xn, w.chunk(5)[...]).astype(BF)
        gates_ref[...] = _nt(xn, w.chunk(6)[...]).astype(BF)

    tab = _rows(tm, LANE)
    tab_t = _cols(8, tm)
    return _call(
        body, (x, g1, w_in, *tabs), comm, name="inproj_fwd", grid=(seq // tm,),
        in_specs=[_rows(tm, D_MODEL), VMEM_FULL, ANY, tab, tab, tab, tab, tab, tab_t, tab_t],
        out_specs=[_rows(tm, D_MODEL), _cols(512, tm), _rows(tm, 256), _cols(256, tm), _rows(tm, 1024),
                   _rows(tm, 1024), _rows(tm, 1024), _rows(tm, 2048)],
        out_shape=[jax.ShapeDtypeStruct(s, BF) for s in ((seq, D_MODEL), (512, seq), (seq, 256), (256, seq),
                                                         (seq, 1024), (seq, 1024), (seq, 1024), (seq, 2048))],
        scratch_shapes=_Resident.scratch(w_in, IN_CHUNKS_FWD))


ATT_Q_PER_KV = ATT_HEADS // 2


def _att_group(ref, kv):
    d = ATT_HEAD_DIM
    g = jnp.concatenate([ref[d * (ATT_Q_PER_KV * kv + j):d * (ATT_Q_PER_KV * kv + j + 1), :]
                         for j in range(ATT_Q_PER_KV)], axis=1)
    z = jnp.zeros_like(g)
    return jnp.concatenate([g, z] if kv == 0 else [z, g], axis=0)


def _att_probs(kwin, qpad, sink_ref, kv, i):
    c = ATT_BLOCK
    kj = lax.broadcasted_iota(jnp.int32, (2 * c, c), 0)
    qi = lax.broadcasted_iota(jnp.int32, (2 * c, c), 1)
    allowed = (kj > qi) & (kj <= qi + c) & ((kj >= c) | (i > 0))
    allowed = jnp.concatenate([allowed] * ATT_Q_PER_KV, axis=1)
    s = jnp.where(allowed, _nn(kwin, qpad) * ATT_SCALE, NEG_INF)
    sk = jnp.concatenate([jnp.full((1, c), sink_ref[ATT_Q_PER_KV * kv + j], F32) for j in range(ATT_Q_PER_KV)], axis=1)
    m = jnp.maximum(jnp.max(s, axis=0, keepdims=True), sk)
    pe = jnp.exp(s - m)
    psink = jnp.exp(sk - m)
    inv = 1.0 / (jnp.sum(pe, axis=0, keepdims=True) + psink)
    return pe * inv, psink * inv


def _seq_fwd(qt, kv, kvt, sinks, qkr, vr, gr, gain, rtabs, comm=None):
    seq = kv.shape[0]
    c = ATT_BLOCK
    assert c == RET_CHUNK
    d = ATT_HEAD_DIM
    nb = seq // c
    dk, dv = RET_KEY_DIM, RET_VAL_DIM

    def body(sink_ref, qt_ref, kvc_ref, kvp_ref, kvtc_ref, kvtp_ref, qk_ref, v_ref, g_ref, gain_ref, intra_ref,
             qd_ref, kd_ref, ot_ref, ret_ref, yrp_ref, st_ref, state):
        i = pl.program_id(0)
        kwin = jnp.concatenate([kvp_ref[:, :LANE], kvc_ref[:, :LANE]], axis=0)
        vtwin = jnp.concatenate([kvtp_ref[LANE:, :], kvtc_ref[LANE:, :]], axis=1)
        for g in (0, 1):
            prob, _ = _att_probs(kwin, _att_group(qt_ref, g), sink_ref, g, i)
            out = _nn(vtwin[d * g:d * (g + 1), :], prob.astype(BF))
            for j in range(ATT_Q_PER_KV):
                h = ATT_Q_PER_KV * g + j
                ot_ref[d * h:d * (h + 1), :] = out[:, c * j:c * (j + 1)].astype(BF)

        @pl.when(i == 0)
        def _():
            state[...] = jnp.zeros_like(state)

        for h in range(RET_HEADS):
            qh = qk_ref[:, dk * h:dk * (h + 1)]
            kh = qk_ref[:, 512 + dk * h:512 + dk * (h + 1)]
            vh = v_ref[:, dv * h:dv * (h + 1)]
            sh = state[h]
            shb = sh.astype(BF)
            st_ref[0, h] = shb
            att = _nt(qh, kh) * intra_ref[h]
            inner = _nn(att.astype(BF), vh)
            cross = _nn((qh.astype(F32) * qd_ref[h]).astype(BF), shb)
            out = inner + cross
            state[h] = sh * RET_CHUNK_DECAY[h] + _tn((kh.astype(F32) * kd_ref[h]).astype(BF), vh)
            ret_ref[:, dv * h:dv * (h + 1)] = out
            mu = jnp.mean(out, axis=-1, keepdims=True)
            dev = out - mu
            var = jnp.mean(dev * dev, axis=-1, keepdims=True)
            y = dev * lax.rsqrt(var + GN_EPS) * gain_ref[:, dv * h:dv * (h + 1)]
            g = g_ref[:, dv * h:dv * (h + 1)].astype(F32)
            yrp_ref[:, dv * h:dv * (h + 1)] = (g * _sigmoid(g) * y).astype(BF)

    prev = lambda i: jnp.maximum(i - 1, 0)
    return _call(
        body, (sinks, qt, kv, kv, kvt, kvt, qkr, vr, gr, gain, *rtabs), comm, name="seq_fwd", grid=(nb,),
        in_specs=[SMEM_FULL, _cols(512, c), _rows(c, 256), pl.BlockSpec((c, 256), lambda i: (prev(i), 0)),
                  _cols(256, c), pl.BlockSpec((256, c), lambda i: (0, prev(i))),
                  _rows(c, 1024), _rows(c, 1024), _rows(c, 1024), VMEM_FULL, VMEM_FULL, VMEM_FULL, VMEM_FULL],
        out_specs=[_cols(512, c), _rows(c, 1024), _rows(c, 1024),
                   pl.BlockSpec((1, RET_HEADS, dk, dv), lambda i: (i, 0, 0, 0))],
        out_shape=[jax.ShapeDtypeStruct((512, seq), BF), jax.ShapeDtypeStruct((seq, 1024), F32),
                   jax.ShapeDtypeStruct((seq, 1024), BF), jax.ShapeDtypeStruct((nb, RET_HEADS, dk, dv), BF)],
        scratch_shapes=[pltpu.VMEM((RET_HEADS, dk, dv), F32)])


def _mix_fwd(oat, yrp, gates, bg, x, w_att, w_ret, w_out):
    seq = x.shape[0]
    tm = min(TOKEN_TILE, seq)

    def body(oat_ref, yrp_ref, gates_ref, bg_ref, x_ref, wa_ref, wr_ref, wo_ref, ya_ref, yr_ref, mg_ref, h1_ref):
        ya = _tn(oat_ref[...], wa_ref[...])
        yr = _nn(yrp_ref[...], wr_ref[...])
        gt = _sigmoid(gates_ref[...].astype(F32) + bg_ref[...])
        merged = (gt[:, :D_MODEL] * ya + gt[:, D_MODEL:] * yr).astype(BF)
        ya_ref[...] = ya.astype(BF)
        yr_ref[...] = yr.astype(BF)
        mg_ref[...] = merged
        h1_ref[...] = x_ref[...] + _nn(merged, wo_ref[...])

    return pl.pallas_call(
        body, name="mix_fwd", grid=(seq // tm,),
        in_specs=[_cols(512, tm), _rows(tm, 1024), _rows(tm, 2048), VMEM_FULL, _rows(tm, D_MODEL),
                  VMEM_FULL, VMEM_FULL, VMEM_FULL],
        out_specs=[_rows(tm, D_MODEL)] * 4,
        out_shape=[jax.ShapeDtypeStruct((seq, D_MODEL), BF)] * 3 + [jax.ShapeDtypeStruct((seq, D_MODEL), F32)],
        compiler_params=_params(1),
    )(oat, yrp, gates, bg, x, w_att, w_ret, w_out)


def _mlp_loss_step(h1, g2, g3, target, w_ff1, w_ff2):
    seq = h1.shape[0]
    tm = min(MLP_TOKEN_TILE, seq)
    n_chunks = len(FF_CHUNKS) - 1

    def body(h1_ref, g2_ref, g3_ref, t_ref, w1_hbm, w2_hbm,
             xn2_ref, hdn_ref, dh2_ref, du_ref, dh1_ref, loss_ref, dg3_ref, dg2_ref,
             w1_vmem, w1_sems, w2_vmem, w2_sems, relu_u):
        w1 = _Resident(w1_hbm, w1_vmem, w1_sems, FF_CHUNKS)
        w2 = _Resident(w2_hbm, w2_vmem, w2_sems, FF_CHUNKS)
        _Resident.load(w1, w2)

        @pl.when(pl.program_id(0) == 0)
        def _():
            loss_ref[...] = jnp.zeros_like(loss_ref)
            dg3_ref[...] = jnp.zeros_like(dg3_ref)
            dg2_ref[...] = jnp.zeros_like(dg2_ref)

        h1v = h1_ref[...]
        r2 = lax.rsqrt(jnp.mean(h1v * h1v, axis=-1, keepdims=True) + NORM_EPS)
        xn2 = (h1v * r2 * g2_ref[...]).astype(BF)
        xn2_ref[...] = xn2
        h2 = h1v
        for c in range(n_chunks):
            cols = slice(FF_CHUNKS[c], FF_CHUNKS[c + 1])
            a = jnp.maximum(_nt(xn2, w1.chunk(c)[...]), 0.0)
            relu_u[:, cols] = a.astype(BF)
            hdn = jnp.square(a).astype(BF)
            hdn_ref[:, cols] = hdn
            h2 = h2 + _nn(hdn, w2.chunk(c)[...])
        r3 = lax.rsqrt(jnp.mean(h2 * h2, axis=-1, keepdims=True) + NORM_EPS)
        hn = h2 * r3
        err = hn * g3_ref[...] - t_ref[...]
        loss_ref[...] += jnp.sum(err * err) * (0.5 / D_MODEL)
        dy = err * (1.0 / D_MODEL)
        dg3_ref[...] += jnp.sum(dy * hn, axis=0, keepdims=True)
        z = dy * g3_ref[...]
        dh2 = r3 * (z - hn * jnp.mean(z * hn, axis=-1, keepdims=True))
        dh2b = dh2.astype(BF)
        dh2_ref[...] = dh2b
        dxn2 = jnp.zeros_like(dh2)
        for c in range(n_chunks):
            cols = slice(FF_CHUNKS[c], FF_CHUNKS[c + 1])
            du = (_nt(dh2b, w2.chunk(c)[...]) * (2.0 * relu_u[:, cols].astype(F32))).astype(BF)
            du_ref[:, cols] = du
            dxn2 = dxn2 + _nn(du, w1.chunk(c)[...])
        dnorm, dg = _rms_bwd(dxn2, h1v, g2_ref[...])
        dh1_ref[...] = dh2 + dnorm
        dg2_ref[...] += dg

    gain_acc = pl.BlockSpec((1, D_MODEL), lambda i: (0, 0))
    return pl.pallas_call(
        body, name="mlp_loss_step", grid=(seq // tm,),
        in_specs=[_rows(tm, D_MODEL), VMEM_FULL, VMEM_FULL, _rows(tm, D_MODEL), ANY, ANY],
        out_specs=[_rows(tm, D_MODEL), _rows(tm, D_FF), _rows(tm, D_MODEL), _rows(tm, D_FF), _rows(tm, D_MODEL),
                   pl.BlockSpec((1, LANE), lambda i: (0, 0)), gain_acc, gain_acc],
        out_shape=[jax.ShapeDtypeStruct((seq, D_MODEL), BF), jax.ShapeDtypeStruct((seq, D_FF), BF),
                   jax.ShapeDtypeStruct((seq, D_MODEL), BF), jax.ShapeDtypeStruct((seq, D_FF), BF),
                   jax.ShapeDtypeStruct((seq, D_MODEL), F32), jax.ShapeDtypeStruct((1, LANE), F32),
                   jax.ShapeDtypeStruct((1, D_MODEL), F32), jax.ShapeDtypeStruct((1, D_MODEL), F32)],
        scratch_shapes=(_Resident.scratch(w_ff1, FF_CHUNKS) + _Resident.scratch(w_ff2, FF_CHUNKS)
                        + [pltpu.VMEM((tm, D_FF), BF)]),
        compiler_params=_params(1),
    )(h1, g2, g3, target, w_ff1, w_ff2)


def _rms_bwd(dxn, xin, gain):
    r = lax.rsqrt(jnp.mean(xin * xin, axis=-1, keepdims=True) + NORM_EPS)
    xhat = xin * r
    z = dxn * gain
    dxin = r * (z - xhat * jnp.mean(z * xhat, axis=-1, keepdims=True))
    return dxin, jnp.sum(dxn * xhat, axis=0, keepdims=True)


def _mm_tn(a, b, name, a_is_t=False, comm=None):
    kdim, seq = a.shape if a_is_t else a.shape[::-1]
    ndim = b.shape[1]
    ts = min(TN_TOKEN_TILE, seq)
    tk = min(kdim, 1024)
    tn = min(ndim, 1024)
    n_steps = seq // ts

    def body(a_ref, b_ref, o_ref):
        @pl.when(pl.program_id(2) == 0)
        def _():
            o_ref[...] = jnp.zeros_like(o_ref)

        mm = _nn if a_is_t else _tn
        o_ref[...] += mm(a_ref[...].astype(BF), b_ref[...].astype(BF))

    a_spec = (pl.BlockSpec((tk, ts), lambda k, n, s: (k, s)) if a_is_t
              else pl.BlockSpec((ts, tk), lambda k, n, s: (s, k)))
    res = _call(
        body, (a, b), comm, name=name, grid=(kdim // tk, ndim // tn, n_steps),
        in_specs=[a_spec, pl.BlockSpec((ts, tn), lambda k, n, s: (s, n))],
        out_specs=pl.BlockSpec((tk, tn), lambda k, n, s: (k, n)),
        out_shape=jax.ShapeDtypeStruct((kdim, ndim), F32))
    return res[0] if comm is None else res


def _dw_in_t(pieces, is_t, xn1, comm=None):
    seq = xn1.shape[0]
    ts = seq
    tr = 256
    n_blk = [(p.shape[0] if t else p.shape[1]) // tr for p, t in zip(pieces, is_t)]
    offs = [sum(n_blk[:i]) for i in range(len(pieces) + 1)]
    n_p = len(pieces)

    def body(*refs):
        piece_refs, xn_ref, o_ref = refs[:n_p], refs[n_p], refs[n_p + 1]
        r, s = pl.program_id(0), pl.program_id(1)

        @pl.when(s == 0)
        def _():
            o_ref[...] = jnp.zeros_like(o_ref)

        for p in range(n_p):
            @pl.when((r >= offs[p]) & (r < offs[p + 1]))
            def _(p=p):
                mm = _nn if is_t[p] else _tn
                o_ref[...] += mm(piece_refs[p][...], xn_ref[pl.ds(pl.multiple_of(s * ts, ts), ts), :])

    def piece_spec(p):
        def index(r, s):
            inside = (r >= offs[p]) & (r < offs[p + 1])
            tok, blk = jnp.where(inside, s, 0), jnp.clip(r - offs[p], 0, n_blk[p] - 1)
            return (blk, tok) if is_t[p] else (tok, blk)
        return pl.BlockSpec((tr, ts) if is_t[p] else (ts, tr), index)

    return _call(
        body, (*pieces, xn1), comm, name="dw_in", grid=(offs[-1], seq // ts),
        in_specs=[piece_spec(p) for p in range(n_p)] + [VMEM_FULL],
        out_specs=pl.BlockSpec((tr, D_MODEL), lambda r, s: (r, 0)),
        out_shape=jax.ShapeDtypeStruct((offs[-1] * tr, D_MODEL), F32))


def _mix_bwd(dh1, ya, yr, gates, bg, w_att, w_ret, w_out, comm=None):
    seq = dh1.shape[0]
    tm = min(TOKEN_TILE, seq)

    def body(dh1_ref, ya_ref, yr_ref, gates_ref, bg_ref, wa_ref, wr_ref, wo_ref,
             dya_ref, dyr_ref, dgates_ref, doa_ref, dyrp_ref, db_ref):
        @pl.when(pl.program_id(0) == 0)
        def _():
            db_ref[...] = jnp.zeros_like(db_ref)

        dm = _nt(dh1_ref[...].astype(BF), wo_ref[...])
        gt = _sigmoid(gates_ref[...].astype(F32) + bg_ref[...])
        ga, gr = gt[:, :D_MODEL], gt[:, D_MODEL:]
        dya = (dm * ga).astype(BF)
        dyr = (dm * gr).astype(BF)
        dya_ref[...] = dya
        dyr_ref[...] = dyr
        dga = dm * ya_ref[...].astype(F32) * ga * (1.0 - ga)
        dgr = dm * yr_ref[...].astype(F32) * gr * (1.0 - gr)
        dgates_ref[:, :D_MODEL] = dga.astype(BF)
        dgates_ref[:, D_MODEL:] = dgr.astype(BF)
        db_ref[:, :D_MODEL] += jnp.sum(dga, axis=0, keepdims=True)
        db_ref[:, D_MODEL:] += jnp.sum(dgr, axis=0, keepdims=True)
        doa_ref[...] = _nt(wa_ref[...], dya).astype(BF)
        dyrp_ref[...] = _nt(dyr, wr_ref[...]).astype(BF)

    return _call(
        body, (dh1, ya, yr, gates, bg, w_att, w_ret, w_out), comm, name="mix_bwd", grid=(seq // tm,),
        in_specs=[_rows(tm, D_MODEL), _rows(tm, D_MODEL), _rows(tm, D_MODEL), _rows(tm, 2048), VMEM_FULL,
                  VMEM_FULL, VMEM_FULL, VMEM_FULL],
        out_specs=[_rows(tm, D_MODEL), _rows(tm, D_MODEL), _rows(tm, 2048), _cols(512, tm), _rows(tm, 1024),
                   pl.BlockSpec((1, 2048), lambda i: (0, 0))],
        out_shape=[jax.ShapeDtypeStruct((seq, D_MODEL), BF), jax.ShapeDtypeStruct((seq, D_MODEL), BF),
                   jax.ShapeDtypeStruct((seq, 2048), BF), jax.ShapeDtypeStruct((512, seq), BF),
                   jax.ShapeDtypeStruct((seq, 1024), BF), jax.ShapeDtypeStruct((1, 2048), F32)])


def _seq_bwd(qt, kv, kvt, dot, sinks, qkr, vr, gr, ret, dyrp, states, gain, rtabs, tabs, comm=None):
    seq = kv.shape[0]
    c = ATT_BLOCK
    assert c == RET_CHUNK
    d = ATT_HEAD_DIM
    nb = seq // c
    dk, dv = RET_KEY_DIM, RET_VAL_DIM
    ca, sa, sb, cr, sr, ct, st = tabs

    def body(sink_ref, qt_ref, kvc_ref, kvp_ref, kvtc_ref, kvtp_ref, dot_ref, ca_ref, sa_ref, sb_ref, ct_ref, st_ref,
             qk_ref, v_ref, g_ref, ret_ref, dyp_ref, st8_ref, gain_ref, intra_ref, qd_ref, kd_ref, cr_ref, sr_ref,
             dqt_ref, dkv_ref, dsink_ref, dall_ref, dgain_ref, carry, dstate):
        step = pl.program_id(0)
        i = nb - 1 - step

        @pl.when(step == 0)
        def _():
            carry[...] = jnp.zeros_like(carry)
            dsink_ref[...] = jnp.zeros_like(dsink_ref)
            dstate[...] = jnp.zeros_like(dstate)
            dgain_ref[...] = jnp.zeros_like(dgain_ref)

        kwin = jnp.concatenate([kvp_ref[:, :LANE], kvc_ref[:, :LANE]], axis=0)
        vwin = jnp.concatenate([kvp_ref[:, LANE:], kvc_ref[:, LANE:]], axis=0)
        ktwin = jnp.concatenate([kvtp_ref[:LANE, :], kvtc_ref[:LANE, :]], axis=1)
        ctv, stv = ct_ref[...], st_ref[...]
        lane1 = lax.broadcasted_iota(jnp.int32, (1, LANE), 1)
        dkw = jnp.zeros((2 * c, LANE), F32)
        dvw = jnp.zeros((2 * c, LANE), F32)
        dsink = jnp.zeros((1, LANE), F32)
        for g in (0, 1):
            qpad, dopad = _att_group(qt_ref, g), _att_group(dot_ref, g)
            prob, psink = _att_probs(kwin, qpad, sink_ref, g, i)
            dprob = _nn(vwin, dopad)
            drow = jnp.sum(dprob * prob, axis=0, keepdims=True)
            ds = (prob * (dprob - drow) * ATT_SCALE).astype(BF)
            dqg = _nn(ktwin[d * g:d * (g + 1), :], ds)
            dkw = dkw + _nt(ds, qpad)
            dvw = dvw + _nt(prob.astype(BF), dopad)
            dsink_lanes = psink * drow
            for j in range(ATT_Q_PER_KV):
                h = ATT_Q_PER_KV * g + j
                dqt_ref[d * h:d * (h + 1), :] = _rope_att_rows(dqg[:, c * j:c * (j + 1)], ctv, stv, -1.0).astype(BF)
                dsink = dsink + jnp.where(lane1 == h, -jnp.sum(dsink_lanes[:, c * j:c * (j + 1)]), 0.0)
        dsink_ref[...] += dsink
        total = carry[...] + jnp.concatenate([dkw[c:], dvw[c:]], axis=1)
        dkv_ref[:, :LANE] = _rope_att_t(total[:, :LANE], ca_ref[...], sa_ref[...], sb_ref[...]).astype(BF)
        dkv_ref[:, LANE:] = total[:, LANE:].astype(BF)
        carry[...] = jnp.concatenate([dkw[:c], dvw[:c]], axis=1)

        crv, srv = cr_ref[...], sr_ref[...]
        for h in range(RET_HEADS):
            vs = slice(dv * h, dv * (h + 1))
            qh = qk_ref[:, dk * h:dk * (h + 1)]
            kh = qk_ref[:, 512 + dk * h:512 + dk * (h + 1)]
            vh = v_ref[:, vs]
            out = ret_ref[:, vs]
            g = g_ref[:, vs].astype(F32)
            dyp = dyp_ref[:, vs].astype(F32)
            gain_h = gain_ref[:, vs]
            mu = jnp.mean(out, axis=-1, keepdims=True)
            dev = out - mu
            rstd = lax.rsqrt(jnp.mean(dev * dev, axis=-1, keepdims=True) + GN_EPS)
            yn = dev * rstd
            sg = _sigmoid(g)
            dg = dyp * (yn * gain_h) * (sg * (1.0 + g * (1.0 - sg)))
            dy = dyp * (g * sg)
            dgain_ref[:, vs] += jnp.sum(dy * yn, axis=0, keepdims=True)
            dyn = dy * gain_h
            dout = rstd * (dyn - jnp.mean(dyn, axis=-1, keepdims=True)
                           - yn * jnp.mean(dyn * yn, axis=-1, keepdims=True))
            doutb = dout.astype(BF)
            sc = st8_ref[0, h]
            dsp = dstate[h]
            dspb = dsp.astype(BF)
            intra, qdv, kdv = intra_ref[h], qd_ref[h], kd_ref[h]
            att = _nt(qh, kh) * intra
            dab = (_nt(doutb, vh) * intra).astype(BF)
            qdec = (qh.astype(F32) * qdv).astype(BF)
            kdec = (kh.astype(F32) * kdv).astype(BF)
            dq = _nn(dab, kh) + _nt(doutb, sc) * qdv
            dkk = _tn(dab, qh) + _nt(vh, dspb) * kdv
            dvv = _tn(att.astype(BF), doutb) + _nn(kdec, dspb)
            dstate[h] = dsp * RET_CHUNK_DECAY[h] + _tn(qdec, doutb)
            dall_ref[:, dk * h:dk * (h + 1)] = _rope_ret_t(dq, crv, srv).astype(BF)
            dall_ref[:, 512 + dk * h:512 + dk * (h + 1)] = (_rope_ret_t(dkk, crv, srv) * RET_SCALE).astype(BF)
            dall_ref[:, 1024 + dv * h:1024 + dv * (h + 1)] = dvv.astype(BF)
            dall_ref[:, 2048 + dv * h:2048 + dv * (h + 1)] = dg.astype(BF)

    cur = lambda s: nb - 1 - s
    prev = lambda s: jnp.maximum(nb - 2 - s, 0)
    rows = lambda w: pl.BlockSpec((c, w), lambda s: (cur(s), 0))
    cols = lambda h: pl.BlockSpec((h, c), lambda s: (0, cur(s)))
    acc = lambda w: pl.BlockSpec((1, w), lambda s: (0, 0))
    return _call(
        body, (sinks, qt, kv, kv, kvt, kvt, dot, ca, sa, sb, ct, st, qkr, vr, gr, ret, dyrp, states, gain, *rtabs,
               cr, sr), comm, name="seq_bwd", grid=(nb,),
        in_specs=[SMEM_FULL, cols(512), rows(256), pl.BlockSpec((c, 256), lambda s: (prev(s), 0)), cols(256),
                  pl.BlockSpec((256, c), lambda s: (0, prev(s))), cols(512), rows(LANE), rows(LANE), rows(LANE),
                  cols(8), cols(8), rows(1024), rows(1024), rows(1024), rows(1024), rows(1024),
                  pl.BlockSpec((1, RET_HEADS, dk, dv), lambda s: (cur(s), 0, 0, 0)),
                  VMEM_FULL, VMEM_FULL, VMEM_FULL, VMEM_FULL, rows(LANE), rows(LANE)],
        out_specs=[cols(512), rows(256), acc(LANE), rows(3072), acc(1024)],
        out_shape=[jax.ShapeDtypeStruct((512, seq), BF), jax.ShapeDtypeStruct((seq, 256), BF),
                   jax.ShapeDtypeStruct((1, LANE), F32), jax.ShapeDtypeStruct((seq, 3072), BF),
                   jax.ShapeDtypeStruct((1, 1024), F32)],
        scratch_shapes=[pltpu.VMEM((c, 256), F32), pltpu.VMEM((RET_HEADS, dk, dv), F32)])


def _inproj_bwd(dqa, dkva, dret, dgates, dh1, x, g1, w_in, comm=None):
    seq = x.shape[0]
    tm = min(TOKEN_TILE, seq)

    def body(dqa_ref, dkva_ref, dret_ref, dgates_ref, dh1_ref, x_ref, g_ref, w_hbm, dx_ref, dg1_ref, w_vmem, w_sems):
        w = _Resident(w_hbm, w_vmem, w_sems, IN_CHUNKS_BWD)
        _Resident.load(w)

        @pl.when(pl.program_id(0) == 0)
        def _():
            dg1_ref[...] = jnp.zeros_like(dg1_ref)

        dxn = _tn(dqa_ref[...], w.chunk(0)[...])
        dxn = dxn + _nn(dkva_ref[...], w.chunk(1)[...])
        dxn = dxn + _nn(dret_ref[...], w.chunk(2)[...])
        dxn = dxn + _nn(dgates_ref[...], w.chunk(3)[...])
        dnorm, dg = _rms_bwd(dxn, x_ref[...], g_ref[...])
        dx_ref[...] = dh1_ref[...] + dnorm
        dg1_ref[...] += dg

    return _call(
        body, (dqa, dkva, dret, dgates, dh1, x, g1, w_in), comm, name="inproj_bwd", grid=(seq // tm,),
        in_specs=[_cols(512, tm), _rows(tm, 256), _rows(tm, 3072), _rows(tm, 2048), _rows(tm, D_MODEL),
                  _rows(tm, D_MODEL), VMEM_FULL, ANY],
        out_specs=[_rows(tm, D_MODEL), pl.BlockSpec((1, D_MODEL), lambda i: (0, 0))],
        out_shape=[jax.ShapeDtypeStruct((seq, D_MODEL), F32), jax.ShapeDtypeStruct((1, D_MODEL), F32)],
        scratch_shapes=_Resident.scratch(w_in, IN_CHUNKS_BWD))


def _local_step(x, target, g1, bg, sinks, gain, g2, g3, w_in, shards):
    seq = x.shape[0]
    ca, sa, sb, cr, sr, ct, st = _rope_tables(seq)
    rtabs = _ret_tables()
    scatter = _Scatter

    comm_mid, unpack_mid = _gather(shards, ("w_att_up", "w_ret_up", "w_out"), relay=True)
    comm_ff1, unpack_ff1 = _gather(shards, ("w_ff1",), relay=True)
    (xn1, qt, kv, kvt, qkr, vr, gr, gates), got = _inproj_fwd(x, g1, w_in, (ca, sa, sb, cr, sr, ct, st),
                                                              _join(comm_mid, comm_ff1))
    w_att, w_ret, w_out = unpack_mid(got[:3])
    (w_ff1,) = unpack_ff1(got[3:])
    comm, unpack = _gather(shards, ("w_ff2",), relay=True)
    (oa, ret, yrp, states), got = _seq_fwd(qt, kv, kvt, sinks, qkr, vr, gr, gain, rtabs, comm)
    (w_ff2,) = unpack(got)
    ya, yr, merged, h1 = _mix_fwd(oa, yrp, gates, bg, x, w_att, w_ret, w_out)
    xn2, hdn, dh2, du, dh1, loss, dg3, dg2 = _mlp_loss_step(h1, g2, g3, target, w_ff1, w_ff2)

    ff2 = scatter(dict(w_ff2=_mm_tn(hdn, dh2, "dw_ff2")), "ff2")
    d_ff1, r1 = _mm_tn(du, xn2, "dw_ff1", comm=ff2.pair_comm())
    ff2.pair_done(r1)
    ff1 = scatter(dict(w_ff1=d_ff1), "ff1")
    (dya, dyr, dgates, doa, dyrp, db), r1 = _mix_bwd(dh1, ya, yr, gates, bg, w_att, w_ret, w_out, ff1.pair_comm())
    ff1.pair_done(r1)
    mid = scatter(dict(w_att_up=_mm_tn(oa, dya, "dw_att_up", a_is_t=True), w_ret_up=_mm_tn(yrp, dyr, "dw_ret_up"),
                       w_out=_mm_tn(merged, dh1, "dw_out")), "mid")
    (dqa, dkva, dsink, dret, dgain), got = _seq_bwd(
        qt, kv, kvt, doa, sinks, qkr, vr, gr, ret, dyrp, states, gain, rtabs, (ca, sa, sb, cr, sr, ct, st),
        _join(_join(ff2.chip_comm(), ff1.chip_comm()), mid.pair_comm()))
    ff2.chip_done(got[:1])
    ff1.chip_done(got[1:2])
    mid.pair_done(got[2:])
    d_in, r2 = _dw_in_t((dqa, dkva, dret, dgates), (True, False, False, False), xn1, mid.chip_comm())
    mid.chip_done(r2)
    win = scatter(dict(w_in=d_in), "in")
    win.pair_done(_comm_alone(win.pair_comm(), "pair_exchange_in"))
    (dx, dg1), r2 = _inproj_bwd(dqa, dkva, dret, dgates, dh1, x, g1, w_in, win.chip_comm())
    win.chip_done(r2)
    small = dict(norm_mix_gain=dg1, b_gates=db, attn_sinks=dsink, ret_gn_gain=dgain, norm_mlp_gain=dg2,
                 norm_final_gain=dg3)
    return loss, dx, small, (win, mid, ff1, ff2)


def _coords():
    return lax.axis_index("x"), lax.axis_index("y"), lax.axis_index("c")


def _flip(v, bit):
    return 1 - v if bit else v


def _xor(a, b):
    return a + b - 2 * a * b


def _gather_relay_comm(shards):
    n = len(shards)

    def parts(ins, outs, sems):
        send_sems, recv_sems, local_sems = sems
        x, y, c = _coords()
        me, sib = (x, y, c), (x, y, 1 - c)

        def ring(core):
            return ((_xor(x, core), _xor(y, 1 - core), core), (_xor(x, 1 - core), _xor(y, core), core),
                    (1 - x, 1 - y, core))

        def slot(a, blk):
            return outs[a].at[4 * blk[0] + 2 * blk[1] + blk[2]]

        def copy(a, k, blk, to, src=None):
            return pltpu.make_async_remote_copy(
                src_ref=slot(a, blk) if src is None else src, dst_ref=slot(a, blk),
                send_sem=send_sems.at[a, k], recv_sem=recv_sems.at[a, k], device_id=to, device_id_type=MESH)

        mine = [pltpu.make_async_copy(ins[a], slot(a, me), local_sems.at[a]) for a in range(n)]
        return me, sib, ring, copy, mine

    def first_copies(ins, outs, sems):
        me, sib, ring, copy, mine = parts(ins, outs, sems)
        src, dst, _ = ring(me[2])
        return mine, [copy(a, k, me, to, src=ins[a]) for a in range(n) for k, to in ((1, src), (2, dst), (0, sib))]

    def start(ins, outs, sems):
        mine, first = first_copies(ins, outs, sems)
        for cp in mine + first:
            cp.start()

    def relay(ins, outs, sems):
        me, sib, ring, copy, _ = parts(ins, outs, sems)
        src, dst, _ = ring(me[2])
        for a in range(n):
            copy(a, 1, src, me).wait_recv()
            copy(a, 3, src, dst).start()
            copy(a, 4, src, sib).start()

    def finish(ins, outs, sems):
        me, sib, ring, copy, _ = parts(ins, outs, sems)
        mine, first = first_copies(ins, outs, sems)
        src, dst, diag = ring(me[2])
        passed = [copy(a, k, src, to) for a in range(n) for k, to in ((3, dst), (4, sib))]
        for k, blk in ((2, dst), (3, diag)):
            for a in range(n):
                copy(a, k, blk, me).wait_recv()
                fwd = copy(a, k + 3, blk, sib)
                fwd.start()
                passed.append(fwd)
        s_src, s_dst, s_diag = ring(sib[2])
        for a in range(n):
            for k, blk in ((0, sib), (4, s_src), (5, s_dst), (6, s_diag)):
                copy(a, k, blk, me).wait_recv()
        for cp in first + passed:
            cp.wait_send()
        for cp in mine:
            cp.wait()

    return _Comm(list(shards), [jax.ShapeDtypeStruct((N_DEV,) + s.shape, s.dtype) for s in shards],
                 [pltpu.SemaphoreType.DMA((n, 7)), pltpu.SemaphoreType.DMA((n, 7)), pltpu.SemaphoreType.DMA((n,))],
                 start, finish, relay)


def _gather_comm(shards):
    n = len(shards)

    def parts(ins, outs, sems):
        send_sems, recv_sems, local_sems = sems
        x, y, c = _coords()
        me, sib = (x, y, c), (x, y, 1 - c)
        chips = [(1 - x, y), (x, 1 - y), (1 - x, 1 - y)]

        def slot(a, blk):
            return outs[a].at[4 * blk[0] + 2 * blk[1] + blk[2]]

        def copy(a, k, blk, to, src=None):
            return pltpu.make_async_remote_copy(
                src_ref=slot(a, blk) if src is None else src, dst_ref=slot(a, blk),
                send_sem=send_sems.at[a, k], recv_sem=recv_sems.at[a, k], device_id=to, device_id_type=MESH)

        mine = [pltpu.make_async_copy(ins[a], slot(a, me), local_sems.at[a]) for a in range(n)]
        first = []
        for a in range(n):
            first.append(copy(a, 0, me, sib, src=ins[a]))
            first += [copy(a, 1 + j, me, (*chip, c), src=ins[a]) for j, chip in enumerate(chips)]
        return me, sib, chips, c, copy, mine, first

    def start(ins, outs, sems):
        *_, mine, first = parts(ins, outs, sems)
        for cp in mine + first:
            cp.start()

    def finish(ins, outs, sems):
        me, sib, chips, c, copy, mine, first = parts(ins, outs, sems)
        passed = []
        for j, chip in enumerate(chips):
            for a in range(n):
                copy(a, 1 + j, (*chip, c), me).wait_recv()
                fwd = copy(a, 4 + j, (*chip, c), sib)
                fwd.start()
                passed.append(fwd)
        for a in range(n):
            copy(a, 0, sib, me).wait_recv()
            for j, chip in enumerate(chips):
                copy(a, 4 + j, (*chip, 1 - c), me).wait_recv()
        for cp in first + passed:
            cp.wait_send()
        for cp in mine:
            cp.wait()

    return _Comm(list(shards), [jax.ShapeDtypeStruct((N_DEV,) + s.shape, s.dtype) for s in shards],
                 [pltpu.SemaphoreType.DMA((n, 7)), pltpu.SemaphoreType.DMA((n, 7)), pltpu.SemaphoreType.DMA((n,))],
                 start, finish)


COLUMN_SHARDED = ("w_in", "w_ff1")
COLUMN_RELAID = ("w_att_up",)


def _gather(shards, names, relay=False):
    def unpack(got):
        return [jnp.transpose(g, (1, 0, 2)).reshape(g.shape[1], N_DEV * g.shape[2]) if k in COLUMN_RELAID
                else g.reshape(N_DEV * g.shape[1], g.shape[2]) for k, g in zip(names, got)]

    return (_gather_relay_comm if relay else _gather_comm)([shards[k] for k in names]), unpack


def _pair_comm(grads):
    n = len(grads)

    def copies(g, r1, sems):
        send_sems, recv_sems = sems
        x, y, c = _coords()
        return [pltpu.make_async_remote_copy(
            src_ref=g[a].at[2 * j + (1 - c)], dst_ref=r1[a].at[j], send_sem=send_sems.at[a, j],
            recv_sem=recv_sems.at[a, j], device_id=(x, y, 1 - c), device_id_type=MESH)
            for a in range(n) for j in range(4)]

    def start(g, r1, sems):
        for cp in copies(g, r1, sems):
            cp.start()

    def finish(g, r1, sems):
        for cp in copies(g, r1, sems):
            cp.wait_recv()
        for cp in copies(g, r1, sems):
            cp.wait_send()

    return _Comm(list(grads), [jax.ShapeDtypeStruct((4,) + g.shape[1:], g.dtype) for g in grads],
                 [pltpu.SemaphoreType.DMA((n, 4)), pltpu.SemaphoreType.DMA((n, 4))], start, finish)


def _small_comm(small):
    def parts(ins, outs, sems):
        (small_ref,), (small_all,) = ins, outs
        ssend, srecv, lsem = sems
        x, y, c = _coords()
        me_idx = 4 * x + 2 * y + c
        own = pltpu.make_async_copy(small_ref, small_all.at[me_idx], lsem)
        sends, recvs = [], []
        for r in range(1, N_DEV):
            px, py, pc = _flip(x, r & 4), _flip(y, r & 2), _flip(c, r & 1)
            sends.append(pltpu.make_async_remote_copy(
                src_ref=small_ref, dst_ref=small_all.at[me_idx], send_sem=ssend.at[r - 1], recv_sem=srecv.at[r - 1],
                device_id=(px, py, pc), device_id_type=MESH))
            recvs.append(pltpu.make_async_remote_copy(
                src_ref=small_ref, dst_ref=small_all.at[4 * px + 2 * py + pc], send_sem=ssend.at[r - 1],
                recv_sem=srecv.at[r - 1], device_id=(px, py, pc), device_id_type=MESH))
        return own, sends, recvs

    def start(ins, outs, sems):
        own, sends, _ = parts(ins, outs, sems)
        own.start()
        for cp in sends:
            cp.start()

    def finish(ins, outs, sems):
        own, sends, recvs = parts(ins, outs, sems)
        for cp in recvs:
            cp.wait_recv()
        for cp in sends:
            cp.wait_send()
        own.wait()

    return _Comm([small], [jax.ShapeDtypeStruct((N_DEV,) + small.shape, small.dtype)],
                 [pltpu.SemaphoreType.DMA((N_DEV - 1,)), pltpu.SemaphoreType.DMA((N_DEV - 1,)),
                  pltpu.SemaphoreType.DMA], start, finish)


def _pair_sum(grads, r1s, c_arr, tag):
    n = len(grads)
    q = 1

    def body(c_ref, *refs):
        g, r, t = refs[:n], refs[n:2 * n], refs[2 * n:]
        for a in range(n):
            t[a][...] = (g[a][...] + r[a][...]).astype(t[a].dtype)

    def blk(arr):
        return (1, arr.shape[1] // q, arr.shape[2])

    grid_spec = pltpu.PrefetchScalarGridSpec(
        num_scalar_prefetch=1, grid=(4, q),
        in_specs=[pl.BlockSpec(blk(g), lambda j, s, c_ref: (2 * j + c_ref[0], s, 0)) for g in grads]
        + [pl.BlockSpec(blk(r), lambda j, s, c_ref: (j, s, 0)) for r in r1s],
        out_specs=[pl.BlockSpec(blk(r), lambda j, s, c_ref: (j, s, 0)) for r in r1s])
    return pl.pallas_call(
        body, name="pair_sum_" + tag, grid_spec=grid_spec,
        out_shape=[jax.ShapeDtypeStruct(r.shape, RS_PAYLOAD) for r in r1s],
        compiler_params=_params(2),
    )(c_arr, *grads, *r1s)


def _chip_comm(ts):
    n = len(ts)

    def copies(t, r2, sems):
        send_sems, recv_sems = sems
        x, y, c = _coords()
        out = []
        for a in range(n):
            for r in range(1, 4):
                tx, ty = _flip(x, r & 2), _flip(y, r & 1)
                out.append(pltpu.make_async_remote_copy(
                    src_ref=t[a].at[2 * tx + ty], dst_ref=r2[a].at[r - 1], send_sem=send_sems.at[a, r - 1],
                    recv_sem=recv_sems.at[a, r - 1], device_id=(tx, ty, c), device_id_type=MESH))
        return out

    def start(t, r2, sems):
        for cp in copies(t, r2, sems):
            cp.start()

    def finish(t, r2, sems):
        for cp in copies(t, r2, sems):
            cp.wait_recv()
        for cp in copies(t, r2, sems):
            cp.wait_send()

    return _Comm(list(ts), [jax.ShapeDtypeStruct((3,) + t.shape[1:], t.dtype) for t in ts],
                 [pltpu.SemaphoreType.DMA((n, 3)), pltpu.SemaphoreType.DMA((n, 3))], start, finish)


class _Scatter:
    def __init__(self, grads, tag):
        self.names, self.tag = tuple(grads), tag
        self.stacks = [jnp.transpose(g.reshape(g.shape[0], N_DEV, g.shape[1] // N_DEV), (1, 0, 2))
                       if k in COLUMN_RELAID else g.reshape(N_DEV, g.shape[0] // N_DEV, g.shape[1])
                       for k, g in grads.items()]

    def pair_comm(self):
        return _pair_comm(self.stacks)

    def pair_done(self, r1s):
        self.r1s = list(r1s)
        c_arr = jnp.reshape(lax.axis_index("c"), (1,)).astype(jnp.int32)
        self.ts = _pair_sum(self.stacks, self.r1s, c_arr, self.tag)

    def chip_comm(self):
        return _chip_comm(self.ts)

    def chip_done(self, r2s):
        self.r2s = list(r2s)


def _adamw(w, g, m, v):
    m = ADAM_B1 * m + (1.0 - ADAM_B1) * g
    v = ADAM_B2 * v + (1.0 - ADAM_B2) * jnp.square(g)
    m_hat = m / (1.0 - ADAM_B1 ** ADAM_STEP)
    v_hat = v / (1.0 - ADAM_B2 ** ADAM_STEP)
    delta = -ADAM_LR * (m_hat / (jnp.sqrt(v_hat) + ADAM_EPS) + ADAM_WD * w)
    return delta, m, v


ADAM_STEPS = 4


def _piece_specs(stacks):
    def rows(s):
        return s.shape[1] // ADAM_STEPS
    return ([pl.BlockSpec((1, rows(s), s.shape[2]), lambda i, idx_ref: (idx_ref[0], i, 0)) for s in stacks]
            + [pl.BlockSpec((1, rows(s), s.shape[2]), lambda i, idx_ref: (idx_ref[1], i, 0)) for s in stacks]
            + [pl.BlockSpec((3, rows(s), s.shape[2]), lambda i, idx_ref: (0, i, 0)) for s in stacks])


def _piece_sum(g0, r1, r2):
    return (((g0[0] + r1[0]) + r2[0].astype(F32)) + r2[1].astype(F32)) + r2[2].astype(F32)


def _shard_sum(stacks, r1s, r2s, idx_arr):
    n = len(stacks)

    def body(idx_ref, *refs):
        g0, r1, r2, outs = (refs[k * n:(k + 1) * n] for k in range(4))
        for a in range(n):
            outs[a][...] = _piece_sum(g0[a], r1[a], r2[a])

    out_specs = [pl.BlockSpec((s.shape[1] // ADAM_STEPS, s.shape[2]), lambda i, idx_ref: (i, 0)) for s in stacks]
    grid_spec = pltpu.PrefetchScalarGridSpec(num_scalar_prefetch=1, grid=(ADAM_STEPS,),
                                             in_specs=_piece_specs(stacks), out_specs=out_specs)
    return pl.pallas_call(
        body, name="shard_sum", grid_spec=grid_spec,
        out_shape=[jax.ShapeDtypeStruct(s.shape[1:], F32) for s in stacks],
        compiler_params=_params(1),
    )(idx_arr, *stacks, *r1s, *r2s)


def _adam_big(pieces, summed, ws, ms, vs, idx_arr):
    stacks, r1s, r2s = pieces
    n_p, n = len(stacks), len(ws)

    def body(idx_ref, *refs):
        it = iter(refs)
        g0, r1, r2, gs, w, m, v = ([next(it) for _ in range(k)] for k in (n_p, n_p, n_p, n - n_p, n, n, n))
        outs = list(it)
        for a in range(n):
            g = _piece_sum(g0[a], r1[a], r2[a]) if a < n_p else gs[a - n_p][...]
            delta, nm, nv = _adamw(w[a][...], g, m[a][...], v[a][...])
            outs[4 * a][...] = g
            outs[4 * a + 1][...] = delta
            outs[4 * a + 2][...] = nm
            outs[4 * a + 3][...] = nv

    def shard_spec(w):
        return pl.BlockSpec((w.shape[0] // ADAM_STEPS, w.shape[1]), lambda i, idx_ref: (i, 0))

    in_specs = _piece_specs(stacks) + [shard_spec(w) for w in ws[n_p:]] + [shard_spec(w) for w in ws] * 3
    out_specs = [shard_spec(w) for w in ws for _ in range(4)]
    grid_spec = pltpu.PrefetchScalarGridSpec(num_scalar_prefetch=1, grid=(ADAM_STEPS,), in_specs=in_specs,
                                             out_specs=out_specs)
    return pl.pallas_call(
        body, name="adam_big", grid_spec=grid_spec,
        out_shape=[jax.ShapeDtypeStruct(w.shape, F32) for w in ws for _ in range(4)],
        compiler_params=_params(1),
    )(idx_arr, *stacks, *r1s, *r2s, *summed, *ws, *ms, *vs)


def _adam_small(small_all, w, m, v):
    def body(all_ref, w_ref, m_ref, v_ref, g_ref, d_ref, nm_ref, nv_ref):
        g = all_ref[0]
        for k in range(1, N_DEV):
            g = g + all_ref[k]
        delta, nm, nv = _adamw(w_ref[...], g, m_ref[...], v_ref[...])
        g_ref[...] = g
        d_ref[...] = delta
        nm_ref[...] = nm
        nv_ref[...] = nv

    return pl.pallas_call(
        body, name="adam_small",
        in_specs=[VMEM_FULL] * 4, out_specs=[VMEM_FULL] * 4,
        out_shape=[jax.ShapeDtypeStruct(w.shape, F32)] * 4,
    )(small_all, w, m, v)


RS_PAYLOAD = BF


def _pack_rows(rows):
    rid = lax.broadcasted_iota(jnp.int32, (8, D_MODEL), 0)
    out = jnp.zeros((8, D_MODEL), F32)
    for i, r in enumerate(rows):
        out = jnp.where(rid == i, jnp.broadcast_to(r, (8, D_MODEL)), out)
    return out


def _small_rows(norm_mix_gain, b_gates, attn_sinks, ret_gn_gain, norm_mlp_gain, norm_final_gain):
    return [norm_mix_gain, b_gates[:, :D_MODEL], b_gates[:, D_MODEL:], ret_gn_gain, norm_mlp_gain,
            norm_final_gain.reshape(1, D_MODEL), jnp.pad(attn_sinks, ((0, 0), (0, D_MODEL - ATT_HEADS)))]


def _unpack_small(p):
    return dict(norm_mix_gain=p[0:1], b_gates=jnp.concatenate([p[1:2], p[2:3]], axis=1), ret_gn_gain=p[3:4],
                norm_mlp_gain=p[4:5], norm_final_gain=p[5], attn_sinks=p[6:7, :ATT_HEADS])


WEIGHTS = ("norm_mix_gain", "w_in", "b_gates", "attn_sinks", "ret_gn_gain", "w_att_up", "w_ret_up", "w_out",
           "norm_mlp_gain", "w_ff1", "w_ff2", "norm_final_gain")
BIG = ("w_in", "w_att_up", "w_ret_up", "w_out", "w_ff1", "w_ff2")


def kernel(x, norm_mix_gain, w_in, b_gates, attn_sinks, ret_gn_gain, w_att_up, w_ret_up, w_out, norm_mlp_gain, w_ff1, w_ff2, norm_final_gain, loss_target, m_norm_mix_gain, m_w_in, m_b_gates, m_attn_sinks, m_ret_gn_gain, m_w_att_up, m_w_ret_up, m_w_out, m_norm_mlp_gain, m_w_ff1, m_w_ff2, m_norm_final_gain, v_norm_mix_gain, v_w_in, v_b_gates, v_attn_sinks, v_ret_gn_gain, v_w_att_up, v_w_ret_up, v_w_out, v_norm_mlp_gain, v_w_ff1, v_w_ff2, v_norm_final_gain):
    w = dict(norm_mix_gain=norm_mix_gain, w_in=w_in, b_gates=b_gates, attn_sinks=attn_sinks, ret_gn_gain=ret_gn_gain,
             w_att_up=w_att_up, w_ret_up=w_ret_up, w_out=w_out, norm_mlp_gain=norm_mlp_gain, w_ff1=w_ff1,
             w_ff2=w_ff2, norm_final_gain=norm_final_gain)
    m = dict(norm_mix_gain=m_norm_mix_gain, w_in=m_w_in, b_gates=m_b_gates, attn_sinks=m_attn_sinks,
             ret_gn_gain=m_ret_gn_gain, w_att_up=m_w_att_up, w_ret_up=m_w_ret_up, w_out=m_w_out,
             norm_mlp_gain=m_norm_mlp_gain, w_ff1=m_w_ff1, w_ff2=m_w_ff2, norm_final_gain=m_norm_final_gain)
    v = dict(norm_mix_gain=v_norm_mix_gain, w_in=v_w_in, b_gates=v_b_gates, attn_sinks=v_attn_sinks,
             ret_gn_gain=v_ret_gn_gain, w_att_up=v_w_att_up, w_ret_up=v_w_ret_up, w_out=v_w_out,
             norm_mlp_gain=v_norm_mlp_gain, w_ff1=v_w_ff1, w_ff2=v_w_ff2, norm_final_gain=v_norm_final_gain)

    shards = {k: (w[k][0].T if k in COLUMN_SHARDED else w[k][0]).astype(BF) for k in BIG}
    comm, unpack = _gather(shards, ("w_in",), relay=True)
    (w_in_full,) = unpack(_comm_alone(comm, "allgather_w_in"))

    loss_p, dx, small, groups = _local_step(
        x[0], loss_target[0], norm_mix_gain, b_gates, attn_sinks[0], ret_gn_gain, norm_mlp_gain,
        norm_final_gain.reshape(1, D_MODEL), w_in_full, shards)

    by_name = {}
    for grp in groups:
        for k, stack, r1, r2 in zip(grp.names, grp.stacks, grp.r1s, grp.r2s):
            by_name[k] = (stack, r1, r2)
    lane = lax.broadcasted_iota(jnp.int32, (1, D_MODEL), 1)
    sink_row = jnp.where(lane < ATT_HEADS, jnp.pad(small["attn_sinks"], ((0, 0), (0, D_MODEL - LANE))),
                         jnp.where(lane == ATT_HEADS, jnp.pad(loss_p, ((0, 0), (0, D_MODEL - LANE)), mode="edge"),
                                   0.0))
    small_pack = _pack_rows([small["norm_mix_gain"], small["b_gates"][:, :D_MODEL], small["b_gates"][:, D_MODEL:],
                             small["ret_gn_gain"], small["norm_mlp_gain"], small["norm_final_gain"], sink_row])
    xi, yi, ci = _coords()
    (small_all,) = _comm_alone(_small_comm(small_pack), "small_exchange")
    idx_arr = jnp.stack([4 * xi + 2 * yi + ci, 2 * xi + yi]).astype(jnp.int32)
    in_pieces = ["w_att_up", "w_ret_up", "w_out", "w_ff2", "w_in"]
    in_sum = ["w_ff1"]
    order = in_pieces + in_sum
    summed_t = _shard_sum(*[[by_name[k][i] for k in in_sum] for i in range(3)], idx_arr)

    def shard(tree, k):
        return tree[k][0].T if k == "w_in" else tree[k][0]

    adam_out = _adam_big([[by_name[k][i] for k in in_pieces] for i in range(3)], [g.T for g in summed_t],
                         [shard(w, k) for k in order], [shard(m, k) for k in order], [shard(v, k) for k in order],
                         idx_arr)
    big_out = [adam_out[4 * order.index(k) + i].T if k == "w_in" else adam_out[4 * order.index(k) + i]
               for k in BIG for i in range(4)]
    sm_g, sm_d, sm_m, sm_v = _adam_small(small_all, _pack_rows(_small_rows(*[w[k] for k in WEIGHTS if k not in BIG])),
                                         _pack_rows(_small_rows(*[m[k] for k in WEIGHTS if k not in BIG])),
                                         _pack_rows(_small_rows(*[v[k] for k in WEIGHTS if k not in BIG])))

    loss = sm_g[6, ATT_HEADS]
    kinds = []
    for idx, packed in enumerate((sm_g, sm_d, sm_m, sm_v)):
        out = _unpack_small(packed)
        for a, k in enumerate(BIG):
            out[k] = big_out[4 * a + idx][None]
        kinds.append(out)
    return (loss, dx[None], *[kinds[0][k] for k in WEIGHTS], *[kinds[1][k] for k in WEIGHTS],
            *[kinds[2][k] for k in WEIGHTS], *[kinds[3][k] for k in WEIGHTS])
```

```python
import functools
import math

import jax
import jax.numpy as jnp
from jax import lax
from jax.experimental import pallas as pl
from jax.experimental.pallas import tpu as pltpu

F32 = jnp.float32
BF = jnp.bfloat16
MESH = pl.DeviceIdType.MESH

D_MODEL = 1024
ATT_HEADS = 8
ATT_HEAD_DIM = 64
ATT_BLOCK = 128
ROPE_DIM = 16
ROPE_THETA = 500000.0
RET_HEADS = 4
RET_KEY_DIM = 128
RET_VAL_DIM = 256
RET_CHUNK = 128
RET_ROT_BASE = 10000.0
D_FF = 4096
NORM_EPS = 1e-6
GN_EPS = 1e-6
NEG_INF = -1e30
ATT_SCALE = ATT_HEAD_DIM ** -0.5
RET_SCALE = RET_KEY_DIM ** -0.5

C_QA, C_KA, C_VA, C_QR, C_KR, C_VR, C_GR, C_GATES, C_END = 0, 512, 640, 768, 1280, 1792, 2816, 3840, 5888

ADAM_LR = 0.001
ADAM_B1 = 0.9
ADAM_B2 = 0.999
ADAM_EPS = 1e-08
ADAM_WD = 0.01
ADAM_STEP = 10

N_DEV = 8
LANE = 128
VMEM_LIMIT = 56 * 1024 * 1024
TOKEN_TILE = 512
MLP_TOKEN_TILE = 256
TN_TOKEN_TILE = 2048
FF_CHUNKS = (0, 1024, 2048, 3072, 4096)
IN_CHUNKS_FWD = (C_QA, C_KA, C_QR, C_KR, C_VR, C_GR, C_GATES, C_END)
IN_CHUNKS_BWD = (C_QA, C_KA, C_QR, C_GATES, C_END)

RET_LOG_GAMMA = tuple(math.log1p(-(2.0 ** (-5.0 - h))) for h in range(RET_HEADS))
RET_CHUNK_DECAY = tuple(math.exp(RET_CHUNK * lg) for lg in RET_LOG_GAMMA)

VMEM_FULL = pl.BlockSpec(memory_space=pltpu.VMEM)
SMEM_FULL = pl.BlockSpec(memory_space=pltpu.SMEM)
ANY = pl.BlockSpec(memory_space=pl.ANY)


def _params(n_axes):
    return pltpu.CompilerParams(dimension_semantics=("arbitrary",) * n_axes, vmem_limit_bytes=VMEM_LIMIT)


def _nn(a, b):
    return jnp.dot(a, b, preferred_element_type=F32)


def _nt(a, b):
    return lax.dot_general(a, b, (((1,), (1,)), ((), ())), preferred_element_type=F32)


def _tn(a, b):
    return lax.dot_general(a, b, (((0,), (0,)), ((), ())), preferred_element_type=F32)


def _sigmoid(v):
    return 1.0 / (1.0 + jnp.exp(-v))


def _rows(tile, width):
    return pl.BlockSpec((tile, width), lambda i: (i, 0))


def _cols(height, tile):
    return pl.BlockSpec((height, tile), lambda i: (0, i))


class _Resident:
    def __init__(self, hbm_ref, vmem_ref, sems, bounds):
        self.hbm, self.vmem, self.sems, self.bounds = hbm_ref, vmem_ref, sems, bounds

    @staticmethod
    def scratch(w, bounds):
        return [pltpu.VMEM(w.shape, w.dtype), pltpu.SemaphoreType.DMA((len(bounds) - 1,))]

    @staticmethod
    def load(*weights):
        @pl.when(pl.program_id(0) == 0)
        def _():
            copies = [w._copy(c) for w in weights for c in range(len(w.bounds) - 1)]
            for cp in copies:
                cp.start()
            for cp in copies:
                cp.wait()

    def _rows(self, c):
        return pl.ds(self.bounds[c], self.bounds[c + 1] - self.bounds[c])

    def _copy(self, c):
        return pltpu.make_async_copy(self.hbm.at[self._rows(c)], self.vmem.at[self._rows(c)], self.sems.at[c])

    def chunk(self, c):
        return self.vmem.at[self._rows(c)]


class _Comm:
    def __init__(self, inputs, out_shapes, scratch, start, finish, relay=None):
        self.inputs, self.out_shapes, self.scratch = inputs, out_shapes, scratch
        self.start, self.finish, self.relay = start, finish, relay


RELAY_AT = 0.7


def _join(a, b):
    na_in, na_out, na_sem = len(a.inputs), len(a.out_shapes), len(a.scratch)

    def both(name):
        def run(ins, outs, sems):
            for part, args in ((a, (ins[:na_in], outs[:na_out], sems[:na_sem])),
                               (b, (ins[na_in:], outs[na_out:], sems[na_sem:]))):
                if getattr(part, name) is not None:
                    getattr(part, name)(*args)
        return run

    return _Comm(list(a.inputs) + list(b.inputs), list(a.out_shapes) + list(b.out_shapes),
                 list(a.scratch) + list(b.scratch), both("start"), both("finish"),
                 both("relay") if a.relay or b.relay else None)


def _call(body, args, comm=None, *, name, grid, in_specs, out_specs, out_shape, scratch_shapes=()):
    params = _params(len(grid))
    if comm is None:
        return pl.pallas_call(body, name=name, grid=grid, in_specs=in_specs, out_specs=out_specs, out_shape=out_shape,
                              scratch_shapes=scratch_shapes, compiler_params=params)(*args), ()
    single = not isinstance(out_specs, (list, tuple))
    out_specs_l = [out_specs] if single else list(out_specs)
    out_shape_l = [out_shape] if single else list(out_shape)
    n_in, n_out, n_scr = len(in_specs), len(out_specs_l), len(scratch_shapes)
    n_cin, n_cout = len(comm.inputs), len(comm.out_shapes)

    def hosted(*refs):
        it = iter(refs)
        ins, cin, outs, cout, scr = ([next(it) for _ in range(k)] for k in (n_in, n_cin, n_out, n_cout, n_scr))
        sems = list(it)
        ids = [pl.program_id(k) for k in range(len(grid))]
        first = functools.reduce(jnp.logical_and, [i == 0 for i in ids])
        last = functools.reduce(jnp.logical_and, [i == g - 1 for i, g in zip(ids, grid)])

        @pl.when(first)
        def _():
            comm.start(cin, cout, sems)

        if comm.relay is not None:
            at = [int(grid[0] * RELAY_AT)] + [0] * (len(grid) - 1)

            @pl.when(functools.reduce(jnp.logical_and, [i == v for i, v in zip(ids, at)]))
            def _():
                comm.relay(cin, cout, sems)

        body(*ins, *outs, *scr)

        @pl.when(last)
        def _():
            comm.finish(cin, cout, sems)

    res = pl.pallas_call(
        hosted, name=name, grid=grid, in_specs=list(in_specs) + [ANY] * n_cin,
        out_specs=out_specs_l + [ANY] * n_cout, out_shape=out_shape_l + list(comm.out_shapes),
        scratch_shapes=list(scratch_shapes) + list(comm.scratch), compiler_params=params)(*args, *comm.inputs)
    return (res[0] if single else res[:n_out]), res[n_out:]


def _comm_alone(comm, name):
    n_cin, n_cout = len(comm.inputs), len(comm.out_shapes)

    def body(*refs):
        cin, cout, sems = refs[:n_cin], refs[n_cin:n_cin + n_cout], refs[n_cin + n_cout:]
        comm.start(cin, cout, sems)
        if comm.relay is not None:
            comm.relay(cin, cout, sems)
        comm.finish(cin, cout, sems)

    return pl.pallas_call(body, name=name, in_specs=[ANY] * n_cin, out_specs=[ANY] * n_cout,
                          out_shape=list(comm.out_shapes), scratch_shapes=list(comm.scratch))(*comm.inputs)


def _slabs(v, fn):
    return jnp.concatenate([fn(v[:, LANE * j:LANE * (j + 1)]) for j in range(v.shape[1] // LANE)], axis=1)


def _rope_att(v, ca, sa, sb):
    return _slabs(v, lambda t: t * ca + pltpu.roll(t, LANE - 8, 1) * sa + pltpu.roll(t, 8, 1) * sb)


def _rope_att_t(v, ca, sa, sb):
    return _slabs(v, lambda t: t * ca + pltpu.roll(t * sa, 8, 1) + pltpu.roll(t * sb, LANE - 8, 1))


def _rope_att_rows(v, ct, st, sign):
    parts = []
    for h in range(v.shape[0] // ATT_HEAD_DIM):
        r0 = ATT_HEAD_DIM * h
        x1, x2 = v[r0:r0 + 8], v[r0 + 8:r0 + 16]
        parts += [x1 * ct - sign * (x2 * st), x2 * ct + sign * (x1 * st), v[r0 + 16:r0 + ATT_HEAD_DIM]]
    return jnp.concatenate(parts, axis=0)


def _rope_ret(v, cr, sr):
    return _slabs(v, lambda t: t * cr + pltpu.roll(t, 64, 1) * sr)


def _rope_ret_t(v, cr, sr):
    return _slabs(v, lambda t: t * cr + pltpu.roll(t * sr, 64, 1))


def _rope_lane_tables():
    def inv(dim, theta):
        return theta ** (-jnp.arange(0, dim, 2, dtype=F32) / dim)

    inv_a, inv_r = inv(ROPE_DIM, ROPE_THETA), inv(RET_KEY_DIM, RET_ROT_BASE)
    half = ROPE_DIM // 2
    zeros = jnp.zeros((ATT_HEAD_DIM - ROPE_DIM,), F32)
    freq64 = jnp.concatenate([inv_a, inv_a, zeros])
    lo64 = jnp.concatenate([-jnp.ones((half,), F32), jnp.zeros((half,), F32), zeros])
    hi64 = jnp.concatenate([jnp.zeros((half,), F32), jnp.ones((half,), F32), zeros])
    sign_r = jnp.concatenate([-jnp.ones((64,), F32), jnp.ones((64,), F32)])
    rows = [jnp.tile(freq64, 2), jnp.tile(lo64, 2), jnp.tile(hi64, 2), jnp.tile(inv_r, 2), sign_r]
    lanes = jnp.stack(rows + [jnp.zeros((LANE,), F32)] * (8 - len(rows)))
    return lanes, jnp.broadcast_to(inv_a[:, None], (half, LANE))


def _rope_tables(seq):
    nb = seq // ATT_BLOCK
    lanes, freq_rows = _rope_lane_tables()
    base = jnp.arange(nb, dtype=F32) * ATT_BLOCK
    off = jnp.arange(ATT_BLOCK, dtype=F32)

    def cos_sin(freq, tokens_last):
        if tokens_last:
            a, b = freq[:, :, None] * base[None, :, None], freq[:, :, None] * off[None, None, :]
        else:
            a, b = base[:, None, None] * freq[None], off[None, :, None] * freq[None]
        ca_, sa_, cb_, sb_ = jnp.cos(a), jnp.sin(a), jnp.cos(b), jnp.sin(b)
        return ca_ * cb_ - sa_ * sb_, sa_ * cb_ + ca_ * sb_

    cos_a, sin_a = cos_sin(lanes[0:1], False)
    cos_r, sin_r = cos_sin(lanes[3:4], False)
    cos_t, sin_t = cos_sin(freq_rows[:, :1], True)
    flat = lambda t: t.reshape(seq, LANE)
    return (flat(cos_a), flat(sin_a * lanes[1]), flat(sin_a * lanes[2]), flat(cos_r), flat(sin_r * lanes[4]),
            cos_t.reshape(8, seq), sin_t.reshape(8, seq))


def _ret_tables():
    c = RET_CHUNK
    lg = jnp.asarray(RET_LOG_GAMMA, F32)
    idx = jnp.arange(c, dtype=F32)
    diff = idx[:, None] - idx[None, :]
    intra = jnp.where(diff >= 0, jnp.exp(jnp.maximum(diff, 0.0) * lg[:, None, None]), 0.0)
    qd = jnp.exp((idx + 1.0)[None, :] * lg[:, None])[..., None]
    kd = jnp.exp((c - 1.0 - idx)[None, :] * lg[:, None])[..., None]
    return intra, jnp.broadcast_to(qd, (RET_HEADS, c, RET_KEY_DIM)), jnp.broadcast_to(kd, (RET_HEADS, c, RET_KEY_DIM))


def _inproj_fwd(x, g1, w_in, tabs, comm=None):
    seq = x.shape[0]
    tm = min(TOKEN_TILE, seq)

    def body(x_ref, g_ref, w_hbm, ca_ref, sa_ref, sb_ref, cr_ref, sr_ref, ct_ref, st_ref,
             xn_ref, qt_ref, kv_ref, kvt_ref, qkr_ref, vr_ref, gr_ref, gates_ref, w_vmem, w_sems):
        w = _Resident(w_hbm, w_vmem, w_sems, IN_CHUNKS_FWD)
        _Resident.load(w)
        ca, sa, sb, cr, sr, ct, st = (ref[...] for ref in (ca_ref, sa_ref, sb_ref, cr_ref, sr_ref, ct_ref, st_ref))
        xf = x_ref[...]
        r = lax.rsqrt(jnp.mean(xf * xf, axis=-1, keepdims=True) + NORM_EPS)
        xn = (xf * r * g_ref[...]).astype(BF)
        xn_ref[...] = xn
        qt_ref[...] = _rope_att_rows(_nt(w.chunk(0)[...], xn), ct, st, 1.0).astype(BF)
        w_kv = w.chunk(1)
        kvt = _nt(w_kv[...], xn)
        kvt_ref[:LANE, :] = _rope_att_rows(kvt[:LANE], ct, st, 1.0).astype(BF)
        kvt_ref[LANE:, :] = kvt[LANE:].astype(BF)
        kvn = _nt(xn, w_kv[...])
        kv_ref[:, :LANE] = _rope_att(kvn[:, :LANE], ca, sa, sb).astype(BF)
        kv_ref[:, LANE:] = kvn[:, LANE:].astype(BF)
        qr = _rope_ret(_nt(xn, w.chunk(2)[...]), cr, sr)
        qkr_ref[:, :512] = qr.astype(BF)
        kr = _rope_ret(_nt(xn, w.chunk(3)[...]), cr, sr) * RET_SCALE
        qkr_ref[:, 512:] = kr.astype(BF)
        vr_ref[...] = _nt(xn, w.chunk(4)[...]).astype(BF)
        gr_ref[...] = _nt(xn, w.chunk(5)[...]).astype(BF)
        gates_ref[...] = _nt(xn, w.chunk(6)[...]).astype(BF)

    tab = _rows(tm, LANE)
    tab_t = _cols(8, tm)
    return _call(
        body, (x, g1, w_in, *tabs), comm, name="inproj_fwd", grid=(seq // tm,),
        in_specs=[_rows(tm, D_MODEL), VMEM_FULL, ANY, tab, tab, tab, tab, tab, tab_t, tab_t],
        out_specs=[_rows(tm, D_MODEL), _cols(512, tm), _rows(tm, 256), _cols(256, tm), _rows(tm, 1024),
                   _rows(tm, 1024), _rows(tm, 1024), _rows(tm, 2048)],
        out_shape=[jax.ShapeDtypeStruct(s, BF) for s in ((seq, D_MODEL), (512, seq), (seq, 256), (256, seq),
                                                         (seq, 1024), (seq, 1024), (seq, 1024), (seq, 2048))],
        scratch_shapes=_Resident.scratch(w_in, IN_CHUNKS_FWD))


ATT_Q_PER_KV = ATT_HEADS // 2


def _att_group(ref, kv):
    d = ATT_HEAD_DIM
    g = jnp.concatenate([ref[d * (ATT_Q_PER_KV * kv + j):d * (ATT_Q_PER_KV * kv + j + 1), :]
                         for j in range(ATT_Q_PER_KV)], axis=1)
    z = jnp.zeros_like(g)
    return jnp.concatenate([g, z] if kv == 0 else [z, g], axis=0)


def _att_probs(kwin, qpad, sink_ref, kv, i):
    c = ATT_BLOCK
    kj = lax.broadcasted_iota(jnp.int32, (2 * c, c), 0)
    qi = lax.broadcasted_iota(jnp.int32, (2 * c, c), 1)
    allowed = (kj > qi) & (kj <= qi + c) & ((kj >= c) | (i > 0))
    allowed = jnp.concatenate([allowed] * ATT_Q_PER_KV, axis=1)
    s = jnp.where(allowed, _nn(kwin, qpad) * ATT_SCALE, NEG_INF)
    sk = jnp.concatenate([jnp.full((1, c), sink_ref[ATT_Q_PER_KV * kv + j], F32) for j in range(ATT_Q_PER_KV)], axis=1)
    m = jnp.maximum(jnp.max(s, axis=0, keepdims=True), sk)
    pe = jnp.exp(s - m)
    psink = jnp.exp(sk - m)
    inv = 1.0 / (jnp.sum(pe, axis=0, keepdims=True) + psink)
    return pe * inv, psink * inv


def _seq_fwd(qt, kv, kvt, sinks, qkr, vr, gr, gain, rtabs, comm=None):
    seq = kv.shape[0]
    c = ATT_BLOCK
    assert c == RET_CHUNK
    d = ATT_HEAD_DIM
    nb = seq // c
    dk, dv = RET_KEY_DIM, RET_VAL_DIM

    def body(sink_ref, qt_ref, kvc_ref, kvp_ref, kvtc_ref, kvtp_ref, qk_ref, v_ref, g_ref, gain_ref, intra_ref,
             qd_ref, kd_ref, ot_ref, ret_ref, yrp_ref, st_ref, state):
        i = pl.program_id(0)
        kwin = jnp.concatenate([kvp_ref[:, :LANE], kvc_ref[:, :LANE]], axis=0)
        vtwin = jnp.concatenate([kvtp_ref[LANE:, :], kvtc_ref[LANE:, :]], axis=1)
        for g in (0, 1):
            prob, _ = _att_probs(kwin, _att_group(qt_ref, g), sink_ref, g, i)
            out = _nn(vtwin[d * g:d * (g + 1), :], prob.astype(BF))
            for j in range(ATT_Q_PER_KV):
                h = ATT_Q_PER_KV * g + j
                ot_ref[d * h:d * (h + 1), :] = out[:, c * j:c * (j + 1)].astype(BF)

        @pl.when(i == 0)
        def _():
            state[...] = jnp.zeros_like(state)

        for h in range(RET_HEADS):
            qh = qk_ref[:, dk * h:dk * (h + 1)]
            kh = qk_ref[:, 512 + dk * h:512 + dk * (h + 1)]
            vh = v_ref[:, dv * h:dv * (h + 1)]
            sh = state[h]
            shb = sh.astype(BF)
            st_ref[0, h] = shb
            att = _nt(qh, kh) * intra_ref[h]
            inner = _nn(att.astype(BF), vh)
            cross = _nn((qh.astype(F32) * qd_ref[h]).astype(BF), shb)
            out = inner + cross
            state[h] = sh * RET_CHUNK_DECAY[h] + _tn((kh.astype(F32) * kd_ref[h]).astype(BF), vh)
            ret_ref[:, dv * h:dv * (h + 1)] = out
            mu = jnp.mean(out, axis=-1, keepdims=True)
            dev = out - mu
            var = jnp.mean(dev * dev, axis=-1, keepdims=True)
            y = dev * lax.rsqrt(var + GN_EPS) * gain_ref[:, dv * h:dv * (h + 1)]
            g = g_ref[:, dv * h:dv * (h + 1)].astype(F32)
            yrp_ref[:, dv * h:dv * (h + 1)] = (g * _sigmoid(g) * y).astype(BF)

    prev = lambda i: jnp.maximum(i - 1, 0)
    return _call(
        body, (sinks, qt, kv, kv, kvt, kvt, qkr, vr, gr, gain, *rtabs), comm, name="seq_fwd", grid=(nb,),
        in_specs=[SMEM_FULL, _cols(512, c), _rows(c, 256), pl.BlockSpec((c, 256), lambda i: (prev(i), 0)),
                  _cols(256, c), pl.BlockSpec((256, c), lambda i: (0, prev(i))),
                  _rows(c, 1024), _rows(c, 1024), _rows(c, 1024), VMEM_FULL, VMEM_FULL, VMEM_FULL, VMEM_FULL],
        out_specs=[_cols(512, c), _rows(c, 1024), _rows(c, 1024),
                   pl.BlockSpec((1, RET_HEADS, dk, dv), lambda i: (i, 0, 0, 0))],
        out_shape=[jax.ShapeDtypeStruct((512, seq), BF), jax.ShapeDtypeStruct((seq, 1024), F32),
                   jax.ShapeDtypeStruct((seq, 1024), BF), jax.ShapeDtypeStruct((nb, RET_HEADS, dk, dv), BF)],
        scratch_shapes=[pltpu.VMEM((RET_HEADS, dk, dv), F32)])


def _mix_fwd(oat, yrp, gates, bg, x, w_att, w_ret, w_out):
    seq = x.shape[0]
    tm = min(TOKEN_TILE, seq)

    def body(oat_ref, yrp_ref, gates_ref, bg_ref, x_ref, wa_ref, wr_ref, wo_ref, ya_ref, yr_ref, mg_ref, h1_ref):
        ya = _tn(oat_ref[...], wa_ref[...])
        yr = _nn(yrp_ref[...], wr_ref[...])
        gt = _sigmoid(gates_ref[...].astype(F32) + bg_ref[...])
        merged = (gt[:, :D_MODEL] * ya + gt[:, D_MODEL:] * yr).astype(BF)
        ya_ref[...] = ya.astype(BF)
        yr_ref[...] = yr.astype(BF)
        mg_ref[...] = merged
        h1_ref[...] = x_ref[...] + _nn(merged, wo_ref[...])

    return pl.pallas_call(
        body, name="mix_fwd", grid=(seq // tm,),
        in_specs=[_cols(512, tm), _rows(tm, 1024), _rows(tm, 2048), VMEM_FULL, _rows(tm, D_MODEL),
                  VMEM_FULL, VMEM_FULL, VMEM_FULL],
        out_specs=[_rows(tm, D_MODEL)] * 4,
        out_shape=[jax.ShapeDtypeStruct((seq, D_MODEL), BF)] * 3 + [jax.ShapeDtypeStruct((seq, D_MODEL), F32)],
        compiler_params=_params(1),
    )(oat, yrp, gates, bg, x, w_att, w_ret, w_out)


def _mlp_loss_step(h1, g2, g3, target, w_ff1, w_ff2):
    seq = h1.shape[0]
    tm = min(MLP_TOKEN_TILE, seq)
    n_chunks = len(FF_CHUNKS) - 1

    def body(h1_ref, g2_ref, g3_ref, t_ref, w1_hbm, w2_hbm,
             xn2_ref, hdn_ref, dh2_ref, du_ref, dh1_ref, loss_ref, dg3_ref, dg2_ref,
             w1_vmem, w1_sems, w2_vmem, w2_sems, relu_u):
        w1 = _Resident(w1_hbm, w1_vmem, w1_sems, FF_CHUNKS)
        w2 = _Resident(w2_hbm, w2_vmem, w2_sems, FF_CHUNKS)
        _Resident.load(w1, w2)

        @pl.when(pl.program_id(0) == 0)
        def _():
            loss_ref[...] = jnp.zeros_like(loss_ref)
            dg3_ref[...] = jnp.zeros_like(dg3_ref)
            dg2_ref[...] = jnp.zeros_like(dg2_ref)

        h1v = h1_ref[...]
        r2 = lax.rsqrt(jnp.mean(h1v * h1v, axis=-1, keepdims=True) + NORM_EPS)
        xn2 = (h1v * r2 * g2_ref[...]).astype(BF)
        xn2_ref[...] = xn2
        h2 = h1v
        for c in range(n_chunks):
            cols = slice(FF_CHUNKS[c], FF_CHUNKS[c + 1])
            a = jnp.maximum(_nt(xn2, w1.chunk(c)[...]), 0.0)
            relu_u[:, cols] = a.astype(BF)
            hdn = jnp.square(a).astype(BF)
            hdn_ref[:, cols] = hdn
            h2 = h2 + _nn(hdn, w2.chunk(c)[...])
        r3 = lax.rsqrt(jnp.mean(h2 * h2, axis=-1, keepdims=True) + NORM_EPS)
        hn = h2 * r3
        err = hn * g3_ref[...] - t_ref[...]
        loss_ref[...] += jnp.sum(err * err) * (0.5 / D_MODEL)
        dy = err * (1.0 / D_MODEL)
        dg3_ref[...] += jnp.sum(dy * hn, axis=0, keepdims=True)
        z = dy * g3_ref[...]
        dh2 = r3 * (z - hn * jnp.mean(z * hn, axis=-1, keepdims=True))
        dh2b = dh2.astype(BF)
        dh2_ref[...] = dh2b
        dxn2 = jnp.zeros_like(dh2)
        for c in range(n_chunks):
            cols = slice(FF_CHUNKS[c], FF_CHUNKS[c + 1])
            du = (_nt(dh2b, w2.chunk(c)[...]) * (2.0 * relu_u[:, cols].astype(F32))).astype(BF)
            du_ref[:, cols] = du
            dxn2 = dxn2 + _nn(du, w1.chunk(c)[...])
        dnorm, dg = _rms_bwd(dxn2, h1v, g2_ref[...])
        dh1_ref[...] = dh2 + dnorm
        dg2_ref[...] += dg

    gain_acc = pl.BlockSpec((1, D_MODEL), lambda i: (0, 0))
    return pl.pallas_call(
        body, name="mlp_loss_step", grid=(seq // tm,),
        in_specs=[_rows(tm, D_MODEL), VMEM_FULL, VMEM_FULL, _rows(tm, D_MODEL), ANY, ANY],
        out_specs=[_rows(tm, D_MODEL), _rows(tm, D_FF), _rows(tm, D_MODEL), _rows(tm, D_FF), _rows(tm, D_MODEL),
                   pl.BlockSpec((1, LANE), lambda i: (0, 0)), gain_acc, gain_acc],
        out_shape=[jax.ShapeDtypeStruct((seq, D_MODEL), BF), jax.ShapeDtypeStruct((seq, D_FF), BF),
                   jax.ShapeDtypeStruct((seq, D_MODEL), BF), jax.ShapeDtypeStruct((seq, D_FF), BF),
                   jax.ShapeDtypeStruct((seq, D_MODEL), F32), jax.ShapeDtypeStruct((1, LANE), F32),
                   jax.ShapeDtypeStruct((1, D_MODEL), F32), jax.ShapeDtypeStruct((1, D_MODEL), F32)],
        scratch_shapes=(_Resident.scratch(w_ff1, FF_CHUNKS) + _Resident.scratch(w_ff2, FF_CHUNKS)
                        + [pltpu.VMEM((tm, D_FF), BF)]),
        compiler_params=_params(1),
    )(h1, g2, g3, target, w_ff1, w_ff2)


def _rms_bwd(dxn, xin, gain):
    r = lax.rsqrt(jnp.mean(xin * xin, axis=-1, keepdims=True) + NORM_EPS)
    xhat = xin * r
    z = dxn * gain
    dxin = r * (z - xhat * jnp.mean(z * xhat, axis=-1, keepdims=True))
    return dxin, jnp.sum(dxn * xhat, axis=0, keepdims=True)


def _mm_tn(a, b, name, a_is_t=False, comm=None):
    kdim, seq = a.shape if a_is_t else a.shape[::-1]
    ndim = b.shape[1]
    ts = min(TN_TOKEN_TILE, seq)
    tk = min(kdim, 1024)
    tn = min(ndim, 1024)
    n_steps = seq // ts

    def body(a_ref, b_ref, o_ref):
        @pl.when(pl.program_id(2) == 0)
        def _():
            o_ref[...] = jnp.zeros_like(o_ref)

        mm = _nn if a_is_t else _tn
        o_ref[...] += mm(a_ref[...].astype(BF), b_ref[...].astype(BF))

    a_spec = (pl.BlockSpec((tk, ts), lambda k, n, s: (k, s)) if a_is_t
              else pl.BlockSpec((ts, tk), lambda k, n, s: (s, k)))
    res = _call(
        body, (a, b), comm, name=name, grid=(kdim // tk, ndim // tn, n_steps),
        in_specs=[a_spec, pl.BlockSpec((ts, tn), lambda k, n, s: (s, n))],
        out_specs=pl.BlockSpec((tk, tn), lambda k, n, s: (k, n)),
        out_shape=jax.ShapeDtypeStruct((kdim, ndim), F32))
    return res[0] if comm is None else res


def _dw_in_t(pieces, is_t, xn1, comm=None):
    seq = xn1.shape[0]
    ts = seq
    tr = 256
    n_blk = [(p.shape[0] if t else p.shape[1]) // tr for p, t in zip(pieces, is_t)]
    offs = [sum(n_blk[:i]) for i in range(len(pieces) + 1)]
    n_p = len(pieces)

    def body(*refs):
        piece_refs, xn_ref, o_ref = refs[:n_p], refs[n_p], refs[n_p + 1]
        r, s = pl.program_id(0), pl.program_id(1)

        @pl.when(s == 0)
        def _():
            o_ref[...] = jnp.zeros_like(o_ref)

        for p in range(n_p):
            @pl.when((r >= offs[p]) & (r < offs[p + 1]))
            def _(p=p):
                mm = _nn if is_t[p] else _tn
                o_ref[...] += mm(piece_refs[p][...], xn_ref[pl.ds(pl.multiple_of(s * ts, ts), ts), :])

    def piece_spec(p):
        def index(r, s):
            inside = (r >= offs[p]) & (r < offs[p + 1])
            tok, blk = jnp.where(inside, s, 0), jnp.clip(r - offs[p], 0, n_blk[p] - 1)
            return (blk, tok) if is_t[p] else (tok, blk)
        return pl.BlockSpec((tr, ts) if is_t[p] else (ts, tr), index)

    return _call(
        body, (*pieces, xn1), comm, name="dw_in", grid=(offs[-1], seq // ts),
        in_specs=[piece_spec(p) for p in range(n_p)] + [VMEM_FULL],
        out_specs=pl.BlockSpec((tr, D_MODEL), lambda r, s: (r, 0)),
        out_shape=jax.ShapeDtypeStruct((offs[-1] * tr, D_MODEL), F32))


def _mix_bwd(dh1, ya, yr, gates, bg, w_att, w_ret, w_out, comm=None):
    seq = dh1.shape[0]
    tm = min(TOKEN_TILE, seq)

    def body(dh1_ref, ya_ref, yr_ref, gates_ref, bg_ref, wa_ref, wr_ref, wo_ref,
             dya_ref, dyr_ref, dgates_ref, doa_ref, dyrp_ref, db_ref):
        @pl.when(pl.program_id(0) == 0)
        def _():
            db_ref[...] = jnp.zeros_like(db_ref)

        dm = _nt(dh1_ref[...].astype(BF), wo_ref[...])
        gt = _sigmoid(gates_ref[...].astype(F32) + bg_ref[...])
        ga, gr = gt[:, :D_MODEL], gt[:, D_MODEL:]
        dya = (dm * ga).astype(BF)
        dyr = (dm * gr).astype(BF)
        dya_ref[...] = dya
        dyr_ref[...] = dyr
        dga = dm * ya_ref[...].astype(F32) * ga * (1.0 - ga)
        dgr = dm * yr_ref[...].astype(F32) * gr * (1.0 - gr)
        dgates_ref[:, :D_MODEL] = dga.astype(BF)
        dgates_ref[:, D_MODEL:] = dgr.astype(BF)
        db_ref[:, :D_MODEL] += jnp.sum(dga, axis=0, keepdims=True)
        db_ref[:, D_MODEL:] += jnp.sum(dgr, axis=0, keepdims=True)
        doa_ref[...] = _nt(wa_ref[...], dya).astype(BF)
        dyrp_ref[...] = _nt(dyr, wr_ref[...]).astype(BF)

    return _call(
        body, (dh1, ya, yr, gates, bg, w_att, w_ret, w_out), comm, name="mix_bwd", grid=(seq // tm,),
        in_specs=[_rows(tm, D_MODEL), _rows(tm, D_MODEL), _rows(tm, D_MODEL), _rows(tm, 2048), VMEM_FULL,
                  VMEM_FULL, VMEM_FULL, VMEM_FULL],
        out_specs=[_rows(tm, D_MODEL), _rows(tm, D_MODEL), _rows(tm, 2048), _cols(512, tm), _rows(tm, 1024),
                   pl.BlockSpec((1, 2048), lambda i: (0, 0))],
        out_shape=[jax.ShapeDtypeStruct((seq, D_MODEL), BF), jax.ShapeDtypeStruct((seq, D_MODEL), BF),
                   jax.ShapeDtypeStruct((seq, 2048), BF), jax.ShapeDtypeStruct((512, seq), BF),
                   jax.ShapeDtypeStruct((seq, 1024), BF), jax.ShapeDtypeStruct((1, 2048), F32)])


def _seq_bwd(qt, kv, kvt, dot, sinks, qkr, vr, gr, ret, dyrp, states, gain, rtabs, tabs, comm=None):
    seq = kv.shape[0]
    c = ATT_BLOCK
    assert c == RET_CHUNK
    d = ATT_HEAD_DIM
    nb = seq // c
    dk, dv = RET_KEY_DIM, RET_VAL_DIM
    ca, sa, sb, cr, sr, ct, st = tabs

    def body(sink_ref, qt_ref, kvc_ref, kvp_ref, kvtc_ref, kvtp_ref, dot_ref, ca_ref, sa_ref, sb_ref, ct_ref, st_ref,
             qk_ref, v_ref, g_ref, ret_ref, dyp_ref, st8_ref, gain_ref, intra_ref, qd_ref, kd_ref, cr_ref, sr_ref,
             dqt_ref, dkv_ref, dsink_ref, dall_ref, dgain_ref, carry, dstate):
        step = pl.program_id(0)
        i = nb - 1 - step

        @pl.when(step == 0)
        def _():
            carry[...] = jnp.zeros_like(carry)
            dsink_ref[...] = jnp.zeros_like(dsink_ref)
            dstate[...] = jnp.zeros_like(dstate)
            dgain_ref[...] = jnp.zeros_like(dgain_ref)

        kwin = jnp.concatenate([kvp_ref[:, :LANE], kvc_ref[:, :LANE]], axis=0)
        vwin = jnp.concatenate([kvp_ref[:, LANE:], kvc_ref[:, LANE:]], axis=0)
        ktwin = jnp.concatenate([kvtp_ref[:LANE, :], kvtc_ref[:LANE, :]], axis=1)
        ctv, stv = ct_ref[...], st_ref[...]
        lane1 = lax.broadcasted_iota(jnp.int32, (1, LANE), 1)
        dkw = jnp.zeros((2 * c, LANE), F32)
        dvw = jnp.zeros((2 * c, LANE), F32)
        dsink = jnp.zeros((1, LANE), F32)
        for g in (0, 1):
            qpad, dopad = _att_group(qt_ref, g), _att_group(dot_ref, g)
            prob, psink = _att_probs(kwin, qpad, sink_ref, g, i)
            dprob = _nn(vwin, dopad)
            drow = jnp.sum(dprob * prob, axis=0, keepdims=True)
            ds = (prob * (dprob - drow) * ATT_SCALE).astype(BF)
            dqg = _nn(ktwin[d * g:d * (g + 1), :], ds)
            dkw = dkw + _nt(ds, qpad)
            dvw = dvw + _nt(prob.astype(BF), dopad)
            dsink_lanes = psink * drow
            for j in range(ATT_Q_PER_KV):
                h = ATT_Q_PER_KV * g + j
                dqt_ref[d * h:d * (h + 1), :] = _rope_att_rows(dqg[:, c * j:c * (j + 1)], ctv, stv, -1.0).astype(BF)
                dsink = dsink + jnp.where(lane1 == h, -jnp.sum(dsink_lanes[:, c * j:c * (j + 1)]), 0.0)
        dsink_ref[...] += dsink
        total = carry[...] + jnp.concatenate([dkw[c:], dvw[c:]], axis=1)
        dkv_ref[:, :LANE] = _rope_att_t(total[:, :LANE], ca_ref[...], sa_ref[...], sb_ref[...]).astype(BF)
        dkv_ref[:, LANE:] = total[:, LANE:].astype(BF)
        carry[...] = jnp.concatenate([dkw[:c], dvw[:c]], axis=1)

        crv, srv = cr_ref[...], sr_ref[...]
        for h in range(RET_HEADS):
            vs = slice(dv * h, dv * (h + 1))
            qh = qk_ref[:, dk * h:dk * (h + 1)]
            kh = qk_ref[:, 512 + dk * h:512 + dk * (h + 1)]
            vh = v_ref[:, vs]
            out = ret_ref[:, vs]
            g = g_ref[:, vs].astype(F32)
            dyp = dyp_ref[:, vs].astype(F32)
            gain_h = gain_ref[:, vs]
            mu = jnp.mean(out, axis=-1, keepdims=True)
            dev = out - mu
            rstd = lax.rsqrt(jnp.mean(dev * dev, axis=-1, keepdims=True) + GN_EPS)
            yn = dev * rstd
            sg = _sigmoid(g)
            dg = dyp * (yn * gain_h) * (sg * (1.0 + g * (1.0 - sg)))
            dy = dyp * (g * sg)
            dgain_ref[:, vs] += jnp.sum(dy * yn, axis=0, keepdims=True)
            dyn = dy * gain_h
            dout = rstd * (dyn - jnp.mean(dyn, axis=-1, keepdims=True)
                           - yn * jnp.mean(dyn * yn, axis=-1, keepdims=True))
            doutb = dout.astype(BF)
            sc = st8_ref[0, h]
            dsp = dstate[h]
            dspb = dsp.astype(BF)
            intra, qdv, kdv = intra_ref[h], qd_ref[h], kd_ref[h]
            att = _nt(qh, kh) * intra
            dab = (_nt(doutb, vh) * intra).astype(BF)
            qdec = (qh.astype(F32) * qdv).astype(BF)
            kdec = (kh.astype(F32) * kdv).astype(BF)
            dq = _nn(dab, kh) + _nt(doutb, sc) * qdv
            dkk = _tn(dab, qh) + _nt(vh, dspb) * kdv
            dvv = _tn(att.astype(BF), doutb) + _nn(kdec, dspb)
            dstate[h] = dsp * RET_CHUNK_DECAY[h] + _tn(qdec, doutb)
            dall_ref[:, dk * h:dk * (h + 1)] = _rope_ret_t(dq, crv, srv).astype(BF)
            dall_ref[:, 512 + dk * h:512 + dk * (h + 1)] = (_rope_ret_t(dkk, crv, srv) * RET_SCALE).astype(BF)
            dall_ref[:, 1024 + dv * h:1024 + dv * (h + 1)] = dvv.astype(BF)
            dall_ref[:, 2048 + dv * h:2048 + dv * (h + 1)] = dg.astype(BF)

    cur = lambda s: nb - 1 - s
    prev = lambda s: jnp.maximum(nb - 2 - s, 0)
    rows = lambda w: pl.BlockSpec((c, w), lambda s: (cur(s), 0))
    cols = lambda h: pl.BlockSpec((h, c), lambda s: (0, cur(s)))
    acc = lambda w: pl.BlockSpec((1, w), lambda s: (0, 0))
    return _call(
        body, (sinks, qt, kv, kv, kvt, kvt, dot, ca, sa, sb, ct, st, qkr, vr, gr, ret, dyrp, states, gain, *rtabs,
               cr, sr), comm, name="seq_bwd", grid=(nb,),
        in_specs=[SMEM_FULL, cols(512), rows(256), pl.BlockSpec((c, 256), lambda s: (prev(s), 0)), cols(256),
                  pl.BlockSpec((256, c), lambda s: (0, prev(s))), cols(512), rows(LANE), rows(LANE), rows(LANE),
                  cols(8), cols(8), rows(1024), rows(1024), rows(1024), rows(1024), rows(1024),
                  pl.BlockSpec((1, RET_HEADS, dk, dv), lambda s: (cur(s), 0, 0, 0)),
                  VMEM_FULL, VMEM_FULL, VMEM_FULL, VMEM_FULL, rows(LANE), rows(LANE)],
        out_specs=[cols(512), rows(256), acc(LANE), rows(3072), acc(1024)],
        out_shape=[jax.ShapeDtypeStruct((512, seq), BF), jax.ShapeDtypeStruct((seq, 256), BF),
                   jax.ShapeDtypeStruct((1, LANE), F32), jax.ShapeDtypeStruct((seq, 3072), BF),
                   jax.ShapeDtypeStruct((1, 1024), F32)],
        scratch_shapes=[pltpu.VMEM((c, 256), F32), pltpu.VMEM((RET_HEADS, dk, dv), F32)])


def _inproj_bwd(dqa, dkva, dret, dgates, dh1, x, g1, w_in, comm=None):
    seq = x.shape[0]
    tm = min(TOKEN_TILE, seq)

    def body(dqa_ref, dkva_ref, dret_ref, dgates_ref, dh1_ref, x_ref, g_ref, w_hbm, dx_ref, dg1_ref, w_vmem, w_sems):
        w = _Resident(w_hbm, w_vmem, w_sems, IN_CHUNKS_BWD)
        _Resident.load(w)

        @pl.when(pl.program_id(0) == 0)
        def _():
            dg1_ref[...] = jnp.zeros_like(dg1_ref)

        dxn = _tn(dqa_ref[...], w.chunk(0)[...])
        dxn = dxn + _nn(dkva_ref[...], w.chunk(1)[...])
        dxn = dxn + _nn(dret_ref[...], w.chunk(2)[...])
        dxn = dxn + _nn(dgates_ref[...], w.chunk(3)[...])
        dnorm, dg = _rms_bwd(dxn, x_ref[...], g_ref[...])
        dx_ref[...] = dh1_ref[...] + dnorm
        dg1_ref[...] += dg

    return _call(
        body, (dqa, dkva, dret, dgates, dh1, x, g1, w_in), comm, name="inproj_bwd", grid=(seq // tm,),
        in_specs=[_cols(512, tm), _rows(tm, 256), _rows(tm, 3072), _rows(tm, 2048), _rows(tm, D_MODEL),
                  _rows(tm, D_MODEL), VMEM_FULL, ANY],
        out_specs=[_rows(tm, D_MODEL), pl.BlockSpec((1, D_MODEL), lambda i: (0, 0))],
        out_shape=[jax.ShapeDtypeStruct((seq, D_MODEL), F32), jax.ShapeDtypeStruct((1, D_MODEL), F32)],
        scratch_shapes=_Resident.scratch(w_in, IN_CHUNKS_BWD))


def _local_step(x, target, g1, bg, sinks, gain, g2, g3, w_in, shards):
    seq = x.shape[0]
    ca, sa, sb, cr, sr, ct, st = _rope_tables(seq)
    rtabs = _ret_tables()
    scatter = _Scatter

    comm_mid, unpack_mid = _gather(shards, ("w_att_up", "w_ret_up", "w_out"), relay=True)
    comm_ff1, unpack_ff1 = _gather(shards, ("w_ff1",), relay=True)
    (xn1, qt, kv, kvt, qkr, vr, gr, gates), got = _inproj_fwd(x, g1, w_in, (ca, sa, sb, cr, sr, ct, st),
                                                              _join(comm_mid, comm_ff1))
    w_att, w_ret, w_out = unpack_mid(got[:3])
    (w_ff1,) = unpack_ff1(got[3:])
    comm, unpack = _gather(shards, ("w_ff2",), relay=True)
    (oa, ret, yrp, states), got = _seq_fwd(qt, kv, kvt, sinks, qkr, vr, gr, gain, rtabs, comm)
    (w_ff2,) = unpack(got)
    ya, yr, merged, h1 = _mix_fwd(oa, yrp, gates, bg, x, w_att, w_ret, w_out)
    xn2, hdn, dh2, du, dh1, loss, dg3, dg2 = _mlp_loss_step(h1, g2, g3, target, w_ff1, w_ff2)

    ff2 = scatter(dict(w_ff2=_mm_tn(hdn, dh2, "dw_ff2")), "ff2")
    d_ff1, r1 = _mm_tn(du, xn2, "dw_ff1", comm=ff2.pair_comm())
    ff2.pair_done(r1)
    ff1 = scatter(dict(w_ff1=d_ff1), "ff1")
    (dya, dyr, dgates, doa, dyrp, db), r1 = _mix_bwd(dh1, ya, yr, gates, bg, w_att, w_ret, w_out, ff1.pair_comm())
    ff1.pair_done(r1)
    mid = scatter(dict(w_att_up=_mm_tn(oa, dya, "dw_att_up", a_is_t=True), w_ret_up=_mm_tn(yrp, dyr, "dw_ret_up"),
                       w_out=_mm_tn(merged, dh1, "dw_out")), "mid")
    (dqa, dkva, dsink, dret, dgain), got = _seq_bwd(
        qt, kv, kvt, doa, sinks, qkr, vr, gr, ret, dyrp, states, gain, rtabs, (ca, sa, sb, cr, sr, ct, st),
        _join(_join(ff2.chip_comm(), ff1.chip_comm()), mid.pair_comm()))
    ff2.chip_done(got[:1])
    ff1.chip_done(got[1:2])
    mid.pair_done(got[2:])
    d_in, r2 = _dw_in_t((dqa, dkva, dret, dgates), (True, False, False, False), xn1, mid.chip_comm())
    mid.chip_done(r2)
    win = scatter(dict(w_in=d_in), "in")
    win.pair_done(_comm_alone(win.pair_comm(), "pair_exchange_in"))
    (dx, dg1), r2 = _inproj_bwd(dqa, dkva, dret, dgates, dh1, x, g1, w_in, win.chip_comm())
    win.chip_done(r2)
    small = dict(norm_mix_gain=dg1, b_gates=db, attn_sinks=dsink, ret_gn_gain=dgain, norm_mlp_gain=dg2,
                 norm_final_gain=dg3)
    return loss, dx, small, (win, mid, ff1, ff2)


def _coords():
    return lax.axis_index("x"), lax.axis_index("y"), lax.axis_index("c")


def _flip(v, bit):
    return 1 - v if bit else v


def _xor(a, b):
    return a + b - 2 * a * b


def _gather_relay_comm(shards):
    n = len(shards)

    def parts(ins, outs, sems):
        send_sems, recv_sems, local_sems = sems
        x, y, c = _coords()
        me, sib = (x, y, c), (x, y, 1 - c)

        def ring(core):
            return ((_xor(x, core), _xor(y, 1 - core), core), (_xor(x, 1 - core), _xor(y, core), core),
                    (1 - x, 1 - y, core))

        def slot(a, blk):
            return outs[a].at[4 * blk[0] + 2 * blk[1] + blk[2]]

        def copy(a, k, blk, to, src=None):
            return pltpu.make_async_remote_copy(
                src_ref=slot(a, blk) if src is None else src, dst_ref=slot(a, blk),
                send_sem=send_sems.at[a, k], recv_sem=recv_sems.at[a, k], device_id=to, device_id_type=MESH)

        mine = [pltpu.make_async_copy(ins[a], slot(a, me), local_sems.at[a]) for a in range(n)]
        return me, sib, ring, copy, mine

    def first_copies(ins, outs, sems):
        me, sib, ring, copy, mine = parts(ins, outs, sems)
        src, dst, _ = ring(me[2])
        return mine, [copy(a, k, me, to, src=ins[a]) for a in range(n) for k, to in ((1, src), (2, dst), (0, sib))]

    def start(ins, outs, sems):
        mine, first = first_copies(ins, outs, sems)
        for cp in mine + first:
            cp.start()

    def relay(ins, outs, sems):
        me, sib, ring, copy, _ = parts(ins, outs, sems)
        src, dst, _ = ring(me[2])
        for a in range(n):
            copy(a, 1, src, me).wait_recv()
            copy(a, 3, src, dst).start()
            copy(a, 4, src, sib).start()

    def finish(ins, outs, sems):
        me, sib, ring, copy, _ = parts(ins, outs, sems)
        mine, first = first_copies(ins, outs, sems)
        src, dst, diag = ring(me[2])
        passed = [copy(a, k, src, to) for a in range(n) for k, to in ((3, dst), (4, sib))]
        for k, blk in ((2, dst), (3, diag)):
            for a in range(n):
                copy(a, k, blk, me).wait_recv()
                fwd = copy(a, k + 3, blk, sib)
                fwd.start()
                passed.append(fwd)
        s_src, s_dst, s_diag = ring(sib[2])
        for a in range(n):
            for k, blk in ((0, sib), (4, s_src), (5, s_dst), (6, s_diag)):
                copy(a, k, blk, me).wait_recv()
        for cp in first + passed:
            cp.wait_send()
        for cp in mine:
            cp.wait()

    return _Comm(list(shards), [jax.ShapeDtypeStruct((N_DEV,) + s.shape, s.dtype) for s in shards],
                 [pltpu.SemaphoreType.DMA((n, 7)), pltpu.SemaphoreType.DMA((n, 7)), pltpu.SemaphoreType.DMA((n,))],
                 start, finish, relay)


def _gather_comm(shards):
    n = len(shards)

    def parts(ins, outs, sems):
        send_sems, recv_sems, local_sems = sems
        x, y, c = _coords()
        me, sib = (x, y, c), (x, y, 1 - c)
        chips = [(1 - x, y), (x, 1 - y), (1 - x, 1 - y)]

        def slot(a, blk):
            return outs[a].at[4 * blk[0] + 2 * blk[1] + blk[2]]

        def copy(a, k, blk, to, src=None):
            return pltpu.make_async_remote_copy(
                src_ref=slot(a, blk) if src is None else src, dst_ref=slot(a, blk),
                send_sem=send_sems.at[a, k], recv_sem=recv_sems.at[a, k], device_id=to, device_id_type=MESH)

        mine = [pltpu.make_async_copy(ins[a], slot(a, me), local_sems.at[a]) for a in range(n)]
        first = []
        for a in range(n):
            first.append(copy(a, 0, me, sib, src=ins[a]))
            first += [copy(a, 1 + j, me, (*chip, c), src=ins[a]) for j, chip in enumerate(chips)]
        return me, sib, chips, c, copy, mine, first

    def start(ins, outs, sems):
        *_, mine, first = parts(ins, outs, sems)
        for cp in mine + first:
            cp.start()

    def finish(ins, outs, sems):
        me, sib, chips, c, copy, mine, first = parts(ins, outs, sems)
        passed = []
        for j, chip in enumerate(chips):
            for a in range(n):
                copy(a, 1 + j, (*chip, c), me).wait_recv()
                fwd = copy(a, 4 + j, (*chip, c), sib)
                fwd.start()
                passed.append(fwd)
        for a in range(n):
            copy(a, 0, sib, me).wait_recv()
            for j, chip in enumerate(chips):
                copy(a, 4 + j, (*chip, 1 - c), me).wait_recv()
        for cp in first + passed:
            cp.wait_send()
        for cp in mine:
            cp.wait()

    return _Comm(list(shards), [jax.ShapeDtypeStruct((N_DEV,) + s.shape, s.dtype) for s in shards],
                 [pltpu.SemaphoreType.DMA((n, 7)), pltpu.SemaphoreType.DMA((n, 7)), pltpu.SemaphoreType.DMA((n,))],
                 start, finish)


COLUMN_SHARDED = ("w_in", "w_ff1")
COLUMN_RELAID = ("w_att_up",)


def _gather(shards, names, relay=False):
    def unpack(got):
        return [jnp.transpose(g, (1, 0, 2)).reshape(g.shape[1], N_DEV * g.shape[2]) if k in COLUMN_RELAID
                else g.reshape(N_DEV * g.shape[1], g.shape[2]) for k, g in zip(names, got)]

    return (_gather_relay_comm if relay else _gather_comm)([shards[k] for k in names]), unpack


def _pair_comm(grads):
    n = len(grads)

    def copies(g, r1, sems):
        send_sems, recv_sems = sems
        x, y, c = _coords()
        return [pltpu.make_async_remote_copy(
            src_ref=g[a].at[2 * j + (1 - c)], dst_ref=r1[a].at[j], send_sem=send_sems.at[a, j],
            recv_sem=recv_sems.at[a, j], device_id=(x, y, 1 - c), device_id_type=MESH)
            for a in range(n) for j in range(4)]

    def start(g, r1, sems):
        for cp in copies(g, r1, sems):
            cp.start()

    def finish(g, r1, sems):
        for cp in copies(g, r1, sems):
            cp.wait_recv()
        for cp in copies(g, r1, sems):
            cp.wait_send()

    return _Comm(list(grads), [jax.ShapeDtypeStruct((4,) + g.shape[1:], g.dtype) for g in grads],
                 [pltpu.SemaphoreType.DMA((n, 4)), pltpu.SemaphoreType.DMA((n, 4))], start, finish)


def _small_comm(small):
    def parts(ins, outs, sems):
        (small_ref,), (small_all,) = ins, outs
        ssend, srecv, lsem = sems
        x, y, c = _coords()
        me_idx = 4 * x + 2 * y + c
        own = pltpu.make_async_copy(small_ref, small_all.at[me_idx], lsem)
        sends, recvs = [], []
        for r in range(1, N_DEV):
            px, py, pc = _flip(x, r & 4), _flip(y, r & 2), _flip(c, r & 1)
            sends.append(pltpu.make_async_remote_copy(
                src_ref=small_ref, dst_ref=small_all.at[me_idx], send_sem=ssend.at[r - 1], recv_sem=srecv.at[r - 1],
                device_id=(px, py, pc), device_id_type=MESH))
            recvs.append(pltpu.make_async_remote_copy(
                src_ref=small_ref, dst_ref=small_all.at[4 * px + 2 * py + pc], send_sem=ssend.at[r - 1],
                recv_sem=srecv.at[r - 1], device_id=(px, py, pc), device_id_type=MESH))
        return own, sends, recvs

    def start(ins, outs, sems):
        own, sends, _ = parts(ins, outs, sems)
        own.start()
        for cp in sends:
            cp.start()

    def finish(ins, outs, sems):
        own, sends, recvs = parts(ins, outs, sems)
        for cp in recvs:
            cp.wait_recv()
        for cp in sends:
            cp.wait_send()
        own.wait()

    return _Comm([small], [jax.ShapeDtypeStruct((N_DEV,) + small.shape, small.dtype)],
                 [pltpu.SemaphoreType.DMA((N_DEV - 1,)), pltpu.SemaphoreType.DMA((N_DEV - 1,)),
                  pltpu.SemaphoreType.DMA], start, finish)


def _pair_sum(grads, r1s, c_arr, tag):
    n = len(grads)
    q = 1

    def body(c_ref, *refs):
        g, r, t = refs[:n], refs[n:2 * n], refs[2 * n:]
        for a in range(n):
            t[a][...] = (g[a][...] + r[a][...]).astype(t[a].dtype)

    def blk(arr):
        return (1, arr.shape[1] // q, arr.shape[2])

    grid_spec = pltpu.PrefetchScalarGridSpec(
        num_scalar_prefetch=1, grid=(4, q),
        in_specs=[pl.BlockSpec(blk(g), lambda j, s, c_ref: (2 * j + c_ref[0], s, 0)) for g in grads]
        + [pl.BlockSpec(blk(r), lambda j, s, c_ref: (j, s, 0)) for r in r1s],
        out_specs=[pl.BlockSpec(blk(r), lambda j, s, c_ref: (j, s, 0)) for r in r1s])
    return pl.pallas_call(
        body, name="pair_sum_" + tag, grid_spec=grid_spec,
        out_shape=[jax.ShapeDtypeStruct(r.shape, RS_PAYLOAD) for r in r1s],
        compiler_params=_params(2),
    )(c_arr, *grads, *r1s)


def _chip_comm(ts):
    n = len(ts)

    def copies(t, r2, sems):
        send_sems, recv_sems = sems
        x, y, c = _coords()
        out = []
        for a in range(n):
            for r in range(1, 4):
                tx, ty = _flip(x, r & 2), _flip(y, r & 1)
                out.append(pltpu.make_async_remote_copy(
                    src_ref=t[a].at[2 * tx + ty], dst_ref=r2[a].at[r - 1], send_sem=send_sems.at[a, r - 1],
                    recv_sem=recv_sems.at[a, r - 1], device_id=(tx, ty, c), device_id_type=MESH))
        return out

    def start(t, r2, sems):
        for cp in copies(t, r2, sems):
            cp.start()

    def finish(t, r2, sems):
        for cp in copies(t, r2, sems):
            cp.wait_recv()
        for cp in copies(t, r2, sems):
            cp.wait_send()

    return _Comm(list(ts), [jax.ShapeDtypeStruct((3,) + t.shape[1:], t.dtype) for t in ts],
                 [pltpu.SemaphoreType.DMA((n, 3)), pltpu.SemaphoreType.DMA((n, 3))], start, finish)


class _Scatter:
    def __init__(self, grads, tag):
        self.names, self.tag = tuple(grads), tag
        self.stacks = [jnp.transpose(g.reshape(g.shape[0], N_DEV, g.shape[1] // N_DEV), (1, 0, 2))
                       if k in COLUMN_RELAID else g.reshape(N_DEV, g.shape[0] // N_DEV, g.shape[1])
                       for k, g in grads.items()]

    def pair_comm(self):
        return _pair_comm(self.stacks)

    def pair_done(self, r1s):
        self.r1s = list(r1s)
        c_arr = jnp.reshape(lax.axis_index("c"), (1,)).astype(jnp.int32)
        self.ts = _pair_sum(self.stacks, self.r1s, c_arr, self.tag)

    def chip_comm(self):
        return _chip_comm(self.ts)

    def chip_done(self, r2s):
        self.r2s = list(r2s)


def _adamw(w, g, m, v):
    m = ADAM_B1 * m + (1.0 - ADAM_B1) * g
    v = ADAM_B2 * v + (1.0 - ADAM_B2) * jnp.square(g)
    m_hat = m / (1.0 - ADAM_B1 ** ADAM_STEP)
    v_hat = v / (1.0 - ADAM_B2 ** ADAM_STEP)
    delta = -ADAM_LR * (m_hat / (jnp.sqrt(v_hat) + ADAM_EPS) + ADAM_WD * w)
    return delta, m, v


ADAM_STEPS = 4


def _piece_specs(stacks):
    def rows(s):
        return s.shape[1] // ADAM_STEPS
    return ([pl.BlockSpec((1, rows(s), s.shape[2]), lambda i, idx_ref: (idx_ref[0], i, 0)) for s in stacks]
            + [pl.BlockSpec((1, rows(s), s.shape[2]), lambda i, idx_ref: (idx_ref[1], i, 0)) for s in stacks]
            + [pl.BlockSpec((3, rows(s), s.shape[2]), lambda i, idx_ref: (0, i, 0)) for s in stacks])


def _piece_sum(g0, r1, r2):
    return (((g0[0] + r1[0]) + r2[0].astype(F32)) + r2[1].astype(F32)) + r2[2].astype(F32)


def _shard_sum(stacks, r1s, r2s, idx_arr):
    n = len(stacks)

    def body(idx_ref, *refs):
        g0, r1, r2, outs = (refs[k * n:(k + 1) * n] for k in range(4))
        for a in range(n):
            outs[a][...] = _piece_sum(g0[a], r1[a], r2[a])

    out_specs = [pl.BlockSpec((s.shape[1] // ADAM_STEPS, s.shape[2]), lambda i, idx_ref: (i, 0)) for s in stacks]
    grid_spec = pltpu.PrefetchScalarGridSpec(num_scalar_prefetch=1, grid=(ADAM_STEPS,),
                                             in_specs=_piece_specs(stacks), out_specs=out_specs)
    return pl.pallas_call(
        body, name="shard_sum", grid_spec=grid_spec,
        out_shape=[jax.ShapeDtypeStruct(s.shape[1:], F32) for s in stacks],
        compiler_params=_params(1),
    )(idx_arr, *stacks, *r1s, *r2s)


def _adam_big(pieces, summed, ws, ms, vs, idx_arr):
    stacks, r1s, r2s = pieces
    n_p, n = len(stacks), len(ws)

    def body(idx_ref, *refs):
        it = iter(refs)
        g0, r1, r2, gs, w, m, v = ([next(it) for _ in range(k)] for k in (n_p, n_p, n_p, n - n_p, n, n, n))
        outs = list(it)
        for a in range(n):
            g = _piece_sum(g0[a], r1[a], r2[a]) if a < n_p else gs[a - n_p][...]
            delta, nm, nv = _adamw(w[a][...], g, m[a][...], v[a][...])
            outs[4 * a][...] = g
            outs[4 * a + 1][...] = delta
            outs[4 * a + 2][...] = nm
            outs[4 * a + 3][...] = nv

    def shard_spec(w):
        return pl.BlockSpec((w.shape[0] // ADAM_STEPS, w.shape[1]), lambda i, idx_ref: (i, 0))

    in_specs = _piece_specs(stacks) + [shard_spec(w) for w in ws[n_p:]] + [shard_spec(w) for w in ws] * 3
    out_specs = [shard_spec(w) for w in ws for _ in range(4)]
    grid_spec = pltpu.PrefetchScalarGridSpec(num_scalar_prefetch=1, grid=(ADAM_STEPS,), in_specs=in_specs,
                                             out_specs=out_specs)
    return pl.pallas_call(
        body, name="adam_big", grid_spec=grid_spec,
        out_shape=[jax.ShapeDtypeStruct(w.shape, F32) for w in ws for _ in range(4)],
        compiler_params=_params(1),
    )(idx_arr, *stacks, *r1s, *r2s, *summed, *ws, *ms, *vs)


def _adam_small(small_all, w, m, v):
    def body(all_ref, w_ref, m_ref, v_ref, g_ref, d_ref, nm_ref, nv_ref):
        g = all_ref[0]
        for k in range(1, N_DEV):
            g = g + all_ref[k]
        delta, nm, nv = _adamw(w_ref[...], g, m_ref[...], v_ref[...])
        g_ref[...] = g
        d_ref[...] = delta
        nm_ref[...] = nm
        nv_ref[...] = nv

    return pl.pallas_call(
        body, name="adam_small",
        in_specs=[VMEM_FULL] * 4, out_specs=[VMEM_FULL] * 4,
        out_shape=[jax.ShapeDtypeStruct(w.shape, F32)] * 4,
    )(small_all, w, m, v)


RS_PAYLOAD = BF


def _pack_rows(rows):
    rid = lax.broadcasted_iota(jnp.int32, (8, D_MODEL), 0)
    out = jnp.zeros((8, D_MODEL), F32)
    for i, r in enumerate(rows):
        out = jnp.where(rid == i, jnp.broadcast_to(r, (8, D_MODEL)), out)
    return out


def _small_rows(norm_mix_gain, b_gates, attn_sinks, ret_gn_gain, norm_mlp_gain, norm_final_gain):
    return [norm_mix_gain, b_gates[:, :D_MODEL], b_gates[:, D_MODEL:], ret_gn_gain, norm_mlp_gain,
            norm_final_gain.reshape(1, D_MODEL), jnp.pad(attn_sinks, ((0, 0), (0, D_MODEL - ATT_HEADS)))]


def _unpack_small(p):
    return dict(norm_mix_gain=p[0:1], b_gates=jnp.concatenate([p[1:2], p[2:3]], axis=1), ret_gn_gain=p[3:4],
                norm_mlp_gain=p[4:5], norm_final_gain=p[5], attn_sinks=p[6:7, :ATT_HEADS])


WEIGHTS = ("norm_mix_gain", "w_in", "b_gates", "attn_sinks", "ret_gn_gain", "w_att_up", "w_ret_up", "w_out",
           "norm_mlp_gain", "w_ff1", "w_ff2", "norm_final_gain")
BIG = ("w_in", "w_att_up", "w_ret_up", "w_out", "w_ff1", "w_ff2")


def kernel(x, norm_mix_gain, w_in, b_gates, attn_sinks, ret_gn_gain, w_att_up, w_ret_up, w_out, norm_mlp_gain, w_ff1, w_ff2, norm_final_gain, loss_target, m_norm_mix_gain, m_w_in, m_b_gates, m_attn_sinks, m_ret_gn_gain, m_w_att_up, m_w_ret_up, m_w_out, m_norm_mlp_gain, m_w_ff1, m_w_ff2, m_norm_final_gain, v_norm_mix_gain, v_w_in, v_b_gates, v_attn_sinks, v_ret_gn_gain, v_w_att_up, v_w_ret_up, v_w_out, v_norm_mlp_gain, v_w_ff1, v_w_ff2, v_norm_final_gain):
    w = dict(norm_mix_gain=norm_mix_gain, w_in=w_in, b_gates=b_gates, attn_sinks=attn_sinks, ret_gn_gain=ret_gn_gain,
             w_att_up=w_att_up, w_ret_up=w_ret_up, w_out=w_out, norm_mlp_gain=norm_mlp_gain, w_ff1=w_ff1,
             w_ff2=w_ff2, norm_final_gain=norm_final_gain)
    m = dict(norm_mix_gain=m_norm_mix_gain, w_in=m_w_in, b_gates=m_b_gates, attn_sinks=m_attn_sinks,
             ret_gn_gain=m_ret_gn_gain, w_att_up=m_w_att_up, w_ret_up=m_w_ret_up, w_out=m_w_out,
             norm_mlp_gain=m_norm_mlp_gain, w_ff1=m_w_ff1, w_ff2=m_w_ff2, norm_final_gain=m_norm_final_gain)
    v = dict(norm_mix_gain=v_norm_mix_gain, w_in=v_w_in, b_gates=v_b_gates, attn_sinks=v_attn_sinks,
             ret_gn_gain=v_ret_gn_gain, w_att_up=v_w_att_up, w_ret_up=v_w_ret_up, w_out=v_w_out,
             norm_mlp_gain=v_norm_mlp_gain, w_ff1=v_w_ff1, w_ff2=v_w_ff2, norm_final_gain=v_norm_final_gain)

    shards = {k: (w[k][0].T if k in COLUMN_SHARDED else w[k][0]).astype(BF) for k in BIG}
    comm, unpack = _gather(shards, ("w_in",), relay=True)
    (w_in_full,) = unpack(_comm_alone(comm, "allgather_w_in"))

    loss_p, dx, small, groups = _local_step(
        x[0], loss_target[0], norm_mix_gain, b_gates, attn_sinks[0], ret_gn_gain, norm_mlp_gain,
        norm_final_gain.reshape(1, D_MODEL), w_in_full, shards)

    by_name = {}
    for grp in groups:
        for k, stack, r1, r2 in zip(grp.names, grp.stacks, grp.r1s, grp.r2s):
            by_name[k] = (stack, r1, r2)
    lane = lax.broadcasted_iota(jnp.int32, (1, D_MODEL), 1)
    sink_row = jnp.where(lane < ATT_HEADS, jnp.pad(small["attn_sinks"], ((0, 0), (0, D_MODEL - LANE))),
                         jnp.where(lane == ATT_HEADS, jnp.pad(loss_p, ((0, 0), (0, D_MODEL - LANE)), mode="edge"),
                                   0.0))
    small_pack = _pack_rows([small["norm_mix_gain"], small["b_gates"][:, :D_MODEL], small["b_gates"][:, D_MODEL:],
                             small["ret_gn_gain"], small["norm_mlp_gain"], small["norm_final_gain"], sink_row])
    xi, yi, ci = _coords()
    (small_all,) = _comm_alone(_small_comm(small_pack), "small_exchange")
    idx_arr = jnp.stack([4 * xi + 2 * yi + ci, 2 * xi + yi]).astype(jnp.int32)
    in_pieces = ["w_att_up", "w_ret_up", "w_out", "w_ff2", "w_in"]
    in_sum = ["w_ff1"]
    order = in_pieces + in_sum
    summed_t = _shard_sum(*[[by_name[k][i] for k in in_sum] for i in range(3)], idx_arr)

    def shard(tree, k):
        return tree[k][0].T if k == "w_in" else tree[k][0]

    adam_out = _adam_big([[by_name[k][i] for k in in_pieces] for i in range(3)], [g.T for g in summed_t],
                         [shard(w, k) for k in order], [shard(m, k) for k in order], [shard(v, k) for k in order],
                         idx_arr)
    big_out = [adam_out[4 * order.index(k) + i].T if k == "w_in" else adam_out[4 * order.index(k) + i]
               for k in BIG for i in range(4)]
    sm_g, sm_d, sm_m, sm_v = _adam_small(small_all, _pack_rows(_small_rows(*[w[k] for k in WEIGHTS if k not in BIG])),
                                         _pack_rows(_small_rows(*[m[k] for k in WEIGHTS if k not in BIG])),
                                         _pack_rows(_small_rows(*[v[k] for k in WEIGHTS if k not in BIG])))

    loss = sm_g[6, ATT_HEADS]
    kinds = []
    for idx, packed in enumerate((sm_g, sm_d, sm_m, sm_v)):
        out = _unpack_small(packed)
        for a, k in enumerate(BIG):
            out[k] = big_out[4 * a + idx][None]
        kinds.append(out)
    return (loss, dx[None], *[kinds[0][k] for k in WEIGHTS], *[kinds[1][k] for k in WEIGHTS],
            *[kinds[2][k] for k in WEIGHTS], *[kinds[3][k] for k in WEIGHTS])
```

```python
import functools
import math

import jax
import jax.numpy as jnp
from jax import lax
from jax.experimental import pallas as pl
from jax.experimental.pallas import tpu as pltpu

F32 = jnp.float32
BF = jnp.bfloat16
MESH = pl.DeviceIdType.MESH

D_MODEL = 1024
ATT_HEADS = 8
ATT_HEAD_DIM = 64
ATT_BLOCK = 128
ROPE_DIM = 16
ROPE_THETA = 500000.0
RET_HEADS = 4
RET_KEY_DIM = 128
RET_VAL_DIM = 256
RET_CHUNK = 128
RET_ROT_BASE = 10000.0
D_FF = 4096
NORM_EPS = 1e-6
GN_EPS = 1e-6
NEG_INF = -1e30
ATT_SCALE = ATT_HEAD_DIM ** -0.5
RET_SCALE = RET_KEY_DIM ** -0.5

C_QA, C_KA, C_VA, C_QR, C_KR, C_VR, C_GR, C_GATES, C_END = 0, 512, 640, 768, 1280, 1792, 2816, 3840, 5888

ADAM_LR = 0.001
ADAM_B1 = 0.9
ADAM_B2 = 0.999
ADAM_EPS = 1e-08
ADAM_WD = 0.01
ADAM_STEP = 10

N_DEV = 8
LANE = 128
VMEM_LIMIT = 56 * 1024 * 1024
TOKEN_TILE = 512
MLP_TOKEN_TILE = 256
TN_TOKEN_TILE = 2048
FF_CHUNKS = (0, 1024, 2048, 3072, 4096)
IN_CHUNKS_FWD = (C_QA, C_KA, C_QR, C_KR, C_VR, C_GR, C_GATES, C_END)
IN_CHUNKS_BWD = (C_QA, C_KA, C_QR, C_GATES, C_END)

RET_LOG_GAMMA = tuple(math.log1p(-(2.0 ** (-5.0 - h))) for h in range(RET_HEADS))
RET_CHUNK_DECAY = tuple(math.exp(RET_CHUNK * lg) for lg in RET_LOG_GAMMA)

VMEM_FULL = pl.BlockSpec(memory_space=pltpu.VMEM)
SMEM_FULL = pl.BlockSpec(memory_space=pltpu.SMEM)
ANY = pl.BlockSpec(memory_space=pl.ANY)


def _params(n_axes):
    return pltpu.CompilerParams(dimension_semantics=("arbitrary",) * n_axes, vmem_limit_bytes=VMEM_LIMIT)


def _nn(a, b):
    return jnp.dot(a, b, preferred_element_type=F32)


def _nt(a, b):
    return lax.dot_general(a, b, (((1,), (1,)), ((), ())), preferred_element_type=F32)


def _tn(a, b):
    return lax.dot_general(a, b, (((0,), (0,)), ((), ())), preferred_element_type=F32)


def _sigmoid(v):
    return 1.0 / (1.0 + jnp.exp(-v))


def _rows(tile, width):
    return pl.BlockSpec((tile, width), lambda i: (i, 0))


def _cols(height, tile):
    return pl.BlockSpec((height, tile), lambda i: (0, i))


class _Resident:
    def __init__(self, hbm_ref, vmem_ref, sems, bounds):
        self.hbm, self.vmem, self.sems, self.bounds = hbm_ref, vmem_ref, sems, bounds

    @staticmethod
    def scratch(w, bounds):
        return [pltpu.VMEM(w.shape, w.dtype), pltpu.SemaphoreType.DMA((len(bounds) - 1,))]

    @staticmethod
    def load(*weights):
        @pl.when(pl.program_id(0) == 0)
        def _():
            copies = [w._copy(c) for w in weights for c in range(len(w.bounds) - 1)]
            for cp in copies:
                cp.start()
            for cp in copies:
                cp.wait()

    def _rows(self, c):
        return pl.ds(self.bounds[c], self.bounds[c + 1] - self.bounds[c])

    def _copy(self, c):
        return pltpu.make_async_copy(self.hbm.at[self._rows(c)], self.vmem.at[self._rows(c)], self.sems.at[c])

    def chunk(self, c):
        return self.vmem.at[self._rows(c)]


class _Comm:
    def __init__(self, inputs, out_shapes, scratch, start, finish, relay=None):
        self.inputs, self.out_shapes, self.scratch = inputs, out_shapes, scratch
        self.start, self.finish, self.relay = start, finish, relay


RELAY_AT = 0.55


def _join(a, b):
    na_in, na_out, na_sem = len(a.inputs), len(a.out_shapes), len(a.scratch)

    def both(name):
        def run(ins, outs, sems):
            for part, args in ((a, (ins[:na_in], outs[:na_out], sems[:na_sem])),
                               (b, (ins[na_in:], outs[na_out:], sems[na_sem:]))):
                if getattr(part, name) is not None:
                    getattr(part, name)(*args)
        return run

    return _Comm(list(a.inputs) + list(b.inputs), list(a.out_shapes) + list(b.out_shapes),
                 list(a.scratch) + list(b.scratch), both("start"), both("finish"),
                 both("relay") if a.relay or b.relay else None)


def _call(body, args, comm=None, *, name, grid, in_specs, out_specs, out_shape, scratch_shapes=()):
    params = _params(len(grid))
    if comm is None:
        return pl.pallas_call(body, name=name, grid=grid, in_specs=in_specs, out_specs=out_specs, out_shape=out_shape,
                              scratch_shapes=scratch_shapes, compiler_params=params)(*args), ()
    single = not isinstance(out_specs, (list, tuple))
    out_specs_l = [out_specs] if single else list(out_specs)
    out_shape_l = [out_shape] if single else list(out_shape)
    n_in, n_out, n_scr = len(in_specs), len(out_specs_l), len(scratch_shapes)
    n_cin, n_cout = len(comm.inputs), len(comm.out_shapes)

    def hosted(*refs):
        it = iter(refs)
        ins, cin, outs, cout, scr = ([next(it) for _ in range(k)] for k in (n_in, n_cin, n_out, n_cout, n_scr))
        sems = list(it)
        ids = [pl.program_id(k) for k in range(len(grid))]
        first = functools.reduce(jnp.logical_and, [i == 0 for i in ids])
        last = functools.reduce(jnp.logical_and, [i == g - 1 for i, g in zip(ids, grid)])

        @pl.when(first)
        def _():
            comm.start(cin, cout, sems)

        if comm.relay is not None:
            at = [int(grid[0] * RELAY_AT)] + [0] * (len(grid) - 1)

            @pl.when(functools.reduce(jnp.logical_and, [i == v for i, v in zip(ids, at)]))
            def _():
                comm.relay(cin, cout, sems)

        body(*ins, *outs, *scr)

        @pl.when(last)
        def _():
            comm.finish(cin, cout, sems)

    res = pl.pallas_call(
        hosted, name=name, grid=grid, in_specs=list(in_specs) + [ANY] * n_cin,
        out_specs=out_specs_l + [ANY] * n_cout, out_shape=out_shape_l + list(comm.out_shapes),
        scratch_shapes=list(scratch_shapes) + list(comm.scratch), compiler_params=params)(*args, *comm.inputs)
    return (res[0] if single else res[:n_out]), res[n_out:]


def _comm_alone(comm, name):
    n_cin, n_cout = len(comm.inputs), len(comm.out_shapes)

    def body(*refs):
        cin, cout, sems = refs[:n_cin], refs[n_cin:n_cin + n_cout], refs[n_cin + n_cout:]
        comm.start(cin, cout, sems)
        if comm.relay is not None:
            comm.relay(cin, cout, sems)
        comm.finish(cin, cout, sems)

    return pl.pallas_call(body, name=name, in_specs=[ANY] * n_cin, out_specs=[ANY] * n_cout,
                          out_shape=list(comm.out_shapes), scratch_shapes=list(comm.scratch))(*comm.inputs)


def _slabs(v, fn):
    return jnp.concatenate([fn(v[:, LANE * j:LANE * (j + 1)]) for j in range(v.shape[1] // LANE)], axis=1)


def _rope_att(v, ca, sa, sb):
    return _slabs(v, lambda t: t * ca + pltpu.roll(t, LANE - 8, 1) * sa + pltpu.roll(t, 8, 1) * sb)


def _rope_att_t(v, ca, sa, sb):
    return _slabs(v, lambda t: t * ca + pltpu.roll(t * sa, 8, 1) + pltpu.roll(t * sb, LANE - 8, 1))


def _rope_att_rows(v, ct, st, sign):
    parts = []
    for h in range(v.shape[0] // ATT_HEAD_DIM):
        r0 = ATT_HEAD_DIM * h
        x1, x2 = v[r0:r0 + 8], v[r0 + 8:r0 + 16]
        parts += [x1 * ct - sign * (x2 * st), x2 * ct + sign * (x1 * st), v[r0 + 16:r0 + ATT_HEAD_DIM]]
    return jnp.concatenate(parts, axis=0)


def _rope_ret(v, cr, sr):
    return _slabs(v, lambda t: t * cr + pltpu.roll(t, 64, 1) * sr)


def _rope_ret_t(v, cr, sr):
    return _slabs(v, lambda t: t * cr + pltpu.roll(t * sr, 64, 1))


def _rope_lane_tables():
    def inv(dim, theta):
        return theta ** (-jnp.arange(0, dim, 2, dtype=F32) / dim)

    inv_a, inv_r = inv(ROPE_DIM, ROPE_THETA), inv(RET_KEY_DIM, RET_ROT_BASE)
    half = ROPE_DIM // 2
    zeros = jnp.zeros((ATT_HEAD_DIM - ROPE_DIM,), F32)
    freq64 = jnp.concatenate([inv_a, inv_a, zeros])
    lo64 = jnp.concatenate([-jnp.ones((half,), F32), jnp.zeros((half,), F32), zeros])
    hi64 = jnp.concatenate([jnp.zeros((half,), F32), jnp.ones((half,), F32), zeros])
    sign_r = jnp.concatenate([-jnp.ones((64,), F32), jnp.ones((64,), F32)])
    rows = [jnp.tile(freq64, 2), jnp.tile(lo64, 2), jnp.tile(hi64, 2), jnp.tile(inv_r, 2), sign_r]
    lanes = jnp.stack(rows + [jnp.zeros((LANE,), F32)] * (8 - len(rows)))
    return lanes, jnp.broadcast_to(inv_a[:, None], (half, LANE))


def _rope_tables(seq):
    nb = seq // ATT_BLOCK
    lanes, freq_rows = _rope_lane_tables()
    base = jnp.arange(nb, dtype=F32) * ATT_BLOCK
    off = jnp.arange(ATT_BLOCK, dtype=F32)

    def cos_sin(freq, tokens_last):
        if tokens_last:
            a, b = freq[:, :, None] * base[None, :, None], freq[:, :, None] * off[None, None, :]
        else:
            a, b = base[:, None, None] * freq[None], off[None, :, None] * freq[None]
        ca_, sa_, cb_, sb_ = jnp.cos(a), jnp.sin(a), jnp.cos(b), jnp.sin(b)
        return ca_ * cb_ - sa_ * sb_, sa_ * cb_ + ca_ * sb_

    cos_a, sin_a = cos_sin(lanes[0:1], False)
    cos_r, sin_r = cos_sin(lanes[3:4], False)
    cos_t, sin_t = cos_sin(freq_rows[:, :1], True)
    flat = lambda t: t.reshape(seq, LANE)
    return (flat(cos_a), flat(sin_a * lanes[1]), flat(sin_a * lanes[2]), flat(cos_r), flat(sin_r * lanes[4]),
            cos_t.reshape(8, seq), sin_t.reshape(8, seq))


def _ret_tables():
    c = RET_CHUNK
    lg = jnp.asarray(RET_LOG_GAMMA, F32)
    idx = jnp.arange(c, dtype=F32)
    diff = idx[:, None] - idx[None, :]
    intra = jnp.where(diff >= 0, jnp.exp(jnp.maximum(diff, 0.0) * lg[:, None, None]), 0.0)
    qd = jnp.exp((idx + 1.0)[None, :] * lg[:, None])[..., None]
    kd = jnp.exp((c - 1.0 - idx)[None, :] * lg[:, None])[..., None]
    return intra, jnp.broadcast_to(qd, (RET_HEADS, c, RET_KEY_DIM)), jnp.broadcast_to(kd, (RET_HEADS, c, RET_KEY_DIM))


def _inproj_fwd(x, g1, w_in, tabs, comm=None):
    seq = x.shape[0]
    tm = min(TOKEN_TILE, seq)

    def body(x_ref, g_ref, w_hbm, ca_ref, sa_ref, sb_ref, cr_ref, sr_ref, ct_ref, st_ref,
             xn_ref, qt_ref, kv_ref, kvt_ref, qkr_ref, vr_ref, gr_ref, gates_ref, w_vmem, w_sems):
        w = _Resident(w_hbm, w_vmem, w_sems, IN_CHUNKS_FWD)
        _Resident.load(w)
        ca, sa, sb, cr, sr, ct, st = (ref[...] for ref in (ca_ref, sa_ref, sb_ref, cr_ref, sr_ref, ct_ref, st_ref))
        xf = x_ref[...]
        r = lax.rsqrt(jnp.mean(xf * xf, axis=-1, keepdims=True) + NORM_EPS)
        xn = (xf * r * g_ref[...]).astype(BF)
        xn_ref[...] = xn
        qt_ref[...] = _rope_att_rows(_nt(w.chunk(0)[...], xn), ct, st, 1.0).astype(BF)
        w_kv = w.chunk(1)
        kvt = _nt(w_kv[...], xn)
        kvt_ref[:LANE, :] = _rope_att_rows(kvt[:LANE], ct, st, 1.0).astype(BF)
        kvt_ref[LANE:, :] = kvt[LANE:].astype(BF)
        kvn = _nt(xn, w_kv[...])
        kv_ref[:, :LANE] = _rope_att(kvn[:, :LANE], ca, sa, sb).astype(BF)
        kv_ref[:, LANE:] = kvn[:, LANE:].astype(BF)
        qr = _rope_ret(_nt(xn, w.chunk(2)[...]), cr, sr)
        qkr_ref[:, :512] = qr.astype(BF)
        kr = _rope_ret(_nt(xn, w.chunk(3)[...]), cr, sr) * RET_SCALE
        qkr_ref[:, 512:] = kr.astype(BF)
        vr_ref[...] = _nt(xn, w.chunk(4)[...]).astype(BF)
        gr_ref[...] = _nt(xn, w.chunk(5)[...]).astype(BF)
        gates_ref[...] = _nt(xn, w.chunk(6)[...]).astype(BF)

    tab = _rows(tm, LANE)
    tab_t = _cols(8, tm)
    return _call(
        body, (x, g1, w_in, *tabs), comm, name="inproj_fwd", grid=(seq // tm,),
        in_specs=[_rows(tm, D_MODEL), VMEM_FULL, ANY, tab, tab, tab, tab, tab, tab_t, tab_t],
        out_specs=[_rows(tm, D_MODEL), _cols(512, tm), _rows(tm, 256), _cols(256, tm), _rows(tm, 1024),
                   _rows(tm, 1024), _rows(tm, 1024), _rows(tm, 2048)],
        out_shape=[jax.ShapeDtypeStruct(s, BF) for s in ((seq, D_MODEL), (512, seq), (seq, 256), (256, seq),
                                                         (seq, 1024), (seq, 1024), (seq, 1024), (seq, 2048))],
        scratch_shapes=_Resident.scratch(w_in, IN_CHUNKS_FWD))


ATT_Q_PER_KV = ATT_HEADS // 2


def _att_group(ref, kv):
    d = ATT_HEAD_DIM
    g = jnp.concatenate([ref[d * (ATT_Q_PER_KV * kv + j):d * (ATT_Q_PER_KV * kv + j + 1), :]
                         for j in range(ATT_Q_PER_KV)], axis=1)
    z = jnp.zeros_like(g)
    return jnp.concatenate([g, z] if kv == 0 else [z, g], axis=0)


def _att_probs(kwin, qpad, sink_ref, kv, i):
    c = ATT_BLOCK
    kj = lax.broadcasted_iota(jnp.int32, (2 * c, c), 0)
    qi = lax.broadcasted_iota(jnp.int32, (2 * c, c), 1)
    allowed = (kj > qi) & (kj <= qi + c) & ((kj >= c) | (i > 0))
    allowed = jnp.concatenate([allowed] * ATT_Q_PER_KV, axis=1)
    s = jnp.where(allowed, _nn(kwin, qpad) * ATT_SCALE, NEG_INF)
    sk = jnp.concatenate([jnp.full((1, c), sink_ref[ATT_Q_PER_KV * kv + j], F32) for j in range(ATT_Q_PER_KV)], axis=1)
    m = jnp.maximum(jnp.max(s, axis=0, keepdims=True), sk)
    pe = jnp.exp(s - m)
    psink = jnp.exp(sk - m)
    inv = 1.0 / (jnp.sum(pe, axis=0, keepdims=True) + psink)
    return pe * inv, psink * inv


def _seq_fwd(qt, kv, kvt, sinks, qkr, vr, gr, gain, rtabs, comm=None):
    seq = kv.shape[0]
    c = ATT_BLOCK
    assert c == RET_CHUNK
    d = ATT_HEAD_DIM
    nb = seq // c
    dk, dv = RET_KEY_DIM, RET_VAL_DIM

    def body(sink_ref, qt_ref, kvc_ref, kvp_ref, kvtc_ref, kvtp_ref, qk_ref, v_ref, g_ref, gain_ref, intra_ref,
             qd_ref, kd_ref, ot_ref, ret_ref, yrp_ref, st_ref, state):
        i = pl.program_id(0)
        kwin = jnp.concatenate([kvp_ref[:, :LANE], kvc_ref[:, :LANE]], axis=0)
        vtwin = jnp.concatenate([kvtp_ref[LANE:, :], kvtc_ref[LANE:, :]], axis=1)
        for g in (0, 1):
            prob, _ = _att_probs(kwin, _att_group(qt_ref, g), sink_ref, g, i)
            out = _nn(vtwin[d * g:d * (g + 1), :], prob.astype(BF))
            for j in range(ATT_Q_PER_KV):
                h = ATT_Q_PER_KV * g + j
                ot_ref[d * h:d * (h + 1), :] = out[:, c * j:c * (j + 1)].astype(BF)

        @pl.when(i == 0)
        def _():
            state[...] = jnp.zeros_like(state)

        for h in range(RET_HEADS):
            qh = qk_ref[:, dk * h:dk * (h + 1)]
            kh = qk_ref[:, 512 + dk * h:512 + dk * (h + 1)]
            vh = v_ref[:, dv * h:dv * (h + 1)]
            sh = state[h]
            shb = sh.astype(BF)
            st_ref[0, h] = shb
            att = _nt(qh, kh) * intra_ref[h]
            inner = _nn(att.astype(BF), vh)
            cross = _nn((qh.astype(F32) * qd_ref[h]).astype(BF), shb)
            out = inner + cross
            state[h] = sh * RET_CHUNK_DECAY[h] + _tn((kh.astype(F32) * kd_ref[h]).astype(BF), vh)
            ret_ref[:, dv * h:dv * (h + 1)] = out
            mu = jnp.mean(out, axis=-1, keepdims=True)
            dev = out - mu
            var = jnp.mean(dev * dev, axis=-1, keepdims=True)
            y = dev * lax.rsqrt(var + GN_EPS) * gain_ref[:, dv * h:dv * (h + 1)]
            g = g_ref[:, dv * h:dv * (h + 1)].astype(F32)
            yrp_ref[:, dv * h:dv * (h + 1)] = (g * _sigmoid(g) * y).astype(BF)

    prev = lambda i: jnp.maximum(i - 1, 0)
    return _call(
        body, (sinks, qt, kv, kv, kvt, kvt, qkr, vr, gr, gain, *rtabs), comm, name="seq_fwd", grid=(nb,),
        in_specs=[SMEM_FULL, _cols(512, c), _rows(c, 256), pl.BlockSpec((c, 256), lambda i: (prev(i), 0)),
                  _cols(256, c), pl.BlockSpec((256, c), lambda i: (0, prev(i))),
                  _rows(c, 1024), _rows(c, 1024), _rows(c, 1024), VMEM_FULL, VMEM_FULL, VMEM_FULL, VMEM_FULL],
        out_specs=[_cols(512, c), _rows(c, 1024), _rows(c, 1024),
                   pl.BlockSpec((1, RET_HEADS, dk, dv), lambda i: (i, 0, 0, 0))],
        out_shape=[jax.ShapeDtypeStruct((512, seq), BF), jax.ShapeDtypeStruct((seq, 1024), F32),
                   jax.ShapeDtypeStruct((seq, 1024), BF), jax.ShapeDtypeStruct((nb, RET_HEADS, dk, dv), BF)],
        scratch_shapes=[pltpu.VMEM((RET_HEADS, dk, dv), F32)])


def _mix_fwd(oat, yrp, gates, bg, x, w_att, w_ret, w_out):
    seq = x.shape[0]
    tm = min(TOKEN_TILE, seq)

    def body(oat_ref, yrp_ref, gates_ref, bg_ref, x_ref, wa_ref, wr_ref, wo_ref, ya_ref, yr_ref, mg_ref, h1_ref):
        ya = _tn(oat_ref[...], wa_ref[...])
        yr = _nn(yrp_ref[...], wr_ref[...])
        gt = _sigmoid(gates_ref[...].astype(F32) + bg_ref[...])
        merged = (gt[:, :D_MODEL] * ya + gt[:, D_MODEL:] * yr).astype(BF)
        ya_ref[...] = ya.astype(BF)
        yr_ref[...] = yr.astype(BF)
        mg_ref[...] = merged
        h1_ref[...] = x_ref[...] + _nn(merged, wo_ref[...])

    return pl.pallas_call(
        body, name="mix_fwd", grid=(seq // tm,),
        in_specs=[_cols(512, tm), _rows(tm, 1024), _rows(tm, 2048), VMEM_FULL, _rows(tm, D_MODEL),
                  VMEM_FULL, VMEM_FULL, VMEM_FULL],
        out_specs=[_rows(tm, D_MODEL)] * 4,
        out_shape=[jax.ShapeDtypeStruct((seq, D_MODEL), BF)] * 3 + [jax.ShapeDtypeStruct((seq, D_MODEL), F32)],
        compiler_params=_params(1),
    )(oat, yrp, gates, bg, x, w_att, w_ret, w_out)


def _mlp_loss_step(h1, g2, g3, target, w_ff1, w_ff2):
    seq = h1.shape[0]
    tm = min(MLP_TOKEN_TILE, seq)
    n_chunks = len(FF_CHUNKS) - 1

    def body(h1_ref, g2_ref, g3_ref, t_ref, w1_hbm, w2_hbm,
             xn2_ref, hdn_ref, dh2_ref, du_ref, dh1_ref, loss_ref, dg3_ref, dg2_ref,
             w1_vmem, w1_sems, w2_vmem, w2_sems, relu_u):
        w1 = _Resident(w1_hbm, w1_vmem, w1_sems, FF_CHUNKS)
        w2 = _Resident(w2_hbm, w2_vmem, w2_sems, FF_CHUNKS)
        _Resident.load(w1, w2)

        @pl.when(pl.program_id(0) == 0)
        def _():
            loss_ref[...] = jnp.zeros_like(loss_ref)
            dg3_ref[...] = jnp.zeros_like(dg3_ref)
            dg2_ref[...] = jnp.zeros_like(dg2_ref)

        h1v = h1_ref[...]
        r2 = lax.rsqrt(jnp.mean(h1v * h1v, axis=-1, keepdims=True) + NORM_EPS)
        xn2 = (h1v * r2 * g2_ref[...]).astype(BF)
        xn2_ref[...] = xn2
        h2 = h1v
        for c in range(n_chunks):
            cols = slice(FF_CHUNKS[c], FF_CHUNKS[c + 1])
            a = jnp.maximum(_nt(xn2, w1.chunk(c)[...]), 0.0)
            relu_u[:, cols] = a.astype(BF)
            hdn = jnp.square(a).astype(BF)
            hdn_ref[:, cols] = hdn
            h2 = h2 + _nn(hdn, w2.chunk(c)[...])
        r3 = lax.rsqrt(jnp.mean(h2 * h2, axis=-1, keepdims=True) + NORM_EPS)
        hn = h2 * r3
        err = hn * g3_ref[...] - t_ref[...]
        loss_ref[...] += jnp.sum(err * err) * (0.5 / D_MODEL)
        dy = err * (1.0 / D_MODEL)
        dg3_ref[...] += jnp.sum(dy * hn, axis=0, keepdims=True)
        z = dy * g3_ref[...]
        dh2 = r3 * (z - hn * jnp.mean(z * hn, axis=-1, keepdims=True))
        dh2b = dh2.astype(BF)
        dh2_ref[...] = dh2b
        dxn2 = jnp.zeros_like(dh2)
        for c in range(n_chunks):
            cols = slice(FF_CHUNKS[c], FF_CHUNKS[c + 1])
            du = (_nt(dh2b, w2.chunk(c)[...]) * (2.0 * relu_u[:, cols].astype(F32))).astype(BF)
            du_ref[:, cols] = du
            dxn2 = dxn2 + _nn(du, w1.chunk(c)[...])
        dnorm, dg = _rms_bwd(dxn2, h1v, g2_ref[...])
        dh1_ref[...] = dh2 + dnorm
        dg2_ref[...] += dg

    gain_acc = pl.BlockSpec((1, D_MODEL), lambda i: (0, 0))
    return pl.pallas_call(
        body, name="mlp_loss_step", grid=(seq // tm,),
        in_specs=[_rows(tm, D_MODEL), VMEM_FULL, VMEM_FULL, _rows(tm, D_MODEL), ANY, ANY],
        out_specs=[_rows(tm, D_MODEL), _rows(tm, D_FF), _rows(tm, D_MODEL), _rows(tm, D_FF), _rows(tm, D_MODEL),
                   pl.BlockSpec((1, LANE), lambda i: (0, 0)), gain_acc, gain_acc],
        out_shape=[jax.ShapeDtypeStruct((seq, D_MODEL), BF), jax.ShapeDtypeStruct((seq, D_FF), BF),
                   jax.ShapeDtypeStruct((seq, D_MODEL), BF), jax.ShapeDtypeStruct((seq, D_FF), BF),
                   jax.ShapeDtypeStruct((seq, D_MODEL), F32), jax.ShapeDtypeStruct((1, LANE), F32),
                   jax.ShapeDtypeStruct((1, D_MODEL), F32), jax.ShapeDtypeStruct((1, D_MODEL), F32)],
        scratch_shapes=(_Resident.scratch(w_ff1, FF_CHUNKS) + _Resident.scratch(w_ff2, FF_CHUNKS)
                        + [pltpu.VMEM((tm, D_FF), BF)]),
        compiler_params=_params(1),
    )(h1, g2, g3, target, w_ff1, w_ff2)


def _rms_bwd(dxn, xin, gain):
    r = lax.rsqrt(jnp.mean(xin * xin, axis=-1, keepdims=True) + NORM_EPS)
    xhat = xin * r
    z = dxn * gain
    dxin = r * (z - xhat * jnp.mean(z * xhat, axis=-1, keepdims=True))
    return dxin, jnp.sum(dxn * xhat, axis=0, keepdims=True)


def _mm_tn(a, b, name, a_is_t=False, comm=None):
    kdim, seq = a.shape if a_is_t else a.shape[::-1]
    ndim = b.shape[1]
    ts = min(TN_TOKEN_TILE, seq)
    tk = min(kdim, 1024)
    tn = min(ndim, 1024)
    n_steps = seq // ts

    def body(a_ref, b_ref, o_ref):
        @pl.when(pl.program_id(2) == 0)
        def _():
            o_ref[...] = jnp.zeros_like(o_ref)

        mm = _nn if a_is_t else _tn
        o_ref[...] += mm(a_ref[...].astype(BF), b_ref[...].astype(BF))

    a_spec = (pl.BlockSpec((tk, ts), lambda k, n, s: (k, s)) if a_is_t
              else pl.BlockSpec((ts, tk), lambda k, n, s: (s, k)))
    res = _call(
        body, (a, b), comm, name=name, grid=(kdim // tk, ndim // tn, n_steps),
        in_specs=[a_spec, pl.BlockSpec((ts, tn), lambda k, n, s: (s, n))],
        out_specs=pl.BlockSpec((tk, tn), lambda k, n, s: (k, n)),
        out_shape=jax.ShapeDtypeStruct((kdim, ndim), F32))
    return res[0] if comm is None else res


def _dw_in_t(pieces, is_t, xn1, comm=None):
    seq = xn1.shape[0]
    ts = seq
    tr = 256
    n_blk = [(p.shape[0] if t else p.shape[1]) // tr for p, t in zip(pieces, is_t)]
    offs = [sum(n_blk[:i]) for i in range(len(pieces) + 1)]
    n_p = len(pieces)

    def body(*refs):
        piece_refs, xn_ref, o_ref = refs[:n_p], refs[n_p], refs[n_p + 1]
        r, s = pl.program_id(0), pl.program_id(1)

        @pl.when(s == 0)
        def _():
            o_ref[...] = jnp.zeros_like(o_ref)

        for p in range(n_p):
            @pl.when((r >= offs[p]) & (r < offs[p + 1]))
            def _(p=p):
                mm = _nn if is_t[p] else _tn
                o_ref[...] += mm(piece_refs[p][...], xn_ref[pl.ds(pl.multiple_of(s * ts, ts), ts), :])

    def piece_spec(p):
        def index(r, s):
            inside = (r >= offs[p]) & (r < offs[p + 1])
            tok, blk = jnp.where(inside, s, 0), jnp.clip(r - offs[p], 0, n_blk[p] - 1)
            return (blk, tok) if is_t[p] else (tok, blk)
        return pl.BlockSpec((tr, ts) if is_t[p] else (ts, tr), index)

    return _call(
        body, (*pieces, xn1), comm, name="dw_in", grid=(offs[-1], seq // ts),
        in_specs=[piece_spec(p) for p in range(n_p)] + [VMEM_FULL],
        out_specs=pl.BlockSpec((tr, D_MODEL), lambda r, s: (r, 0)),
        out_shape=jax.ShapeDtypeStruct((offs[-1] * tr, D_MODEL), F32))


def _mix_bwd(dh1, ya, yr, gates, bg, merged, oat, yrp, w_att, w_ret, w_out, comm=None):
    seq = dh1.shape[0]
    tm = min(TOKEN_TILE, seq)

    def body(dh1_ref, ya_ref, yr_ref, gates_ref, bg_ref, mg_ref, oat_ref, yrp_ref, wa_ref, wr_ref, wo_ref,
             dgates_ref, doa_ref, dyrp_ref, db_ref, dwa_ref, dwr_ref, dwo_ref):
        @pl.when(pl.program_id(0) == 0)
        def _():
            for ref in (db_ref, dwa_ref, dwr_ref, dwo_ref):
                ref[...] = jnp.zeros_like(ref)

        dh1b = dh1_ref[...].astype(BF)
        dm = _nt(dh1b, wo_ref[...])
        gt = _sigmoid(gates_ref[...].astype(F32) + bg_ref[...])
        ga, gr = gt[:, :D_MODEL], gt[:, D_MODEL:]
        dya = (dm * ga).astype(BF)
        dyr = (dm * gr).astype(BF)
        dga = dm * ya_ref[...].astype(F32) * ga * (1.0 - ga)
        dgr = dm * yr_ref[...].astype(F32) * gr * (1.0 - gr)
        dgates_ref[:, :D_MODEL] = dga.astype(BF)
        dgates_ref[:, D_MODEL:] = dgr.astype(BF)
        db_ref[:, :D_MODEL] += jnp.sum(dga, axis=0, keepdims=True)
        db_ref[:, D_MODEL:] += jnp.sum(dgr, axis=0, keepdims=True)
        doa_ref[...] = _nt(wa_ref[...], dya).astype(BF)
        dyrp_ref[...] = _nt(dyr, wr_ref[...]).astype(BF)
        dwo_ref[...] += _tn(mg_ref[...], dh1b)
        dwr_ref[...] += _tn(yrp_ref[...], dyr)
        dwa_ref[...] += _nn(oat_ref[...], dya)

    acc = lambda r, c: pl.BlockSpec((r, c), lambda i: (0, 0))
    return _call(
        body, (dh1, ya, yr, gates, bg, merged, oat, yrp, w_att, w_ret, w_out), comm, name="mix_bwd",
        grid=(seq // tm,),
        in_specs=[_rows(tm, D_MODEL), _rows(tm, D_MODEL), _rows(tm, D_MODEL), _rows(tm, 2048), VMEM_FULL,
                  _rows(tm, D_MODEL), _cols(512, tm), _rows(tm, 1024), VMEM_FULL, VMEM_FULL, VMEM_FULL],
        out_specs=[_rows(tm, 2048), _cols(512, tm), _rows(tm, 1024), acc(1, 2048), acc(512, D_MODEL),
                   acc(1024, D_MODEL), acc(D_MODEL, D_MODEL)],
        out_shape=[jax.ShapeDtypeStruct((seq, 2048), BF), jax.ShapeDtypeStruct((512, seq), BF),
                   jax.ShapeDtypeStruct((seq, 1024), BF), jax.ShapeDtypeStruct((1, 2048), F32),
                   jax.ShapeDtypeStruct((512, D_MODEL), F32), jax.ShapeDtypeStruct((1024, D_MODEL), F32),
                   jax.ShapeDtypeStruct((D_MODEL, D_MODEL), F32)])


def _seq_bwd(qt, kv, kvt, dot, sinks, qkr, vr, gr, ret, dyrp, states, gain, rtabs, tabs, comm=None):
    seq = kv.shape[0]
    c = ATT_BLOCK
    assert c == RET_CHUNK
    d = ATT_HEAD_DIM
    nb = seq // c
    dk, dv = RET_KEY_DIM, RET_VAL_DIM
    ca, sa, sb, cr, sr, ct, st = tabs

    def body(sink_ref, qt_ref, kvc_ref, kvp_ref, kvtc_ref, kvtp_ref, dot_ref, ca_ref, sa_ref, sb_ref, ct_ref, st_ref,
             qk_ref, v_ref, g_ref, ret_ref, dyp_ref, st8_ref, gain_ref, intra_ref, qd_ref, kd_ref, cr_ref, sr_ref,
             dqt_ref, dkv_ref, dsink_ref, dall_ref, dgain_ref, carry, dstate):
        step = pl.program_id(0)
        i = nb - 1 - step

        @pl.when(step == 0)
        def _():
            carry[...] = jnp.zeros_like(carry)
            dsink_ref[...] = jnp.zeros_like(dsink_ref)
            dstate[...] = jnp.zeros_like(dstate)
            dgain_ref[...] = jnp.zeros_like(dgain_ref)

        kwin = jnp.concatenate([kvp_ref[:, :LANE], kvc_ref[:, :LANE]], axis=0)
        vwin = jnp.concatenate([kvp_ref[:, LANE:], kvc_ref[:, LANE:]], axis=0)
        ktwin = jnp.concatenate([kvtp_ref[:LANE, :], kvtc_ref[:LANE, :]], axis=1)
        ctv, stv = ct_ref[...], st_ref[...]
        lane1 = lax.broadcasted_iota(jnp.int32, (1, LANE), 1)
        dkw = jnp.zeros((2 * c, LANE), F32)
        dvw = jnp.zeros((2 * c, LANE), F32)
        dsink = jnp.zeros((1, LANE), F32)
        for g in (0, 1):
            qpad, dopad = _att_group(qt_ref, g), _att_group(dot_ref, g)
            prob, psink = _att_probs(kwin, qpad, sink_ref, g, i)
            dprob = _nn(vwin, dopad)
            drow = jnp.sum(dprob * prob, axis=0, keepdims=True)
            ds = (prob * (dprob - drow) * ATT_SCALE).astype(BF)
            dqg = _nn(ktwin[d * g:d * (g + 1), :], ds)
            dkw = dkw + _nt(ds, qpad)
            dvw = dvw + _nt(prob.astype(BF), dopad)
            dsink_lanes = psink * drow
            for j in range(ATT_Q_PER_KV):
                h = ATT_Q_PER_KV * g + j
                dqt_ref[d * h:d * (h + 1), :] = _rope_att_rows(dqg[:, c * j:c * (j + 1)], ctv, stv, -1.0).astype(BF)
                dsink = dsink + jnp.where(lane1 == h, -jnp.sum(dsink_lanes[:, c * j:c * (j + 1)]), 0.0)
        dsink_ref[...] += dsink
        total = carry[...] + jnp.concatenate([dkw[c:], dvw[c:]], axis=1)
        dkv_ref[:, :LANE] = _rope_att_t(total[:, :LANE], ca_ref[...], sa_ref[...], sb_ref[...]).astype(BF)
        dkv_ref[:, LANE:] = total[:, LANE:].astype(BF)
        carry[...] = jnp.concatenate([dkw[:c], dvw[:c]], axis=1)

        crv, srv = cr_ref[...], sr_ref[...]
        for h in range(RET_HEADS):
            vs = slice(dv * h, dv * (h + 1))
            qh = qk_ref[:, dk * h:dk * (h + 1)]
            kh = qk_ref[:, 512 + dk * h:512 + dk * (h + 1)]
            vh = v_ref[:, vs]
            out = ret_ref[:, vs]
            g = g_ref[:, vs].astype(F32)
            dyp = dyp_ref[:, vs].astype(F32)
            gain_h = gain_ref[:, vs]
            mu = jnp.mean(out, axis=-1, keepdims=True)
            dev = out - mu
            rstd = lax.rsqrt(jnp.mean(dev * dev, axis=-1, keepdims=True) + GN_EPS)
            yn = dev * rstd
            sg = _sigmoid(g)
            dg = dyp * (yn * gain_h) * (sg * (1.0 + g * (1.0 - sg)))
            dy = dyp * (g * sg)
            dgain_ref[:, vs] += jnp.sum(dy * yn, axis=0, keepdims=True)
            dyn = dy * gain_h
            dout = rstd * (dyn - jnp.mean(dyn, axis=-1, keepdims=True)
                           - yn * jnp.mean(dyn * yn, axis=-1, keepdims=True))
            doutb = dout.astype(BF)
            sc = st8_ref[0, h]
            dsp = dstate[h]
            dspb = dsp.astype(BF)
            intra, qdv, kdv = intra_ref[h], qd_ref[h], kd_ref[h]
            att = _nt(qh, kh) * intra
            dab = (_nt(doutb, vh) * intra).astype(BF)
            qdec = (qh.astype(F32) * qdv).astype(BF)
            kdec = (kh.astype(F32) * kdv).astype(BF)
            dq = _nn(dab, kh) + _nt(doutb, sc) * qdv
            dkk = _tn(dab, qh) + _nt(vh, dspb) * kdv
            dvv = _tn(att.astype(BF), doutb) + _nn(kdec, dspb)
            dstate[h] = dsp * RET_CHUNK_DECAY[h] + _tn(qdec, doutb)
            dall_ref[:, dk * h:dk * (h + 1)] = _rope_ret_t(dq, crv, srv).astype(BF)
            dall_ref[:, 512 + dk * h:512 + dk * (h + 1)] = (_rope_ret_t(dkk, crv, srv) * RET_SCALE).astype(BF)
            dall_ref[:, 1024 + dv * h:1024 + dv * (h + 1)] = dvv.astype(BF)
            dall_ref[:, 2048 + dv * h:2048 + dv * (h + 1)] = dg.astype(BF)

    cur = lambda s: nb - 1 - s
    prev = lambda s: jnp.maximum(nb - 2 - s, 0)
    rows = lambda w: pl.BlockSpec((c, w), lambda s: (cur(s), 0))
    cols = lambda h: pl.BlockSpec((h, c), lambda s: (0, cur(s)))
    acc = lambda w: pl.BlockSpec((1, w), lambda s: (0, 0))
    return _call(
        body, (sinks, qt, kv, kv, kvt, kvt, dot, ca, sa, sb, ct, st, qkr, vr, gr, ret, dyrp, states, gain, *rtabs,
               cr, sr), comm, name="seq_bwd", grid=(nb,),
        in_specs=[SMEM_FULL, cols(512), rows(256), pl.BlockSpec((c, 256), lambda s: (prev(s), 0)), cols(256),
                  pl.BlockSpec((256, c), lambda s: (0, prev(s))), cols(512), rows(LANE), rows(LANE), rows(LANE),
                  cols(8), cols(8), rows(1024), rows(1024), rows(1024), rows(1024), rows(1024),
                  pl.BlockSpec((1, RET_HEADS, dk, dv), lambda s: (cur(s), 0, 0, 0)),
                  VMEM_FULL, VMEM_FULL, VMEM_FULL, VMEM_FULL, rows(LANE), rows(LANE)],
        out_specs=[cols(512), rows(256), acc(LANE), rows(3072), acc(1024)],
        out_shape=[jax.ShapeDtypeStruct((512, seq), BF), jax.ShapeDtypeStruct((seq, 256), BF),
                   jax.ShapeDtypeStruct((1, LANE), F32), jax.ShapeDtypeStruct((seq, 3072), BF),
                   jax.ShapeDtypeStruct((1, 1024), F32)],
        scratch_shapes=[pltpu.VMEM((c, 256), F32), pltpu.VMEM((RET_HEADS, dk, dv), F32)])


def _inproj_bwd(dqa, dkva, dret, dgates, dh1, x, g1, w_in, comm=None):
    seq = x.shape[0]
    tm = min(TOKEN_TILE, seq)

    def body(dqa_ref, dkva_ref, dret_ref, dgates_ref, dh1_ref, x_ref, g_ref, w_hbm, dx_ref, dg1_ref, w_vmem, w_sems):
        w = _Resident(w_hbm, w_vmem, w_sems, IN_CHUNKS_BWD)
        _Resident.load(w)

        @pl.when(pl.program_id(0) == 0)
        def _():
            dg1_ref[...] = jnp.zeros_like(dg1_ref)

        dxn = _tn(dqa_ref[...], w.chunk(0)[...])
        dxn = dxn + _nn(dkva_ref[...], w.chunk(1)[...])
        dxn = dxn + _nn(dret_ref[...], w.chunk(2)[...])
        dxn = dxn + _nn(dgates_ref[...], w.chunk(3)[...])
        dnorm, dg = _rms_bwd(dxn, x_ref[...], g_ref[...])
        dx_ref[...] = dh1_ref[...] + dnorm
        dg1_ref[...] += dg

    return _call(
        body, (dqa, dkva, dret, dgates, dh1, x, g1, w_in), comm, name="inproj_bwd", grid=(seq // tm,),
        in_specs=[_cols(512, tm), _rows(tm, 256), _rows(tm, 3072), _rows(tm, 2048), _rows(tm, D_MODEL),
                  _rows(tm, D_MODEL), VMEM_FULL, ANY],
        out_specs=[_rows(tm, D_MODEL), pl.BlockSpec((1, D_MODEL), lambda i: (0, 0))],
        out_shape=[jax.ShapeDtypeStruct((seq, D_MODEL), F32), jax.ShapeDtypeStruct((1, D_MODEL), F32)],
        scratch_shapes=_Resident.scratch(w_in, IN_CHUNKS_BWD))


def _local_step(x, target, g1, bg, sinks, gain, g2, g3, w_in, shards):
    seq = x.shape[0]
    ca, sa, sb, cr, sr, ct, st = _rope_tables(seq)
    rtabs = _ret_tables()
    scatter = _Scatter

    comm_mid, unpack_mid = _gather(shards, ("w_att_up", "w_ret_up", "w_out"), relay=True)
    comm_ff1, unpack_ff1 = _gather(shards, ("w_ff1",), relay=True)
    (xn1, qt, kv, kvt, qkr, vr, gr, gates), got = _inproj_fwd(x, g1, w_in, (ca, sa, sb, cr, sr, ct, st),
                                                              _join(comm_mid, comm_ff1))
    w_att, w_ret, w_out = unpack_mid(got[:3])
    (w_ff1,) = unpack_ff1(got[3:])
    comm, unpack = _gather(shards, ("w_ff2",), relay=True)
    (oa, ret, yrp, states), got = _seq_fwd(qt, kv, kvt, sinks, qkr, vr, gr, gain, rtabs, comm)
    (w_ff2,) = unpack(got)
    ya, yr, merged, h1 = _mix_fwd(oa, yrp, gates, bg, x, w_att, w_ret, w_out)
    xn2, hdn, dh2, du, dh1, loss, dg3, dg2 = _mlp_loss_step(h1, g2, g3, target, w_ff1, w_ff2)

    ff2 = scatter(dict(w_ff2=_mm_tn(hdn, dh2, "dw_ff2")), "ff2")
    d_ff1, r1 = _mm_tn(du, xn2, "dw_ff1", comm=ff2.pair_comm())
    ff2.pair_done(r1)
    ff1 = scatter(dict(w_ff1=d_ff1), "ff1")
    (dgates, doa, dyrp, db, d_att, d_ret, d_out), r1 = _mix_bwd(dh1, ya, yr, gates, bg, merged, oa, yrp,
                                                                w_att, w_ret, w_out, ff1.pair_comm())
    ff1.pair_done(r1)
    mid = scatter(dict(w_att_up=d_att, w_ret_up=d_ret, w_out=d_out), "mid")
    (dqa, dkva, dsink, dret, dgain), got = _seq_bwd(
        qt, kv, kvt, doa, sinks, qkr, vr, gr, ret, dyrp, states, gain, rtabs, (ca, sa, sb, cr, sr, ct, st),
        _join(_join(ff2.chip_comm(), ff1.chip_comm()), mid.pair_comm()))
    ff2.chip_done(got[:1])
    ff1.chip_done(got[1:2])
    mid.pair_done(got[2:])
    d_in, r2 = _dw_in_t((dqa, dkva, dret, dgates), (True, False, False, False), xn1, mid.chip_comm())
    mid.chip_done(r2)
    win = scatter(dict(w_in=d_in), "in")
    win.pair_done(_comm_alone(win.pair_comm(), "pair_exchange_in"))
    (dx, dg1), r2 = _inproj_bwd(dqa, dkva, dret, dgates, dh1, x, g1, w_in, win.chip_comm())
    win.chip_done(r2)
    small = dict(norm_mix_gain=dg1, b_gates=db, attn_sinks=dsink, ret_gn_gain=dgain, norm_mlp_gain=dg2,
                 norm_final_gain=dg3)
    return loss, dx, small, (win, mid, ff1, ff2)


def _coords():
    return lax.axis_index("x"), lax.axis_index("y"), lax.axis_index("c")


def _flip(v, bit):
    return 1 - v if bit else v


def _xor(a, b):
    return a + b - 2 * a * b


def _gather_relay_comm(shards):
    n = len(shards)

    def parts(ins, outs, sems):
        send_sems, recv_sems, local_sems = sems
        x, y, c = _coords()
        me, sib = (x, y, c), (x, y, 1 - c)

        def ring(core):
            return ((_xor(x, core), _xor(y, 1 - core), core), (_xor(x, 1 - core), _xor(y, core), core),
                    (1 - x, 1 - y, core))

        def slot(a, blk):
            return outs[a].at[4 * blk[0] + 2 * blk[1] + blk[2]]

        def copy(a, k, blk, to, src=None):
            return pltpu.make_async_remote_copy(
                src_ref=slot(a, blk) if src is None else src, dst_ref=slot(a, blk),
                send_sem=send_sems.at[a, k], recv_sem=recv_sems.at[a, k], device_id=to, device_id_type=MESH)

        mine = [pltpu.make_async_copy(ins[a], slot(a, me), local_sems.at[a]) for a in range(n)]
        return me, sib, ring, copy, mine

    def first_copies(ins, outs, sems):
        me, sib, ring, copy, mine = parts(ins, outs, sems)
        src, dst, _ = ring(me[2])
        return mine, [copy(a, k, me, to, src=ins[a]) for a in range(n) for k, to in ((1, src), (2, dst), (0, sib))]

    def start(ins, outs, sems):
        mine, first = first_copies(ins, outs, sems)
        for cp in mine + first:
            cp.start()

    def relay(ins, outs, sems):
        me, sib, ring, copy, _ = parts(ins, outs, sems)
        src, dst, _ = ring(me[2])
        for a in range(n):
            copy(a, 1, src, me).wait_recv()
            copy(a, 3, src, dst).start()
            copy(a, 4, src, sib).start()

    def finish(ins, outs, sems):
        me, sib, ring, copy, _ = parts(ins, outs, sems)
        mine, first = first_copies(ins, outs, sems)
        src, dst, diag = ring(me[2])
        passed = [copy(a, k, src, to) for a in range(n) for k, to in ((3, dst), (4, sib))]
        for k, blk in ((2, dst), (3, diag)):
            for a in range(n):
                copy(a, k, blk, me).wait_recv()
                fwd = copy(a, k + 3, blk, sib)
                fwd.start()
                passed.append(fwd)
        s_src, s_dst, s_diag = ring(sib[2])
        for a in range(n):
            for k, blk in ((0, sib), (4, s_src), (5, s_dst), (6, s_diag)):
                copy(a, k, blk, me).wait_recv()
        for cp in first + passed:
            cp.wait_send()
        for cp in mine:
            cp.wait()

    return _Comm(list(shards), [jax.ShapeDtypeStruct((N_DEV,) + s.shape, s.dtype) for s in shards],
                 [pltpu.SemaphoreType.DMA((n, 7)), pltpu.SemaphoreType.DMA((n, 7)), pltpu.SemaphoreType.DMA((n,))],
                 start, finish, relay)


def _gather_comm(shards):
    n = len(shards)

    def parts(ins, outs, sems):
        send_sems, recv_sems, local_sems = sems
        x, y, c = _coords()
        me, sib = (x, y, c), (x, y, 1 - c)
        chips = [(1 - x, y), (x, 1 - y), (1 - x, 1 - y)]

        def slot(a, blk):
            return outs[a].at[4 * blk[0] + 2 * blk[1] + blk[2]]

        def copy(a, k, blk, to, src=None):
            return pltpu.make_async_remote_copy(
                src_ref=slot(a, blk) if src is None else src, dst_ref=slot(a, blk),
                send_sem=send_sems.at[a, k], recv_sem=recv_sems.at[a, k], device_id=to, device_id_type=MESH)

        mine = [pltpu.make_async_copy(ins[a], slot(a, me), local_sems.at[a]) for a in range(n)]
        first = []
        for a in range(n):
            first.append(copy(a, 0, me, sib, src=ins[a]))
            first += [copy(a, 1 + j, me, (*chip, c), src=ins[a]) for j, chip in enumerate(chips)]
        return me, sib, chips, c, copy, mine, first

    def start(ins, outs, sems):
        *_, mine, first = parts(ins, outs, sems)
        for cp in mine + first:
            cp.start()

    def finish(ins, outs, sems):
        me, sib, chips, c, copy, mine, first = parts(ins, outs, sems)
        passed = []
        for j, chip in enumerate(chips):
            for a in range(n):
                copy(a, 1 + j, (*chip, c), me).wait_recv()
                fwd = copy(a, 4 + j, (*chip, c), sib)
                fwd.start()
                passed.append(fwd)
        for a in range(n):
            copy(a, 0, sib, me).wait_recv()
            for j, chip in enumerate(chips):
                copy(a, 4 + j, (*chip, 1 - c), me).wait_recv()
        for cp in first + passed:
            cp.wait_send()
        for cp in mine:
            cp.wait()

    return _Comm(list(shards), [jax.ShapeDtypeStruct((N_DEV,) + s.shape, s.dtype) for s in shards],
                 [pltpu.SemaphoreType.DMA((n, 7)), pltpu.SemaphoreType.DMA((n, 7)), pltpu.SemaphoreType.DMA((n,))],
                 start, finish)


COLUMN_SHARDED = ("w_in", "w_ff1")
COLUMN_RELAID = ("w_att_up",)


def _gather(shards, names, relay=False):
    def unpack(got):
        return [jnp.transpose(g, (1, 0, 2)).reshape(g.shape[1], N_DEV * g.shape[2]) if k in COLUMN_RELAID
                else g.reshape(N_DEV * g.shape[1], g.shape[2]) for k, g in zip(names, got)]

    return (_gather_relay_comm if relay else _gather_comm)([shards[k] for k in names]), unpack


def _pair_comm(grads):
    n = len(grads)

    def copies(g, r1, sems):
        send_sems, recv_sems = sems
        x, y, c = _coords()
        return [pltpu.make_async_remote_copy(
            src_ref=g[a].at[2 * j + (1 - c)], dst_ref=r1[a].at[j], send_sem=send_sems.at[a, j],
            recv_sem=recv_sems.at[a, j], device_id=(x, y, 1 - c), device_id_type=MESH)
            for a in range(n) for j in range(4)]

    def start(g, r1, sems):
        for cp in copies(g, r1, sems):
            cp.start()

    def finish(g, r1, sems):
        for cp in copies(g, r1, sems):
            cp.wait_recv()
        for cp in copies(g, r1, sems):
            cp.wait_send()

    return _Comm(list(grads), [jax.ShapeDtypeStruct((4,) + g.shape[1:], g.dtype) for g in grads],
                 [pltpu.SemaphoreType.DMA((n, 4)), pltpu.SemaphoreType.DMA((n, 4))], start, finish)


def _small_comm(small):
    def parts(ins, outs, sems):
        (small_ref,), (small_all,) = ins, outs
        ssend, srecv, lsem = sems
        x, y, c = _coords()
        me_idx = 4 * x + 2 * y + c
        own = pltpu.make_async_copy(small_ref, small_all.at[me_idx], lsem)
        sends, recvs = [], []
        for r in range(1, N_DEV):
            px, py, pc = _flip(x, r & 4), _flip(y, r & 2), _flip(c, r & 1)
            sends.append(pltpu.make_async_remote_copy(
                src_ref=small_ref, dst_ref=small_all.at[me_idx], send_sem=ssend.at[r - 1], recv_sem=srecv.at[r - 1],
                device_id=(px, py, pc), device_id_type=MESH))
            recvs.append(pltpu.make_async_remote_copy(
                src_ref=small_ref, dst_ref=small_all.at[4 * px + 2 * py + pc], send_sem=ssend.at[r - 1],
                recv_sem=srecv.at[r - 1], device_id=(px, py, pc), device_id_type=MESH))
        return own, sends, recvs

    def start(ins, outs, sems):
        own, sends, _ = parts(ins, outs, sems)
        own.start()
        for cp in sends:
            cp.start()

    def finish(ins, outs, sems):
        own, sends, recvs = parts(ins, outs, sems)
        for cp in recvs:
            cp.wait_recv()
        for cp in sends:
            cp.wait_send()
        own.wait()

    return _Comm([small], [jax.ShapeDtypeStruct((N_DEV,) + small.shape, small.dtype)],
                 [pltpu.SemaphoreType.DMA((N_DEV - 1,)), pltpu.SemaphoreType.DMA((N_DEV - 1,)),
                  pltpu.SemaphoreType.DMA], start, finish)


def _pair_sum(grads, r1s, c_arr, tag):
    n = len(grads)
    q = 1

    def body(c_ref, *refs):
        g, r, t = refs[:n], refs[n:2 * n], refs[2 * n:]
        for a in range(n):
            t[a][...] = (g[a][...] + r[a][...]).astype(t[a].dtype)

    def blk(arr):
        return (1, arr.shape[1] // q, arr.shape[2])

    grid_spec = pltpu.PrefetchScalarGridSpec(
        num_scalar_prefetch=1, grid=(4, q),
        in_specs=[pl.BlockSpec(blk(g), lambda j, s, c_ref: (2 * j + c_ref[0], s, 0)) for g in grads]
        + [pl.BlockSpec(blk(r), lambda j, s, c_ref: (j, s, 0)) for r in r1s],
        out_specs=[pl.BlockSpec(blk(r), lambda j, s, c_ref: (j, s, 0)) for r in r1s])
    return pl.pallas_call(
        body, name="pair_sum_" + tag, grid_spec=grid_spec,
        out_shape=[jax.ShapeDtypeStruct(r.shape, RS_PAYLOAD) for r in r1s],
        compiler_params=_params(2),
    )(c_arr, *grads, *r1s)


def _chip_comm(ts):
    n = len(ts)

    def copies(t, r2, sems):
        send_sems, recv_sems = sems
        x, y, c = _coords()
        out = []
        for a in range(n):
            for r in range(1, 4):
                tx, ty = _flip(x, r & 2), _flip(y, r & 1)
                out.append(pltpu.make_async_remote_copy(
                    src_ref=t[a].at[2 * tx + ty], dst_ref=r2[a].at[r - 1], send_sem=send_sems.at[a, r - 1],
                    recv_sem=recv_sems.at[a, r - 1], device_id=(tx, ty, c), device_id_type=MESH))
        return out

    def start(t, r2, sems):
        for cp in copies(t, r2, sems):
            cp.start()

    def finish(t, r2, sems):
        for cp in copies(t, r2, sems):
            cp.wait_recv()
        for cp in copies(t, r2, sems):
            cp.wait_send()

    return _Comm(list(ts), [jax.ShapeDtypeStruct((3,) + t.shape[1:], t.dtype) for t in ts],
                 [pltpu.SemaphoreType.DMA((n, 3)), pltpu.SemaphoreType.DMA((n, 3))], start, finish)


class _Scatter:
    def __init__(self, grads, tag):
        self.names, self.tag = tuple(grads), tag
        self.stacks = [jnp.transpose(g.reshape(g.shape[0], N_DEV, g.shape[1] // N_DEV), (1, 0, 2))
                       if k in COLUMN_RELAID else g.reshape(N_DEV, g.shape[0] // N_DEV, g.shape[1])
                       for k, g in grads.items()]

    def pair_comm(self):
        return _pair_comm(self.stacks)

    def pair_done(self, r1s):
        self.r1s = list(r1s)
        c_arr = jnp.reshape(lax.axis_index("c"), (1,)).astype(jnp.int32)
        self.ts = _pair_sum(self.stacks, self.r1s, c_arr, self.tag)

    def chip_comm(self):
        return _chip_comm(self.ts)

    def chip_done(self, r2s):
        self.r2s = list(r2s)


def _adamw(w, g, m, v):
    m = ADAM_B1 * m + (1.0 - ADAM_B1) * g
    v = ADAM_B2 * v + (1.0 - ADAM_B2) * jnp.square(g)
    m_hat = m / (1.0 - ADAM_B1 ** ADAM_STEP)
    v_hat = v / (1.0 - ADAM_B2 ** ADAM_STEP)
    delta = -ADAM_LR * (m_hat / (jnp.sqrt(v_hat) + ADAM_EPS) + ADAM_WD * w)
    return delta, m, v


ADAM_STEPS = 4


def _piece_specs(stacks):
    def rows(s):
        return s.shape[1] // ADAM_STEPS
    return ([pl.BlockSpec((1, rows(s), s.shape[2]), lambda i, idx_ref: (idx_ref[0], i, 0)) for s in stacks]
            + [pl.BlockSpec((1, rows(s), s.shape[2]), lambda i, idx_ref: (idx_ref[1], i, 0)) for s in stacks]
            + [pl.BlockSpec((3, rows(s), s.shape[2]), lambda i, idx_ref: (0, i, 0)) for s in stacks])


def _piece_sum(g0, r1, r2):
    return (((g0[0] + r1[0]) + r2[0].astype(F32)) + r2[1].astype(F32)) + r2[2].astype(F32)


def _shard_sum(stacks, r1s, r2s, idx_arr):
    n = len(stacks)

    def body(idx_ref, *refs):
        g0, r1, r2, outs = (refs[k * n:(k + 1) * n] for k in range(4))
        for a in range(n):
            outs[a][...] = _piece_sum(g0[a], r1[a], r2[a])

    out_specs = [pl.BlockSpec((s.shape[1] // ADAM_STEPS, s.shape[2]), lambda i, idx_ref: (i, 0)) for s in stacks]
    grid_spec = pltpu.PrefetchScalarGridSpec(num_scalar_prefetch=1, grid=(ADAM_STEPS,),
                                             in_specs=_piece_specs(stacks), out_specs=out_specs)
    return pl.pallas_call(
        body, name="shard_sum", grid_spec=grid_spec,
        out_shape=[jax.ShapeDtypeStruct(s.shape[1:], F32) for s in stacks],
        compiler_params=_params(1),
    )(idx_arr, *stacks, *r1s, *r2s)


def _adam_big(pieces, summed, ws, ms, vs, idx_arr):
    stacks, r1s, r2s = pieces
    n_p, n = len(stacks), len(ws)

    def body(idx_ref, *refs):
        it = iter(refs)
        g0, r1, r2, gs, w, m, v = ([next(it) for _ in range(k)] for k in (n_p, n_p, n_p, n - n_p, n, n, n))
        outs = list(it)
        for a in range(n):
            g = _piece_sum(g0[a], r1[a], r2[a]) if a < n_p else gs[a - n_p][...]
            delta, nm, nv = _adamw(w[a][...], g, m[a][...], v[a][...])
            outs[4 * a][...] = g
            outs[4 * a + 1][...] = delta
            outs[4 * a + 2][...] = nm
            outs[4 * a + 3][...] = nv

    def shard_spec(w):
        return pl.BlockSpec((w.shape[0] // ADAM_STEPS, w.shape[1]), lambda i, idx_ref: (i, 0))

    in_specs = _piece_specs(stacks) + [shard_spec(w) for w in ws[n_p:]] + [shard_spec(w) for w in ws] * 3
    out_specs = [shard_spec(w) for w in ws for _ in range(4)]
    grid_spec = pltpu.PrefetchScalarGridSpec(num_scalar_prefetch=1, grid=(ADAM_STEPS,), in_specs=in_specs,
                                             out_specs=out_specs)
    return pl.pallas_call(
        body, name="adam_big", grid_spec=grid_spec,
        out_shape=[jax.ShapeDtypeStruct(w.shape, F32) for w in ws for _ in range(4)],
        compiler_params=_params(1),
    )(idx_arr, *stacks, *r1s, *r2s, *summed, *ws, *ms, *vs)


def _adam_small(small_all, w, m, v):
    def body(all_ref, w_ref, m_ref, v_ref, g_ref, d_ref, nm_ref, nv_ref):
        g = all_ref[0]
        for k in range(1, N_DEV):
            g = g + all_ref[k]
        delta, nm, nv = _adamw(w_ref[...], g, m_ref[...], v_ref[...])
        g_ref[...] = g
        d_ref[...] = delta
        nm_ref[...] = nm
        nv_ref[...] = nv

    return pl.pallas_call(
        body, name="adam_small",
        in_specs=[VMEM_FULL] * 4, out_specs=[VMEM_FULL] * 4,
        out_shape=[jax.ShapeDtypeStruct(w.shape, F32)] * 4,
    )(small_all, w, m, v)


RS_PAYLOAD = BF


def _pack_rows(rows):
    rid = lax.broadcasted_iota(jnp.int32, (8, D_MODEL), 0)
    out = jnp.zeros((8, D_MODEL), F32)
    for i, r in enumerate(rows):
        out = jnp.where(rid == i, jnp.broadcast_to(r, (8, D_MODEL)), out)
    return out


def _small_rows(norm_mix_gain, b_gates, attn_sinks, ret_gn_gain, norm_mlp_gain, norm_final_gain):
    return [norm_mix_gain, b_gates[:, :D_MODEL], b_gates[:, D_MODEL:], ret_gn_gain, norm_mlp_gain,
            norm_final_gain.reshape(1, D_MODEL), jnp.pad(attn_sinks, ((0, 0), (0, D_MODEL - ATT_HEADS)))]


def _unpack_small(p):
    return dict(norm_mix_gain=p[0:1], b_gates=jnp.concatenate([p[1:2], p[2:3]], axis=1), ret_gn_gain=p[3:4],
                norm_mlp_gain=p[4:5], norm_final_gain=p[5], attn_sinks=p[6:7, :ATT_HEADS])


WEIGHTS = ("norm_mix_gain", "w_in", "b_gates", "attn_sinks", "ret_gn_gain", "w_att_up", "w_ret_up", "w_out",
           "norm_mlp_gain", "w_ff1", "w_ff2", "norm_final_gain")
BIG = ("w_in", "w_att_up", "w_ret_up", "w_out", "w_ff1", "w_ff2")


def kernel(x, norm_mix_gain, w_in, b_gates, attn_sinks, ret_gn_gain, w_att_up, w_ret_up, w_out, norm_mlp_gain, w_ff1, w_ff2, norm_final_gain, loss_target, m_norm_mix_gain, m_w_in, m_b_gates, m_attn_sinks, m_ret_gn_gain, m_w_att_up, m_w_ret_up, m_w_out, m_norm_mlp_gain, m_w_ff1, m_w_ff2, m_norm_final_gain, v_norm_mix_gain, v_w_in, v_b_gates, v_attn_sinks, v_ret_gn_gain, v_w_att_up, v_w_ret_up, v_w_out, v_norm_mlp_gain, v_w_ff1, v_w_ff2, v_norm_final_gain):
    w = dict(norm_mix_gain=norm_mix_gain, w_in=w_in, b_gates=b_gates, attn_sinks=attn_sinks, ret_gn_gain=ret_gn_gain,
             w_att_up=w_att_up, w_ret_up=w_ret_up, w_out=w_out, norm_mlp_gain=norm_mlp_gain, w_ff1=w_ff1,
             w_ff2=w_ff2, norm_final_gain=norm_final_gain)
    m = dict(norm_mix_gain=m_norm_mix_gain, w_in=m_w_in, b_gates=m_b_gates, attn_sinks=m_attn_sinks,
             ret_gn_gain=m_ret_gn_gain, w_att_up=m_w_att_up, w_ret_up=m_w_ret_up, w_out=m_w_out,
             norm_mlp_gain=m_norm_mlp_gain, w_ff1=m_w_ff1, w_ff2=m_w_ff2, norm_final_gain=m_norm_final_gain)
    v = dict(norm_mix_gain=v_norm_mix_gain, w_in=v_w_in, b_gates=v_b_gates, attn_sinks=v_attn_sinks,
             ret_gn_gain=v_ret_gn_gain, w_att_up=v_w_att_up, w_ret_up=v_w_ret_up, w_out=v_w_out,
             norm_mlp_gain=v_norm_mlp_gain, w_ff1=v_w_ff1, w_ff2=v_w_ff2, norm_final_gain=v_norm_final_gain)

    shards = {k: (w[k][0].T if k in COLUMN_SHARDED else w[k][0]).astype(BF) for k in BIG}
    comm, unpack = _gather(shards, ("w_in",), relay=True)
    (w_in_full,) = unpack(_comm_alone(comm, "allgather_w_in"))

    loss_p, dx, small, groups = _local_step(
        x[0], loss_target[0], norm_mix_gain, b_gates, attn_sinks[0], ret_gn_gain, norm_mlp_gain,
        norm_final_gain.reshape(1, D_MODEL), w_in_full, shards)

    by_name = {}
    for grp in groups:
        for k, stack, r1, r2 in zip(grp.names, grp.stacks, grp.r1s, grp.r2s):
            by_name[k] = (stack, r1, r2)
    lane = lax.broadcasted_iota(jnp.int32, (1, D_MODEL), 1)
    sink_row = jnp.where(lane < ATT_HEADS, jnp.pad(small["attn_sinks"], ((0, 0), (0, D_MODEL - LANE))),
                         jnp.where(lane == ATT_HEADS, jnp.pad(loss_p, ((0, 0), (0, D_MODEL - LANE)), mode="edge"),
                                   0.0))
    small_pack = _pack_rows([small["norm_mix_gain"], small["b_gates"][:, :D_MODEL], small["b_gates"][:, D_MODEL:],
                             small["ret_gn_gain"], small["norm_mlp_gain"], small["norm_final_gain"], sink_row])
    xi, yi, ci = _coords()
    (small_all,) = _comm_alone(_small_comm(small_pack), "small_exchange")
    idx_arr = jnp.stack([4 * xi + 2 * yi + ci, 2 * xi + yi]).astype(jnp.int32)
    in_pieces = ["w_att_up", "w_ret_up", "w_out", "w_ff2", "w_in"]
    in_sum = ["w_ff1"]
    order = in_pieces + in_sum
    summed_t = _shard_sum(*[[by_name[k][i] for k in in_sum] for i in range(3)], idx_arr)

    def shard(tree, k):
        return tree[k][0].T if k == "w_in" else tree[k][0]

    adam_out = _adam_big([[by_name[k][i] for k in in_pieces] for i in range(3)], [g.T for g in summed_t],
                         [shard(w, k) for k in order], [shard(m, k) for k in order], [shard(v, k) for k in order],
                         idx_arr)
    big_out = [adam_out[4 * order.index(k) + i].T if k == "w_in" else adam_out[4 * order.index(k) + i]
               for k in BIG for i in range(4)]
    sm_g, sm_d, sm_m, sm_v = _adam_small(small_all, _pack_rows(_small_rows(*[w[k] for k in WEIGHTS if k not in BIG])),
                                         _pack_rows(_small_rows(*[m[k] for k in WEIGHTS if k not in BIG])),
                                         _pack_rows(_small_rows(*[v[k] for k in WEIGHTS if k not in BIG])))

    loss = sm_g[6, ATT_HEADS]
    kinds = []
    for idx, packed in enumerate((sm_g, sm_d, sm_m, sm_v)):
        out = _unpack_small(packed)
        for a, k in enumerate(BIG):
            out[k] = big_out[4 * a + idx][None]
        kinds.append(out)
    return (loss, dx[None], *[kinds[0][k] for k in WEIGHTS], *[kinds[1][k] for k in WEIGHTS],
            *[kinds[2][k] for k in WEIGHTS], *[kinds[3][k] for k in WEIGHTS])
```

```python
import functools
import math

import jax
import jax.numpy as jnp
from jax import lax
from jax.experimental import pallas as pl
from jax.experimental.pallas import tpu as pltpu

F32 = jnp.float32
BF = jnp.bfloat16
MESH = pl.DeviceIdType.MESH

D_MODEL = 1024
ATT_HEADS = 8
ATT_HEAD_DIM = 64
ATT_BLOCK = 128
ROPE_DIM = 16
ROPE_THETA = 500000.0
RET_HEADS = 4
RET_KEY_DIM = 128
RET_VAL_DIM = 256
RET_CHUNK = 128
RET_ROT_BASE = 10000.0
D_FF = 4096
NORM_EPS = 1e-6
GN_EPS = 1e-6
NEG_INF = -1e30
ATT_SCALE = ATT_HEAD_DIM ** -0.5
RET_SCALE = RET_KEY_DIM ** -0.5

C_QA, C_KA, C_VA, C_QR, C_KR, C_VR, C_GR, C_GATES, C_END = 0, 512, 640, 768, 1280, 1792, 2816, 3840, 5888

ADAM_LR = 0.001
ADAM_B1 = 0.9
ADAM_B2 = 0.999
ADAM_EPS = 1e-08
ADAM_WD = 0.01
ADAM_STEP = 10

N_DEV = 8
LANE = 128
VMEM_LIMIT = 56 * 1024 * 1024
TOKEN_TILE = 512
MLP_TOKEN_TILE = 256
TN_TOKEN_TILE = 2048
FF_CHUNKS = (0, 1024, 2048, 3072, 4096)
IN_CHUNKS_FWD = (C_QA, C_KA, C_QR, C_KR, C_VR, C_GR, C_GATES, C_END)
IN_CHUNKS_BWD = (C_QA, C_KA, C_QR, C_GATES, C_END)

RET_LOG_GAMMA = tuple(math.log1p(-(2.0 ** (-5.0 - h))) for h in range(RET_HEADS))
RET_CHUNK_DECAY = tuple(math.exp(RET_CHUNK * lg) for lg in RET_LOG_GAMMA)

VMEM_FULL = pl.BlockSpec(memory_space=pltpu.VMEM)
SMEM_FULL = pl.BlockSpec(memory_space=pltpu.SMEM)
ANY = pl.BlockSpec(memory_space=pl.ANY)


def _params(n_axes):
    return pltpu.CompilerParams(dimension_semantics=("arbitrary",) * n_axes, vmem_limit_bytes=VMEM_LIMIT)


def _nn(a, b):
    return jnp.dot(a, b, preferred_element_type=F32)


def _nt(a, b):
    return lax.dot_general(a, b, (((1,), (1,)), ((), ())), preferred_element_type=F32)


def _tn(a, b):
    return lax.dot_general(a, b, (((0,), (0,)), ((), ())), preferred_element_type=F32)


def _sigmoid(v):
    return 1.0 / (1.0 + jnp.exp(-v))


def _rows(tile, width):
    return pl.BlockSpec((tile, width), lambda i: (i, 0))


def _cols(height, tile):
    return pl.BlockSpec((height, tile), lambda i: (0, i))


class _Resident:
    def __init__(self, hbm_ref, vmem_ref, sems, bounds):
        self.hbm, self.vmem, self.sems, self.bounds = hbm_ref, vmem_ref, sems, bounds

    @staticmethod
    def scratch(w, bounds):
        return [pltpu.VMEM(w.shape, w.dtype), pltpu.SemaphoreType.DMA((len(bounds) - 1,))]

    @staticmethod
    def load(*weights):
        @pl.when(pl.program_id(0) == 0)
        def _():
            copies = [w._copy(c) for w in weights for c in range(len(w.bounds) - 1)]
            for cp in copies:
                cp.start()
            for cp in copies:
                cp.wait()

    def _rows(self, c):
        return pl.ds(self.bounds[c], self.bounds[c + 1] - self.bounds[c])

    def _copy(self, c):
        return pltpu.make_async_copy(self.hbm.at[self._rows(c)], self.vmem.at[self._rows(c)], self.sems.at[c])

    def chunk(self, c):
        return self.vmem.at[self._rows(c)]


class _Comm:
    def __init__(self, inputs, out_shapes, scratch, start, finish, relay=None):
        self.inputs, self.out_shapes, self.scratch = inputs, out_shapes, scratch
        self.start, self.finish, self.relay = start, finish, relay


RELAY_AT = 0.75


def _join(a, b):
    na_in, na_out, na_sem = len(a.inputs), len(a.out_shapes), len(a.scratch)

    def both(name):
        def run(ins, outs, sems):
            for part, args in ((a, (ins[:na_in], outs[:na_out], sems[:na_sem])),
                               (b, (ins[na_in:], outs[na_out:], sems[na_sem:]))):
                if getattr(part, name) is not None:
                    getattr(part, name)(*args)
        return run

    return _Comm(list(a.inputs) + list(b.inputs), list(a.out_shapes) + list(b.out_shapes),
                 list(a.scratch) + list(b.scratch), both("start"), both("finish"),
                 both("relay") if a.relay or b.relay else None)


def _call(body, args, comm=None, *, name, grid, in_specs, out_specs, out_shape, scratch_shapes=()):
    params = _params(len(grid))
    if comm is None:
        return pl.pallas_call(body, name=name, grid=grid, in_specs=in_specs, out_specs=out_specs, out_shape=out_shape,
                              scratch_shapes=scratch_shapes, compiler_params=params)(*args), ()
    single = not isinstance(out_specs, (list, tuple))
    out_specs_l = [out_specs] if single else list(out_specs)
    out_shape_l = [out_shape] if single else list(out_shape)
    n_in, n_out, n_scr = len(in_specs), len(out_specs_l), len(scratch_shapes)
    n_cin, n_cout = len(comm.inputs), len(comm.out_shapes)

    def hosted(*refs):
        it = iter(refs)
        ins, cin, outs, cout, scr = ([next(it) for _ in range(k)] for k in (n_in, n_cin, n_out, n_cout, n_scr))
        sems = list(it)
        ids = [pl.program_id(k) for k in range(len(grid))]
        first = functools.reduce(jnp.logical_and, [i == 0 for i in ids])
        last = functools.reduce(jnp.logical_and, [i == g - 1 for i, g in zip(ids, grid)])

        @pl.when(first)
        def _():
            comm.start(cin, cout, sems)

        if comm.relay is not None:
            at = [int(grid[0] * RELAY_AT)] + [0] * (len(grid) - 1)

            @pl.when(functools.reduce(jnp.logical_and, [i == v for i, v in zip(ids, at)]))
            def _():
                comm.relay(cin, cout, sems)

        body(*ins, *outs, *scr)

        @pl.when(last)
        def _():
            comm.finish(cin, cout, sems)

    res = pl.pallas_call(
        hosted, name=name, grid=grid, in_specs=list(in_specs) + [ANY] * n_cin,
        out_specs=out_specs_l + [ANY] * n_cout, out_shape=out_shape_l + list(comm.out_shapes),
        scratch_shapes=list(scratch_shapes) + list(comm.scratch), compiler_params=params)(*args, *comm.inputs)
    return (res[0] if single else res[:n_out]), res[n_out:]


def _comm_alone(comm, name):
    n_cin, n_cout = len(comm.inputs), len(comm.out_shapes)

    def body(*refs):
        cin, cout, sems = refs[:n_cin], refs[n_cin:n_cin + n_cout], refs[n_cin + n_cout:]
        comm.start(cin, cout, sems)
        if comm.relay is not None:
            comm.relay(cin, cout, sems)
        comm.finish(cin, cout, sems)

    return pl.pallas_call(body, name=name, in_specs=[ANY] * n_cin, out_specs=[ANY] * n_cout,
                          out_shape=list(comm.out_shapes), scratch_shapes=list(comm.scratch))(*comm.inputs)


def _slabs(v, fn):
    return jnp.concatenate([fn(v[:, LANE * j:LANE * (j + 1)]) for j in range(v.shape[1] // LANE)], axis=1)


def _rope_att(v, ca, sa, sb):
    return _slabs(v, lambda t: t * ca + pltpu.roll(t, LANE - 8, 1) * sa + pltpu.roll(t, 8, 1) * sb)


def _rope_att_t(v, ca, sa, sb):
    return _slabs(v, lambda t: t * ca + pltpu.roll(t * sa, 8, 1) + pltpu.roll(t * sb, LANE - 8, 1))


def _rope_att_rows(v, ct, st, sign):
    parts = []
    for h in range(v.shape[0] // ATT_HEAD_DIM):
        r0 = ATT_HEAD_DIM * h
        x1, x2 = v[r0:r0 + 8], v[r0 + 8:r0 + 16]
        parts += [x1 * ct - sign * (x2 * st), x2 * ct + sign * (x1 * st), v[r0 + 16:r0 + ATT_HEAD_DIM]]
    return jnp.concatenate(parts, axis=0)


def _rope_ret(v, cr, sr):
    return _slabs(v, lambda t: t * cr + pltpu.roll(t, 64, 1) * sr)


def _rope_ret_t(v, cr, sr):
    return _slabs(v, lambda t: t * cr + pltpu.roll(t * sr, 64, 1))


def _rope_lane_tables():
    def inv(dim, theta):
        return theta ** (-jnp.arange(0, dim, 2, dtype=F32) / dim)

    inv_a, inv_r = inv(ROPE_DIM, ROPE_THETA), inv(RET_KEY_DIM, RET_ROT_BASE)
    half = ROPE_DIM // 2
    zeros = jnp.zeros((ATT_HEAD_DIM - ROPE_DIM,), F32)
    freq64 = jnp.concatenate([inv_a, inv_a, zeros])
    lo64 = jnp.concatenate([-jnp.ones((half,), F32), jnp.zeros((half,), F32), zeros])
    hi64 = jnp.concatenate([jnp.zeros((half,), F32), jnp.ones((half,), F32), zeros])
    sign_r = jnp.concatenate([-jnp.ones((64,), F32), jnp.ones((64,), F32)])
    rows = [jnp.tile(freq64, 2), jnp.tile(lo64, 2), jnp.tile(hi64, 2), jnp.tile(inv_r, 2), sign_r]
    lanes = jnp.stack(rows + [jnp.zeros((LANE,), F32)] * (8 - len(rows)))
    return lanes, jnp.broadcast_to(inv_a[:, None], (half, LANE))


def _prologue(x, g1, comm):
    seq = x.shape[0]
    tm = min(TOKEN_TILE, seq)

    def body(x_ref, g_ref, lanes_ref, freq_rows_ref, xn_ref, ca_ref, sa_ref, sb_ref, cr_ref, sr_ref, ct_ref, st_ref):
        xf = x_ref[...]
        r = lax.rsqrt(jnp.mean(xf * xf, axis=-1, keepdims=True) + NORM_EPS)
        xn_ref[...] = (xf * r * g_ref[...]).astype(BF)
        row0 = pl.program_id(0) * tm
        pos = (row0 + lax.broadcasted_iota(jnp.int32, (tm, LANE), 0)).astype(F32)
        ang_a = pos * lanes_ref[0:1, :]
        sin_a = jnp.sin(ang_a)
        ca_ref[...] = jnp.cos(ang_a)
        sa_ref[...] = sin_a * lanes_ref[1:2, :]
        sb_ref[...] = sin_a * lanes_ref[2:3, :]
        ang_r = pos * lanes_ref[3:4, :]
        cr_ref[...] = jnp.cos(ang_r)
        sr_ref[...] = jnp.sin(ang_r) * lanes_ref[4:5, :]
        pos_t = (row0 + lax.broadcasted_iota(jnp.int32, (8, tm), 1)).astype(F32)
        ang_t = pos_t * jnp.concatenate([freq_rows_ref[...]] * (tm // LANE), axis=1)
        ct_ref[...] = jnp.cos(ang_t)
        st_ref[...] = jnp.sin(ang_t)

    tab = _rows(tm, LANE)
    tab_t = _cols(8, tm)
    return _call(
        body, (x, g1, *_rope_lane_tables()), comm, name="prologue", grid=(seq // tm,),
        in_specs=[_rows(tm, D_MODEL), VMEM_FULL, VMEM_FULL, VMEM_FULL],
        out_specs=[_rows(tm, D_MODEL), tab, tab, tab, tab, tab, tab_t, tab_t],
        out_shape=[jax.ShapeDtypeStruct((seq, D_MODEL), BF)] + [jax.ShapeDtypeStruct((seq, LANE), F32)] * 5
        + [jax.ShapeDtypeStruct((8, seq), F32)] * 2)


def _ret_tables():
    c = RET_CHUNK
    lg = jnp.asarray(RET_LOG_GAMMA, F32)
    idx = jnp.arange(c, dtype=F32)
    diff = idx[:, None] - idx[None, :]
    intra = jnp.where(diff >= 0, jnp.exp(jnp.maximum(diff, 0.0) * lg[:, None, None]), 0.0)
    qd = jnp.exp((idx + 1.0)[None, :] * lg[:, None])[..., None]
    kd = jnp.exp((c - 1.0 - idx)[None, :] * lg[:, None])[..., None]
    return intra, jnp.broadcast_to(qd, (RET_HEADS, c, RET_KEY_DIM)), jnp.broadcast_to(kd, (RET_HEADS, c, RET_KEY_DIM))


def _inproj_fwd(xn1, w_in, tabs, comm=None):
    seq = xn1.shape[0]
    tm = min(TOKEN_TILE, seq)

    def body(xn_ref, w_hbm, ca_ref, sa_ref, sb_ref, cr_ref, sr_ref, ct_ref, st_ref,
             qt_ref, kv_ref, kvt_ref, qkr_ref, vr_ref, gr_ref, gates_ref, w_vmem, w_sems):
        w = _Resident(w_hbm, w_vmem, w_sems, IN_CHUNKS_FWD)
        _Resident.load(w)
        ca, sa, sb, cr, sr, ct, st = (ref[...] for ref in (ca_ref, sa_ref, sb_ref, cr_ref, sr_ref, ct_ref, st_ref))
        xn = xn_ref[...]
        qt_ref[...] = _rope_att_rows(_nt(w.chunk(0)[...], xn), ct, st, 1.0).astype(BF)
        w_kv = w.chunk(1)
        kvt = _nt(w_kv[...], xn)
        kvt_ref[:LANE, :] = _rope_att_rows(kvt[:LANE], ct, st, 1.0).astype(BF)
        kvt_ref[LANE:, :] = kvt[LANE:].astype(BF)
        kvn = _nt(xn, w_kv[...])
        kv_ref[:, :LANE] = _rope_att(kvn[:, :LANE], ca, sa, sb).astype(BF)
        kv_ref[:, LANE:] = kvn[:, LANE:].astype(BF)
        qr = _rope_ret(_nt(xn, w.chunk(2)[...]), cr, sr)
        qkr_ref[:, :512] = qr.astype(BF)
        kr = _rope_ret(_nt(xn, w.chunk(3)[...]), cr, sr) * RET_SCALE
        qkr_ref[:, 512:] = kr.astype(BF)
        vr_ref[...] = _nt(xn, w.chunk(4)[...]).astype(BF)
        gr_ref[...] = _nt(xn, w.chunk(5)[...]).astype(BF)
        gates_ref[...] = _nt(xn, w.chunk(6)[...]).astype(BF)

    tab = _rows(tm, LANE)
    tab_t = _cols(8, tm)
    return _call(
        body, (xn1, w_in, *tabs), comm, name="inproj_fwd", grid=(seq // tm,),
        in_specs=[_rows(tm, D_MODEL), ANY, tab, tab, tab, tab, tab, tab_t, tab_t],
        out_specs=[_cols(512, tm), _rows(tm, 256), _cols(256, tm), _rows(tm, 1024),
                   _rows(tm, 1024), _rows(tm, 1024), _rows(tm, 2048)],
        out_shape=[jax.ShapeDtypeStruct(s, BF) for s in ((512, seq), (seq, 256), (256, seq),
                                                         (seq, 1024), (seq, 1024), (seq, 1024), (seq, 2048))],
        scratch_shapes=_Resident.scratch(w_in, IN_CHUNKS_FWD))


ATT_Q_PER_KV = ATT_HEADS // 2


def _att_group(ref, kv):
    d = ATT_HEAD_DIM
    g = jnp.concatenate([ref[d * (ATT_Q_PER_KV * kv + j):d * (ATT_Q_PER_KV * kv + j + 1), :]
                         for j in range(ATT_Q_PER_KV)], axis=1)
    z = jnp.zeros_like(g)
    return jnp.concatenate([g, z] if kv == 0 else [z, g], axis=0)


def _att_probs(kwin, qpad, sink_ref, kv, i):
    c = ATT_BLOCK
    kj = lax.broadcasted_iota(jnp.int32, (2 * c, c), 0)
    qi = lax.broadcasted_iota(jnp.int32, (2 * c, c), 1)
    allowed = (kj > qi) & (kj <= qi + c) & ((kj >= c) | (i > 0))
    allowed = jnp.concatenate([allowed] * ATT_Q_PER_KV, axis=1)
    s = jnp.where(allowed, _nn(kwin, qpad) * ATT_SCALE, NEG_INF)
    sk = jnp.concatenate([jnp.full((1, c), sink_ref[ATT_Q_PER_KV * kv + j], F32) for j in range(ATT_Q_PER_KV)], axis=1)
    m = jnp.maximum(jnp.max(s, axis=0, keepdims=True), sk)
    pe = jnp.exp(s - m)
    psink = jnp.exp(sk - m)
    inv = 1.0 / (jnp.sum(pe, axis=0, keepdims=True) + psink)
    return pe * inv, psink * inv


def _seq_fwd(qt, kv, kvt, sinks, qkr, vr, gr, gain, rtabs, comm=None):
    seq = kv.shape[0]
    c = ATT_BLOCK
    assert c == RET_CHUNK
    d = ATT_HEAD_DIM
    nb = seq // c
    dk, dv = RET_KEY_DIM, RET_VAL_DIM

    def body(sink_ref, qt_ref, kvc_ref, kvp_ref, kvtc_ref, kvtp_ref, qk_ref, v_ref, g_ref, gain_ref, intra_ref,
             qd_ref, kd_ref, ot_ref, ret_ref, yrp_ref, st_ref, state):
        i = pl.program_id(0)
        kwin = jnp.concatenate([kvp_ref[:, :LANE], kvc_ref[:, :LANE]], axis=0)
        vtwin = jnp.concatenate([kvtp_ref[LANE:, :], kvtc_ref[LANE:, :]], axis=1)
        for g in (0, 1):
            prob, _ = _att_probs(kwin, _att_group(qt_ref, g), sink_ref, g, i)
            out = _nn(vtwin[d * g:d * (g + 1), :], prob.astype(BF))
            for j in range(ATT_Q_PER_KV):
                h = ATT_Q_PER_KV * g + j
                ot_ref[d * h:d * (h + 1), :] = out[:, c * j:c * (j + 1)].astype(BF)

        @pl.when(i == 0)
        def _():
            state[...] = jnp.zeros_like(state)

        for h in range(RET_HEADS):
            qh = qk_ref[:, dk * h:dk * (h + 1)]
            kh = qk_ref[:, 512 + dk * h:512 + dk * (h + 1)]
            vh = v_ref[:, dv * h:dv * (h + 1)]
            sh = state[h]
            shb = sh.astype(BF)
            st_ref[0, h] = shb
            att = _nt(qh, kh) * intra_ref[h]
            inner = _nn(att.astype(BF), vh)
            cross = _nn((qh.astype(F32) * qd_ref[h]).astype(BF), shb)
            out = inner + cross
            state[h] = sh * RET_CHUNK_DECAY[h] + _tn((kh.astype(F32) * kd_ref[h]).astype(BF), vh)
            ret_ref[:, dv * h:dv * (h + 1)] = out
            mu = jnp.mean(out, axis=-1, keepdims=True)
            dev = out - mu
            var = jnp.mean(dev * dev, axis=-1, keepdims=True)
            y = dev * lax.rsqrt(var + GN_EPS) * gain_ref[:, dv * h:dv * (h + 1)]
            g = g_ref[:, dv * h:dv * (h + 1)].astype(F32)
            yrp_ref[:, dv * h:dv * (h + 1)] = (g * _sigmoid(g) * y).astype(BF)

    prev = lambda i: jnp.maximum(i - 1, 0)
    return _call(
        body, (sinks, qt, kv, kv, kvt, kvt, qkr, vr, gr, gain, *rtabs), comm, name="seq_fwd", grid=(nb,),
        in_specs=[SMEM_FULL, _cols(512, c), _rows(c, 256), pl.BlockSpec((c, 256), lambda i: (prev(i), 0)),
                  _cols(256, c), pl.BlockSpec((256, c), lambda i: (0, prev(i))),
                  _rows(c, 1024), _rows(c, 1024), _rows(c, 1024), VMEM_FULL, VMEM_FULL, VMEM_FULL, VMEM_FULL],
        out_specs=[_cols(512, c), _rows(c, 1024), _rows(c, 1024),
                   pl.BlockSpec((1, RET_HEADS, dk, dv), lambda i: (i, 0, 0, 0))],
        out_shape=[jax.ShapeDtypeStruct((512, seq), BF), jax.ShapeDtypeStruct((seq, 1024), F32),
                   jax.ShapeDtypeStruct((seq, 1024), BF), jax.ShapeDtypeStruct((nb, RET_HEADS, dk, dv), BF)],
        scratch_shapes=[pltpu.VMEM((RET_HEADS, dk, dv), F32)])


def _mix_fwd(oat, yrp, gates, bg, x, w_att, w_ret, w_out):
    seq = x.shape[0]
    tm = min(TOKEN_TILE, seq)

    def body(oat_ref, yrp_ref, gates_ref, bg_ref, x_ref, wa_ref, wr_ref, wo_ref, ya_ref, yr_ref, mg_ref, h1_ref):
        ya = _tn(oat_ref[...], wa_ref[...])
        yr = _nn(yrp_ref[...], wr_ref[...])
        gt = _sigmoid(gates_ref[...].astype(F32) + bg_ref[...])
        merged = (gt[:, :D_MODEL] * ya + gt[:, D_MODEL:] * yr).astype(BF)
        ya_ref[...] = ya.astype(BF)
        yr_ref[...] = yr.astype(BF)
        mg_ref[...] = merged
        h1_ref[...] = x_ref[...] + _nn(merged, wo_ref[...])

    return pl.pallas_call(
        body, name="mix_fwd", grid=(seq // tm,),
        in_specs=[_cols(512, tm), _rows(tm, 1024), _rows(tm, 2048), VMEM_FULL, _rows(tm, D_MODEL),
                  VMEM_FULL, VMEM_FULL, VMEM_FULL],
        out_specs=[_rows(tm, D_MODEL)] * 4,
        out_shape=[jax.ShapeDtypeStruct((seq, D_MODEL), BF)] * 3 + [jax.ShapeDtypeStruct((seq, D_MODEL), F32)],
        compiler_params=_params(1),
    )(oat, yrp, gates, bg, x, w_att, w_ret, w_out)


def _mlp_loss_step(h1, g2, g3, target, w_ff1, w_ff2):
    seq = h1.shape[0]
    tm = min(MLP_TOKEN_TILE, seq)
    n_chunks = len(FF_CHUNKS) - 1

    def body(h1_ref, g2_ref, g3_ref, t_ref, w1_hbm, w2_hbm,
             xn2_ref, hdn_ref, dh2_ref, du_ref, dh1_ref, loss_ref, dg3_ref, dg2_ref,
             w1_vmem, w1_sems, w2_vmem, w2_sems, relu_u):
        w1 = _Resident(w1_hbm, w1_vmem, w1_sems, FF_CHUNKS)
        w2 = _Resident(w2_hbm, w2_vmem, w2_sems, FF_CHUNKS)
        _Resident.load(w1, w2)

        @pl.when(pl.program_id(0) == 0)
        def _():
            loss_ref[...] = jnp.zeros_like(loss_ref)
            dg3_ref[...] = jnp.zeros_like(dg3_ref)
            dg2_ref[...] = jnp.zeros_like(dg2_ref)

        h1v = h1_ref[...]
        r2 = lax.rsqrt(jnp.mean(h1v * h1v, axis=-1, keepdims=True) + NORM_EPS)
        xn2 = (h1v * r2 * g2_ref[...]).astype(BF)
        xn2_ref[...] = xn2
        h2 = h1v
        for c in range(n_chunks):
            cols = slice(FF_CHUNKS[c], FF_CHUNKS[c + 1])
            a = jnp.maximum(_nt(xn2, w1.chunk(c)[...]), 0.0)
            relu_u[:, cols] = a.astype(BF)
            hdn = jnp.square(a).astype(BF)
            hdn_ref[:, cols] = hdn
            h2 = h2 + _nn(hdn, w2.chunk(c)[...])
        r3 = lax.rsqrt(jnp.mean(h2 * h2, axis=-1, keepdims=True) + NORM_EPS)
        hn = h2 * r3
        err = hn * g3_ref[...] - t_ref[...]
        loss_ref[...] += jnp.sum(err * err) * (0.5 / D_MODEL)
        dy = err * (1.0 / D_MODEL)
        dg3_ref[...] += jnp.sum(dy * hn, axis=0, keepdims=True)
        z = dy * g3_ref[...]
        dh2 = r3 * (z - hn * jnp.mean(z * hn, axis=-1, keepdims=True))
        dh2b = dh2.astype(BF)
        dh2_ref[...] = dh2b
        dxn2 = jnp.zeros_like(dh2)
        for c in range(n_chunks):
            cols = slice(FF_CHUNKS[c], FF_CHUNKS[c + 1])
            du = (_nt(dh2b, w2.chunk(c)[...]) * (2.0 * relu_u[:, cols].astype(F32))).astype(BF)
            du_ref[:, cols] = du
            dxn2 = dxn2 + _nn(du, w1.chunk(c)[...])
        dnorm, dg = _rms_bwd(dxn2, h1v, g2_ref[...])
        dh1_ref[...] = dh2 + dnorm
        dg2_ref[...] += dg

    gain_acc = pl.BlockSpec((1, D_MODEL), lambda i: (0, 0))
    return pl.pallas_call(
        body, name="mlp_loss_step", grid=(seq // tm,),
        in_specs=[_rows(tm, D_MODEL), VMEM_FULL, VMEM_FULL, _rows(tm, D_MODEL), ANY, ANY],
        out_specs=[_rows(tm, D_MODEL), _rows(tm, D_FF), _rows(tm, D_MODEL), _rows(tm, D_FF), _rows(tm, D_MODEL),
                   pl.BlockSpec((1, LANE), lambda i: (0, 0)), gain_acc, gain_acc],
        out_shape=[jax.ShapeDtypeStruct((seq, D_MODEL), BF), jax.ShapeDtypeStruct((seq, D_FF), BF),
                   jax.ShapeDtypeStruct((seq, D_MODEL), BF), jax.ShapeDtypeStruct((seq, D_FF), BF),
                   jax.ShapeDtypeStruct((seq, D_MODEL), F32), jax.ShapeDtypeStruct((1, LANE), F32),
                   jax.ShapeDtypeStruct((1, D_MODEL), F32), jax.ShapeDtypeStruct((1, D_MODEL), F32)],
        scratch_shapes=(_Resident.scratch(w_ff1, FF_CHUNKS) + _Resident.scratch(w_ff2, FF_CHUNKS)
                        + [pltpu.VMEM((tm, D_FF), BF)]),
        compiler_params=_params(1),
    )(h1, g2, g3, target, w_ff1, w_ff2)


def _rms_bwd(dxn, xin, gain):
    r = lax.rsqrt(jnp.mean(xin * xin, axis=-1, keepdims=True) + NORM_EPS)
    xhat = xin * r
    z = dxn * gain
    dxin = r * (z - xhat * jnp.mean(z * xhat, axis=-1, keepdims=True))
    return dxin, jnp.sum(dxn * xhat, axis=0, keepdims=True)


def _mm_tn(a, b, name, a_is_t=False, comm=None):
    kdim, seq = a.shape if a_is_t else a.shape[::-1]
    ndim = b.shape[1]
    ts = min(TN_TOKEN_TILE, seq)
    tk = min(kdim, 1024)
    tn = min(ndim, 1024)
    n_steps = seq // ts

    def body(a_ref, b_ref, o_ref):
        @pl.when(pl.program_id(2) == 0)
        def _():
            o_ref[...] = jnp.zeros_like(o_ref)

        mm = _nn if a_is_t else _tn
        o_ref[...] += mm(a_ref[...].astype(BF), b_ref[...].astype(BF))

    a_spec = (pl.BlockSpec((tk, ts), lambda k, n, s: (k, s)) if a_is_t
              else pl.BlockSpec((ts, tk), lambda k, n, s: (s, k)))
    res = _call(
        body, (a, b), comm, name=name, grid=(kdim // tk, ndim // tn, n_steps),
        in_specs=[a_spec, pl.BlockSpec((ts, tn), lambda k, n, s: (s, n))],
        out_specs=pl.BlockSpec((tk, tn), lambda k, n, s: (k, n)),
        out_shape=jax.ShapeDtypeStruct((kdim, ndim), F32))
    return res[0] if comm is None else res


def _dw_in_t(pieces, is_t, xn1, comm=None):
    seq = xn1.shape[0]
    ts = seq
    tr = 256
    n_blk = [(p.shape[0] if t else p.shape[1]) // tr for p, t in zip(pieces, is_t)]
    offs = [sum(n_blk[:i]) for i in range(len(pieces) + 1)]
    n_p = len(pieces)

    def body(*refs):
        piece_refs, xn_ref, o_ref = refs[:n_p], refs[n_p], refs[n_p + 1]
        r, s = pl.program_id(0), pl.program_id(1)

        @pl.when(s == 0)
        def _():
            o_ref[...] = jnp.zeros_like(o_ref)

        for p in range(n_p):
            @pl.when((r >= offs[p]) & (r < offs[p + 1]))
            def _(p=p):
                mm = _nn if is_t[p] else _tn
                o_ref[...] += mm(piece_refs[p][...], xn_ref[pl.ds(pl.multiple_of(s * ts, ts), ts), :])

    def piece_spec(p):
        def index(r, s):
            inside = (r >= offs[p]) & (r < offs[p + 1])
            tok, blk = jnp.where(inside, s, 0), jnp.clip(r - offs[p], 0, n_blk[p] - 1)
            return (blk, tok) if is_t[p] else (tok, blk)
        return pl.BlockSpec((tr, ts) if is_t[p] else (ts, tr), index)

    return _call(
        body, (*pieces, xn1), comm, name="dw_in", grid=(offs[-1], seq // ts),
        in_specs=[piece_spec(p) for p in range(n_p)] + [VMEM_FULL],
        out_specs=pl.BlockSpec((tr, D_MODEL), lambda r, s: (r, 0)),
        out_shape=jax.ShapeDtypeStruct((offs[-1] * tr, D_MODEL), F32))


def _mix_bwd(dh1, ya, yr, gates, bg, merged, oat, yrp, w_att, w_ret, w_out, comm=None):
    seq = dh1.shape[0]
    tm = min(TOKEN_TILE, seq)

    def body(dh1_ref, ya_ref, yr_ref, gates_ref, bg_ref, mg_ref, oat_ref, yrp_ref, wa_ref, wr_ref, wo_ref,
             dgates_ref, doa_ref, dyrp_ref, db_ref, dwa_ref, dwr_ref, dwo_ref):
        @pl.when(pl.program_id(0) == 0)
        def _():
            for ref in (db_ref, dwa_ref, dwr_ref, dwo_ref):
                ref[...] = jnp.zeros_like(ref)

        dh1b = dh1_ref[...].astype(BF)
        dm = _nt(dh1b, wo_ref[...])
        gt = _sigmoid(gates_ref[...].astype(F32) + bg_ref[...])
        ga, gr = gt[:, :D_MODEL], gt[:, D_MODEL:]
        dya = (dm * ga).astype(BF)
        dyr = (dm * gr).astype(BF)
        dga = dm * ya_ref[...].astype(F32) * ga * (1.0 - ga)
        dgr = dm * yr_ref[...].astype(F32) * gr * (1.0 - gr)
        dgates_ref[:, :D_MODEL] = dga.astype(BF)
        dgates_ref[:, D_MODEL:] = dgr.astype(BF)
        db_ref[:, :D_MODEL] += jnp.sum(dga, axis=0, keepdims=True)
        db_ref[:, D_MODEL:] += jnp.sum(dgr, axis=0, keepdims=True)
        doa_ref[...] = _nt(wa_ref[...], dya).astype(BF)
        dyrp_ref[...] = _nt(dyr, wr_ref[...]).astype(BF)
        dwo_ref[...] += _tn(mg_ref[...], dh1b)
        dwr_ref[...] += _tn(yrp_ref[...], dyr)
        dwa_ref[...] += _nn(oat_ref[...], dya)

    acc = lambda r, c: pl.BlockSpec((r, c), lambda i: (0, 0))
    return _call(
        body, (dh1, ya, yr, gates, bg, merged, oat, yrp, w_att, w_ret, w_out), comm, name="mix_bwd",
        grid=(seq // tm,),
        in_specs=[_rows(tm, D_MODEL), _rows(tm, D_MODEL), _rows(tm, D_MODEL), _rows(tm, 2048), VMEM_FULL,
                  _rows(tm, D_MODEL), _cols(512, tm), _rows(tm, 1024), VMEM_FULL, VMEM_FULL, VMEM_FULL],
        out_specs=[_rows(tm, 2048), _cols(512, tm), _rows(tm, 1024), acc(1, 2048), acc(512, D_MODEL),
                   acc(1024, D_MODEL), acc(D_MODEL, D_MODEL)],
        out_shape=[jax.ShapeDtypeStruct((seq, 2048), BF), jax.ShapeDtypeStruct((512, seq), BF),
                   jax.ShapeDtypeStruct((seq, 1024), BF), jax.ShapeDtypeStruct((1, 2048), F32),
                   jax.ShapeDtypeStruct((512, D_MODEL), F32), jax.ShapeDtypeStruct((1024, D_MODEL), F32),
                   jax.ShapeDtypeStruct((D_MODEL, D_MODEL), F32)])


def _seq_bwd(qt, kv, kvt, dot, sinks, qkr, vr, gr, ret, dyrp, states, gain, rtabs, tabs, comm=None):
    seq = kv.shape[0]
    c = ATT_BLOCK
    assert c == RET_CHUNK
    d = ATT_HEAD_DIM
    nb = seq // c
    dk, dv = RET_KEY_DIM, RET_VAL_DIM
    ca, sa, sb, cr, sr, ct, st = tabs

    def body(sink_ref, qt_ref, kvc_ref, kvp_ref, kvtc_ref, kvtp_ref, dot_ref, ca_ref, sa_ref, sb_ref, ct_ref, st_ref,
             qk_ref, v_ref, g_ref, ret_ref, dyp_ref, st8_ref, gain_ref, intra_ref, qd_ref, kd_ref, cr_ref, sr_ref,
             dqt_ref, dkv_ref, dsink_ref, dall_ref, dgain_ref, carry, dstate):
        step = pl.program_id(0)
        i = nb - 1 - step

        @pl.when(step == 0)
        def _():
            carry[...] = jnp.zeros_like(carry)
            dsink_ref[...] = jnp.zeros_like(dsink_ref)
            dstate[...] = jnp.zeros_like(dstate)
            dgain_ref[...] = jnp.zeros_like(dgain_ref)

        kwin = jnp.concatenate([kvp_ref[:, :LANE], kvc_ref[:, :LANE]], axis=0)
        vwin = jnp.concatenate([kvp_ref[:, LANE:], kvc_ref[:, LANE:]], axis=0)
        ktwin = jnp.concatenate([kvtp_ref[:LANE, :], kvtc_ref[:LANE, :]], axis=1)
        ctv, stv = ct_ref[...], st_ref[...]
        lane1 = lax.broadcasted_iota(jnp.int32, (1, LANE), 1)
        dkw = jnp.zeros((2 * c, LANE), F32)
        dvw = jnp.zeros((2 * c, LANE), F32)
        dsink = jnp.zeros((1, LANE), F32)
        for g in (0, 1):
            qpad, dopad = _att_group(qt_ref, g), _att_group(dot_ref, g)
            prob, psink = _att_probs(kwin, qpad, sink_ref, g, i)
            dprob = _nn(vwin, dopad)
            drow = jnp.sum(dprob * prob, axis=0, keepdims=True)
            ds = (prob * (dprob - drow) * ATT_SCALE).astype(BF)
            dqg = _nn(ktwin[d * g:d * (g + 1), :], ds)
            dkw = dkw + _nt(ds, qpad)
            dvw = dvw + _nt(prob.astype(BF), dopad)
            dsink_lanes = psink * drow
            for j in range(ATT_Q_PER_KV):
                h = ATT_Q_PER_KV * g + j
                dqt_ref[d * h:d * (h + 1), :] = _rope_att_rows(dqg[:, c * j:c * (j + 1)], ctv, stv, -1.0).astype(BF)
                dsink = dsink + jnp.where(lane1 == h, -jnp.sum(dsink_lanes[:, c * j:c * (j + 1)]), 0.0)
        dsink_ref[...] += dsink
        total = carry[...] + jnp.concatenate([dkw[c:], dvw[c:]], axis=1)
        dkv_ref[:, :LANE] = _rope_att_t(total[:, :LANE], ca_ref[...], sa_ref[...], sb_ref[...]).astype(BF)
        dkv_ref[:, LANE:] = total[:, LANE:].astype(BF)
        carry[...] = jnp.concatenate([dkw[:c], dvw[:c]], axis=1)

        crv, srv = cr_ref[...], sr_ref[...]
        for h in range(RET_HEADS):
            vs = slice(dv * h, dv * (h + 1))
            qh = qk_ref[:, dk * h:dk * (h + 1)]
            kh = qk_ref[:, 512 + dk * h:512 + dk * (h + 1)]
            vh = v_ref[:, vs]
            out = ret_ref[:, vs]
            g = g_ref[:, vs].astype(F32)
            dyp = dyp_ref[:, vs].astype(F32)
            gain_h = gain_ref[:, vs]
            mu = jnp.mean(out, axis=-1, keepdims=True)
            dev = out - mu
            rstd = lax.rsqrt(jnp.mean(dev * dev, axis=-1, keepdims=True) + GN_EPS)
            yn = dev * rstd
            sg = _sigmoid(g)
            dg = dyp * (yn * gain_h) * (sg * (1.0 + g * (1.0 - sg)))
            dy = dyp * (g * sg)
            dgain_ref[:, vs] += jnp.sum(dy * yn, axis=0, keepdims=True)
            dyn = dy * gain_h
            dout = rstd * (dyn - jnp.mean(dyn, axis=-1, keepdims=True)
                           - yn * jnp.mean(dyn * yn, axis=-1, keepdims=True))
            doutb = dout.astype(BF)
            sc = st8_ref[0, h]
            dsp = dstate[h]
            dspb = dsp.astype(BF)
            intra, qdv, kdv = intra_ref[h], qd_ref[h], kd_ref[h]
            att = _nt(qh, kh) * intra
            dab = (_nt(doutb, vh) * intra).astype(BF)
            qdec = (qh.astype(F32) * qdv).astype(BF)
            kdec = (kh.astype(F32) * kdv).astype(BF)
            dq = _nn(dab, kh) + _nt(doutb, sc) * qdv
            dkk = _tn(dab, qh) + _nt(vh, dspb) * kdv
            dvv = _tn(att.astype(BF), doutb) + _nn(kdec, dspb)
            dstate[h] = dsp * RET_CHUNK_DECAY[h] + _tn(qdec, doutb)
            dall_ref[:, dk * h:dk * (h + 1)] = _rope_ret_t(dq, crv, srv).astype(BF)
            dall_ref[:, 512 + dk * h:512 + dk * (h + 1)] = (_rope_ret_t(dkk, crv, srv) * RET_SCALE).astype(BF)
            dall_ref[:, 1024 + dv * h:1024 + dv * (h + 1)] = dvv.astype(BF)
            dall_ref[:, 2048 + dv * h:2048 + dv * (h + 1)] = dg.astype(BF)

    cur = lambda s: nb - 1 - s
    prev = lambda s: jnp.maximum(nb - 2 - s, 0)
    rows = lambda w: pl.BlockSpec((c, w), lambda s: (cur(s), 0))
    cols = lambda h: pl.BlockSpec((h, c), lambda s: (0, cur(s)))
    acc = lambda w: pl.BlockSpec((1, w), lambda s: (0, 0))
    return _call(
        body, (sinks, qt, kv, kv, kvt, kvt, dot, ca, sa, sb, ct, st, qkr, vr, gr, ret, dyrp, states, gain, *rtabs,
               cr, sr), comm, name="seq_bwd", grid=(nb,),
        in_specs=[SMEM_FULL, cols(512), rows(256), pl.BlockSpec((c, 256), lambda s: (prev(s), 0)), cols(256),
                  pl.BlockSpec((256, c), lambda s: (0, prev(s))), cols(512), rows(LANE), rows(LANE), rows(LANE),
                  cols(8), cols(8), rows(1024), rows(1024), rows(1024), rows(1024), rows(1024),
                  pl.BlockSpec((1, RET_HEADS, dk, dv), lambda s: (cur(s), 0, 0, 0)),
                  VMEM_FULL, VMEM_FULL, VMEM_FULL, VMEM_FULL, rows(LANE), rows(LANE)],
        out_specs=[cols(512), rows(256), acc(LANE), rows(3072), acc(1024)],
        out_shape=[jax.ShapeDtypeStruct((512, seq), BF), jax.ShapeDtypeStruct((seq, 256), BF),
                   jax.ShapeDtypeStruct((1, LANE), F32), jax.ShapeDtypeStruct((seq, 3072), BF),
                   jax.ShapeDtypeStruct((1, 1024), F32)],
        scratch_shapes=[pltpu.VMEM((c, 256), F32), pltpu.VMEM((RET_HEADS, dk, dv), F32)])


def _inproj_bwd(dqa, dkva, dret, dgates, dh1, x, g1, w_in, comm=None):
    seq = x.shape[0]
    tm = min(TOKEN_TILE, seq)

    def body(dqa_ref, dkva_ref, dret_ref, dgates_ref, dh1_ref, x_ref, g_ref, w_hbm, dx_ref, dg1_ref, w_vmem, w_sems):
        w = _Resident(w_hbm, w_vmem, w_sems, IN_CHUNKS_BWD)
        _Resident.load(w)

        @pl.when(pl.program_id(0) == 0)
        def _():
            dg1_ref[...] = jnp.zeros_like(dg1_ref)

        dxn = _tn(dqa_ref[...], w.chunk(0)[...])
        dxn = dxn + _nn(dkva_ref[...], w.chunk(1)[...])
        dxn = dxn + _nn(dret_ref[...], w.chunk(2)[...])
        dxn = dxn + _nn(dgates_ref[...], w.chunk(3)[...])
        dnorm, dg = _rms_bwd(dxn, x_ref[...], g_ref[...])
        dx_ref[...] = dh1_ref[...] + dnorm
        dg1_ref[...] += dg

    return _call(
        body, (dqa, dkva, dret, dgates, dh1, x, g1, w_in), comm, name="inproj_bwd", grid=(seq // tm,),
        in_specs=[_cols(512, tm), _rows(tm, 256), _rows(tm, 3072), _rows(tm, 2048), _rows(tm, D_MODEL),
                  _rows(tm, D_MODEL), VMEM_FULL, ANY],
        out_specs=[_rows(tm, D_MODEL), pl.BlockSpec((1, D_MODEL), lambda i: (0, 0))],
        out_shape=[jax.ShapeDtypeStruct((seq, D_MODEL), F32), jax.ShapeDtypeStruct((1, D_MODEL), F32)],
        scratch_shapes=_Resident.scratch(w_in, IN_CHUNKS_BWD))


def _local_step(x, target, g1, bg, sinks, gain, g2, g3, shards):
    rtabs = _ret_tables()
    scatter = _Scatter

    comm, unpack = _gather(shards, ("w_in",), relay=True)
    (xn1, ca, sa, sb, cr, sr, ct, st), got = _prologue(x, g1, comm)
    (w_in,) = unpack(got)
    comm_mid, unpack_mid = _gather(shards, ("w_att_up", "w_ret_up", "w_out"), relay=True)
    comm_ff1, unpack_ff1 = _gather(shards, ("w_ff1",), relay=True)
    (qt, kv, kvt, qkr, vr, gr, gates), got = _inproj_fwd(xn1, w_in, (ca, sa, sb, cr, sr, ct, st),
                                                         _join(comm_mid, comm_ff1))
    w_att, w_ret, w_out = unpack_mid(got[:3])
    (w_ff1,) = unpack_ff1(got[3:])
    comm, unpack = _gather(shards, ("w_ff2",), relay=True)
    (oa, ret, yrp, states), got = _seq_fwd(qt, kv, kvt, sinks, qkr, vr, gr, gain, rtabs, comm)
    (w_ff2,) = unpack(got)
    ya, yr, merged, h1 = _mix_fwd(oa, yrp, gates, bg, x, w_att, w_ret, w_out)
    xn2, hdn, dh2, du, dh1, loss, dg3, dg2 = _mlp_loss_step(h1, g2, g3, target, w_ff1, w_ff2)

    ff2 = scatter(dict(w_ff2=_mm_tn(hdn, dh2, "dw_ff2")), "ff2")
    d_ff1, r1 = _mm_tn(du, xn2, "dw_ff1", comm=ff2.pair_comm())
    ff2.pair_done(r1)
    ff1 = scatter(dict(w_ff1=d_ff1), "ff1")
    (dgates, doa, dyrp, db, d_att, d_ret, d_out), r1 = _mix_bwd(dh1, ya, yr, gates, bg, merged, oa, yrp,
                                                                w_att, w_ret, w_out, ff1.pair_comm())
    ff1.pair_done(r1)
    mid = scatter(dict(w_att_up=d_att, w_ret_up=d_ret, w_out=d_out), "mid")
    (dqa, dkva, dsink, dret, dgain), got = _seq_bwd(
        qt, kv, kvt, doa, sinks, qkr, vr, gr, ret, dyrp, states, gain, rtabs, (ca, sa, sb, cr, sr, ct, st),
        _join(_join(ff2.chip_comm(), ff1.chip_comm()), mid.pair_comm()))
    ff2.chip_done(got[:1])
    ff1.chip_done(got[1:2])
    mid.pair_done(got[2:])
    d_in, r2 = _dw_in_t((dqa, dkva, dret, dgates), (True, False, False, False), xn1, mid.chip_comm())
    mid.chip_done(r2)
    win = scatter(dict(w_in=d_in), "in")
    win.pair_done(_comm_alone(win.pair_comm(), "pair_exchange_in"))
    (dx, dg1), r2 = _inproj_bwd(dqa, dkva, dret, dgates, dh1, x, g1, w_in, win.chip_comm())
    win.chip_done(r2)
    small = dict(norm_mix_gain=dg1, b_gates=db, attn_sinks=dsink, ret_gn_gain=dgain, norm_mlp_gain=dg2,
                 norm_final_gain=dg3)
    return loss, dx, small, (win, mid, ff1, ff2)


def _coords():
    return lax.axis_index("x"), lax.axis_index("y"), lax.axis_index("c")


def _flip(v, bit):
    return 1 - v if bit else v


def _xor(a, b):
    return a + b - 2 * a * b


def _gather_relay_comm(shards):
    n = len(shards)

    def parts(ins, outs, sems):
        send_sems, recv_sems, local_sems = sems
        x, y, c = _coords()
        me, sib = (x, y, c), (x, y, 1 - c)

        def ring(core):
            return ((_xor(x, core), _xor(y, 1 - core), core), (_xor(x, 1 - core), _xor(y, core), core),
                    (1 - x, 1 - y, core))

        def slot(a, blk):
            return outs[a].at[4 * blk[0] + 2 * blk[1] + blk[2]]

        def copy(a, k, blk, to, src=None):
            return pltpu.make_async_remote_copy(
                src_ref=slot(a, blk) if src is None else src, dst_ref=slot(a, blk),
                send_sem=send_sems.at[a, k], recv_sem=recv_sems.at[a, k], device_id=to, device_id_type=MESH)

        mine = [pltpu.make_async_copy(ins[a], slot(a, me), local_sems.at[a]) for a in range(n)]
        return me, sib, ring, copy, mine

    def first_copies(ins, outs, sems):
        me, sib, ring, copy, mine = parts(ins, outs, sems)
        src, dst, _ = ring(me[2])
        return mine, [copy(a, k, me, to, src=ins[a]) for a in range(n) for k, to in ((1, src), (2, dst), (0, sib))]

    def start(ins, outs, sems):
        mine, first = first_copies(ins, outs, sems)
        for cp in mine + first:
            cp.start()

    def relay(ins, outs, sems):
        me, sib, ring, copy, _ = parts(ins, outs, sems)
        src, dst, _ = ring(me[2])
        for a in range(n):
            copy(a, 1, src, me).wait_recv()
            copy(a, 3, src, dst).start()
            copy(a, 4, src, sib).start()

    def finish(ins, outs, sems):
        me, sib, ring, copy, _ = parts(ins, outs, sems)
        mine, first = first_copies(ins, outs, sems)
        src, dst, diag = ring(me[2])
        passed = [copy(a, k, src, to) for a in range(n) for k, to in ((3, dst), (4, sib))]
        for k, blk in ((2, dst), (3, diag)):
            for a in range(n):
                copy(a, k, blk, me).wait_recv()
                fwd = copy(a, k + 3, blk, sib)
                fwd.start()
                passed.append(fwd)
        s_src, s_dst, s_diag = ring(sib[2])
        for a in range(n):
            for k, blk in ((0, sib), (4, s_src), (5, s_dst), (6, s_diag)):
                copy(a, k, blk, me).wait_recv()
        for cp in first + passed:
            cp.wait_send()
        for cp in mine:
            cp.wait()

    return _Comm(list(shards), [jax.ShapeDtypeStruct((N_DEV,) + s.shape, s.dtype) for s in shards],
                 [pltpu.SemaphoreType.DMA((n, 7)), pltpu.SemaphoreType.DMA((n, 7)), pltpu.SemaphoreType.DMA((n,))],
                 start, finish, relay)


def _gather_comm(shards):
    n = len(shards)

    def parts(ins, outs, sems):
        send_sems, recv_sems, local_sems = sems
        x, y, c = _coords()
        me, sib = (x, y, c), (x, y, 1 - c)
        chips = [(1 - x, y), (x, 1 - y), (1 - x, 1 - y)]

        def slot(a, blk):
            return outs[a].at[4 * blk[0] + 2 * blk[1] + blk[2]]

        def copy(a, k, blk, to, src=None):
            return pltpu.make_async_remote_copy(
                src_ref=slot(a, blk) if src is None else src, dst_ref=slot(a, blk),
                send_sem=send_sems.at[a, k], recv_sem=recv_sems.at[a, k], device_id=to, device_id_type=MESH)

        mine = [pltpu.make_async_copy(ins[a], slot(a, me), local_sems.at[a]) for a in range(n)]
        first = []
        for a in range(n):
            first.append(copy(a, 0, me, sib, src=ins[a]))
            first += [copy(a, 1 + j, me, (*chip, c), src=ins[a]) for j, chip in enumerate(chips)]
        return me, sib, chips, c, copy, mine, first

    def start(ins, outs, sems):
        *_, mine, first = parts(ins, outs, sems)
        for cp in mine + first:
            cp.start()

    def finish(ins, outs, sems):
        me, sib, chips, c, copy, mine, first = parts(ins, outs, sems)
        passed = []
        for j, chip in enumerate(chips):
            for a in range(n):
                copy(a, 1 + j, (*chip, c), me).wait_recv()
                fwd = copy(a, 4 + j, (*chip, c), sib)
                fwd.start()
                passed.append(fwd)
        for a in range(n):
            copy(a, 0, sib, me).wait_recv()
            for j, chip in enumerate(chips):
                copy(a, 4 + j, (*chip, 1 - c), me).wait_recv()
        for cp in first + passed:
            cp.wait_send()
        for cp in mine:
            cp.wait()

    return _Comm(list(shards), [jax.ShapeDtypeStruct((N_DEV,) + s.shape, s.dtype) for s in shards],
                 [pltpu.SemaphoreType.DMA((n, 7)), pltpu.SemaphoreType.DMA((n, 7)), pltpu.SemaphoreType.DMA((n,))],
                 start, finish)


COLUMN_SHARDED = ("w_in", "w_ff1")
COLUMN_RELAID = ("w_att_up",)


def _gather(shards, names, relay=False):
    def unpack(got):
        return [jnp.transpose(g, (1, 0, 2)).reshape(g.shape[1], N_DEV * g.shape[2]) if k in COLUMN_RELAID
                else g.reshape(N_DEV * g.shape[1], g.shape[2]) for k, g in zip(names, got)]

    return (_gather_relay_comm if relay else _gather_comm)([shards[k] for k in names]), unpack


def _pair_comm(grads):
    n = len(grads)

    def copies(g, r1, sems):
        send_sems, recv_sems = sems
        x, y, c = _coords()
        return [pltpu.make_async_remote_copy(
            src_ref=g[a].at[2 * j + (1 - c)], dst_ref=r1[a].at[j], send_sem=send_sems.at[a, j],
            recv_sem=recv_sems.at[a, j], device_id=(x, y, 1 - c), device_id_type=MESH)
            for a in range(n) for j in range(4)]

    def start(g, r1, sems):
        for cp in copies(g, r1, sems):
            cp.start()

    def finish(g, r1, sems):
        for cp in copies(g, r1, sems):
            cp.wait_recv()
        for cp in copies(g, r1, sems):
            cp.wait_send()

    return _Comm(list(grads), [jax.ShapeDtypeStruct((4,) + g.shape[1:], g.dtype) for g in grads],
                 [pltpu.SemaphoreType.DMA((n, 4)), pltpu.SemaphoreType.DMA((n, 4))], start, finish)


def _small_comm(small):
    def parts(ins, outs, sems):
        (small_ref,), (small_all,) = ins, outs
        ssend, srecv, lsem = sems
        x, y, c = _coords()
        me_idx = 4 * x + 2 * y + c
        own = pltpu.make_async_copy(small_ref, small_all.at[me_idx], lsem)
        sends, recvs = [], []
        for r in range(1, N_DEV):
            px, py, pc = _flip(x, r & 4), _flip(y, r & 2), _flip(c, r & 1)
            sends.append(pltpu.make_async_remote_copy(
                src_ref=small_ref, dst_ref=small_all.at[me_idx], send_sem=ssend.at[r - 1], recv_sem=srecv.at[r - 1],
                device_id=(px, py, pc), device_id_type=MESH))
            recvs.append(pltpu.make_async_remote_copy(
                src_ref=small_ref, dst_ref=small_all.at[4 * px + 2 * py + pc], send_sem=ssend.at[r - 1],
                recv_sem=srecv.at[r - 1], device_id=(px, py, pc), device_id_type=MESH))
        return own, sends, recvs

    def start(ins, outs, sems):
        own, sends, _ = parts(ins, outs, sems)
        own.start()
        for cp in sends:
            cp.start()

    def finish(ins, outs, sems):
        own, sends, recvs = parts(ins, outs, sems)
        for cp in recvs:
            cp.wait_recv()
        for cp in sends:
            cp.wait_send()
        own.wait()

    return _Comm([small], [jax.ShapeDtypeStruct((N_DEV,) + small.shape, small.dtype)],
                 [pltpu.SemaphoreType.DMA((N_DEV - 1,)), pltpu.SemaphoreType.DMA((N_DEV - 1,)),
                  pltpu.SemaphoreType.DMA], start, finish)


def _pair_sum(grads, r1s, c_arr, tag):
    n = len(grads)
    q = 1

    def body(c_ref, *refs):
        g, r, t = refs[:n], refs[n:2 * n], refs[2 * n:]
        for a in range(n):
            t[a][...] = (g[a][...] + r[a][...]).astype(t[a].dtype)

    def blk(arr):
        return (1, arr.shape[1] // q, arr.shape[2])

    grid_spec = pltpu.PrefetchScalarGridSpec(
        num_scalar_prefetch=1, grid=(4, q),
        in_specs=[pl.BlockSpec(blk(g), lambda j, s, c_ref: (2 * j + c_ref[0], s, 0)) for g in grads]
        + [pl.BlockSpec(blk(r), lambda j, s, c_ref: (j, s, 0)) for r in r1s],
        out_specs=[pl.BlockSpec(blk(r), lambda j, s, c_ref: (j, s, 0)) for r in r1s])
    return pl.pallas_call(
        body, name="pair_sum_" + tag, grid_spec=grid_spec,
        out_shape=[jax.ShapeDtypeStruct(r.shape, RS_PAYLOAD) for r in r1s],
        compiler_params=_params(2),
    )(c_arr, *grads, *r1s)


def _chip_comm(ts):
    n = len(ts)

    def copies(t, r2, sems):
        send_sems, recv_sems = sems
        x, y, c = _coords()
        out = []
        for a in range(n):
            for r in range(1, 4):
                tx, ty = _flip(x, r & 2), _flip(y, r & 1)
                out.append(pltpu.make_async_remote_copy(
                    src_ref=t[a].at[2 * tx + ty], dst_ref=r2[a].at[r - 1], send_sem=send_sems.at[a, r - 1],
                    recv_sem=recv_sems.at[a, r - 1], device_id=(tx, ty, c), device_id_type=MESH))
        return out

    def start(t, r2, sems):
        for cp in copies(t, r2, sems):
            cp.start()

    def finish(t, r2, sems):
        for cp in copies(t, r2, sems):
            cp.wait_recv()
        for cp in copies(t, r2, sems):
            cp.wait_send()

    return _Comm(list(ts), [jax.ShapeDtypeStruct((3,) + t.shape[1:], t.dtype) for t in ts],
                 [pltpu.SemaphoreType.DMA((n, 3)), pltpu.SemaphoreType.DMA((n, 3))], start, finish)


class _Scatter:
    def __init__(self, grads, tag):
        self.names, self.tag = tuple(grads), tag
        self.stacks = [jnp.transpose(g.reshape(g.shape[0], N_DEV, g.shape[1] // N_DEV), (1, 0, 2))
                       if k in COLUMN_RELAID else g.reshape(N_DEV, g.shape[0] // N_DEV, g.shape[1])
                       for k, g in grads.items()]

    def pair_comm(self):
        return _pair_comm(self.stacks)

    def pair_done(self, r1s):
        self.r1s = list(r1s)
        c_arr = jnp.reshape(lax.axis_index("c"), (1,)).astype(jnp.int32)
        self.ts = _pair_sum(self.stacks, self.r1s, c_arr, self.tag)

    def chip_comm(self):
        return _chip_comm(self.ts)

    def chip_done(self, r2s):
        self.r2s = list(r2s)


def _adamw(w, g, m, v):
    m = ADAM_B1 * m + (1.0 - ADAM_B1) * g
    v = ADAM_B2 * v + (1.0 - ADAM_B2) * jnp.square(g)
    m_hat = m / (1.0 - ADAM_B1 ** ADAM_STEP)
    v_hat = v / (1.0 - ADAM_B2 ** ADAM_STEP)
    delta = -ADAM_LR * (m_hat / (jnp.sqrt(v_hat) + ADAM_EPS) + ADAM_WD * w)
    return delta, m, v


ADAM_STEPS = 4


def _piece_specs(stacks):
    def rows(s):
        return s.shape[1] // ADAM_STEPS
    return ([pl.BlockSpec((1, rows(s), s.shape[2]), lambda i, idx_ref: (idx_ref[0], i, 0)) for s in stacks]
            + [pl.BlockSpec((1, rows(s), s.shape[2]), lambda i, idx_ref: (idx_ref[1], i, 0)) for s in stacks]
            + [pl.BlockSpec((3, rows(s), s.shape[2]), lambda i, idx_ref: (0, i, 0)) for s in stacks])


def _piece_sum(g0, r1, r2):
    return (((g0[0] + r1[0]) + r2[0].astype(F32)) + r2[1].astype(F32)) + r2[2].astype(F32)


def _shard_sum(stacks, r1s, r2s, idx_arr):
    n = len(stacks)

    def body(idx_ref, *refs):
        g0, r1, r2, outs = (refs[k * n:(k + 1) * n] for k in range(4))
        for a in range(n):
            outs[a][...] = _piece_sum(g0[a], r1[a], r2[a])

    out_specs = [pl.BlockSpec((s.shape[1] // ADAM_STEPS, s.shape[2]), lambda i, idx_ref: (i, 0)) for s in stacks]
    grid_spec = pltpu.PrefetchScalarGridSpec(num_scalar_prefetch=1, grid=(ADAM_STEPS,),
                                             in_specs=_piece_specs(stacks), out_specs=out_specs)
    return pl.pallas_call(
        body, name="shard_sum", grid_spec=grid_spec,
        out_shape=[jax.ShapeDtypeStruct(s.shape[1:], F32) for s in stacks],
        compiler_params=_params(1),
    )(idx_arr, *stacks, *r1s, *r2s)


def _adam_big(pieces, summed, ws, ms, vs, idx_arr):
    stacks, r1s, r2s = pieces
    n_p, n = len(stacks), len(ws)

    def body(idx_ref, *refs):
        it = iter(refs)
        g0, r1, r2, gs, w, m, v = ([next(it) for _ in range(k)] for k in (n_p, n_p, n_p, n - n_p, n, n, n))
        outs = list(it)
        for a in range(n):
            g = _piece_sum(g0[a], r1[a], r2[a]) if a < n_p else gs[a - n_p][...]
            delta, nm, nv = _adamw(w[a][...], g, m[a][...], v[a][...])
            outs[4 * a][...] = g
            outs[4 * a + 1][...] = delta
            outs[4 * a + 2][...] = nm
            outs[4 * a + 3][...] = nv

    def shard_spec(w):
        return pl.BlockSpec((w.shape[0] // ADAM_STEPS, w.shape[1]), lambda i, idx_ref: (i, 0))

    in_specs = _piece_specs(stacks) + [shard_spec(w) for w in ws[n_p:]] + [shard_spec(w) for w in ws] * 3
    out_specs = [shard_spec(w) for w in ws for _ in range(4)]
    grid_spec = pltpu.PrefetchScalarGridSpec(num_scalar_prefetch=1, grid=(ADAM_STEPS,), in_specs=in_specs,
                                             out_specs=out_specs)
    return pl.pallas_call(
        body, name="adam_big", grid_spec=grid_spec,
        out_shape=[jax.ShapeDtypeStruct(w.shape, F32) for w in ws for _ in range(4)],
        compiler_params=_params(1),
    )(idx_arr, *stacks, *r1s, *r2s, *summed, *ws, *ms, *vs)


def _adam_small(small_all, w, m, v):
    def body(all_ref, w_ref, m_ref, v_ref, g_ref, d_ref, nm_ref, nv_ref):
        g = all_ref[0]
        for k in range(1, N_DEV):
            g = g + all_ref[k]
        delta, nm, nv = _adamw(w_ref[...], g, m_ref[...], v_ref[...])
        g_ref[...] = g
        d_ref[...] = delta
        nm_ref[...] = nm
        nv_ref[...] = nv

    return pl.pallas_call(
        body, name="adam_small",
        in_specs=[VMEM_FULL] * 4, out_specs=[VMEM_FULL] * 4,
        out_shape=[jax.ShapeDtypeStruct(w.shape, F32)] * 4,
    )(small_all, w, m, v)


RS_PAYLOAD = BF


def _pack_rows(rows):
    rid = lax.broadcasted_iota(jnp.int32, (8, D_MODEL), 0)
    out = jnp.zeros((8, D_MODEL), F32)
    for i, r in enumerate(rows):
        out = jnp.where(rid == i, jnp.broadcast_to(r, (8, D_MODEL)), out)
    return out


def _small_rows(norm_mix_gain, b_gates, attn_sinks, ret_gn_gain, norm_mlp_gain, norm_final_gain):
    return [norm_mix_gain, b_gates[:, :D_MODEL], b_gates[:, D_MODEL:], ret_gn_gain, norm_mlp_gain,
            norm_final_gain.reshape(1, D_MODEL), jnp.pad(attn_sinks, ((0, 0), (0, D_MODEL - ATT_HEADS)))]


def _unpack_small(p):
    return dict(norm_mix_gain=p[0:1], b_gates=jnp.concatenate([p[1:2], p[2:3]], axis=1), ret_gn_gain=p[3:4],
                norm_mlp_gain=p[4:5], norm_final_gain=p[5], attn_sinks=p[6:7, :ATT_HEADS])


WEIGHTS = ("norm_mix_gain", "w_in", "b_gates", "attn_sinks", "ret_gn_gain", "w_att_up", "w_ret_up", "w_out",
           "norm_mlp_gain", "w_ff1", "w_ff2", "norm_final_gain")
BIG = ("w_in", "w_att_up", "w_ret_up", "w_out", "w_ff1", "w_ff2")


def kernel(x, norm_mix_gain, w_in, b_gates, attn_sinks, ret_gn_gain, w_att_up, w_ret_up, w_out, norm_mlp_gain, w_ff1, w_ff2, norm_final_gain, loss_target, m_norm_mix_gain, m_w_in, m_b_gates, m_attn_sinks, m_ret_gn_gain, m_w_att_up, m_w_ret_up, m_w_out, m_norm_mlp_gain, m_w_ff1, m_w_ff2, m_norm_final_gain, v_norm_mix_gain, v_w_in, v_b_gates, v_attn_sinks, v_ret_gn_gain, v_w_att_up, v_w_ret_up, v_w_out, v_norm_mlp_gain, v_w_ff1, v_w_ff2, v_norm_final_gain):
    w = dict(norm_mix_gain=norm_mix_gain, w_in=w_in, b_gates=b_gates, attn_sinks=attn_sinks, ret_gn_gain=ret_gn_gain,
             w_att_up=w_att_up, w_ret_up=w_ret_up, w_out=w_out, norm_mlp_gain=norm_mlp_gain, w_ff1=w_ff1,
             w_ff2=w_ff2, norm_final_gain=norm_final_gain)
    m = dict(norm_mix_gain=m_norm_mix_gain, w_in=m_w_in, b_gates=m_b_gates, attn_sinks=m_attn_sinks,
             ret_gn_gain=m_ret_gn_gain, w_att_up=m_w_att_up, w_ret_up=m_w_ret_up, w_out=m_w_out,
             norm_mlp_gain=m_norm_mlp_gain, w_ff1=m_w_ff1, w_ff2=m_w_ff2, norm_final_gain=m_norm_final_gain)
    v = dict(norm_mix_gain=v_norm_mix_gain, w_in=v_w_in, b_gates=v_b_gates, attn_sinks=v_attn_sinks,
             ret_gn_gain=v_ret_gn_gain, w_att_up=v_w_att_up, w_ret_up=v_w_ret_up, w_out=v_w_out,
             norm_mlp_gain=v_norm_mlp_gain, w_ff1=v_w_ff1, w_ff2=v_w_ff2, norm_final_gain=v_norm_final_gain)

    shards = {k: (w[k][0].T if k in COLUMN_SHARDED else w[k][0]).astype(BF) for k in BIG}
    loss_p, dx, small, groups = _local_step(
        x[0], loss_target[0], norm_mix_gain, b_gates, attn_sinks[0], ret_gn_gain, norm_mlp_gain,
        norm_final_gain.reshape(1, D_MODEL), shards)

    by_name = {}
    for grp in groups:
        for k, stack, r1, r2 in zip(grp.names, grp.stacks, grp.r1s, grp.r2s):
            by_name[k] = (stack, r1, r2)
    lane = lax.broadcasted_iota(jnp.int32, (1, D_MODEL), 1)
    sink_row = jnp.where(lane < ATT_HEADS, jnp.pad(small["attn_sinks"], ((0, 0), (0, D_MODEL - LANE))),
                         jnp.where(lane == ATT_HEADS, jnp.pad(loss_p, ((0, 0), (0, D_MODEL - LANE)), mode="edge"),
                                   0.0))
    small_pack = _pack_rows([small["norm_mix_gain"], small["b_gates"][:, :D_MODEL], small["b_gates"][:, D_MODEL:],
                             small["ret_gn_gain"], small["norm_mlp_gain"], small["norm_final_gain"], sink_row])
    xi, yi, ci = _coords()
    (small_all,) = _comm_alone(_small_comm(small_pack), "small_exchange")
    idx_arr = jnp.stack([4 * xi + 2 * yi + ci, 2 * xi + yi]).astype(jnp.int32)
    in_pieces = ["w_att_up", "w_ret_up", "w_out", "w_ff2", "w_in"]
    in_sum = ["w_ff1"]
    order = in_pieces + in_sum
    summed_t = _shard_sum(*[[by_name[k][i] for k in in_sum] for i in range(3)], idx_arr)

    def shard(tree, k):
        return tree[k][0].T if k == "w_in" else tree[k][0]

    adam_out = _adam_big([[by_name[k][i] for k in in_pieces] for i in range(3)], [g.T for g in summed_t],
                         [shard(w, k) for k in order], [shard(m, k) for k in order], [shard(v, k) for k in order],
                         idx_arr)
    big_out = [adam_out[4 * order.index(k) + i].T if k == "w_in" else adam_out[4 * order.index(k) + i]
               for k in BIG for i in range(4)]
    sm_g, sm_d, sm_m, sm_v = _adam_small(small_all, _pack_rows(_small_rows(*[w[k] for k in WEIGHTS if k not in BIG])),
                                         _pack_rows(_small_rows(*[m[k] for k in WEIGHTS if k not in BIG])),
                                         _pack_rows(_small_rows(*[v[k] for k in WEIGHTS if k not in BIG])))

    loss = sm_g[6, ATT_HEADS]
    kinds = []
    for idx, packed in enumerate((sm_g, sm_d, sm_m, sm_v)):
        out = _unpack_small(packed)
        for a, k in enumerate(BIG):
            out[k] = big_out[4 * a + idx][None]
        kinds.append(out)
    return (loss, dx[None], *[kinds[0][k] for k in WEIGHTS], *[kinds[1][k] for k in WEIGHTS],
            *[kinds[2][k] for k in WEIGHTS], *[kinds[3][k] for k in WEIGHTS])
```

```python
import functools
import math

import jax
import jax.numpy as jnp
from jax import lax
from jax.experimental import pallas as pl
from jax.experimental.pallas import tpu as pltpu

F32 = jnp.float32
BF = jnp.bfloat16
MESH = pl.DeviceIdType.MESH

D_MODEL = 1024
ATT_HEADS = 8
ATT_HEAD_DIM = 64
ATT_BLOCK = 128
ROPE_DIM = 16
ROPE_THETA = 500000.0
RET_HEADS = 4
RET_KEY_DIM = 128
RET_VAL_DIM = 256
RET_CHUNK = 128
RET_ROT_BASE = 10000.0
D_FF = 4096
NORM_EPS = 1e-6
GN_EPS = 1e-6
NEG_INF = -1e30
ATT_SCALE = ATT_HEAD_DIM ** -0.5
RET_SCALE = RET_KEY_DIM ** -0.5

C_QA, C_KA, C_VA, C_QR, C_KR, C_VR, C_GR, C_GATES, C_END = 0, 512, 640, 768, 1280, 1792, 2816, 3840, 5888

ADAM_LR = 0.001
ADAM_B1 = 0.9
ADAM_B2 = 0.999
ADAM_EPS = 1e-08
ADAM_WD = 0.01
ADAM_STEP = 10

N_DEV = 8
LANE = 128
VMEM_LIMIT = 56 * 1024 * 1024
TOKEN_TILE = 512
MLP_TOKEN_TILE = 256
TN_TOKEN_TILE = 2048
FF_CHUNKS = (0, 1024, 2048, 3072, 4096)
IN_CHUNKS_FWD = (C_QA, C_KA, C_QR, C_KR, C_VR, C_GR, C_GATES, C_END)
IN_CHUNKS_BWD = (C_QA, C_KA, C_QR, C_GATES, C_END)

RET_LOG_GAMMA = tuple(math.log1p(-(2.0 ** (-5.0 - h))) for h in range(RET_HEADS))
RET_CHUNK_DECAY = tuple(math.exp(RET_CHUNK * lg) for lg in RET_LOG_GAMMA)

VMEM_FULL = pl.BlockSpec(memory_space=pltpu.VMEM)
SMEM_FULL = pl.BlockSpec(memory_space=pltpu.SMEM)
ANY = pl.BlockSpec(memory_space=pl.ANY)


def _params(n_axes):
    return pltpu.CompilerParams(dimension_semantics=("arbitrary",) * n_axes, vmem_limit_bytes=VMEM_LIMIT)


def _nn(a, b):
    return jnp.dot(a, b, preferred_element_type=F32)


def _nt(a, b):
    return lax.dot_general(a, b, (((1,), (1,)), ((), ())), preferred_element_type=F32)


def _tn(a, b):
    return lax.dot_general(a, b, (((0,), (0,)), ((), ())), preferred_element_type=F32)


def _sigmoid(v):
    return 1.0 / (1.0 + jnp.exp(-v))


def _rows(tile, width):
    return pl.BlockSpec((tile, width), lambda i: (i, 0))


def _cols(height, tile):
    return pl.BlockSpec((height, tile), lambda i: (0, i))


class _Resident:
    def __init__(self, hbm_ref, vmem_ref, sems, bounds):
        self.hbm, self.vmem, self.sems, self.bounds = hbm_ref, vmem_ref, sems, bounds

    @staticmethod
    def scratch(w, bounds):
        return [pltpu.VMEM(w.shape, w.dtype), pltpu.SemaphoreType.DMA((len(bounds) - 1,))]

    @staticmethod
    def load(*weights):
        @pl.when(pl.program_id(0) == 0)
        def _():
            copies = [w._copy(c) for w in weights for c in range(len(w.bounds) - 1)]
            for cp in copies:
                cp.start()
            for cp in copies:
                cp.wait()

    def _rows(self, c):
        return pl.ds(self.bounds[c], self.bounds[c + 1] - self.bounds[c])

    def _copy(self, c):
        return pltpu.make_async_copy(self.hbm.at[self._rows(c)], self.vmem.at[self._rows(c)], self.sems.at[c])

    def chunk(self, c):
        return self.vmem.at[self._rows(c)]


class _Comm:
    def __init__(self, inputs, out_shapes, scratch, start, finish, relay=None):
        self.inputs, self.out_shapes, self.scratch = inputs, out_shapes, scratch
        self.start, self.finish, self.relay = start, finish, relay


RELAY_AT = 0.75


def _join(a, b):
    na_in, na_out, na_sem = len(a.inputs), len(a.out_shapes), len(a.scratch)

    def both(name):
        def run(ins, outs, sems):
            for part, args in ((a, (ins[:na_in], outs[:na_out], sems[:na_sem])),
                               (b, (ins[na_in:], outs[na_out:], sems[na_sem:]))):
                if getattr(part, name) is not None:
                    getattr(part, name)(*args)
        return run

    return _Comm(list(a.inputs) + list(b.inputs), list(a.out_shapes) + list(b.out_shapes),
                 list(a.scratch) + list(b.scratch), both("start"), both("finish"),
                 both("relay") if a.relay or b.relay else None)


def _call(body, args, comm=None, *, name, grid, in_specs, out_specs, out_shape, scratch_shapes=()):
    params = _params(len(grid))
    if comm is None:
        return pl.pallas_call(body, name=name, grid=grid, in_specs=in_specs, out_specs=out_specs, out_shape=out_shape,
                              scratch_shapes=scratch_shapes, compiler_params=params)(*args), ()
    single = not isinstance(out_specs, (list, tuple))
    out_specs_l = [out_specs] if single else list(out_specs)
    out_shape_l = [out_shape] if single else list(out_shape)
    n_in, n_out, n_scr = len(in_specs), len(out_specs_l), len(scratch_shapes)
    n_cin, n_cout = len(comm.inputs), len(comm.out_shapes)

    def hosted(*refs):
        it = iter(refs)
        ins, cin, outs, cout, scr = ([next(it) for _ in range(k)] for k in (n_in, n_cin, n_out, n_cout, n_scr))
        sems = list(it)
        ids = [pl.program_id(k) for k in range(len(grid))]
        first = functools.reduce(jnp.logical_and, [i == 0 for i in ids])
        last = functools.reduce(jnp.logical_and, [i == g - 1 for i, g in zip(ids, grid)])

        @pl.when(first)
        def _():
            comm.start(cin, cout, sems)

        if comm.relay is not None:
            at = [int(grid[0] * RELAY_AT)] + [0] * (len(grid) - 1)

            @pl.when(functools.reduce(jnp.logical_and, [i == v for i, v in zip(ids, at)]))
            def _():
                comm.relay(cin, cout, sems)

        body(*ins, *outs, *scr)

        @pl.when(last)
        def _():
            comm.finish(cin, cout, sems)

    res = pl.pallas_call(
        hosted, name=name, grid=grid, in_specs=list(in_specs) + [ANY] * n_cin,
        out_specs=out_specs_l + [ANY] * n_cout, out_shape=out_shape_l + list(comm.out_shapes),
        scratch_shapes=list(scratch_shapes) + list(comm.scratch), compiler_params=params)(*args, *comm.inputs)
    return (res[0] if single else res[:n_out]), res[n_out:]


def _comm_alone(comm, name):
    n_cin, n_cout = len(comm.inputs), len(comm.out_shapes)

    def body(*refs):
        cin, cout, sems = refs[:n_cin], refs[n_cin:n_cin + n_cout], refs[n_cin + n_cout:]
        comm.start(cin, cout, sems)
        if comm.relay is not None:
            comm.relay(cin, cout, sems)
        comm.finish(cin, cout, sems)

    return pl.pallas_call(body, name=name, in_specs=[ANY] * n_cin, out_specs=[ANY] * n_cout,
                          out_shape=list(comm.out_shapes), scratch_shapes=list(comm.scratch))(*comm.inputs)


def _slabs(v, fn):
    return jnp.concatenate([fn(v[:, LANE * j:LANE * (j + 1)]) for j in range(v.shape[1] // LANE)], axis=1)


def _rope_att(v, ca, sa, sb):
    return _slabs(v, lambda t: t * ca + pltpu.roll(t, LANE - 8, 1) * sa + pltpu.roll(t, 8, 1) * sb)


def _rope_att_t(v, ca, sa, sb):
    return _slabs(v, lambda t: t * ca + pltpu.roll(t * sa, 8, 1) + pltpu.roll(t * sb, LANE - 8, 1))


def _rope_att_rows(v, ct, st, sign):
    parts = []
    for h in range(v.shape[0] // ATT_HEAD_DIM):
        r0 = ATT_HEAD_DIM * h
        x1, x2 = v[r0:r0 + 8], v[r0 + 8:r0 + 16]
        parts += [x1 * ct - sign * (x2 * st), x2 * ct + sign * (x1 * st), v[r0 + 16:r0 + ATT_HEAD_DIM]]
    return jnp.concatenate(parts, axis=0)


def _rope_ret(v, cr, sr):
    return _slabs(v, lambda t: t * cr + pltpu.roll(t, 64, 1) * sr)


def _rope_ret_t(v, cr, sr):
    return _slabs(v, lambda t: t * cr + pltpu.roll(t * sr, 64, 1))


def _rope_lane_tables():
    def inv(dim, theta):
        return theta ** (-jnp.arange(0, dim, 2, dtype=F32) / dim)

    inv_a, inv_r = inv(ROPE_DIM, ROPE_THETA), inv(RET_KEY_DIM, RET_ROT_BASE)
    half = ROPE_DIM // 2
    zeros = jnp.zeros((ATT_HEAD_DIM - ROPE_DIM,), F32)
    freq64 = jnp.concatenate([inv_a, inv_a, zeros])
    lo64 = jnp.concatenate([-jnp.ones((half,), F32), jnp.zeros((half,), F32), zeros])
    hi64 = jnp.concatenate([jnp.zeros((half,), F32), jnp.ones((half,), F32), zeros])
    sign_r = jnp.concatenate([-jnp.ones((64,), F32), jnp.ones((64,), F32)])
    rows = [jnp.tile(freq64, 2), jnp.tile(lo64, 2), jnp.tile(hi64, 2), jnp.tile(inv_r, 2), sign_r]
    lanes = jnp.stack(rows + [jnp.zeros((LANE,), F32)] * (8 - len(rows)))
    return lanes, jnp.broadcast_to(inv_a[:, None], (half, LANE))


def _prologue(x, g1, comm):
    seq = x.shape[0]
    tm = min(TOKEN_TILE, seq)

    def body(x_ref, g_ref, lanes_ref, freq_rows_ref, xn_ref, ca_ref, sa_ref, sb_ref, cr_ref, sr_ref, ct_ref, st_ref):
        xf = x_ref[...]
        r = lax.rsqrt(jnp.mean(xf * xf, axis=-1, keepdims=True) + NORM_EPS)
        xn_ref[...] = (xf * r * g_ref[...]).astype(BF)
        row0 = pl.program_id(0) * tm
        pos = (row0 + lax.broadcasted_iota(jnp.int32, (tm, LANE), 0)).astype(F32)
        ang_a = pos * lanes_ref[0:1, :]
        sin_a = jnp.sin(ang_a)
        ca_ref[...] = jnp.cos(ang_a)
        sa_ref[...] = sin_a * lanes_ref[1:2, :]
        sb_ref[...] = sin_a * lanes_ref[2:3, :]
        ang_r = pos * lanes_ref[3:4, :]
        cr_ref[...] = jnp.cos(ang_r)
        sr_ref[...] = jnp.sin(ang_r) * lanes_ref[4:5, :]
        pos_t = (row0 + lax.broadcasted_iota(jnp.int32, (8, tm), 1)).astype(F32)
        ang_t = pos_t * jnp.concatenate([freq_rows_ref[...]] * (tm // LANE), axis=1)
        ct_ref[...] = jnp.cos(ang_t)
        st_ref[...] = jnp.sin(ang_t)

    tab = _rows(tm, LANE)
    tab_t = _cols(8, tm)
    return _call(
        body, (x, g1, *_rope_lane_tables()), comm, name="prologue", grid=(seq // tm,),
        in_specs=[_rows(tm, D_MODEL), VMEM_FULL, VMEM_FULL, VMEM_FULL],
        out_specs=[_rows(tm, D_MODEL), tab, tab, tab, tab, tab, tab_t, tab_t],
        out_shape=[jax.ShapeDtypeStruct((seq, D_MODEL), BF)] + [jax.ShapeDtypeStruct((seq, LANE), F32)] * 5
        + [jax.ShapeDtypeStruct((8, seq), F32)] * 2)


def _ret_tables():
    c = RET_CHUNK
    lg = jnp.asarray(RET_LOG_GAMMA, F32)
    idx = jnp.arange(c, dtype=F32)
    diff = idx[:, None] - idx[None, :]
    intra = jnp.where(diff >= 0, jnp.exp(jnp.maximum(diff, 0.0) * lg[:, None, None]), 0.0)
    qd = jnp.exp((idx + 1.0)[None, :] * lg[:, None])[..., None]
    kd = jnp.exp((c - 1.0 - idx)[None, :] * lg[:, None])[..., None]
    return intra, jnp.broadcast_to(qd, (RET_HEADS, c, RET_KEY_DIM)), jnp.broadcast_to(kd, (RET_HEADS, c, RET_KEY_DIM))


def _inproj_fwd(xn1, w_in, tabs, comm=None):
    seq = xn1.shape[0]
    tm = min(TOKEN_TILE, seq)

    def body(xn_ref, w_hbm, ca_ref, sa_ref, sb_ref, cr_ref, sr_ref, ct_ref, st_ref,
             qt_ref, kv_ref, kvt_ref, qkr_ref, vr_ref, gr_ref, gates_ref, w_vmem, w_sems):
        w = _Resident(w_hbm, w_vmem, w_sems, IN_CHUNKS_FWD)
        _Resident.load(w)
        ca, sa, sb, cr, sr, ct, st = (ref[...] for ref in (ca_ref, sa_ref, sb_ref, cr_ref, sr_ref, ct_ref, st_ref))
        xn = xn_ref[...]
        qt_ref[...] = _rope_att_rows(_nt(w.chunk(0)[...], xn), ct, st, 1.0).astype(BF)
        w_kv = w.chunk(1)
        kvt = _nt(w_kv[...], xn)
        kvt_ref[:LANE, :] = _rope_att_rows(kvt[:LANE], ct, st, 1.0).astype(BF)
        kvt_ref[LANE:, :] = kvt[LANE:].astype(BF)
        kvn = _nt(xn, w_kv[...])
        kv_ref[:, :LANE] = _rope_att(kvn[:, :LANE], ca, sa, sb).astype(BF)
        kv_ref[:, LANE:] = kvn[:, LANE:].astype(BF)
        qr = _rope_ret(_nt(xn, w.chunk(2)[...]), cr, sr)
        qkr_ref[:, :512] = qr.astype(BF)
        kr = _rope_ret(_nt(xn, w.chunk(3)[...]), cr, sr) * RET_SCALE
        qkr_ref[:, 512:] = kr.astype(BF)
        vr_ref[...] = _nt(xn, w.chunk(4)[...]).astype(BF)
        gr_ref[...] = _nt(xn, w.chunk(5)[...]).astype(BF)
        gates_ref[...] = _nt(xn, w.chunk(6)[...]).astype(BF)

    tab = _rows(tm, LANE)
    tab_t = _cols(8, tm)
    return _call(
        body, (xn1, w_in, *tabs), comm, name="inproj_fwd", grid=(seq // tm,),
        in_specs=[_rows(tm, D_MODEL), ANY, tab, tab, tab, tab, tab, tab_t, tab_t],
        out_specs=[_cols(512, tm), _rows(tm, 256), _cols(256, tm), _rows(tm, 1024),
                   _rows(tm, 1024), _rows(tm, 1024), _rows(tm, 2048)],
        out_shape=[jax.ShapeDtypeStruct(s, BF) for s in ((512, seq), (seq, 256), (256, seq),
                                                         (seq, 1024), (seq, 1024), (seq, 1024), (seq, 2048))],
        scratch_shapes=_Resident.scratch(w_in, IN_CHUNKS_FWD))


ATT_Q_PER_KV = ATT_HEADS // 2


def _att_group(ref, kv):
    d = ATT_HEAD_DIM
    g = jnp.concatenate([ref[d * (ATT_Q_PER_KV * kv + j):d * (ATT_Q_PER_KV * kv + j + 1), :]
                         for j in range(ATT_Q_PER_KV)], axis=1)
    z = jnp.zeros_like(g)
    return jnp.concatenate([g, z] if kv == 0 else [z, g], axis=0)


def _att_probs(kwin, qpad, sink_ref, kv, i):
    c = ATT_BLOCK
    kj = lax.broadcasted_iota(jnp.int32, (2 * c, c), 0)
    qi = lax.broadcasted_iota(jnp.int32, (2 * c, c), 1)
    allowed = (kj > qi) & (kj <= qi + c) & ((kj >= c) | (i > 0))
    allowed = jnp.concatenate([allowed] * ATT_Q_PER_KV, axis=1)
    s = jnp.where(allowed, _nn(kwin, qpad) * ATT_SCALE, NEG_INF)
    sk = jnp.concatenate([jnp.full((1, c), sink_ref[ATT_Q_PER_KV * kv + j], F32) for j in range(ATT_Q_PER_KV)], axis=1)
    m = jnp.maximum(jnp.max(s, axis=0, keepdims=True), sk)
    pe = jnp.exp(s - m)
    psink = jnp.exp(sk - m)
    inv = 1.0 / (jnp.sum(pe, axis=0, keepdims=True) + psink)
    return pe * inv, psink * inv


def _seq_fwd(qt, kv, kvt, sinks, qkr, vr, gr, gain, rtabs, comm=None):
    seq = kv.shape[0]
    c = ATT_BLOCK
    assert c == RET_CHUNK
    d = ATT_HEAD_DIM
    nb = seq // c
    dk, dv = RET_KEY_DIM, RET_VAL_DIM

    def body(sink_ref, qt_ref, kvc_ref, kvp_ref, kvtc_ref, kvtp_ref, qk_ref, v_ref, g_ref, gain_ref, intra_ref,
             qd_ref, kd_ref, ot_ref, ret_ref, yrp_ref, st_ref, state):
        i = pl.program_id(0)
        kwin = jnp.concatenate([kvp_ref[:, :LANE], kvc_ref[:, :LANE]], axis=0)
        vtwin = jnp.concatenate([kvtp_ref[LANE:, :], kvtc_ref[LANE:, :]], axis=1)
        for g in (0, 1):
            prob, _ = _att_probs(kwin, _att_group(qt_ref, g), sink_ref, g, i)
            out = _nn(vtwin[d * g:d * (g + 1), :], prob.astype(BF))
            for j in range(ATT_Q_PER_KV):
                h = ATT_Q_PER_KV * g + j
                ot_ref[d * h:d * (h + 1), :] = out[:, c * j:c * (j + 1)].astype(BF)

        @pl.when(i == 0)
        def _():
            state[...] = jnp.zeros_like(state)

        for h in range(RET_HEADS):
            qh = qk_ref[:, dk * h:dk * (h + 1)]
            kh = qk_ref[:, 512 + dk * h:512 + dk * (h + 1)]
            vh = v_ref[:, dv * h:dv * (h + 1)]
            sh = state[h]
            shb = sh.astype(BF)
            st_ref[0, h] = shb
            att = _nt(qh, kh) * intra_ref[h]
            inner = _nn(att.astype(BF), vh)
            cross = _nn((qh.astype(F32) * qd_ref[h]).astype(BF), shb)
            out = inner + cross
            state[h] = sh * RET_CHUNK_DECAY[h] + _tn((kh.astype(F32) * kd_ref[h]).astype(BF), vh)
            ret_ref[:, dv * h:dv * (h + 1)] = out
            mu = jnp.mean(out, axis=-1, keepdims=True)
            dev = out - mu
            var = jnp.mean(dev * dev, axis=-1, keepdims=True)
            y = dev * lax.rsqrt(var + GN_EPS) * gain_ref[:, dv * h:dv * (h + 1)]
            g = g_ref[:, dv * h:dv * (h + 1)].astype(F32)
            yrp_ref[:, dv * h:dv * (h + 1)] = (g * _sigmoid(g) * y).astype(BF)

    prev = lambda i: jnp.maximum(i - 1, 0)
    return _call(
        body, (sinks, qt, kv, kv, kvt, kvt, qkr, vr, gr, gain, *rtabs), comm, name="seq_fwd", grid=(nb,),
        in_specs=[SMEM_FULL, _cols(512, c), _rows(c, 256), pl.BlockSpec((c, 256), lambda i: (prev(i), 0)),
                  _cols(256, c), pl.BlockSpec((256, c), lambda i: (0, prev(i))),
                  _rows(c, 1024), _rows(c, 1024), _rows(c, 1024), VMEM_FULL, VMEM_FULL, VMEM_FULL, VMEM_FULL],
        out_specs=[_cols(512, c), _rows(c, 1024), _rows(c, 1024),
                   pl.BlockSpec((1, RET_HEADS, dk, dv), lambda i: (i, 0, 0, 0))],
        out_shape=[jax.ShapeDtypeStruct((512, seq), BF), jax.ShapeDtypeStruct((seq, 1024), F32),
                   jax.ShapeDtypeStruct((seq, 1024), BF), jax.ShapeDtypeStruct((nb, RET_HEADS, dk, dv), BF)],
        scratch_shapes=[pltpu.VMEM((RET_HEADS, dk, dv), F32)])


def _mix_fwd(oat, yrp, gates, bg, x, w_att, w_ret, w_out):
    seq = x.shape[0]
    tm = min(TOKEN_TILE, seq)

    def body(oat_ref, yrp_ref, gates_ref, bg_ref, x_ref, wa_ref, wr_ref, wo_ref, ya_ref, yr_ref, mg_ref, h1_ref):
        ya = _tn(oat_ref[...], wa_ref[...])
        yr = _nn(yrp_ref[...], wr_ref[...])
        gt = _sigmoid(gates_ref[...].astype(F32) + bg_ref[...])
        merged = (gt[:, :D_MODEL] * ya + gt[:, D_MODEL:] * yr).astype(BF)
        ya_ref[...] = ya.astype(BF)
        yr_ref[...] = yr.astype(BF)
        mg_ref[...] = merged
        h1_ref[...] = x_ref[...] + _nn(merged, wo_ref[...])

    return pl.pallas_call(
        body, name="mix_fwd", grid=(seq // tm,),
        in_specs=[_cols(512, tm), _rows(tm, 1024), _rows(tm, 2048), VMEM_FULL, _rows(tm, D_MODEL),
                  VMEM_FULL, VMEM_FULL, VMEM_FULL],
        out_specs=[_rows(tm, D_MODEL)] * 4,
        out_shape=[jax.ShapeDtypeStruct((seq, D_MODEL), BF)] * 3 + [jax.ShapeDtypeStruct((seq, D_MODEL), F32)],
        compiler_params=_params(1),
    )(oat, yrp, gates, bg, x, w_att, w_ret, w_out)


def _mlp_loss_step(h1, g2, g3, target, w_ff1, w_ff2):
    seq = h1.shape[0]
    tm = min(MLP_TOKEN_TILE, seq)
    n_chunks = len(FF_CHUNKS) - 1

    def body(h1_ref, g2_ref, g3_ref, t_ref, w1_hbm, w2_hbm,
             xn2_ref, hdn_ref, dh2_ref, du_ref, dh1_ref, loss_ref, dg3_ref, dg2_ref,
             w1_vmem, w1_sems, w2_vmem, w2_sems, relu_u):
        w1 = _Resident(w1_hbm, w1_vmem, w1_sems, FF_CHUNKS)
        w2 = _Resident(w2_hbm, w2_vmem, w2_sems, FF_CHUNKS)
        _Resident.load(w1, w2)

        @pl.when(pl.program_id(0) == 0)
        def _():
            loss_ref[...] = jnp.zeros_like(loss_ref)
            dg3_ref[...] = jnp.zeros_like(dg3_ref)
            dg2_ref[...] = jnp.zeros_like(dg2_ref)

        h1v = h1_ref[...]
        r2 = lax.rsqrt(jnp.mean(h1v * h1v, axis=-1, keepdims=True) + NORM_EPS)
        xn2 = (h1v * r2 * g2_ref[...]).astype(BF)
        xn2_ref[...] = xn2
        h2 = h1v
        for c in range(n_chunks):
            cols = slice(FF_CHUNKS[c], FF_CHUNKS[c + 1])
            a = jnp.maximum(_nt(xn2, w1.chunk(c)[...]), 0.0)
            relu_u[:, cols] = a.astype(BF)
            hdn = jnp.square(a).astype(BF)
            hdn_ref[:, cols] = hdn
            h2 = h2 + _nn(hdn, w2.chunk(c)[...])
        r3 = lax.rsqrt(jnp.mean(h2 * h2, axis=-1, keepdims=True) + NORM_EPS)
        hn = h2 * r3
        err = hn * g3_ref[...] - t_ref[...]
        loss_ref[...] += jnp.sum(err * err) * (0.5 / D_MODEL)
        dy = err * (1.0 / D_MODEL)
        dg3_ref[...] += jnp.sum(dy * hn, axis=0, keepdims=True)
        z = dy * g3_ref[...]
        dh2 = r3 * (z - hn * jnp.mean(z * hn, axis=-1, keepdims=True))
        dh2b = dh2.astype(BF)
        dh2_ref[...] = dh2b
        dxn2 = jnp.zeros_like(dh2)
        for c in range(n_chunks):
            cols = slice(FF_CHUNKS[c], FF_CHUNKS[c + 1])
            du = (_nt(dh2b, w2.chunk(c)[...]) * (2.0 * relu_u[:, cols].astype(F32))).astype(BF)
            du_ref[:, cols] = du
            dxn2 = dxn2 + _nn(du, w1.chunk(c)[...])
        dnorm, dg = _rms_bwd(dxn2, h1v, g2_ref[...])
        dh1_ref[...] = dh2 + dnorm
        dg2_ref[...] += dg

    gain_acc = pl.BlockSpec((1, D_MODEL), lambda i: (0, 0))
    return pl.pallas_call(
        body, name="mlp_loss_step", grid=(seq // tm,),
        in_specs=[_rows(tm, D_MODEL), VMEM_FULL, VMEM_FULL, _rows(tm, D_MODEL), ANY, ANY],
        out_specs=[_rows(tm, D_MODEL), _rows(tm, D_FF), _rows(tm, D_MODEL), _rows(tm, D_FF), _rows(tm, D_MODEL),
                   pl.BlockSpec((1, LANE), lambda i: (0, 0)), gain_acc, gain_acc],
        out_shape=[jax.ShapeDtypeStruct((seq, D_MODEL), BF), jax.ShapeDtypeStruct((seq, D_FF), BF),
                   jax.ShapeDtypeStruct((seq, D_MODEL), BF), jax.ShapeDtypeStruct((seq, D_FF), BF),
                   jax.ShapeDtypeStruct((seq, D_MODEL), F32), jax.ShapeDtypeStruct((1, LANE), F32),
                   jax.ShapeDtypeStruct((1, D_MODEL), F32), jax.ShapeDtypeStruct((1, D_MODEL), F32)],
        scratch_shapes=(_Resident.scratch(w_ff1, FF_CHUNKS) + _Resident.scratch(w_ff2, FF_CHUNKS)
                        + [pltpu.VMEM((tm, D_FF), BF)]),
        compiler_params=_params(1),
    )(h1, g2, g3, target, w_ff1, w_ff2)


def _rms_bwd(dxn, xin, gain):
    r = lax.rsqrt(jnp.mean(xin * xin, axis=-1, keepdims=True) + NORM_EPS)
    xhat = xin * r
    z = dxn * gain
    dxin = r * (z - xhat * jnp.mean(z * xhat, axis=-1, keepdims=True))
    return dxin, jnp.sum(dxn * xhat, axis=0, keepdims=True)


def _mm_tn(a, b, name, a_is_t=False, comm=None):
    kdim, seq = a.shape if a_is_t else a.shape[::-1]
    ndim = b.shape[1]
    ts = min(TN_TOKEN_TILE, seq)
    tk = min(kdim, 1024)
    tn = min(ndim, 1024)
    n_steps = seq // ts

    def body(a_ref, b_ref, o_ref):
        @pl.when(pl.program_id(2) == 0)
        def _():
            o_ref[...] = jnp.zeros_like(o_ref)

        mm = _nn if a_is_t else _tn
        o_ref[...] += mm(a_ref[...].astype(BF), b_ref[...].astype(BF))

    a_spec = (pl.BlockSpec((tk, ts), lambda k, n, s: (k, s)) if a_is_t
              else pl.BlockSpec((ts, tk), lambda k, n, s: (s, k)))
    res = _call(
        body, (a, b), comm, name=name, grid=(kdim // tk, ndim // tn, n_steps),
        in_specs=[a_spec, pl.BlockSpec((ts, tn), lambda k, n, s: (s, n))],
        out_specs=pl.BlockSpec((tk, tn), lambda k, n, s: (k, n)),
        out_shape=jax.ShapeDtypeStruct((kdim, ndim), F32))
    return res[0] if comm is None else res


def _dw_in_t(pieces, is_t, xn1, comm=None):
    seq = xn1.shape[0]
    tr = 256
    n_blk = [(p.shape[0] if t else p.shape[1]) // tr for p, t in zip(pieces, is_t)]
    offs = [sum(n_blk[:i]) for i in range(len(pieces) + 1)]
    n_p = len(pieces)

    def body(*refs):
        piece_refs, xn_ref, o_ref, ob_ref = refs[:n_p], refs[n_p], refs[n_p + 1], refs[n_p + 2]
        r = pl.program_id(0)
        for p in range(n_p):
            @pl.when((r >= offs[p]) & (r < offs[p + 1]))
            def _(p=p):
                res = (_nn if is_t[p] else _tn)(piece_refs[p][...], xn_ref[...])
                o_ref[...] = res
                ob_ref[...] = res.astype(BF)

    def piece_spec(p):
        def index(r):
            blk = jnp.clip(r - offs[p], 0, n_blk[p] - 1)
            return (blk, 0) if is_t[p] else (0, blk)
        return pl.BlockSpec((tr, seq) if is_t[p] else (seq, tr), index)

    return _call(
        body, (*pieces, xn1), comm, name="dw_in", grid=(offs[-1],),
        in_specs=[piece_spec(p) for p in range(n_p)] + [VMEM_FULL],
        out_specs=[_rows(tr, D_MODEL), _rows(tr, D_MODEL)],
        out_shape=[jax.ShapeDtypeStruct((offs[-1] * tr, D_MODEL), F32),
                   jax.ShapeDtypeStruct((offs[-1] * tr, D_MODEL), BF)])


def _mix_bwd(dh1, ya, yr, gates, bg, merged, oat, yrp, w_att, w_ret, w_out, comm=None):
    seq = dh1.shape[0]
    tm = min(TOKEN_TILE, seq)

    def body(dh1_ref, ya_ref, yr_ref, gates_ref, bg_ref, mg_ref, oat_ref, yrp_ref, wa_ref, wr_ref, wo_ref,
             dgates_ref, doa_ref, dyrp_ref, db_ref, dwa_ref, dwr_ref, dwo_ref):
        @pl.when(pl.program_id(0) == 0)
        def _():
            for ref in (db_ref, dwa_ref, dwr_ref, dwo_ref):
                ref[...] = jnp.zeros_like(ref)

        dh1b = dh1_ref[...].astype(BF)
        dm = _nt(dh1b, wo_ref[...])
        gt = _sigmoid(gates_ref[...].astype(F32) + bg_ref[...])
        ga, gr = gt[:, :D_MODEL], gt[:, D_MODEL:]
        dya = (dm * ga).astype(BF)
        dyr = (dm * gr).astype(BF)
        dga = dm * ya_ref[...].astype(F32) * ga * (1.0 - ga)
        dgr = dm * yr_ref[...].astype(F32) * gr * (1.0 - gr)
        dgates_ref[:, :D_MODEL] = dga.astype(BF)
        dgates_ref[:, D_MODEL:] = dgr.astype(BF)
        db_ref[:, :D_MODEL] += jnp.sum(dga, axis=0, keepdims=True)
        db_ref[:, D_MODEL:] += jnp.sum(dgr, axis=0, keepdims=True)
        doa_ref[...] = _nt(wa_ref[...], dya).astype(BF)
        dyrp_ref[...] = _nt(dyr, wr_ref[...]).astype(BF)
        dwo_ref[...] += _tn(mg_ref[...], dh1b)
        dwr_ref[...] += _tn(yrp_ref[...], dyr)
        dwa_ref[...] += _nn(oat_ref[...], dya)

    acc = lambda r, c: pl.BlockSpec((r, c), lambda i: (0, 0))
    return _call(
        body, (dh1, ya, yr, gates, bg, merged, oat, yrp, w_att, w_ret, w_out), comm, name="mix_bwd",
        grid=(seq // tm,),
        in_specs=[_rows(tm, D_MODEL), _rows(tm, D_MODEL), _rows(tm, D_MODEL), _rows(tm, 2048), VMEM_FULL,
                  _rows(tm, D_MODEL), _cols(512, tm), _rows(tm, 1024), VMEM_FULL, VMEM_FULL, VMEM_FULL],
        out_specs=[_rows(tm, 2048), _cols(512, tm), _rows(tm, 1024), acc(1, 2048), acc(512, D_MODEL),
                   acc(1024, D_MODEL), acc(D_MODEL, D_MODEL)],
        out_shape=[jax.ShapeDtypeStruct((seq, 2048), BF), jax.ShapeDtypeStruct((512, seq), BF),
                   jax.ShapeDtypeStruct((seq, 1024), BF), jax.ShapeDtypeStruct((1, 2048), F32),
                   jax.ShapeDtypeStruct((512, D_MODEL), F32), jax.ShapeDtypeStruct((1024, D_MODEL), F32),
                   jax.ShapeDtypeStruct((D_MODEL, D_MODEL), F32)])


def _seq_bwd(qt, kv, kvt, dot, sinks, qkr, vr, gr, ret, dyrp, states, gain, rtabs, tabs, comm=None):
    seq = kv.shape[0]
    c = ATT_BLOCK
    assert c == RET_CHUNK
    d = ATT_HEAD_DIM
    nb = seq // c
    dk, dv = RET_KEY_DIM, RET_VAL_DIM
    ca, sa, sb, cr, sr, ct, st = tabs

    def body(sink_ref, qt_ref, kvc_ref, kvp_ref, kvtc_ref, kvtp_ref, dot_ref, ca_ref, sa_ref, sb_ref, ct_ref, st_ref,
             qk_ref, v_ref, g_ref, ret_ref, dyp_ref, st8_ref, gain_ref, intra_ref, qd_ref, kd_ref, cr_ref, sr_ref,
             dqt_ref, dkv_ref, dsink_ref, dall_ref, dgain_ref, carry, dstate):
        step = pl.program_id(0)
        i = nb - 1 - step

        @pl.when(step == 0)
        def _():
            carry[...] = jnp.zeros_like(carry)
            dsink_ref[...] = jnp.zeros_like(dsink_ref)
            dstate[...] = jnp.zeros_like(dstate)
            dgain_ref[...] = jnp.zeros_like(dgain_ref)

        kwin = jnp.concatenate([kvp_ref[:, :LANE], kvc_ref[:, :LANE]], axis=0)
        vwin = jnp.concatenate([kvp_ref[:, LANE:], kvc_ref[:, LANE:]], axis=0)
        ktwin = jnp.concatenate([kvtp_ref[:LANE, :], kvtc_ref[:LANE, :]], axis=1)
        ctv, stv = ct_ref[...], st_ref[...]
        lane1 = lax.broadcasted_iota(jnp.int32, (1, LANE), 1)
        dkw = jnp.zeros((2 * c, LANE), F32)
        dvw = jnp.zeros((2 * c, LANE), F32)
        dsink = jnp.zeros((1, LANE), F32)
        for g in (0, 1):
            qpad, dopad = _att_group(qt_ref, g), _att_group(dot_ref, g)
            prob, psink = _att_probs(kwin, qpad, sink_ref, g, i)
            dprob = _nn(vwin, dopad)
            drow = jnp.sum(dprob * prob, axis=0, keepdims=True)
            ds = (prob * (dprob - drow) * ATT_SCALE).astype(BF)
            dqg = _nn(ktwin[d * g:d * (g + 1), :], ds)
            dkw = dkw + _nt(ds, qpad)
            dvw = dvw + _nt(prob.astype(BF), dopad)
            dsink_lanes = psink * drow
            for j in range(ATT_Q_PER_KV):
                h = ATT_Q_PER_KV * g + j
                dqt_ref[d * h:d * (h + 1), :] = _rope_att_rows(dqg[:, c * j:c * (j + 1)], ctv, stv, -1.0).astype(BF)
                dsink = dsink + jnp.where(lane1 == h, -jnp.sum(dsink_lanes[:, c * j:c * (j + 1)]), 0.0)
        dsink_ref[...] += dsink
        total = carry[...] + jnp.concatenate([dkw[c:], dvw[c:]], axis=1)
        dkv_ref[:, :LANE] = _rope_att_t(total[:, :LANE], ca_ref[...], sa_ref[...], sb_ref[...]).astype(BF)
        dkv_ref[:, LANE:] = total[:, LANE:].astype(BF)
        carry[...] = jnp.concatenate([dkw[:c], dvw[:c]], axis=1)

        crv, srv = cr_ref[...], sr_ref[...]
        for h in range(RET_HEADS):
            vs = slice(dv * h, dv * (h + 1))
            qh = qk_ref[:, dk * h:dk * (h + 1)]
            kh = qk_ref[:, 512 + dk * h:512 + dk * (h + 1)]
            vh = v_ref[:, vs]
            out = ret_ref[:, vs]
            g = g_ref[:, vs].astype(F32)
            dyp = dyp_ref[:, vs].astype(F32)
            gain_h = gain_ref[:, vs]
            mu = jnp.mean(out, axis=-1, keepdims=True)
            dev = out - mu
            rstd = lax.rsqrt(jnp.mean(dev * dev, axis=-1, keepdims=True) + GN_EPS)
            yn = dev * rstd
            sg = _sigmoid(g)
            dg = dyp * (yn * gain_h) * (sg * (1.0 + g * (1.0 - sg)))
            dy = dyp * (g * sg)
            dgain_ref[:, vs] += jnp.sum(dy * yn, axis=0, keepdims=True)
            dyn = dy * gain_h
            dout = rstd * (dyn - jnp.mean(dyn, axis=-1, keepdims=True)
                           - yn * jnp.mean(dyn * yn, axis=-1, keepdims=True))
            doutb = dout.astype(BF)
            sc = st8_ref[0, h]
            dsp = dstate[h]
            dspb = dsp.astype(BF)
            intra, qdv, kdv = intra_ref[h], qd_ref[h], kd_ref[h]
            att = _nt(qh, kh) * intra
            dab = (_nt(doutb, vh) * intra).astype(BF)
            qdec = (qh.astype(F32) * qdv).astype(BF)
            kdec = (kh.astype(F32) * kdv).astype(BF)
            dq = _nn(dab, kh) + _nt(doutb, sc) * qdv
            dkk = _tn(dab, qh) + _nt(vh, dspb) * kdv
            dvv = _tn(att.astype(BF), doutb) + _nn(kdec, dspb)
            dstate[h] = dsp * RET_CHUNK_DECAY[h] + _tn(qdec, doutb)
            dall_ref[:, dk * h:dk * (h + 1)] = _rope_ret_t(dq, crv, srv).astype(BF)
            dall_ref[:, 512 + dk * h:512 + dk * (h + 1)] = (_rope_ret_t(dkk, crv, srv) * RET_SCALE).astype(BF)
            dall_ref[:, 1024 + dv * h:1024 + dv * (h + 1)] = dvv.astype(BF)
            dall_ref[:, 2048 + dv * h:2048 + dv * (h + 1)] = dg.astype(BF)

    cur = lambda s: nb - 1 - s
    prev = lambda s: jnp.maximum(nb - 2 - s, 0)
    rows = lambda w: pl.BlockSpec((c, w), lambda s: (cur(s), 0))
    cols = lambda h: pl.BlockSpec((h, c), lambda s: (0, cur(s)))
    acc = lambda w: pl.BlockSpec((1, w), lambda s: (0, 0))
    return _call(
        body, (sinks, qt, kv, kv, kvt, kvt, dot, ca, sa, sb, ct, st, qkr, vr, gr, ret, dyrp, states, gain, *rtabs,
               cr, sr), comm, name="seq_bwd", grid=(nb,),
        in_specs=[SMEM_FULL, cols(512), rows(256), pl.BlockSpec((c, 256), lambda s: (prev(s), 0)), cols(256),
                  pl.BlockSpec((256, c), lambda s: (0, prev(s))), cols(512), rows(LANE), rows(LANE), rows(LANE),
                  cols(8), cols(8), rows(1024), rows(1024), rows(1024), rows(1024), rows(1024),
                  pl.BlockSpec((1, RET_HEADS, dk, dv), lambda s: (cur(s), 0, 0, 0)),
                  VMEM_FULL, VMEM_FULL, VMEM_FULL, VMEM_FULL, rows(LANE), rows(LANE)],
        out_specs=[cols(512), rows(256), acc(LANE), rows(3072), acc(1024)],
        out_shape=[jax.ShapeDtypeStruct((512, seq), BF), jax.ShapeDtypeStruct((seq, 256), BF),
                   jax.ShapeDtypeStruct((1, LANE), F32), jax.ShapeDtypeStruct((seq, 3072), BF),
                   jax.ShapeDtypeStruct((1, 1024), F32)],
        scratch_shapes=[pltpu.VMEM((c, 256), F32), pltpu.VMEM((RET_HEADS, dk, dv), F32)])


def _inproj_bwd(dqa, dkva, dret, dgates, dh1, x, g1, w_in, comm=None):
    seq = x.shape[0]
    tm = min(TOKEN_TILE, seq)

    def body(dqa_ref, dkva_ref, dret_ref, dgates_ref, dh1_ref, x_ref, g_ref, w_hbm, dx_ref, dg1_ref, w_vmem, w_sems):
        w = _Resident(w_hbm, w_vmem, w_sems, IN_CHUNKS_BWD)
        _Resident.load(w)

        @pl.when(pl.program_id(0) == 0)
        def _():
            dg1_ref[...] = jnp.zeros_like(dg1_ref)

        dxn = _tn(dqa_ref[...], w.chunk(0)[...])
        dxn = dxn + _nn(dkva_ref[...], w.chunk(1)[...])
        dxn = dxn + _nn(dret_ref[...], w.chunk(2)[...])
        dxn = dxn + _nn(dgates_ref[...], w.chunk(3)[...])
        dnorm, dg = _rms_bwd(dxn, x_ref[...], g_ref[...])
        dx_ref[...] = dh1_ref[...] + dnorm
        dg1_ref[...] += dg

    return _call(
        body, (dqa, dkva, dret, dgates, dh1, x, g1, w_in), comm, name="inproj_bwd", grid=(seq // tm,),
        in_specs=[_cols(512, tm), _rows(tm, 256), _rows(tm, 3072), _rows(tm, 2048), _rows(tm, D_MODEL),
                  _rows(tm, D_MODEL), VMEM_FULL, ANY],
        out_specs=[_rows(tm, D_MODEL), pl.BlockSpec((1, D_MODEL), lambda i: (0, 0))],
        out_shape=[jax.ShapeDtypeStruct((seq, D_MODEL), F32), jax.ShapeDtypeStruct((1, D_MODEL), F32)],
        scratch_shapes=_Resident.scratch(w_in, IN_CHUNKS_BWD))


def _local_step(x, target, g1, bg, sinks, gain, g2, g3, shards):
    rtabs = _ret_tables()
    scatter = _Scatter

    comm, unpack = _gather(shards, ("w_in",), relay=True)
    (xn1, ca, sa, sb, cr, sr, ct, st), got = _prologue(x, g1, comm)
    (w_in,) = unpack(got)
    comm_mid, unpack_mid = _gather(shards, ("w_att_up", "w_ret_up", "w_out"), relay=True)
    comm_ff1, unpack_ff1 = _gather(shards, ("w_ff1",), relay=True)
    (qt, kv, kvt, qkr, vr, gr, gates), got = _inproj_fwd(xn1, w_in, (ca, sa, sb, cr, sr, ct, st),
                                                         _join(comm_mid, comm_ff1))
    w_att, w_ret, w_out = unpack_mid(got[:3])
    (w_ff1,) = unpack_ff1(got[3:])
    comm, unpack = _gather(shards, ("w_ff2",), relay=True)
    (oa, ret, yrp, states), got = _seq_fwd(qt, kv, kvt, sinks, qkr, vr, gr, gain, rtabs, comm)
    (w_ff2,) = unpack(got)
    ya, yr, merged, h1 = _mix_fwd(oa, yrp, gates, bg, x, w_att, w_ret, w_out)
    xn2, hdn, dh2, du, dh1, loss, dg3, dg2 = _mlp_loss_step(h1, g2, g3, target, w_ff1, w_ff2)

    ff2 = scatter(dict(w_ff2=_mm_tn(hdn, dh2, "dw_ff2")), "ff2")
    d_ff1, r1 = _mm_tn(du, xn2, "dw_ff1", comm=ff2.pair_comm())
    ff2.pair_done(r1)
    ff1 = scatter(dict(w_ff1=d_ff1), "ff1")
    (dgates, doa, dyrp, db, d_att, d_ret, d_out), r1 = _mix_bwd(dh1, ya, yr, gates, bg, merged, oa, yrp,
                                                                w_att, w_ret, w_out, ff1.pair_comm())
    ff1.pair_done(r1)
    mid = scatter(dict(w_att_up=d_att, w_ret_up=d_ret, w_out=d_out), "mid")
    (dqa, dkva, dsink, dret, dgain), got = _seq_bwd(
        qt, kv, kvt, doa, sinks, qkr, vr, gr, ret, dyrp, states, gain, rtabs, (ca, sa, sb, cr, sr, ct, st),
        _join(_join(ff2.chip_comm(), ff1.chip_comm()), mid.pair_comm()))
    ff2.chip_done(got[:1])
    ff1.chip_done(got[1:2])
    mid.pair_done(got[2:])
    (d_in, d_in_bf), r2 = _dw_in_t((dqa, dkva, dret, dgates), (True, False, False, False), xn1, mid.chip_comm())
    mid.chip_done(r2)
    win = scatter(dict(w_in=d_in), "in", sent=dict(w_in=d_in_bf))
    win.pair_done(_comm_alone(win.pair_comm(), "pair_exchange_in"))
    (dx, dg1), r2 = _inproj_bwd(dqa, dkva, dret, dgates, dh1, x, g1, w_in, win.chip_comm())
    win.chip_done(r2)
    small = dict(norm_mix_gain=dg1, b_gates=db, attn_sinks=dsink, ret_gn_gain=dgain, norm_mlp_gain=dg2,
                 norm_final_gain=dg3)
    return loss, dx, small, (win, mid, ff1, ff2)


def _coords():
    return lax.axis_index("x"), lax.axis_index("y"), lax.axis_index("c")


def _flip(v, bit):
    return 1 - v if bit else v


def _xor(a, b):
    return a + b - 2 * a * b


def _gather_relay_comm(shards):
    n = len(shards)

    def parts(ins, outs, sems):
        send_sems, recv_sems, local_sems = sems
        x, y, c = _coords()
        me, sib = (x, y, c), (x, y, 1 - c)

        def ring(core):
            return ((_xor(x, core), _xor(y, 1 - core), core), (_xor(x, 1 - core), _xor(y, core), core),
                    (1 - x, 1 - y, core))

        def slot(a, blk):
            return outs[a].at[4 * blk[0] + 2 * blk[1] + blk[2]]

        def copy(a, k, blk, to, src=None):
            return pltpu.make_async_remote_copy(
                src_ref=slot(a, blk) if src is None else src, dst_ref=slot(a, blk),
                send_sem=send_sems.at[a, k], recv_sem=recv_sems.at[a, k], device_id=to, device_id_type=MESH)

        mine = [pltpu.make_async_copy(ins[a], slot(a, me), local_sems.at[a]) for a in range(n)]
        return me, sib, ring, copy, mine

    def first_copies(ins, outs, sems):
        me, sib, ring, copy, mine = parts(ins, outs, sems)
        src, dst, _ = ring(me[2])
        return mine, [copy(a, k, me, to, src=ins[a]) for a in range(n) for k, to in ((1, src), (2, dst), (0, sib))]

    def start(ins, outs, sems):
        mine, first = first_copies(ins, outs, sems)
        for cp in mine + first:
            cp.start()

    def relay(ins, outs, sems):
        me, sib, ring, copy, _ = parts(ins, outs, sems)
        src, dst, _ = ring(me[2])
        for a in range(n):
            copy(a, 1, src, me).wait_recv()
            copy(a, 3, src, dst).start()
            copy(a, 4, src, sib).start()

    def finish(ins, outs, sems):
        me, sib, ring, copy, _ = parts(ins, outs, sems)
        mine, first = first_copies(ins, outs, sems)
        src, dst, diag = ring(me[2])
        passed = [copy(a, k, src, to) for a in range(n) for k, to in ((3, dst), (4, sib))]
        for k, blk in ((2, dst), (3, diag)):
            for a in range(n):
                copy(a, k, blk, me).wait_recv()
                fwd = copy(a, k + 3, blk, sib)
                fwd.start()
                passed.append(fwd)
        s_src, s_dst, s_diag = ring(sib[2])
        for a in range(n):
            for k, blk in ((0, sib), (4, s_src), (5, s_dst), (6, s_diag)):
                copy(a, k, blk, me).wait_recv()
        for cp in first + passed:
            cp.wait_send()
        for cp in mine:
            cp.wait()

    return _Comm(list(shards), [jax.ShapeDtypeStruct((N_DEV,) + s.shape, s.dtype) for s in shards],
                 [pltpu.SemaphoreType.DMA((n, 7)), pltpu.SemaphoreType.DMA((n, 7)), pltpu.SemaphoreType.DMA((n,))],
                 start, finish, relay)


def _gather_comm(shards):
    n = len(shards)

    def parts(ins, outs, sems):
        send_sems, recv_sems, local_sems = sems
        x, y, c = _coords()
        me, sib = (x, y, c), (x, y, 1 - c)
        chips = [(1 - x, y), (x, 1 - y), (1 - x, 1 - y)]

        def slot(a, blk):
            return outs[a].at[4 * blk[0] + 2 * blk[1] + blk[2]]

        def copy(a, k, blk, to, src=None):
            return pltpu.make_async_remote_copy(
                src_ref=slot(a, blk) if src is None else src, dst_ref=slot(a, blk),
                send_sem=send_sems.at[a, k], recv_sem=recv_sems.at[a, k], device_id=to, device_id_type=MESH)

        mine = [pltpu.make_async_copy(ins[a], slot(a, me), local_sems.at[a]) for a in range(n)]
        first = []
        for a in range(n):
            first.append(copy(a, 0, me, sib, src=ins[a]))
            first += [copy(a, 1 + j, me, (*chip, c), src=ins[a]) for j, chip in enumerate(chips)]
        return me, sib, chips, c, copy, mine, first

    def start(ins, outs, sems):
        *_, mine, first = parts(ins, outs, sems)
        for cp in mine + first:
            cp.start()

    def finish(ins, outs, sems):
        me, sib, chips, c, copy, mine, first = parts(ins, outs, sems)
        passed = []
        for j, chip in enumerate(chips):
            for a in range(n):
                copy(a, 1 + j, (*chip, c), me).wait_recv()
                fwd = copy(a, 4 + j, (*chip, c), sib)
                fwd.start()
                passed.append(fwd)
        for a in range(n):
            copy(a, 0, sib, me).wait_recv()
            for j, chip in enumerate(chips):
                copy(a, 4 + j, (*chip, 1 - c), me).wait_recv()
        for cp in first + passed:
            cp.wait_send()
        for cp in mine:
            cp.wait()

    return _Comm(list(shards), [jax.ShapeDtypeStruct((N_DEV,) + s.shape, s.dtype) for s in shards],
                 [pltpu.SemaphoreType.DMA((n, 7)), pltpu.SemaphoreType.DMA((n, 7)), pltpu.SemaphoreType.DMA((n,))],
                 start, finish)


COLUMN_SHARDED = ("w_in", "w_ff1")
COLUMN_RELAID = ("w_att_up",)


def _gather(shards, names, relay=False):
    def unpack(got):
        return [jnp.transpose(g, (1, 0, 2)).reshape(g.shape[1], N_DEV * g.shape[2]) if k in COLUMN_RELAID
                else g.reshape(N_DEV * g.shape[1], g.shape[2]) for k, g in zip(names, got)]

    return (_gather_relay_comm if relay else _gather_comm)([shards[k] for k in names]), unpack


def _pair_comm(grads):
    n = len(grads)

    def copies(g, r1, sems):
        send_sems, recv_sems = sems
        x, y, c = _coords()
        return [pltpu.make_async_remote_copy(
            src_ref=g[a].at[2 * j + (1 - c)], dst_ref=r1[a].at[j], send_sem=send_sems.at[a, j],
            recv_sem=recv_sems.at[a, j], device_id=(x, y, 1 - c), device_id_type=MESH)
            for a in range(n) for j in range(4)]

    def start(g, r1, sems):
        for cp in copies(g, r1, sems):
            cp.start()

    def finish(g, r1, sems):
        for cp in copies(g, r1, sems):
            cp.wait_recv()
        for cp in copies(g, r1, sems):
            cp.wait_send()

    return _Comm(list(grads), [jax.ShapeDtypeStruct((4,) + g.shape[1:], g.dtype) for g in grads],
                 [pltpu.SemaphoreType.DMA((n, 4)), pltpu.SemaphoreType.DMA((n, 4))], start, finish)


def _small_comm(small):
    def parts(ins, outs, sems):
        (small_ref,), (small_all,) = ins, outs
        ssend, srecv, lsem = sems
        x, y, c = _coords()
        me_idx = 4 * x + 2 * y + c
        own = pltpu.make_async_copy(small_ref, small_all.at[me_idx], lsem)
        sends, recvs = [], []
        for r in range(1, N_DEV):
            px, py, pc = _flip(x, r & 4), _flip(y, r & 2), _flip(c, r & 1)
            sends.append(pltpu.make_async_remote_copy(
                src_ref=small_ref, dst_ref=small_all.at[me_idx], send_sem=ssend.at[r - 1], recv_sem=srecv.at[r - 1],
                device_id=(px, py, pc), device_id_type=MESH))
            recvs.append(pltpu.make_async_remote_copy(
                src_ref=small_ref, dst_ref=small_all.at[4 * px + 2 * py + pc], send_sem=ssend.at[r - 1],
                recv_sem=srecv.at[r - 1], device_id=(px, py, pc), device_id_type=MESH))
        return own, sends, recvs

    def start(ins, outs, sems):
        own, sends, _ = parts(ins, outs, sems)
        own.start()
        for cp in sends:
            cp.start()

    def finish(ins, outs, sems):
        own, sends, recvs = parts(ins, outs, sems)
        for cp in recvs:
            cp.wait_recv()
        for cp in sends:
            cp.wait_send()
        own.wait()

    return _Comm([small], [jax.ShapeDtypeStruct((N_DEV,) + small.shape, small.dtype)],
                 [pltpu.SemaphoreType.DMA((N_DEV - 1,)), pltpu.SemaphoreType.DMA((N_DEV - 1,)),
                  pltpu.SemaphoreType.DMA], start, finish)


def _pair_sum(grads, r1s, c_arr, tag):
    n = len(grads)
    q = 1

    def body(c_ref, *refs):
        g, r, t = refs[:n], refs[n:2 * n], refs[2 * n:]
        for a in range(n):
            t[a][...] = (g[a][...] + r[a][...]).astype(t[a].dtype)

    def blk(arr):
        return (1, arr.shape[1] // q, arr.shape[2])

    grid_spec = pltpu.PrefetchScalarGridSpec(
        num_scalar_prefetch=1, grid=(4, q),
        in_specs=[pl.BlockSpec(blk(g), lambda j, s, c_ref: (2 * j + c_ref[0], s, 0)) for g in grads]
        + [pl.BlockSpec(blk(r), lambda j, s, c_ref: (j, s, 0)) for r in r1s],
        out_specs=[pl.BlockSpec(blk(r), lambda j, s, c_ref: (j, s, 0)) for r in r1s])
    return pl.pallas_call(
        body, name="pair_sum_" + tag, grid_spec=grid_spec,
        out_shape=[jax.ShapeDtypeStruct(r.shape, RS_PAYLOAD) for r in r1s],
        compiler_params=_params(2),
    )(c_arr, *grads, *r1s)


def _chip_comm(ts):
    n = len(ts)

    def copies(t, r2, sems):
        send_sems, recv_sems = sems
        x, y, c = _coords()
        out = []
        for a in range(n):
            for r in range(1, 4):
                tx, ty = _flip(x, r & 2), _flip(y, r & 1)
                out.append(pltpu.make_async_remote_copy(
                    src_ref=t[a].at[2 * tx + ty], dst_ref=r2[a].at[r - 1], send_sem=send_sems.at[a, r - 1],
                    recv_sem=recv_sems.at[a, r - 1], device_id=(tx, ty, c), device_id_type=MESH))
        return out

    def start(t, r2, sems):
        for cp in copies(t, r2, sems):
            cp.start()

    def finish(t, r2, sems):
        for cp in copies(t, r2, sems):
            cp.wait_recv()
        for cp in copies(t, r2, sems):
            cp.wait_send()

    return _Comm(list(ts), [jax.ShapeDtypeStruct((3,) + t.shape[1:], t.dtype) for t in ts],
                 [pltpu.SemaphoreType.DMA((n, 3)), pltpu.SemaphoreType.DMA((n, 3))], start, finish)


class _Scatter:
    def __init__(self, grads, tag, sent=None):
        def stack(k, g):
            if k in COLUMN_RELAID:
                return jnp.transpose(g.reshape(g.shape[0], N_DEV, g.shape[1] // N_DEV), (1, 0, 2))
            return g.reshape(N_DEV, g.shape[0] // N_DEV, g.shape[1])

        self.names, self.tag = tuple(grads), tag
        self.stacks = [stack(k, g) for k, g in grads.items()]
        self.sent = self.stacks if sent is None else [stack(k, sent[k]) for k in grads]

    def pair_comm(self):
        return _pair_comm(self.sent)

    def pair_done(self, r1s):
        self.r1s = list(r1s)
        c_arr = jnp.reshape(lax.axis_index("c"), (1,)).astype(jnp.int32)
        self.ts = _pair_sum(self.stacks, self.r1s, c_arr, self.tag)

    def chip_comm(self):
        return _chip_comm(self.ts)

    def chip_done(self, r2s):
        self.r2s = list(r2s)


def _adamw(w, g, m, v):
    m = ADAM_B1 * m + (1.0 - ADAM_B1) * g
    v = ADAM_B2 * v + (1.0 - ADAM_B2) * jnp.square(g)
    m_hat = m / (1.0 - ADAM_B1 ** ADAM_STEP)
    v_hat = v / (1.0 - ADAM_B2 ** ADAM_STEP)
    delta = -ADAM_LR * (m_hat / (jnp.sqrt(v_hat) + ADAM_EPS) + ADAM_WD * w)
    return delta, m, v


ADAM_STEPS = 4


def _piece_specs(stacks):
    def rows(s):
        return s.shape[1] // ADAM_STEPS
    return ([pl.BlockSpec((1, rows(s), s.shape[2]), lambda i, idx_ref: (idx_ref[0], i, 0)) for s in stacks]
            + [pl.BlockSpec((1, rows(s), s.shape[2]), lambda i, idx_ref: (idx_ref[1], i, 0)) for s in stacks]
            + [pl.BlockSpec((3, rows(s), s.shape[2]), lambda i, idx_ref: (0, i, 0)) for s in stacks])


def _piece_sum(g0, r1, r2):
    return (((g0[0] + r1[0]) + r2[0].astype(F32)) + r2[1].astype(F32)) + r2[2].astype(F32)


def _shard_sum(stacks, r1s, r2s, idx_arr):
    n = len(stacks)

    def body(idx_ref, *refs):
        g0, r1, r2, outs = (refs[k * n:(k + 1) * n] for k in range(4))
        for a in range(n):
            outs[a][...] = _piece_sum(g0[a], r1[a], r2[a])

    out_specs = [pl.BlockSpec((s.shape[1] // ADAM_STEPS, s.shape[2]), lambda i, idx_ref: (i, 0)) for s in stacks]
    grid_spec = pltpu.PrefetchScalarGridSpec(num_scalar_prefetch=1, grid=(ADAM_STEPS,),
                                             in_specs=_piece_specs(stacks), out_specs=out_specs)
    return pl.pallas_call(
        body, name="shard_sum", grid_spec=grid_spec,
        out_shape=[jax.ShapeDtypeStruct(s.shape[1:], F32) for s in stacks],
        compiler_params=_params(1),
    )(idx_arr, *stacks, *r1s, *r2s)


def _adam_big(pieces, summed, ws, ms, vs, idx_arr):
    stacks, r1s, r2s = pieces
    n_p, n = len(stacks), len(ws)

    def body(idx_ref, *refs):
        it = iter(refs)
        g0, r1, r2, gs, w, m, v = ([next(it) for _ in range(k)] for k in (n_p, n_p, n_p, n - n_p, n, n, n))
        outs = list(it)
        for a in range(n):
            g = _piece_sum(g0[a], r1[a], r2[a]) if a < n_p else gs[a - n_p][...]
            delta, nm, nv = _adamw(w[a][...], g, m[a][...], v[a][...])
            outs[4 * a][...] = g
            outs[4 * a + 1][...] = delta
            outs[4 * a + 2][...] = nm
            outs[4 * a + 3][...] = nv

    def shard_spec(w):
        return pl.BlockSpec((w.shape[0] // ADAM_STEPS, w.shape[1]), lambda i, idx_ref: (i, 0))

    in_specs = _piece_specs(stacks) + [shard_spec(w) for w in ws[n_p:]] + [shard_spec(w) for w in ws] * 3
    out_specs = [shard_spec(w) for w in ws for _ in range(4)]
    grid_spec = pltpu.PrefetchScalarGridSpec(num_scalar_prefetch=1, grid=(ADAM_STEPS,), in_specs=in_specs,
                                             out_specs=out_specs)
    return pl.pallas_call(
        body, name="adam_big", grid_spec=grid_spec,
        out_shape=[jax.ShapeDtypeStruct(w.shape, F32) for w in ws for _ in range(4)],
        compiler_params=_params(1),
    )(idx_arr, *stacks, *r1s, *r2s, *summed, *ws, *ms, *vs)


def _adam_small(small_all, w, m, v):
    def body(all_ref, w_ref, m_ref, v_ref, g_ref, d_ref, nm_ref, nv_ref):
        g = all_ref[0]
        for k in range(1, N_DEV):
            g = g + all_ref[k]
        delta, nm, nv = _adamw(w_ref[...], g, m_ref[...], v_ref[...])
        g_ref[...] = g
        d_ref[...] = delta
        nm_ref[...] = nm
        nv_ref[...] = nv

    return pl.pallas_call(
        body, name="adam_small",
        in_specs=[VMEM_FULL] * 4, out_specs=[VMEM_FULL] * 4,
        out_shape=[jax.ShapeDtypeStruct(w.shape, F32)] * 4,
    )(small_all, w, m, v)


RS_PAYLOAD = BF


def _pack_rows(rows):
    rid = lax.broadcasted_iota(jnp.int32, (8, D_MODEL), 0)
    out = jnp.zeros((8, D_MODEL), F32)
    for i, r in enumerate(rows):
        out = jnp.where(rid == i, jnp.broadcast_to(r, (8, D_MODEL)), out)
    return out


def _small_rows(norm_mix_gain, b_gates, attn_sinks, ret_gn_gain, norm_mlp_gain, norm_final_gain):
    return [norm_mix_gain, b_gates[:, :D_MODEL], b_gates[:, D_MODEL:], ret_gn_gain, norm_mlp_gain,
            norm_final_gain.reshape(1, D_MODEL), jnp.pad(attn_sinks, ((0, 0), (0, D_MODEL - ATT_HEADS)))]


def _unpack_small(p):
    return dict(norm_mix_gain=p[0:1], b_gates=jnp.concatenate([p[1:2], p[2:3]], axis=1), ret_gn_gain=p[3:4],
                norm_mlp_gain=p[4:5], norm_final_gain=p[5], attn_sinks=p[6:7, :ATT_HEADS])


WEIGHTS = ("norm_mix_gain", "w_in", "b_gates", "attn_sinks", "ret_gn_gain", "w_att_up", "w_ret_up", "w_out",
           "norm_mlp_gain", "w_ff1", "w_ff2", "norm_final_gain")
BIG = ("w_in", "w_att_up", "w_ret_up", "w_out", "w_ff1", "w_ff2")


def kernel(x, norm_mix_gain, w_in, b_gates, attn_sinks, ret_gn_gain, w_att_up, w_ret_up, w_out, norm_mlp_gain, w_ff1, w_ff2, norm_final_gain, loss_target, m_norm_mix_gain, m_w_in, m_b_gates, m_attn_sinks, m_ret_gn_gain, m_w_att_up, m_w_ret_up, m_w_out, m_norm_mlp_gain, m_w_ff1, m_w_ff2, m_norm_final_gain, v_norm_mix_gain, v_w_in, v_b_gates, v_attn_sinks, v_ret_gn_gain, v_w_att_up, v_w_ret_up, v_w_out, v_norm_mlp_gain, v_w_ff1, v_w_ff2, v_norm_final_gain):
    w = dict(norm_mix_gain=norm_mix_gain, w_in=w_in, b_gates=b_gates, attn_sinks=attn_sinks, ret_gn_gain=ret_gn_gain,
             w_att_up=w_att_up, w_ret_up=w_ret_up, w_out=w_out, norm_mlp_gain=norm_mlp_gain, w_ff1=w_ff1,
             w_ff2=w_ff2, norm_final_gain=norm_final_gain)
    m = dict(norm_mix_gain=m_norm_mix_gain, w_in=m_w_in, b_gates=m_b_gates, attn_sinks=m_attn_sinks,
             ret_gn_gain=m_ret_gn_gain, w_att_up=m_w_att_up, w_ret_up=m_w_ret_up, w_out=m_w_out,
             norm_mlp_gain=m_norm_mlp_gain, w_ff1=m_w_ff1, w_ff2=m_w_ff2, norm_final_gain=m_norm_final_gain)
    v = dict(norm_mix_gain=v_norm_mix_gain, w_in=v_w_in, b_gates=v_b_gates, attn_sinks=v_attn_sinks,
             ret_gn_gain=v_ret_gn_gain, w_att_up=v_w_att_up, w_ret_up=v_w_ret_up, w_out=v_w_out,
             norm_mlp_gain=v_norm_mlp_gain, w_ff1=v_w_ff1, w_ff2=v_w_ff2, norm_final_gain=v_norm_final_gain)

    shards = {k: (w[k][0].T if k in COLUMN_SHARDED else w[k][0]).astype(BF) for k in BIG}
    loss_p, dx, small, groups = _local_step(
        x[0], loss_target[0], norm_mix_gain, b_gates, attn_sinks[0], ret_gn_gain, norm_mlp_gain,
        norm_final_gain.reshape(1, D_MODEL), shards)

    by_name = {}
    for grp in groups:
        for k, stack, r1, r2 in zip(grp.names, grp.stacks, grp.r1s, grp.r2s):
            by_name[k] = (stack, r1, r2)
    lane = lax.broadcasted_iota(jnp.int32, (1, D_MODEL), 1)
    sink_row = jnp.where(lane < ATT_HEADS, jnp.pad(small["attn_sinks"], ((0, 0), (0, D_MODEL - LANE))),
                         jnp.where(lane == ATT_HEADS, jnp.pad(loss_p, ((0, 0), (0, D_MODEL - LANE)), mode="edge"),
                                   0.0))
    small_pack = _pack_rows([small["norm_mix_gain"], small["b_gates"][:, :D_MODEL], small["b_gates"][:, D_MODEL:],
                             small["ret_gn_gain"], small["norm_mlp_gain"], small["norm_final_gain"], sink_row])
    xi, yi, ci = _coords()
    (small_all,) = _comm_alone(_small_comm(small_pack), "small_exchange")
    idx_arr = jnp.stack([4 * xi + 2 * yi + ci, 2 * xi + yi]).astype(jnp.int32)
    in_pieces = ["w_att_up", "w_ret_up", "w_out", "w_ff2", "w_in"]
    in_sum = ["w_ff1"]
    order = in_pieces + in_sum
    summed_t = _shard_sum(*[[by_name[k][i] for k in in_sum] for i in range(3)], idx_arr)

    def shard(tree, k):
        return tree[k][0].T if k == "w_in" else tree[k][0]

    adam_out = _adam_big([[by_name[k][i] for k in in_pieces] for i in range(3)], [g.T for g in summed_t],
                         [shard(w, k) for k in order], [shard(m, k) for k in order], [shard(v, k) for k in order],
                         idx_arr)
    big_out = [adam_out[4 * order.index(k) + i].T if k == "w_in" else adam_out[4 * order.index(k) + i]
               for k in BIG for i in range(4)]
    sm_g, sm_d, sm_m, sm_v = _adam_small(small_all, _pack_rows(_small_rows(*[w[k] for k in WEIGHTS if k not in BIG])),
                                         _pack_rows(_small_rows(*[m[k] for k in WEIGHTS if k not in BIG])),
                                         _pack_rows(_small_rows(*[v[k] for k in WEIGHTS if k not in BIG])))

    loss = sm_g[6, ATT_HEADS]
    kinds = []
    for idx, packed in enumerate((sm_g, sm_d, sm_m, sm_v)):
        out = _unpack_small(packed)
        for a, k in enumerate(BIG):
            out[k] = big_out[4 * a + idx][None]
        kinds.append(out)
    return (loss, dx[None], *[kinds[0][k] for k in WEIGHTS], *[kinds[1][k] for k in WEIGHTS],
            *[kinds[2][k] for k in WEIGHTS], *[kinds[3][k] for k in WEIGHTS])
```

```python
import functools
import math

import jax
import jax.numpy as jnp
from jax import lax
from jax.experimental import pallas as pl
from jax.experimental.pallas import tpu as pltpu

F32 = jnp.float32
BF = jnp.bfloat16
MESH = pl.DeviceIdType.MESH

D_MODEL = 1024
ATT_HEADS = 8
ATT_HEAD_DIM = 64
ATT_BLOCK = 128
ROPE_DIM = 16
ROPE_THETA = 500000.0
RET_HEADS = 4
RET_KEY_DIM = 128
RET_VAL_DIM = 256
RET_CHUNK = 128
RET_ROT_BASE = 10000.0
D_FF = 4096
NORM_EPS = 1e-6
GN_EPS = 1e-6
NEG_INF = -1e30
ATT_SCALE = ATT_HEAD_DIM ** -0.5
RET_SCALE = RET_KEY_DIM ** -0.5

C_QA, C_KA, C_VA, C_QR, C_KR, C_VR, C_GR, C_GATES, C_END = 0, 512, 640, 768, 1280, 1792, 2816, 3840, 5888

ADAM_LR = 0.001
ADAM_B1 = 0.9
ADAM_B2 = 0.999
ADAM_EPS = 1e-08
ADAM_WD = 0.01
ADAM_STEP = 10

N_DEV = 8
LANE = 128
VMEM_LIMIT = 56 * 1024 * 1024
TOKEN_TILE = 512
MLP_TOKEN_TILE = 256
TN_TOKEN_TILE = 2048
FF_CHUNKS = (0, 1024, 2048, 3072, 4096)
IN_CHUNKS_FWD = (C_QA, C_KA, C_QR, C_KR, C_VR, C_GR, C_GATES, C_END)
IN_CHUNKS_BWD = (C_QA, C_KA, C_QR, C_GATES, C_END)

RET_LOG_GAMMA = tuple(math.log1p(-(2.0 ** (-5.0 - h))) for h in range(RET_HEADS))
RET_CHUNK_DECAY = tuple(math.exp(RET_CHUNK * lg) for lg in RET_LOG_GAMMA)

VMEM_FULL = pl.BlockSpec(memory_space=pltpu.VMEM)
SMEM_FULL = pl.BlockSpec(memory_space=pltpu.SMEM)
ANY = pl.BlockSpec(memory_space=pl.ANY)


def _params(n_axes):
    return pltpu.CompilerParams(dimension_semantics=("arbitrary",) * n_axes, vmem_limit_bytes=VMEM_LIMIT)


def _nn(a, b):
    return jnp.dot(a, b, preferred_element_type=F32)


def _nt(a, b):
    return lax.dot_general(a, b, (((1,), (1,)), ((), ())), preferred_element_type=F32)


def _tn(a, b):
    return lax.dot_general(a, b, (((0,), (0,)), ((), ())), preferred_element_type=F32)


def _sigmoid(v):
    return 1.0 / (1.0 + jnp.exp(-v))


def _rows(tile, width):
    return pl.BlockSpec((tile, width), lambda i: (i, 0))


def _cols(height, tile):
    return pl.BlockSpec((height, tile), lambda i: (0, i))


class _Resident:
    def __init__(self, hbm_ref, vmem_ref, sems, bounds):
        self.hbm, self.vmem, self.sems, self.bounds = hbm_ref, vmem_ref, sems, bounds

    @staticmethod
    def scratch(w, bounds):
        return [pltpu.VMEM(w.shape, w.dtype), pltpu.SemaphoreType.DMA((len(bounds) - 1,))]

    @staticmethod
    def load(*weights):
        @pl.when(pl.program_id(0) == 0)
        def _():
            copies = [w._copy(c) for w in weights for c in range(len(w.bounds) - 1)]
            for cp in copies:
                cp.start()
            for cp in copies:
                cp.wait()

    def _rows(self, c):
        return pl.ds(self.bounds[c], self.bounds[c + 1] - self.bounds[c])

    def _copy(self, c):
        return pltpu.make_async_copy(self.hbm.at[self._rows(c)], self.vmem.at[self._rows(c)], self.sems.at[c])

    def chunk(self, c):
        return self.vmem.at[self._rows(c)]


class _Comm:
    def __init__(self, inputs, out_shapes, scratch, start, finish, relay=None):
        self.inputs, self.out_shapes, self.scratch = inputs, out_shapes, scratch
        self.start, self.finish, self.relay = start, finish, relay


RELAY_AT = 0.75


def _join(a, b):
    na_in, na_out, na_sem = len(a.inputs), len(a.out_shapes), len(a.scratch)

    def both(name):
        def run(ins, outs, sems):
            for part, args in ((a, (ins[:na_in], outs[:na_out], sems[:na_sem])),
                               (b, (ins[na_in:], outs[na_out:], sems[na_sem:]))):
                if getattr(part, name) is not None:
                    getattr(part, name)(*args)
        return run

    return _Comm(list(a.inputs) + list(b.inputs), list(a.out_shapes) + list(b.out_shapes),
                 list(a.scratch) + list(b.scratch), both("start"), both("finish"),
                 both("relay") if a.relay or b.relay else None)


def _call(body, args, comm=None, *, name, grid, in_specs, out_specs, out_shape, scratch_shapes=()):
    params = _params(len(grid))
    if comm is None:
        return pl.pallas_call(body, name=name, grid=grid, in_specs=in_specs, out_specs=out_specs, out_shape=out_shape,
                              scratch_shapes=scratch_shapes, compiler_params=params)(*args), ()
    single = not isinstance(out_specs, (list, tuple))
    out_specs_l = [out_specs] if single else list(out_specs)
    out_shape_l = [out_shape] if single else list(out_shape)
    n_in, n_out, n_scr = len(in_specs), len(out_specs_l), len(scratch_shapes)
    n_cin, n_cout = len(comm.inputs), len(comm.out_shapes)

    def hosted(*refs):
        it = iter(refs)
        ins, cin, outs, cout, scr = ([next(it) for _ in range(k)] for k in (n_in, n_cin, n_out, n_cout, n_scr))
        sems = list(it)
        ids = [pl.program_id(k) for k in range(len(grid))]
        first = functools.reduce(jnp.logical_and, [i == 0 for i in ids])
        last = functools.reduce(jnp.logical_and, [i == g - 1 for i, g in zip(ids, grid)])

        @pl.when(first)
        def _():
            comm.start(cin, cout, sems)

        if comm.relay is not None:
            at = [int(grid[0] * RELAY_AT)] + [0] * (len(grid) - 1)

            @pl.when(functools.reduce(jnp.logical_and, [i == v for i, v in zip(ids, at)]))
            def _():
                comm.relay(cin, cout, sems)

        body(*ins, *outs, *scr)

        @pl.when(last)
        def _():
            comm.finish(cin, cout, sems)

    res = pl.pallas_call(
        hosted, name=name, grid=grid, in_specs=list(in_specs) + [ANY] * n_cin,
        out_specs=out_specs_l + [ANY] * n_cout, out_shape=out_shape_l + list(comm.out_shapes),
        scratch_shapes=list(scratch_shapes) + list(comm.scratch), compiler_params=params)(*args, *comm.inputs)
    return (res[0] if single else res[:n_out]), res[n_out:]


def _comm_alone(comm, name):
    n_cin, n_cout = len(comm.inputs), len(comm.out_shapes)

    def body(*refs):
        cin, cout, sems = refs[:n_cin], refs[n_cin:n_cin + n_cout], refs[n_cin + n_cout:]
        comm.start(cin, cout, sems)
        if comm.relay is not None:
            comm.relay(cin, cout, sems)
        comm.finish(cin, cout, sems)

    return pl.pallas_call(body, name=name, in_specs=[ANY] * n_cin, out_specs=[ANY] * n_cout,
                          out_shape=list(comm.out_shapes), scratch_shapes=list(comm.scratch))(*comm.inputs)


def _slabs(v, fn):
    return jnp.concatenate([fn(v[:, LANE * j:LANE * (j + 1)]) for j in range(v.shape[1] // LANE)], axis=1)


def _rope_att(v, ca, sa, sb):
    return _slabs(v, lambda t: t * ca + pltpu.roll(t, LANE - 8, 1) * sa + pltpu.roll(t, 8, 1) * sb)


def _rope_att_t(v, ca, sa, sb):
    return _slabs(v, lambda t: t * ca + pltpu.roll(t * sa, 8, 1) + pltpu.roll(t * sb, LANE - 8, 1))


def _rope_att_rows(v, ct, st, sign):
    parts = []
    for h in range(v.shape[0] // ATT_HEAD_DIM):
        r0 = ATT_HEAD_DIM * h
        x1, x2 = v[r0:r0 + 8], v[r0 + 8:r0 + 16]
        parts += [x1 * ct - sign * (x2 * st), x2 * ct + sign * (x1 * st), v[r0 + 16:r0 + ATT_HEAD_DIM]]
    return jnp.concatenate(parts, axis=0)


def _rope_ret(v, cr, sr):
    return _slabs(v, lambda t: t * cr + pltpu.roll(t, 64, 1) * sr)


def _rope_ret_t(v, cr, sr):
    return _slabs(v, lambda t: t * cr + pltpu.roll(t * sr, 64, 1))


def _rope_lane_tables():
    def inv(dim, theta):
        return theta ** (-jnp.arange(0, dim, 2, dtype=F32) / dim)

    inv_a, inv_r = inv(ROPE_DIM, ROPE_THETA), inv(RET_KEY_DIM, RET_ROT_BASE)
    half = ROPE_DIM // 2
    zeros = jnp.zeros((ATT_HEAD_DIM - ROPE_DIM,), F32)
    freq64 = jnp.concatenate([inv_a, inv_a, zeros])
    lo64 = jnp.concatenate([-jnp.ones((half,), F32), jnp.zeros((half,), F32), zeros])
    hi64 = jnp.concatenate([jnp.zeros((half,), F32), jnp.ones((half,), F32), zeros])
    sign_r = jnp.concatenate([-jnp.ones((64,), F32), jnp.ones((64,), F32)])
    rows = [jnp.tile(freq64, 2), jnp.tile(lo64, 2), jnp.tile(hi64, 2), jnp.tile(inv_r, 2), sign_r]
    lanes = jnp.stack(rows + [jnp.zeros((LANE,), F32)] * (8 - len(rows)))
    return lanes, jnp.broadcast_to(inv_a[:, None], (half, LANE))


def _prologue(x, g1, comm):
    seq = x.shape[0]
    tm = min(TOKEN_TILE, seq)

    def body(x_ref, g_ref, lanes_ref, freq_rows_ref, xn_ref, ca_ref, sa_ref, sb_ref, cr_ref, sr_ref, ct_ref, st_ref):
        xf = x_ref[...]
        r = lax.rsqrt(jnp.mean(xf * xf, axis=-1, keepdims=True) + NORM_EPS)
        xn_ref[...] = (xf * r * g_ref[...]).astype(BF)
        row0 = pl.program_id(0) * tm
        pos = (row0 + lax.broadcasted_iota(jnp.int32, (tm, LANE), 0)).astype(F32)
        ang_a = pos * lanes_ref[0:1, :]
        sin_a = jnp.sin(ang_a)
        ca_ref[...] = jnp.cos(ang_a)
        sa_ref[...] = sin_a * lanes_ref[1:2, :]
        sb_ref[...] = sin_a * lanes_ref[2:3, :]
        ang_r = pos * lanes_ref[3:4, :]
        cr_ref[...] = jnp.cos(ang_r)
        sr_ref[...] = jnp.sin(ang_r) * lanes_ref[4:5, :]
        pos_t = (row0 + lax.broadcasted_iota(jnp.int32, (8, tm), 1)).astype(F32)
        ang_t = pos_t * jnp.concatenate([freq_rows_ref[...]] * (tm // LANE), axis=1)
        ct_ref[...] = jnp.cos(ang_t)
        st_ref[...] = jnp.sin(ang_t)

    tab = _rows(tm, LANE)
    tab_t = _cols(8, tm)
    return _call(
        body, (x, g1, *_rope_lane_tables()), comm, name="prologue", grid=(seq // tm,),
        in_specs=[_rows(tm, D_MODEL), VMEM_FULL, VMEM_FULL, VMEM_FULL],
        out_specs=[_rows(tm, D_MODEL), tab, tab, tab, tab, tab, tab_t, tab_t],
        out_shape=[jax.ShapeDtypeStruct((seq, D_MODEL), BF)] + [jax.ShapeDtypeStruct((seq, LANE), F32)] * 5
        + [jax.ShapeDtypeStruct((8, seq), F32)] * 2)


def _ret_tables():
    c = RET_CHUNK
    lg = jnp.asarray(RET_LOG_GAMMA, F32)
    idx = jnp.arange(c, dtype=F32)
    diff = idx[:, None] - idx[None, :]
    intra = jnp.where(diff >= 0, jnp.exp(jnp.maximum(diff, 0.0) * lg[:, None, None]), 0.0)
    qd = jnp.exp((idx + 1.0)[None, :] * lg[:, None])[..., None]
    kd = jnp.exp((c - 1.0 - idx)[None, :] * lg[:, None])[..., None]
    return intra, jnp.broadcast_to(qd, (RET_HEADS, c, RET_KEY_DIM)), jnp.broadcast_to(kd, (RET_HEADS, c, RET_KEY_DIM))


def _inproj_fwd(xn1, w_in, tabs, comm=None):
    seq = xn1.shape[0]
    tm = min(TOKEN_TILE, seq)

    def body(xn_ref, w_hbm, ca_ref, sa_ref, sb_ref, cr_ref, sr_ref, ct_ref, st_ref,
             qt_ref, kv_ref, kvt_ref, qkr_ref, vr_ref, gr_ref, gates_ref, w_vmem, w_sems):
        w = _Resident(w_hbm, w_vmem, w_sems, IN_CHUNKS_FWD)
        _Resident.load(w)
        ca, sa, sb, cr, sr, ct, st = (ref[...] for ref in (ca_ref, sa_ref, sb_ref, cr_ref, sr_ref, ct_ref, st_ref))
        xn = xn_ref[...]
        qt_ref[...] = _rope_att_rows(_nt(w.chunk(0)[...], xn), ct, st, 1.0).astype(BF)
        w_kv = w.chunk(1)
        kvt = _nt(w_kv[...], xn)
        kvt_ref[:LANE, :] = _rope_att_rows(kvt[:LANE], ct, st, 1.0).astype(BF)
        kvt_ref[LANE:, :] = kvt[LANE:].astype(BF)
        kvn = _nt(xn, w_kv[...])
        kv_ref[:, :LANE] = _rope_att(kvn[:, :LANE], ca, sa, sb).astype(BF)
        kv_ref[:, LANE:] = kvn[:, LANE:].astype(BF)
        qr = _rope_ret(_nt(xn, w.chunk(2)[...]), cr, sr)
        qkr_ref[:, :512] = qr.astype(BF)
        kr = _rope_ret(_nt(xn, w.chunk(3)[...]), cr, sr) * RET_SCALE
        qkr_ref[:, 512:] = kr.astype(BF)
        vr_ref[...] = _nt(xn, w.chunk(4)[...]).astype(BF)
        gr_ref[...] = _nt(xn, w.chunk(5)[...]).astype(BF)
        gates_ref[...] = _nt(xn, w.chunk(6)[...]).astype(BF)

    tab = _rows(tm, LANE)
    tab_t = _cols(8, tm)
    return _call(
        body, (xn1, w_in, *tabs), comm, name="inproj_fwd", grid=(seq // tm,),
        in_specs=[_rows(tm, D_MODEL), ANY, tab, tab, tab, tab, tab, tab_t, tab_t],
        out_specs=[_cols(512, tm), _rows(tm, 256), _cols(256, tm), _rows(tm, 1024),
                   _rows(tm, 1024), _rows(tm, 1024), _rows(tm, 2048)],
        out_shape=[jax.ShapeDtypeStruct(s, BF) for s in ((512, seq), (seq, 256), (256, seq),
                                                         (seq, 1024), (seq, 1024), (seq, 1024), (seq, 2048))],
        scratch_shapes=_Resident.scratch(w_in, IN_CHUNKS_FWD))


ATT_Q_PER_KV = ATT_HEADS // 2


def _att_group(ref, kv):
    d = ATT_HEAD_DIM
    g = jnp.concatenate([ref[d * (ATT_Q_PER_KV * kv + j):d * (ATT_Q_PER_KV * kv + j + 1), :]
                         for j in range(ATT_Q_PER_KV)], axis=1)
    z = jnp.zeros_like(g)
    return jnp.concatenate([g, z] if kv == 0 else [z, g], axis=0)


def _att_probs(kwin, qpad, sink_ref, kv, i):
    c = ATT_BLOCK
    kj = lax.broadcasted_iota(jnp.int32, (2 * c, c), 0)
    qi = lax.broadcasted_iota(jnp.int32, (2 * c, c), 1)
    allowed = (kj > qi) & (kj <= qi + c) & ((kj >= c) | (i > 0))
    allowed = jnp.concatenate([allowed] * ATT_Q_PER_KV, axis=1)
    s = jnp.where(allowed, _nn(kwin, qpad) * ATT_SCALE, NEG_INF)
    sk = jnp.concatenate([jnp.full((1, c), sink_ref[ATT_Q_PER_KV * kv + j], F32) for j in range(ATT_Q_PER_KV)], axis=1)
    m = jnp.maximum(jnp.max(s, axis=0, keepdims=True), sk)
    pe = jnp.exp(s - m)
    psink = jnp.exp(sk - m)
    inv = 1.0 / (jnp.sum(pe, axis=0, keepdims=True) + psink)
    return pe * inv, psink * inv


def _seq_fwd(qt, kv, kvt, sinks, qkr, vr, gr, gain, rtabs, comm=None):
    seq = kv.shape[0]
    c = ATT_BLOCK
    assert c == RET_CHUNK
    d = ATT_HEAD_DIM
    nb = seq // c
    dk, dv = RET_KEY_DIM, RET_VAL_DIM

    def body(sink_ref, qt_ref, kvc_ref, kvp_ref, kvtc_ref, kvtp_ref, qk_ref, v_ref, g_ref, gain_ref, intra_ref,
             qd_ref, kd_ref, ot_ref, ret_ref, yrp_ref, st_ref, state):
        i = pl.program_id(0)
        kwin = jnp.concatenate([kvp_ref[:, :LANE], kvc_ref[:, :LANE]], axis=0)
        vtwin = jnp.concatenate([kvtp_ref[LANE:, :], kvtc_ref[LANE:, :]], axis=1)
        for g in (0, 1):
            prob, _ = _att_probs(kwin, _att_group(qt_ref, g), sink_ref, g, i)
            out = _nn(vtwin[d * g:d * (g + 1), :], prob.astype(BF))
            for j in range(ATT_Q_PER_KV):
                h = ATT_Q_PER_KV * g + j
                ot_ref[d * h:d * (h + 1), :] = out[:, c * j:c * (j + 1)].astype(BF)

        @pl.when(i == 0)
        def _():
            state[...] = jnp.zeros_like(state)

        for h in range(RET_HEADS):
            qh = qk_ref[:, dk * h:dk * (h + 1)]
            kh = qk_ref[:, 512 + dk * h:512 + dk * (h + 1)]
            vh = v_ref[:, dv * h:dv * (h + 1)]
            sh = state[h]
            shb = sh.astype(BF)
            st_ref[0, h] = shb
            att = _nt(qh, kh) * intra_ref[h]
            inner = _nn(att.astype(BF), vh)
            cross = _nn((qh.astype(F32) * qd_ref[h]).astype(BF), shb)
            out = inner + cross
            state[h] = sh * RET_CHUNK_DECAY[h] + _tn((kh.astype(F32) * kd_ref[h]).astype(BF), vh)
            ret_ref[:, dv * h:dv * (h + 1)] = out
            mu = jnp.mean(out, axis=-1, keepdims=True)
            dev = out - mu
            var = jnp.mean(dev * dev, axis=-1, keepdims=True)
            y = dev * lax.rsqrt(var + GN_EPS) * gain_ref[:, dv * h:dv * (h + 1)]
            g = g_ref[:, dv * h:dv * (h + 1)].astype(F32)
            yrp_ref[:, dv * h:dv * (h + 1)] = (g * _sigmoid(g) * y).astype(BF)

    prev = lambda i: jnp.maximum(i - 1, 0)
    return _call(
        body, (sinks, qt, kv, kv, kvt, kvt, qkr, vr, gr, gain, *rtabs), comm, name="seq_fwd", grid=(nb,),
        in_specs=[SMEM_FULL, _cols(512, c), _rows(c, 256), pl.BlockSpec((c, 256), lambda i: (prev(i), 0)),
                  _cols(256, c), pl.BlockSpec((256, c), lambda i: (0, prev(i))),
                  _rows(c, 1024), _rows(c, 1024), _rows(c, 1024), VMEM_FULL, VMEM_FULL, VMEM_FULL, VMEM_FULL],
        out_specs=[_cols(512, c), _rows(c, 1024), _rows(c, 1024),
                   pl.BlockSpec((1, RET_HEADS, dk, dv), lambda i: (i, 0, 0, 0))],
        out_shape=[jax.ShapeDtypeStruct((512, seq), BF), jax.ShapeDtypeStruct((seq, 1024), F32),
                   jax.ShapeDtypeStruct((seq, 1024), BF), jax.ShapeDtypeStruct((nb, RET_HEADS, dk, dv), BF)],
        scratch_shapes=[pltpu.VMEM((RET_HEADS, dk, dv), F32)])


def _mix_fwd(oat, yrp, gates, bg, x, w_att, w_ret, w_out):
    seq = x.shape[0]
    tm = min(TOKEN_TILE, seq)

    def body(oat_ref, yrp_ref, gates_ref, bg_ref, x_ref, wa_ref, wr_ref, wo_ref, ya_ref, yr_ref, mg_ref, h1_ref):
        ya = _tn(oat_ref[...], wa_ref[...])
        yr = _nn(yrp_ref[...], wr_ref[...])
        gt = _sigmoid(gates_ref[...].astype(F32) + bg_ref[...])
        merged = (gt[:, :D_MODEL] * ya + gt[:, D_MODEL:] * yr).astype(BF)
        ya_ref[...] = ya.astype(BF)
        yr_ref[...] = yr.astype(BF)
        mg_ref[...] = merged
        h1_ref[...] = x_ref[...] + _nn(merged, wo_ref[...])

    return pl.pallas_call(
        body, name="mix_fwd", grid=(seq // tm,),
        in_specs=[_cols(512, tm), _rows(tm, 1024), _rows(tm, 2048), VMEM_FULL, _rows(tm, D_MODEL),
                  VMEM_FULL, VMEM_FULL, VMEM_FULL],
        out_specs=[_rows(tm, D_MODEL)] * 4,
        out_shape=[jax.ShapeDtypeStruct((seq, D_MODEL), BF)] * 3 + [jax.ShapeDtypeStruct((seq, D_MODEL), F32)],
        compiler_params=_params(1),
    )(oat, yrp, gates, bg, x, w_att, w_ret, w_out)


def _mlp_loss_step(h1, g2, g3, target, w_ff1, w_ff2):
    seq = h1.shape[0]
    tm = min(MLP_TOKEN_TILE, seq)
    n_chunks = len(FF_CHUNKS) - 1

    def body(h1_ref, g2_ref, g3_ref, t_ref, w1_hbm, w2_hbm,
             xn2_ref, hdn_ref, dh2_ref, du_ref, dh1_ref, loss_ref, dg3_ref, dg2_ref,
             w1_vmem, w1_sems, w2_vmem, w2_sems, relu_u):
        w1 = _Resident(w1_hbm, w1_vmem, w1_sems, FF_CHUNKS)
        w2 = _Resident(w2_hbm, w2_vmem, w2_sems, FF_CHUNKS)
        _Resident.load(w1, w2)

        @pl.when(pl.program_id(0) == 0)
        def _():
            loss_ref[...] = jnp.zeros_like(loss_ref)
            dg3_ref[...] = jnp.zeros_like(dg3_ref)
            dg2_ref[...] = jnp.zeros_like(dg2_ref)

        h1v = h1_ref[...]
        r2 = lax.rsqrt(jnp.mean(h1v * h1v, axis=-1, keepdims=True) + NORM_EPS)
        xn2 = (h1v * r2 * g2_ref[...]).astype(BF)
        xn2_ref[...] = xn2
        h2 = h1v
        for c in range(n_chunks):
            cols = slice(FF_CHUNKS[c], FF_CHUNKS[c + 1])
            a = jnp.maximum(_nt(xn2, w1.chunk(c)[...]), 0.0)
            relu_u[:, cols] = a.astype(BF)
            hdn = jnp.square(a).astype(BF)
            hdn_ref[:, cols] = hdn
            h2 = h2 + _nn(hdn, w2.chunk(c)[...])
        r3 = lax.rsqrt(jnp.mean(h2 * h2, axis=-1, keepdims=True) + NORM_EPS)
        hn = h2 * r3
        err = hn * g3_ref[...] - t_ref[...]
        loss_ref[...] += jnp.sum(err * err) * (0.5 / D_MODEL)
        dy = err * (1.0 / D_MODEL)
        dg3_ref[...] += jnp.sum(dy * hn, axis=0, keepdims=True)
        z = dy * g3_ref[...]
        dh2 = r3 * (z - hn * jnp.mean(z * hn, axis=-1, keepdims=True))
        dh2b = dh2.astype(BF)
        dh2_ref[...] = dh2b
        dxn2 = jnp.zeros_like(dh2)
        for c in range(n_chunks):
            cols = slice(FF_CHUNKS[c], FF_CHUNKS[c + 1])
            du = (_nt(dh2b, w2.chunk(c)[...]) * (2.0 * relu_u[:, cols].astype(F32))).astype(BF)
            du_ref[:, cols] = du
            dxn2 = dxn2 + _nn(du, w1.chunk(c)[...])
        dnorm, dg = _rms_bwd(dxn2, h1v, g2_ref[...])
        dh1_ref[...] = dh2 + dnorm
        dg2_ref[...] += dg

    gain_acc = pl.BlockSpec((1, D_MODEL), lambda i: (0, 0))
    return pl.pallas_call(
        body, name="mlp_loss_step", grid=(seq // tm,),
        in_specs=[_rows(tm, D_MODEL), VMEM_FULL, VMEM_FULL, _rows(tm, D_MODEL), ANY, ANY],
        out_specs=[_rows(tm, D_MODEL), _rows(tm, D_FF), _rows(tm, D_MODEL), _rows(tm, D_FF), _rows(tm, D_MODEL),
                   pl.BlockSpec((1, LANE), lambda i: (0, 0)), gain_acc, gain_acc],
        out_shape=[jax.ShapeDtypeStruct((seq, D_MODEL), BF), jax.ShapeDtypeStruct((seq, D_FF), BF),
                   jax.ShapeDtypeStruct((seq, D_MODEL), BF), jax.ShapeDtypeStruct((seq, D_FF), BF),
                   jax.ShapeDtypeStruct((seq, D_MODEL), F32), jax.ShapeDtypeStruct((1, LANE), F32),
                   jax.ShapeDtypeStruct((1, D_MODEL), F32), jax.ShapeDtypeStruct((1, D_MODEL), F32)],
        scratch_shapes=(_Resident.scratch(w_ff1, FF_CHUNKS) + _Resident.scratch(w_ff2, FF_CHUNKS)
                        + [pltpu.VMEM((tm, D_FF), BF)]),
        compiler_params=_params(1),
    )(h1, g2, g3, target, w_ff1, w_ff2)


def _rms_bwd(dxn, xin, gain):
    r = lax.rsqrt(jnp.mean(xin * xin, axis=-1, keepdims=True) + NORM_EPS)
    xhat = xin * r
    z = dxn * gain
    dxin = r * (z - xhat * jnp.mean(z * xhat, axis=-1, keepdims=True))
    return dxin, jnp.sum(dxn * xhat, axis=0, keepdims=True)


def _mm_tn(a, b, name, a_is_t=False, comm=None):
    kdim, seq = a.shape if a_is_t else a.shape[::-1]
    ndim = b.shape[1]
    ts = min(TN_TOKEN_TILE, seq)
    tk = min(kdim, 1024)
    tn = min(ndim, 1024)
    n_steps = seq // ts

    def body(a_ref, b_ref, o_ref):
        @pl.when(pl.program_id(2) == 0)
        def _():
            o_ref[...] = jnp.zeros_like(o_ref)

        mm = _nn if a_is_t else _tn
        o_ref[...] += mm(a_ref[...].astype(BF), b_ref[...].astype(BF))

    a_spec = (pl.BlockSpec((tk, ts), lambda k, n, s: (k, s)) if a_is_t
              else pl.BlockSpec((ts, tk), lambda k, n, s: (s, k)))
    res = _call(
        body, (a, b), comm, name=name, grid=(kdim // tk, ndim // tn, n_steps),
        in_specs=[a_spec, pl.BlockSpec((ts, tn), lambda k, n, s: (s, n))],
        out_specs=pl.BlockSpec((tk, tn), lambda k, n, s: (k, n)),
        out_shape=jax.ShapeDtypeStruct((kdim, ndim), F32))
    return res[0] if comm is None else res


def _dw_in_t(pieces, is_t, xn1, comm=None):
    seq = xn1.shape[0]
    tr = 256
    n_blk = [(p.shape[0] if t else p.shape[1]) // tr for p, t in zip(pieces, is_t)]
    offs = [sum(n_blk[:i]) for i in range(len(pieces) + 1)]
    n_p = len(pieces)

    def body(*refs):
        piece_refs, xn_ref, o_ref, ob_ref = refs[:n_p], refs[n_p], refs[n_p + 1], refs[n_p + 2]
        r = pl.program_id(0)
        for p in range(n_p):
            @pl.when((r >= offs[p]) & (r < offs[p + 1]))
            def _(p=p):
                res = (_nn if is_t[p] else _tn)(piece_refs[p][...], xn_ref[...])
                o_ref[...] = res
                ob_ref[...] = res.astype(BF)

    def piece_spec(p):
        def index(r):
            blk = jnp.clip(r - offs[p], 0, n_blk[p] - 1)
            return (blk, 0) if is_t[p] else (0, blk)
        return pl.BlockSpec((tr, seq) if is_t[p] else (seq, tr), index)

    return _call(
        body, (*pieces, xn1), comm, name="dw_in", grid=(offs[-1],),
        in_specs=[piece_spec(p) for p in range(n_p)] + [VMEM_FULL],
        out_specs=[_rows(tr, D_MODEL), _rows(tr, D_MODEL)],
        out_shape=[jax.ShapeDtypeStruct((offs[-1] * tr, D_MODEL), F32),
                   jax.ShapeDtypeStruct((offs[-1] * tr, D_MODEL), BF)])


def _mix_bwd(dh1, ya, yr, gates, bg, merged, oat, yrp, w_att, w_ret, w_out, comm=None):
    seq = dh1.shape[0]
    tm = min(TOKEN_TILE, seq)

    def body(dh1_ref, ya_ref, yr_ref, gates_ref, bg_ref, mg_ref, oat_ref, yrp_ref, wa_ref, wr_ref, wo_ref,
             dgates_ref, doa_ref, dyrp_ref, db_ref, dwa_ref, dwr_ref, dwo_ref):
        @pl.when(pl.program_id(0) == 0)
        def _():
            for ref in (db_ref, dwa_ref, dwr_ref, dwo_ref):
                ref[...] = jnp.zeros_like(ref)

        dh1b = dh1_ref[...].astype(BF)
        dm = _nt(dh1b, wo_ref[...])
        gt = _sigmoid(gates_ref[...].astype(F32) + bg_ref[...])
        ga, gr = gt[:, :D_MODEL], gt[:, D_MODEL:]
        dya = (dm * ga).astype(BF)
        dyr = (dm * gr).astype(BF)
        dga = dm * ya_ref[...].astype(F32) * ga * (1.0 - ga)
        dgr = dm * yr_ref[...].astype(F32) * gr * (1.0 - gr)
        dgates_ref[:, :D_MODEL] = dga.astype(BF)
        dgates_ref[:, D_MODEL:] = dgr.astype(BF)
        db_ref[:, :D_MODEL] += jnp.sum(dga, axis=0, keepdims=True)
        db_ref[:, D_MODEL:] += jnp.sum(dgr, axis=0, keepdims=True)
        doa_ref[...] = _nt(wa_ref[...], dya).astype(BF)
        dyrp_ref[...] = _nt(dyr, wr_ref[...]).astype(BF)
        dwo_ref[...] += _tn(mg_ref[...], dh1b)
        dwr_ref[...] += _tn(yrp_ref[...], dyr)
        dwa_ref[...] += _nn(oat_ref[...], dya)

    acc = lambda r, c: pl.BlockSpec((r, c), lambda i: (0, 0))
    return _call(
        body, (dh1, ya, yr, gates, bg, merged, oat, yrp, w_att, w_ret, w_out), comm, name="mix_bwd",
        grid=(seq // tm,),
        in_specs=[_rows(tm, D_MODEL), _rows(tm, D_MODEL), _rows(tm, D_MODEL), _rows(tm, 2048), VMEM_FULL,
                  _rows(tm, D_MODEL), _cols(512, tm), _rows(tm, 1024), VMEM_FULL, VMEM_FULL, VMEM_FULL],
        out_specs=[_rows(tm, 2048), _cols(512, tm), _rows(tm, 1024), acc(1, 2048), acc(512, D_MODEL),
                   acc(1024, D_MODEL), acc(D_MODEL, D_MODEL)],
        out_shape=[jax.ShapeDtypeStruct((seq, 2048), BF), jax.ShapeDtypeStruct((512, seq), BF),
                   jax.ShapeDtypeStruct((seq, 1024), BF), jax.ShapeDtypeStruct((1, 2048), F32),
                   jax.ShapeDtypeStruct((512, D_MODEL), F32), jax.ShapeDtypeStruct((1024, D_MODEL), F32),
                   jax.ShapeDtypeStruct((D_MODEL, D_MODEL), F32)])


def _seq_bwd(qt, kv, kvt, dot, sinks, qkr, vr, gr, ret, dyrp, states, gain, rtabs, tabs, comm=None):
    seq = kv.shape[0]
    c = ATT_BLOCK
    assert c == RET_CHUNK
    d = ATT_HEAD_DIM
    nb = seq // c
    dk, dv = RET_KEY_DIM, RET_VAL_DIM
    ca, sa, sb, cr, sr, ct, st = tabs

    def body(sink_ref, qt_ref, kvc_ref, kvp_ref, kvtc_ref, kvtp_ref, dot_ref, ca_ref, sa_ref, sb_ref, ct_ref, st_ref,
             qk_ref, v_ref, g_ref, ret_ref, dyp_ref, st8_ref, gain_ref, intra_ref, qd_ref, kd_ref, cr_ref, sr_ref,
             dqt_ref, dkv_ref, dsink_ref, dall_ref, dgain_ref, carry, dstate):
        step = pl.program_id(0)
        i = nb - 1 - step

        @pl.when(step == 0)
        def _():
            carry[...] = jnp.zeros_like(carry)
            dsink_ref[...] = jnp.zeros_like(dsink_ref)
            dstate[...] = jnp.zeros_like(dstate)
            dgain_ref[...] = jnp.zeros_like(dgain_ref)

        kwin = jnp.concatenate([kvp_ref[:, :LANE], kvc_ref[:, :LANE]], axis=0)
        vwin = jnp.concatenate([kvp_ref[:, LANE:], kvc_ref[:, LANE:]], axis=0)
        ktwin = jnp.concatenate([kvtp_ref[:LANE, :], kvtc_ref[:LANE, :]], axis=1)
        ctv, stv = ct_ref[...], st_ref[...]
        lane1 = lax.broadcasted_iota(jnp.int32, (1, LANE), 1)
        dkw = jnp.zeros((2 * c, LANE), F32)
        dvw = jnp.zeros((2 * c, LANE), F32)
        dsink = jnp.zeros((1, LANE), F32)
        for g in (0, 1):
            qpad, dopad = _att_group(qt_ref, g), _att_group(dot_ref, g)
            prob, psink = _att_probs(kwin, qpad, sink_ref, g, i)
            dprob = _nn(vwin, dopad)
            drow = jnp.sum(dprob * prob, axis=0, keepdims=True)
            ds = (prob * (dprob - drow) * ATT_SCALE).astype(BF)
            dqg = _nn(ktwin[d * g:d * (g + 1), :], ds)
            dkw = dkw + _nt(ds, qpad)
            dvw = dvw + _nt(prob.astype(BF), dopad)
            dsink_lanes = psink * drow
            for j in range(ATT_Q_PER_KV):
                h = ATT_Q_PER_KV * g + j
                dqt_ref[d * h:d * (h + 1), :] = _rope_att_rows(dqg[:, c * j:c * (j + 1)], ctv, stv, -1.0).astype(BF)
                dsink = dsink + jnp.where(lane1 == h, -jnp.sum(dsink_lanes[:, c * j:c * (j + 1)]), 0.0)
        dsink_ref[...] += dsink
        total = carry[...] + jnp.concatenate([dkw[c:], dvw[c:]], axis=1)
        dkv_ref[:, :LANE] = _rope_att_t(total[:, :LANE], ca_ref[...], sa_ref[...], sb_ref[...]).astype(BF)
        dkv_ref[:, LANE:] = total[:, LANE:].astype(BF)
        carry[...] = jnp.concatenate([dkw[:c], dvw[:c]], axis=1)

        crv, srv = cr_ref[...], sr_ref[...]
        for h in range(RET_HEADS):
            vs = slice(dv * h, dv * (h + 1))
            qh = qk_ref[:, dk * h:dk * (h + 1)]
            kh = qk_ref[:, 512 + dk * h:512 + dk * (h + 1)]
            vh = v_ref[:, vs]
            out = ret_ref[:, vs]
            g = g_ref[:, vs].astype(F32)
            dyp = dyp_ref[:, vs].astype(F32)
            gain_h = gain_ref[:, vs]
            mu = jnp.mean(out, axis=-1, keepdims=True)
            dev = out - mu
            rstd = lax.rsqrt(jnp.mean(dev * dev, axis=-1, keepdims=True) + GN_EPS)
            yn = dev * rstd
            sg = _sigmoid(g)
            dg = dyp * (yn * gain_h) * (sg * (1.0 + g * (1.0 - sg)))
            dy = dyp * (g * sg)
            dgain_ref[:, vs] += jnp.sum(dy * yn, axis=0, keepdims=True)
            dyn = dy * gain_h
            dout = rstd * (dyn - jnp.mean(dyn, axis=-1, keepdims=True)
                           - yn * jnp.mean(dyn * yn, axis=-1, keepdims=True))
            doutb = dout.astype(BF)
            sc = st8_ref[0, h]
            dsp = dstate[h]
            dspb = dsp.astype(BF)
            intra, qdv, kdv = intra_ref[h], qd_ref[h], kd_ref[h]
            att = _nt(qh, kh) * intra
            dab = (_nt(doutb, vh) * intra).astype(BF)
            qdec = (qh.astype(F32) * qdv).astype(BF)
            kdec = (kh.astype(F32) * kdv).astype(BF)
            dq = _nn(dab, kh) + _nt(doutb, sc) * qdv
            dkk = _tn(dab, qh) + _nt(vh, dspb) * kdv
            dvv = _tn(att.astype(BF), doutb) + _nn(kdec, dspb)
            dstate[h] = dsp * RET_CHUNK_DECAY[h] + _tn(qdec, doutb)
            dall_ref[:, dk * h:dk * (h + 1)] = _rope_ret_t(dq, crv, srv).astype(BF)
            dall_ref[:, 512 + dk * h:512 + dk * (h + 1)] = (_rope_ret_t(dkk, crv, srv) * RET_SCALE).astype(BF)
            dall_ref[:, 1024 + dv * h:1024 + dv * (h + 1)] = dvv.astype(BF)
            dall_ref[:, 2048 + dv * h:2048 + dv * (h + 1)] = dg.astype(BF)

    cur = lambda s: nb - 1 - s
    prev = lambda s: jnp.maximum(nb - 2 - s, 0)
    rows = lambda w: pl.BlockSpec((c, w), lambda s: (cur(s), 0))
    cols = lambda h: pl.BlockSpec((h, c), lambda s: (0, cur(s)))
    acc = lambda w: pl.BlockSpec((1, w), lambda s: (0, 0))
    return _call(
        body, (sinks, qt, kv, kv, kvt, kvt, dot, ca, sa, sb, ct, st, qkr, vr, gr, ret, dyrp, states, gain, *rtabs,
               cr, sr), comm, name="seq_bwd", grid=(nb,),
        in_specs=[SMEM_FULL, cols(512), rows(256), pl.BlockSpec((c, 256), lambda s: (prev(s), 0)), cols(256),
                  pl.BlockSpec((256, c), lambda s: (0, prev(s))), cols(512), rows(LANE), rows(LANE), rows(LANE),
                  cols(8), cols(8), rows(1024), rows(1024), rows(1024), rows(1024), rows(1024),
                  pl.BlockSpec((1, RET_HEADS, dk, dv), lambda s: (cur(s), 0, 0, 0)),
                  VMEM_FULL, VMEM_FULL, VMEM_FULL, VMEM_FULL, rows(LANE), rows(LANE)],
        out_specs=[cols(512), rows(256), acc(LANE), rows(3072), acc(1024)],
        out_shape=[jax.ShapeDtypeStruct((512, seq), BF), jax.ShapeDtypeStruct((seq, 256), BF),
                   jax.ShapeDtypeStruct((1, LANE), F32), jax.ShapeDtypeStruct((seq, 3072), BF),
                   jax.ShapeDtypeStruct((1, 1024), F32)],
        scratch_shapes=[pltpu.VMEM((c, 256), F32), pltpu.VMEM((RET_HEADS, dk, dv), F32)])


def _inproj_bwd(dqa, dkva, dret, dgates, dh1, x, g1, w_in, comm=None):
    seq = x.shape[0]
    tm = min(TOKEN_TILE, seq)

    def body(dqa_ref, dkva_ref, dret_ref, dgates_ref, dh1_ref, x_ref, g_ref, w_hbm, dx_ref, dg1_ref, w_vmem, w_sems):
        w = _Resident(w_hbm, w_vmem, w_sems, IN_CHUNKS_BWD)
        _Resident.load(w)

        @pl.when(pl.program_id(0) == 0)
        def _():
            dg1_ref[...] = jnp.zeros_like(dg1_ref)

        dxn = _tn(dqa_ref[...], w.chunk(0)[...])
        dxn = dxn + _nn(dkva_ref[...], w.chunk(1)[...])
        dxn = dxn + _nn(dret_ref[...], w.chunk(2)[...])
        dxn = dxn + _nn(dgates_ref[...], w.chunk(3)[...])
        dnorm, dg = _rms_bwd(dxn, x_ref[...], g_ref[...])
        dx_ref[...] = dh1_ref[...] + dnorm
        dg1_ref[...] += dg

    return _call(
        body, (dqa, dkva, dret, dgates, dh1, x, g1, w_in), comm, name="inproj_bwd", grid=(seq // tm,),
        in_specs=[_cols(512, tm), _rows(tm, 256), _rows(tm, 3072), _rows(tm, 2048), _rows(tm, D_MODEL),
                  _rows(tm, D_MODEL), VMEM_FULL, ANY],
        out_specs=[_rows(tm, D_MODEL), pl.BlockSpec((1, D_MODEL), lambda i: (0, 0))],
        out_shape=[jax.ShapeDtypeStruct((seq, D_MODEL), F32), jax.ShapeDtypeStruct((1, D_MODEL), F32)],
        scratch_shapes=_Resident.scratch(w_in, IN_CHUNKS_BWD))


def _local_step(x, target, g1, bg, sinks, gain, g2, g3, shards):
    rtabs = _ret_tables()
    scatter = _Scatter

    comm, unpack = _gather(shards, ("w_in",))
    (xn1, ca, sa, sb, cr, sr, ct, st), got = _prologue(x, g1, comm)
    (w_in,) = unpack(got)
    comm_mid, unpack_mid = _gather(shards, ("w_att_up", "w_ret_up", "w_out"))
    comm_ff1, unpack_ff1 = _gather(shards, ("w_ff1",))
    (qt, kv, kvt, qkr, vr, gr, gates), got = _inproj_fwd(xn1, w_in, (ca, sa, sb, cr, sr, ct, st),
                                                         _join(comm_mid, comm_ff1))
    w_att, w_ret, w_out = unpack_mid(got[:3])
    (w_ff1,) = unpack_ff1(got[3:])
    comm, unpack = _gather(shards, ("w_ff2",))
    (oa, ret, yrp, states), got = _seq_fwd(qt, kv, kvt, sinks, qkr, vr, gr, gain, rtabs, comm)
    (w_ff2,) = unpack(got)
    ya, yr, merged, h1 = _mix_fwd(oa, yrp, gates, bg, x, w_att, w_ret, w_out)
    xn2, hdn, dh2, du, dh1, loss, dg3, dg2 = _mlp_loss_step(h1, g2, g3, target, w_ff1, w_ff2)

    ff2 = scatter(dict(w_ff2=_mm_tn(hdn, dh2, "dw_ff2")), "ff2")
    d_ff1, r1_ff2 = _mm_tn(du, xn2, "dw_ff1", comm=ff2.pair_comm())
    ff1 = scatter(dict(w_ff1=d_ff1), "ff1")
    (dgates, doa, dyrp, db, d_att, d_ret, d_out), r1_ff1 = _mix_bwd(dh1, ya, yr, gates, bg, merged, oa, yrp,
                                                                    w_att, w_ret, w_out, ff1.pair_comm())
    _Scatter.pairs_done((ff2, ff1), (r1_ff2, r1_ff1), "ff")
    mid = scatter(dict(w_att_up=d_att, w_ret_up=d_ret, w_out=d_out), "mid")
    (dqa, dkva, dsink, dret, dgain), got = _seq_bwd(
        qt, kv, kvt, doa, sinks, qkr, vr, gr, ret, dyrp, states, gain, rtabs, (ca, sa, sb, cr, sr, ct, st),
        _join(_join(ff2.chip_comm(), ff1.chip_comm()), mid.pair_comm()))
    ff2.chip_done(got[:1])
    ff1.chip_done(got[1:2])
    mid.pair_done(got[2:])
    (d_in, d_in_bf), r2 = _dw_in_t((dqa, dkva, dret, dgates), (True, False, False, False), xn1, mid.chip_comm())
    mid.chip_done(r2)
    win = scatter(dict(w_in=d_in), "in", sent=dict(w_in=d_in_bf))
    win.pair_done(_comm_alone(win.pair_comm(), "pair_exchange_in"))
    (dx, dg1), r2 = _inproj_bwd(dqa, dkva, dret, dgates, dh1, x, g1, w_in, win.chip_comm())
    win.chip_done(r2)
    small = dict(norm_mix_gain=dg1, b_gates=db, attn_sinks=dsink, ret_gn_gain=dgain, norm_mlp_gain=dg2,
                 norm_final_gain=dg3)
    return loss, dx, small, (win, mid, ff1, ff2)


def _coords():
    return lax.axis_index("x"), lax.axis_index("y"), lax.axis_index("c")


def _flip(v, bit):
    return 1 - v if bit else v


def _xor(a, b):
    return a + b - 2 * a * b


def _gather_relay_comm(shards):
    n = len(shards)

    def parts(ins, outs, sems):
        send_sems, recv_sems, local_sems = sems
        x, y, c = _coords()
        me, sib = (x, y, c), (x, y, 1 - c)

        def ring(core):
            return ((_xor(x, core), _xor(y, 1 - core), core), (_xor(x, 1 - core), _xor(y, core), core),
                    (1 - x, 1 - y, core))

        def slot(a, blk):
            return outs[a].at[4 * blk[0] + 2 * blk[1] + blk[2]]

        def copy(a, k, blk, to, src=None):
            return pltpu.make_async_remote_copy(
                src_ref=slot(a, blk) if src is None else src, dst_ref=slot(a, blk),
                send_sem=send_sems.at[a, k], recv_sem=recv_sems.at[a, k], device_id=to, device_id_type=MESH)

        mine = [pltpu.make_async_copy(ins[a], slot(a, me), local_sems.at[a]) for a in range(n)]
        return me, sib, ring, copy, mine

    def first_copies(ins, outs, sems):
        me, sib, ring, copy, mine = parts(ins, outs, sems)
        src, dst, _ = ring(me[2])
        return mine, [copy(a, k, me, to, src=ins[a]) for a in range(n) for k, to in ((1, src), (2, dst), (0, sib))]

    def start(ins, outs, sems):
        mine, first = first_copies(ins, outs, sems)
        for cp in mine + first:
            cp.start()

    def relay(ins, outs, sems):
        me, sib, ring, copy, _ = parts(ins, outs, sems)
        src, dst, _ = ring(me[2])
        for a in range(n):
            copy(a, 1, src, me).wait_recv()
            copy(a, 3, src, dst).start()
            copy(a, 4, src, sib).start()
        for a in range(n):
            copy(a, 2, dst, me).wait_recv()
            copy(a, 5, dst, sib).start()

    def finish(ins, outs, sems):
        me, sib, ring, copy, _ = parts(ins, outs, sems)
        mine, first = first_copies(ins, outs, sems)
        src, dst, diag = ring(me[2])
        passed = [copy(a, k, blk, to) for a in range(n) for k, blk, to in ((3, src, dst), (4, src, sib), (5, dst, sib))]
        for a in range(n):
            copy(a, 3, diag, me).wait_recv()
            fwd = copy(a, 6, diag, sib)
            fwd.start()
            passed.append(fwd)
        s_src, s_dst, s_diag = ring(sib[2])
        for a in range(n):
            for k, blk in ((0, sib), (4, s_src), (5, s_dst), (6, s_diag)):
                copy(a, k, blk, me).wait_recv()
        for cp in first + passed:
            cp.wait_send()
        for cp in mine:
            cp.wait()

    return _Comm(list(shards), [jax.ShapeDtypeStruct((N_DEV,) + s.shape, s.dtype) for s in shards],
                 [pltpu.SemaphoreType.DMA((n, 7)), pltpu.SemaphoreType.DMA((n, 7)), pltpu.SemaphoreType.DMA((n,))],
                 start, finish, relay)


COLUMN_SHARDED = ("w_in", "w_ff1")
COLUMN_RELAID = ("w_att_up",)


def _gather(shards, names):
    def unpack(got):
        return [jnp.transpose(g, (1, 0, 2)).reshape(g.shape[1], N_DEV * g.shape[2]) if k in COLUMN_RELAID
                else g.reshape(N_DEV * g.shape[1], g.shape[2]) for k, g in zip(names, got)]

    return _gather_relay_comm([shards[k] for k in names]), unpack


def _pair_comm(grads):
    n = len(grads)

    def copies(g, r1, sems):
        send_sems, recv_sems = sems
        x, y, c = _coords()
        return [pltpu.make_async_remote_copy(
            src_ref=g[a].at[2 * j + (1 - c)], dst_ref=r1[a].at[j], send_sem=send_sems.at[a, j],
            recv_sem=recv_sems.at[a, j], device_id=(x, y, 1 - c), device_id_type=MESH)
            for a in range(n) for j in range(4)]

    def start(g, r1, sems):
        for cp in copies(g, r1, sems):
            cp.start()

    def finish(g, r1, sems):
        for cp in copies(g, r1, sems):
            cp.wait_recv()
        for cp in copies(g, r1, sems):
            cp.wait_send()

    return _Comm(list(grads), [jax.ShapeDtypeStruct((4,) + g.shape[1:], g.dtype) for g in grads],
                 [pltpu.SemaphoreType.DMA((n, 4)), pltpu.SemaphoreType.DMA((n, 4))], start, finish)


def _small_comm(small):
    def parts(ins, outs, sems):
        (small_ref,), (small_all,) = ins, outs
        ssend, srecv, lsem = sems
        x, y, c = _coords()
        me_idx = 4 * x + 2 * y + c
        own = pltpu.make_async_copy(small_ref, small_all.at[me_idx], lsem)
        sends, recvs = [], []
        for r in range(1, N_DEV):
            px, py, pc = _flip(x, r & 4), _flip(y, r & 2), _flip(c, r & 1)
            sends.append(pltpu.make_async_remote_copy(
                src_ref=small_ref, dst_ref=small_all.at[me_idx], send_sem=ssend.at[r - 1], recv_sem=srecv.at[r - 1],
                device_id=(px, py, pc), device_id_type=MESH))
            recvs.append(pltpu.make_async_remote_copy(
                src_ref=small_ref, dst_ref=small_all.at[4 * px + 2 * py + pc], send_sem=ssend.at[r - 1],
                recv_sem=srecv.at[r - 1], device_id=(px, py, pc), device_id_type=MESH))
        return own, sends, recvs

    def start(ins, outs, sems):
        own, sends, _ = parts(ins, outs, sems)
        own.start()
        for cp in sends:
            cp.start()

    def finish(ins, outs, sems):
        own, sends, recvs = parts(ins, outs, sems)
        for cp in recvs:
            cp.wait_recv()
        for cp in sends:
            cp.wait_send()
        own.wait()

    return _Comm([small], [jax.ShapeDtypeStruct((N_DEV,) + small.shape, small.dtype)],
                 [pltpu.SemaphoreType.DMA((N_DEV - 1,)), pltpu.SemaphoreType.DMA((N_DEV - 1,)),
                  pltpu.SemaphoreType.DMA], start, finish)


def _pair_sum(grads, r1s, c_arr, tag):
    n = len(grads)
    q = 1

    def body(c_ref, *refs):
        g, r, t = refs[:n], refs[n:2 * n], refs[2 * n:]
        for a in range(n):
            t[a][...] = (g[a][...] + r[a][...]).astype(t[a].dtype)

    def blk(arr):
        return (1, arr.shape[1] // q, arr.shape[2])

    grid_spec = pltpu.PrefetchScalarGridSpec(
        num_scalar_prefetch=1, grid=(4, q),
        in_specs=[pl.BlockSpec(blk(g), lambda j, s, c_ref: (2 * j + c_ref[0], s, 0)) for g in grads]
        + [pl.BlockSpec(blk(r), lambda j, s, c_ref: (j, s, 0)) for r in r1s],
        out_specs=[pl.BlockSpec(blk(r), lambda j, s, c_ref: (j, s, 0)) for r in r1s])
    return pl.pallas_call(
        body, name="pair_sum_" + tag, grid_spec=grid_spec,
        out_shape=[jax.ShapeDtypeStruct(r.shape, RS_PAYLOAD) for r in r1s],
        compiler_params=_params(2),
    )(c_arr, *grads, *r1s)


def _chip_comm(ts):
    n = len(ts)

    def copies(t, r2, sems):
        send_sems, recv_sems = sems
        x, y, c = _coords()
        out = []
        for a in range(n):
            for r in range(1, 4):
                tx, ty = _flip(x, r & 2), _flip(y, r & 1)
                out.append(pltpu.make_async_remote_copy(
                    src_ref=t[a].at[2 * tx + ty], dst_ref=r2[a].at[r - 1], send_sem=send_sems.at[a, r - 1],
                    recv_sem=recv_sems.at[a, r - 1], device_id=(tx, ty, c), device_id_type=MESH))
        return out

    def start(t, r2, sems):
        for cp in copies(t, r2, sems):
            cp.start()

    def finish(t, r2, sems):
        for cp in copies(t, r2, sems):
            cp.wait_recv()
        for cp in copies(t, r2, sems):
            cp.wait_send()

    return _Comm(list(ts), [jax.ShapeDtypeStruct((3,) + t.shape[1:], t.dtype) for t in ts],
                 [pltpu.SemaphoreType.DMA((n, 3)), pltpu.SemaphoreType.DMA((n, 3))], start, finish)


class _Scatter:
    def __init__(self, grads, tag, sent=None):
        def stack(k, g):
            if k in COLUMN_RELAID:
                return jnp.transpose(g.reshape(g.shape[0], N_DEV, g.shape[1] // N_DEV), (1, 0, 2))
            return g.reshape(N_DEV, g.shape[0] // N_DEV, g.shape[1])

        self.names, self.tag = tuple(grads), tag
        self.stacks = [stack(k, g) for k, g in grads.items()]
        self.sent = self.stacks if sent is None else [stack(k, sent[k]) for k in grads]

    def pair_comm(self):
        return _pair_comm(self.sent)

    def pair_done(self, r1s):
        _Scatter.pairs_done((self,), (r1s,), self.tag)

    @staticmethod
    def pairs_done(groups, r1s_of, tag):
        for grp, r1s in zip(groups, r1s_of):
            grp.r1s = list(r1s)
        c_arr = jnp.reshape(lax.axis_index("c"), (1,)).astype(jnp.int32)
        ts = _pair_sum([s for grp in groups for s in grp.stacks], [r for grp in groups for r in grp.r1s], c_arr, tag)
        for grp in groups:
            grp.ts, ts = ts[:len(grp.stacks)], ts[len(grp.stacks):]

    def chip_comm(self):
        return _chip_comm(self.ts)

    def chip_done(self, r2s):
        self.r2s = list(r2s)


def _adamw(w, g, m, v):
    m = ADAM_B1 * m + (1.0 - ADAM_B1) * g
    v = ADAM_B2 * v + (1.0 - ADAM_B2) * jnp.square(g)
    m_hat = m / (1.0 - ADAM_B1 ** ADAM_STEP)
    v_hat = v / (1.0 - ADAM_B2 ** ADAM_STEP)
    delta = -ADAM_LR * (m_hat / (jnp.sqrt(v_hat) + ADAM_EPS) + ADAM_WD * w)
    return delta, m, v


ADAM_STEPS = 4


def _piece_specs(stacks):
    def rows(s):
        return s.shape[1] // ADAM_STEPS
    return ([pl.BlockSpec((1, rows(s), s.shape[2]), lambda i, idx_ref: (idx_ref[0], i, 0)) for s in stacks]
            + [pl.BlockSpec((1, rows(s), s.shape[2]), lambda i, idx_ref: (idx_ref[1], i, 0)) for s in stacks]
            + [pl.BlockSpec((3, rows(s), s.shape[2]), lambda i, idx_ref: (0, i, 0)) for s in stacks])


def _piece_sum(g0, r1, r2):
    return (((g0[0] + r1[0]) + r2[0].astype(F32)) + r2[1].astype(F32)) + r2[2].astype(F32)


def _shard_sum(stacks, r1s, r2s, idx_arr):
    n = len(stacks)

    def body(idx_ref, *refs):
        g0, r1, r2, outs = (refs[k * n:(k + 1) * n] for k in range(4))
        for a in range(n):
            outs[a][...] = _piece_sum(g0[a], r1[a], r2[a])

    out_specs = [pl.BlockSpec((s.shape[1] // ADAM_STEPS, s.shape[2]), lambda i, idx_ref: (i, 0)) for s in stacks]
    grid_spec = pltpu.PrefetchScalarGridSpec(num_scalar_prefetch=1, grid=(ADAM_STEPS,),
                                             in_specs=_piece_specs(stacks), out_specs=out_specs)
    return pl.pallas_call(
        body, name="shard_sum", grid_spec=grid_spec,
        out_shape=[jax.ShapeDtypeStruct(s.shape[1:], F32) for s in stacks],
        compiler_params=_params(1),
    )(idx_arr, *stacks, *r1s, *r2s)


def _adam_big(pieces, summed, ws, ms, vs, idx_arr):
    stacks, r1s, r2s = pieces
    n_p, n = len(stacks), len(ws)

    def body(idx_ref, *refs):
        it = iter(refs)
        g0, r1, r2, gs, w, m, v = ([next(it) for _ in range(k)] for k in (n_p, n_p, n_p, n - n_p, n, n, n))
        outs = list(it)
        for a in range(n):
            g = _piece_sum(g0[a], r1[a], r2[a]) if a < n_p else gs[a - n_p][...]
            delta, nm, nv = _adamw(w[a][...], g, m[a][...], v[a][...])
            outs[4 * a][...] = g
            outs[4 * a + 1][...] = delta
            outs[4 * a + 2][...] = nm
            outs[4 * a + 3][...] = nv

    def shard_spec(w):
        return pl.BlockSpec((w.shape[0] // ADAM_STEPS, w.shape[1]), lambda i, idx_ref: (i, 0))

    in_specs = _piece_specs(stacks) + [shard_spec(w) for w in ws[n_p:]] + [shard_spec(w) for w in ws] * 3
    out_specs = [shard_spec(w) for w in ws for _ in range(4)]
    grid_spec = pltpu.PrefetchScalarGridSpec(num_scalar_prefetch=1, grid=(ADAM_STEPS,), in_specs=in_specs,
                                             out_specs=out_specs)
    return pl.pallas_call(
        body, name="adam_big", grid_spec=grid_spec,
        out_shape=[jax.ShapeDtypeStruct(w.shape, F32) for w in ws for _ in range(4)],
        compiler_params=_params(1),
    )(idx_arr, *stacks, *r1s, *r2s, *summed, *ws, *ms, *vs)


def _adam_small(small_all, w, m, v):
    def body(all_ref, w_ref, m_ref, v_ref, g_ref, d_ref, nm_ref, nv_ref):
        g = all_ref[0]
        for k in range(1, N_DEV):
            g = g + all_ref[k]
        delta, nm, nv = _adamw(w_ref[...], g, m_ref[...], v_ref[...])
        g_ref[...] = g
        d_ref[...] = delta
        nm_ref[...] = nm
        nv_ref[...] = nv

    return pl.pallas_call(
        body, name="adam_small",
        in_specs=[VMEM_FULL] * 4, out_specs=[VMEM_FULL] * 4,
        out_shape=[jax.ShapeDtypeStruct(w.shape, F32)] * 4,
    )(small_all, w, m, v)


RS_PAYLOAD = BF


def _pack_rows(rows):
    rid = lax.broadcasted_iota(jnp.int32, (8, D_MODEL), 0)
    out = jnp.zeros((8, D_MODEL), F32)
    for i, r in enumerate(rows):
        out = jnp.where(rid == i, jnp.broadcast_to(r, (8, D_MODEL)), out)
    return out


def _small_rows(norm_mix_gain, b_gates, attn_sinks, ret_gn_gain, norm_mlp_gain, norm_final_gain):
    return [norm_mix_gain, b_gates[:, :D_MODEL], b_gates[:, D_MODEL:], ret_gn_gain, norm_mlp_gain,
            norm_final_gain.reshape(1, D_MODEL), jnp.pad(attn_sinks, ((0, 0), (0, D_MODEL - ATT_HEADS)))]


def _unpack_small(p):
    return dict(norm_mix_gain=p[0:1], b_gates=jnp.concatenate([p[1:2], p[2:3]], axis=1), ret_gn_gain=p[3:4],
                norm_mlp_gain=p[4:5], norm_final_gain=p[5], attn_sinks=p[6:7, :ATT_HEADS])


WEIGHTS = ("norm_mix_gain", "w_in", "b_gates", "attn_sinks", "ret_gn_gain", "w_att_up", "w_ret_up", "w_out",
           "norm_mlp_gain", "w_ff1", "w_ff2", "norm_final_gain")
BIG = ("w_in", "w_att_up", "w_ret_up", "w_out", "w_ff1", "w_ff2")


def kernel(x, norm_mix_gain, w_in, b_gates, attn_sinks, ret_gn_gain, w_att_up, w_ret_up, w_out, norm_mlp_gain, w_ff1, w_ff2, norm_final_gain, loss_target, m_norm_mix_gain, m_w_in, m_b_gates, m_attn_sinks, m_ret_gn_gain, m_w_att_up, m_w_ret_up, m_w_out, m_norm_mlp_gain, m_w_ff1, m_w_ff2, m_norm_final_gain, v_norm_mix_gain, v_w_in, v_b_gates, v_attn_sinks, v_ret_gn_gain, v_w_att_up, v_w_ret_up, v_w_out, v_norm_mlp_gain, v_w_ff1, v_w_ff2, v_norm_final_gain):
    w = dict(norm_mix_gain=norm_mix_gain, w_in=w_in, b_gates=b_gates, attn_sinks=attn_sinks, ret_gn_gain=ret_gn_gain,
             w_att_up=w_att_up, w_ret_up=w_ret_up, w_out=w_out, norm_mlp_gain=norm_mlp_gain, w_ff1=w_ff1,
             w_ff2=w_ff2, norm_final_gain=norm_final_gain)
    m = dict(norm_mix_gain=m_norm_mix_gain, w_in=m_w_in, b_gates=m_b_gates, attn_sinks=m_attn_sinks,
             ret_gn_gain=m_ret_gn_gain, w_att_up=m_w_att_up, w_ret_up=m_w_ret_up, w_out=m_w_out,
             norm_mlp_gain=m_norm_mlp_gain, w_ff1=m_w_ff1, w_ff2=m_w_ff2, norm_final_gain=m_norm_final_gain)
    v = dict(norm_mix_gain=v_norm_mix_gain, w_in=v_w_in, b_gates=v_b_gates, attn_sinks=v_attn_sinks,
             ret_gn_gain=v_ret_gn_gain, w_att_up=v_w_att_up, w_ret_up=v_w_ret_up, w_out=v_w_out,
             norm_mlp_gain=v_norm_mlp_gain, w_ff1=v_w_ff1, w_ff2=v_w_ff2, norm_final_gain=v_norm_final_gain)

    shards = {k: (w[k][0].T if k in COLUMN_SHARDED else w[k][0]).astype(BF) for k in BIG}
    loss_p, dx, small, groups = _local_step(
        x[0], loss_target[0], norm_mix_gain, b_gates, attn_sinks[0], ret_gn_gain, norm_mlp_gain,
        norm_final_gain.reshape(1, D_MODEL), shards)

    by_name = {}
    for grp in groups:
        for k, stack, r1, r2 in zip(grp.names, grp.stacks, grp.r1s, grp.r2s):
            by_name[k] = (stack, r1, r2)
    lane = lax.broadcasted_iota(jnp.int32, (1, D_MODEL), 1)
    sink_row = jnp.where(lane < ATT_HEADS, jnp.pad(small["attn_sinks"], ((0, 0), (0, D_MODEL - LANE))),
                         jnp.where(lane == ATT_HEADS, jnp.pad(loss_p, ((0, 0), (0, D_MODEL - LANE)), mode="edge"),
                                   0.0))
    small_pack = _pack_rows([small["norm_mix_gain"], small["b_gates"][:, :D_MODEL], small["b_gates"][:, D_MODEL:],
                             small["ret_gn_gain"], small["norm_mlp_gain"], small["norm_final_gain"], sink_row])
    xi, yi, ci = _coords()
    (small_all,) = _comm_alone(_small_comm(small_pack), "small_exchange")
    idx_arr = jnp.stack([4 * xi + 2 * yi + ci, 2 * xi + yi]).astype(jnp.int32)
    in_pieces = ["w_att_up", "w_ret_up", "w_out", "w_ff2", "w_in"]
    in_sum = ["w_ff1"]
    order = in_pieces + in_sum
    summed_t = _shard_sum(*[[by_name[k][i] for k in in_sum] for i in range(3)], idx_arr)

    def shard(tree, k):
        return tree[k][0].T if k == "w_in" else tree[k][0]

    adam_out = _adam_big([[by_name[k][i] for k in in_pieces] for i in range(3)], [g.T for g in summed_t],
                         [shard(w, k) for k in order], [shard(m, k) for k in order], [shard(v, k) for k in order],
                         idx_arr)
    big_out = [adam_out[4 * order.index(k) + i].T if k == "w_in" else adam_out[4 * order.index(k) + i]
               for k in BIG for i in range(4)]
    sm_g, sm_d, sm_m, sm_v = _adam_small(small_all, _pack_rows(_small_rows(*[w[k] for k in WEIGHTS if k not in BIG])),
                                         _pack_rows(_small_rows(*[m[k] for k in WEIGHTS if k not in BIG])),
                                         _pack_rows(_small_rows(*[v[k] for k in WEIGHTS if k not in BIG])))

    loss = sm_g[6, ATT_HEADS]
    kinds = []
    for idx, packed in enumerate((sm_g, sm_d, sm_m, sm_v)):
        out = _unpack_small(packed)
        for a, k in enumerate(BIG):
            out[k] = big_out[4 * a + idx][None]
        kinds.append(out)
    return (loss, dx[None], *[kinds[0][k] for k in WEIGHTS], *[kinds[1][k] for k in WEIGHTS],
            *[kinds[2][k] for k in WEIGHTS], *[kinds[3][k] for k in WEIGHTS])
```

```python
import functools
import math

import jax
import jax.numpy as jnp
from jax import lax
from jax.experimental import pallas as pl
from jax.experimental.pallas import tpu as pltpu

F32 = jnp.float32
BF = jnp.bfloat16
MESH = pl.DeviceIdType.MESH

D_MODEL = 1024
ATT_HEADS = 8
ATT_HEAD_DIM = 64
ATT_BLOCK = 128
ROPE_DIM = 16
ROPE_THETA = 500000.0
RET_HEADS = 4
RET_KEY_DIM = 128
RET_VAL_DIM = 256
RET_CHUNK = 128
RET_ROT_BASE = 10000.0
D_FF = 4096
NORM_EPS = 1e-6
GN_EPS = 1e-6
NEG_INF = -1e30
ATT_SCALE = ATT_HEAD_DIM ** -0.5
RET_SCALE = RET_KEY_DIM ** -0.5

C_QA, C_KA, C_VA, C_QR, C_KR, C_VR, C_GR, C_GATES, C_END = 0, 512, 640, 768, 1280, 1792, 2816, 3840, 5888

ADAM_LR = 0.001
ADAM_B1 = 0.9
ADAM_B2 = 0.999
ADAM_EPS = 1e-08
ADAM_WD = 0.01
ADAM_STEP = 10

N_DEV = 8
LANE = 128
VMEM_LIMIT = 56 * 1024 * 1024
TOKEN_TILE = 512
MLP_TOKEN_TILE = 256
TN_TOKEN_TILE = 2048
FF_CHUNKS = (0, 1024, 2048, 3072, 4096)
IN_CHUNKS_FWD = (C_QA, C_KA, C_QR, C_KR, C_VR, C_GR, C_GATES, C_END)
IN_CHUNKS_BWD = (C_QA, C_KA, C_QR, C_GATES, C_END)

RET_LOG_GAMMA = tuple(math.log1p(-(2.0 ** (-5.0 - h))) for h in range(RET_HEADS))
RET_CHUNK_DECAY = tuple(math.exp(RET_CHUNK * lg) for lg in RET_LOG_GAMMA)

VMEM_FULL = pl.BlockSpec(memory_space=pltpu.VMEM)
SMEM_FULL = pl.BlockSpec(memory_space=pltpu.SMEM)
ANY = pl.BlockSpec(memory_space=pl.ANY)


def _params(n_axes):
    return pltpu.CompilerParams(dimension_semantics=("arbitrary",) * n_axes, vmem_limit_bytes=VMEM_LIMIT)


def _nn(a, b):
    return jnp.dot(a, b, preferred_element_type=F32)


def _nt(a, b):
    return lax.dot_general(a, b, (((1,), (1,)), ((), ())), preferred_element_type=F32)


def _tn(a, b):
    return lax.dot_general(a, b, (((0,), (0,)), ((), ())), preferred_element_type=F32)


def _sigmoid(v):
    return 1.0 / (1.0 + jnp.exp(-v))


def _rows(tile, width):
    return pl.BlockSpec((tile, width), lambda i: (i, 0))


def _cols(height, tile):
    return pl.BlockSpec((height, tile), lambda i: (0, i))


class _Resident:
    def __init__(self, hbm_ref, vmem_ref, sems, bounds):
        self.hbm, self.vmem, self.sems, self.bounds = hbm_ref, vmem_ref, sems, bounds

    @staticmethod
    def scratch(w, bounds):
        return [pltpu.VMEM(w.shape, w.dtype), pltpu.SemaphoreType.DMA((len(bounds) - 1,))]

    @staticmethod
    def load(*weights):
        @pl.when(pl.program_id(0) == 0)
        def _():
            copies = [w._copy(c) for w in weights for c in range(len(w.bounds) - 1)]
            for cp in copies:
                cp.start()
            for cp in copies:
                cp.wait()

    def _rows(self, c):
        return pl.ds(self.bounds[c], self.bounds[c + 1] - self.bounds[c])

    def _copy(self, c):
        return pltpu.make_async_copy(self.hbm.at[self._rows(c)], self.vmem.at[self._rows(c)], self.sems.at[c])

    def chunk(self, c):
        return self.vmem.at[self._rows(c)]


class _Comm:
    def __init__(self, inputs, out_shapes, scratch, start, finish, relay=None):
        self.inputs, self.out_shapes, self.scratch = inputs, out_shapes, scratch
        self.start, self.finish, self.relay = start, finish, relay


RELAY_AT = 0.7


def _join(a, b):
    na_in, na_out, na_sem = len(a.inputs), len(a.out_shapes), len(a.scratch)

    def both(name):
        def run(ins, outs, sems):
            for part, args in ((a, (ins[:na_in], outs[:na_out], sems[:na_sem])),
                               (b, (ins[na_in:], outs[na_out:], sems[na_sem:]))):
                if getattr(part, name) is not None:
                    getattr(part, name)(*args)
        return run

    return _Comm(list(a.inputs) + list(b.inputs), list(a.out_shapes) + list(b.out_shapes),
                 list(a.scratch) + list(b.scratch), both("start"), both("finish"),
                 both("relay") if a.relay or b.relay else None)


def _call(body, args, comm=None, *, name, grid, in_specs, out_specs, out_shape, scratch_shapes=()):
    params = _params(len(grid))
    if comm is None:
        return pl.pallas_call(body, name=name, grid=grid, in_specs=in_specs, out_specs=out_specs, out_shape=out_shape,
                              scratch_shapes=scratch_shapes, compiler_params=params)(*args), ()
    single = not isinstance(out_specs, (list, tuple))
    out_specs_l = [out_specs] if single else list(out_specs)
    out_shape_l = [out_shape] if single else list(out_shape)
    n_in, n_out, n_scr = len(in_specs), len(out_specs_l), len(scratch_shapes)
    n_cin, n_cout = len(comm.inputs), len(comm.out_shapes)

    def hosted(*refs):
        it = iter(refs)
        ins, cin, outs, cout, scr = ([next(it) for _ in range(k)] for k in (n_in, n_cin, n_out, n_cout, n_scr))
        sems = list(it)
        ids = [pl.program_id(k) for k in range(len(grid))]
        first = functools.reduce(jnp.logical_and, [i == 0 for i in ids])
        last = functools.reduce(jnp.logical_and, [i == g - 1 for i, g in zip(ids, grid)])

        @pl.when(first)
        def _():
            comm.start(cin, cout, sems)

        if comm.relay is not None:
            at = [int(grid[0] * RELAY_AT)] + [0] * (len(grid) - 1)

            @pl.when(functools.reduce(jnp.logical_and, [i == v for i, v in zip(ids, at)]))
            def _():
                comm.relay(cin, cout, sems)

        body(*ins, *outs, *scr)

        @pl.when(last)
        def _():
            comm.finish(cin, cout, sems)

    res = pl.pallas_call(
        hosted, name=name, grid=grid, in_specs=list(in_specs) + [ANY] * n_cin,
        out_specs=out_specs_l + [ANY] * n_cout, out_shape=out_shape_l + list(comm.out_shapes),
        scratch_shapes=list(scratch_shapes) + list(comm.scratch), compiler_params=params)(*args, *comm.inputs)
    return (res[0] if single else res[:n_out]), res[n_out:]


def _comm_alone(comm, name):
    n_cin, n_cout = len(comm.inputs), len(comm.out_shapes)

    def body(*refs):
        cin, cout, sems = refs[:n_cin], refs[n_cin:n_cin + n_cout], refs[n_cin + n_cout:]
        comm.start(cin, cout, sems)
        if comm.relay is not None:
            comm.relay(cin, cout, sems)
        comm.finish(cin, cout, sems)

    return pl.pallas_call(body, name=name, in_specs=[ANY] * n_cin, out_specs=[ANY] * n_cout,
                          out_shape=list(comm.out_shapes), scratch_shapes=list(comm.scratch))(*comm.inputs)


def _slabs(v, fn):
    return jnp.concatenate([fn(v[:, LANE * j:LANE * (j + 1)]) for j in range(v.shape[1] // LANE)], axis=1)


def _rope_att(v, ca, sa, sb):
    return _slabs(v, lambda t: t * ca + pltpu.roll(t, LANE - 8, 1) * sa + pltpu.roll(t, 8, 1) * sb)


def _rope_att_t(v, ca, sa, sb):
    return _slabs(v, lambda t: t * ca + pltpu.roll(t * sa, 8, 1) + pltpu.roll(t * sb, LANE - 8, 1))


def _rope_att_rows(v, ct, st, sign):
    parts = []
    for h in range(v.shape[0] // ATT_HEAD_DIM):
        r0 = ATT_HEAD_DIM * h
        x1, x2 = v[r0:r0 + 8], v[r0 + 8:r0 + 16]
        parts += [x1 * ct - sign * (x2 * st), x2 * ct + sign * (x1 * st), v[r0 + 16:r0 + ATT_HEAD_DIM]]
    return jnp.concatenate(parts, axis=0)


def _rope_ret(v, cr, sr):
    return _slabs(v, lambda t: t * cr + pltpu.roll(t, 64, 1) * sr)


def _rope_ret_t(v, cr, sr):
    return _slabs(v, lambda t: t * cr + pltpu.roll(t * sr, 64, 1))


def _rope_lane_tables():
    def inv(dim, theta):
        return theta ** (-jnp.arange(0, dim, 2, dtype=F32) / dim)

    inv_a, inv_r = inv(ROPE_DIM, ROPE_THETA), inv(RET_KEY_DIM, RET_ROT_BASE)
    half = ROPE_DIM // 2
    zeros = jnp.zeros((ATT_HEAD_DIM - ROPE_DIM,), F32)
    freq64 = jnp.concatenate([inv_a, inv_a, zeros])
    lo64 = jnp.concatenate([-jnp.ones((half,), F32), jnp.zeros((half,), F32), zeros])
    hi64 = jnp.concatenate([jnp.zeros((half,), F32), jnp.ones((half,), F32), zeros])
    sign_r = jnp.concatenate([-jnp.ones((64,), F32), jnp.ones((64,), F32)])
    rows = [jnp.tile(freq64, 2), jnp.tile(lo64, 2), jnp.tile(hi64, 2), jnp.tile(inv_r, 2), sign_r]
    lanes = jnp.stack(rows + [jnp.zeros((LANE,), F32)] * (8 - len(rows)))
    return lanes, jnp.broadcast_to(inv_a[:, None], (half, LANE))


def _prologue(x, g1, comm):
    seq = x.shape[0]
    tm = min(TOKEN_TILE, seq)

    def body(x_ref, g_ref, lanes_ref, freq_rows_ref, xn_ref, ca_ref, sa_ref, sb_ref, cr_ref, sr_ref, ct_ref, st_ref):
        xf = x_ref[...]
        r = lax.rsqrt(jnp.mean(xf * xf, axis=-1, keepdims=True) + NORM_EPS)
        xn_ref[...] = (xf * r * g_ref[...]).astype(BF)
        row0 = pl.program_id(0) * tm
        pos = (row0 + lax.broadcasted_iota(jnp.int32, (tm, LANE), 0)).astype(F32)
        ang_a = pos * lanes_ref[0:1, :]
        sin_a = jnp.sin(ang_a)
        ca_ref[...] = jnp.cos(ang_a)
        sa_ref[...] = sin_a * lanes_ref[1:2, :]
        sb_ref[...] = sin_a * lanes_ref[2:3, :]
        ang_r = pos * lanes_ref[3:4, :]
        cr_ref[...] = jnp.cos(ang_r)
        sr_ref[...] = jnp.sin(ang_r) * lanes_ref[4:5, :]
        pos_t = (row0 + lax.broadcasted_iota(jnp.int32, (8, tm), 1)).astype(F32)
        ang_t = pos_t * jnp.concatenate([freq_rows_ref[...]] * (tm // LANE), axis=1)
        ct_ref[...] = jnp.cos(ang_t)
        st_ref[...] = jnp.sin(ang_t)

    tab = _rows(tm, LANE)
    tab_t = _cols(8, tm)
    return _call(
        body, (x, g1, *_rope_lane_tables()), comm, name="prologue", grid=(seq // tm,),
        in_specs=[_rows(tm, D_MODEL), VMEM_FULL, VMEM_FULL, VMEM_FULL],
        out_specs=[_rows(tm, D_MODEL), tab, tab, tab, tab, tab, tab_t, tab_t],
        out_shape=[jax.ShapeDtypeStruct((seq, D_MODEL), BF)] + [jax.ShapeDtypeStruct((seq, LANE), F32)] * 5
        + [jax.ShapeDtypeStruct((8, seq), F32)] * 2)


def _ret_tables():
    c = RET_CHUNK
    lg = jnp.asarray(RET_LOG_GAMMA, F32)
    idx = jnp.arange(c, dtype=F32)
    diff = idx[:, None] - idx[None, :]
    intra = jnp.where(diff >= 0, jnp.exp(jnp.maximum(diff, 0.0) * lg[:, None, None]), 0.0)
    qd = jnp.exp((idx + 1.0)[None, :] * lg[:, None])[..., None]
    kd = jnp.exp((c - 1.0 - idx)[None, :] * lg[:, None])[..., None]
    return intra, jnp.broadcast_to(qd, (RET_HEADS, c, RET_KEY_DIM)), jnp.broadcast_to(kd, (RET_HEADS, c, RET_KEY_DIM))


def _inproj_fwd(xn1, w_in, tabs, comm=None):
    seq = xn1.shape[0]
    tm = min(TOKEN_TILE, seq)

    def body(xn_ref, w_hbm, ca_ref, sa_ref, sb_ref, cr_ref, sr_ref, ct_ref, st_ref,
             qt_ref, kv_ref, kvt_ref, qkr_ref, vr_ref, gr_ref, gates_ref, w_vmem, w_sems):
        w = _Resident(w_hbm, w_vmem, w_sems, IN_CHUNKS_FWD)
        _Resident.load(w)
        ca, sa, sb, cr, sr, ct, st = (ref[...] for ref in (ca_ref, sa_ref, sb_ref, cr_ref, sr_ref, ct_ref, st_ref))
        xn = xn_ref[...]
        qt_ref[...] = _rope_att_rows(_nt(w.chunk(0)[...], xn), ct, st, 1.0).astype(BF)
        w_kv = w.chunk(1)
        kvt = _nt(w_kv[...], xn)
        kvt_ref[:LANE, :] = _rope_att_rows(kvt[:LANE], ct, st, 1.0).astype(BF)
        kvt_ref[LANE:, :] = kvt[LANE:].astype(BF)
        kvn = _nt(xn, w_kv[...])
        kv_ref[:, :LANE] = _rope_att(kvn[:, :LANE], ca, sa, sb).astype(BF)
        kv_ref[:, LANE:] = kvn[:, LANE:].astype(BF)
        qr = _rope_ret(_nt(xn, w.chunk(2)[...]), cr, sr)
        qkr_ref[:, :512] = qr.astype(BF)
        kr = _rope_ret(_nt(xn, w.chunk(3)[...]), cr, sr) * RET_SCALE
        qkr_ref[:, 512:] = kr.astype(BF)
        vr_ref[...] = _nt(xn, w.chunk(4)[...]).astype(BF)
        gr_ref[...] = _nt(xn, w.chunk(5)[...]).astype(BF)
        gates_ref[...] = _nt(xn, w.chunk(6)[...]).astype(BF)

    tab = _rows(tm, LANE)
    tab_t = _cols(8, tm)
    return _call(
        body, (xn1, w_in, *tabs), comm, name="inproj_fwd", grid=(seq // tm,),
        in_specs=[_rows(tm, D_MODEL), ANY, tab, tab, tab, tab, tab, tab_t, tab_t],
        out_specs=[_cols(512, tm), _rows(tm, 256), _cols(256, tm), _rows(tm, 1024),
                   _rows(tm, 1024), _rows(tm, 1024), _rows(tm, 2048)],
        out_shape=[jax.ShapeDtypeStruct(s, BF) for s in ((512, seq), (seq, 256), (256, seq),
                                                         (seq, 1024), (seq, 1024), (seq, 1024), (seq, 2048))],
        scratch_shapes=_Resident.scratch(w_in, IN_CHUNKS_FWD))


ATT_Q_PER_KV = ATT_HEADS // 2


def _att_group(ref, kv):
    d = ATT_HEAD_DIM
    g = jnp.concatenate([ref[d * (ATT_Q_PER_KV * kv + j):d * (ATT_Q_PER_KV * kv + j + 1), :]
                         for j in range(ATT_Q_PER_KV)], axis=1)
    z = jnp.zeros_like(g)
    return jnp.concatenate([g, z] if kv == 0 else [z, g], axis=0)


def _att_probs(kwin, qpad, sink_ref, kv, i):
    c = ATT_BLOCK
    kj = lax.broadcasted_iota(jnp.int32, (2 * c, c), 0)
    qi = lax.broadcasted_iota(jnp.int32, (2 * c, c), 1)
    allowed = (kj > qi) & (kj <= qi + c) & ((kj >= c) | (i > 0))
    allowed = jnp.concatenate([allowed] * ATT_Q_PER_KV, axis=1)
    s = jnp.where(allowed, _nn(kwin, qpad) * ATT_SCALE, NEG_INF)
    sk = jnp.concatenate([jnp.full((1, c), sink_ref[ATT_Q_PER_KV * kv + j], F32) for j in range(ATT_Q_PER_KV)], axis=1)
    m = jnp.maximum(jnp.max(s, axis=0, keepdims=True), sk)
    pe = jnp.exp(s - m)
    psink = jnp.exp(sk - m)
    inv = 1.0 / (jnp.sum(pe, axis=0, keepdims=True) + psink)
    return pe * inv, psink * inv


def _seq_fwd(qt, kv, kvt, sinks, qkr, vr, gr, gain, rtabs, comm=None):
    seq = kv.shape[0]
    c = ATT_BLOCK
    assert c == RET_CHUNK
    d = ATT_HEAD_DIM
    nb = seq // c
    dk, dv = RET_KEY_DIM, RET_VAL_DIM

    def body(sink_ref, qt_ref, kvc_ref, kvp_ref, kvtc_ref, kvtp_ref, qk_ref, v_ref, g_ref, gain_ref, intra_ref,
             qd_ref, kd_ref, ot_ref, ret_ref, yrp_ref, st_ref, state):
        i = pl.program_id(0)
        kwin = jnp.concatenate([kvp_ref[:, :LANE], kvc_ref[:, :LANE]], axis=0)
        vtwin = jnp.concatenate([kvtp_ref[LANE:, :], kvtc_ref[LANE:, :]], axis=1)
        for g in (0, 1):
            prob, _ = _att_probs(kwin, _att_group(qt_ref, g), sink_ref, g, i)
            out = _nn(vtwin[d * g:d * (g + 1), :], prob.astype(BF))
            for j in range(ATT_Q_PER_KV):
                h = ATT_Q_PER_KV * g + j
                ot_ref[d * h:d * (h + 1), :] = out[:, c * j:c * (j + 1)].astype(BF)

        @pl.when(i == 0)
        def _():
            state[...] = jnp.zeros_like(state)

        for h in range(RET_HEADS):
            qh = qk_ref[:, dk * h:dk * (h + 1)]
            kh = qk_ref[:, 512 + dk * h:512 + dk * (h + 1)]
            vh = v_ref[:, dv * h:dv * (h + 1)]
            sh = state[h]
            shb = sh.astype(BF)
            st_ref[0, h] = shb
            att = _nt(qh, kh) * intra_ref[h]
            inner = _nn(att.astype(BF), vh)
            cross = _nn((qh.astype(F32) * qd_ref[h]).astype(BF), shb)
            out = inner + cross
            state[h] = sh * RET_CHUNK_DECAY[h] + _tn((kh.astype(F32) * kd_ref[h]).astype(BF), vh)
            ret_ref[:, dv * h:dv * (h + 1)] = out
            mu = jnp.mean(out, axis=-1, keepdims=True)
            dev = out - mu
            var = jnp.mean(dev * dev, axis=-1, keepdims=True)
            y = dev * lax.rsqrt(var + GN_EPS) * gain_ref[:, dv * h:dv * (h + 1)]
            g = g_ref[:, dv * h:dv * (h + 1)].astype(F32)
            yrp_ref[:, dv * h:dv * (h + 1)] = (g * _sigmoid(g) * y).astype(BF)

    prev = lambda i: jnp.maximum(i - 1, 0)
    return _call(
        body, (sinks, qt, kv, kv, kvt, kvt, qkr, vr, gr, gain, *rtabs), comm, name="seq_fwd", grid=(nb,),
        in_specs=[SMEM_FULL, _cols(512, c), _rows(c, 256), pl.BlockSpec((c, 256), lambda i: (prev(i), 0)),
                  _cols(256, c), pl.BlockSpec((256, c), lambda i: (0, prev(i))),
                  _rows(c, 1024), _rows(c, 1024), _rows(c, 1024), VMEM_FULL, VMEM_FULL, VMEM_FULL, VMEM_FULL],
        out_specs=[_cols(512, c), _rows(c, 1024), _rows(c, 1024),
                   pl.BlockSpec((1, RET_HEADS, dk, dv), lambda i: (i, 0, 0, 0))],
        out_shape=[jax.ShapeDtypeStruct((512, seq), BF), jax.ShapeDtypeStruct((seq, 1024), F32),
                   jax.ShapeDtypeStruct((seq, 1024), BF), jax.ShapeDtypeStruct((nb, RET_HEADS, dk, dv), BF)],
        scratch_shapes=[pltpu.VMEM((RET_HEADS, dk, dv), F32)])


def _mix_fwd(oat, yrp, gates, bg, x, w_att, w_ret, w_out):
    seq = x.shape[0]
    tm = min(TOKEN_TILE, seq)

    def body(oat_ref, yrp_ref, gates_ref, bg_ref, x_ref, wa_ref, wr_ref, wo_ref, ya_ref, yr_ref, mg_ref, h1_ref):
        ya = _tn(oat_ref[...], wa_ref[...])
        yr = _nn(yrp_ref[...], wr_ref[...])
        gt = _sigmoid(gates_ref[...].astype(F32) + bg_ref[...])
        merged = (gt[:, :D_MODEL] * ya + gt[:, D_MODEL:] * yr).astype(BF)
        ya_ref[...] = ya.astype(BF)
        yr_ref[...] = yr.astype(BF)
        mg_ref[...] = merged
        h1_ref[...] = x_ref[...] + _nn(merged, wo_ref[...])

    return pl.pallas_call(
        body, name="mix_fwd", grid=(seq // tm,),
        in_specs=[_cols(512, tm), _rows(tm, 1024), _rows(tm, 2048), VMEM_FULL, _rows(tm, D_MODEL),
                  VMEM_FULL, VMEM_FULL, VMEM_FULL],
        out_specs=[_rows(tm, D_MODEL)] * 4,
        out_shape=[jax.ShapeDtypeStruct((seq, D_MODEL), BF)] * 3 + [jax.ShapeDtypeStruct((seq, D_MODEL), F32)],
        compiler_params=_params(1),
    )(oat, yrp, gates, bg, x, w_att, w_ret, w_out)


def _mlp_loss_step(h1, g2, g3, target, w_ff1, w_ff2):
    seq = h1.shape[0]
    tm = min(MLP_TOKEN_TILE, seq)
    n_chunks = len(FF_CHUNKS) - 1

    def body(h1_ref, g2_ref, g3_ref, t_ref, w1_hbm, w2_hbm,
             xn2_ref, hdn_ref, dh2_ref, du_ref, dh1_ref, loss_ref, dg3_ref, dg2_ref,
             w1_vmem, w1_sems, w2_vmem, w2_sems, relu_u):
        w1 = _Resident(w1_hbm, w1_vmem, w1_sems, FF_CHUNKS)
        w2 = _Resident(w2_hbm, w2_vmem, w2_sems, FF_CHUNKS)
        _Resident.load(w1, w2)

        @pl.when(pl.program_id(0) == 0)
        def _():
            loss_ref[...] = jnp.zeros_like(loss_ref)
            dg3_ref[...] = jnp.zeros_like(dg3_ref)
            dg2_ref[...] = jnp.zeros_like(dg2_ref)

        h1v = h1_ref[...]
        r2 = lax.rsqrt(jnp.mean(h1v * h1v, axis=-1, keepdims=True) + NORM_EPS)
        xn2 = (h1v * r2 * g2_ref[...]).astype(BF)
        xn2_ref[...] = xn2
        h2 = h1v
        for c in range(n_chunks):
            cols = slice(FF_CHUNKS[c], FF_CHUNKS[c + 1])
            a = jnp.maximum(_nt(xn2, w1.chunk(c)[...]), 0.0)
            relu_u[:, cols] = a.astype(BF)
            hdn = jnp.square(a).astype(BF)
            hdn_ref[:, cols] = hdn
            h2 = h2 + _nn(hdn, w2.chunk(c)[...])
        r3 = lax.rsqrt(jnp.mean(h2 * h2, axis=-1, keepdims=True) + NORM_EPS)
        hn = h2 * r3
        err = hn * g3_ref[...] - t_ref[...]
        loss_ref[...] += jnp.sum(err * err) * (0.5 / D_MODEL)
        dy = err * (1.0 / D_MODEL)
        dg3_ref[...] += jnp.sum(dy * hn, axis=0, keepdims=True)
        z = dy * g3_ref[...]
        dh2 = r3 * (z - hn * jnp.mean(z * hn, axis=-1, keepdims=True))
        dh2b = dh2.astype(BF)
        dh2_ref[...] = dh2b
        dxn2 = jnp.zeros_like(dh2)
        for c in range(n_chunks):
            cols = slice(FF_CHUNKS[c], FF_CHUNKS[c + 1])
            du = (_nt(dh2b, w2.chunk(c)[...]) * (2.0 * relu_u[:, cols].astype(F32))).astype(BF)
            du_ref[:, cols] = du
            dxn2 = dxn2 + _nn(du, w1.chunk(c)[...])
        dnorm, dg = _rms_bwd(dxn2, h1v, g2_ref[...])
        dh1_ref[...] = dh2 + dnorm
        dg2_ref[...] += dg

    gain_acc = pl.BlockSpec((1, D_MODEL), lambda i: (0, 0))
    return pl.pallas_call(
        body, name="mlp_loss_step", grid=(seq // tm,),
        in_specs=[_rows(tm, D_MODEL), VMEM_FULL, VMEM_FULL, _rows(tm, D_MODEL), ANY, ANY],
        out_specs=[_rows(tm, D_MODEL), _rows(tm, D_FF), _rows(tm, D_MODEL), _rows(tm, D_FF), _rows(tm, D_MODEL),
                   pl.BlockSpec((1, LANE), lambda i: (0, 0)), gain_acc, gain_acc],
        out_shape=[jax.ShapeDtypeStruct((seq, D_MODEL), BF), jax.ShapeDtypeStruct((seq, D_FF), BF),
                   jax.ShapeDtypeStruct((seq, D_MODEL), BF), jax.ShapeDtypeStruct((seq, D_FF), BF),
                   jax.ShapeDtypeStruct((seq, D_MODEL), F32), jax.ShapeDtypeStruct((1, LANE), F32),
                   jax.ShapeDtypeStruct((1, D_MODEL), F32), jax.ShapeDtypeStruct((1, D_MODEL), F32)],
        scratch_shapes=(_Resident.scratch(w_ff1, FF_CHUNKS) + _Resident.scratch(w_ff2, FF_CHUNKS)
                        + [pltpu.VMEM((tm, D_FF), BF)]),
        compiler_params=_params(1),
    )(h1, g2, g3, target, w_ff1, w_ff2)


def _rms_bwd(dxn, xin, gain):
    r = lax.rsqrt(jnp.mean(xin * xin, axis=-1, keepdims=True) + NORM_EPS)
    xhat = xin * r
    z = dxn * gain
    dxin = r * (z - xhat * jnp.mean(z * xhat, axis=-1, keepdims=True))
    return dxin, jnp.sum(dxn * xhat, axis=0, keepdims=True)


def _mm_tn(a, b, name, a_is_t=False, comm=None):
    kdim, seq = a.shape if a_is_t else a.shape[::-1]
    ndim = b.shape[1]
    ts = min(TN_TOKEN_TILE, seq)
    tk = min(kdim, 1024)
    tn = min(ndim, 1024)
    n_steps = seq // ts

    def body(a_ref, b_ref, o_ref):
        @pl.when(pl.program_id(2) == 0)
        def _():
            o_ref[...] = jnp.zeros_like(o_ref)

        mm = _nn if a_is_t else _tn
        o_ref[...] += mm(a_ref[...].astype(BF), b_ref[...].astype(BF))

    a_spec = (pl.BlockSpec((tk, ts), lambda k, n, s: (k, s)) if a_is_t
              else pl.BlockSpec((ts, tk), lambda k, n, s: (s, k)))
    res = _call(
        body, (a, b), comm, name=name, grid=(kdim // tk, ndim // tn, n_steps),
        in_specs=[a_spec, pl.BlockSpec((ts, tn), lambda k, n, s: (s, n))],
        out_specs=pl.BlockSpec((tk, tn), lambda k, n, s: (k, n)),
        out_shape=jax.ShapeDtypeStruct((kdim, ndim), F32))
    return res[0] if comm is None else res


def _dw_in_t(pieces, is_t, xn1, comm=None):
    seq = xn1.shape[0]
    tr = 256
    n_blk = [(p.shape[0] if t else p.shape[1]) // tr for p, t in zip(pieces, is_t)]
    offs = [sum(n_blk[:i]) for i in range(len(pieces) + 1)]
    n_p = len(pieces)

    def body(*refs):
        piece_refs, xn_ref, o_ref, ob_ref = refs[:n_p], refs[n_p], refs[n_p + 1], refs[n_p + 2]
        r = pl.program_id(0)
        for p in range(n_p):
            @pl.when((r >= offs[p]) & (r < offs[p + 1]))
            def _(p=p):
                res = (_nn if is_t[p] else _tn)(piece_refs[p][...], xn_ref[...])
                o_ref[...] = res
                ob_ref[...] = res.astype(BF)

    def piece_spec(p):
        def index(r):
            blk = jnp.clip(r - offs[p], 0, n_blk[p] - 1)
            return (blk, 0) if is_t[p] else (0, blk)
        return pl.BlockSpec((tr, seq) if is_t[p] else (seq, tr), index)

    return _call(
        body, (*pieces, xn1), comm, name="dw_in", grid=(offs[-1],),
        in_specs=[piece_spec(p) for p in range(n_p)] + [VMEM_FULL],
        out_specs=[_rows(tr, D_MODEL), _rows(tr, D_MODEL)],
        out_shape=[jax.ShapeDtypeStruct((offs[-1] * tr, D_MODEL), F32),
                   jax.ShapeDtypeStruct((offs[-1] * tr, D_MODEL), BF)])


def _mix_bwd(dh1, ya, yr, gates, bg, merged, oat, yrp, w_att, w_ret, w_out, comm=None):
    seq = dh1.shape[0]
    tm = min(TOKEN_TILE, seq)

    def body(dh1_ref, ya_ref, yr_ref, gates_ref, bg_ref, mg_ref, oat_ref, yrp_ref, wa_ref, wr_ref, wo_ref,
             dgates_ref, doa_ref, dyrp_ref, db_ref, dwa_ref, dwr_ref, dwo_ref):
        @pl.when(pl.program_id(0) == 0)
        def _():
            for ref in (db_ref, dwa_ref, dwr_ref, dwo_ref):
                ref[...] = jnp.zeros_like(ref)

        dh1b = dh1_ref[...].astype(BF)
        dm = _nt(dh1b, wo_ref[...])
        gt = _sigmoid(gates_ref[...].astype(F32) + bg_ref[...])
        ga, gr = gt[:, :D_MODEL], gt[:, D_MODEL:]
        dya = (dm * ga).astype(BF)
        dyr = (dm * gr).astype(BF)
        dga = dm * ya_ref[...].astype(F32) * ga * (1.0 - ga)
        dgr = dm * yr_ref[...].astype(F32) * gr * (1.0 - gr)
        dgates_ref[:, :D_MODEL] = dga.astype(BF)
        dgates_ref[:, D_MODEL:] = dgr.astype(BF)
        db_ref[:, :D_MODEL] += jnp.sum(dga, axis=0, keepdims=True)
        db_ref[:, D_MODEL:] += jnp.sum(dgr, axis=0, keepdims=True)
        doa_ref[...] = _nt(wa_ref[...], dya).astype(BF)
        dyrp_ref[...] = _nt(dyr, wr_ref[...]).astype(BF)
        dwo_ref[...] += _tn(mg_ref[...], dh1b)
        dwr_ref[...] += _tn(yrp_ref[...], dyr)
        dwa_ref[...] += _nn(oat_ref[...], dya)

    acc = lambda r, c: pl.BlockSpec((r, c), lambda i: (0, 0))
    return _call(
        body, (dh1, ya, yr, gates, bg, merged, oat, yrp, w_att, w_ret, w_out), comm, name="mix_bwd",
        grid=(seq // tm,),
        in_specs=[_rows(tm, D_MODEL), _rows(tm, D_MODEL), _rows(tm, D_MODEL), _rows(tm, 2048), VMEM_FULL,
                  _rows(tm, D_MODEL), _cols(512, tm), _rows(tm, 1024), VMEM_FULL, VMEM_FULL, VMEM_FULL],
        out_specs=[_rows(tm, 2048), _cols(512, tm), _rows(tm, 1024), acc(1, 2048), acc(512, D_MODEL),
                   acc(1024, D_MODEL), acc(D_MODEL, D_MODEL)],
        out_shape=[jax.ShapeDtypeStruct((seq, 2048), BF), jax.ShapeDtypeStruct((512, seq), BF),
                   jax.ShapeDtypeStruct((seq, 1024), BF), jax.ShapeDtypeStruct((1, 2048), F32),
                   jax.ShapeDtypeStruct((512, D_MODEL), F32), jax.ShapeDtypeStruct((1024, D_MODEL), F32),
                   jax.ShapeDtypeStruct((D_MODEL, D_MODEL), F32)])


def _seq_bwd(qt, kv, kvt, dot, sinks, qkr, vr, gr, ret, dyrp, states, gain, rtabs, tabs, comm=None):
    seq = kv.shape[0]
    c = ATT_BLOCK
    assert c == RET_CHUNK
    d = ATT_HEAD_DIM
    nb = seq // c
    dk, dv = RET_KEY_DIM, RET_VAL_DIM
    ca, sa, sb, cr, sr, ct, st = tabs

    def body(sink_ref, qt_ref, kvc_ref, kvp_ref, kvtc_ref, kvtp_ref, dot_ref, ca_ref, sa_ref, sb_ref, ct_ref, st_ref,
             qk_ref, v_ref, g_ref, ret_ref, dyp_ref, st8_ref, gain_ref, intra_ref, qd_ref, kd_ref, cr_ref, sr_ref,
             dqt_ref, dkv_ref, dsink_ref, dall_ref, dgain_ref, carry, dstate):
        step = pl.program_id(0)
        i = nb - 1 - step

        @pl.when(step == 0)
        def _():
            carry[...] = jnp.zeros_like(carry)
            dsink_ref[...] = jnp.zeros_like(dsink_ref)
            dstate[...] = jnp.zeros_like(dstate)
            dgain_ref[...] = jnp.zeros_like(dgain_ref)

        kwin = jnp.concatenate([kvp_ref[:, :LANE], kvc_ref[:, :LANE]], axis=0)
        vwin = jnp.concatenate([kvp_ref[:, LANE:], kvc_ref[:, LANE:]], axis=0)
        ktwin = jnp.concatenate([kvtp_ref[:LANE, :], kvtc_ref[:LANE, :]], axis=1)
        ctv, stv = ct_ref[...], st_ref[...]
        lane1 = lax.broadcasted_iota(jnp.int32, (1, LANE), 1)
        dkw = jnp.zeros((2 * c, LANE), F32)
        dvw = jnp.zeros((2 * c, LANE), F32)
        dsink = jnp.zeros((1, LANE), F32)
        for g in (0, 1):
            qpad, dopad = _att_group(qt_ref, g), _att_group(dot_ref, g)
            prob, psink = _att_probs(kwin, qpad, sink_ref, g, i)
            dprob = _nn(vwin, dopad)
            drow = jnp.sum(dprob * prob, axis=0, keepdims=True)
            ds = (prob * (dprob - drow) * ATT_SCALE).astype(BF)
            dqg = _nn(ktwin[d * g:d * (g + 1), :], ds)
            dkw = dkw + _nt(ds, qpad)
            dvw = dvw + _nt(prob.astype(BF), dopad)
            dsink_lanes = psink * drow
            for j in range(ATT_Q_PER_KV):
                h = ATT_Q_PER_KV * g + j
                dqt_ref[d * h:d * (h + 1), :] = _rope_att_rows(dqg[:, c * j:c * (j + 1)], ctv, stv, -1.0).astype(BF)
                dsink = dsink + jnp.where(lane1 == h, -jnp.sum(dsink_lanes[:, c * j:c * (j + 1)]), 0.0)
        dsink_ref[...] += dsink
        total = carry[...] + jnp.concatenate([dkw[c:], dvw[c:]], axis=1)
        dkv_ref[:, :LANE] = _rope_att_t(total[:, :LANE], ca_ref[...], sa_ref[...], sb_ref[...]).astype(BF)
        dkv_ref[:, LANE:] = total[:, LANE:].astype(BF)
        carry[...] = jnp.concatenate([dkw[:c], dvw[:c]], axis=1)

        crv, srv = cr_ref[...], sr_ref[...]
        for h in range(RET_HEADS):
            vs = slice(dv * h, dv * (h + 1))
            qh = qk_ref[:, dk * h:dk * (h + 1)]
            kh = qk_ref[:, 512 + dk * h:512 + dk * (h + 1)]
            vh = v_ref[:, vs]
            out = ret_ref[:, vs]
            g = g_ref[:, vs].astype(F32)
            dyp = dyp_ref[:, vs].astype(F32)
            gain_h = gain_ref[:, vs]
            mu = jnp.mean(out, axis=-1, keepdims=True)
            dev = out - mu
            rstd = lax.rsqrt(jnp.mean(dev * dev, axis=-1, keepdims=True) + GN_EPS)
            yn = dev * rstd
            sg = _sigmoid(g)
            dg = dyp * (yn * gain_h) * (sg * (1.0 + g * (1.0 - sg)))
            dy = dyp * (g * sg)
            dgain_ref[:, vs] += jnp.sum(dy * yn, axis=0, keepdims=True)
            dyn = dy * gain_h
            dout = rstd * (dyn - jnp.mean(dyn, axis=-1, keepdims=True)
                           - yn * jnp.mean(dyn * yn, axis=-1, keepdims=True))
            doutb = dout.astype(BF)
            sc = st8_ref[0, h]
            dsp = dstate[h]
            dspb = dsp.astype(BF)
            intra, qdv, kdv = intra_ref[h], qd_ref[h], kd_ref[h]
            att = _nt(qh, kh) * intra
            dab = (_nt(doutb, vh) * intra).astype(BF)
            qdec = (qh.astype(F32) * qdv).astype(BF)
            kdec = (kh.astype(F32) * kdv).astype(BF)
            dq = _nn(dab, kh) + _nt(doutb, sc) * qdv
            dkk = _tn(dab, qh) + _nt(vh, dspb) * kdv
            dvv = _tn(att.astype(BF), doutb) + _nn(kdec, dspb)
            dstate[h] = dsp * RET_CHUNK_DECAY[h] + _tn(qdec, doutb)
            dall_ref[:, dk * h:dk * (h + 1)] = _rope_ret_t(dq, crv, srv).astype(BF)
            dall_ref[:, 512 + dk * h:512 + dk * (h + 1)] = (_rope_ret_t(dkk, crv, srv) * RET_SCALE).astype(BF)
            dall_ref[:, 1024 + dv * h:1024 + dv * (h + 1)] = dvv.astype(BF)
            dall_ref[:, 2048 + dv * h:2048 + dv * (h + 1)] = dg.astype(BF)

    cur = lambda s: nb - 1 - s
    prev = lambda s: jnp.maximum(nb - 2 - s, 0)
    rows = lambda w: pl.BlockSpec((c, w), lambda s: (cur(s), 0))
    cols = lambda h: pl.BlockSpec((h, c), lambda s: (0, cur(s)))
    acc = lambda w: pl.BlockSpec((1, w), lambda s: (0, 0))
    return _call(
        body, (sinks, qt, kv, kv, kvt, kvt, dot, ca, sa, sb, ct, st, qkr, vr, gr, ret, dyrp, states, gain, *rtabs,
               cr, sr), comm, name="seq_bwd", grid=(nb,),
        in_specs=[SMEM_FULL, cols(512), rows(256), pl.BlockSpec((c, 256), lambda s: (prev(s), 0)), cols(256),
                  pl.BlockSpec((256, c), lambda s: (0, prev(s))), cols(512), rows(LANE), rows(LANE), rows(LANE),
                  cols(8), cols(8), rows(1024), rows(1024), rows(1024), rows(1024), rows(1024),
                  pl.BlockSpec((1, RET_HEADS, dk, dv), lambda s: (cur(s), 0, 0, 0)),
                  VMEM_FULL, VMEM_FULL, VMEM_FULL, VMEM_FULL, rows(LANE), rows(LANE)],
        out_specs=[cols(512), rows(256), acc(LANE), rows(3072), acc(1024)],
        out_shape=[jax.ShapeDtypeStruct((512, seq), BF), jax.ShapeDtypeStruct((seq, 256), BF),
                   jax.ShapeDtypeStruct((1, LANE), F32), jax.ShapeDtypeStruct((seq, 3072), BF),
                   jax.ShapeDtypeStruct((1, 1024), F32)],
        scratch_shapes=[pltpu.VMEM((c, 256), F32), pltpu.VMEM((RET_HEADS, dk, dv), F32)])


def _inproj_bwd(dqa, dkva, dret, dgates, dh1, x, g1, w_in, comm=None):
    seq = x.shape[0]
    tm = min(TOKEN_TILE, seq)

    def body(dqa_ref, dkva_ref, dret_ref, dgates_ref, dh1_ref, x_ref, g_ref, w_hbm, dx_ref, dg1_ref, w_vmem, w_sems):
        w = _Resident(w_hbm, w_vmem, w_sems, IN_CHUNKS_BWD)
        _Resident.load(w)

        @pl.when(pl.program_id(0) == 0)
        def _():
            dg1_ref[...] = jnp.zeros_like(dg1_ref)

        dxn = _tn(dqa_ref[...], w.chunk(0)[...])
        dxn = dxn + _nn(dkva_ref[...], w.chunk(1)[...])
        dxn = dxn + _nn(dret_ref[...], w.chunk(2)[...])
        dxn = dxn + _nn(dgates_ref[...], w.chunk(3)[...])
        dnorm, dg = _rms_bwd(dxn, x_ref[...], g_ref[...])
        dx_ref[...] = dh1_ref[...] + dnorm
        dg1_ref[...] += dg

    return _call(
        body, (dqa, dkva, dret, dgates, dh1, x, g1, w_in), comm, name="inproj_bwd", grid=(seq // tm,),
        in_specs=[_cols(512, tm), _rows(tm, 256), _rows(tm, 3072), _rows(tm, 2048), _rows(tm, D_MODEL),
                  _rows(tm, D_MODEL), VMEM_FULL, ANY],
        out_specs=[_rows(tm, D_MODEL), pl.BlockSpec((1, D_MODEL), lambda i: (0, 0))],
        out_shape=[jax.ShapeDtypeStruct((seq, D_MODEL), F32), jax.ShapeDtypeStruct((1, D_MODEL), F32)],
        scratch_shapes=_Resident.scratch(w_in, IN_CHUNKS_BWD))


def _local_step(x, target, g1, bg, sinks, gain, g2, g3, shards):
    rtabs = _ret_tables()
    scatter = _Scatter

    comm, unpack = _gather(shards, ("w_in",))
    (xn1, ca, sa, sb, cr, sr, ct, st), got = _prologue(x, g1, comm)
    (w_in,) = unpack(got)
    comm_mid, unpack_mid = _gather(shards, ("w_att_up", "w_ret_up", "w_out"))
    comm_ff1, unpack_ff1 = _gather(shards, ("w_ff1",))
    (qt, kv, kvt, qkr, vr, gr, gates), got = _inproj_fwd(xn1, w_in, (ca, sa, sb, cr, sr, ct, st),
                                                         _join(comm_mid, comm_ff1))
    w_att, w_ret, w_out = unpack_mid(got[:3])
    (w_ff1,) = unpack_ff1(got[3:])
    comm, unpack = _gather(shards, ("w_ff2",))
    (oa, ret, yrp, states), got = _seq_fwd(qt, kv, kvt, sinks, qkr, vr, gr, gain, rtabs, comm)
    (w_ff2,) = unpack(got)
    ya, yr, merged, h1 = _mix_fwd(oa, yrp, gates, bg, x, w_att, w_ret, w_out)
    xn2, hdn, dh2, du, dh1, loss, dg3, dg2 = _mlp_loss_step(h1, g2, g3, target, w_ff1, w_ff2)

    ff2 = scatter(dict(w_ff2=_mm_tn(hdn, dh2, "dw_ff2")), "ff2")
    ff1 = scatter(dict(w_ff1=_mm_tn(du, xn2, "dw_ff1")), "ff1")
    (dgates, doa, dyrp, db, d_att, d_ret, d_out), got = _mix_bwd(
        dh1, ya, yr, gates, bg, merged, oa, yrp, w_att, w_ret, w_out, _join(ff2.pair_comm(), ff1.pair_comm()))
    _Scatter.pairs_done((ff2, ff1), (got[:1], got[1:]), "ff")
    mid = scatter(dict(w_att_up=d_att, w_ret_up=d_ret, w_out=d_out), "mid")
    (dqa, dkva, dsink, dret, dgain), got = _seq_bwd(
        qt, kv, kvt, doa, sinks, qkr, vr, gr, ret, dyrp, states, gain, rtabs, (ca, sa, sb, cr, sr, ct, st),
        _join(_join(ff2.chip_comm(), ff1.chip_comm()), mid.pair_comm()))
    ff2.chip_done(got[:1])
    ff1.chip_done(got[1:2])
    mid.pair_done(got[2:])
    (d_in, d_in_bf), r2 = _dw_in_t((dqa, dkva, dret, dgates), (True, False, False, False), xn1, mid.chip_comm())
    mid.chip_done(r2)
    win = scatter(dict(w_in=d_in), "in", sent=dict(w_in=d_in_bf))
    win.pair_done(_comm_alone(win.pair_comm(), "pair_exchange_in"))
    (dx, dg1), r2 = _inproj_bwd(dqa, dkva, dret, dgates, dh1, x, g1, w_in, win.chip_comm())
    win.chip_done(r2)
    small = dict(norm_mix_gain=dg1, b_gates=db, attn_sinks=dsink, ret_gn_gain=dgain, norm_mlp_gain=dg2,
                 norm_final_gain=dg3)
    return loss, dx, small, (win, mid, ff1, ff2)


def _coords():
    return lax.axis_index("x"), lax.axis_index("y"), lax.axis_index("c")


def _flip(v, bit):
    return 1 - v if bit else v


def _xor(a, b):
    return a + b - 2 * a * b


def _gather_relay_comm(shards):
    n = len(shards)

    def parts(ins, outs, sems):
        send_sems, recv_sems, local_sems = sems
        x, y, c = _coords()
        me, sib = (x, y, c), (x, y, 1 - c)

        def ring(core):
            return ((_xor(x, core), _xor(y, 1 - core), core), (_xor(x, 1 - core), _xor(y, core), core),
                    (1 - x, 1 - y, core))

        def slot(a, blk):
            return outs[a].at[4 * blk[0] + 2 * blk[1] + blk[2]]

        def copy(a, k, blk, to, src=None):
            return pltpu.make_async_remote_copy(
                src_ref=slot(a, blk) if src is None else src, dst_ref=slot(a, blk),
                send_sem=send_sems.at[a, k], recv_sem=recv_sems.at[a, k], device_id=to, device_id_type=MESH)

        mine = [pltpu.make_async_copy(ins[a], slot(a, me), local_sems.at[a]) for a in range(n)]
        return me, sib, ring, copy, mine

    def first_copies(ins, outs, sems):
        me, sib, ring, copy, mine = parts(ins, outs, sems)
        src, dst, _ = ring(me[2])
        return mine, [copy(a, k, me, to, src=ins[a]) for a in range(n) for k, to in ((1, src), (2, dst), (0, sib))]

    def start(ins, outs, sems):
        mine, first = first_copies(ins, outs, sems)
        for cp in mine + first:
            cp.start()

    def relay(ins, outs, sems):
        me, sib, ring, copy, _ = parts(ins, outs, sems)
        src, dst, _ = ring(me[2])
        for a in range(n):
            copy(a, 1, src, me).wait_recv()
            copy(a, 3, src, dst).start()
            copy(a, 4, src, sib).start()
        for a in range(n):
            copy(a, 2, dst, me).wait_recv()
            copy(a, 5, dst, sib).start()

    def finish(ins, outs, sems):
        me, sib, ring, copy, _ = parts(ins, outs, sems)
        mine, first = first_copies(ins, outs, sems)
        src, dst, diag = ring(me[2])
        passed = [copy(a, k, blk, to) for a in range(n) for k, blk, to in ((3, src, dst), (4, src, sib), (5, dst, sib))]
        for a in range(n):
            copy(a, 3, diag, me).wait_recv()
            fwd = copy(a, 6, diag, sib)
            fwd.start()
            passed.append(fwd)
        s_src, s_dst, s_diag = ring(sib[2])
        for a in range(n):
            for k, blk in ((0, sib), (4, s_src), (5, s_dst), (6, s_diag)):
                copy(a, k, blk, me).wait_recv()
        for cp in first + passed:
            cp.wait_send()
        for cp in mine:
            cp.wait()

    return _Comm(list(shards), [jax.ShapeDtypeStruct((N_DEV,) + s.shape, s.dtype) for s in shards],
                 [pltpu.SemaphoreType.DMA((n, 7)), pltpu.SemaphoreType.DMA((n, 7)), pltpu.SemaphoreType.DMA((n,))],
                 start, finish, relay)


COLUMN_SHARDED = ("w_in", "w_ff1")
COLUMN_RELAID = ("w_att_up",)


def _gather(shards, names):
    def unpack(got):
        return [jnp.transpose(g, (1, 0, 2)).reshape(g.shape[1], N_DEV * g.shape[2]) if k in COLUMN_RELAID
                else g.reshape(N_DEV * g.shape[1], g.shape[2]) for k, g in zip(names, got)]

    return _gather_relay_comm([shards[k] for k in names]), unpack


def _pair_comm(grads):
    n = len(grads)

    def copies(g, r1, sems):
        send_sems, recv_sems = sems
        x, y, c = _coords()
        return [pltpu.make_async_remote_copy(
            src_ref=g[a].at[2 * j + (1 - c)], dst_ref=r1[a].at[j], send_sem=send_sems.at[a, j],
            recv_sem=recv_sems.at[a, j], device_id=(x, y, 1 - c), device_id_type=MESH)
            for a in range(n) for j in range(4)]

    def start(g, r1, sems):
        for cp in copies(g, r1, sems):
            cp.start()

    def finish(g, r1, sems):
        for cp in copies(g, r1, sems):
            cp.wait_recv()
        for cp in copies(g, r1, sems):
            cp.wait_send()

    return _Comm(list(grads), [jax.ShapeDtypeStruct((4,) + g.shape[1:], g.dtype) for g in grads],
                 [pltpu.SemaphoreType.DMA((n, 4)), pltpu.SemaphoreType.DMA((n, 4))], start, finish)


def _small_comm(small):
    def parts(ins, outs, sems):
        (small_ref,), (small_all,) = ins, outs
        ssend, srecv, lsem = sems
        x, y, c = _coords()
        me_idx = 4 * x + 2 * y + c
        own = pltpu.make_async_copy(small_ref, small_all.at[me_idx], lsem)
        sends, recvs = [], []
        for r in range(1, N_DEV):
            px, py, pc = _flip(x, r & 4), _flip(y, r & 2), _flip(c, r & 1)
            sends.append(pltpu.make_async_remote_copy(
                src_ref=small_ref, dst_ref=small_all.at[me_idx], send_sem=ssend.at[r - 1], recv_sem=srecv.at[r - 1],
                device_id=(px, py, pc), device_id_type=MESH))
            recvs.append(pltpu.make_async_remote_copy(
                src_ref=small_ref, dst_ref=small_all.at[4 * px + 2 * py + pc], send_sem=ssend.at[r - 1],
                recv_sem=srecv.at[r - 1], device_id=(px, py, pc), device_id_type=MESH))
        return own, sends, recvs

    def start(ins, outs, sems):
        own, sends, _ = parts(ins, outs, sems)
        own.start()
        for cp in sends:
            cp.start()

    def finish(ins, outs, sems):
        own, sends, recvs = parts(ins, outs, sems)
        for cp in recvs:
            cp.wait_recv()
        for cp in sends:
            cp.wait_send()
        own.wait()

    return _Comm([small], [jax.ShapeDtypeStruct((N_DEV,) + small.shape, small.dtype)],
                 [pltpu.SemaphoreType.DMA((N_DEV - 1,)), pltpu.SemaphoreType.DMA((N_DEV - 1,)),
                  pltpu.SemaphoreType.DMA], start, finish)


def _pair_sum(grads, r1s, c_arr, tag):
    n = len(grads)
    q = 1

    def body(c_ref, *refs):
        g, r, t = refs[:n], refs[n:2 * n], refs[2 * n:]
        for a in range(n):
            t[a][...] = (g[a][...] + r[a][...]).astype(t[a].dtype)

    def blk(arr):
        return (1, arr.shape[1] // q, arr.shape[2])

    grid_spec = pltpu.PrefetchScalarGridSpec(
        num_scalar_prefetch=1, grid=(4, q),
        in_specs=[pl.BlockSpec(blk(g), lambda j, s, c_ref: (2 * j + c_ref[0], s, 0)) for g in grads]
        + [pl.BlockSpec(blk(r), lambda j, s, c_ref: (j, s, 0)) for r in r1s],
        out_specs=[pl.BlockSpec(blk(r), lambda j, s, c_ref: (j, s, 0)) for r in r1s])
    return pl.pallas_call(
        body, name="pair_sum_" + tag, grid_spec=grid_spec,
        out_shape=[jax.ShapeDtypeStruct(r.shape, RS_PAYLOAD) for r in r1s],
        compiler_params=_params(2),
    )(c_arr, *grads, *r1s)


def _chip_comm(ts):
    n = len(ts)

    def copies(t, r2, sems):
        send_sems, recv_sems = sems
        x, y, c = _coords()
        out = []
        for a in range(n):
            for r in range(1, 4):
                tx, ty = _flip(x, r & 2), _flip(y, r & 1)
                out.append(pltpu.make_async_remote_copy(
                    src_ref=t[a].at[2 * tx + ty], dst_ref=r2[a].at[r - 1], send_sem=send_sems.at[a, r - 1],
                    recv_sem=recv_sems.at[a, r - 1], device_id=(tx, ty, c), device_id_type=MESH))
        return out

    def start(t, r2, sems):
        for cp in copies(t, r2, sems):
            cp.start()

    def finish(t, r2, sems):
        for cp in copies(t, r2, sems):
            cp.wait_recv()
        for cp in copies(t, r2, sems):
            cp.wait_send()

    return _Comm(list(ts), [jax.ShapeDtypeStruct((3,) + t.shape[1:], t.dtype) for t in ts],
                 [pltpu.SemaphoreType.DMA((n, 3)), pltpu.SemaphoreType.DMA((n, 3))], start, finish)


class _Scatter:
    def __init__(self, grads, tag, sent=None):
        def stack(k, g):
            if k in COLUMN_RELAID:
                return jnp.transpose(g.reshape(g.shape[0], N_DEV, g.shape[1] // N_DEV), (1, 0, 2))
            return g.reshape(N_DEV, g.shape[0] // N_DEV, g.shape[1])

        self.names, self.tag = tuple(grads), tag
        self.stacks = [stack(k, g) for k, g in grads.items()]
        self.sent = self.stacks if sent is None else [stack(k, sent[k]) for k in grads]

    def pair_comm(self):
        return _pair_comm(self.sent)

    def pair_done(self, r1s):
        _Scatter.pairs_done((self,), (r1s,), self.tag)

    @staticmethod
    def pairs_done(groups, r1s_of, tag):
        for grp, r1s in zip(groups, r1s_of):
            grp.r1s = list(r1s)
        c_arr = jnp.reshape(lax.axis_index("c"), (1,)).astype(jnp.int32)
        ts = _pair_sum([s for grp in groups for s in grp.stacks], [r for grp in groups for r in grp.r1s], c_arr, tag)
        for grp in groups:
            grp.ts, ts = ts[:len(grp.stacks)], ts[len(grp.stacks):]

    def chip_comm(self):
        return _chip_comm(self.ts)

    def chip_done(self, r2s):
        self.r2s = list(r2s)


def _adamw(w, g, m, v):
    m = ADAM_B1 * m + (1.0 - ADAM_B1) * g
    v = ADAM_B2 * v + (1.0 - ADAM_B2) * jnp.square(g)
    m_hat = m / (1.0 - ADAM_B1 ** ADAM_STEP)
    v_hat = v / (1.0 - ADAM_B2 ** ADAM_STEP)
    delta = -ADAM_LR * (m_hat / (jnp.sqrt(v_hat) + ADAM_EPS) + ADAM_WD * w)
    return delta, m, v


ADAM_STEPS = 4


def _piece_specs(stacks):
    def rows(s):
        return s.shape[1] // ADAM_STEPS
    return ([pl.BlockSpec((1, rows(s), s.shape[2]), lambda i, idx_ref: (idx_ref[0], i, 0)) for s in stacks]
            + [pl.BlockSpec((1, rows(s), s.shape[2]), lambda i, idx_ref: (idx_ref[1], i, 0)) for s in stacks]
            + [pl.BlockSpec((3, rows(s), s.shape[2]), lambda i, idx_ref: (0, i, 0)) for s in stacks])


def _piece_sum(g0, r1, r2):
    return (((g0[0] + r1[0]) + r2[0].astype(F32)) + r2[1].astype(F32)) + r2[2].astype(F32)


def _shard_sum(stacks, r1s, r2s, idx_arr):
    n = len(stacks)

    def body(idx_ref, *refs):
        g0, r1, r2, outs = (refs[k * n:(k + 1) * n] for k in range(4))
        for a in range(n):
            outs[a][...] = _piece_sum(g0[a], r1[a], r2[a])

    out_specs = [pl.BlockSpec((s.shape[1] // ADAM_STEPS, s.shape[2]), lambda i, idx_ref: (i, 0)) for s in stacks]
    grid_spec = pltpu.PrefetchScalarGridSpec(num_scalar_prefetch=1, grid=(ADAM_STEPS,),
                                             in_specs=_piece_specs(stacks), out_specs=out_specs)
    return pl.pallas_call(
        body, name="shard_sum", grid_spec=grid_spec,
        out_shape=[jax.ShapeDtypeStruct(s.shape[1:], F32) for s in stacks],
        compiler_params=_params(1),
    )(idx_arr, *stacks, *r1s, *r2s)


def _adam_big(pieces, summed, ws, ms, vs, idx_arr):
    stacks, r1s, r2s = pieces
    n_p, n = len(stacks), len(ws)

    def body(idx_ref, *refs):
        it = iter(refs)
        g0, r1, r2, gs, w, m, v = ([next(it) for _ in range(k)] for k in (n_p, n_p, n_p, n - n_p, n, n, n))
        outs = list(it)
        for a in range(n):
            g = _piece_sum(g0[a], r1[a], r2[a]) if a < n_p else gs[a - n_p][...]
            delta, nm, nv = _adamw(w[a][...], g, m[a][...], v[a][...])
            outs[4 * a][...] = g
            outs[4 * a + 1][...] = delta
            outs[4 * a + 2][...] = nm
            outs[4 * a + 3][...] = nv

    def shard_spec(w):
        return pl.BlockSpec((w.shape[0] // ADAM_STEPS, w.shape[1]), lambda i, idx_ref: (i, 0))

    in_specs = _piece_specs(stacks) + [shard_spec(w) for w in ws[n_p:]] + [shard_spec(w) for w in ws] * 3
    out_specs = [shard_spec(w) for w in ws for _ in range(4)]
    grid_spec = pltpu.PrefetchScalarGridSpec(num_scalar_prefetch=1, grid=(ADAM_STEPS,), in_specs=in_specs,
                                             out_specs=out_specs)
    return pl.pallas_call(
        body, name="adam_big", grid_spec=grid_spec,
        out_shape=[jax.ShapeDtypeStruct(w.shape, F32) for w in ws for _ in range(4)],
        compiler_params=_params(1),
    )(idx_arr, *stacks, *r1s, *r2s, *summed, *ws, *ms, *vs)


def _adam_small(small_all, w, m, v):
    def body(all_ref, w_ref, m_ref, v_ref, g_ref, d_ref, nm_ref, nv_ref):
        g = all_ref[0]
        for k in range(1, N_DEV):
            g = g + all_ref[k]
        delta, nm, nv = _adamw(w_ref[...], g, m_ref[...], v_ref[...])
        g_ref[...] = g
        d_ref[...] = delta
        nm_ref[...] = nm
        nv_ref[...] = nv

    return pl.pallas_call(
        body, name="adam_small",
        in_specs=[VMEM_FULL] * 4, out_specs=[VMEM_FULL] * 4,
        out_shape=[jax.ShapeDtypeStruct(w.shape, F32)] * 4,
    )(small_all, w, m, v)


RS_PAYLOAD = BF


def _pack_rows(rows):
    rid = lax.broadcasted_iota(jnp.int32, (8, D_MODEL), 0)
    out = jnp.zeros((8, D_MODEL), F32)
    for i, r in enumerate(rows):
        out = jnp.where(rid == i, jnp.broadcast_to(r, (8, D_MODEL)), out)
    return out


def _small_rows(norm_mix_gain, b_gates, attn_sinks, ret_gn_gain, norm_mlp_gain, norm_final_gain):
    return [norm_mix_gain, b_gates[:, :D_MODEL], b_gates[:, D_MODEL:], ret_gn_gain, norm_mlp_gain,
            norm_final_gain.reshape(1, D_MODEL), jnp.pad(attn_sinks, ((0, 0), (0, D_MODEL - ATT_HEADS)))]


def _unpack_small(p):
    return dict(norm_mix_gain=p[0:1], b_gates=jnp.concatenate([p[1:2], p[2:3]], axis=1), ret_gn_gain=p[3:4],
                norm_mlp_gain=p[4:5], norm_final_gain=p[5], attn_sinks=p[6:7, :ATT_HEADS])


WEIGHTS = ("norm_mix_gain", "w_in", "b_gates", "attn_sinks", "ret_gn_gain", "w_att_up", "w_ret_up", "w_out",
           "norm_mlp_gain", "w_ff1", "w_ff2", "norm_final_gain")
BIG = ("w_in", "w_att_up", "w_ret_up", "w_out", "w_ff1", "w_ff2")


def kernel(x, norm_mix_gain, w_in, b_gates, attn_sinks, ret_gn_gain, w_att_up, w_ret_up, w_out, norm_mlp_gain, w_ff1, w_ff2, norm_final_gain, loss_target, m_norm_mix_gain, m_w_in, m_b_gates, m_attn_sinks, m_ret_gn_gain, m_w_att_up, m_w_ret_up, m_w_out, m_norm_mlp_gain, m_w_ff1, m_w_ff2, m_norm_final_gain, v_norm_mix_gain, v_w_in, v_b_gates, v_attn_sinks, v_ret_gn_gain, v_w_att_up, v_w_ret_up, v_w_out, v_norm_mlp_gain, v_w_ff1, v_w_ff2, v_norm_final_gain):
    w = dict(norm_mix_gain=norm_mix_gain, w_in=w_in, b_gates=b_gates, attn_sinks=attn_sinks, ret_gn_gain=ret_gn_gain,
             w_att_up=w_att_up, w_ret_up=w_ret_up, w_out=w_out, norm_mlp_gain=norm_mlp_gain, w_ff1=w_ff1,
             w_ff2=w_ff2, norm_final_gain=norm_final_gain)
    m = dict(norm_mix_gain=m_norm_mix_gain, w_in=m_w_in, b_gates=m_b_gates, attn_sinks=m_attn_sinks,
             ret_gn_gain=m_ret_gn_gain, w_att_up=m_w_att_up, w_ret_up=m_w_ret_up, w_out=m_w_out,
             norm_mlp_gain=m_norm_mlp_gain, w_ff1=m_w_ff1, w_ff2=m_w_ff2, norm_final_gain=m_norm_final_gain)
    v = dict(norm_mix_gain=v_norm_mix_gain, w_in=v_w_in, b_gates=v_b_gates, attn_sinks=v_attn_sinks,
             ret_gn_gain=v_ret_gn_gain, w_att_up=v_w_att_up, w_ret_up=v_w_ret_up, w_out=v_w_out,
             norm_mlp_gain=v_norm_mlp_gain, w_ff1=v_w_ff1, w_ff2=v_w_ff2, norm_final_gain=v_norm_final_gain)

    shards = {k: (w[k][0].T if k in COLUMN_SHARDED else w[k][0]).astype(BF) for k in BIG}
    loss_p, dx, small, groups = _local_step(
        x[0], loss_target[0], norm_mix_gain, b_gates, attn_sinks[0], ret_gn_gain, norm_mlp_gain,
        norm_final_gain.reshape(1, D_MODEL), shards)

    by_name = {}
    for grp in groups:
        for k, stack, r1, r2 in zip(grp.names, grp.stacks, grp.r1s, grp.r2s):
            by_name[k] = (stack, r1, r2)
    lane = lax.broadcasted_iota(jnp.int32, (1, D_MODEL), 1)
    sink_row = jnp.where(lane < ATT_HEADS, jnp.pad(small["attn_sinks"], ((0, 0), (0, D_MODEL - LANE))),
                         jnp.where(lane == ATT_HEADS, jnp.pad(loss_p, ((0, 0), (0, D_MODEL - LANE)), mode="edge"),
                                   0.0))
    small_pack = _pack_rows([small["norm_mix_gain"], small["b_gates"][:, :D_MODEL], small["b_gates"][:, D_MODEL:],
                             small["ret_gn_gain"], small["norm_mlp_gain"], small["norm_final_gain"], sink_row])
    xi, yi, ci = _coords()
    (small_all,) = _comm_alone(_small_comm(small_pack), "small_exchange")
    idx_arr = jnp.stack([4 * xi + 2 * yi + ci, 2 * xi + yi]).astype(jnp.int32)
    in_pieces = ["w_att_up", "w_ret_up", "w_out", "w_ff2", "w_in"]
    in_sum = ["w_ff1"]
    order = in_pieces + in_sum
    summed_t = _shard_sum(*[[by_name[k][i] for k in in_sum] for i in range(3)], idx_arr)

    def shard(tree, k):
        return tree[k][0].T if k == "w_in" else tree[k][0]

    adam_out = _adam_big([[by_name[k][i] for k in in_pieces] for i in range(3)], [g.T for g in summed_t],
                         [shard(w, k) for k in order], [shard(m, k) for k in order], [shard(v, k) for k in order],
                         idx_arr)
    big_out = [adam_out[4 * order.index(k) + i].T if k == "w_in" else adam_out[4 * order.index(k) + i]
               for k in BIG for i in range(4)]
    sm_g, sm_d, sm_m, sm_v = _adam_small(small_all, _pack_rows(_small_rows(*[w[k] for k in WEIGHTS if k not in BIG])),
                                         _pack_rows(_small_rows(*[m[k] for k in WEIGHTS if k not in BIG])),
                                         _pack_rows(_small_rows(*[v[k] for k in WEIGHTS if k not in BIG])))

    loss = sm_g[6, ATT_HEADS]
    kinds = []
    for idx, packed in enumerate((sm_g, sm_d, sm_m, sm_v)):
        out = _unpack_small(packed)
        for a, k in enumerate(BIG):
            out[k] = big_out[4 * a + idx][None]
        kinds.append(out)
    return (loss, dx[None], *[kinds[0][k] for k in WEIGHTS], *[kinds[1][k] for k in WEIGHTS],
            *[kinds[2][k] for k in WEIGHTS], *[kinds[3][k] for k in WEIGHTS])
```

```python
import functools
import math

import jax
import jax.numpy as jnp
from jax import lax
from jax.experimental import pallas as pl
from jax.experimental.pallas import tpu as pltpu

F32 = jnp.float32
BF = jnp.bfloat16
MESH = pl.DeviceIdType.MESH

D_MODEL = 1024
ATT_HEADS = 8
ATT_HEAD_DIM = 64
ATT_BLOCK = 128
ROPE_DIM = 16
ROPE_THETA = 500000.0
RET_HEADS = 4
RET_KEY_DIM = 128
RET_VAL_DIM = 256
RET_CHUNK = 128
RET_ROT_BASE = 10000.0
D_FF = 4096
NORM_EPS = 1e-6
GN_EPS = 1e-6
NEG_INF = -1e30
ATT_SCALE = ATT_HEAD_DIM ** -0.5
RET_SCALE = RET_KEY_DIM ** -0.5

C_QA, C_KA, C_VA, C_QR, C_KR, C_VR, C_GR, C_GATES, C_END = 0, 512, 640, 768, 1280, 1792, 2816, 3840, 5888

ADAM_LR = 0.001
ADAM_B1 = 0.9
ADAM_B2 = 0.999
ADAM_EPS = 1e-08
ADAM_WD = 0.01
ADAM_STEP = 10

N_DEV = 8
LANE = 128
VMEM_LIMIT = 56 * 1024 * 1024
TOKEN_TILE = 512
MLP_TOKEN_TILE = 256
TN_TOKEN_TILE = 4096
FF_CHUNKS = (0, 1024, 2048, 3072, 4096)
IN_CHUNKS_FWD = (C_QA, C_KA, C_QR, C_KR, C_VR, C_GR, C_GATES, C_END)
IN_CHUNKS_BWD = (C_QA, C_KA, C_QR, C_GATES, C_END)

RET_LOG_GAMMA = tuple(math.log1p(-(2.0 ** (-5.0 - h))) for h in range(RET_HEADS))
RET_CHUNK_DECAY = tuple(math.exp(RET_CHUNK * lg) for lg in RET_LOG_GAMMA)

VMEM_FULL = pl.BlockSpec(memory_space=pltpu.VMEM)
SMEM_FULL = pl.BlockSpec(memory_space=pltpu.SMEM)
ANY = pl.BlockSpec(memory_space=pl.ANY)


def _params(n_axes):
    return pltpu.CompilerParams(dimension_semantics=("arbitrary",) * n_axes, vmem_limit_bytes=VMEM_LIMIT)


def _nn(a, b):
    return jnp.dot(a, b, preferred_element_type=F32)


def _nt(a, b):
    return lax.dot_general(a, b, (((1,), (1,)), ((), ())), preferred_element_type=F32)


def _tn(a, b):
    return lax.dot_general(a, b, (((0,), (0,)), ((), ())), preferred_element_type=F32)


def _sigmoid(v):
    return 1.0 / (1.0 + jnp.exp(-v))


def _rows(tile, width):
    return pl.BlockSpec((tile, width), lambda i: (i, 0))


def _cols(height, tile):
    return pl.BlockSpec((height, tile), lambda i: (0, i))


class _Resident:
    def __init__(self, hbm_ref, vmem_ref, sems, bounds):
        self.hbm, self.vmem, self.sems, self.bounds = hbm_ref, vmem_ref, sems, bounds

    @staticmethod
    def scratch(w, bounds):
        return [pltpu.VMEM(w.shape, w.dtype), pltpu.SemaphoreType.DMA((len(bounds) - 1,))]

    @staticmethod
    def load(*weights):
        @pl.when(pl.program_id(0) == 0)
        def _():
            copies = [w._copy(c) for w in weights for c in range(len(w.bounds) - 1)]
            for cp in copies:
                cp.start()
            for cp in copies:
                cp.wait()

    def _rows(self, c):
        return pl.ds(self.bounds[c], self.bounds[c + 1] - self.bounds[c])

    def _copy(self, c):
        return pltpu.make_async_copy(self.hbm.at[self._rows(c)], self.vmem.at[self._rows(c)], self.sems.at[c])

    def chunk(self, c):
        return self.vmem.at[self._rows(c)]


class _Comm:
    def __init__(self, inputs, out_shapes, scratch, start, finish, relay=None):
        self.inputs, self.out_shapes, self.scratch = inputs, out_shapes, scratch
        self.start, self.finish, self.relay = start, finish, relay


RELAY_AT = 0.7


def _join(a, b):
    na_in, na_out, na_sem = len(a.inputs), len(a.out_shapes), len(a.scratch)

    def both(name):
        def run(ins, outs, sems):
            for part, args in ((a, (ins[:na_in], outs[:na_out], sems[:na_sem])),
                               (b, (ins[na_in:], outs[na_out:], sems[na_sem:]))):
                if getattr(part, name) is not None:
                    getattr(part, name)(*args)
        return run

    return _Comm(list(a.inputs) + list(b.inputs), list(a.out_shapes) + list(b.out_shapes),
                 list(a.scratch) + list(b.scratch), both("start"), both("finish"),
                 both("relay") if a.relay or b.relay else None)


def _call(body, args, comm=None, *, name, grid, in_specs, out_specs, out_shape, scratch_shapes=()):
    params = _params(len(grid))
    if comm is None:
        return pl.pallas_call(body, name=name, grid=grid, in_specs=in_specs, out_specs=out_specs, out_shape=out_shape,
                              scratch_shapes=scratch_shapes, compiler_params=params)(*args), ()
    single = not isinstance(out_specs, (list, tuple))
    out_specs_l = [out_specs] if single else list(out_specs)
    out_shape_l = [out_shape] if single else list(out_shape)
    n_in, n_out, n_scr = len(in_specs), len(out_specs_l), len(scratch_shapes)
    n_cin, n_cout = len(comm.inputs), len(comm.out_shapes)

    def hosted(*refs):
        it = iter(refs)
        ins, cin, outs, cout, scr = ([next(it) for _ in range(k)] for k in (n_in, n_cin, n_out, n_cout, n_scr))
        sems = list(it)
        ids = [pl.program_id(k) for k in range(len(grid))]
        first = functools.reduce(jnp.logical_and, [i == 0 for i in ids])
        last = functools.reduce(jnp.logical_and, [i == g - 1 for i, g in zip(ids, grid)])

        @pl.when(first)
        def _():
            comm.start(cin, cout, sems)

        if comm.relay is not None:
            at = [int(grid[0] * RELAY_AT)] + [0] * (len(grid) - 1)

            @pl.when(functools.reduce(jnp.logical_and, [i == v for i, v in zip(ids, at)]))
            def _():
                comm.relay(cin, cout, sems)

        body(*ins, *outs, *scr)

        @pl.when(last)
        def _():
            comm.finish(cin, cout, sems)

    res = pl.pallas_call(
        hosted, name=name, grid=grid, in_specs=list(in_specs) + [ANY] * n_cin,
        out_specs=out_specs_l + [ANY] * n_cout, out_shape=out_shape_l + list(comm.out_shapes),
        scratch_shapes=list(scratch_shapes) + list(comm.scratch), compiler_params=params)(*args, *comm.inputs)
    return (res[0] if single else res[:n_out]), res[n_out:]


def _comm_alone(comm, name):
    n_cin, n_cout = len(comm.inputs), len(comm.out_shapes)

    def body(*refs):
        cin, cout, sems = refs[:n_cin], refs[n_cin:n_cin + n_cout], refs[n_cin + n_cout:]
        comm.start(cin, cout, sems)
        if comm.relay is not None:
            comm.relay(cin, cout, sems)
        comm.finish(cin, cout, sems)

    return pl.pallas_call(body, name=name, in_specs=[ANY] * n_cin, out_specs=[ANY] * n_cout,
                          out_shape=list(comm.out_shapes), scratch_shapes=list(comm.scratch))(*comm.inputs)


def _slabs(v, fn):
    return jnp.concatenate([fn(v[:, LANE * j:LANE * (j + 1)]) for j in range(v.shape[1] // LANE)], axis=1)


def _rope_att(v, ca, sa, sb):
    return _slabs(v, lambda t: t * ca + pltpu.roll(t, LANE - 8, 1) * sa + pltpu.roll(t, 8, 1) * sb)


def _rope_att_t(v, ca, sa, sb):
    return _slabs(v, lambda t: t * ca + pltpu.roll(t * sa, 8, 1) + pltpu.roll(t * sb, LANE - 8, 1))


def _rope_att_rows(v, ct, st, sign):
    parts = []
    for h in range(v.shape[0] // ATT_HEAD_DIM):
        r0 = ATT_HEAD_DIM * h
        x1, x2 = v[r0:r0 + 8], v[r0 + 8:r0 + 16]
        parts += [x1 * ct - sign * (x2 * st), x2 * ct + sign * (x1 * st), v[r0 + 16:r0 + ATT_HEAD_DIM]]
    return jnp.concatenate(parts, axis=0)


def _rope_ret(v, cr, sr):
    return _slabs(v, lambda t: t * cr + pltpu.roll(t, 64, 1) * sr)


def _rope_ret_t(v, cr, sr):
    return _slabs(v, lambda t: t * cr + pltpu.roll(t * sr, 64, 1))


def _rope_lane_tables():
    def inv(dim, theta):
        return theta ** (-jnp.arange(0, dim, 2, dtype=F32) / dim)

    inv_a, inv_r = inv(ROPE_DIM, ROPE_THETA), inv(RET_KEY_DIM, RET_ROT_BASE)
    half = ROPE_DIM // 2
    zeros = jnp.zeros((ATT_HEAD_DIM - ROPE_DIM,), F32)
    freq64 = jnp.concatenate([inv_a, inv_a, zeros])
    lo64 = jnp.concatenate([-jnp.ones((half,), F32), jnp.zeros((half,), F32), zeros])
    hi64 = jnp.concatenate([jnp.zeros((half,), F32), jnp.ones((half,), F32), zeros])
    sign_r = jnp.concatenate([-jnp.ones((64,), F32), jnp.ones((64,), F32)])
    rows = [jnp.tile(freq64, 2), jnp.tile(lo64, 2), jnp.tile(hi64, 2), jnp.tile(inv_r, 2), sign_r]
    lanes = jnp.stack(rows + [jnp.zeros((LANE,), F32)] * (8 - len(rows)))
    return lanes, jnp.broadcast_to(inv_a[:, None], (half, LANE))


def _prologue(x, g1, comm):
    seq = x.shape[0]
    tm = min(TOKEN_TILE, seq)

    def body(x_ref, g_ref, lanes_ref, freq_rows_ref, xn_ref, ca_ref, sa_ref, sb_ref, cr_ref, sr_ref, ct_ref, st_ref):
        xf = x_ref[...]
        r = lax.rsqrt(jnp.mean(xf * xf, axis=-1, keepdims=True) + NORM_EPS)
        xn_ref[...] = (xf * r * g_ref[...]).astype(BF)
        row0 = pl.program_id(0) * tm
        pos = (row0 + lax.broadcasted_iota(jnp.int32, (tm, LANE), 0)).astype(F32)
        ang_a = pos * lanes_ref[0:1, :]
        sin_a = jnp.sin(ang_a)
        ca_ref[...] = jnp.cos(ang_a)
        sa_ref[...] = sin_a * lanes_ref[1:2, :]
        sb_ref[...] = sin_a * lanes_ref[2:3, :]
        ang_r = pos * lanes_ref[3:4, :]
        cr_ref[...] = jnp.cos(ang_r)
        sr_ref[...] = jnp.sin(ang_r) * lanes_ref[4:5, :]
        pos_t = (row0 + lax.broadcasted_iota(jnp.int32, (8, tm), 1)).astype(F32)
        ang_t = pos_t * jnp.concatenate([freq_rows_ref[...]] * (tm // LANE), axis=1)
        ct_ref[...] = jnp.cos(ang_t)
        st_ref[...] = jnp.sin(ang_t)

    tab = _rows(tm, LANE)
    tab_t = _cols(8, tm)
    return _call(
        body, (x, g1, *_rope_lane_tables()), comm, name="prologue", grid=(seq // tm,),
        in_specs=[_rows(tm, D_MODEL), VMEM_FULL, VMEM_FULL, VMEM_FULL],
        out_specs=[_rows(tm, D_MODEL), tab, tab, tab, tab, tab, tab_t, tab_t],
        out_shape=[jax.ShapeDtypeStruct((seq, D_MODEL), BF)] + [jax.ShapeDtypeStruct((seq, LANE), F32)] * 5
        + [jax.ShapeDtypeStruct((8, seq), F32)] * 2)


def _ret_tables():
    c = RET_CHUNK
    lg = jnp.asarray(RET_LOG_GAMMA, F32)
    idx = jnp.arange(c, dtype=F32)
    diff = idx[:, None] - idx[None, :]
    intra = jnp.where(diff >= 0, jnp.exp(jnp.maximum(diff, 0.0) * lg[:, None, None]), 0.0)
    qd = jnp.exp((idx + 1.0)[None, :] * lg[:, None])[..., None]
    kd = jnp.exp((c - 1.0 - idx)[None, :] * lg[:, None])[..., None]
    return intra, jnp.broadcast_to(qd, (RET_HEADS, c, RET_KEY_DIM)), jnp.broadcast_to(kd, (RET_HEADS, c, RET_KEY_DIM))


def _inproj_fwd(xn1, w_in, tabs, comm=None):
    seq = xn1.shape[0]
    tm = min(TOKEN_TILE, seq)

    def body(xn_ref, w_hbm, ca_ref, sa_ref, sb_ref, cr_ref, sr_ref, ct_ref, st_ref,
             qt_ref, kv_ref, kvt_ref, qkr_ref, vr_ref, gr_ref, gates_ref, w_vmem, w_sems):
        w = _Resident(w_hbm, w_vmem, w_sems, IN_CHUNKS_FWD)
        _Resident.load(w)
        ca, sa, sb, cr, sr, ct, st = (ref[...] for ref in (ca_ref, sa_ref, sb_ref, cr_ref, sr_ref, ct_ref, st_ref))
        xn = xn_ref[...]
        qt_ref[...] = _rope_att_rows(_nt(w.chunk(0)[...], xn), ct, st, 1.0).astype(BF)
        w_kv = w.chunk(1)
        kvt = _nt(w_kv[...], xn)
        kvt_ref[:LANE, :] = _rope_att_rows(kvt[:LANE], ct, st, 1.0).astype(BF)
        kvt_ref[LANE:, :] = kvt[LANE:].astype(BF)
        kvn = _nt(xn, w_kv[...])
        kv_ref[:, :LANE] = _rope_att(kvn[:, :LANE], ca, sa, sb).astype(BF)
        kv_ref[:, LANE:] = kvn[:, LANE:].astype(BF)
        qr = _rope_ret(_nt(xn, w.chunk(2)[...]), cr, sr)
        qkr_ref[:, :512] = qr.astype(BF)
        kr = _rope_ret(_nt(xn, w.chunk(3)[...]), cr, sr) * RET_SCALE
        qkr_ref[:, 512:] = kr.astype(BF)
        vr_ref[...] = _nt(xn, w.chunk(4)[...]).astype(BF)
        gr_ref[...] = _nt(xn, w.chunk(5)[...]).astype(BF)
        gates_ref[...] = _nt(xn, w.chunk(6)[...]).astype(BF)

    tab = _rows(tm, LANE)
    tab_t = _cols(8, tm)
    return _call(
        body, (xn1, w_in, *tabs), comm, name="inproj_fwd", grid=(seq // tm,),
        in_specs=[_rows(tm, D_MODEL), ANY, tab, tab, tab, tab, tab, tab_t, tab_t],
        out_specs=[_cols(512, tm), _rows(tm, 256), _cols(256, tm), _rows(tm, 1024),
                   _rows(tm, 1024), _rows(tm, 1024), _rows(tm, 2048)],
        out_shape=[jax.ShapeDtypeStruct(s, BF) for s in ((512, seq), (seq, 256), (256, seq),
                                                         (seq, 1024), (seq, 1024), (seq, 1024), (seq, 2048))],
        scratch_shapes=_Resident.scratch(w_in, IN_CHUNKS_FWD))


ATT_Q_PER_KV = ATT_HEADS // 2


def _att_group(ref, kv):
    d = ATT_HEAD_DIM
    g = jnp.concatenate([ref[d * (ATT_Q_PER_KV * kv + j):d * (ATT_Q_PER_KV * kv + j + 1), :]
                         for j in range(ATT_Q_PER_KV)], axis=1)
    z = jnp.zeros_like(g)
    return jnp.concatenate([g, z] if kv == 0 else [z, g], axis=0)


def _att_probs(kwin, qpad, sink_ref, kv, i):
    c = ATT_BLOCK
    kj = lax.broadcasted_iota(jnp.int32, (2 * c, c), 0)
    qi = lax.broadcasted_iota(jnp.int32, (2 * c, c), 1)
    allowed = (kj > qi) & (kj <= qi + c) & ((kj >= c) | (i > 0))
    allowed = jnp.concatenate([allowed] * ATT_Q_PER_KV, axis=1)
    s = jnp.where(allowed, _nn(kwin, qpad) * ATT_SCALE, NEG_INF)
    sk = jnp.concatenate([jnp.full((1, c), sink_ref[ATT_Q_PER_KV * kv + j], F32) for j in range(ATT_Q_PER_KV)], axis=1)
    m = jnp.maximum(jnp.max(s, axis=0, keepdims=True), sk)
    pe = jnp.exp(s - m)
    psink = jnp.exp(sk - m)
    inv = 1.0 / (jnp.sum(pe, axis=0, keepdims=True) + psink)
    return pe * inv, psink * inv


def _seq_fwd(qt, kv, kvt, sinks, qkr, vr, gr, gain, rtabs, comm=None):
    seq = kv.shape[0]
    c = ATT_BLOCK
    assert c == RET_CHUNK
    d = ATT_HEAD_DIM
    nb = seq // c
    dk, dv = RET_KEY_DIM, RET_VAL_DIM

    def body(sink_ref, qt_ref, kvc_ref, kvp_ref, kvtc_ref, kvtp_ref, qk_ref, v_ref, g_ref, gain_ref, intra_ref,
             qd_ref, kd_ref, ot_ref, ret_ref, yrp_ref, st_ref, state):
        i = pl.program_id(0)
        kwin = jnp.concatenate([kvp_ref[:, :LANE], kvc_ref[:, :LANE]], axis=0)
        vtwin = jnp.concatenate([kvtp_ref[LANE:, :], kvtc_ref[LANE:, :]], axis=1)
        for g in (0, 1):
            prob, _ = _att_probs(kwin, _att_group(qt_ref, g), sink_ref, g, i)
            out = _nn(vtwin[d * g:d * (g + 1), :], prob.astype(BF))
            for j in range(ATT_Q_PER_KV):
                h = ATT_Q_PER_KV * g + j
                ot_ref[d * h:d * (h + 1), :] = out[:, c * j:c * (j + 1)].astype(BF)

        @pl.when(i == 0)
        def _():
            state[...] = jnp.zeros_like(state)

        for h in range(RET_HEADS):
            qh = qk_ref[:, dk * h:dk * (h + 1)]
            kh = qk_ref[:, 512 + dk * h:512 + dk * (h + 1)]
            vh = v_ref[:, dv * h:dv * (h + 1)]
            sh = state[h]
            shb = sh.astype(BF)
            st_ref[0, h] = shb
            att = _nt(qh, kh) * intra_ref[h]
            inner = _nn(att.astype(BF), vh)
            cross = _nn((qh.astype(F32) * qd_ref[h]).astype(BF), shb)
            out = inner + cross
            state[h] = sh * RET_CHUNK_DECAY[h] + _tn((kh.astype(F32) * kd_ref[h]).astype(BF), vh)
            ret_ref[:, dv * h:dv * (h + 1)] = out
            mu = jnp.mean(out, axis=-1, keepdims=True)
            dev = out - mu
            var = jnp.mean(dev * dev, axis=-1, keepdims=True)
            y = dev * lax.rsqrt(var + GN_EPS) * gain_ref[:, dv * h:dv * (h + 1)]
            g = g_ref[:, dv * h:dv * (h + 1)].astype(F32)
            yrp_ref[:, dv * h:dv * (h + 1)] = (g * _sigmoid(g) * y).astype(BF)

    prev = lambda i: jnp.maximum(i - 1, 0)
    return _call(
        body, (sinks, qt, kv, kv, kvt, kvt, qkr, vr, gr, gain, *rtabs), comm, name="seq_fwd", grid=(nb,),
        in_specs=[SMEM_FULL, _cols(512, c), _rows(c, 256), pl.BlockSpec((c, 256), lambda i: (prev(i), 0)),
                  _cols(256, c), pl.BlockSpec((256, c), lambda i: (0, prev(i))),
                  _rows(c, 1024), _rows(c, 1024), _rows(c, 1024), VMEM_FULL, VMEM_FULL, VMEM_FULL, VMEM_FULL],
        out_specs=[_cols(512, c), _rows(c, 1024), _rows(c, 1024),
                   pl.BlockSpec((1, RET_HEADS, dk, dv), lambda i: (i, 0, 0, 0))],
        out_shape=[jax.ShapeDtypeStruct((512, seq), BF), jax.ShapeDtypeStruct((seq, 1024), F32),
                   jax.ShapeDtypeStruct((seq, 1024), BF), jax.ShapeDtypeStruct((nb, RET_HEADS, dk, dv), BF)],
        scratch_shapes=[pltpu.VMEM((RET_HEADS, dk, dv), F32)])


def _mix_fwd(oat, yrp, gates, bg, x, w_att, w_ret, w_out):
    seq = x.shape[0]
    tm = min(TOKEN_TILE, seq)

    def body(oat_ref, yrp_ref, gates_ref, bg_ref, x_ref, wa_ref, wr_ref, wo_ref, ya_ref, yr_ref, mg_ref, h1_ref):
        ya = _tn(oat_ref[...], wa_ref[...])
        yr = _nn(yrp_ref[...], wr_ref[...])
        gt = _sigmoid(gates_ref[...].astype(F32) + bg_ref[...])
        merged = (gt[:, :D_MODEL] * ya + gt[:, D_MODEL:] * yr).astype(BF)
        ya_ref[...] = ya.astype(BF)
        yr_ref[...] = yr.astype(BF)
        mg_ref[...] = merged
        h1_ref[...] = x_ref[...] + _nn(merged, wo_ref[...])

    return pl.pallas_call(
        body, name="mix_fwd", grid=(seq // tm,),
        in_specs=[_cols(512, tm), _rows(tm, 1024), _rows(tm, 2048), VMEM_FULL, _rows(tm, D_MODEL),
                  VMEM_FULL, VMEM_FULL, VMEM_FULL],
        out_specs=[_rows(tm, D_MODEL)] * 4,
        out_shape=[jax.ShapeDtypeStruct((seq, D_MODEL), BF)] * 3 + [jax.ShapeDtypeStruct((seq, D_MODEL), F32)],
        compiler_params=_params(1),
    )(oat, yrp, gates, bg, x, w_att, w_ret, w_out)


def _mlp_loss_step(h1, g2, g3, target, w_ff1, w_ff2):
    seq = h1.shape[0]
    tm = min(MLP_TOKEN_TILE, seq)
    n_chunks = len(FF_CHUNKS) - 1

    def body(h1_ref, g2_ref, g3_ref, t_ref, w1_hbm, w2_hbm,
             xn2_ref, hdn_ref, dh2_ref, du_ref, dh1_ref, loss_ref, dg3_ref, dg2_ref,
             w1_vmem, w1_sems, w2_vmem, w2_sems, relu_u):
        w1 = _Resident(w1_hbm, w1_vmem, w1_sems, FF_CHUNKS)
        w2 = _Resident(w2_hbm, w2_vmem, w2_sems, FF_CHUNKS)
        _Resident.load(w1, w2)

        @pl.when(pl.program_id(0) == 0)
        def _():
            loss_ref[...] = jnp.zeros_like(loss_ref)
            dg3_ref[...] = jnp.zeros_like(dg3_ref)
            dg2_ref[...] = jnp.zeros_like(dg2_ref)

        h1v = h1_ref[...]
        r2 = lax.rsqrt(jnp.mean(h1v * h1v, axis=-1, keepdims=True) + NORM_EPS)
        xn2 = (h1v * r2 * g2_ref[...]).astype(BF)
        xn2_ref[...] = xn2
        h2 = h1v
        for c in range(n_chunks):
            cols = slice(FF_CHUNKS[c], FF_CHUNKS[c + 1])
            a = jnp.maximum(_nt(xn2, w1.chunk(c)[...]), 0.0)
            relu_u[:, cols] = a.astype(BF)
            hdn = jnp.square(a).astype(BF)
            hdn_ref[:, cols] = hdn
            h2 = h2 + _nn(hdn, w2.chunk(c)[...])
        r3 = lax.rsqrt(jnp.mean(h2 * h2, axis=-1, keepdims=True) + NORM_EPS)
        hn = h2 * r3
        err = hn * g3_ref[...] - t_ref[...]
        loss_ref[...] += jnp.sum(err * err) * (0.5 / D_MODEL)
        dy = err * (1.0 / D_MODEL)
        dg3_ref[...] += jnp.sum(dy * hn, axis=0, keepdims=True)
        z = dy * g3_ref[...]
        dh2 = r3 * (z - hn * jnp.mean(z * hn, axis=-1, keepdims=True))
        dh2b = dh2.astype(BF)
        dh2_ref[...] = dh2b
        dxn2 = jnp.zeros_like(dh2)
        for c in range(n_chunks):
            cols = slice(FF_CHUNKS[c], FF_CHUNKS[c + 1])
            du = (_nt(dh2b, w2.chunk(c)[...]) * (2.0 * relu_u[:, cols].astype(F32))).astype(BF)
            du_ref[:, cols] = du
            dxn2 = dxn2 + _nn(du, w1.chunk(c)[...])
        dnorm, dg = _rms_bwd(dxn2, h1v, g2_ref[...])
        dh1_ref[...] = dh2 + dnorm
        dg2_ref[...] += dg

    gain_acc = pl.BlockSpec((1, D_MODEL), lambda i: (0, 0))
    return pl.pallas_call(
        body, name="mlp_loss_step", grid=(seq // tm,),
        in_specs=[_rows(tm, D_MODEL), VMEM_FULL, VMEM_FULL, _rows(tm, D_MODEL), ANY, ANY],
        out_specs=[_rows(tm, D_MODEL), _rows(tm, D_FF), _rows(tm, D_MODEL), _rows(tm, D_FF), _rows(tm, D_MODEL),
                   pl.BlockSpec((1, LANE), lambda i: (0, 0)), gain_acc, gain_acc],
        out_shape=[jax.ShapeDtypeStruct((seq, D_MODEL), BF), jax.ShapeDtypeStruct((seq, D_FF), BF),
                   jax.ShapeDtypeStruct((seq, D_MODEL), BF), jax.ShapeDtypeStruct((seq, D_FF), BF),
                   jax.ShapeDtypeStruct((seq, D_MODEL), F32), jax.ShapeDtypeStruct((1, LANE), F32),
                   jax.ShapeDtypeStruct((1, D_MODEL), F32), jax.ShapeDtypeStruct((1, D_MODEL), F32)],
        scratch_shapes=(_Resident.scratch(w_ff1, FF_CHUNKS) + _Resident.scratch(w_ff2, FF_CHUNKS)
                        + [pltpu.VMEM((tm, D_FF), BF)]),
        compiler_params=_params(1),
    )(h1, g2, g3, target, w_ff1, w_ff2)


def _rms_bwd(dxn, xin, gain):
    r = lax.rsqrt(jnp.mean(xin * xin, axis=-1, keepdims=True) + NORM_EPS)
    xhat = xin * r
    z = dxn * gain
    dxin = r * (z - xhat * jnp.mean(z * xhat, axis=-1, keepdims=True))
    return dxin, jnp.sum(dxn * xhat, axis=0, keepdims=True)


def _mm_tn(a, b, name, a_is_t=False, comm=None):
    kdim, seq = a.shape if a_is_t else a.shape[::-1]
    ndim = b.shape[1]
    ts = min(TN_TOKEN_TILE, seq)
    tk = min(kdim, 1024)
    tn = min(ndim, 1024)
    n_steps = seq // ts

    def body(a_ref, b_ref, o_ref):
        mm = _nn if a_is_t else _tn
        part = mm(a_ref[...].astype(BF), b_ref[...].astype(BF))
        if n_steps == 1:
            o_ref[...] = part
        else:
            @pl.when(pl.program_id(2) == 0)
            def _():
                o_ref[...] = jnp.zeros_like(o_ref)

            o_ref[...] += part

    a_spec = (pl.BlockSpec((tk, ts), lambda k, n, s: (k, s)) if a_is_t
              else pl.BlockSpec((ts, tk), lambda k, n, s: (s, k)))
    res = _call(
        body, (a, b), comm, name=name, grid=(kdim // tk, ndim // tn, n_steps),
        in_specs=[a_spec, pl.BlockSpec((ts, tn), lambda k, n, s: (s, n))],
        out_specs=pl.BlockSpec((tk, tn), lambda k, n, s: (k, n)),
        out_shape=jax.ShapeDtypeStruct((kdim, ndim), F32))
    return res[0] if comm is None else res


def _dw_in_t(pieces, is_t, xn1, comm=None):
    seq = xn1.shape[0]
    tr = 256
    n_blk = [(p.shape[0] if t else p.shape[1]) // tr for p, t in zip(pieces, is_t)]
    offs = [sum(n_blk[:i]) for i in range(len(pieces) + 1)]
    n_p = len(pieces)

    def body(*refs):
        piece_refs, xn_ref, o_ref, ob_ref = refs[:n_p], refs[n_p], refs[n_p + 1], refs[n_p + 2]
        r = pl.program_id(0)
        for p in range(n_p):
            @pl.when((r >= offs[p]) & (r < offs[p + 1]))
            def _(p=p):
                res = (_nn if is_t[p] else _tn)(piece_refs[p][...], xn_ref[...])
                o_ref[...] = res
                ob_ref[...] = res.astype(BF)

    def piece_spec(p):
        def index(r):
            blk = jnp.clip(r - offs[p], 0, n_blk[p] - 1)
            return (blk, 0) if is_t[p] else (0, blk)
        return pl.BlockSpec((tr, seq) if is_t[p] else (seq, tr), index)

    return _call(
        body, (*pieces, xn1), comm, name="dw_in", grid=(offs[-1],),
        in_specs=[piece_spec(p) for p in range(n_p)] + [VMEM_FULL],
        out_specs=[_rows(tr, D_MODEL), _rows(tr, D_MODEL)],
        out_shape=[jax.ShapeDtypeStruct((offs[-1] * tr, D_MODEL), F32),
                   jax.ShapeDtypeStruct((offs[-1] * tr, D_MODEL), BF)])


def _mix_bwd(dh1, ya, yr, gates, bg, merged, oat, yrp, w_att, w_ret, w_out, comm=None):
    seq = dh1.shape[0]
    tm = min(TOKEN_TILE, seq)

    def body(dh1_ref, ya_ref, yr_ref, gates_ref, bg_ref, mg_ref, oat_ref, yrp_ref, wa_ref, wr_ref, wo_ref,
             dgates_ref, doa_ref, dyrp_ref, db_ref, dwa_ref, dwr_ref, dwo_ref):
        @pl.when(pl.program_id(0) == 0)
        def _():
            for ref in (db_ref, dwa_ref, dwr_ref, dwo_ref):
                ref[...] = jnp.zeros_like(ref)

        dh1b = dh1_ref[...].astype(BF)
        dm = _nt(dh1b, wo_ref[...])
        gt = _sigmoid(gates_ref[...].astype(F32) + bg_ref[...])
        ga, gr = gt[:, :D_MODEL], gt[:, D_MODEL:]
        dya = (dm * ga).astype(BF)
        dyr = (dm * gr).astype(BF)
        dga = dm * ya_ref[...].astype(F32) * ga * (1.0 - ga)
        dgr = dm * yr_ref[...].astype(F32) * gr * (1.0 - gr)
        dgates_ref[:, :D_MODEL] = dga.astype(BF)
        dgates_ref[:, D_MODEL:] = dgr.astype(BF)
        db_ref[:, :D_MODEL] += jnp.sum(dga, axis=0, keepdims=True)
        db_ref[:, D_MODEL:] += jnp.sum(dgr, axis=0, keepdims=True)
        doa_ref[...] = _nt(wa_ref[...], dya).astype(BF)
        dyrp_ref[...] = _nt(dyr, wr_ref[...]).astype(BF)
        dwo_ref[...] += _tn(mg_ref[...], dh1b)
        dwr_ref[...] += _tn(yrp_ref[...], dyr)
        dwa_ref[...] += _nn(oat_ref[...], dya)

    acc = lambda r, c: pl.BlockSpec((r, c), lambda i: (0, 0))
    return _call(
        body, (dh1, ya, yr, gates, bg, merged, oat, yrp, w_att, w_ret, w_out), comm, name="mix_bwd",
        grid=(seq // tm,),
        in_specs=[_rows(tm, D_MODEL), _rows(tm, D_MODEL), _rows(tm, D_MODEL), _rows(tm, 2048), VMEM_FULL,
                  _rows(tm, D_MODEL), _cols(512, tm), _rows(tm, 1024), VMEM_FULL, VMEM_FULL, VMEM_FULL],
        out_specs=[_rows(tm, 2048), _cols(512, tm), _rows(tm, 1024), acc(1, 2048), acc(512, D_MODEL),
                   acc(1024, D_MODEL), acc(D_MODEL, D_MODEL)],
        out_shape=[jax.ShapeDtypeStruct((seq, 2048), BF), jax.ShapeDtypeStruct((512, seq), BF),
                   jax.ShapeDtypeStruct((seq, 1024), BF), jax.ShapeDtypeStruct((1, 2048), F32),
                   jax.ShapeDtypeStruct((512, D_MODEL), F32), jax.ShapeDtypeStruct((1024, D_MODEL), F32),
                   jax.ShapeDtypeStruct((D_MODEL, D_MODEL), F32)])


def _seq_bwd(qt, kv, kvt, dot, sinks, qkr, vr, gr, ret, dyrp, states, gain, rtabs, tabs, comm=None):
    seq = kv.shape[0]
    c = ATT_BLOCK
    assert c == RET_CHUNK
    d = ATT_HEAD_DIM
    nb = seq // c
    dk, dv = RET_KEY_DIM, RET_VAL_DIM
    ca, sa, sb, cr, sr, ct, st = tabs

    def body(sink_ref, qt_ref, kvc_ref, kvp_ref, kvtc_ref, kvtp_ref, dot_ref, ca_ref, sa_ref, sb_ref, ct_ref, st_ref,
             qk_ref, v_ref, g_ref, ret_ref, dyp_ref, st8_ref, gain_ref, intra_ref, qd_ref, kd_ref, cr_ref, sr_ref,
             dqt_ref, dkv_ref, dsink_ref, dall_ref, dgain_ref, carry, dstate):
        step = pl.program_id(0)
        i = nb - 1 - step

        @pl.when(step == 0)
        def _():
            carry[...] = jnp.zeros_like(carry)
            dsink_ref[...] = jnp.zeros_like(dsink_ref)
            dstate[...] = jnp.zeros_like(dstate)
            dgain_ref[...] = jnp.zeros_like(dgain_ref)

        kwin = jnp.concatenate([kvp_ref[:, :LANE], kvc_ref[:, :LANE]], axis=0)
        vwin = jnp.concatenate([kvp_ref[:, LANE:], kvc_ref[:, LANE:]], axis=0)
        ktwin = jnp.concatenate([kvtp_ref[:LANE, :], kvtc_ref[:LANE, :]], axis=1)
        ctv, stv = ct_ref[...], st_ref[...]
        lane1 = lax.broadcasted_iota(jnp.int32, (1, LANE), 1)
        dkw = jnp.zeros((2 * c, LANE), F32)
        dvw = jnp.zeros((2 * c, LANE), F32)
        dsink = jnp.zeros((1, LANE), F32)
        for g in (0, 1):
            qpad, dopad = _att_group(qt_ref, g), _att_group(dot_ref, g)
            prob, psink = _att_probs(kwin, qpad, sink_ref, g, i)
            dprob = _nn(vwin, dopad)
            drow = jnp.sum(dprob * prob, axis=0, keepdims=True)
            ds = (prob * (dprob - drow) * ATT_SCALE).astype(BF)
            dqg = _nn(ktwin[d * g:d * (g + 1), :], ds)
            dkw = dkw + _nt(ds, qpad)
            dvw = dvw + _nt(prob.astype(BF), dopad)
            dsink_lanes = psink * drow
            for j in range(ATT_Q_PER_KV):
                h = ATT_Q_PER_KV * g + j
                dqt_ref[d * h:d * (h + 1), :] = _rope_att_rows(dqg[:, c * j:c * (j + 1)], ctv, stv, -1.0).astype(BF)
                dsink = dsink + jnp.where(lane1 == h, -jnp.sum(dsink_lanes[:, c * j:c * (j + 1)]), 0.0)
        dsink_ref[...] += dsink
        total = carry[...] + jnp.concatenate([dkw[c:], dvw[c:]], axis=1)
        dkv_ref[:, :LANE] = _rope_att_t(total[:, :LANE], ca_ref[...], sa_ref[...], sb_ref[...]).astype(BF)
        dkv_ref[:, LANE:] = total[:, LANE:].astype(BF)
        carry[...] = jnp.concatenate([dkw[:c], dvw[:c]], axis=1)

        crv, srv = cr_ref[...], sr_ref[...]
        for h in range(RET_HEADS):
            vs = slice(dv * h, dv * (h + 1))
            qh = qk_ref[:, dk * h:dk * (h + 1)]
            kh = qk_ref[:, 512 + dk * h:512 + dk * (h + 1)]
            vh = v_ref[:, vs]
            out = ret_ref[:, vs]
            g = g_ref[:, vs].astype(F32)
            dyp = dyp_ref[:, vs].astype(F32)
            gain_h = gain_ref[:, vs]
            mu = jnp.mean(out, axis=-1, keepdims=True)
            dev = out - mu
            rstd = lax.rsqrt(jnp.mean(dev * dev, axis=-1, keepdims=True) + GN_EPS)
            yn = dev * rstd
            sg = _sigmoid(g)
            dg = dyp * (yn * gain_h) * (sg * (1.0 + g * (1.0 - sg)))
            dy = dyp * (g * sg)
            dgain_ref[:, vs] += jnp.sum(dy * yn, axis=0, keepdims=True)
            dyn = dy * gain_h
            dout = rstd * (dyn - jnp.mean(dyn, axis=-1, keepdims=True)
                           - yn * jnp.mean(dyn * yn, axis=-1, keepdims=True))
            doutb = dout.astype(BF)
            sc = st8_ref[0, h]
            dsp = dstate[h]
            dspb = dsp.astype(BF)
            intra, qdv, kdv = intra_ref[h], qd_ref[h], kd_ref[h]
            att = _nt(qh, kh) * intra
            dab = (_nt(doutb, vh) * intra).astype(BF)
            qdec = (qh.astype(F32) * qdv).astype(BF)
            kdec = (kh.astype(F32) * kdv).astype(BF)
            dq = _nn(dab, kh) + _nt(doutb, sc) * qdv
            dkk = _tn(dab, qh) + _nt(vh, dspb) * kdv
            dvv = _tn(att.astype(BF), doutb) + _nn(kdec, dspb)
            dstate[h] = dsp * RET_CHUNK_DECAY[h] + _tn(qdec, doutb)
            dall_ref[:, dk * h:dk * (h + 1)] = _rope_ret_t(dq, crv, srv).astype(BF)
            dall_ref[:, 512 + dk * h:512 + dk * (h + 1)] = (_rope_ret_t(dkk, crv, srv) * RET_SCALE).astype(BF)
            dall_ref[:, 1024 + dv * h:1024 + dv * (h + 1)] = dvv.astype(BF)
            dall_ref[:, 2048 + dv * h:2048 + dv * (h + 1)] = dg.astype(BF)

    cur = lambda s: nb - 1 - s
    prev = lambda s: jnp.maximum(nb - 2 - s, 0)
    rows = lambda w: pl.BlockSpec((c, w), lambda s: (cur(s), 0))
    cols = lambda h: pl.BlockSpec((h, c), lambda s: (0, cur(s)))
    acc = lambda w: pl.BlockSpec((1, w), lambda s: (0, 0))
    return _call(
        body, (sinks, qt, kv, kv, kvt, kvt, dot, ca, sa, sb, ct, st, qkr, vr, gr, ret, dyrp, states, gain, *rtabs,
               cr, sr), comm, name="seq_bwd", grid=(nb,),
        in_specs=[SMEM_FULL, cols(512), rows(256), pl.BlockSpec((c, 256), lambda s: (prev(s), 0)), cols(256),
                  pl.BlockSpec((256, c), lambda s: (0, prev(s))), cols(512), rows(LANE), rows(LANE), rows(LANE),
                  cols(8), cols(8), rows(1024), rows(1024), rows(1024), rows(1024), rows(1024),
                  pl.BlockSpec((1, RET_HEADS, dk, dv), lambda s: (cur(s), 0, 0, 0)),
                  VMEM_FULL, VMEM_FULL, VMEM_FULL, VMEM_FULL, rows(LANE), rows(LANE)],
        out_specs=[cols(512), rows(256), acc(LANE), rows(3072), acc(1024)],
        out_shape=[jax.ShapeDtypeStruct((512, seq), BF), jax.ShapeDtypeStruct((seq, 256), BF),
                   jax.ShapeDtypeStruct((1, LANE), F32), jax.ShapeDtypeStruct((seq, 3072), BF),
                   jax.ShapeDtypeStruct((1, 1024), F32)],
        scratch_shapes=[pltpu.VMEM((c, 256), F32), pltpu.VMEM((RET_HEADS, dk, dv), F32)])


def _inproj_bwd(dqa, dkva, dret, dgates, dh1, x, g1, w_in, comm=None):
    seq = x.shape[0]
    tm = min(TOKEN_TILE, seq)

    def body(dqa_ref, dkva_ref, dret_ref, dgates_ref, dh1_ref, x_ref, g_ref, w_hbm, dx_ref, dg1_ref, w_vmem, w_sems):
        w = _Resident(w_hbm, w_vmem, w_sems, IN_CHUNKS_BWD)
        _Resident.load(w)

        @pl.when(pl.program_id(0) == 0)
        def _():
            dg1_ref[...] = jnp.zeros_like(dg1_ref)

        dxn = _tn(dqa_ref[...], w.chunk(0)[...])
        dxn = dxn + _nn(dkva_ref[...], w.chunk(1)[...])
        dxn = dxn + _nn(dret_ref[...], w.chunk(2)[...])
        dxn = dxn + _nn(dgates_ref[...], w.chunk(3)[...])
        dnorm, dg = _rms_bwd(dxn, x_ref[...], g_ref[...])
        dx_ref[...] = dh1_ref[...] + dnorm
        dg1_ref[...] += dg

    return _call(
        body, (dqa, dkva, dret, dgates, dh1, x, g1, w_in), comm, name="inproj_bwd", grid=(seq // tm,),
        in_specs=[_cols(512, tm), _rows(tm, 256), _rows(tm, 3072), _rows(tm, 2048), _rows(tm, D_MODEL),
                  _rows(tm, D_MODEL), VMEM_FULL, ANY],
        out_specs=[_rows(tm, D_MODEL), pl.BlockSpec((1, D_MODEL), lambda i: (0, 0))],
        out_shape=[jax.ShapeDtypeStruct((seq, D_MODEL), F32), jax.ShapeDtypeStruct((1, D_MODEL), F32)],
        scratch_shapes=_Resident.scratch(w_in, IN_CHUNKS_BWD))


def _local_step(x, target, g1, bg, sinks, gain, g2, g3, shards):
    rtabs = _ret_tables()
    scatter = _Scatter

    comm, unpack = _gather(shards, ("w_in",))
    (xn1, ca, sa, sb, cr, sr, ct, st), got = _prologue(x, g1, comm)
    (w_in,) = unpack(got)
    comm_mid, unpack_mid = _gather(shards, ("w_att_up", "w_ret_up", "w_out"))
    comm_ff1, unpack_ff1 = _gather(shards, ("w_ff1",))
    (qt, kv, kvt, qkr, vr, gr, gates), got = _inproj_fwd(xn1, w_in, (ca, sa, sb, cr, sr, ct, st),
                                                         _join(comm_mid, comm_ff1))
    w_att, w_ret, w_out = unpack_mid(got[:3])
    (w_ff1,) = unpack_ff1(got[3:])
    comm, unpack = _gather(shards, ("w_ff2",))
    (oa, ret, yrp, states), got = _seq_fwd(qt, kv, kvt, sinks, qkr, vr, gr, gain, rtabs, comm)
    (w_ff2,) = unpack(got)
    ya, yr, merged, h1 = _mix_fwd(oa, yrp, gates, bg, x, w_att, w_ret, w_out)
    xn2, hdn, dh2, du, dh1, loss, dg3, dg2 = _mlp_loss_step(h1, g2, g3, target, w_ff1, w_ff2)

    ff2 = scatter(dict(w_ff2=_mm_tn(hdn, dh2, "dw_ff2")), "ff2")
    ff1 = scatter(dict(w_ff1=_mm_tn(du, xn2, "dw_ff1")), "ff1")
    (dgates, doa, dyrp, db, d_att, d_ret, d_out), got = _mix_bwd(
        dh1, ya, yr, gates, bg, merged, oa, yrp, w_att, w_ret, w_out, _join(ff2.pair_comm(), ff1.pair_comm()))
    _Scatter.pairs_done((ff2, ff1), (got[:1], got[1:]), "ff")
    mid = scatter(dict(w_att_up=d_att, w_ret_up=d_ret, w_out=d_out), "mid")
    (dqa, dkva, dsink, dret, dgain), got = _seq_bwd(
        qt, kv, kvt, doa, sinks, qkr, vr, gr, ret, dyrp, states, gain, rtabs, (ca, sa, sb, cr, sr, ct, st),
        _join(_join(ff2.chip_comm(), ff1.chip_comm()), mid.pair_comm()))
    ff2.chip_done(got[:1])
    ff1.chip_done(got[1:2])
    mid.pair_done(got[2:])
    (d_in, d_in_bf), r2 = _dw_in_t((dqa, dkva, dret, dgates), (True, False, False, False), xn1, mid.chip_comm())
    mid.chip_done(r2)
    win = scatter(dict(w_in=d_in), "in", sent=dict(w_in=d_in_bf))
    win.pair_done(_comm_alone(win.pair_comm(), "pair_exchange_in"))
    (dx, dg1), r2 = _inproj_bwd(dqa, dkva, dret, dgates, dh1, x, g1, w_in, win.chip_comm())
    win.chip_done(r2)
    small = dict(norm_mix_gain=dg1, b_gates=db, attn_sinks=dsink, ret_gn_gain=dgain, norm_mlp_gain=dg2,
                 norm_final_gain=dg3)
    return loss, dx, small, (win, mid, ff1, ff2)


def _coords():
    return lax.axis_index("x"), lax.axis_index("y"), lax.axis_index("c")


def _flip(v, bit):
    return 1 - v if bit else v


def _xor(a, b):
    return a + b - 2 * a * b


def _gather_relay_comm(shards):
    n = len(shards)

    def parts(ins, outs, sems):
        send_sems, recv_sems, local_sems = sems
        x, y, c = _coords()
        me, sib = (x, y, c), (x, y, 1 - c)

        def ring(core):
            return ((_xor(x, core), _xor(y, 1 - core), core), (_xor(x, 1 - core), _xor(y, core), core),
                    (1 - x, 1 - y, core))

        def slot(a, blk):
            return outs[a].at[4 * blk[0] + 2 * blk[1] + blk[2]]

        def copy(a, k, blk, to, src=None):
            return pltpu.make_async_remote_copy(
                src_ref=slot(a, blk) if src is None else src, dst_ref=slot(a, blk),
                send_sem=send_sems.at[a, k], recv_sem=recv_sems.at[a, k], device_id=to, device_id_type=MESH)

        mine = [pltpu.make_async_copy(ins[a], slot(a, me), local_sems.at[a]) for a in range(n)]
        return me, sib, ring, copy, mine

    def first_copies(ins, outs, sems):
        me, sib, ring, copy, mine = parts(ins, outs, sems)
        src, dst, _ = ring(me[2])
        return mine, [copy(a, k, me, to, src=ins[a]) for a in range(n) for k, to in ((1, src), (2, dst), (0, sib))]

    def start(ins, outs, sems):
        mine, first = first_copies(ins, outs, sems)
        for cp in mine + first:
            cp.start()

    def relay(ins, outs, sems):
        me, sib, ring, copy, _ = parts(ins, outs, sems)
        src, dst, _ = ring(me[2])
        for a in range(n):
            copy(a, 1, src, me).wait_recv()
            copy(a, 3, src, dst).start()
            copy(a, 4, src, sib).start()
        for a in range(n):
            copy(a, 2, dst, me).wait_recv()
            copy(a, 5, dst, sib).start()

    def finish(ins, outs, sems):
        me, sib, ring, copy, _ = parts(ins, outs, sems)
        mine, first = first_copies(ins, outs, sems)
        src, dst, diag = ring(me[2])
        passed = [copy(a, k, blk, to) for a in range(n) for k, blk, to in ((3, src, dst), (4, src, sib), (5, dst, sib))]
        for a in range(n):
            copy(a, 3, diag, me).wait_recv()
            fwd = copy(a, 6, diag, sib)
            fwd.start()
            passed.append(fwd)
        s_src, s_dst, s_diag = ring(sib[2])
        for a in range(n):
            for k, blk in ((0, sib), (4, s_src), (5, s_dst), (6, s_diag)):
                copy(a, k, blk, me).wait_recv()
        for cp in first + passed:
            cp.wait_send()
        for cp in mine:
            cp.wait()

    return _Comm(list(shards), [jax.ShapeDtypeStruct((N_DEV,) + s.shape, s.dtype) for s in shards],
                 [pltpu.SemaphoreType.DMA((n, 7)), pltpu.SemaphoreType.DMA((n, 7)), pltpu.SemaphoreType.DMA((n,))],
                 start, finish, relay)


COLUMN_SHARDED = ("w_in", "w_ff1")
COLUMN_RELAID = ("w_att_up",)


def _gather(shards, names):
    def unpack(got):
        return [jnp.transpose(g, (1, 0, 2)).reshape(g.shape[1], N_DEV * g.shape[2]) if k in COLUMN_RELAID
                else g.reshape(N_DEV * g.shape[1], g.shape[2]) for k, g in zip(names, got)]

    return _gather_relay_comm([shards[k] for k in names]), unpack


def _pair_comm(grads):
    n = len(grads)

    def copies(g, r1, sems):
        send_sems, recv_sems = sems
        x, y, c = _coords()
        return [pltpu.make_async_remote_copy(
            src_ref=g[a].at[2 * j + (1 - c)], dst_ref=r1[a].at[j], send_sem=send_sems.at[a, j],
            recv_sem=recv_sems.at[a, j], device_id=(x, y, 1 - c), device_id_type=MESH)
            for a in range(n) for j in range(4)]

    def start(g, r1, sems):
        for cp in copies(g, r1, sems):
            cp.start()

    def finish(g, r1, sems):
        for cp in copies(g, r1, sems):
            cp.wait_recv()
        for cp in copies(g, r1, sems):
            cp.wait_send()

    return _Comm(list(grads), [jax.ShapeDtypeStruct((4,) + g.shape[1:], g.dtype) for g in grads],
                 [pltpu.SemaphoreType.DMA((n, 4)), pltpu.SemaphoreType.DMA((n, 4))], start, finish)


def _small_comm(small):
    def parts(ins, outs, sems):
        (small_ref,), (small_all,) = ins, outs
        ssend, srecv, lsem = sems
        x, y, c = _coords()
        me_idx = 4 * x + 2 * y + c
        own = pltpu.make_async_copy(small_ref, small_all.at[me_idx], lsem)
        sends, recvs = [], []
        for r in range(1, N_DEV):
            px, py, pc = _flip(x, r & 4), _flip(y, r & 2), _flip(c, r & 1)
            sends.append(pltpu.make_async_remote_copy(
                src_ref=small_ref, dst_ref=small_all.at[me_idx], send_sem=ssend.at[r - 1], recv_sem=srecv.at[r - 1],
                device_id=(px, py, pc), device_id_type=MESH))
            recvs.append(pltpu.make_async_remote_copy(
                src_ref=small_ref, dst_ref=small_all.at[4 * px + 2 * py + pc], send_sem=ssend.at[r - 1],
                recv_sem=srecv.at[r - 1], device_id=(px, py, pc), device_id_type=MESH))
        return own, sends, recvs

    def start(ins, outs, sems):
        own, sends, _ = parts(ins, outs, sems)
        own.start()
        for cp in sends:
            cp.start()

    def finish(ins, outs, sems):
        own, sends, recvs = parts(ins, outs, sems)
        for cp in recvs:
            cp.wait_recv()
        for cp in sends:
            cp.wait_send()
        own.wait()

    return _Comm([small], [jax.ShapeDtypeStruct((N_DEV,) + small.shape, small.dtype)],
                 [pltpu.SemaphoreType.DMA((N_DEV - 1,)), pltpu.SemaphoreType.DMA((N_DEV - 1,)),
                  pltpu.SemaphoreType.DMA], start, finish)


def _pair_sum(grads, r1s, c_arr, tag):
    n = len(grads)
    q = 1

    def body(c_ref, *refs):
        g, r, t = refs[:n], refs[n:2 * n], refs[2 * n:]
        for a in range(n):
            t[a][...] = (g[a][...] + r[a][...]).astype(t[a].dtype)

    def blk(arr):
        return (1, arr.shape[1] // q, arr.shape[2])

    grid_spec = pltpu.PrefetchScalarGridSpec(
        num_scalar_prefetch=1, grid=(4, q),
        in_specs=[pl.BlockSpec(blk(g), lambda j, s, c_ref: (2 * j + c_ref[0], s, 0)) for g in grads]
        + [pl.BlockSpec(blk(r), lambda j, s, c_ref: (j, s, 0)) for r in r1s],
        out_specs=[pl.BlockSpec(blk(r), lambda j, s, c_ref: (j, s, 0)) for r in r1s])
    return pl.pallas_call(
        body, name="pair_sum_" + tag, grid_spec=grid_spec,
        out_shape=[jax.ShapeDtypeStruct(r.shape, RS_PAYLOAD) for r in r1s],
        compiler_params=_params(2),
    )(c_arr, *grads, *r1s)


def _chip_comm(ts):
    n = len(ts)

    def copies(t, r2, sems):
        send_sems, recv_sems = sems
        x, y, c = _coords()
        out = []
        for a in range(n):
            for r in range(1, 4):
                tx, ty = _flip(x, r & 2), _flip(y, r & 1)
                out.append(pltpu.make_async_remote_copy(
                    src_ref=t[a].at[2 * tx + ty], dst_ref=r2[a].at[r - 1], send_sem=send_sems.at[a, r - 1],
                    recv_sem=recv_sems.at[a, r - 1], device_id=(tx, ty, c), device_id_type=MESH))
        return out

    def start(t, r2, sems):
        for cp in copies(t, r2, sems):
            cp.start()

    def finish(t, r2, sems):
        for cp in copies(t, r2, sems):
            cp.wait_recv()
        for cp in copies(t, r2, sems):
            cp.wait_send()

    return _Comm(list(ts), [jax.ShapeDtypeStruct((3,) + t.shape[1:], t.dtype) for t in ts],
                 [pltpu.SemaphoreType.DMA((n, 3)), pltpu.SemaphoreType.DMA((n, 3))], start, finish)


class _Scatter:
    def __init__(self, grads, tag, sent=None):
        def stack(k, g):
            if k in COLUMN_RELAID:
                return jnp.transpose(g.reshape(g.shape[0], N_DEV, g.shape[1] // N_DEV), (1, 0, 2))
            return g.reshape(N_DEV, g.shape[0] // N_DEV, g.shape[1])

        self.names, self.tag = tuple(grads), tag
        self.stacks = [stack(k, g) for k, g in grads.items()]
        self.sent = self.stacks if sent is None else [stack(k, sent[k]) for k in grads]

    def pair_comm(self):
        return _pair_comm(self.sent)

    def pair_done(self, r1s):
        _Scatter.pairs_done((self,), (r1s,), self.tag)

    @staticmethod
    def pairs_done(groups, r1s_of, tag):
        for grp, r1s in zip(groups, r1s_of):
            grp.r1s = list(r1s)
        c_arr = jnp.reshape(lax.axis_index("c"), (1,)).astype(jnp.int32)
        ts = _pair_sum([s for grp in groups for s in grp.stacks], [r for grp in groups for r in grp.r1s], c_arr, tag)
        for grp in groups:
            grp.ts, ts = ts[:len(grp.stacks)], ts[len(grp.stacks):]

    def chip_comm(self):
        return _chip_comm(self.ts)

    def chip_done(self, r2s):
        self.r2s = list(r2s)


def _adamw(w, g, m, v):
    m = ADAM_B1 * m + (1.0 - ADAM_B1) * g
    v = ADAM_B2 * v + (1.0 - ADAM_B2) * jnp.square(g)
    m_hat = m / (1.0 - ADAM_B1 ** ADAM_STEP)
    v_hat = v / (1.0 - ADAM_B2 ** ADAM_STEP)
    delta = -ADAM_LR * (m_hat / (jnp.sqrt(v_hat) + ADAM_EPS) + ADAM_WD * w)
    return delta, m, v


ADAM_STEPS = 4


def _piece_specs(stacks):
    def rows(s):
        return s.shape[1] // ADAM_STEPS
    return ([pl.BlockSpec((1, rows(s), s.shape[2]), lambda i, idx_ref: (idx_ref[0], i, 0)) for s in stacks]
            + [pl.BlockSpec((1, rows(s), s.shape[2]), lambda i, idx_ref: (idx_ref[1], i, 0)) for s in stacks]
            + [pl.BlockSpec((3, rows(s), s.shape[2]), lambda i, idx_ref: (0, i, 0)) for s in stacks])


def _piece_sum(g0, r1, r2):
    return (((g0[0] + r1[0]) + r2[0].astype(F32)) + r2[1].astype(F32)) + r2[2].astype(F32)


def _shard_sum(stacks, r1s, r2s, idx_arr):
    n = len(stacks)

    def body(idx_ref, *refs):
        g0, r1, r2, outs = (refs[k * n:(k + 1) * n] for k in range(4))
        for a in range(n):
            outs[a][...] = _piece_sum(g0[a], r1[a], r2[a])

    out_specs = [pl.BlockSpec((s.shape[1] // ADAM_STEPS, s.shape[2]), lambda i, idx_ref: (i, 0)) for s in stacks]
    grid_spec = pltpu.PrefetchScalarGridSpec(num_scalar_prefetch=1, grid=(ADAM_STEPS,),
                                             in_specs=_piece_specs(stacks), out_specs=out_specs)
    return pl.pallas_call(
        body, name="shard_sum", grid_spec=grid_spec,
        out_shape=[jax.ShapeDtypeStruct(s.shape[1:], F32) for s in stacks],
        compiler_params=_params(1),
    )(idx_arr, *stacks, *r1s, *r2s)


def _adam_big(pieces, summed, ws, ms, vs, idx_arr):
    stacks, r1s, r2s = pieces
    n_p, n = len(stacks), len(ws)

    def body(idx_ref, *refs):
        it = iter(refs)
        g0, r1, r2, gs, w, m, v = ([next(it) for _ in range(k)] for k in (n_p, n_p, n_p, n - n_p, n, n, n))
        outs = list(it)
        for a in range(n):
            g = _piece_sum(g0[a], r1[a], r2[a]) if a < n_p else gs[a - n_p][...]
            delta, nm, nv = _adamw(w[a][...], g, m[a][...], v[a][...])
            outs[4 * a][...] = g
            outs[4 * a + 1][...] = delta
            outs[4 * a + 2][...] = nm
            outs[4 * a + 3][...] = nv

    def shard_spec(w):
        return pl.BlockSpec((w.shape[0] // ADAM_STEPS, w.shape[1]), lambda i, idx_ref: (i, 0))

    in_specs = _piece_specs(stacks) + [shard_spec(w) for w in ws[n_p:]] + [shard_spec(w) for w in ws] * 3
    out_specs = [shard_spec(w) for w in ws for _ in range(4)]
    grid_spec = pltpu.PrefetchScalarGridSpec(num_scalar_prefetch=1, grid=(ADAM_STEPS,), in_specs=in_specs,
                                             out_specs=out_specs)
    return pl.pallas_call(
        body, name="adam_big", grid_spec=grid_spec,
        out_shape=[jax.ShapeDtypeStruct(w.shape, F32) for w in ws for _ in range(4)],
        compiler_params=_params(1),
    )(idx_arr, *stacks, *r1s, *r2s, *summed, *ws, *ms, *vs)


def _adam_small(small_all, w, m, v):
    def body(all_ref, w_ref, m_ref, v_ref, g_ref, d_ref, nm_ref, nv_ref):
        g = all_ref[0]
        for k in range(1, N_DEV):
            g = g + all_ref[k]
        delta, nm, nv = _adamw(w_ref[...], g, m_ref[...], v_ref[...])
        g_ref[...] = g
        d_ref[...] = delta
        nm_ref[...] = nm
        nv_ref[...] = nv

    return pl.pallas_call(
        body, name="adam_small",
        in_specs=[VMEM_FULL] * 4, out_specs=[VMEM_FULL] * 4,
        out_shape=[jax.ShapeDtypeStruct(w.shape, F32)] * 4,
    )(small_all, w, m, v)


RS_PAYLOAD = BF


def _pack_rows(rows):
    rid = lax.broadcasted_iota(jnp.int32, (8, D_MODEL), 0)
    out = jnp.zeros((8, D_MODEL), F32)
    for i, r in enumerate(rows):
        out = jnp.where(rid == i, jnp.broadcast_to(r, (8, D_MODEL)), out)
    return out


def _small_rows(norm_mix_gain, b_gates, attn_sinks, ret_gn_gain, norm_mlp_gain, norm_final_gain):
    return [norm_mix_gain, b_gates[:, :D_MODEL], b_gates[:, D_MODEL:], ret_gn_gain, norm_mlp_gain,
            norm_final_gain.reshape(1, D_MODEL), jnp.pad(attn_sinks, ((0, 0), (0, D_MODEL - ATT_HEADS)))]


def _unpack_small(p):
    return dict(norm_mix_gain=p[0:1], b_gates=jnp.concatenate([p[1:2], p[2:3]], axis=1), ret_gn_gain=p[3:4],
                norm_mlp_gain=p[4:5], norm_final_gain=p[5], attn_sinks=p[6:7, :ATT_HEADS])


WEIGHTS = ("norm_mix_gain", "w_in", "b_gates", "attn_sinks", "ret_gn_gain", "w_att_up", "w_ret_up", "w_out",
           "norm_mlp_gain", "w_ff1", "w_ff2", "norm_final_gain")
BIG = ("w_in", "w_att_up", "w_ret_up", "w_out", "w_ff1", "w_ff2")


def kernel(x, norm_mix_gain, w_in, b_gates, attn_sinks, ret_gn_gain, w_att_up, w_ret_up, w_out, norm_mlp_gain, w_ff1, w_ff2, norm_final_gain, loss_target, m_norm_mix_gain, m_w_in, m_b_gates, m_attn_sinks, m_ret_gn_gain, m_w_att_up, m_w_ret_up, m_w_out, m_norm_mlp_gain, m_w_ff1, m_w_ff2, m_norm_final_gain, v_norm_mix_gain, v_w_in, v_b_gates, v_attn_sinks, v_ret_gn_gain, v_w_att_up, v_w_ret_up, v_w_out, v_norm_mlp_gain, v_w_ff1, v_w_ff2, v_norm_final_gain):
    w = dict(norm_mix_gain=norm_mix_gain, w_in=w_in, b_gates=b_gates, attn_sinks=attn_sinks, ret_gn_gain=ret_gn_gain,
             w_att_up=w_att_up, w_ret_up=w_ret_up, w_out=w_out, norm_mlp_gain=norm_mlp_gain, w_ff1=w_ff1,
             w_ff2=w_ff2, norm_final_gain=norm_final_gain)
    m = dict(norm_mix_gain=m_norm_mix_gain, w_in=m_w_in, b_gates=m_b_gates, attn_sinks=m_attn_sinks,
             ret_gn_gain=m_ret_gn_gain, w_att_up=m_w_att_up, w_ret_up=m_w_ret_up, w_out=m_w_out,
             norm_mlp_gain=m_norm_mlp_gain, w_ff1=m_w_ff1, w_ff2=m_w_ff2, norm_final_gain=m_norm_final_gain)
    v = dict(norm_mix_gain=v_norm_mix_gain, w_in=v_w_in, b_gates=v_b_gates, attn_sinks=v_attn_sinks,
             ret_gn_gain=v_ret_gn_gain, w_att_up=v_w_att_up, w_ret_up=v_w_ret_up, w_out=v_w_out,
             norm_mlp_gain=v_norm_mlp_gain, w_ff1=v_w_ff1, w_ff2=v_w_ff2, norm_final_gain=v_norm_final_gain)

    shards = {k: (w[k][0].T if k in COLUMN_SHARDED else w[k][0]).astype(BF) for k in BIG}
    loss_p, dx, small, groups = _local_step(
        x[0], loss_target[0], norm_mix_gain, b_gates, attn_sinks[0], ret_gn_gain, norm_mlp_gain,
        norm_final_gain.reshape(1, D_MODEL), shards)

    by_name = {}
    for grp in groups:
        for k, stack, r1, r2 in zip(grp.names, grp.stacks, grp.r1s, grp.r2s):
            by_name[k] = (stack, r1, r2)
    lane = lax.broadcasted_iota(jnp.int32, (1, D_MODEL), 1)
    sink_row = jnp.where(lane < ATT_HEADS, jnp.pad(small["attn_sinks"], ((0, 0), (0, D_MODEL - LANE))),
                         jnp.where(lane == ATT_HEADS, jnp.pad(loss_p, ((0, 0), (0, D_MODEL - LANE)), mode="edge"),
                                   0.0))
    small_pack = _pack_rows([small["norm_mix_gain"], small["b_gates"][:, :D_MODEL], small["b_gates"][:, D_MODEL:],
                             small["ret_gn_gain"], small["norm_mlp_gain"], small["norm_final_gain"], sink_row])
    xi, yi, ci = _coords()
    (small_all,) = _comm_alone(_small_comm(small_pack), "small_exchange")
    idx_arr = jnp.stack([4 * xi + 2 * yi + ci, 2 * xi + yi]).astype(jnp.int32)
    in_pieces = ["w_att_up", "w_ret_up", "w_out", "w_ff2", "w_in"]
    in_sum = ["w_ff1"]
    order = in_pieces + in_sum
    summed_t = _shard_sum(*[[by_name[k][i] for k in in_sum] for i in range(3)], idx_arr)

    def shard(tree, k):
        return tree[k][0].T if k == "w_in" else tree[k][0]

    adam_out = _adam_big([[by_name[k][i] for k in in_pieces] for i in range(3)], [g.T for g in summed_t],
                         [shard(w, k) for k in order], [shard(m, k) for k in order], [shard(v, k) for k in order],
                         idx_arr)
    big_out = [adam_out[4 * order.index(k) + i].T if k == "w_in" else adam_out[4 * order.index(k) + i]
               for k in BIG for i in range(4)]
    sm_g, sm_d, sm_m, sm_v = _adam_small(small_all, _pack_rows(_small_rows(*[w[k] for k in WEIGHTS if k not in BIG])),
                                         _pack_rows(_small_rows(*[m[k] for k in WEIGHTS if k not in BIG])),
                                         _pack_rows(_small_rows(*[v[k] for k in WEIGHTS if k not in BIG])))

    loss = sm_g[6, ATT_HEADS]
    kinds = []
    for idx, packed in enumerate((sm_g, sm_d, sm_m, sm_v)):
        out = _unpack_small(packed)
        for a, k in enumerate(BIG):
            out[k] = big_out[4 * a + idx][None]
        kinds.append(out)
    return (loss, dx[None], *[kinds[0][k] for k in WEIGHTS], *[kinds[1][k] for k in WEIGHTS],
            *[kinds[2][k] for k in WEIGHTS], *[kinds[3][k] for k in WEIGHTS])
```

```python
import functools
import math

import jax
import jax.numpy as jnp
from jax import lax
from jax.experimental import pallas as pl
from jax.experimental.pallas import tpu as pltpu

F32 = jnp.float32
BF = jnp.bfloat16
MESH = pl.DeviceIdType.MESH

D_MODEL = 1024
ATT_HEADS = 8
ATT_HEAD_DIM = 64
ATT_BLOCK = 128
ROPE_DIM = 16
ROPE_THETA = 500000.0
RET_HEADS = 4
RET_KEY_DIM = 128
RET_VAL_DIM = 256
RET_CHUNK = 128
RET_ROT_BASE = 10000.0
D_FF = 4096
NORM_EPS = 1e-6
GN_EPS = 1e-6
NEG_INF = -1e30
ATT_SCALE = ATT_HEAD_DIM ** -0.5
RET_SCALE = RET_KEY_DIM ** -0.5

C_QA, C_KA, C_VA, C_QR, C_KR, C_VR, C_GR, C_GATES, C_END = 0, 512, 640, 768, 1280, 1792, 2816, 3840, 5888

ADAM_LR = 0.001
ADAM_B1 = 0.9
ADAM_B2 = 0.999
ADAM_EPS = 1e-08
ADAM_WD = 0.01
ADAM_STEP = 10

N_DEV = 8
LANE = 128
VMEM_LIMIT = 56 * 1024 * 1024
TOKEN_TILE = 512
MLP_TOKEN_TILE = 256
TN_TOKEN_TILE = 4096
FF_CHUNKS = (0, 1024, 2048, 3072, 4096)
IN_CHUNKS_FWD = (C_QA, C_KA, C_QR, C_KR, C_VR, C_GR, C_GATES, C_END)
IN_CHUNKS_BWD = (C_QA, C_KA, C_QR, C_GATES, C_END)

RET_LOG_GAMMA = tuple(math.log1p(-(2.0 ** (-5.0 - h))) for h in range(RET_HEADS))
RET_CHUNK_DECAY = tuple(math.exp(RET_CHUNK * lg) for lg in RET_LOG_GAMMA)

VMEM_FULL = pl.BlockSpec(memory_space=pltpu.VMEM)
SMEM_FULL = pl.BlockSpec(memory_space=pltpu.SMEM)
ANY = pl.BlockSpec(memory_space=pl.ANY)


def _params(n_axes):
    return pltpu.CompilerParams(dimension_semantics=("arbitrary",) * n_axes, vmem_limit_bytes=VMEM_LIMIT)


def _nn(a, b):
    return jnp.dot(a, b, preferred_element_type=F32)


def _nt(a, b):
    return lax.dot_general(a, b, (((1,), (1,)), ((), ())), preferred_element_type=F32)


def _tn(a, b):
    return lax.dot_general(a, b, (((0,), (0,)), ((), ())), preferred_element_type=F32)


def _sigmoid(v):
    return 1.0 / (1.0 + jnp.exp(-v))


def _rows(tile, width):
    return pl.BlockSpec((tile, width), lambda i: (i, 0))


def _cols(height, tile):
    return pl.BlockSpec((height, tile), lambda i: (0, i))


class _Resident:
    def __init__(self, hbm_ref, vmem_ref, sems, bounds):
        self.hbm, self.vmem, self.sems, self.bounds = hbm_ref, vmem_ref, sems, bounds

    @staticmethod
    def scratch(w, bounds):
        return [pltpu.VMEM(w.shape, w.dtype), pltpu.SemaphoreType.DMA((len(bounds) - 1,))]

    @staticmethod
    def load(*weights):
        @pl.when(pl.program_id(0) == 0)
        def _():
            copies = [w._copy(c) for w in weights for c in range(len(w.bounds) - 1)]
            for cp in copies:
                cp.start()
            for cp in copies:
                cp.wait()

    def _rows(self, c):
        return pl.ds(self.bounds[c], self.bounds[c + 1] - self.bounds[c])

    def _copy(self, c):
        return pltpu.make_async_copy(self.hbm.at[self._rows(c)], self.vmem.at[self._rows(c)], self.sems.at[c])

    def chunk(self, c):
        return self.vmem.at[self._rows(c)]


class _Comm:
    def __init__(self, inputs, out_shapes, scratch, start, finish, relay=None):
        self.inputs, self.out_shapes, self.scratch = inputs, out_shapes, scratch
        self.start, self.finish, self.relay = start, finish, relay


RELAY_AT = 0.7


def _join(a, b):
    na_in, na_out, na_sem = len(a.inputs), len(a.out_shapes), len(a.scratch)

    def both(name):
        def run(ins, outs, sems):
            for part, args in ((a, (ins[:na_in], outs[:na_out], sems[:na_sem])),
                               (b, (ins[na_in:], outs[na_out:], sems[na_sem:]))):
                if getattr(part, name) is not None:
                    getattr(part, name)(*args)
        return run

    return _Comm(list(a.inputs) + list(b.inputs), list(a.out_shapes) + list(b.out_shapes),
                 list(a.scratch) + list(b.scratch), both("start"), both("finish"),
                 both("relay") if a.relay or b.relay else None)


def _call(body, args, comm=None, *, name, grid, in_specs, out_specs, out_shape, scratch_shapes=()):
    params = _params(len(grid))
    if comm is None:
        return pl.pallas_call(body, name=name, grid=grid, in_specs=in_specs, out_specs=out_specs, out_shape=out_shape,
                              scratch_shapes=scratch_shapes, compiler_params=params)(*args), ()
    single = not isinstance(out_specs, (list, tuple))
    out_specs_l = [out_specs] if single else list(out_specs)
    out_shape_l = [out_shape] if single else list(out_shape)
    n_in, n_out, n_scr = len(in_specs), len(out_specs_l), len(scratch_shapes)
    n_cin, n_cout = len(comm.inputs), len(comm.out_shapes)

    def hosted(*refs):
        it = iter(refs)
        ins, cin, outs, cout, scr = ([next(it) for _ in range(k)] for k in (n_in, n_cin, n_out, n_cout, n_scr))
        sems = list(it)
        ids = [pl.program_id(k) for k in range(len(grid))]
        first = functools.reduce(jnp.logical_and, [i == 0 for i in ids])
        last = functools.reduce(jnp.logical_and, [i == g - 1 for i, g in zip(ids, grid)])

        @pl.when(first)
        def _():
            comm.start(cin, cout, sems)

        if comm.relay is not None:
            at = [int(grid[0] * RELAY_AT)] + [0] * (len(grid) - 1)

            @pl.when(functools.reduce(jnp.logical_and, [i == v for i, v in zip(ids, at)]))
            def _():
                comm.relay(cin, cout, sems)

        body(*ins, *outs, *scr)

        @pl.when(last)
        def _():
            comm.finish(cin, cout, sems)

    res = pl.pallas_call(
        hosted, name=name, grid=grid, in_specs=list(in_specs) + [ANY] * n_cin,
        out_specs=out_specs_l + [ANY] * n_cout, out_shape=out_shape_l + list(comm.out_shapes),
        scratch_shapes=list(scratch_shapes) + list(comm.scratch), compiler_params=params)(*args, *comm.inputs)
    return (res[0] if single else res[:n_out]), res[n_out:]


def _comm_alone(comm, name):
    n_cin, n_cout = len(comm.inputs), len(comm.out_shapes)

    def body(*refs):
        cin, cout, sems = refs[:n_cin], refs[n_cin:n_cin + n_cout], refs[n_cin + n_cout:]
        comm.start(cin, cout, sems)
        if comm.relay is not None:
            comm.relay(cin, cout, sems)
        comm.finish(cin, cout, sems)

    return pl.pallas_call(body, name=name, in_specs=[ANY] * n_cin, out_specs=[ANY] * n_cout,
                          out_shape=list(comm.out_shapes), scratch_shapes=list(comm.scratch))(*comm.inputs)


def _slabs(v, fn):
    return jnp.concatenate([fn(v[:, LANE * j:LANE * (j + 1)]) for j in range(v.shape[1] // LANE)], axis=1)


def _rope_att(v, ca, sa, sb):
    return _slabs(v, lambda t: t * ca + pltpu.roll(t, LANE - 8, 1) * sa + pltpu.roll(t, 8, 1) * sb)


def _rope_att_t(v, ca, sa, sb):
    return _slabs(v, lambda t: t * ca + pltpu.roll(t * sa, 8, 1) + pltpu.roll(t * sb, LANE - 8, 1))


def _rope_att_rows(v, ct, st, sign):
    parts = []
    for h in range(v.shape[0] // ATT_HEAD_DIM):
        r0 = ATT_HEAD_DIM * h
        x1, x2 = v[r0:r0 + 8], v[r0 + 8:r0 + 16]
        parts += [x1 * ct - sign * (x2 * st), x2 * ct + sign * (x1 * st), v[r0 + 16:r0 + ATT_HEAD_DIM]]
    return jnp.concatenate(parts, axis=0)


def _rope_ret(v, cr, sr):
    return _slabs(v, lambda t: t * cr + pltpu.roll(t, 64, 1) * sr)


def _rope_ret_t(v, cr, sr):
    return _slabs(v, lambda t: t * cr + pltpu.roll(t * sr, 64, 1))


def _rope_lane_tables():
    def inv(dim, theta):
        return theta ** (-jnp.arange(0, dim, 2, dtype=F32) / dim)

    inv_a, inv_r = inv(ROPE_DIM, ROPE_THETA), inv(RET_KEY_DIM, RET_ROT_BASE)
    half = ROPE_DIM // 2
    zeros = jnp.zeros((ATT_HEAD_DIM - ROPE_DIM,), F32)
    freq64 = jnp.concatenate([inv_a, inv_a, zeros])
    lo64 = jnp.concatenate([-jnp.ones((half,), F32), jnp.zeros((half,), F32), zeros])
    hi64 = jnp.concatenate([jnp.zeros((half,), F32), jnp.ones((half,), F32), zeros])
    sign_r = jnp.concatenate([-jnp.ones((64,), F32), jnp.ones((64,), F32)])
    rows = [jnp.tile(freq64, 2), jnp.tile(lo64, 2), jnp.tile(hi64, 2), jnp.tile(inv_r, 2), sign_r]
    lanes = jnp.stack(rows + [jnp.zeros((LANE,), F32)] * (8 - len(rows)))
    return lanes, jnp.broadcast_to(inv_a[:, None], (half, LANE))


def _prologue(x, g1, comm):
    seq = x.shape[0]
    tm = min(TOKEN_TILE, seq)

    def body(x_ref, g_ref, lanes_ref, freq_rows_ref, xn_ref, ca_ref, sa_ref, sb_ref, cr_ref, sr_ref, ct_ref, st_ref):
        xf = x_ref[...]
        r = lax.rsqrt(jnp.mean(xf * xf, axis=-1, keepdims=True) + NORM_EPS)
        xn_ref[...] = (xf * r * g_ref[...]).astype(BF)
        row0 = pl.program_id(0) * tm
        pos = (row0 + lax.broadcasted_iota(jnp.int32, (tm, LANE), 0)).astype(F32)
        ang_a = pos * lanes_ref[0:1, :]
        sin_a = jnp.sin(ang_a)
        ca_ref[...] = jnp.cos(ang_a)
        sa_ref[...] = sin_a * lanes_ref[1:2, :]
        sb_ref[...] = sin_a * lanes_ref[2:3, :]
        ang_r = pos * lanes_ref[3:4, :]
        cr_ref[...] = jnp.cos(ang_r)
        sr_ref[...] = jnp.sin(ang_r) * lanes_ref[4:5, :]
        pos_t = (row0 + lax.broadcasted_iota(jnp.int32, (8, tm), 1)).astype(F32)
        ang_t = pos_t * jnp.concatenate([freq_rows_ref[...]] * (tm // LANE), axis=1)
        ct_ref[...] = jnp.cos(ang_t)
        st_ref[...] = jnp.sin(ang_t)

    tab = _rows(tm, LANE)
    tab_t = _cols(8, tm)
    return _call(
        body, (x, g1, *_rope_lane_tables()), comm, name="prologue", grid=(seq // tm,),
        in_specs=[_rows(tm, D_MODEL), VMEM_FULL, VMEM_FULL, VMEM_FULL],
        out_specs=[_rows(tm, D_MODEL), tab, tab, tab, tab, tab, tab_t, tab_t],
        out_shape=[jax.ShapeDtypeStruct((seq, D_MODEL), BF)] + [jax.ShapeDtypeStruct((seq, LANE), F32)] * 5
        + [jax.ShapeDtypeStruct((8, seq), F32)] * 2)


def _ret_tables():
    c = RET_CHUNK
    lg = jnp.asarray(RET_LOG_GAMMA, F32)
    idx = jnp.arange(c, dtype=F32)
    diff = idx[:, None] - idx[None, :]
    intra = jnp.where(diff >= 0, jnp.exp(jnp.maximum(diff, 0.0) * lg[:, None, None]), 0.0)
    qd = jnp.exp((idx + 1.0)[None, :] * lg[:, None])[..., None]
    kd = jnp.exp((c - 1.0 - idx)[None, :] * lg[:, None])[..., None]
    return intra, jnp.broadcast_to(qd, (RET_HEADS, c, RET_KEY_DIM)), jnp.broadcast_to(kd, (RET_HEADS, c, RET_KEY_DIM))


def _inproj_fwd(xn1, w_in, tabs, comm=None):
    seq = xn1.shape[0]
    tm = min(TOKEN_TILE, seq)

    def body(xn_ref, w_hbm, ca_ref, sa_ref, sb_ref, cr_ref, sr_ref, ct_ref, st_ref,
             qt_ref, kv_ref, kvt_ref, qkr_ref, vr_ref, gr_ref, gates_ref, w_vmem, w_sems):
        w = _Resident(w_hbm, w_vmem, w_sems, IN_CHUNKS_FWD)
        _Resident.load(w)
        ca, sa, sb, cr, sr, ct, st = (ref[...] for ref in (ca_ref, sa_ref, sb_ref, cr_ref, sr_ref, ct_ref, st_ref))
        xn = xn_ref[...]
        qt_ref[...] = _rope_att_rows(_nt(w.chunk(0)[...], xn), ct, st, 1.0).astype(BF)
        w_kv = w.chunk(1)
        kvt = _nt(w_kv[...], xn)
        kvt_ref[:LANE, :] = _rope_att_rows(kvt[:LANE], ct, st, 1.0).astype(BF)
        kvt_ref[LANE:, :] = kvt[LANE:].astype(BF)
        kvn = _nt(xn, w_kv[...])
        kv_ref[:, :LANE] = _rope_att(kvn[:, :LANE], ca, sa, sb).astype(BF)
        kv_ref[:, LANE:] = kvn[:, LANE:].astype(BF)
        qr = _rope_ret(_nt(xn, w.chunk(2)[...]), cr, sr)
        qkr_ref[:, :512] = qr.astype(BF)
        kr = _rope_ret(_nt(xn, w.chunk(3)[...]), cr, sr) * RET_SCALE
        qkr_ref[:, 512:] = kr.astype(BF)
        vr_ref[...] = _nt(xn, w.chunk(4)[...]).astype(BF)
        gr_ref[...] = _nt(xn, w.chunk(5)[...]).astype(BF)
        gates_ref[...] = _nt(xn, w.chunk(6)[...]).astype(BF)

    tab = _rows(tm, LANE)
    tab_t = _cols(8, tm)
    return _call(
        body, (xn1, w_in, *tabs), comm, name="inproj_fwd", grid=(seq // tm,),
        in_specs=[_rows(tm, D_MODEL), ANY, tab, tab, tab, tab, tab, tab_t, tab_t],
        out_specs=[_cols(512, tm), _rows(tm, 256), _cols(256, tm), _rows(tm, 1024),
                   _rows(tm, 1024), _rows(tm, 1024), _rows(tm, 2048)],
        out_shape=[jax.ShapeDtypeStruct(s, BF) for s in ((512, seq), (seq, 256), (256, seq),
                                                         (seq, 1024), (seq, 1024), (seq, 1024), (seq, 2048))],
        scratch_shapes=_Resident.scratch(w_in, IN_CHUNKS_FWD))


ATT_Q_PER_KV = ATT_HEADS // 2


def _att_group(ref, kv):
    d = ATT_HEAD_DIM
    g = jnp.concatenate([ref[d * (ATT_Q_PER_KV * kv + j):d * (ATT_Q_PER_KV * kv + j + 1), :]
                         for j in range(ATT_Q_PER_KV)], axis=1)
    z = jnp.zeros_like(g)
    return jnp.concatenate([g, z] if kv == 0 else [z, g], axis=0)


def _att_probs(kwin, qpad, sink_ref, kv, i):
    c = ATT_BLOCK
    kj = lax.broadcasted_iota(jnp.int32, (2 * c, c), 0)
    qi = lax.broadcasted_iota(jnp.int32, (2 * c, c), 1)
    allowed = (kj > qi) & (kj <= qi + c) & ((kj >= c) | (i > 0))
    allowed = jnp.concatenate([allowed] * ATT_Q_PER_KV, axis=1)
    s = jnp.where(allowed, _nn(kwin, qpad) * ATT_SCALE, NEG_INF)
    sk = jnp.concatenate([jnp.full((1, c), sink_ref[ATT_Q_PER_KV * kv + j], F32) for j in range(ATT_Q_PER_KV)], axis=1)
    m = jnp.maximum(jnp.max(s, axis=0, keepdims=True), sk)
    pe = jnp.exp(s - m)
    psink = jnp.exp(sk - m)
    inv = 1.0 / (jnp.sum(pe, axis=0, keepdims=True) + psink)
    return pe * inv, psink * inv


def _seq_fwd(qt, kv, kvt, sinks, qkr, vr, gr, gain, rtabs, comm=None):
    seq = kv.shape[0]
    c = ATT_BLOCK
    assert c == RET_CHUNK
    d = ATT_HEAD_DIM
    nb = seq // c
    dk, dv = RET_KEY_DIM, RET_VAL_DIM

    def body(sink_ref, qt_ref, kvc_ref, kvp_ref, kvtc_ref, kvtp_ref, qk_ref, v_ref, g_ref, gain_ref, intra_ref,
             qd_ref, kd_ref, ot_ref, ret_ref, yrp_ref, st_ref, state):
        i = pl.program_id(0)
        kwin = jnp.concatenate([kvp_ref[:, :LANE], kvc_ref[:, :LANE]], axis=0)
        vtwin = jnp.concatenate([kvtp_ref[LANE:, :], kvtc_ref[LANE:, :]], axis=1)
        for g in (0, 1):
            prob, _ = _att_probs(kwin, _att_group(qt_ref, g), sink_ref, g, i)
            out = _nn(vtwin[d * g:d * (g + 1), :], prob.astype(BF))
            for j in range(ATT_Q_PER_KV):
                h = ATT_Q_PER_KV * g + j
                ot_ref[d * h:d * (h + 1), :] = out[:, c * j:c * (j + 1)].astype(BF)

        @pl.when(i == 0)
        def _():
            state[...] = jnp.zeros_like(state)

        for h in range(RET_HEADS):
            qh = qk_ref[:, dk * h:dk * (h + 1)]
            kh = qk_ref[:, 512 + dk * h:512 + dk * (h + 1)]
            vh = v_ref[:, dv * h:dv * (h + 1)]
            sh = state[h]
            shb = sh.astype(BF)
            st_ref[0, h] = shb
            att = _nt(qh, kh) * intra_ref[h]
            inner = _nn(att.astype(BF), vh)
            cross = _nn((qh.astype(F32) * qd_ref[h]).astype(BF), shb)
            out = inner + cross
            state[h] = sh * RET_CHUNK_DECAY[h] + _tn((kh.astype(F32) * kd_ref[h]).astype(BF), vh)
            ret_ref[:, dv * h:dv * (h + 1)] = out
            mu = jnp.mean(out, axis=-1, keepdims=True)
            dev = out - mu
            var = jnp.mean(dev * dev, axis=-1, keepdims=True)
            y = dev * lax.rsqrt(var + GN_EPS) * gain_ref[:, dv * h:dv * (h + 1)]
            g = g_ref[:, dv * h:dv * (h + 1)].astype(F32)
            yrp_ref[:, dv * h:dv * (h + 1)] = (g * _sigmoid(g) * y).astype(BF)

    prev = lambda i: jnp.maximum(i - 1, 0)
    return _call(
        body, (sinks, qt, kv, kv, kvt, kvt, qkr, vr, gr, gain, *rtabs), comm, name="seq_fwd", grid=(nb,),
        in_specs=[SMEM_FULL, _cols(512, c), _rows(c, 256), pl.BlockSpec((c, 256), lambda i: (prev(i), 0)),
                  _cols(256, c), pl.BlockSpec((256, c), lambda i: (0, prev(i))),
                  _rows(c, 1024), _rows(c, 1024), _rows(c, 1024), VMEM_FULL, VMEM_FULL, VMEM_FULL, VMEM_FULL],
        out_specs=[_cols(512, c), _rows(c, 1024), _rows(c, 1024),
                   pl.BlockSpec((1, RET_HEADS, dk, dv), lambda i: (i, 0, 0, 0))],
        out_shape=[jax.ShapeDtypeStruct((512, seq), BF), jax.ShapeDtypeStruct((seq, 1024), F32),
                   jax.ShapeDtypeStruct((seq, 1024), BF), jax.ShapeDtypeStruct((nb, RET_HEADS, dk, dv), BF)],
        scratch_shapes=[pltpu.VMEM((RET_HEADS, dk, dv), F32)])


def _mix_fwd(oat, yrp, gates, bg, x, w_att, w_ret, w_out):
    seq = x.shape[0]
    tm = min(TOKEN_TILE, seq)

    def body(oat_ref, yrp_ref, gates_ref, bg_ref, x_ref, wa_ref, wr_ref, wo_ref, ya_ref, yr_ref, mg_ref, h1_ref):
        ya = _tn(oat_ref[...], wa_ref[...])
        yr = _nn(yrp_ref[...], wr_ref[...])
        gt = _sigmoid(gates_ref[...].astype(F32) + bg_ref[...])
        merged = (gt[:, :D_MODEL] * ya + gt[:, D_MODEL:] * yr).astype(BF)
        ya_ref[...] = ya.astype(BF)
        yr_ref[...] = yr.astype(BF)
        mg_ref[...] = merged
        h1_ref[...] = x_ref[...] + _nn(merged, wo_ref[...])

    return pl.pallas_call(
        body, name="mix_fwd", grid=(seq // tm,),
        in_specs=[_cols(512, tm), _rows(tm, 1024), _rows(tm, 2048), VMEM_FULL, _rows(tm, D_MODEL),
                  VMEM_FULL, VMEM_FULL, VMEM_FULL],
        out_specs=[_rows(tm, D_MODEL)] * 4,
        out_shape=[jax.ShapeDtypeStruct((seq, D_MODEL), BF)] * 3 + [jax.ShapeDtypeStruct((seq, D_MODEL), F32)],
        compiler_params=_params(1),
    )(oat, yrp, gates, bg, x, w_att, w_ret, w_out)


def _mlp_loss_step(h1, g2, g3, target, w_ff1, w_ff2):
    seq = h1.shape[0]
    tm = min(MLP_TOKEN_TILE, seq)
    n_chunks = len(FF_CHUNKS) - 1

    def body(h1_ref, g2_ref, g3_ref, t_ref, w1_hbm, w2_hbm,
             xn2_ref, hdn_ref, dh2_ref, du_ref, dh1_ref, loss_ref, dg3_ref, dg2_ref,
             w1_vmem, w1_sems, w2_vmem, w2_sems, relu_u):
        w1 = _Resident(w1_hbm, w1_vmem, w1_sems, FF_CHUNKS)
        w2 = _Resident(w2_hbm, w2_vmem, w2_sems, FF_CHUNKS)
        _Resident.load(w1, w2)

        @pl.when(pl.program_id(0) == 0)
        def _():
            loss_ref[...] = jnp.zeros_like(loss_ref)
            dg3_ref[...] = jnp.zeros_like(dg3_ref)
            dg2_ref[...] = jnp.zeros_like(dg2_ref)

        h1v = h1_ref[...]
        r2 = lax.rsqrt(jnp.mean(h1v * h1v, axis=-1, keepdims=True) + NORM_EPS)
        xn2 = (h1v * r2 * g2_ref[...]).astype(BF)
        xn2_ref[...] = xn2
        h2 = h1v
        for c in range(n_chunks):
            cols = slice(FF_CHUNKS[c], FF_CHUNKS[c + 1])
            a = jnp.maximum(_nt(xn2, w1.chunk(c)[...]), 0.0)
            relu_u[:, cols] = a.astype(BF)
            hdn = jnp.square(a).astype(BF)
            hdn_ref[:, cols] = hdn
            h2 = h2 + _nn(hdn, w2.chunk(c)[...])
        r3 = lax.rsqrt(jnp.mean(h2 * h2, axis=-1, keepdims=True) + NORM_EPS)
        hn = h2 * r3
        err = hn * g3_ref[...] - t_ref[...]
        loss_ref[...] += jnp.sum(err * err) * (0.5 / D_MODEL)
        dy = err * (1.0 / D_MODEL)
        dg3_ref[...] += jnp.sum(dy * hn, axis=0, keepdims=True)
        z = dy * g3_ref[...]
        dh2 = r3 * (z - hn * jnp.mean(z * hn, axis=-1, keepdims=True))
        dh2b = dh2.astype(BF)
        dh2_ref[...] = dh2b
        dxn2 = jnp.zeros_like(dh2)
        for c in range(n_chunks):
            cols = slice(FF_CHUNKS[c], FF_CHUNKS[c + 1])
            du = (_nt(dh2b, w2.chunk(c)[...]) * (2.0 * relu_u[:, cols].astype(F32))).astype(BF)
            du_ref[:, cols] = du
            dxn2 = dxn2 + _nn(du, w1.chunk(c)[...])
        dnorm, dg = _rms_bwd(dxn2, h1v, g2_ref[...])
        dh1_ref[...] = dh2 + dnorm
        dg2_ref[...] += dg

    gain_acc = pl.BlockSpec((1, D_MODEL), lambda i: (0, 0))
    return pl.pallas_call(
        body, name="mlp_loss_step", grid=(seq // tm,),
        in_specs=[_rows(tm, D_MODEL), VMEM_FULL, VMEM_FULL, _rows(tm, D_MODEL), ANY, ANY],
        out_specs=[_rows(tm, D_MODEL), _rows(tm, D_FF), _rows(tm, D_MODEL), _rows(tm, D_FF), _rows(tm, D_MODEL),
                   pl.BlockSpec((1, LANE), lambda i: (0, 0)), gain_acc, gain_acc],
        out_shape=[jax.ShapeDtypeStruct((seq, D_MODEL), BF), jax.ShapeDtypeStruct((seq, D_FF), BF),
                   jax.ShapeDtypeStruct((seq, D_MODEL), BF), jax.ShapeDtypeStruct((seq, D_FF), BF),
                   jax.ShapeDtypeStruct((seq, D_MODEL), F32), jax.ShapeDtypeStruct((1, LANE), F32),
                   jax.ShapeDtypeStruct((1, D_MODEL), F32), jax.ShapeDtypeStruct((1, D_MODEL), F32)],
        scratch_shapes=(_Resident.scratch(w_ff1, FF_CHUNKS) + _Resident.scratch(w_ff2, FF_CHUNKS)
                        + [pltpu.VMEM((tm, D_FF), BF)]),
        compiler_params=_params(1),
    )(h1, g2, g3, target, w_ff1, w_ff2)


def _rms_bwd(dxn, xin, gain):
    r = lax.rsqrt(jnp.mean(xin * xin, axis=-1, keepdims=True) + NORM_EPS)
    xhat = xin * r
    z = dxn * gain
    dxin = r * (z - xhat * jnp.mean(z * xhat, axis=-1, keepdims=True))
    return dxin, jnp.sum(dxn * xhat, axis=0, keepdims=True)


def _mm_tn(a, b, name, a_is_t=False, comm=None):
    kdim, seq = a.shape if a_is_t else a.shape[::-1]
    ndim = b.shape[1]
    ts = min(TN_TOKEN_TILE, seq)
    tk = min(kdim, 1024)
    tn = min(ndim, 1024)
    n_steps = seq // ts

    def body(a_ref, b_ref, o_ref):
        mm = _nn if a_is_t else _tn
        part = mm(a_ref[...].astype(BF), b_ref[...].astype(BF))
        if n_steps == 1:
            o_ref[...] = part
        else:
            @pl.when(pl.program_id(2) == 0)
            def _():
                o_ref[...] = jnp.zeros_like(o_ref)

            o_ref[...] += part

    a_spec = (pl.BlockSpec((tk, ts), lambda k, n, s: (k, s)) if a_is_t
              else pl.BlockSpec((ts, tk), lambda k, n, s: (s, k)))
    res = _call(
        body, (a, b), comm, name=name, grid=(kdim // tk, ndim // tn, n_steps),
        in_specs=[a_spec, pl.BlockSpec((ts, tn), lambda k, n, s: (s, n))],
        out_specs=pl.BlockSpec((tk, tn), lambda k, n, s: (k, n)),
        out_shape=jax.ShapeDtypeStruct((kdim, ndim), F32))
    return res[0] if comm is None else res


def _dw_in_t(pieces, is_t, xn1, comm=None):
    seq = xn1.shape[0]
    tr = 256
    n_blk = [(p.shape[0] if t else p.shape[1]) // tr for p, t in zip(pieces, is_t)]
    offs = [sum(n_blk[:i]) for i in range(len(pieces) + 1)]
    n_p = len(pieces)

    def body(*refs):
        piece_refs, xn_ref, o_ref, ob_ref = refs[:n_p], refs[n_p], refs[n_p + 1], refs[n_p + 2]
        r = pl.program_id(0)
        for p in range(n_p):
            @pl.when((r >= offs[p]) & (r < offs[p + 1]))
            def _(p=p):
                res = (_nn if is_t[p] else _tn)(piece_refs[p][...], xn_ref[...])
                o_ref[...] = res
                ob_ref[...] = res.astype(BF)

    def piece_spec(p):
        def index(r):
            blk = jnp.clip(r - offs[p], 0, n_blk[p] - 1)
            return (blk, 0) if is_t[p] else (0, blk)
        return pl.BlockSpec((tr, seq) if is_t[p] else (seq, tr), index)

    return _call(
        body, (*pieces, xn1), comm, name="dw_in", grid=(offs[-1],),
        in_specs=[piece_spec(p) for p in range(n_p)] + [VMEM_FULL],
        out_specs=[_rows(tr, D_MODEL), _rows(tr, D_MODEL)],
        out_shape=[jax.ShapeDtypeStruct((offs[-1] * tr, D_MODEL), F32),
                   jax.ShapeDtypeStruct((offs[-1] * tr, D_MODEL), BF)])


def _mix_bwd(dh1, ya, yr, gates, bg, merged, oat, yrp, w_att, w_ret, w_out, comm=None):
    seq = dh1.shape[0]
    tm = min(TOKEN_TILE, seq)

    def body(dh1_ref, ya_ref, yr_ref, gates_ref, bg_ref, mg_ref, oat_ref, yrp_ref, wa_ref, wr_ref, wo_ref,
             dgates_ref, doa_ref, dyrp_ref, db_ref, dwa_ref, dwr_ref, dwo_ref):
        @pl.when(pl.program_id(0) == 0)
        def _():
            for ref in (db_ref, dwa_ref, dwr_ref, dwo_ref):
                ref[...] = jnp.zeros_like(ref)

        dh1b = dh1_ref[...].astype(BF)
        dm = _nt(dh1b, wo_ref[...])
        gt = _sigmoid(gates_ref[...].astype(F32) + bg_ref[...])
        ga, gr = gt[:, :D_MODEL], gt[:, D_MODEL:]
        dya = (dm * ga).astype(BF)
        dyr = (dm * gr).astype(BF)
        dga = dm * ya_ref[...].astype(F32) * ga * (1.0 - ga)
        dgr = dm * yr_ref[...].astype(F32) * gr * (1.0 - gr)
        dgates_ref[:, :D_MODEL] = dga.astype(BF)
        dgates_ref[:, D_MODEL:] = dgr.astype(BF)
        db_ref[:, :D_MODEL] += jnp.sum(dga, axis=0, keepdims=True)
        db_ref[:, D_MODEL:] += jnp.sum(dgr, axis=0, keepdims=True)
        doa_ref[...] = _nt(wa_ref[...], dya).astype(BF)
        dyrp_ref[...] = _nt(dyr, wr_ref[...]).astype(BF)
        dwo_ref[...] += _tn(mg_ref[...], dh1b)
        dwr_ref[...] += _tn(yrp_ref[...], dyr)
        dwa_ref[...] += _nn(oat_ref[...], dya)

    acc = lambda r, c: pl.BlockSpec((r, c), lambda i: (0, 0))
    return _call(
        body, (dh1, ya, yr, gates, bg, merged, oat, yrp, w_att, w_ret, w_out), comm, name="mix_bwd",
        grid=(seq // tm,),
        in_specs=[_rows(tm, D_MODEL), _rows(tm, D_MODEL), _rows(tm, D_MODEL), _rows(tm, 2048), VMEM_FULL,
                  _rows(tm, D_MODEL), _cols(512, tm), _rows(tm, 1024), VMEM_FULL, VMEM_FULL, VMEM_FULL],
        out_specs=[_rows(tm, 2048), _cols(512, tm), _rows(tm, 1024), acc(1, 2048), acc(512, D_MODEL),
                   acc(1024, D_MODEL), acc(D_MODEL, D_MODEL)],
        out_shape=[jax.ShapeDtypeStruct((seq, 2048), BF), jax.ShapeDtypeStruct((512, seq), BF),
                   jax.ShapeDtypeStruct((seq, 1024), BF), jax.ShapeDtypeStruct((1, 2048), F32),
                   jax.ShapeDtypeStruct((512, D_MODEL), F32), jax.ShapeDtypeStruct((1024, D_MODEL), F32),
                   jax.ShapeDtypeStruct((D_MODEL, D_MODEL), F32)])


def _seq_bwd(qt, kv, kvt, dot, sinks, qkr, vr, gr, ret, dyrp, states, gain, rtabs, tabs, comm=None):
    seq = kv.shape[0]
    c = ATT_BLOCK
    assert c == RET_CHUNK
    d = ATT_HEAD_DIM
    nb = seq // c
    dk, dv = RET_KEY_DIM, RET_VAL_DIM
    ca, sa, sb, cr, sr, ct, st = tabs

    def body(sink_ref, qt_ref, kvc_ref, kvp_ref, kvtc_ref, kvtp_ref, dot_ref, ca_ref, sa_ref, sb_ref, ct_ref, st_ref,
             qk_ref, v_ref, g_ref, ret_ref, dyp_ref, st8_ref, gain_ref, intra_ref, qd_ref, kd_ref, cr_ref, sr_ref,
             dqt_ref, dkv_ref, dsink_ref, dall_ref, dgain_ref, carry, dstate):
        step = pl.program_id(0)
        i = nb - 1 - step

        @pl.when(step == 0)
        def _():
            carry[...] = jnp.zeros_like(carry)
            dsink_ref[...] = jnp.zeros_like(dsink_ref)
            dstate[...] = jnp.zeros_like(dstate)
            dgain_ref[...] = jnp.zeros_like(dgain_ref)

        kwin = jnp.concatenate([kvp_ref[:, :LANE], kvc_ref[:, :LANE]], axis=0)
        vwin = jnp.concatenate([kvp_ref[:, LANE:], kvc_ref[:, LANE:]], axis=0)
        ktwin = jnp.concatenate([kvtp_ref[:LANE, :], kvtc_ref[:LANE, :]], axis=1)
        ctv, stv = ct_ref[...], st_ref[...]
        lane1 = lax.broadcasted_iota(jnp.int32, (1, LANE), 1)
        dkw = jnp.zeros((2 * c, LANE), F32)
        dvw = jnp.zeros((2 * c, LANE), F32)
        dsink = jnp.zeros((1, LANE), F32)
        for g in (0, 1):
            qpad, dopad = _att_group(qt_ref, g), _att_group(dot_ref, g)
            prob, psink = _att_probs(kwin, qpad, sink_ref, g, i)
            dprob = _nn(vwin, dopad)
            drow = jnp.sum(dprob * prob, axis=0, keepdims=True)
            ds = (prob * (dprob - drow) * ATT_SCALE).astype(BF)
            dqg = _nn(ktwin[d * g:d * (g + 1), :], ds)
            dkw = dkw + _nt(ds, qpad)
            dvw = dvw + _nt(prob.astype(BF), dopad)
            dsink_lanes = psink * drow
            for j in range(ATT_Q_PER_KV):
                h = ATT_Q_PER_KV * g + j
                dqt_ref[d * h:d * (h + 1), :] = _rope_att_rows(dqg[:, c * j:c * (j + 1)], ctv, stv, -1.0).astype(BF)
                dsink = dsink + jnp.where(lane1 == h, -jnp.sum(dsink_lanes[:, c * j:c * (j + 1)]), 0.0)
        dsink_ref[...] += dsink
        total = carry[...] + jnp.concatenate([dkw[c:], dvw[c:]], axis=1)
        dkv_ref[:, :LANE] = _rope_att_t(total[:, :LANE], ca_ref[...], sa_ref[...], sb_ref[...]).astype(BF)
        dkv_ref[:, LANE:] = total[:, LANE:].astype(BF)
        carry[...] = jnp.concatenate([dkw[:c], dvw[:c]], axis=1)

        crv, srv = cr_ref[...], sr_ref[...]
        for h in range(RET_HEADS):
            vs = slice(dv * h, dv * (h + 1))
            qh = qk_ref[:, dk * h:dk * (h + 1)]
            kh = qk_ref[:, 512 + dk * h:512 + dk * (h + 1)]
            vh = v_ref[:, vs]
            out = ret_ref[:, vs]
            g = g_ref[:, vs].astype(F32)
            dyp = dyp_ref[:, vs].astype(F32)
            gain_h = gain_ref[:, vs]
            mu = jnp.mean(out, axis=-1, keepdims=True)
            dev = out - mu
            rstd = lax.rsqrt(jnp.mean(dev * dev, axis=-1, keepdims=True) + GN_EPS)
            yn = dev * rstd
            sg = _sigmoid(g)
            dg = dyp * (yn * gain_h) * (sg * (1.0 + g * (1.0 - sg)))
            dy = dyp * (g * sg)
            dgain_ref[:, vs] += jnp.sum(dy * yn, axis=0, keepdims=True)
            dyn = dy * gain_h
            dout = rstd * (dyn - jnp.mean(dyn, axis=-1, keepdims=True)
                           - yn * jnp.mean(dyn * yn, axis=-1, keepdims=True))
            doutb = dout.astype(BF)
            sc = st8_ref[0, h]
            dsp = dstate[h]
            dspb = dsp.astype(BF)
            intra, qdv, kdv = intra_ref[h], qd_ref[h], kd_ref[h]
            att = _nt(qh, kh) * intra
            dab = (_nt(doutb, vh) * intra).astype(BF)
            qdec = (qh.astype(F32) * qdv).astype(BF)
            kdec = (kh.astype(F32) * kdv).astype(BF)
            dq = _nn(dab, kh) + _nt(doutb, sc) * qdv
            dkk = _tn(dab, qh) + _nt(vh, dspb) * kdv
            dvv = _tn(att.astype(BF), doutb) + _nn(kdec, dspb)
            dstate[h] = dsp * RET_CHUNK_DECAY[h] + _tn(qdec, doutb)
            dall_ref[:, dk * h:dk * (h + 1)] = _rope_ret_t(dq, crv, srv).astype(BF)
            dall_ref[:, 512 + dk * h:512 + dk * (h + 1)] = (_rope_ret_t(dkk, crv, srv) * RET_SCALE).astype(BF)
            dall_ref[:, 1024 + dv * h:1024 + dv * (h + 1)] = dvv.astype(BF)
            dall_ref[:, 2048 + dv * h:2048 + dv * (h + 1)] = dg.astype(BF)

    cur = lambda s: nb - 1 - s
    prev = lambda s: jnp.maximum(nb - 2 - s, 0)
    rows = lambda w: pl.BlockSpec((c, w), lambda s: (cur(s), 0))
    cols = lambda h: pl.BlockSpec((h, c), lambda s: (0, cur(s)))
    acc = lambda w: pl.BlockSpec((1, w), lambda s: (0, 0))
    return _call(
        body, (sinks, qt, kv, kv, kvt, kvt, dot, ca, sa, sb, ct, st, qkr, vr, gr, ret, dyrp, states, gain, *rtabs,
               cr, sr), comm, name="seq_bwd", grid=(nb,),
        in_specs=[SMEM_FULL, cols(512), rows(256), pl.BlockSpec((c, 256), lambda s: (prev(s), 0)), cols(256),
                  pl.BlockSpec((256, c), lambda s: (0, prev(s))), cols(512), rows(LANE), rows(LANE), rows(LANE),
                  cols(8), cols(8), rows(1024), rows(1024), rows(1024), rows(1024), rows(1024),
                  pl.BlockSpec((1, RET_HEADS, dk, dv), lambda s: (cur(s), 0, 0, 0)),
                  VMEM_FULL, VMEM_FULL, VMEM_FULL, VMEM_FULL, rows(LANE), rows(LANE)],
        out_specs=[cols(512), rows(256), acc(LANE), rows(3072), acc(1024)],
        out_shape=[jax.ShapeDtypeStruct((512, seq), BF), jax.ShapeDtypeStruct((seq, 256), BF),
                   jax.ShapeDtypeStruct((1, LANE), F32), jax.ShapeDtypeStruct((seq, 3072), BF),
                   jax.ShapeDtypeStruct((1, 1024), F32)],
        scratch_shapes=[pltpu.VMEM((c, 256), F32), pltpu.VMEM((RET_HEADS, dk, dv), F32)])


def _inproj_bwd(dqa, dkva, dret, dgates, dh1, x, g1, w_in, comm=None):
    seq = x.shape[0]
    tm = min(TOKEN_TILE, seq)

    def body(dqa_ref, dkva_ref, dret_ref, dgates_ref, dh1_ref, x_ref, g_ref, w_hbm, dx_ref, dg1_ref, w_vmem, w_sems):
        w = _Resident(w_hbm, w_vmem, w_sems, IN_CHUNKS_BWD)
        _Resident.load(w)

        @pl.when(pl.program_id(0) == 0)
        def _():
            dg1_ref[...] = jnp.zeros_like(dg1_ref)

        dxn = _tn(dqa_ref[...], w.chunk(0)[...])
        dxn = dxn + _nn(dkva_ref[...], w.chunk(1)[...])
        dxn = dxn + _nn(dret_ref[...], w.chunk(2)[...])
        dxn = dxn + _nn(dgates_ref[...], w.chunk(3)[...])
        dnorm, dg = _rms_bwd(dxn, x_ref[...], g_ref[...])
        dx_ref[...] = dh1_ref[...] + dnorm
        dg1_ref[...] += dg

    return _call(
        body, (dqa, dkva, dret, dgates, dh1, x, g1, w_in), comm, name="inproj_bwd", grid=(seq // tm,),
        in_specs=[_cols(512, tm), _rows(tm, 256), _rows(tm, 3072), _rows(tm, 2048), _rows(tm, D_MODEL),
                  _rows(tm, D_MODEL), VMEM_FULL, ANY],
        out_specs=[_rows(tm, D_MODEL), pl.BlockSpec((1, D_MODEL), lambda i: (0, 0))],
        out_shape=[jax.ShapeDtypeStruct((seq, D_MODEL), F32), jax.ShapeDtypeStruct((1, D_MODEL), F32)],
        scratch_shapes=_Resident.scratch(w_in, IN_CHUNKS_BWD))


def _local_step(x, target, g1, bg, sinks, gain, g2, g3, shards):
    rtabs = _ret_tables()
    scatter = _Scatter

    comm, unpack = _gather(shards, ("w_in",))
    (xn1, ca, sa, sb, cr, sr, ct, st), got = _prologue(x, g1, comm)
    (w_in,) = unpack(got)
    comm_mid, unpack_mid = _gather(shards, ("w_att_up", "w_ret_up", "w_out"))
    comm_ff1, unpack_ff1 = _gather(shards, ("w_ff1",))
    (qt, kv, kvt, qkr, vr, gr, gates), got = _inproj_fwd(xn1, w_in, (ca, sa, sb, cr, sr, ct, st),
                                                         _join(comm_mid, comm_ff1))
    w_att, w_ret, w_out = unpack_mid(got[:3])
    (w_ff1,) = unpack_ff1(got[3:])
    comm, unpack = _gather(shards, ("w_ff2",))
    (oa, ret, yrp, states), got = _seq_fwd(qt, kv, kvt, sinks, qkr, vr, gr, gain, rtabs, comm)
    (w_ff2,) = unpack(got)
    ya, yr, merged, h1 = _mix_fwd(oa, yrp, gates, bg, x, w_att, w_ret, w_out)
    xn2, hdn, dh2, du, dh1, loss, dg3, dg2 = _mlp_loss_step(h1, g2, g3, target, w_ff1, w_ff2)

    ff2 = scatter(dict(w_ff2=_mm_tn(hdn, dh2, "dw_ff2")), "ff2")
    ff1 = scatter(dict(w_ff1=_mm_tn(du, xn2, "dw_ff1")), "ff1")
    (dgates, doa, dyrp, db, d_att, d_ret, d_out), got = _mix_bwd(
        dh1, ya, yr, gates, bg, merged, oa, yrp, w_att, w_ret, w_out, _join(ff2.pair_comm(), ff1.pair_comm()))
    _Scatter.pairs_done((ff2, ff1), (got[:1], got[1:]), "ff")
    mid = scatter(dict(w_att_up=d_att, w_ret_up=d_ret, w_out=d_out), "mid")
    (dqa, dkva, dsink, dret, dgain), got = _seq_bwd(
        qt, kv, kvt, doa, sinks, qkr, vr, gr, ret, dyrp, states, gain, rtabs, (ca, sa, sb, cr, sr, ct, st),
        _join(_join(ff2.chip_comm(), ff1.chip_comm()), mid.pair_comm()))
    ff2.chip_done(got[:1])
    ff1.chip_done(got[1:2])
    mid.pair_done(got[2:])
    (d_in, d_in_bf), r2 = _dw_in_t((dqa, dkva, dret, dgates), (True, False, False, False), xn1, mid.chip_comm())
    mid.chip_done(r2)
    win = scatter(dict(w_in=d_in), "in", sent=dict(w_in=d_in_bf))
    win.exchange_and_sum()
    (dx, dg1), r2 = _inproj_bwd(dqa, dkva, dret, dgates, dh1, x, g1, w_in, win.chip_comm())
    win.chip_done(r2)
    small = dict(norm_mix_gain=dg1, b_gates=db, attn_sinks=dsink, ret_gn_gain=dgain, norm_mlp_gain=dg2,
                 norm_final_gain=dg3)
    return loss, dx, small, (win, mid, ff1, ff2)


def _coords():
    return lax.axis_index("x"), lax.axis_index("y"), lax.axis_index("c")


def _flip(v, bit):
    return 1 - v if bit else v


def _xor(a, b):
    return a + b - 2 * a * b


def _gather_relay_comm(shards):
    n = len(shards)

    def parts(ins, outs, sems):
        send_sems, recv_sems, local_sems = sems
        x, y, c = _coords()
        me, sib = (x, y, c), (x, y, 1 - c)

        def ring(core):
            return ((_xor(x, core), _xor(y, 1 - core), core), (_xor(x, 1 - core), _xor(y, core), core),
                    (1 - x, 1 - y, core))

        def slot(a, blk):
            return outs[a].at[4 * blk[0] + 2 * blk[1] + blk[2]]

        def copy(a, k, blk, to, src=None):
            return pltpu.make_async_remote_copy(
                src_ref=slot(a, blk) if src is None else src, dst_ref=slot(a, blk),
                send_sem=send_sems.at[a, k], recv_sem=recv_sems.at[a, k], device_id=to, device_id_type=MESH)

        mine = [pltpu.make_async_copy(ins[a], slot(a, me), local_sems.at[a]) for a in range(n)]
        return me, sib, ring, copy, mine

    def first_copies(ins, outs, sems):
        me, sib, ring, copy, mine = parts(ins, outs, sems)
        src, dst, _ = ring(me[2])
        return mine, [copy(a, k, me, to, src=ins[a]) for a in range(n) for k, to in ((1, src), (2, dst), (0, sib))]

    def start(ins, outs, sems):
        mine, first = first_copies(ins, outs, sems)
        for cp in mine + first:
            cp.start()

    def relay(ins, outs, sems):
        me, sib, ring, copy, _ = parts(ins, outs, sems)
        src, dst, _ = ring(me[2])
        for a in range(n):
            copy(a, 1, src, me).wait_recv()
            copy(a, 3, src, dst).start()
            copy(a, 4, src, sib).start()
        for a in range(n):
            copy(a, 2, dst, me).wait_recv()
            copy(a, 5, dst, sib).start()

    def finish(ins, outs, sems):
        me, sib, ring, copy, _ = parts(ins, outs, sems)
        mine, first = first_copies(ins, outs, sems)
        src, dst, diag = ring(me[2])
        passed = [copy(a, k, blk, to) for a in range(n) for k, blk, to in ((3, src, dst), (4, src, sib), (5, dst, sib))]
        for a in range(n):
            copy(a, 3, diag, me).wait_recv()
            fwd = copy(a, 6, diag, sib)
            fwd.start()
            passed.append(fwd)
        s_src, s_dst, s_diag = ring(sib[2])
        for a in range(n):
            for k, blk in ((0, sib), (4, s_src), (5, s_dst), (6, s_diag)):
                copy(a, k, blk, me).wait_recv()
        for cp in first + passed:
            cp.wait_send()
        for cp in mine:
            cp.wait()

    return _Comm(list(shards), [jax.ShapeDtypeStruct((N_DEV,) + s.shape, s.dtype) for s in shards],
                 [pltpu.SemaphoreType.DMA((n, 7)), pltpu.SemaphoreType.DMA((n, 7)), pltpu.SemaphoreType.DMA((n,))],
                 start, finish, relay)


COLUMN_SHARDED = ("w_in", "w_ff1")
COLUMN_RELAID = ("w_att_up",)


def _gather(shards, names):
    def unpack(got):
        return [jnp.transpose(g, (1, 0, 2)).reshape(g.shape[1], N_DEV * g.shape[2]) if k in COLUMN_RELAID
                else g.reshape(N_DEV * g.shape[1], g.shape[2]) for k, g in zip(names, got)]

    return _gather_relay_comm([shards[k] for k in names]), unpack


def _pair_comm(grads):
    n = len(grads)

    def copies(g, r1, sems):
        send_sems, recv_sems = sems
        x, y, c = _coords()
        return [pltpu.make_async_remote_copy(
            src_ref=g[a].at[2 * j + (1 - c)], dst_ref=r1[a].at[j], send_sem=send_sems.at[a, j],
            recv_sem=recv_sems.at[a, j], device_id=(x, y, 1 - c), device_id_type=MESH)
            for a in range(n) for j in range(4)]

    def start(g, r1, sems):
        for cp in copies(g, r1, sems):
            cp.start()

    def finish(g, r1, sems):
        for cp in copies(g, r1, sems):
            cp.wait_recv()
        for cp in copies(g, r1, sems):
            cp.wait_send()

    return _Comm(list(grads), [jax.ShapeDtypeStruct((4,) + g.shape[1:], g.dtype) for g in grads],
                 [pltpu.SemaphoreType.DMA((n, 4)), pltpu.SemaphoreType.DMA((n, 4))], start, finish)


def _small_comm(small):
    def parts(ins, outs, sems):
        (small_ref,), (small_all,) = ins, outs
        ssend, srecv, lsem = sems
        x, y, c = _coords()
        me_idx = 4 * x + 2 * y + c
        own = pltpu.make_async_copy(small_ref, small_all.at[me_idx], lsem)
        sends, recvs = [], []
        for r in range(1, N_DEV):
            px, py, pc = _flip(x, r & 4), _flip(y, r & 2), _flip(c, r & 1)
            sends.append(pltpu.make_async_remote_copy(
                src_ref=small_ref, dst_ref=small_all.at[me_idx], send_sem=ssend.at[r - 1], recv_sem=srecv.at[r - 1],
                device_id=(px, py, pc), device_id_type=MESH))
            recvs.append(pltpu.make_async_remote_copy(
                src_ref=small_ref, dst_ref=small_all.at[4 * px + 2 * py + pc], send_sem=ssend.at[r - 1],
                recv_sem=srecv.at[r - 1], device_id=(px, py, pc), device_id_type=MESH))
        return own, sends, recvs

    def start(ins, outs, sems):
        own, sends, _ = parts(ins, outs, sems)
        own.start()
        for cp in sends:
            cp.start()

    def finish(ins, outs, sems):
        own, sends, recvs = parts(ins, outs, sems)
        for cp in recvs:
            cp.wait_recv()
        for cp in sends:
            cp.wait_send()
        own.wait()

    return _Comm([small], [jax.ShapeDtypeStruct((N_DEV,) + small.shape, small.dtype)],
                 [pltpu.SemaphoreType.DMA((N_DEV - 1,)), pltpu.SemaphoreType.DMA((N_DEV - 1,)),
                  pltpu.SemaphoreType.DMA], start, finish)


def _pair_sum(grads, r1s, c_arr, tag):
    n = len(grads)
    q = 1

    def body(c_ref, *refs):
        g, r, t = refs[:n], refs[n:2 * n], refs[2 * n:]
        for a in range(n):
            t[a][...] = (g[a][...] + r[a][...]).astype(t[a].dtype)

    def blk(arr):
        return (1, arr.shape[1] // q, arr.shape[2])

    grid_spec = pltpu.PrefetchScalarGridSpec(
        num_scalar_prefetch=1, grid=(4, q),
        in_specs=[pl.BlockSpec(blk(g), lambda j, s, c_ref: (2 * j + c_ref[0], s, 0)) for g in grads]
        + [pl.BlockSpec(blk(r), lambda j, s, c_ref: (j, s, 0)) for r in r1s],
        out_specs=[pl.BlockSpec(blk(r), lambda j, s, c_ref: (j, s, 0)) for r in r1s])
    return pl.pallas_call(
        body, name="pair_sum_" + tag, grid_spec=grid_spec,
        out_shape=[jax.ShapeDtypeStruct(r.shape, RS_PAYLOAD) for r in r1s],
        compiler_params=_params(2),
    )(c_arr, *grads, *r1s)


def _pair_exchange_sum(sent, stacks, c_arr, tag):
    n = len(stacks)

    def body(c_ref, *refs):
        it = iter(refs)
        src, g, r1, t, buf = ([next(it) for _ in range(n)] for _ in range(5))
        send_sems, recv_sems, local_sems = it
        j = pl.program_id(0)
        x, y, c = _coords()

        def copy(a, blk):
            return pltpu.make_async_remote_copy(
                src_ref=src[a].at[2 * blk + (1 - c)], dst_ref=r1[a].at[blk], send_sem=send_sems.at[a, blk],
                recv_sem=recv_sems.at[a, blk], device_id=(x, y, 1 - c), device_id_type=MESH)

        @pl.when(j == 0)
        def _():
            for a in range(n):
                for blk in range(4):
                    copy(a, blk).start()

        for a in range(n):
            copy(a, j).wait_recv()
            landed = pltpu.make_async_copy(r1[a].at[j], buf[a], local_sems.at[a])
            landed.start()
            landed.wait()
            t[a][0] = (g[a][0] + buf[a][...]).astype(t[a].dtype)

        @pl.when(j == 3)
        def _():
            for a in range(n):
                for blk in range(4):
                    copy(a, blk).wait_send()

    def blk_spec(arr, index):
        return pl.BlockSpec((1,) + arr.shape[1:], index)

    grid_spec = pltpu.PrefetchScalarGridSpec(
        num_scalar_prefetch=1, grid=(4,),
        in_specs=[ANY] * n + [blk_spec(g, lambda j, c_ref: (2 * j + c_ref[0], 0, 0)) for g in stacks],
        out_specs=[ANY] * n + [blk_spec(g, lambda j, c_ref: (j, 0, 0)) for g in stacks],
        scratch_shapes=[pltpu.VMEM(s.shape[1:], s.dtype) for s in sent]
        + [pltpu.SemaphoreType.DMA((n, 4)), pltpu.SemaphoreType.DMA((n, 4)), pltpu.SemaphoreType.DMA((n,))])
    res = pl.pallas_call(
        body, name="pair_exchange_sum_" + tag, grid_spec=grid_spec,
        out_shape=[jax.ShapeDtypeStruct((4,) + s.shape[1:], s.dtype) for s in sent]
        + [jax.ShapeDtypeStruct((4,) + g.shape[1:], RS_PAYLOAD) for g in stacks],
        compiler_params=_params(1),
    )(c_arr, *sent, *stacks)
    return res[:n], res[n:]


def _chip_comm(ts):
    n = len(ts)

    def copies(t, r2, sems):
        send_sems, recv_sems = sems
        x, y, c = _coords()
        out = []
        for a in range(n):
            for r in range(1, 4):
                tx, ty = _flip(x, r & 2), _flip(y, r & 1)
                out.append(pltpu.make_async_remote_copy(
                    src_ref=t[a].at[2 * tx + ty], dst_ref=r2[a].at[r - 1], send_sem=send_sems.at[a, r - 1],
                    recv_sem=recv_sems.at[a, r - 1], device_id=(tx, ty, c), device_id_type=MESH))
        return out

    def start(t, r2, sems):
        for cp in copies(t, r2, sems):
            cp.start()

    def finish(t, r2, sems):
        for cp in copies(t, r2, sems):
            cp.wait_recv()
        for cp in copies(t, r2, sems):
            cp.wait_send()

    return _Comm(list(ts), [jax.ShapeDtypeStruct((3,) + t.shape[1:], t.dtype) for t in ts],
                 [pltpu.SemaphoreType.DMA((n, 3)), pltpu.SemaphoreType.DMA((n, 3))], start, finish)


class _Scatter:
    def __init__(self, grads, tag, sent=None):
        def stack(k, g):
            if k in COLUMN_RELAID:
                return jnp.transpose(g.reshape(g.shape[0], N_DEV, g.shape[1] // N_DEV), (1, 0, 2))
            return g.reshape(N_DEV, g.shape[0] // N_DEV, g.shape[1])

        self.names, self.tag = tuple(grads), tag
        self.stacks = [stack(k, g) for k, g in grads.items()]
        self.sent = self.stacks if sent is None else [stack(k, sent[k]) for k in grads]

    def pair_comm(self):
        return _pair_comm(self.sent)

    def pair_done(self, r1s):
        _Scatter.pairs_done((self,), (r1s,), self.tag)

    def exchange_and_sum(self):
        c_arr = jnp.reshape(lax.axis_index("c"), (1,)).astype(jnp.int32)
        r1s, ts = _pair_exchange_sum(self.sent, self.stacks, c_arr, self.tag)
        self.r1s, self.ts = list(r1s), list(ts)

    @staticmethod
    def pairs_done(groups, r1s_of, tag):
        for grp, r1s in zip(groups, r1s_of):
            grp.r1s = list(r1s)
        c_arr = jnp.reshape(lax.axis_index("c"), (1,)).astype(jnp.int32)
        ts = _pair_sum([s for grp in groups for s in grp.stacks], [r for grp in groups for r in grp.r1s], c_arr, tag)
        for grp in groups:
            grp.ts, ts = ts[:len(grp.stacks)], ts[len(grp.stacks):]

    def chip_comm(self):
        return _chip_comm(self.ts)

    def chip_done(self, r2s):
        self.r2s = list(r2s)


def _adamw(w, g, m, v):
    m = ADAM_B1 * m + (1.0 - ADAM_B1) * g
    v = ADAM_B2 * v + (1.0 - ADAM_B2) * jnp.square(g)
    m_hat = m / (1.0 - ADAM_B1 ** ADAM_STEP)
    v_hat = v / (1.0 - ADAM_B2 ** ADAM_STEP)
    delta = -ADAM_LR * (m_hat / (jnp.sqrt(v_hat) + ADAM_EPS) + ADAM_WD * w)
    return delta, m, v


ADAM_STEPS = 4


def _piece_specs(stacks):
    def rows(s):
        return s.shape[1] // ADAM_STEPS
    return ([pl.BlockSpec((1, rows(s), s.shape[2]), lambda i, idx_ref: (idx_ref[0], i, 0)) for s in stacks]
            + [pl.BlockSpec((1, rows(s), s.shape[2]), lambda i, idx_ref: (idx_ref[1], i, 0)) for s in stacks]
            + [pl.BlockSpec((3, rows(s), s.shape[2]), lambda i, idx_ref: (0, i, 0)) for s in stacks])


def _piece_sum(g0, r1, r2):
    return (((g0[0] + r1[0]) + r2[0].astype(F32)) + r2[1].astype(F32)) + r2[2].astype(F32)


def _shard_sum(stacks, r1s, r2s, idx_arr):
    n = len(stacks)

    def body(idx_ref, *refs):
        g0, r1, r2, outs = (refs[k * n:(k + 1) * n] for k in range(4))
        for a in range(n):
            outs[a][...] = _piece_sum(g0[a], r1[a], r2[a])

    out_specs = [pl.BlockSpec((s.shape[1] // ADAM_STEPS, s.shape[2]), lambda i, idx_ref: (i, 0)) for s in stacks]
    grid_spec = pltpu.PrefetchScalarGridSpec(num_scalar_prefetch=1, grid=(ADAM_STEPS,),
                                             in_specs=_piece_specs(stacks), out_specs=out_specs)
    return pl.pallas_call(
        body, name="shard_sum", grid_spec=grid_spec,
        out_shape=[jax.ShapeDtypeStruct(s.shape[1:], F32) for s in stacks],
        compiler_params=_params(1),
    )(idx_arr, *stacks, *r1s, *r2s)


def _adam_big(pieces, summed, ws, ms, vs, idx_arr):
    stacks, r1s, r2s = pieces
    n_p, n = len(stacks), len(ws)

    def body(idx_ref, *refs):
        it = iter(refs)
        g0, r1, r2, gs, w, m, v = ([next(it) for _ in range(k)] for k in (n_p, n_p, n_p, n - n_p, n, n, n))
        outs = list(it)
        for a in range(n):
            g = _piece_sum(g0[a], r1[a], r2[a]) if a < n_p else gs[a - n_p][...]
            delta, nm, nv = _adamw(w[a][...], g, m[a][...], v[a][...])
            outs[4 * a][...] = g
            outs[4 * a + 1][...] = delta
            outs[4 * a + 2][...] = nm
            outs[4 * a + 3][...] = nv

    def shard_spec(w):
        return pl.BlockSpec((w.shape[0] // ADAM_STEPS, w.shape[1]), lambda i, idx_ref: (i, 0))

    in_specs = _piece_specs(stacks) + [shard_spec(w) for w in ws[n_p:]] + [shard_spec(w) for w in ws] * 3
    out_specs = [shard_spec(w) for w in ws for _ in range(4)]
    grid_spec = pltpu.PrefetchScalarGridSpec(num_scalar_prefetch=1, grid=(ADAM_STEPS,), in_specs=in_specs,
                                             out_specs=out_specs)
    return pl.pallas_call(
        body, name="adam_big", grid_spec=grid_spec,
        out_shape=[jax.ShapeDtypeStruct(w.shape, F32) for w in ws for _ in range(4)],
        compiler_params=_params(1),
    )(idx_arr, *stacks, *r1s, *r2s, *summed, *ws, *ms, *vs)


def _adam_small(small_all, w, m, v):
    def body(all_ref, w_ref, m_ref, v_ref, g_ref, d_ref, nm_ref, nv_ref):
        g = all_ref[0]
        for k in range(1, N_DEV):
            g = g + all_ref[k]
        delta, nm, nv = _adamw(w_ref[...], g, m_ref[...], v_ref[...])
        g_ref[...] = g
        d_ref[...] = delta
        nm_ref[...] = nm
        nv_ref[...] = nv

    return pl.pallas_call(
        body, name="adam_small",
        in_specs=[VMEM_FULL] * 4, out_specs=[VMEM_FULL] * 4,
        out_shape=[jax.ShapeDtypeStruct(w.shape, F32)] * 4,
    )(small_all, w, m, v)


RS_PAYLOAD = BF


def _pack_rows(rows):
    rid = lax.broadcasted_iota(jnp.int32, (8, D_MODEL), 0)
    out = jnp.zeros((8, D_MODEL), F32)
    for i, r in enumerate(rows):
        out = jnp.where(rid == i, jnp.broadcast_to(r, (8, D_MODEL)), out)
    return out


def _small_rows(norm_mix_gain, b_gates, attn_sinks, ret_gn_gain, norm_mlp_gain, norm_final_gain):
    return [norm_mix_gain, b_gates[:, :D_MODEL], b_gates[:, D_MODEL:], ret_gn_gain, norm_mlp_gain,
            norm_final_gain.reshape(1, D_MODEL), jnp.pad(attn_sinks, ((0, 0), (0, D_MODEL - ATT_HEADS)))]


def _unpack_small(p):
    return dict(norm_mix_gain=p[0:1], b_gates=jnp.concatenate([p[1:2], p[2:3]], axis=1), ret_gn_gain=p[3:4],
                norm_mlp_gain=p[4:5], norm_final_gain=p[5], attn_sinks=p[6:7, :ATT_HEADS])


WEIGHTS = ("norm_mix_gain", "w_in", "b_gates", "attn_sinks", "ret_gn_gain", "w_att_up", "w_ret_up", "w_out",
           "norm_mlp_gain", "w_ff1", "w_ff2", "norm_final_gain")
BIG = ("w_in", "w_att_up", "w_ret_up", "w_out", "w_ff1", "w_ff2")


def kernel(x, norm_mix_gain, w_in, b_gates, attn_sinks, ret_gn_gain, w_att_up, w_ret_up, w_out, norm_mlp_gain, w_ff1, w_ff2, norm_final_gain, loss_target, m_norm_mix_gain, m_w_in, m_b_gates, m_attn_sinks, m_ret_gn_gain, m_w_att_up, m_w_ret_up, m_w_out, m_norm_mlp_gain, m_w_ff1, m_w_ff2, m_norm_final_gain, v_norm_mix_gain, v_w_in, v_b_gates, v_attn_sinks, v_ret_gn_gain, v_w_att_up, v_w_ret_up, v_w_out, v_norm_mlp_gain, v_w_ff1, v_w_ff2, v_norm_final_gain):
    w = dict(norm_mix_gain=norm_mix_gain, w_in=w_in, b_gates=b_gates, attn_sinks=attn_sinks, ret_gn_gain=ret_gn_gain,
             w_att_up=w_att_up, w_ret_up=w_ret_up, w_out=w_out, norm_mlp_gain=norm_mlp_gain, w_ff1=w_ff1,
             w_ff2=w_ff2, norm_final_gain=norm_final_gain)
    m = dict(norm_mix_gain=m_norm_mix_gain, w_in=m_w_in, b_gates=m_b_gates, attn_sinks=m_attn_sinks,
             ret_gn_gain=m_ret_gn_gain, w_att_up=m_w_att_up, w_ret_up=m_w_ret_up, w_out=m_w_out,
             norm_mlp_gain=m_norm_mlp_gain, w_ff1=m_w_ff1, w_ff2=m_w_ff2, norm_final_gain=m_norm_final_gain)
    v = dict(norm_mix_gain=v_norm_mix_gain, w_in=v_w_in, b_gates=v_b_gates, attn_sinks=v_attn_sinks,
             ret_gn_gain=v_ret_gn_gain, w_att_up=v_w_att_up, w_ret_up=v_w_ret_up, w_out=v_w_out,
             norm_mlp_gain=v_norm_mlp_gain, w_ff1=v_w_ff1, w_ff2=v_w_ff2, norm_final_gain=v_norm_final_gain)

    shards = {k: (w[k][0].T if k in COLUMN_SHARDED else w[k][0]).astype(BF) for k in BIG}
    loss_p, dx, small, groups = _local_step(
        x[0], loss_target[0], norm_mix_gain, b_gates, attn_sinks[0], ret_gn_gain, norm_mlp_gain,
        norm_final_gain.reshape(1, D_MODEL), shards)

    by_name = {}
    for grp in groups:
        for k, stack, r1, r2 in zip(grp.names, grp.stacks, grp.r1s, grp.r2s):
            by_name[k] = (stack, r1, r2)
    lane = lax.broadcasted_iota(jnp.int32, (1, D_MODEL), 1)
    sink_row = jnp.where(lane < ATT_HEADS, jnp.pad(small["attn_sinks"], ((0, 0), (0, D_MODEL - LANE))),
                         jnp.where(lane == ATT_HEADS, jnp.pad(loss_p, ((0, 0), (0, D_MODEL - LANE)), mode="edge"),
                                   0.0))
    small_pack = _pack_rows([small["norm_mix_gain"], small["b_gates"][:, :D_MODEL], small["b_gates"][:, D_MODEL:],
                             small["ret_gn_gain"], small["norm_mlp_gain"], small["norm_final_gain"], sink_row])
    xi, yi, ci = _coords()
    (small_all,) = _comm_alone(_small_comm(small_pack), "small_exchange")
    idx_arr = jnp.stack([4 * xi + 2 * yi + ci, 2 * xi + yi]).astype(jnp.int32)
    in_pieces = ["w_att_up", "w_ret_up", "w_out", "w_ff2", "w_in"]
    in_sum = ["w_ff1"]
    order = in_pieces + in_sum
    summed_t = _shard_sum(*[[by_name[k][i] for k in in_sum] for i in range(3)], idx_arr)

    def shard(tree, k):
        return tree[k][0].T if k == "w_in" else tree[k][0]

    adam_out = _adam_big([[by_name[k][i] for k in in_pieces] for i in range(3)], [g.T for g in summed_t],
                         [shard(w, k) for k in order], [shard(m, k) for k in order], [shard(v, k) for k in order],
                         idx_arr)
    big_out = [adam_out[4 * order.index(k) + i].T if k == "w_in" else adam_out[4 * order.index(k) + i]
               for k in BIG for i in range(4)]
    sm_g, sm_d, sm_m, sm_v = _adam_small(small_all, _pack_rows(_small_rows(*[w[k] for k in WEIGHTS if k not in BIG])),
                                         _pack_rows(_small_rows(*[m[k] for k in WEIGHTS if k not in BIG])),
                                         _pack_rows(_small_rows(*[v[k] for k in WEIGHTS if k not in BIG])))

    loss = sm_g[6, ATT_HEADS]
    kinds = []
    for idx, packed in enumerate((sm_g, sm_d, sm_m, sm_v)):
        out = _unpack_small(packed)
        for a, k in enumerate(BIG):
            out[k] = big_out[4 * a + idx][None]
        kinds.append(out)
    return (loss, dx[None], *[kinds[0][k] for k in WEIGHTS], *[kinds[1][k] for k in WEIGHTS],
            *[kinds[2][k] for k in WEIGHTS], *[kinds[3][k] for k in WEIGHTS])
```

```python
import functools
import math

import jax
import jax.numpy as jnp
from jax import lax
from jax.experimental import pallas as pl
from jax.experimental.pallas import tpu as pltpu

F32 = jnp.float32
BF = jnp.bfloat16
MESH = pl.DeviceIdType.MESH

D_MODEL = 1024
ATT_HEADS = 8
ATT_HEAD_DIM = 64
ATT_BLOCK = 128
ROPE_DIM = 16
ROPE_THETA = 500000.0
RET_HEADS = 4
RET_KEY_DIM = 128
RET_VAL_DIM = 256
RET_CHUNK = 128
RET_ROT_BASE = 10000.0
D_FF = 4096
NORM_EPS = 1e-6
GN_EPS = 1e-6
NEG_INF = -1e30
ATT_SCALE = ATT_HEAD_DIM ** -0.5
RET_SCALE = RET_KEY_DIM ** -0.5

C_QA, C_KA, C_VA, C_QR, C_KR, C_VR, C_GR, C_GATES, C_END = 0, 512, 640, 768, 1280, 1792, 2816, 3840, 5888

ADAM_LR = 0.001
ADAM_B1 = 0.9
ADAM_B2 = 0.999
ADAM_EPS = 1e-08
ADAM_WD = 0.01
ADAM_STEP = 10

N_DEV = 8
LANE = 128
VMEM_LIMIT = 56 * 1024 * 1024
TOKEN_TILE = 512
MLP_TOKEN_TILE = 256
TN_TOKEN_TILE = 4096
FF_CHUNKS = (0, 1024, 2048, 3072, 4096)
IN_CHUNKS_FWD = (C_QA, C_KA, C_QR, C_KR, C_VR, C_GR, C_GATES, C_END)
IN_CHUNKS_BWD = (C_QA, C_KA, C_QR, C_GATES, C_END)

RET_LOG_GAMMA = tuple(math.log1p(-(2.0 ** (-5.0 - h))) for h in range(RET_HEADS))
RET_CHUNK_DECAY = tuple(math.exp(RET_CHUNK * lg) for lg in RET_LOG_GAMMA)

VMEM_FULL = pl.BlockSpec(memory_space=pltpu.VMEM)
SMEM_FULL = pl.BlockSpec(memory_space=pltpu.SMEM)
ANY = pl.BlockSpec(memory_space=pl.ANY)


def _params(n_axes):
    return pltpu.CompilerParams(dimension_semantics=("arbitrary",) * n_axes, vmem_limit_bytes=VMEM_LIMIT)


def _nn(a, b):
    return jnp.dot(a, b, preferred_element_type=F32)


def _nt(a, b):
    return lax.dot_general(a, b, (((1,), (1,)), ((), ())), preferred_element_type=F32)


def _tn(a, b):
    return lax.dot_general(a, b, (((0,), (0,)), ((), ())), preferred_element_type=F32)


def _sigmoid(v):
    return 1.0 / (1.0 + jnp.exp(-v))


def _rows(tile, width):
    return pl.BlockSpec((tile, width), lambda i: (i, 0))


def _cols(height, tile):
    return pl.BlockSpec((height, tile), lambda i: (0, i))


class _Resident:
    def __init__(self, hbm_ref, vmem_ref, sems, bounds):
        self.hbm, self.vmem, self.sems, self.bounds = hbm_ref, vmem_ref, sems, bounds

    @staticmethod
    def scratch(w, bounds):
        return [pltpu.VMEM(w.shape, w.dtype), pltpu.SemaphoreType.DMA((len(bounds) - 1,))]

    @staticmethod
    def load(*weights):
        @pl.when(pl.program_id(0) == 0)
        def _():
            copies = [w._copy(c) for w in weights for c in range(len(w.bounds) - 1)]
            for cp in copies:
                cp.start()
            for cp in copies:
                cp.wait()

    def _rows(self, c):
        return pl.ds(self.bounds[c], self.bounds[c + 1] - self.bounds[c])

    def _copy(self, c):
        return pltpu.make_async_copy(self.hbm.at[self._rows(c)], self.vmem.at[self._rows(c)], self.sems.at[c])

    def chunk(self, c):
        return self.vmem.at[self._rows(c)]


class _Comm:
    def __init__(self, inputs, out_shapes, scratch, start, finish, relay=None):
        self.inputs, self.out_shapes, self.scratch = inputs, out_shapes, scratch
        self.start, self.finish, self.relay = start, finish, relay


RELAY_AT = 0.7


def _join(a, b):
    na_in, na_out, na_sem = len(a.inputs), len(a.out_shapes), len(a.scratch)

    def both(name):
        def run(ins, outs, sems):
            for part, args in ((a, (ins[:na_in], outs[:na_out], sems[:na_sem])),
                               (b, (ins[na_in:], outs[na_out:], sems[na_sem:]))):
                if getattr(part, name) is not None:
                    getattr(part, name)(*args)
        return run

    return _Comm(list(a.inputs) + list(b.inputs), list(a.out_shapes) + list(b.out_shapes),
                 list(a.scratch) + list(b.scratch), both("start"), both("finish"),
                 both("relay") if a.relay or b.relay else None)


def _call(body, args, comm=None, *, name, grid, in_specs, out_specs, out_shape, scratch_shapes=()):
    params = _params(len(grid))
    if comm is None:
        return pl.pallas_call(body, name=name, grid=grid, in_specs=in_specs, out_specs=out_specs, out_shape=out_shape,
                              scratch_shapes=scratch_shapes, compiler_params=params)(*args), ()
    single = not isinstance(out_specs, (list, tuple))
    out_specs_l = [out_specs] if single else list(out_specs)
    out_shape_l = [out_shape] if single else list(out_shape)
    n_in, n_out, n_scr = len(in_specs), len(out_specs_l), len(scratch_shapes)
    n_cin, n_cout = len(comm.inputs), len(comm.out_shapes)

    def hosted(*refs):
        it = iter(refs)
        ins, cin, outs, cout, scr = ([next(it) for _ in range(k)] for k in (n_in, n_cin, n_out, n_cout, n_scr))
        sems = list(it)
        ids = [pl.program_id(k) for k in range(len(grid))]
        first = functools.reduce(jnp.logical_and, [i == 0 for i in ids])
        last = functools.reduce(jnp.logical_and, [i == g - 1 for i, g in zip(ids, grid)])

        @pl.when(first)
        def _():
            comm.start(cin, cout, sems)

        if comm.relay is not None:
            at = [int(grid[0] * RELAY_AT)] + [0] * (len(grid) - 1)

            @pl.when(functools.reduce(jnp.logical_and, [i == v for i, v in zip(ids, at)]))
            def _():
                comm.relay(cin, cout, sems)

        body(*ins, *outs, *scr)

        @pl.when(last)
        def _():
            comm.finish(cin, cout, sems)

    res = pl.pallas_call(
        hosted, name=name, grid=grid, in_specs=list(in_specs) + [ANY] * n_cin,
        out_specs=out_specs_l + [ANY] * n_cout, out_shape=out_shape_l + list(comm.out_shapes),
        scratch_shapes=list(scratch_shapes) + list(comm.scratch), compiler_params=params)(*args, *comm.inputs)
    return (res[0] if single else res[:n_out]), res[n_out:]


def _comm_alone(comm, name):
    n_cin, n_cout = len(comm.inputs), len(comm.out_shapes)

    def body(*refs):
        cin, cout, sems = refs[:n_cin], refs[n_cin:n_cin + n_cout], refs[n_cin + n_cout:]
        comm.start(cin, cout, sems)
        if comm.relay is not None:
            comm.relay(cin, cout, sems)
        comm.finish(cin, cout, sems)

    return pl.pallas_call(body, name=name, in_specs=[ANY] * n_cin, out_specs=[ANY] * n_cout,
                          out_shape=list(comm.out_shapes), scratch_shapes=list(comm.scratch))(*comm.inputs)


def _slabs(v, fn):
    return jnp.concatenate([fn(v[:, LANE * j:LANE * (j + 1)]) for j in range(v.shape[1] // LANE)], axis=1)


def _rope_att(v, ca, sa, sb):
    return _slabs(v, lambda t: t * ca + pltpu.roll(t, LANE - 8, 1) * sa + pltpu.roll(t, 8, 1) * sb)


def _rope_att_t(v, ca, sa, sb):
    return _slabs(v, lambda t: t * ca + pltpu.roll(t * sa, 8, 1) + pltpu.roll(t * sb, LANE - 8, 1))


def _rope_att_rows(v, ct, st, sign):
    parts = []
    for h in range(v.shape[0] // ATT_HEAD_DIM):
        r0 = ATT_HEAD_DIM * h
        x1, x2 = v[r0:r0 + 8], v[r0 + 8:r0 + 16]
        parts += [x1 * ct - sign * (x2 * st), x2 * ct + sign * (x1 * st), v[r0 + 16:r0 + ATT_HEAD_DIM]]
    return jnp.concatenate(parts, axis=0)


def _rope_ret(v, cr, sr):
    return _slabs(v, lambda t: t * cr + pltpu.roll(t, 64, 1) * sr)


def _rope_ret_t(v, cr, sr):
    return _slabs(v, lambda t: t * cr + pltpu.roll(t * sr, 64, 1))


def _rope_lane_tables():
    def inv(dim, theta):
        return theta ** (-jnp.arange(0, dim, 2, dtype=F32) / dim)

    inv_a, inv_r = inv(ROPE_DIM, ROPE_THETA), inv(RET_KEY_DIM, RET_ROT_BASE)
    half = ROPE_DIM // 2
    zeros = jnp.zeros((ATT_HEAD_DIM - ROPE_DIM,), F32)
    freq64 = jnp.concatenate([inv_a, inv_a, zeros])
    lo64 = jnp.concatenate([-jnp.ones((half,), F32), jnp.zeros((half,), F32), zeros])
    hi64 = jnp.concatenate([jnp.zeros((half,), F32), jnp.ones((half,), F32), zeros])
    sign_r = jnp.concatenate([-jnp.ones((64,), F32), jnp.ones((64,), F32)])
    rows = [jnp.tile(freq64, 2), jnp.tile(lo64, 2), jnp.tile(hi64, 2), jnp.tile(inv_r, 2), sign_r]
    lanes = jnp.stack(rows + [jnp.zeros((LANE,), F32)] * (8 - len(rows)))
    return lanes, jnp.broadcast_to(inv_a[:, None], (half, LANE))


def _prologue(x, g1, comm):
    seq = x.shape[0]
    tm = min(TOKEN_TILE, seq)

    def body(x_ref, g_ref, lanes_ref, freq_rows_ref, xn_ref, ca_ref, sa_ref, sb_ref, cr_ref, sr_ref, ct_ref, st_ref):
        xf = x_ref[...]
        r = lax.rsqrt(jnp.mean(xf * xf, axis=-1, keepdims=True) + NORM_EPS)
        xn_ref[...] = (xf * r * g_ref[...]).astype(BF)
        row0 = pl.program_id(0) * tm
        pos = (row0 + lax.broadcasted_iota(jnp.int32, (tm, LANE), 0)).astype(F32)
        ang_a = pos * lanes_ref[0:1, :]
        sin_a = jnp.sin(ang_a)
        ca_ref[...] = jnp.cos(ang_a)
        sa_ref[...] = sin_a * lanes_ref[1:2, :]
        sb_ref[...] = sin_a * lanes_ref[2:3, :]
        ang_r = pos * lanes_ref[3:4, :]
        cr_ref[...] = jnp.cos(ang_r)
        sr_ref[...] = jnp.sin(ang_r) * lanes_ref[4:5, :]
        pos_t = (row0 + lax.broadcasted_iota(jnp.int32, (8, tm), 1)).astype(F32)
        ang_t = pos_t * jnp.concatenate([freq_rows_ref[...]] * (tm // LANE), axis=1)
        ct_ref[...] = jnp.cos(ang_t)
        st_ref[...] = jnp.sin(ang_t)

    tab = _rows(tm, LANE)
    tab_t = _cols(8, tm)
    return _call(
        body, (x, g1, *_rope_lane_tables()), comm, name="prologue", grid=(seq // tm,),
        in_specs=[_rows(tm, D_MODEL), VMEM_FULL, VMEM_FULL, VMEM_FULL],
        out_specs=[_rows(tm, D_MODEL), tab, tab, tab, tab, tab, tab_t, tab_t],
        out_shape=[jax.ShapeDtypeStruct((seq, D_MODEL), BF)] + [jax.ShapeDtypeStruct((seq, LANE), F32)] * 5
        + [jax.ShapeDtypeStruct((8, seq), F32)] * 2)


def _ret_tables():
    c = RET_CHUNK
    lg = jnp.asarray(RET_LOG_GAMMA, F32)
    idx = jnp.arange(c, dtype=F32)
    diff = idx[:, None] - idx[None, :]
    intra = jnp.where(diff >= 0, jnp.exp(jnp.maximum(diff, 0.0) * lg[:, None, None]), 0.0)
    qd = jnp.exp((idx + 1.0)[None, :] * lg[:, None])[..., None]
    kd = jnp.exp((c - 1.0 - idx)[None, :] * lg[:, None])[..., None]
    return intra, jnp.broadcast_to(qd, (RET_HEADS, c, RET_KEY_DIM)), jnp.broadcast_to(kd, (RET_HEADS, c, RET_KEY_DIM))


def _inproj_fwd(xn1, w_in, tabs, comm=None):
    seq = xn1.shape[0]
    tm = min(TOKEN_TILE, seq)

    def body(xn_ref, w_hbm, ca_ref, sa_ref, sb_ref, cr_ref, sr_ref, ct_ref, st_ref,
             qt_ref, kv_ref, kvt_ref, qkr_ref, vr_ref, gr_ref, gates_ref, w_vmem, w_sems):
        w = _Resident(w_hbm, w_vmem, w_sems, IN_CHUNKS_FWD)
        _Resident.load(w)
        ca, sa, sb, cr, sr, ct, st = (ref[...] for ref in (ca_ref, sa_ref, sb_ref, cr_ref, sr_ref, ct_ref, st_ref))
        xn = xn_ref[...]
        qt_ref[...] = _rope_att_rows(_nt(w.chunk(0)[...], xn), ct, st, 1.0).astype(BF)
        w_kv = w.chunk(1)
        kvt = _nt(w_kv[...], xn)
        kvt_ref[:LANE, :] = _rope_att_rows(kvt[:LANE], ct, st, 1.0).astype(BF)
        kvt_ref[LANE:, :] = kvt[LANE:].astype(BF)
        kvn = _nt(xn, w_kv[...])
        kv_ref[:, :LANE] = _rope_att(kvn[:, :LANE], ca, sa, sb).astype(BF)
        kv_ref[:, LANE:] = kvn[:, LANE:].astype(BF)
        qr = _rope_ret(_nt(xn, w.chunk(2)[...]), cr, sr)
        qkr_ref[:, :512] = qr.astype(BF)
        kr = _rope_ret(_nt(xn, w.chunk(3)[...]), cr, sr) * RET_SCALE
        qkr_ref[:, 512:] = kr.astype(BF)
        vr_ref[...] = _nt(xn, w.chunk(4)[...]).astype(BF)
        gr_ref[...] = _nt(xn, w.chunk(5)[...]).astype(BF)
        gates_ref[...] = _nt(xn, w.chunk(6)[...]).astype(BF)

    tab = _rows(tm, LANE)
    tab_t = _cols(8, tm)
    return _call(
        body, (xn1, w_in, *tabs), comm, name="inproj_fwd", grid=(seq // tm,),
        in_specs=[_rows(tm, D_MODEL), ANY, tab, tab, tab, tab, tab, tab_t, tab_t],
        out_specs=[_cols(512, tm), _rows(tm, 256), _cols(256, tm), _rows(tm, 1024),
                   _rows(tm, 1024), _rows(tm, 1024), _rows(tm, 2048)],
        out_shape=[jax.ShapeDtypeStruct(s, BF) for s in ((512, seq), (seq, 256), (256, seq),
                                                         (seq, 1024), (seq, 1024), (seq, 1024), (seq, 2048))],
        scratch_shapes=_Resident.scratch(w_in, IN_CHUNKS_FWD))


ATT_Q_PER_KV = ATT_HEADS // 2


def _att_group(ref, kv):
    d = ATT_HEAD_DIM
    g = jnp.concatenate([ref[d * (ATT_Q_PER_KV * kv + j):d * (ATT_Q_PER_KV * kv + j + 1), :]
                         for j in range(ATT_Q_PER_KV)], axis=1)
    z = jnp.zeros_like(g)
    return jnp.concatenate([g, z] if kv == 0 else [z, g], axis=0)


def _att_probs(kwin, qpad, sink_ref, kv, i):
    c = ATT_BLOCK
    kj = lax.broadcasted_iota(jnp.int32, (2 * c, c), 0)
    qi = lax.broadcasted_iota(jnp.int32, (2 * c, c), 1)
    allowed = (kj > qi) & (kj <= qi + c) & ((kj >= c) | (i > 0))
    allowed = jnp.concatenate([allowed] * ATT_Q_PER_KV, axis=1)
    s = jnp.where(allowed, _nn(kwin, qpad) * ATT_SCALE, NEG_INF)
    sk = jnp.concatenate([jnp.full((1, c), sink_ref[ATT_Q_PER_KV * kv + j], F32) for j in range(ATT_Q_PER_KV)], axis=1)
    m = jnp.maximum(jnp.max(s, axis=0, keepdims=True), sk)
    pe = jnp.exp(s - m)
    psink = jnp.exp(sk - m)
    inv = 1.0 / (jnp.sum(pe, axis=0, keepdims=True) + psink)
    return pe * inv, psink * inv


def _seq_fwd(qt, kv, kvt, sinks, qkr, vr, gr, gain, rtabs, comm=None):
    seq = kv.shape[0]
    c = ATT_BLOCK
    assert c == RET_CHUNK
    d = ATT_HEAD_DIM
    nb = seq // c
    dk, dv = RET_KEY_DIM, RET_VAL_DIM

    def body(sink_ref, qt_ref, kvc_ref, kvp_ref, kvtc_ref, kvtp_ref, qk_ref, v_ref, g_ref, gain_ref, intra_ref,
             qd_ref, kd_ref, ot_ref, ret_ref, yrp_ref, st_ref, state):
        i = pl.program_id(0)
        kwin = jnp.concatenate([kvp_ref[:, :LANE], kvc_ref[:, :LANE]], axis=0)
        vtwin = jnp.concatenate([kvtp_ref[LANE:, :], kvtc_ref[LANE:, :]], axis=1)
        for g in (0, 1):
            prob, _ = _att_probs(kwin, _att_group(qt_ref, g), sink_ref, g, i)
            out = _nn(vtwin[d * g:d * (g + 1), :], prob.astype(BF))
            for j in range(ATT_Q_PER_KV):
                h = ATT_Q_PER_KV * g + j
                ot_ref[d * h:d * (h + 1), :] = out[:, c * j:c * (j + 1)].astype(BF)

        @pl.when(i == 0)
        def _():
            state[...] = jnp.zeros_like(state)

        for h in range(RET_HEADS):
            qh = qk_ref[:, dk * h:dk * (h + 1)]
            kh = qk_ref[:, 512 + dk * h:512 + dk * (h + 1)]
            vh = v_ref[:, dv * h:dv * (h + 1)]
            sh = state[h]
            shb = sh.astype(BF)
            st_ref[0, h] = shb
            att = _nt(qh, kh) * intra_ref[h]
            inner = _nn(att.astype(BF), vh)
            cross = _nn((qh.astype(F32) * qd_ref[h]).astype(BF), shb)
            out = inner + cross
            state[h] = sh * RET_CHUNK_DECAY[h] + _tn((kh.astype(F32) * kd_ref[h]).astype(BF), vh)
            ret_ref[:, dv * h:dv * (h + 1)] = out
            mu = jnp.mean(out, axis=-1, keepdims=True)
            dev = out - mu
            var = jnp.mean(dev * dev, axis=-1, keepdims=True)
            y = dev * lax.rsqrt(var + GN_EPS) * gain_ref[:, dv * h:dv * (h + 1)]
            g = g_ref[:, dv * h:dv * (h + 1)].astype(F32)
            yrp_ref[:, dv * h:dv * (h + 1)] = (g * _sigmoid(g) * y).astype(BF)

    prev = lambda i: jnp.maximum(i - 1, 0)
    return _call(
        body, (sinks, qt, kv, kv, kvt, kvt, qkr, vr, gr, gain, *rtabs), comm, name="seq_fwd", grid=(nb,),
        in_specs=[SMEM_FULL, _cols(512, c), _rows(c, 256), pl.BlockSpec((c, 256), lambda i: (prev(i), 0)),
                  _cols(256, c), pl.BlockSpec((256, c), lambda i: (0, prev(i))),
                  _rows(c, 1024), _rows(c, 1024), _rows(c, 1024), VMEM_FULL, VMEM_FULL, VMEM_FULL, VMEM_FULL],
        out_specs=[_cols(512, c), _rows(c, 1024), _rows(c, 1024),
                   pl.BlockSpec((1, RET_HEADS, dk, dv), lambda i: (i, 0, 0, 0))],
        out_shape=[jax.ShapeDtypeStruct((512, seq), BF), jax.ShapeDtypeStruct((seq, 1024), F32),
                   jax.ShapeDtypeStruct((seq, 1024), BF), jax.ShapeDtypeStruct((nb, RET_HEADS, dk, dv), BF)],
        scratch_shapes=[pltpu.VMEM((RET_HEADS, dk, dv), F32)])


def _mix_fwd(oat, yrp, gates, bg, x, w_att, w_ret, w_out):
    seq = x.shape[0]
    tm = min(TOKEN_TILE, seq)

    def body(oat_ref, yrp_ref, gates_ref, bg_ref, x_ref, wa_ref, wr_ref, wo_ref, ya_ref, yr_ref, mg_ref, h1_ref):
        ya = _tn(oat_ref[...], wa_ref[...])
        yr = _nn(yrp_ref[...], wr_ref[...])
        gt = _sigmoid(gates_ref[...].astype(F32) + bg_ref[...])
        merged = (gt[:, :D_MODEL] * ya + gt[:, D_MODEL:] * yr).astype(BF)
        ya_ref[...] = ya.astype(BF)
        yr_ref[...] = yr.astype(BF)
        mg_ref[...] = merged
        h1_ref[...] = x_ref[...] + _nn(merged, wo_ref[...])

    return pl.pallas_call(
        body, name="mix_fwd", grid=(seq // tm,),
        in_specs=[_cols(512, tm), _rows(tm, 1024), _rows(tm, 2048), VMEM_FULL, _rows(tm, D_MODEL),
                  VMEM_FULL, VMEM_FULL, VMEM_FULL],
        out_specs=[_rows(tm, D_MODEL)] * 4,
        out_shape=[jax.ShapeDtypeStruct((seq, D_MODEL), BF)] * 3 + [jax.ShapeDtypeStruct((seq, D_MODEL), F32)],
        compiler_params=_params(1),
    )(oat, yrp, gates, bg, x, w_att, w_ret, w_out)


def _mlp_loss_step(h1, g2, g3, target, w_ff1, w_ff2):
    seq = h1.shape[0]
    tm = min(MLP_TOKEN_TILE, seq)
    n_chunks = len(FF_CHUNKS) - 1

    def body(h1_ref, g2_ref, g3_ref, t_ref, w1_hbm, w2_hbm,
             xn2_ref, hdn_ref, dh2_ref, du_ref, dh1_ref, loss_ref, dg3_ref, dg2_ref,
             w1_vmem, w1_sems, w2_vmem, w2_sems, relu_u):
        w1 = _Resident(w1_hbm, w1_vmem, w1_sems, FF_CHUNKS)
        w2 = _Resident(w2_hbm, w2_vmem, w2_sems, FF_CHUNKS)
        _Resident.load(w1, w2)

        @pl.when(pl.program_id(0) == 0)
        def _():
            loss_ref[...] = jnp.zeros_like(loss_ref)
            dg3_ref[...] = jnp.zeros_like(dg3_ref)
            dg2_ref[...] = jnp.zeros_like(dg2_ref)

        h1v = h1_ref[...]
        r2 = lax.rsqrt(jnp.mean(h1v * h1v, axis=-1, keepdims=True) + NORM_EPS)
        xn2 = (h1v * r2 * g2_ref[...]).astype(BF)
        xn2_ref[...] = xn2
        h2 = h1v
        for c in range(n_chunks):
            cols = slice(FF_CHUNKS[c], FF_CHUNKS[c + 1])
            a = jnp.maximum(_nt(xn2, w1.chunk(c)[...]), 0.0)
            relu_u[:, cols] = a.astype(BF)
            hdn = jnp.square(a).astype(BF)
            hdn_ref[:, cols] = hdn
            h2 = h2 + _nn(hdn, w2.chunk(c)[...])
        r3 = lax.rsqrt(jnp.mean(h2 * h2, axis=-1, keepdims=True) + NORM_EPS)
        hn = h2 * r3
        err = hn * g3_ref[...] - t_ref[...]
        loss_ref[...] += jnp.sum(err * err) * (0.5 / D_MODEL)
        dy = err * (1.0 / D_MODEL)
        dg3_ref[...] += jnp.sum(dy * hn, axis=0, keepdims=True)
        z = dy * g3_ref[...]
        dh2 = r3 * (z - hn * jnp.mean(z * hn, axis=-1, keepdims=True))
        dh2b = dh2.astype(BF)
        dh2_ref[...] = dh2b
        dxn2 = jnp.zeros_like(dh2)
        for c in range(n_chunks):
            cols = slice(FF_CHUNKS[c], FF_CHUNKS[c + 1])
            du = (_nt(dh2b, w2.chunk(c)[...]) * (2.0 * relu_u[:, cols].astype(F32))).astype(BF)
            du_ref[:, cols] = du
            dxn2 = dxn2 + _nn(du, w1.chunk(c)[...])
        dnorm, dg = _rms_bwd(dxn2, h1v, g2_ref[...])
        dh1_ref[...] = dh2 + dnorm
        dg2_ref[...] += dg

    gain_acc = pl.BlockSpec((1, D_MODEL), lambda i: (0, 0))
    return pl.pallas_call(
        body, name="mlp_loss_step", grid=(seq // tm,),
        in_specs=[_rows(tm, D_MODEL), VMEM_FULL, VMEM_FULL, _rows(tm, D_MODEL), ANY, ANY],
        out_specs=[_rows(tm, D_MODEL), _rows(tm, D_FF), _rows(tm, D_MODEL), _rows(tm, D_FF), _rows(tm, D_MODEL),
                   pl.BlockSpec((1, LANE), lambda i: (0, 0)), gain_acc, gain_acc],
        out_shape=[jax.ShapeDtypeStruct((seq, D_MODEL), BF), jax.ShapeDtypeStruct((seq, D_FF), BF),
                   jax.ShapeDtypeStruct((seq, D_MODEL), BF), jax.ShapeDtypeStruct((seq, D_FF), BF),
                   jax.ShapeDtypeStruct((seq, D_MODEL), F32), jax.ShapeDtypeStruct((1, LANE), F32),
                   jax.ShapeDtypeStruct((1, D_MODEL), F32), jax.ShapeDtypeStruct((1, D_MODEL), F32)],
        scratch_shapes=(_Resident.scratch(w_ff1, FF_CHUNKS) + _Resident.scratch(w_ff2, FF_CHUNKS)
                        + [pltpu.VMEM((tm, D_FF), BF)]),
        compiler_params=_params(1),
    )(h1, g2, g3, target, w_ff1, w_ff2)


def _rms_bwd(dxn, xin, gain):
    r = lax.rsqrt(jnp.mean(xin * xin, axis=-1, keepdims=True) + NORM_EPS)
    xhat = xin * r
    z = dxn * gain
    dxin = r * (z - xhat * jnp.mean(z * xhat, axis=-1, keepdims=True))
    return dxin, jnp.sum(dxn * xhat, axis=0, keepdims=True)


def _mm_tn(a, b, name, a_is_t=False, comm=None):
    kdim, seq = a.shape if a_is_t else a.shape[::-1]
    ndim = b.shape[1]
    ts = min(TN_TOKEN_TILE, seq)
    tk = min(kdim, 1024)
    tn = min(ndim, 1024)
    n_steps = seq // ts

    def body(a_ref, b_ref, o_ref):
        mm = _nn if a_is_t else _tn
        part = mm(a_ref[...].astype(BF), b_ref[...].astype(BF))
        if n_steps == 1:
            o_ref[...] = part
        else:
            @pl.when(pl.program_id(2) == 0)
            def _():
                o_ref[...] = jnp.zeros_like(o_ref)

            o_ref[...] += part

    a_spec = (pl.BlockSpec((tk, ts), lambda k, n, s: (k, s)) if a_is_t
              else pl.BlockSpec((ts, tk), lambda k, n, s: (s, k)))
    res = _call(
        body, (a, b), comm, name=name, grid=(kdim // tk, ndim // tn, n_steps),
        in_specs=[a_spec, pl.BlockSpec((ts, tn), lambda k, n, s: (s, n))],
        out_specs=pl.BlockSpec((tk, tn), lambda k, n, s: (k, n)),
        out_shape=jax.ShapeDtypeStruct((kdim, ndim), F32))
    return res[0] if comm is None else res


def _dw_in_t(pieces, is_t, xn1, comm=None):
    seq = xn1.shape[0]
    tr = 256
    n_blk = [(p.shape[0] if t else p.shape[1]) // tr for p, t in zip(pieces, is_t)]
    offs = [sum(n_blk[:i]) for i in range(len(pieces) + 1)]
    n_p = len(pieces)

    def body(*refs):
        piece_refs, xn_ref, o_ref, ob_ref = refs[:n_p], refs[n_p], refs[n_p + 1], refs[n_p + 2]
        r = pl.program_id(0)
        for p in range(n_p):
            @pl.when((r >= offs[p]) & (r < offs[p + 1]))
            def _(p=p):
                res = (_nn if is_t[p] else _tn)(piece_refs[p][...], xn_ref[...])
                o_ref[...] = res
                ob_ref[...] = res.astype(BF)

    def piece_spec(p):
        def index(r):
            blk = jnp.clip(r - offs[p], 0, n_blk[p] - 1)
            return (blk, 0) if is_t[p] else (0, blk)
        return pl.BlockSpec((tr, seq) if is_t[p] else (seq, tr), index)

    return _call(
        body, (*pieces, xn1), comm, name="dw_in", grid=(offs[-1],),
        in_specs=[piece_spec(p) for p in range(n_p)] + [VMEM_FULL],
        out_specs=[_rows(tr, D_MODEL), _rows(tr, D_MODEL)],
        out_shape=[jax.ShapeDtypeStruct((offs[-1] * tr, D_MODEL), F32),
                   jax.ShapeDtypeStruct((offs[-1] * tr, D_MODEL), BF)])


def _mix_bwd(dh1, ya, yr, gates, bg, merged, oat, yrp, w_att, w_ret, w_out, comm=None):
    seq = dh1.shape[0]
    tm = min(TOKEN_TILE, seq)

    def body(dh1_ref, ya_ref, yr_ref, gates_ref, bg_ref, mg_ref, oat_ref, yrp_ref, wa_ref, wr_ref, wo_ref,
             dgates_ref, doa_ref, dyrp_ref, db_ref, dwa_ref, dwr_ref, dwo_ref):
        @pl.when(pl.program_id(0) == 0)
        def _():
            for ref in (db_ref, dwa_ref, dwr_ref, dwo_ref):
                ref[...] = jnp.zeros_like(ref)

        dh1b = dh1_ref[...].astype(BF)
        dm = _nt(dh1b, wo_ref[...])
        gt = _sigmoid(gates_ref[...].astype(F32) + bg_ref[...])
        ga, gr = gt[:, :D_MODEL], gt[:, D_MODEL:]
        dya = (dm * ga).astype(BF)
        dyr = (dm * gr).astype(BF)
        dga = dm * ya_ref[...].astype(F32) * ga * (1.0 - ga)
        dgr = dm * yr_ref[...].astype(F32) * gr * (1.0 - gr)
        dgates_ref[:, :D_MODEL] = dga.astype(BF)
        dgates_ref[:, D_MODEL:] = dgr.astype(BF)
        db_ref[:, :D_MODEL] += jnp.sum(dga, axis=0, keepdims=True)
        db_ref[:, D_MODEL:] += jnp.sum(dgr, axis=0, keepdims=True)
        doa_ref[...] = _nt(wa_ref[...], dya).astype(BF)
        dyrp_ref[...] = _nt(dyr, wr_ref[...]).astype(BF)
        dwo_ref[...] += _tn(mg_ref[...], dh1b)
        dwr_ref[...] += _tn(yrp_ref[...], dyr)
        dwa_ref[...] += _nn(oat_ref[...], dya)

    acc = lambda r, c: pl.BlockSpec((r, c), lambda i: (0, 0))
    return _call(
        body, (dh1, ya, yr, gates, bg, merged, oat, yrp, w_att, w_ret, w_out), comm, name="mix_bwd",
        grid=(seq // tm,),
        in_specs=[_rows(tm, D_MODEL), _rows(tm, D_MODEL), _rows(tm, D_MODEL), _rows(tm, 2048), VMEM_FULL,
                  _rows(tm, D_MODEL), _cols(512, tm), _rows(tm, 1024), VMEM_FULL, VMEM_FULL, VMEM_FULL],
        out_specs=[_rows(tm, 2048), _cols(512, tm), _rows(tm, 1024), acc(1, 2048), acc(512, D_MODEL),
                   acc(1024, D_MODEL), acc(D_MODEL, D_MODEL)],
        out_shape=[jax.ShapeDtypeStruct((seq, 2048), BF), jax.ShapeDtypeStruct((512, seq), BF),
                   jax.ShapeDtypeStruct((seq, 1024), BF), jax.ShapeDtypeStruct((1, 2048), F32),
                   jax.ShapeDtypeStruct((512, D_MODEL), F32), jax.ShapeDtypeStruct((1024, D_MODEL), F32),
                   jax.ShapeDtypeStruct((D_MODEL, D_MODEL), F32)])


def _seq_bwd(qt, kv, kvt, dot, sinks, qkr, vr, gr, ret, dyrp, states, gain, rtabs, tabs, comm=None):
    seq = kv.shape[0]
    c = ATT_BLOCK
    assert c == RET_CHUNK
    d = ATT_HEAD_DIM
    nb = seq // c
    dk, dv = RET_KEY_DIM, RET_VAL_DIM
    ca, sa, sb, cr, sr, ct, st = tabs

    def body(sink_ref, qt_ref, kvc_ref, kvp_ref, kvtc_ref, kvtp_ref, dot_ref, ca_ref, sa_ref, sb_ref, ct_ref, st_ref,
             qk_ref, v_ref, g_ref, ret_ref, dyp_ref, st8_ref, gain_ref, intra_ref, qd_ref, kd_ref, cr_ref, sr_ref,
             dqt_ref, dkv_ref, dsink_ref, dall_ref, dgain_ref, carry, dstate):
        step = pl.program_id(0)
        i = nb - 1 - step

        @pl.when(step == 0)
        def _():
            carry[...] = jnp.zeros_like(carry)
            dsink_ref[...] = jnp.zeros_like(dsink_ref)
            dstate[...] = jnp.zeros_like(dstate)
            dgain_ref[...] = jnp.zeros_like(dgain_ref)

        kwin = jnp.concatenate([kvp_ref[:, :LANE], kvc_ref[:, :LANE]], axis=0)
        vwin = jnp.concatenate([kvp_ref[:, LANE:], kvc_ref[:, LANE:]], axis=0)
        ktwin = jnp.concatenate([kvtp_ref[:LANE, :], kvtc_ref[:LANE, :]], axis=1)
        ctv, stv = ct_ref[...], st_ref[...]
        lane1 = lax.broadcasted_iota(jnp.int32, (1, LANE), 1)
        dkw = jnp.zeros((2 * c, LANE), F32)
        dvw = jnp.zeros((2 * c, LANE), F32)
        dsink = jnp.zeros((1, LANE), F32)
        for g in (0, 1):
            qpad, dopad = _att_group(qt_ref, g), _att_group(dot_ref, g)
            prob, psink = _att_probs(kwin, qpad, sink_ref, g, i)
            dprob = _nn(vwin, dopad)
            drow = jnp.sum(dprob * prob, axis=0, keepdims=True)
            ds = (prob * (dprob - drow) * ATT_SCALE).astype(BF)
            dqg = _nn(ktwin[d * g:d * (g + 1), :], ds)
            dkw = dkw + _nt(ds, qpad)
            dvw = dvw + _nt(prob.astype(BF), dopad)
            dsink_lanes = psink * drow
            for j in range(ATT_Q_PER_KV):
                h = ATT_Q_PER_KV * g + j
                dqt_ref[d * h:d * (h + 1), :] = _rope_att_rows(dqg[:, c * j:c * (j + 1)], ctv, stv, -1.0).astype(BF)
                dsink = dsink + jnp.where(lane1 == h, -jnp.sum(dsink_lanes[:, c * j:c * (j + 1)]), 0.0)
        dsink_ref[...] += dsink
        total = carry[...] + jnp.concatenate([dkw[c:], dvw[c:]], axis=1)
        dkv_ref[:, :LANE] = _rope_att_t(total[:, :LANE], ca_ref[...], sa_ref[...], sb_ref[...]).astype(BF)
        dkv_ref[:, LANE:] = total[:, LANE:].astype(BF)
        carry[...] = jnp.concatenate([dkw[:c], dvw[:c]], axis=1)

        crv, srv = cr_ref[...], sr_ref[...]
        for h in range(RET_HEADS):
            vs = slice(dv * h, dv * (h + 1))
            qh = qk_ref[:, dk * h:dk * (h + 1)]
            kh = qk_ref[:, 512 + dk * h:512 + dk * (h + 1)]
            vh = v_ref[:, vs]
            out = ret_ref[:, vs]
            g = g_ref[:, vs].astype(F32)
            dyp = dyp_ref[:, vs].astype(F32)
            gain_h = gain_ref[:, vs]
            mu = jnp.mean(out, axis=-1, keepdims=True)
            dev = out - mu
            rstd = lax.rsqrt(jnp.mean(dev * dev, axis=-1, keepdims=True) + GN_EPS)
            yn = dev * rstd
            sg = _sigmoid(g)
            dg = dyp * (yn * gain_h) * (sg * (1.0 + g * (1.0 - sg)))
            dy = dyp * (g * sg)
            dgain_ref[:, vs] += jnp.sum(dy * yn, axis=0, keepdims=True)
            dyn = dy * gain_h
            dout = rstd * (dyn - jnp.mean(dyn, axis=-1, keepdims=True)
                           - yn * jnp.mean(dyn * yn, axis=-1, keepdims=True))
            doutb = dout.astype(BF)
            sc = st8_ref[0, h]
            dsp = dstate[h]
            dspb = dsp.astype(BF)
            intra, qdv, kdv = intra_ref[h], qd_ref[h], kd_ref[h]
            att = _nt(qh, kh) * intra
            dab = (_nt(doutb, vh) * intra).astype(BF)
            qdec = (qh.astype(F32) * qdv).astype(BF)
            kdec = (kh.astype(F32) * kdv).astype(BF)
            dq = _nn(dab, kh) + _nt(doutb, sc) * qdv
            dkk = _tn(dab, qh) + _nt(vh, dspb) * kdv
            dvv = _tn(att.astype(BF), doutb) + _nn(kdec, dspb)
            dstate[h] = dsp * RET_CHUNK_DECAY[h] + _tn(qdec, doutb)
            dall_ref[:, dk * h:dk * (h + 1)] = _rope_ret_t(dq, crv, srv).astype(BF)
            dall_ref[:, 512 + dk * h:512 + dk * (h + 1)] = (_rope_ret_t(dkk, crv, srv) * RET_SCALE).astype(BF)
            dall_ref[:, 1024 + dv * h:1024 + dv * (h + 1)] = dvv.astype(BF)
            dall_ref[:, 2048 + dv * h:2048 + dv * (h + 1)] = dg.astype(BF)

    cur = lambda s: nb - 1 - s
    prev = lambda s: jnp.maximum(nb - 2 - s, 0)
    rows = lambda w: pl.BlockSpec((c, w), lambda s: (cur(s), 0))
    cols = lambda h: pl.BlockSpec((h, c), lambda s: (0, cur(s)))
    acc = lambda w: pl.BlockSpec((1, w), lambda s: (0, 0))
    return _call(
        body, (sinks, qt, kv, kv, kvt, kvt, dot, ca, sa, sb, ct, st, qkr, vr, gr, ret, dyrp, states, gain, *rtabs,
               cr, sr), comm, name="seq_bwd", grid=(nb,),
        in_specs=[SMEM_FULL, cols(512), rows(256), pl.BlockSpec((c, 256), lambda s: (prev(s), 0)), cols(256),
                  pl.BlockSpec((256, c), lambda s: (0, prev(s))), cols(512), rows(LANE), rows(LANE), rows(LANE),
                  cols(8), cols(8), rows(1024), rows(1024), rows(1024), rows(1024), rows(1024),
                  pl.BlockSpec((1, RET_HEADS, dk, dv), lambda s: (cur(s), 0, 0, 0)),
                  VMEM_FULL, VMEM_FULL, VMEM_FULL, VMEM_FULL, rows(LANE), rows(LANE)],
        out_specs=[cols(512), rows(256), acc(LANE), rows(3072), acc(1024)],
        out_shape=[jax.ShapeDtypeStruct((512, seq), BF), jax.ShapeDtypeStruct((seq, 256), BF),
                   jax.ShapeDtypeStruct((1, LANE), F32), jax.ShapeDtypeStruct((seq, 3072), BF),
                   jax.ShapeDtypeStruct((1, 1024), F32)],
        scratch_shapes=[pltpu.VMEM((c, 256), F32), pltpu.VMEM((RET_HEADS, dk, dv), F32)])


def _inproj_bwd(dqa, dkva, dret, dgates, dh1, x, g1, w_in, comm=None):
    seq = x.shape[0]
    tm = min(TOKEN_TILE, seq)

    def body(dqa_ref, dkva_ref, dret_ref, dgates_ref, dh1_ref, x_ref, g_ref, w_hbm, dx_ref, dg1_ref, w_vmem, w_sems):
        w = _Resident(w_hbm, w_vmem, w_sems, IN_CHUNKS_BWD)
        _Resident.load(w)

        @pl.when(pl.program_id(0) == 0)
        def _():
            dg1_ref[...] = jnp.zeros_like(dg1_ref)

        dxn = _tn(dqa_ref[...], w.chunk(0)[...])
        dxn = dxn + _nn(dkva_ref[...], w.chunk(1)[...])
        dxn = dxn + _nn(dret_ref[...], w.chunk(2)[...])
        dxn = dxn + _nn(dgates_ref[...], w.chunk(3)[...])
        dnorm, dg = _rms_bwd(dxn, x_ref[...], g_ref[...])
        dx_ref[...] = dh1_ref[...] + dnorm
        dg1_ref[...] += dg

    return _call(
        body, (dqa, dkva, dret, dgates, dh1, x, g1, w_in), comm, name="inproj_bwd", grid=(seq // tm,),
        in_specs=[_cols(512, tm), _rows(tm, 256), _rows(tm, 3072), _rows(tm, 2048), _rows(tm, D_MODEL),
                  _rows(tm, D_MODEL), VMEM_FULL, ANY],
        out_specs=[_rows(tm, D_MODEL), pl.BlockSpec((1, D_MODEL), lambda i: (0, 0))],
        out_shape=[jax.ShapeDtypeStruct((seq, D_MODEL), F32), jax.ShapeDtypeStruct((1, D_MODEL), F32)],
        scratch_shapes=_Resident.scratch(w_in, IN_CHUNKS_BWD))


def _local_step(x, target, g1, bg, sinks, gain, g2, g3, shards):
    rtabs = _ret_tables()
    scatter = _Scatter

    comm, unpack = _gather(shards, ("w_in",))
    (xn1, ca, sa, sb, cr, sr, ct, st), got = _prologue(x, g1, comm)
    (w_in,) = unpack(got)
    comm_mid, unpack_mid = _gather(shards, ("w_att_up", "w_ret_up", "w_out"))
    comm_ff1, unpack_ff1 = _gather(shards, ("w_ff1",))
    (qt, kv, kvt, qkr, vr, gr, gates), got = _inproj_fwd(xn1, w_in, (ca, sa, sb, cr, sr, ct, st),
                                                         _join(comm_mid, comm_ff1))
    w_att, w_ret, w_out = unpack_mid(got[:3])
    (w_ff1,) = unpack_ff1(got[3:])
    comm, unpack = _gather(shards, ("w_ff2",))
    (oa, ret, yrp, states), got = _seq_fwd(qt, kv, kvt, sinks, qkr, vr, gr, gain, rtabs, comm)
    (w_ff2,) = unpack(got)
    ya, yr, merged, h1 = _mix_fwd(oa, yrp, gates, bg, x, w_att, w_ret, w_out)
    xn2, hdn, dh2, du, dh1, loss, dg3, dg2 = _mlp_loss_step(h1, g2, g3, target, w_ff1, w_ff2)

    ff2 = scatter(dict(w_ff2=_mm_tn(hdn, dh2, "dw_ff2")), "ff2")
    ff1 = scatter(dict(w_ff1=_mm_tn(du, xn2, "dw_ff1")), "ff1")
    (dgates, doa, dyrp, db, d_att, d_ret, d_out), got = _mix_bwd(
        dh1, ya, yr, gates, bg, merged, oa, yrp, w_att, w_ret, w_out, _join(ff2.pair_comm(), ff1.pair_comm()))
    _Scatter.pairs_done((ff2, ff1), (got[:1], got[1:]), "ff")
    mid = scatter(dict(w_att_up=d_att, w_ret_up=d_ret, w_out=d_out), "mid")
    (dqa, dkva, dsink, dret, dgain), got = _seq_bwd(
        qt, kv, kvt, doa, sinks, qkr, vr, gr, ret, dyrp, states, gain, rtabs, (ca, sa, sb, cr, sr, ct, st),
        _join(_join(ff2.chip_comm(), ff1.chip_comm()), mid.pair_comm()))
    ff2.chip_done(got[:1])
    ff1.chip_done(got[1:2])
    mid.pair_done(got[2:])
    (d_in, d_in_bf), r2 = _dw_in_t((dqa, dkva, dret, dgates), (True, False, False, False), xn1, mid.chip_comm())
    mid.chip_done(r2)
    win = scatter(dict(w_in=d_in), "in", sent=dict(w_in=d_in_bf))
    win.pair_done(_comm_alone(win.pair_comm(), "pair_exchange_in"))
    (dx, dg1), r2 = _inproj_bwd(dqa, dkva, dret, dgates, dh1, x, g1, w_in, win.chip_comm())
    win.chip_done(r2)
    small = dict(norm_mix_gain=dg1, b_gates=db, attn_sinks=dsink, ret_gn_gain=dgain, norm_mlp_gain=dg2,
                 norm_final_gain=dg3)
    return loss, dx, small, (win, mid, ff1, ff2)


def _coords():
    return lax.axis_index("x"), lax.axis_index("y"), lax.axis_index("c")


def _flip(v, bit):
    return 1 - v if bit else v


def _xor(a, b):
    return a + b - 2 * a * b


def _gather_relay_comm(shards):
    n = len(shards)

    def parts(ins, outs, sems):
        send_sems, recv_sems, local_sems = sems
        x, y, c = _coords()
        me, sib = (x, y, c), (x, y, 1 - c)

        def ring(core):
            return ((_xor(x, core), _xor(y, 1 - core), core), (_xor(x, 1 - core), _xor(y, core), core),
                    (1 - x, 1 - y, core))

        def slot(a, blk):
            return outs[a].at[4 * blk[0] + 2 * blk[1] + blk[2]]

        def copy(a, k, blk, to, src=None):
            return pltpu.make_async_remote_copy(
                src_ref=slot(a, blk) if src is None else src, dst_ref=slot(a, blk),
                send_sem=send_sems.at[a, k], recv_sem=recv_sems.at[a, k], device_id=to, device_id_type=MESH)

        mine = [pltpu.make_async_copy(ins[a], slot(a, me), local_sems.at[a]) for a in range(n)]
        return me, sib, ring, copy, mine

    def first_copies(ins, outs, sems):
        me, sib, ring, copy, mine = parts(ins, outs, sems)
        src, dst, _ = ring(me[2])
        return mine, [copy(a, k, me, to, src=ins[a]) for a in range(n) for k, to in ((1, src), (2, dst), (0, sib))]

    def start(ins, outs, sems):
        mine, first = first_copies(ins, outs, sems)
        for cp in mine + first:
            cp.start()

    def relay(ins, outs, sems):
        me, sib, ring, copy, _ = parts(ins, outs, sems)
        src, dst, _ = ring(me[2])
        for a in range(n):
            copy(a, 1, src, me).wait_recv()
            copy(a, 3, src, dst).start()
            copy(a, 4, src, sib).start()
        for a in range(n):
            copy(a, 2, dst, me).wait_recv()
            copy(a, 5, dst, sib).start()

    def finish(ins, outs, sems):
        me, sib, ring, copy, _ = parts(ins, outs, sems)
        mine, first = first_copies(ins, outs, sems)
        src, dst, diag = ring(me[2])
        passed = [copy(a, k, blk, to) for a in range(n) for k, blk, to in ((3, src, dst), (4, src, sib), (5, dst, sib))]
        for a in range(n):
            copy(a, 3, diag, me).wait_recv()
            fwd = copy(a, 6, diag, sib)
            fwd.start()
            passed.append(fwd)
        s_src, s_dst, s_diag = ring(sib[2])
        for a in range(n):
            for k, blk in ((0, sib), (4, s_src), (5, s_dst), (6, s_diag)):
                copy(a, k, blk, me).wait_recv()
        for cp in first + passed:
            cp.wait_send()
        for cp in mine:
            cp.wait()

    return _Comm(list(shards), [jax.ShapeDtypeStruct((N_DEV,) + s.shape, s.dtype) for s in shards],
                 [pltpu.SemaphoreType.DMA((n, 7)), pltpu.SemaphoreType.DMA((n, 7)), pltpu.SemaphoreType.DMA((n,))],
                 start, finish, relay)


COLUMN_SHARDED = ("w_in", "w_ff1")
COLUMN_RELAID = ("w_att_up",)


def _gather(shards, names):
    def unpack(got):
        return [jnp.transpose(g, (1, 0, 2)).reshape(g.shape[1], N_DEV * g.shape[2]) if k in COLUMN_RELAID
                else g.reshape(N_DEV * g.shape[1], g.shape[2]) for k, g in zip(names, got)]

    return _gather_relay_comm([shards[k] for k in names]), unpack


def _pair_comm(grads):
    n = len(grads)

    def copies(g, r1, sems):
        send_sems, recv_sems = sems
        x, y, c = _coords()
        return [pltpu.make_async_remote_copy(
            src_ref=g[a].at[2 * j + (1 - c)], dst_ref=r1[a].at[j], send_sem=send_sems.at[a, j],
            recv_sem=recv_sems.at[a, j], device_id=(x, y, 1 - c), device_id_type=MESH)
            for a in range(n) for j in range(4)]

    def start(g, r1, sems):
        for cp in copies(g, r1, sems):
            cp.start()

    def finish(g, r1, sems):
        for cp in copies(g, r1, sems):
            cp.wait_recv()
        for cp in copies(g, r1, sems):
            cp.wait_send()

    return _Comm(list(grads), [jax.ShapeDtypeStruct((4,) + g.shape[1:], g.dtype) for g in grads],
                 [pltpu.SemaphoreType.DMA((n, 4)), pltpu.SemaphoreType.DMA((n, 4))], start, finish)


def _small_comm(small):
    def parts(ins, outs, sems):
        (small_ref,), (small_all,) = ins, outs
        ssend, srecv, lsem = sems
        x, y, c = _coords()
        me_idx = 4 * x + 2 * y + c
        own = pltpu.make_async_copy(small_ref, small_all.at[me_idx], lsem)
        sends, recvs = [], []
        for r in range(1, N_DEV):
            px, py, pc = _flip(x, r & 4), _flip(y, r & 2), _flip(c, r & 1)
            sends.append(pltpu.make_async_remote_copy(
                src_ref=small_ref, dst_ref=small_all.at[me_idx], send_sem=ssend.at[r - 1], recv_sem=srecv.at[r - 1],
                device_id=(px, py, pc), device_id_type=MESH))
            recvs.append(pltpu.make_async_remote_copy(
                src_ref=small_ref, dst_ref=small_all.at[4 * px + 2 * py + pc], send_sem=ssend.at[r - 1],
                recv_sem=srecv.at[r - 1], device_id=(px, py, pc), device_id_type=MESH))
        return own, sends, recvs

    def start(ins, outs, sems):
        own, sends, _ = parts(ins, outs, sems)
        own.start()
        for cp in sends:
            cp.start()

    def finish(ins, outs, sems):
        own, sends, recvs = parts(ins, outs, sems)
        for cp in recvs:
            cp.wait_recv()
        for cp in sends:
            cp.wait_send()
        own.wait()

    return _Comm([small], [jax.ShapeDtypeStruct((N_DEV,) + small.shape, small.dtype)],
                 [pltpu.SemaphoreType.DMA((N_DEV - 1,)), pltpu.SemaphoreType.DMA((N_DEV - 1,)),
                  pltpu.SemaphoreType.DMA], start, finish)


def _pair_sum(grads, r1s, c_arr, tag):
    n = len(grads)
    q = 1

    def body(c_ref, *refs):
        g, r, t = refs[:n], refs[n:2 * n], refs[2 * n:]
        for a in range(n):
            t[a][...] = (g[a][...] + r[a][...]).astype(t[a].dtype)

    def blk(arr):
        return (1, arr.shape[1] // q, arr.shape[2])

    grid_spec = pltpu.PrefetchScalarGridSpec(
        num_scalar_prefetch=1, grid=(4, q),
        in_specs=[pl.BlockSpec(blk(g), lambda j, s, c_ref: (2 * j + c_ref[0], s, 0)) for g in grads]
        + [pl.BlockSpec(blk(r), lambda j, s, c_ref: (j, s, 0)) for r in r1s],
        out_specs=[pl.BlockSpec(blk(r), lambda j, s, c_ref: (j, s, 0)) for r in r1s])
    return pl.pallas_call(
        body, name="pair_sum_" + tag, grid_spec=grid_spec,
        out_shape=[jax.ShapeDtypeStruct(r.shape, RS_PAYLOAD) for r in r1s],
        compiler_params=_params(2),
    )(c_arr, *grads, *r1s)


def _chip_comm(ts):
    n = len(ts)

    def copies(t, r2, sems):
        send_sems, recv_sems = sems
        x, y, c = _coords()
        out = []
        for a in range(n):
            for r in (3, 1, 2):
                tx, ty = _flip(x, r & 2), _flip(y, r & 1)
                out.append(pltpu.make_async_remote_copy(
                    src_ref=t[a].at[2 * tx + ty], dst_ref=r2[a].at[r - 1], send_sem=send_sems.at[a, r - 1],
                    recv_sem=recv_sems.at[a, r - 1], device_id=(tx, ty, c), device_id_type=MESH))
        return out

    def start(t, r2, sems):
        for cp in copies(t, r2, sems):
            cp.start()

    def finish(t, r2, sems):
        for cp in copies(t, r2, sems):
            cp.wait_recv()
        for cp in copies(t, r2, sems):
            cp.wait_send()

    return _Comm(list(ts), [jax.ShapeDtypeStruct((3,) + t.shape[1:], t.dtype) for t in ts],
                 [pltpu.SemaphoreType.DMA((n, 3)), pltpu.SemaphoreType.DMA((n, 3))], start, finish)


class _Scatter:
    def __init__(self, grads, tag, sent=None):
        def stack(k, g):
            if k in COLUMN_RELAID:
                return jnp.transpose(g.reshape(g.shape[0], N_DEV, g.shape[1] // N_DEV), (1, 0, 2))
            return g.reshape(N_DEV, g.shape[0] // N_DEV, g.shape[1])

        self.names, self.tag = tuple(grads), tag
        self.stacks = [stack(k, g) for k, g in grads.items()]
        self.sent = self.stacks if sent is None else [stack(k, sent[k]) for k in grads]

    def pair_comm(self):
        return _pair_comm(self.sent)

    def pair_done(self, r1s):
        _Scatter.pairs_done((self,), (r1s,), self.tag)

    @staticmethod
    def pairs_done(groups, r1s_of, tag):
        for grp, r1s in zip(groups, r1s_of):
            grp.r1s = list(r1s)
        c_arr = jnp.reshape(lax.axis_index("c"), (1,)).astype(jnp.int32)
        ts = _pair_sum([s for grp in groups for s in grp.stacks], [r for grp in groups for r in grp.r1s], c_arr, tag)
        for grp in groups:
            grp.ts, ts = ts[:len(grp.stacks)], ts[len(grp.stacks):]

    def chip_comm(self):
        return _chip_comm(self.ts)

    def chip_done(self, r2s):
        self.r2s = list(r2s)


def _adamw(w, g, m, v):
    m = ADAM_B1 * m + (1.0 - ADAM_B1) * g
    v = ADAM_B2 * v + (1.0 - ADAM_B2) * jnp.square(g)
    m_hat = m / (1.0 - ADAM_B1 ** ADAM_STEP)
    v_hat = v / (1.0 - ADAM_B2 ** ADAM_STEP)
    delta = -ADAM_LR * (m_hat / (jnp.sqrt(v_hat) + ADAM_EPS) + ADAM_WD * w)
    return delta, m, v


ADAM_STEPS = 4


def _piece_specs(stacks):
    def rows(s):
        return s.shape[1] // ADAM_STEPS
    return ([pl.BlockSpec((1, rows(s), s.shape[2]), lambda i, idx_ref: (idx_ref[0], i, 0)) for s in stacks]
            + [pl.BlockSpec((1, rows(s), s.shape[2]), lambda i, idx_ref: (idx_ref[1], i, 0)) for s in stacks]
            + [pl.BlockSpec((3, rows(s), s.shape[2]), lambda i, idx_ref: (0, i, 0)) for s in stacks])


def _piece_sum(g0, r1, r2):
    return (((g0[0] + r1[0]) + r2[0].astype(F32)) + r2[1].astype(F32)) + r2[2].astype(F32)


def _shard_sum(stacks, r1s, r2s, idx_arr):
    n = len(stacks)

    def body(idx_ref, *refs):
        g0, r1, r2, outs = (refs[k * n:(k + 1) * n] for k in range(4))
        for a in range(n):
            outs[a][...] = _piece_sum(g0[a], r1[a], r2[a])

    out_specs = [pl.BlockSpec((s.shape[1] // ADAM_STEPS, s.shape[2]), lambda i, idx_ref: (i, 0)) for s in stacks]
    grid_spec = pltpu.PrefetchScalarGridSpec(num_scalar_prefetch=1, grid=(ADAM_STEPS,),
                                             in_specs=_piece_specs(stacks), out_specs=out_specs)
    return pl.pallas_call(
        body, name="shard_sum", grid_spec=grid_spec,
        out_shape=[jax.ShapeDtypeStruct(s.shape[1:], F32) for s in stacks],
        compiler_params=_params(1),
    )(idx_arr, *stacks, *r1s, *r2s)


def _adam_big(pieces, summed, ws, ms, vs, idx_arr):
    stacks, r1s, r2s = pieces
    n_p, n = len(stacks), len(ws)

    def body(idx_ref, *refs):
        it = iter(refs)
        g0, r1, r2, gs, w, m, v = ([next(it) for _ in range(k)] for k in (n_p, n_p, n_p, n - n_p, n, n, n))
        outs = list(it)
        for a in range(n):
            g = _piece_sum(g0[a], r1[a], r2[a]) if a < n_p else gs[a - n_p][...]
            delta, nm, nv = _adamw(w[a][...], g, m[a][...], v[a][...])
            outs[4 * a][...] = g
            outs[4 * a + 1][...] = delta
            outs[4 * a + 2][...] = nm
            outs[4 * a + 3][...] = nv

    def shard_spec(w):
        return pl.BlockSpec((w.shape[0] // ADAM_STEPS, w.shape[1]), lambda i, idx_ref: (i, 0))

    in_specs = _piece_specs(stacks) + [shard_spec(w) for w in ws[n_p:]] + [shard_spec(w) for w in ws] * 3
    out_specs = [shard_spec(w) for w in ws for _ in range(4)]
    grid_spec = pltpu.PrefetchScalarGridSpec(num_scalar_prefetch=1, grid=(ADAM_STEPS,), in_specs=in_specs,
                                             out_specs=out_specs)
    return pl.pallas_call(
        body, name="adam_big", grid_spec=grid_spec,
        out_shape=[jax.ShapeDtypeStruct(w.shape, F32) for w in ws for _ in range(4)],
        compiler_params=_params(1),
    )(idx_arr, *stacks, *r1s, *r2s, *summed, *ws, *ms, *vs)


def _adam_small(small_all, w, m, v):
    def body(all_ref, w_ref, m_ref, v_ref, g_ref, d_ref, nm_ref, nv_ref):
        g = all_ref[0]
        for k in range(1, N_DEV):
            g = g + all_ref[k]
        delta, nm, nv = _adamw(w_ref[...], g, m_ref[...], v_ref[...])
        g_ref[...] = g
        d_ref[...] = delta
        nm_ref[...] = nm
        nv_ref[...] = nv

    return pl.pallas_call(
        body, name="adam_small",
        in_specs=[VMEM_FULL] * 4, out_specs=[VMEM_FULL] * 4,
        out_shape=[jax.ShapeDtypeStruct(w.shape, F32)] * 4,
    )(small_all, w, m, v)


RS_PAYLOAD = BF


def _pack_rows(rows):
    rid = lax.broadcasted_iota(jnp.int32, (8, D_MODEL), 0)
    out = jnp.zeros((8, D_MODEL), F32)
    for i, r in enumerate(rows):
        out = jnp.where(rid == i, jnp.broadcast_to(r, (8, D_MODEL)), out)
    return out


def _small_rows(norm_mix_gain, b_gates, attn_sinks, ret_gn_gain, norm_mlp_gain, norm_final_gain):
    return [norm_mix_gain, b_gates[:, :D_MODEL], b_gates[:, D_MODEL:], ret_gn_gain, norm_mlp_gain,
            norm_final_gain.reshape(1, D_MODEL), jnp.pad(attn_sinks, ((0, 0), (0, D_MODEL - ATT_HEADS)))]


def _unpack_small(p):
    return dict(norm_mix_gain=p[0:1], b_gates=jnp.concatenate([p[1:2], p[2:3]], axis=1), ret_gn_gain=p[3:4],
                norm_mlp_gain=p[4:5], norm_final_gain=p[5], attn_sinks=p[6:7, :ATT_HEADS])


WEIGHTS = ("norm_mix_gain", "w_in", "b_gates", "attn_sinks", "ret_gn_gain", "w_att_up", "w_ret_up", "w_out",
           "norm_mlp_gain", "w_ff1", "w_ff2", "norm_final_gain")
BIG = ("w_in", "w_att_up", "w_ret_up", "w_out", "w_ff1", "w_ff2")


def kernel(x, norm_mix_gain, w_in, b_gates, attn_sinks, ret_gn_gain, w_att_up, w_ret_up, w_out, norm_mlp_gain, w_ff1, w_ff2, norm_final_gain, loss_target, m_norm_mix_gain, m_w_in, m_b_gates, m_attn_sinks, m_ret_gn_gain, m_w_att_up, m_w_ret_up, m_w_out, m_norm_mlp_gain, m_w_ff1, m_w_ff2, m_norm_final_gain, v_norm_mix_gain, v_w_in, v_b_gates, v_attn_sinks, v_ret_gn_gain, v_w_att_up, v_w_ret_up, v_w_out, v_norm_mlp_gain, v_w_ff1, v_w_ff2, v_norm_final_gain):
    w = dict(norm_mix_gain=norm_mix_gain, w_in=w_in, b_gates=b_gates, attn_sinks=attn_sinks, ret_gn_gain=ret_gn_gain,
             w_att_up=w_att_up, w_ret_up=w_ret_up, w_out=w_out, norm_mlp_gain=norm_mlp_gain, w_ff1=w_ff1,
             w_ff2=w_ff2, norm_final_gain=norm_final_gain)
    m = dict(norm_mix_gain=m_norm_mix_gain, w_in=m_w_in, b_gates=m_b_gates, attn_sinks=m_attn_sinks,
             ret_gn_gain=m_ret_gn_gain, w_att_up=m_w_att_up, w_ret_up=m_w_ret_up, w_out=m_w_out,
             norm_mlp_gain=m_norm_mlp_gain, w_ff1=m_w_ff1, w_ff2=m_w_ff2, norm_final_gain=m_norm_final_gain)
    v = dict(norm_mix_gain=v_norm_mix_gain, w_in=v_w_in, b_gates=v_b_gates, attn_sinks=v_attn_sinks,
             ret_gn_gain=v_ret_gn_gain, w_att_up=v_w_att_up, w_ret_up=v_w_ret_up, w_out=v_w_out,
             norm_mlp_gain=v_norm_mlp_gain, w_ff1=v_w_ff1, w_ff2=v_w_ff2, norm_final_gain=v_norm_final_gain)

    shards = {k: (w[k][0].T if k in COLUMN_SHARDED else w[k][0]).astype(BF) for k in BIG}
    loss_p, dx, small, groups = _local_step(
        x[0], loss_target[0], norm_mix_gain, b_gates, attn_sinks[0], ret_gn_gain, norm_mlp_gain,
        norm_final_gain.reshape(1, D_MODEL), shards)

    by_name = {}
    for grp in groups:
        for k, stack, r1, r2 in zip(grp.names, grp.stacks, grp.r1s, grp.r2s):
            by_name[k] = (stack, r1, r2)
    lane = lax.broadcasted_iota(jnp.int32, (1, D_MODEL), 1)
    sink_row = jnp.where(lane < ATT_HEADS, jnp.pad(small["attn_sinks"], ((0, 0), (0, D_MODEL - LANE))),
                         jnp.where(lane == ATT_HEADS, jnp.pad(loss_p, ((0, 0), (0, D_MODEL - LANE)), mode="edge"),
                                   0.0))
    small_pack = _pack_rows([small["norm_mix_gain"], small["b_gates"][:, :D_MODEL], small["b_gates"][:, D_MODEL:],
                             small["ret_gn_gain"], small["norm_mlp_gain"], small["norm_final_gain"], sink_row])
    xi, yi, ci = _coords()
    (small_all,) = _comm_alone(_small_comm(small_pack), "small_exchange")
    idx_arr = jnp.stack([4 * xi + 2 * yi + ci, 2 * xi + yi]).astype(jnp.int32)
    in_pieces = ["w_att_up", "w_ret_up", "w_out", "w_ff2", "w_in"]
    in_sum = ["w_ff1"]
    order = in_pieces + in_sum
    summed_t = _shard_sum(*[[by_name[k][i] for k in in_sum] for i in range(3)], idx_arr)

    def shard(tree, k):
        return tree[k][0].T if k == "w_in" else tree[k][0]

    adam_out = _adam_big([[by_name[k][i] for k in in_pieces] for i in range(3)], [g.T for g in summed_t],
                         [shard(w, k) for k in order], [shard(m, k) for k in order], [shard(v, k) for k in order],
                         idx_arr)
    big_out = [adam_out[4 * order.index(k) + i].T if k == "w_in" else adam_out[4 * order.index(k) + i]
               for k in BIG for i in range(4)]
    sm_g, sm_d, sm_m, sm_v = _adam_small(small_all, _pack_rows(_small_rows(*[w[k] for k in WEIGHTS if k not in BIG])),
                                         _pack_rows(_small_rows(*[m[k] for k in WEIGHTS if k not in BIG])),
                                         _pack_rows(_small_rows(*[v[k] for k in WEIGHTS if k not in BIG])))

    loss = sm_g[6, ATT_HEADS]
    kinds = []
    for idx, packed in enumerate((sm_g, sm_d, sm_m, sm_v)):
        out = _unpack_small(packed)
        for a, k in enumerate(BIG):
            out[k] = big_out[4 * a + idx][None]
        kinds.append(out)
    return (loss, dx[None], *[kinds[0][k] for k in WEIGHTS], *[kinds[1][k] for k in WEIGHTS],
            *[kinds[2][k] for k in WEIGHTS], *[kinds[3][k] for k in WEIGHTS])
```

```python
import functools
import math

import jax
import jax.numpy as jnp
from jax import lax
from jax.experimental import pallas as pl
from jax.experimental.pallas import tpu as pltpu

F32 = jnp.float32
BF = jnp.bfloat16
MESH = pl.DeviceIdType.MESH

D_MODEL = 1024
ATT_HEADS = 8
ATT_HEAD_DIM = 64
ATT_BLOCK = 128
ROPE_DIM = 16
ROPE_THETA = 500000.0
RET_HEADS = 4
RET_KEY_DIM = 128
RET_VAL_DIM = 256
RET_CHUNK = 128
RET_ROT_BASE = 10000.0
D_FF = 4096
NORM_EPS = 1e-6
GN_EPS = 1e-6
NEG_INF = -1e30
ATT_SCALE = ATT_HEAD_DIM ** -0.5
RET_SCALE = RET_KEY_DIM ** -0.5

C_QA, C_KA, C_VA, C_QR, C_KR, C_VR, C_GR, C_GATES, C_END = 0, 512, 640, 768, 1280, 1792, 2816, 3840, 5888

ADAM_LR = 0.001
ADAM_B1 = 0.9
ADAM_B2 = 0.999
ADAM_EPS = 1e-08
ADAM_WD = 0.01
ADAM_STEP = 10

N_DEV = 8
LANE = 128
VMEM_LIMIT = 56 * 1024 * 1024
TOKEN_TILE = 512
MLP_TOKEN_TILE = 256
TN_TOKEN_TILE = 4096
FF_CHUNKS = (0, 1024, 2048, 3072, 4096)
IN_CHUNKS_FWD = (C_QA, C_KA, C_QR, C_KR, C_VR, C_GR, C_GATES, C_END)
IN_CHUNKS_BWD = (C_QA, C_KA, C_QR, C_GATES, C_END)

RET_LOG_GAMMA = tuple(math.log1p(-(2.0 ** (-5.0 - h))) for h in range(RET_HEADS))
RET_CHUNK_DECAY = tuple(math.exp(RET_CHUNK * lg) for lg in RET_LOG_GAMMA)

VMEM_FULL = pl.BlockSpec(memory_space=pltpu.VMEM)
SMEM_FULL = pl.BlockSpec(memory_space=pltpu.SMEM)
ANY = pl.BlockSpec(memory_space=pl.ANY)


def _params(n_axes):
    return pltpu.CompilerParams(dimension_semantics=("arbitrary",) * n_axes, vmem_limit_bytes=VMEM_LIMIT)


def _nn(a, b):
    return jnp.dot(a, b, preferred_element_type=F32)


def _nt(a, b):
    return lax.dot_general(a, b, (((1,), (1,)), ((), ())), preferred_element_type=F32)


def _tn(a, b):
    return lax.dot_general(a, b, (((0,), (0,)), ((), ())), preferred_element_type=F32)


def _sigmoid(v):
    return 1.0 / (1.0 + jnp.exp(-v))


def _rows(tile, width):
    return pl.BlockSpec((tile, width), lambda i: (i, 0))


def _cols(height, tile):
    return pl.BlockSpec((height, tile), lambda i: (0, i))


class _Resident:
    def __init__(self, hbm_ref, vmem_ref, sems, bounds):
        self.hbm, self.vmem, self.sems, self.bounds = hbm_ref, vmem_ref, sems, bounds

    @staticmethod
    def scratch(w, bounds):
        return [pltpu.VMEM(w.shape, w.dtype), pltpu.SemaphoreType.DMA((len(bounds) - 1,))]

    @staticmethod
    def load(*weights):
        @pl.when(pl.program_id(0) == 0)
        def _():
            copies = [w._copy(c) for w in weights for c in range(len(w.bounds) - 1)]
            for cp in copies:
                cp.start()
            for cp in copies:
                cp.wait()

    def _rows(self, c):
        return pl.ds(self.bounds[c], self.bounds[c + 1] - self.bounds[c])

    def _copy(self, c):
        return pltpu.make_async_copy(self.hbm.at[self._rows(c)], self.vmem.at[self._rows(c)], self.sems.at[c])

    def chunk(self, c):
        return self.vmem.at[self._rows(c)]


class _Comm:
    def __init__(self, inputs, out_shapes, scratch, start, finish, relay=None):
        self.inputs, self.out_shapes, self.scratch = inputs, out_shapes, scratch
        self.start, self.finish, self.relay = start, finish, relay


RELAY_AT = 0.7


def _join(a, b):
    na_in, na_out, na_sem = len(a.inputs), len(a.out_shapes), len(a.scratch)

    def both(name):
        def run(ins, outs, sems):
            for part, args in ((a, (ins[:na_in], outs[:na_out], sems[:na_sem])),
                               (b, (ins[na_in:], outs[na_out:], sems[na_sem:]))):
                if getattr(part, name) is not None:
                    getattr(part, name)(*args)
        return run

    return _Comm(list(a.inputs) + list(b.inputs), list(a.out_shapes) + list(b.out_shapes),
                 list(a.scratch) + list(b.scratch), both("start"), both("finish"),
                 both("relay") if a.relay or b.relay else None)


def _call(body, args, comm=None, *, name, grid, in_specs, out_specs, out_shape, scratch_shapes=()):
    params = _params(len(grid))
    if comm is None:
        return pl.pallas_call(body, name=name, grid=grid, in_specs=in_specs, out_specs=out_specs, out_shape=out_shape,
                              scratch_shapes=scratch_shapes, compiler_params=params)(*args), ()
    single = not isinstance(out_specs, (list, tuple))
    out_specs_l = [out_specs] if single else list(out_specs)
    out_shape_l = [out_shape] if single else list(out_shape)
    n_in, n_out, n_scr = len(in_specs), len(out_specs_l), len(scratch_shapes)
    n_cin, n_cout = len(comm.inputs), len(comm.out_shapes)

    def hosted(*refs):
        it = iter(refs)
        ins, cin, outs, cout, scr = ([next(it) for _ in range(k)] for k in (n_in, n_cin, n_out, n_cout, n_scr))
        sems = list(it)
        ids = [pl.program_id(k) for k in range(len(grid))]
        first = functools.reduce(jnp.logical_and, [i == 0 for i in ids])
        last = functools.reduce(jnp.logical_and, [i == g - 1 for i, g in zip(ids, grid)])

        @pl.when(first)
        def _():
            comm.start(cin, cout, sems)

        if comm.relay is not None:
            at = [int(grid[0] * RELAY_AT)] + [0] * (len(grid) - 1)

            @pl.when(functools.reduce(jnp.logical_and, [i == v for i, v in zip(ids, at)]))
            def _():
                comm.relay(cin, cout, sems)

        body(*ins, *outs, *scr)

        @pl.when(last)
        def _():
            comm.finish(cin, cout, sems)

    res = pl.pallas_call(
        hosted, name=name, grid=grid, in_specs=list(in_specs) + [ANY] * n_cin,
        out_specs=out_specs_l + [ANY] * n_cout, out_shape=out_shape_l + list(comm.out_shapes),
        scratch_shapes=list(scratch_shapes) + list(comm.scratch), compiler_params=params)(*args, *comm.inputs)
    return (res[0] if single else res[:n_out]), res[n_out:]


def _comm_alone(comm, name):
    n_cin, n_cout = len(comm.inputs), len(comm.out_shapes)

    def body(*refs):
        cin, cout, sems = refs[:n_cin], refs[n_cin:n_cin + n_cout], refs[n_cin + n_cout:]
        comm.start(cin, cout, sems)
        if comm.relay is not None:
            comm.relay(cin, cout, sems)
        comm.finish(cin, cout, sems)

    return pl.pallas_call(body, name=name, in_specs=[ANY] * n_cin, out_specs=[ANY] * n_cout,
                          out_shape=list(comm.out_shapes), scratch_shapes=list(comm.scratch))(*comm.inputs)


def _slabs(v, fn):
    return jnp.concatenate([fn(v[:, LANE * j:LANE * (j + 1)]) for j in range(v.shape[1] // LANE)], axis=1)


def _rope_att(v, ca, sa, sb):
    return _slabs(v, lambda t: t * ca + pltpu.roll(t, LANE - 8, 1) * sa + pltpu.roll(t, 8, 1) * sb)


def _rope_att_t(v, ca, sa, sb):
    return _slabs(v, lambda t: t * ca + pltpu.roll(t * sa, 8, 1) + pltpu.roll(t * sb, LANE - 8, 1))


def _rope_att_rows(v, ct, st, sign):
    parts = []
    for h in range(v.shape[0] // ATT_HEAD_DIM):
        r0 = ATT_HEAD_DIM * h
        x1, x2 = v[r0:r0 + 8], v[r0 + 8:r0 + 16]
        parts += [x1 * ct - sign * (x2 * st), x2 * ct + sign * (x1 * st), v[r0 + 16:r0 + ATT_HEAD_DIM]]
    return jnp.concatenate(parts, axis=0)


def _rope_ret(v, cr, sr):
    return _slabs(v, lambda t: t * cr + pltpu.roll(t, 64, 1) * sr)


def _rope_ret_t(v, cr, sr):
    return _slabs(v, lambda t: t * cr + pltpu.roll(t * sr, 64, 1))


def _rope_lane_tables():
    def inv(dim, theta):
        return theta ** (-jnp.arange(0, dim, 2, dtype=F32) / dim)

    inv_a, inv_r = inv(ROPE_DIM, ROPE_THETA), inv(RET_KEY_DIM, RET_ROT_BASE)
    half = ROPE_DIM // 2
    zeros = jnp.zeros((ATT_HEAD_DIM - ROPE_DIM,), F32)
    freq64 = jnp.concatenate([inv_a, inv_a, zeros])
    lo64 = jnp.concatenate([-jnp.ones((half,), F32), jnp.zeros((half,), F32), zeros])
    hi64 = jnp.concatenate([jnp.zeros((half,), F32), jnp.ones((half,), F32), zeros])
    sign_r = jnp.concatenate([-jnp.ones((64,), F32), jnp.ones((64,), F32)])
    rows = [jnp.tile(freq64, 2), jnp.tile(lo64, 2), jnp.tile(hi64, 2), jnp.tile(inv_r, 2), sign_r]
    lanes = jnp.stack(rows + [jnp.zeros((LANE,), F32)] * (8 - len(rows)))
    return lanes, jnp.broadcast_to(inv_a[:, None], (half, LANE))


def _prologue(x, g1, comm):
    seq = x.shape[0]
    tm = min(TOKEN_TILE, seq)

    def body(x_ref, g_ref, lanes_ref, freq_rows_ref, xn_ref, ca_ref, sa_ref, sb_ref, cr_ref, sr_ref, ct_ref, st_ref):
        xf = x_ref[...]
        r = lax.rsqrt(jnp.mean(xf * xf, axis=-1, keepdims=True) + NORM_EPS)
        xn_ref[...] = (xf * r * g_ref[...]).astype(BF)
        row0 = pl.program_id(0) * tm
        pos = (row0 + lax.broadcasted_iota(jnp.int32, (tm, LANE), 0)).astype(F32)
        ang_a = pos * lanes_ref[0:1, :]
        sin_a = jnp.sin(ang_a)
        ca_ref[...] = jnp.cos(ang_a)
        sa_ref[...] = sin_a * lanes_ref[1:2, :]
        sb_ref[...] = sin_a * lanes_ref[2:3, :]
        ang_r = pos * lanes_ref[3:4, :]
        cr_ref[...] = jnp.cos(ang_r)
        sr_ref[...] = jnp.sin(ang_r) * lanes_ref[4:5, :]
        pos_t = (row0 + lax.broadcasted_iota(jnp.int32, (8, tm), 1)).astype(F32)
        ang_t = pos_t * jnp.concatenate([freq_rows_ref[...]] * (tm // LANE), axis=1)
        ct_ref[...] = jnp.cos(ang_t)
        st_ref[...] = jnp.sin(ang_t)

    tab = _rows(tm, LANE)
    tab_t = _cols(8, tm)
    return _call(
        body, (x, g1, *_rope_lane_tables()), comm, name="prologue", grid=(seq // tm,),
        in_specs=[_rows(tm, D_MODEL), VMEM_FULL, VMEM_FULL, VMEM_FULL],
        out_specs=[_rows(tm, D_MODEL), tab, tab, tab, tab, tab, tab_t, tab_t],
        out_shape=[jax.ShapeDtypeStruct((seq, D_MODEL), BF)] + [jax.ShapeDtypeStruct((seq, LANE), F32)] * 5
        + [jax.ShapeDtypeStruct((8, seq), F32)] * 2)


def _ret_tables():
    c = RET_CHUNK
    lg = jnp.asarray(RET_LOG_GAMMA, F32)
    idx = jnp.arange(c, dtype=F32)
    diff = idx[:, None] - idx[None, :]
    intra = jnp.where(diff >= 0, jnp.exp(jnp.maximum(diff, 0.0) * lg[:, None, None]), 0.0)
    qd = jnp.exp((idx + 1.0)[None, :] * lg[:, None])[..., None]
    kd = jnp.exp((c - 1.0 - idx)[None, :] * lg[:, None])[..., None]
    return intra, jnp.broadcast_to(qd, (RET_HEADS, c, RET_KEY_DIM)), jnp.broadcast_to(kd, (RET_HEADS, c, RET_KEY_DIM))


def _inproj_fwd(xn1, w_in, tabs, comm=None):
    seq = xn1.shape[0]
    tm = min(TOKEN_TILE, seq)

    def body(xn_ref, w_hbm, ca_ref, sa_ref, sb_ref, cr_ref, sr_ref, ct_ref, st_ref,
             qt_ref, kv_ref, kvt_ref, qkr_ref, vr_ref, gr_ref, gates_ref, w_vmem, w_sems):
        w = _Resident(w_hbm, w_vmem, w_sems, IN_CHUNKS_FWD)
        _Resident.load(w)
        ca, sa, sb, cr, sr, ct, st = (ref[...] for ref in (ca_ref, sa_ref, sb_ref, cr_ref, sr_ref, ct_ref, st_ref))
        xn = xn_ref[...]
        qt_ref[...] = _rope_att_rows(_nt(w.chunk(0)[...], xn), ct, st, 1.0).astype(BF)
        w_kv = w.chunk(1)
        kvt = _nt(w_kv[...], xn)
        kvt_ref[:LANE, :] = _rope_att_rows(kvt[:LANE], ct, st, 1.0).astype(BF)
        kvt_ref[LANE:, :] = kvt[LANE:].astype(BF)
        kvn = _nt(xn, w_kv[...])
        kv_ref[:, :LANE] = _rope_att(kvn[:, :LANE], ca, sa, sb).astype(BF)
        kv_ref[:, LANE:] = kvn[:, LANE:].astype(BF)
        qr = _rope_ret(_nt(xn, w.chunk(2)[...]), cr, sr)
        qkr_ref[:, :512] = qr.astype(BF)
        kr = _rope_ret(_nt(xn, w.chunk(3)[...]), cr, sr) * RET_SCALE
        qkr_ref[:, 512:] = kr.astype(BF)
        vr_ref[...] = _nt(xn, w.chunk(4)[...]).astype(BF)
        gr_ref[...] = _nt(xn, w.chunk(5)[...]).astype(BF)
        gates_ref[...] = _nt(xn, w.chunk(6)[...]).astype(BF)

    tab = _rows(tm, LANE)
    tab_t = _cols(8, tm)
    return _call(
        body, (xn1, w_in, *tabs), comm, name="inproj_fwd", grid=(seq // tm,),
        in_specs=[_rows(tm, D_MODEL), ANY, tab, tab, tab, tab, tab, tab_t, tab_t],
        out_specs=[_cols(512, tm), _rows(tm, 256), _cols(256, tm), _rows(tm, 1024),
                   _rows(tm, 1024), _rows(tm, 1024), _rows(tm, 2048)],
        out_shape=[jax.ShapeDtypeStruct(s, BF) for s in ((512, seq), (seq, 256), (256, seq),
                                                         (seq, 1024), (seq, 1024), (seq, 1024), (seq, 2048))],
        scratch_shapes=_Resident.scratch(w_in, IN_CHUNKS_FWD))


ATT_Q_PER_KV = ATT_HEADS // 2


def _att_group(ref, kv):
    d = ATT_HEAD_DIM
    g = jnp.concatenate([ref[d * (ATT_Q_PER_KV * kv + j):d * (ATT_Q_PER_KV * kv + j + 1), :]
                         for j in range(ATT_Q_PER_KV)], axis=1)
    z = jnp.zeros_like(g)
    return jnp.concatenate([g, z] if kv == 0 else [z, g], axis=0)


def _att_probs(kwin, qpad, sink_ref, kv, i):
    c = ATT_BLOCK
    kj = lax.broadcasted_iota(jnp.int32, (2 * c, c), 0)
    qi = lax.broadcasted_iota(jnp.int32, (2 * c, c), 1)
    allowed = (kj > qi) & (kj <= qi + c) & ((kj >= c) | (i > 0))
    allowed = jnp.concatenate([allowed] * ATT_Q_PER_KV, axis=1)
    s = jnp.where(allowed, _nn(kwin, qpad) * ATT_SCALE, NEG_INF)
    sk = jnp.concatenate([jnp.full((1, c), sink_ref[ATT_Q_PER_KV * kv + j], F32) for j in range(ATT_Q_PER_KV)], axis=1)
    m = jnp.maximum(jnp.max(s, axis=0, keepdims=True), sk)
    pe = jnp.exp(s - m)
    psink = jnp.exp(sk - m)
    inv = 1.0 / (jnp.sum(pe, axis=0, keepdims=True) + psink)
    return pe * inv, psink * inv


def _seq_fwd(qt, kv, kvt, sinks, qkr, vr, gr, gain, rtabs, comm=None):
    seq = kv.shape[0]
    c = ATT_BLOCK
    assert c == RET_CHUNK
    d = ATT_HEAD_DIM
    nb = seq // c
    dk, dv = RET_KEY_DIM, RET_VAL_DIM

    def body(sink_ref, qt_ref, kvc_ref, kvp_ref, kvtc_ref, kvtp_ref, qk_ref, v_ref, g_ref, gain_ref, intra_ref,
             qd_ref, kd_ref, ot_ref, ret_ref, yrp_ref, st_ref, state):
        i = pl.program_id(0)
        kwin = jnp.concatenate([kvp_ref[:, :LANE], kvc_ref[:, :LANE]], axis=0)
        vtwin = jnp.concatenate([kvtp_ref[LANE:, :], kvtc_ref[LANE:, :]], axis=1)
        for g in (0, 1):
            prob, _ = _att_probs(kwin, _att_group(qt_ref, g), sink_ref, g, i)
            out = _nn(vtwin[d * g:d * (g + 1), :], prob.astype(BF))
            for j in range(ATT_Q_PER_KV):
                h = ATT_Q_PER_KV * g + j
                ot_ref[d * h:d * (h + 1), :] = out[:, c * j:c * (j + 1)].astype(BF)

        @pl.when(i == 0)
        def _():
            state[...] = jnp.zeros_like(state)

        for h in range(RET_HEADS):
            qh = qk_ref[:, dk * h:dk * (h + 1)]
            kh = qk_ref[:, 512 + dk * h:512 + dk * (h + 1)]
            vh = v_ref[:, dv * h:dv * (h + 1)]
            sh = state[h]
            shb = sh.astype(BF)
            st_ref[0, h] = shb
            att = _nt(qh, kh) * intra_ref[h]
            inner = _nn(att.astype(BF), vh)
            cross = _nn((qh.astype(F32) * qd_ref[h]).astype(BF), shb)
            out = inner + cross
            state[h] = sh * RET_CHUNK_DECAY[h] + _tn((kh.astype(F32) * kd_ref[h]).astype(BF), vh)
            ret_ref[:, dv * h:dv * (h + 1)] = out.astype(BF)
            mu = jnp.mean(out, axis=-1, keepdims=True)
            dev = out - mu
            var = jnp.mean(dev * dev, axis=-1, keepdims=True)
            y = dev * lax.rsqrt(var + GN_EPS) * gain_ref[:, dv * h:dv * (h + 1)]
            g = g_ref[:, dv * h:dv * (h + 1)].astype(F32)
            yrp_ref[:, dv * h:dv * (h + 1)] = (g * _sigmoid(g) * y).astype(BF)

    prev = lambda i: jnp.maximum(i - 1, 0)
    return _call(
        body, (sinks, qt, kv, kv, kvt, kvt, qkr, vr, gr, gain, *rtabs), comm, name="seq_fwd", grid=(nb,),
        in_specs=[SMEM_FULL, _cols(512, c), _rows(c, 256), pl.BlockSpec((c, 256), lambda i: (prev(i), 0)),
                  _cols(256, c), pl.BlockSpec((256, c), lambda i: (0, prev(i))),
                  _rows(c, 1024), _rows(c, 1024), _rows(c, 1024), VMEM_FULL, VMEM_FULL, VMEM_FULL, VMEM_FULL],
        out_specs=[_cols(512, c), _rows(c, 1024), _rows(c, 1024),
                   pl.BlockSpec((1, RET_HEADS, dk, dv), lambda i: (i, 0, 0, 0))],
        out_shape=[jax.ShapeDtypeStruct((512, seq), BF), jax.ShapeDtypeStruct((seq, 1024), BF),
                   jax.ShapeDtypeStruct((seq, 1024), BF), jax.ShapeDtypeStruct((nb, RET_HEADS, dk, dv), BF)],
        scratch_shapes=[pltpu.VMEM((RET_HEADS, dk, dv), F32)])


def _mix_fwd(oat, yrp, gates, bg, x, w_att, w_ret, w_out):
    seq = x.shape[0]
    tm = min(TOKEN_TILE, seq)

    def body(oat_ref, yrp_ref, gates_ref, bg_ref, x_ref, wa_ref, wr_ref, wo_ref, ya_ref, yr_ref, mg_ref, h1_ref):
        ya = _tn(oat_ref[...], wa_ref[...])
        yr = _nn(yrp_ref[...], wr_ref[...])
        gt = _sigmoid(gates_ref[...].astype(F32) + bg_ref[...])
        merged = (gt[:, :D_MODEL] * ya + gt[:, D_MODEL:] * yr).astype(BF)
        ya_ref[...] = ya.astype(BF)
        yr_ref[...] = yr.astype(BF)
        mg_ref[...] = merged
        h1_ref[...] = x_ref[...] + _nn(merged, wo_ref[...])

    return pl.pallas_call(
        body, name="mix_fwd", grid=(seq // tm,),
        in_specs=[_cols(512, tm), _rows(tm, 1024), _rows(tm, 2048), VMEM_FULL, _rows(tm, D_MODEL),
                  VMEM_FULL, VMEM_FULL, VMEM_FULL],
        out_specs=[_rows(tm, D_MODEL)] * 4,
        out_shape=[jax.ShapeDtypeStruct((seq, D_MODEL), BF)] * 3 + [jax.ShapeDtypeStruct((seq, D_MODEL), F32)],
        compiler_params=_params(1),
    )(oat, yrp, gates, bg, x, w_att, w_ret, w_out)


def _mlp_loss_step(h1, g2, g3, target, w_ff1, w_ff2):
    seq = h1.shape[0]
    tm = min(MLP_TOKEN_TILE, seq)
    n_chunks = len(FF_CHUNKS) - 1

    def body(h1_ref, g2_ref, g3_ref, t_ref, w1_hbm, w2_hbm,
             xn2_ref, hdn_ref, dh2_ref, du_ref, dh1_ref, loss_ref, dg3_ref, dg2_ref,
             w1_vmem, w1_sems, w2_vmem, w2_sems, relu_u):
        w1 = _Resident(w1_hbm, w1_vmem, w1_sems, FF_CHUNKS)
        w2 = _Resident(w2_hbm, w2_vmem, w2_sems, FF_CHUNKS)
        _Resident.load(w1, w2)

        @pl.when(pl.program_id(0) == 0)
        def _():
            loss_ref[...] = jnp.zeros_like(loss_ref)
            dg3_ref[...] = jnp.zeros_like(dg3_ref)
            dg2_ref[...] = jnp.zeros_like(dg2_ref)

        h1v = h1_ref[...]
        r2 = lax.rsqrt(jnp.mean(h1v * h1v, axis=-1, keepdims=True) + NORM_EPS)
        xn2 = (h1v * r2 * g2_ref[...]).astype(BF)
        xn2_ref[...] = xn2
        h2 = h1v
        for c in range(n_chunks):
            cols = slice(FF_CHUNKS[c], FF_CHUNKS[c + 1])
            a = jnp.maximum(_nt(xn2, w1.chunk(c)[...]), 0.0)
            relu_u[:, cols] = a.astype(BF)
            hdn = jnp.square(a).astype(BF)
            hdn_ref[:, cols] = hdn
            h2 = h2 + _nn(hdn, w2.chunk(c)[...])
        r3 = lax.rsqrt(jnp.mean(h2 * h2, axis=-1, keepdims=True) + NORM_EPS)
        hn = h2 * r3
        err = hn * g3_ref[...] - t_ref[...]
        loss_ref[...] += jnp.sum(err * err) * (0.5 / D_MODEL)
        dy = err * (1.0 / D_MODEL)
        dg3_ref[...] += jnp.sum(dy * hn, axis=0, keepdims=True)
        z = dy * g3_ref[...]
        dh2 = r3 * (z - hn * jnp.mean(z * hn, axis=-1, keepdims=True))
        dh2b = dh2.astype(BF)
        dh2_ref[...] = dh2b
        dxn2 = jnp.zeros_like(dh2)
        for c in range(n_chunks):
            cols = slice(FF_CHUNKS[c], FF_CHUNKS[c + 1])
            du = (_nt(dh2b, w2.chunk(c)[...]) * (2.0 * relu_u[:, cols].astype(F32))).astype(BF)
            du_ref[:, cols] = du
            dxn2 = dxn2 + _nn(du, w1.chunk(c)[...])
        dnorm, dg = _rms_bwd(dxn2, h1v, g2_ref[...])
        dh1_ref[...] = dh2 + dnorm
        dg2_ref[...] += dg

    gain_acc = pl.BlockSpec((1, D_MODEL), lambda i: (0, 0))
    return pl.pallas_call(
        body, name="mlp_loss_step", grid=(seq // tm,),
        in_specs=[_rows(tm, D_MODEL), VMEM_FULL, VMEM_FULL, _rows(tm, D_MODEL), ANY, ANY],
        out_specs=[_rows(tm, D_MODEL), _rows(tm, D_FF), _rows(tm, D_MODEL), _rows(tm, D_FF), _rows(tm, D_MODEL),
                   pl.BlockSpec((1, LANE), lambda i: (0, 0)), gain_acc, gain_acc],
        out_shape=[jax.ShapeDtypeStruct((seq, D_MODEL), BF), jax.ShapeDtypeStruct((seq, D_FF), BF),
                   jax.ShapeDtypeStruct((seq, D_MODEL), BF), jax.ShapeDtypeStruct((seq, D_FF), BF),
                   jax.ShapeDtypeStruct((seq, D_MODEL), F32), jax.ShapeDtypeStruct((1, LANE), F32),
                   jax.ShapeDtypeStruct((1, D_MODEL), F32), jax.ShapeDtypeStruct((1, D_MODEL), F32)],
        scratch_shapes=(_Resident.scratch(w_ff1, FF_CHUNKS) + _Resident.scratch(w_ff2, FF_CHUNKS)
                        + [pltpu.VMEM((tm, D_FF), BF)]),
        compiler_params=_params(1),
    )(h1, g2, g3, target, w_ff1, w_ff2)


def _rms_bwd(dxn, xin, gain):
    r = lax.rsqrt(jnp.mean(xin * xin, axis=-1, keepdims=True) + NORM_EPS)
    xhat = xin * r
    z = dxn * gain
    dxin = r * (z - xhat * jnp.mean(z * xhat, axis=-1, keepdims=True))
    return dxin, jnp.sum(dxn * xhat, axis=0, keepdims=True)


def _mm_tn(a, b, name, a_is_t=False, comm=None):
    kdim, seq = a.shape if a_is_t else a.shape[::-1]
    ndim = b.shape[1]
    ts = min(TN_TOKEN_TILE, seq)
    tk = min(kdim, 1024)
    tn = min(ndim, 1024)
    n_steps = seq // ts

    def body(a_ref, b_ref, o_ref):
        mm = _nn if a_is_t else _tn
        part = mm(a_ref[...].astype(BF), b_ref[...].astype(BF))
        if n_steps == 1:
            o_ref[...] = part
        else:
            @pl.when(pl.program_id(2) == 0)
            def _():
                o_ref[...] = jnp.zeros_like(o_ref)

            o_ref[...] += part

    a_spec = (pl.BlockSpec((tk, ts), lambda k, n, s: (k, s)) if a_is_t
              else pl.BlockSpec((ts, tk), lambda k, n, s: (s, k)))
    res = _call(
        body, (a, b), comm, name=name, grid=(kdim // tk, ndim // tn, n_steps),
        in_specs=[a_spec, pl.BlockSpec((ts, tn), lambda k, n, s: (s, n))],
        out_specs=pl.BlockSpec((tk, tn), lambda k, n, s: (k, n)),
        out_shape=jax.ShapeDtypeStruct((kdim, ndim), F32))
    return res[0] if comm is None else res


def _dw_in_t(pieces, is_t, xn1, comm=None):
    seq = xn1.shape[0]
    tr = 256
    n_blk = [(p.shape[0] if t else p.shape[1]) // tr for p, t in zip(pieces, is_t)]
    offs = [sum(n_blk[:i]) for i in range(len(pieces) + 1)]
    n_p = len(pieces)

    def body(*refs):
        piece_refs, xn_ref, o_ref, ob_ref = refs[:n_p], refs[n_p], refs[n_p + 1], refs[n_p + 2]
        r = pl.program_id(0)
        for p in range(n_p):
            @pl.when((r >= offs[p]) & (r < offs[p + 1]))
            def _(p=p):
                res = (_nn if is_t[p] else _tn)(piece_refs[p][...], xn_ref[...])
                o_ref[...] = res
                ob_ref[...] = res.astype(BF)

    def piece_spec(p):
        def index(r):
            blk = jnp.clip(r - offs[p], 0, n_blk[p] - 1)
            return (blk, 0) if is_t[p] else (0, blk)
        return pl.BlockSpec((tr, seq) if is_t[p] else (seq, tr), index)

    return _call(
        body, (*pieces, xn1), comm, name="dw_in", grid=(offs[-1],),
        in_specs=[piece_spec(p) for p in range(n_p)] + [VMEM_FULL],
        out_specs=[_rows(tr, D_MODEL), _rows(tr, D_MODEL)],
        out_shape=[jax.ShapeDtypeStruct((offs[-1] * tr, D_MODEL), F32),
                   jax.ShapeDtypeStruct((offs[-1] * tr, D_MODEL), BF)])


def _mix_bwd(dh1, ya, yr, gates, bg, merged, oat, yrp, w_att, w_ret, w_out, comm=None):
    seq = dh1.shape[0]
    tm = min(TOKEN_TILE, seq)

    def body(dh1_ref, ya_ref, yr_ref, gates_ref, bg_ref, mg_ref, oat_ref, yrp_ref, wa_ref, wr_ref, wo_ref,
             dgates_ref, doa_ref, dyrp_ref, db_ref, dwa_ref, dwr_ref, dwo_ref):
        @pl.when(pl.program_id(0) == 0)
        def _():
            for ref in (db_ref, dwa_ref, dwr_ref, dwo_ref):
                ref[...] = jnp.zeros_like(ref)

        dh1b = dh1_ref[...].astype(BF)
        dm = _nt(dh1b, wo_ref[...])
        gt = _sigmoid(gates_ref[...].astype(F32) + bg_ref[...])
        ga, gr = gt[:, :D_MODEL], gt[:, D_MODEL:]
        dya = (dm * ga).astype(BF)
        dyr = (dm * gr).astype(BF)
        dga = dm * ya_ref[...].astype(F32) * ga * (1.0 - ga)
        dgr = dm * yr_ref[...].astype(F32) * gr * (1.0 - gr)
        dgates_ref[:, :D_MODEL] = dga.astype(BF)
        dgates_ref[:, D_MODEL:] = dgr.astype(BF)
        db_ref[:, :D_MODEL] += jnp.sum(dga, axis=0, keepdims=True)
        db_ref[:, D_MODEL:] += jnp.sum(dgr, axis=0, keepdims=True)
        doa_ref[...] = _nt(wa_ref[...], dya).astype(BF)
        dyrp_ref[...] = _nt(dyr, wr_ref[...]).astype(BF)
        dwo_ref[...] += _tn(mg_ref[...], dh1b)
        dwr_ref[...] += _tn(yrp_ref[...], dyr)
        dwa_ref[...] += _nn(oat_ref[...], dya)

    acc = lambda r, c: pl.BlockSpec((r, c), lambda i: (0, 0))
    return _call(
        body, (dh1, ya, yr, gates, bg, merged, oat, yrp, w_att, w_ret, w_out), comm, name="mix_bwd",
        grid=(seq // tm,),
        in_specs=[_rows(tm, D_MODEL), _rows(tm, D_MODEL), _rows(tm, D_MODEL), _rows(tm, 2048), VMEM_FULL,
                  _rows(tm, D_MODEL), _cols(512, tm), _rows(tm, 1024), VMEM_FULL, VMEM_FULL, VMEM_FULL],
        out_specs=[_rows(tm, 2048), _cols(512, tm), _rows(tm, 1024), acc(1, 2048), acc(512, D_MODEL),
                   acc(1024, D_MODEL), acc(D_MODEL, D_MODEL)],
        out_shape=[jax.ShapeDtypeStruct((seq, 2048), BF), jax.ShapeDtypeStruct((512, seq), BF),
                   jax.ShapeDtypeStruct((seq, 1024), BF), jax.ShapeDtypeStruct((1, 2048), F32),
                   jax.ShapeDtypeStruct((512, D_MODEL), F32), jax.ShapeDtypeStruct((1024, D_MODEL), F32),
                   jax.ShapeDtypeStruct((D_MODEL, D_MODEL), F32)])


def _seq_bwd(qt, kv, kvt, dot, sinks, qkr, vr, gr, ret, dyrp, states, gain, rtabs, tabs, comm=None):
    seq = kv.shape[0]
    c = ATT_BLOCK
    assert c == RET_CHUNK
    d = ATT_HEAD_DIM
    nb = seq // c
    dk, dv = RET_KEY_DIM, RET_VAL_DIM
    ca, sa, sb, cr, sr, ct, st = tabs

    def body(sink_ref, qt_ref, kvc_ref, kvp_ref, kvtc_ref, kvtp_ref, dot_ref, ca_ref, sa_ref, sb_ref, ct_ref, st_ref,
             qk_ref, v_ref, g_ref, ret_ref, dyp_ref, st8_ref, gain_ref, intra_ref, qd_ref, kd_ref, cr_ref, sr_ref,
             dqt_ref, dkv_ref, dsink_ref, dall_ref, dgain_ref, carry, dstate):
        step = pl.program_id(0)
        i = nb - 1 - step

        @pl.when(step == 0)
        def _():
            carry[...] = jnp.zeros_like(carry)
            dsink_ref[...] = jnp.zeros_like(dsink_ref)
            dstate[...] = jnp.zeros_like(dstate)
            dgain_ref[...] = jnp.zeros_like(dgain_ref)

        kwin = jnp.concatenate([kvp_ref[:, :LANE], kvc_ref[:, :LANE]], axis=0)
        vwin = jnp.concatenate([kvp_ref[:, LANE:], kvc_ref[:, LANE:]], axis=0)
        ktwin = jnp.concatenate([kvtp_ref[:LANE, :], kvtc_ref[:LANE, :]], axis=1)
        ctv, stv = ct_ref[...], st_ref[...]
        lane1 = lax.broadcasted_iota(jnp.int32, (1, LANE), 1)
        dkw = jnp.zeros((2 * c, LANE), F32)
        dvw = jnp.zeros((2 * c, LANE), F32)
        dsink = jnp.zeros((1, LANE), F32)
        for g in (0, 1):
            qpad, dopad = _att_group(qt_ref, g), _att_group(dot_ref, g)
            prob, psink = _att_probs(kwin, qpad, sink_ref, g, i)
            dprob = _nn(vwin, dopad)
            drow = jnp.sum(dprob * prob, axis=0, keepdims=True)
            ds = (prob * (dprob - drow) * ATT_SCALE).astype(BF)
            dqg = _nn(ktwin[d * g:d * (g + 1), :], ds)
            dkw = dkw + _nt(ds, qpad)
            dvw = dvw + _nt(prob.astype(BF), dopad)
            dsink_lanes = psink * drow
            for j in range(ATT_Q_PER_KV):
                h = ATT_Q_PER_KV * g + j
                dqt_ref[d * h:d * (h + 1), :] = _rope_att_rows(dqg[:, c * j:c * (j + 1)], ctv, stv, -1.0).astype(BF)
                dsink = dsink + jnp.where(lane1 == h, -jnp.sum(dsink_lanes[:, c * j:c * (j + 1)]), 0.0)
        dsink_ref[...] += dsink
        total = carry[...] + jnp.concatenate([dkw[c:], dvw[c:]], axis=1)
        dkv_ref[:, :LANE] = _rope_att_t(total[:, :LANE], ca_ref[...], sa_ref[...], sb_ref[...]).astype(BF)
        dkv_ref[:, LANE:] = total[:, LANE:].astype(BF)
        carry[...] = jnp.concatenate([dkw[:c], dvw[:c]], axis=1)

        crv, srv = cr_ref[...], sr_ref[...]
        for h in range(RET_HEADS):
            vs = slice(dv * h, dv * (h + 1))
            qh = qk_ref[:, dk * h:dk * (h + 1)]
            kh = qk_ref[:, 512 + dk * h:512 + dk * (h + 1)]
            vh = v_ref[:, vs]
            out = ret_ref[:, vs].astype(F32)
            g = g_ref[:, vs].astype(F32)
            dyp = dyp_ref[:, vs].astype(F32)
            gain_h = gain_ref[:, vs]
            mu = jnp.mean(out, axis=-1, keepdims=True)
            dev = out - mu
            rstd = lax.rsqrt(jnp.mean(dev * dev, axis=-1, keepdims=True) + GN_EPS)
            yn = dev * rstd
            sg = _sigmoid(g)
            dg = dyp * (yn * gain_h) * (sg * (1.0 + g * (1.0 - sg)))
            dy = dyp * (g * sg)
            dgain_ref[:, vs] += jnp.sum(dy * yn, axis=0, keepdims=True)
            dyn = dy * gain_h
            dout = rstd * (dyn - jnp.mean(dyn, axis=-1, keepdims=True)
                           - yn * jnp.mean(dyn * yn, axis=-1, keepdims=True))
            doutb = dout.astype(BF)
            sc = st8_ref[0, h]
            dsp = dstate[h]
            dspb = dsp.astype(BF)
            intra, qdv, kdv = intra_ref[h], qd_ref[h], kd_ref[h]
            att = _nt(qh, kh) * intra
            dab = (_nt(doutb, vh) * intra).astype(BF)
            qdec = (qh.astype(F32) * qdv).astype(BF)
            kdec = (kh.astype(F32) * kdv).astype(BF)
            dq = _nn(dab, kh) + _nt(doutb, sc) * qdv
            dkk = _tn(dab, qh) + _nt(vh, dspb) * kdv
            dvv = _tn(att.astype(BF), doutb) + _nn(kdec, dspb)
            dstate[h] = dsp * RET_CHUNK_DECAY[h] + _tn(qdec, doutb)
            dall_ref[:, dk * h:dk * (h + 1)] = _rope_ret_t(dq, crv, srv).astype(BF)
            dall_ref[:, 512 + dk * h:512 + dk * (h + 1)] = (_rope_ret_t(dkk, crv, srv) * RET_SCALE).astype(BF)
            dall_ref[:, 1024 + dv * h:1024 + dv * (h + 1)] = dvv.astype(BF)
            dall_ref[:, 2048 + dv * h:2048 + dv * (h + 1)] = dg.astype(BF)

    cur = lambda s: nb - 1 - s
    prev = lambda s: jnp.maximum(nb - 2 - s, 0)
    rows = lambda w: pl.BlockSpec((c, w), lambda s: (cur(s), 0))
    cols = lambda h: pl.BlockSpec((h, c), lambda s: (0, cur(s)))
    acc = lambda w: pl.BlockSpec((1, w), lambda s: (0, 0))
    return _call(
        body, (sinks, qt, kv, kv, kvt, kvt, dot, ca, sa, sb, ct, st, qkr, vr, gr, ret, dyrp, states, gain, *rtabs,
               cr, sr), comm, name="seq_bwd", grid=(nb,),
        in_specs=[SMEM_FULL, cols(512), rows(256), pl.BlockSpec((c, 256), lambda s: (prev(s), 0)), cols(256),
                  pl.BlockSpec((256, c), lambda s: (0, prev(s))), cols(512), rows(LANE), rows(LANE), rows(LANE),
                  cols(8), cols(8), rows(1024), rows(1024), rows(1024), rows(1024), rows(1024),
                  pl.BlockSpec((1, RET_HEADS, dk, dv), lambda s: (cur(s), 0, 0, 0)),
                  VMEM_FULL, VMEM_FULL, VMEM_FULL, VMEM_FULL, rows(LANE), rows(LANE)],
        out_specs=[cols(512), rows(256), acc(LANE), rows(3072), acc(1024)],
        out_shape=[jax.ShapeDtypeStruct((512, seq), BF), jax.ShapeDtypeStruct((seq, 256), BF),
                   jax.ShapeDtypeStruct((1, LANE), F32), jax.ShapeDtypeStruct((seq, 3072), BF),
                   jax.ShapeDtypeStruct((1, 1024), F32)],
        scratch_shapes=[pltpu.VMEM((c, 256), F32), pltpu.VMEM((RET_HEADS, dk, dv), F32)])


def _inproj_bwd(dqa, dkva, dret, dgates, dh1, x, g1, w_in, comm=None):
    seq = x.shape[0]
    tm = min(TOKEN_TILE, seq)

    def body(dqa_ref, dkva_ref, dret_ref, dgates_ref, dh1_ref, x_ref, g_ref, w_hbm, dx_ref, dg1_ref, w_vmem, w_sems):
        w = _Resident(w_hbm, w_vmem, w_sems, IN_CHUNKS_BWD)
        _Resident.load(w)

        @pl.when(pl.program_id(0) == 0)
        def _():
            dg1_ref[...] = jnp.zeros_like(dg1_ref)

        dxn = _tn(dqa_ref[...], w.chunk(0)[...])
        dxn = dxn + _nn(dkva_ref[...], w.chunk(1)[...])
        dxn = dxn + _nn(dret_ref[...], w.chunk(2)[...])
        dxn = dxn + _nn(dgates_ref[...], w.chunk(3)[...])
        dnorm, dg = _rms_bwd(dxn, x_ref[...], g_ref[...])
        dx_ref[...] = dh1_ref[...] + dnorm
        dg1_ref[...] += dg

    return _call(
        body, (dqa, dkva, dret, dgates, dh1, x, g1, w_in), comm, name="inproj_bwd", grid=(seq // tm,),
        in_specs=[_cols(512, tm), _rows(tm, 256), _rows(tm, 3072), _rows(tm, 2048), _rows(tm, D_MODEL),
                  _rows(tm, D_MODEL), VMEM_FULL, ANY],
        out_specs=[_rows(tm, D_MODEL), pl.BlockSpec((1, D_MODEL), lambda i: (0, 0))],
        out_shape=[jax.ShapeDtypeStruct((seq, D_MODEL), F32), jax.ShapeDtypeStruct((1, D_MODEL), F32)],
        scratch_shapes=_Resident.scratch(w_in, IN_CHUNKS_BWD))


def _local_step(x, target, g1, bg, sinks, gain, g2, g3, shards):
    rtabs = _ret_tables()
    scatter = _Scatter

    comm, unpack = _gather(shards, ("w_in",))
    (xn1, ca, sa, sb, cr, sr, ct, st), got = _prologue(x, g1, comm)
    (w_in,) = unpack(got)
    comm_mid, unpack_mid = _gather(shards, ("w_att_up", "w_ret_up", "w_out"))
    comm_ff1, unpack_ff1 = _gather(shards, ("w_ff1",))
    (qt, kv, kvt, qkr, vr, gr, gates), got = _inproj_fwd(xn1, w_in, (ca, sa, sb, cr, sr, ct, st),
                                                         _join(comm_mid, comm_ff1))
    w_att, w_ret, w_out = unpack_mid(got[:3])
    (w_ff1,) = unpack_ff1(got[3:])
    comm, unpack = _gather(shards, ("w_ff2",))
    (oa, ret, yrp, states), got = _seq_fwd(qt, kv, kvt, sinks, qkr, vr, gr, gain, rtabs, comm)
    (w_ff2,) = unpack(got)
    ya, yr, merged, h1 = _mix_fwd(oa, yrp, gates, bg, x, w_att, w_ret, w_out)
    xn2, hdn, dh2, du, dh1, loss, dg3, dg2 = _mlp_loss_step(h1, g2, g3, target, w_ff1, w_ff2)

    ff2 = scatter(dict(w_ff2=_mm_tn(hdn, dh2, "dw_ff2")), "ff2")
    ff1 = scatter(dict(w_ff1=_mm_tn(du, xn2, "dw_ff1")), "ff1")
    (dgates, doa, dyrp, db, d_att, d_ret, d_out), got = _mix_bwd(
        dh1, ya, yr, gates, bg, merged, oa, yrp, w_att, w_ret, w_out, _join(ff2.pair_comm(), ff1.pair_comm()))
    _Scatter.pairs_done((ff2, ff1), (got[:1], got[1:]), "ff")
    mid = scatter(dict(w_att_up=d_att, w_ret_up=d_ret, w_out=d_out), "mid")
    (dqa, dkva, dsink, dret, dgain), got = _seq_bwd(
        qt, kv, kvt, doa, sinks, qkr, vr, gr, ret, dyrp, states, gain, rtabs, (ca, sa, sb, cr, sr, ct, st),
        _join(_join(ff2.chip_comm(), ff1.chip_comm()), mid.pair_comm()))
    ff2.chip_done(got[:1])
    ff1.chip_done(got[1:2])
    mid.pair_done(got[2:])
    (d_in, d_in_bf), r2 = _dw_in_t((dqa, dkva, dret, dgates), (True, False, False, False), xn1, mid.chip_comm())
    mid.chip_done(r2)
    win = scatter(dict(w_in=d_in), "in", sent=dict(w_in=d_in_bf))
    win.pair_done(_comm_alone(win.pair_comm(), "pair_exchange_in"))
    (dx, dg1), r2 = _inproj_bwd(dqa, dkva, dret, dgates, dh1, x, g1, w_in, win.chip_comm())
    win.chip_done(r2)
    small = dict(norm_mix_gain=dg1, b_gates=db, attn_sinks=dsink, ret_gn_gain=dgain, norm_mlp_gain=dg2,
                 norm_final_gain=dg3)
    return loss, dx, small, (win, mid, ff1, ff2)


def _coords():
    return lax.axis_index("x"), lax.axis_index("y"), lax.axis_index("c")


def _flip(v, bit):
    return 1 - v if bit else v


def _xor(a, b):
    return a + b - 2 * a * b


def _gather_relay_comm(shards):
    n = len(shards)

    def parts(ins, outs, sems):
        send_sems, recv_sems, local_sems = sems
        x, y, c = _coords()
        me, sib = (x, y, c), (x, y, 1 - c)

        def ring(core):
            return ((_xor(x, core), _xor(y, 1 - core), core), (_xor(x, 1 - core), _xor(y, core), core),
                    (1 - x, 1 - y, core))

        def slot(a, blk):
            return outs[a].at[4 * blk[0] + 2 * blk[1] + blk[2]]

        def copy(a, k, blk, to, src=None):
            return pltpu.make_async_remote_copy(
                src_ref=slot(a, blk) if src is None else src, dst_ref=slot(a, blk),
                send_sem=send_sems.at[a, k], recv_sem=recv_sems.at[a, k], device_id=to, device_id_type=MESH)

        mine = [pltpu.make_async_copy(ins[a], slot(a, me), local_sems.at[a]) for a in range(n)]
        return me, sib, ring, copy, mine

    def first_copies(ins, outs, sems):
        me, sib, ring, copy, mine = parts(ins, outs, sems)
        src, dst, _ = ring(me[2])
        return mine, [copy(a, k, me, to, src=ins[a]) for a in range(n) for k, to in ((1, src), (2, dst), (0, sib))]

    def start(ins, outs, sems):
        mine, first = first_copies(ins, outs, sems)
        for cp in mine + first:
            cp.start()

    def relay(ins, outs, sems):
        me, sib, ring, copy, _ = parts(ins, outs, sems)
        src, dst, _ = ring(me[2])
        for a in range(n):
            copy(a, 1, src, me).wait_recv()
            copy(a, 3, src, dst).start()
            copy(a, 4, src, sib).start()
        for a in range(n):
            copy(a, 2, dst, me).wait_recv()
            copy(a, 5, dst, sib).start()

    def finish(ins, outs, sems):
        me, sib, ring, copy, _ = parts(ins, outs, sems)
        mine, first = first_copies(ins, outs, sems)
        src, dst, diag = ring(me[2])
        passed = [copy(a, k, blk, to) for a in range(n) for k, blk, to in ((3, src, dst), (4, src, sib), (5, dst, sib))]
        for a in range(n):
            copy(a, 3, diag, me).wait_recv()
            fwd = copy(a, 6, diag, sib)
            fwd.start()
            passed.append(fwd)
        s_src, s_dst, s_diag = ring(sib[2])
        for a in range(n):
            for k, blk in ((0, sib), (4, s_src), (5, s_dst), (6, s_diag)):
                copy(a, k, blk, me).wait_recv()
        for cp in first + passed:
            cp.wait_send()
        for cp in mine:
            cp.wait()

    return _Comm(list(shards), [jax.ShapeDtypeStruct((N_DEV,) + s.shape, s.dtype) for s in shards],
                 [pltpu.SemaphoreType.DMA((n, 7)), pltpu.SemaphoreType.DMA((n, 7)), pltpu.SemaphoreType.DMA((n,))],
                 start, finish, relay)


COLUMN_SHARDED = ("w_in", "w_ff1")
COLUMN_RELAID = ("w_att_up",)


def _gather(shards, names):
    def unpack(got):
        return [jnp.transpose(g, (1, 0, 2)).reshape(g.shape[1], N_DEV * g.shape[2]) if k in COLUMN_RELAID
                else g.reshape(N_DEV * g.shape[1], g.shape[2]) for k, g in zip(names, got)]

    return _gather_relay_comm([shards[k] for k in names]), unpack


def _pair_comm(grads):
    n = len(grads)

    def copies(g, r1, sems):
        send_sems, recv_sems = sems
        x, y, c = _coords()
        return [pltpu.make_async_remote_copy(
            src_ref=g[a].at[2 * j + (1 - c)], dst_ref=r1[a].at[j], send_sem=send_sems.at[a, j],
            recv_sem=recv_sems.at[a, j], device_id=(x, y, 1 - c), device_id_type=MESH)
            for a in range(n) for j in range(4)]

    def start(g, r1, sems):
        for cp in copies(g, r1, sems):
            cp.start()

    def finish(g, r1, sems):
        for cp in copies(g, r1, sems):
            cp.wait_recv()
        for cp in copies(g, r1, sems):
            cp.wait_send()

    return _Comm(list(grads), [jax.ShapeDtypeStruct((4,) + g.shape[1:], g.dtype) for g in grads],
                 [pltpu.SemaphoreType.DMA((n, 4)), pltpu.SemaphoreType.DMA((n, 4))], start, finish)


def _small_comm(small):
    def parts(ins, outs, sems):
        (small_ref,), (small_all,) = ins, outs
        ssend, srecv, lsem = sems
        x, y, c = _coords()
        me_idx = 4 * x + 2 * y + c
        own = pltpu.make_async_copy(small_ref, small_all.at[me_idx], lsem)
        sends, recvs = [], []
        for r in range(1, N_DEV):
            px, py, pc = _flip(x, r & 4), _flip(y, r & 2), _flip(c, r & 1)
            sends.append(pltpu.make_async_remote_copy(
                src_ref=small_ref, dst_ref=small_all.at[me_idx], send_sem=ssend.at[r - 1], recv_sem=srecv.at[r - 1],
                device_id=(px, py, pc), device_id_type=MESH))
            recvs.append(pltpu.make_async_remote_copy(
                src_ref=small_ref, dst_ref=small_all.at[4 * px + 2 * py + pc], send_sem=ssend.at[r - 1],
                recv_sem=srecv.at[r - 1], device_id=(px, py, pc), device_id_type=MESH))
        return own, sends, recvs

    def start(ins, outs, sems):
        own, sends, _ = parts(ins, outs, sems)
        own.start()
        for cp in sends:
            cp.start()

    def finish(ins, outs, sems):
        own, sends, recvs = parts(ins, outs, sems)
        for cp in recvs:
            cp.wait_recv()
        for cp in sends:
            cp.wait_send()
        own.wait()

    return _Comm([small], [jax.ShapeDtypeStruct((N_DEV,) + small.shape, small.dtype)],
                 [pltpu.SemaphoreType.DMA((N_DEV - 1,)), pltpu.SemaphoreType.DMA((N_DEV - 1,)),
                  pltpu.SemaphoreType.DMA], start, finish)


def _pair_sum(grads, r1s, c_arr, tag):
    n = len(grads)
    q = 1

    def body(c_ref, *refs):
        g, r, t = refs[:n], refs[n:2 * n], refs[2 * n:]
        for a in range(n):
            t[a][...] = (g[a][...] + r[a][...]).astype(t[a].dtype)

    def blk(arr):
        return (1, arr.shape[1] // q, arr.shape[2])

    grid_spec = pltpu.PrefetchScalarGridSpec(
        num_scalar_prefetch=1, grid=(4, q),
        in_specs=[pl.BlockSpec(blk(g), lambda j, s, c_ref: (2 * j + c_ref[0], s, 0)) for g in grads]
        + [pl.BlockSpec(blk(r), lambda j, s, c_ref: (j, s, 0)) for r in r1s],
        out_specs=[pl.BlockSpec(blk(r), lambda j, s, c_ref: (j, s, 0)) for r in r1s])
    return pl.pallas_call(
        body, name="pair_sum_" + tag, grid_spec=grid_spec,
        out_shape=[jax.ShapeDtypeStruct(r.shape, RS_PAYLOAD) for r in r1s],
        compiler_params=_params(2),
    )(c_arr, *grads, *r1s)


def _chip_comm(ts):
    n = len(ts)

    def copies(t, r2, sems):
        send_sems, recv_sems = sems
        x, y, c = _coords()
        out = []
        for a in range(n):
            for r in range(1, 4):
                tx, ty = _flip(x, r & 2), _flip(y, r & 1)
                out.append(pltpu.make_async_remote_copy(
                    src_ref=t[a].at[2 * tx + ty], dst_ref=r2[a].at[r - 1], send_sem=send_sems.at[a, r - 1],
                    recv_sem=recv_sems.at[a, r - 1], device_id=(tx, ty, c), device_id_type=MESH))
        return out

    def start(t, r2, sems):
        for cp in copies(t, r2, sems):
            cp.start()

    def finish(t, r2, sems):
        for cp in copies(t, r2, sems):
            cp.wait_recv()
        for cp in copies(t, r2, sems):
            cp.wait_send()

    return _Comm(list(ts), [jax.ShapeDtypeStruct((3,) + t.shape[1:], t.dtype) for t in ts],
                 [pltpu.SemaphoreType.DMA((n, 3)), pltpu.SemaphoreType.DMA((n, 3))], start, finish)


class _Scatter:
    def __init__(self, grads, tag, sent=None):
        def stack(k, g):
            if k in COLUMN_RELAID:
                return jnp.transpose(g.reshape(g.shape[0], N_DEV, g.shape[1] // N_DEV), (1, 0, 2))
            return g.reshape(N_DEV, g.shape[0] // N_DEV, g.shape[1])

        self.names, self.tag = tuple(grads), tag
        self.stacks = [stack(k, g) for k, g in grads.items()]
        self.sent = self.stacks if sent is None else [stack(k, sent[k]) for k in grads]

    def pair_comm(self):
        return _pair_comm(self.sent)

    def pair_done(self, r1s):
        _Scatter.pairs_done((self,), (r1s,), self.tag)

    @staticmethod
    def pairs_done(groups, r1s_of, tag):
        for grp, r1s in zip(groups, r1s_of):
            grp.r1s = list(r1s)
        c_arr = jnp.reshape(lax.axis_index("c"), (1,)).astype(jnp.int32)
        ts = _pair_sum([s for grp in groups for s in grp.stacks], [r for grp in groups for r in grp.r1s], c_arr, tag)
        for grp in groups:
            grp.ts, ts = ts[:len(grp.stacks)], ts[len(grp.stacks):]

    def chip_comm(self):
        return _chip_comm(self.ts)

    def chip_done(self, r2s):
        self.r2s = list(r2s)


def _adamw(w, g, m, v):
    m = ADAM_B1 * m + (1.0 - ADAM_B1) * g
    v = ADAM_B2 * v + (1.0 - ADAM_B2) * jnp.square(g)
    m_hat = m / (1.0 - ADAM_B1 ** ADAM_STEP)
    v_hat = v / (1.0 - ADAM_B2 ** ADAM_STEP)
    delta = -ADAM_LR * (m_hat / (jnp.sqrt(v_hat) + ADAM_EPS) + ADAM_WD * w)
    return delta, m, v


ADAM_STEPS = 4


def _piece_specs(stacks):
    def rows(s):
        return s.shape[1] // ADAM_STEPS
    return ([pl.BlockSpec((1, rows(s), s.shape[2]), lambda i, idx_ref: (idx_ref[0], i, 0)) for s in stacks]
            + [pl.BlockSpec((1, rows(s), s.shape[2]), lambda i, idx_ref: (idx_ref[1], i, 0)) for s in stacks]
            + [pl.BlockSpec((3, rows(s), s.shape[2]), lambda i, idx_ref: (0, i, 0)) for s in stacks])


def _piece_sum(g0, r1, r2):
    return (((g0[0] + r1[0]) + r2[0].astype(F32)) + r2[1].astype(F32)) + r2[2].astype(F32)


def _shard_sum(stacks, r1s, r2s, idx_arr):
    n = len(stacks)

    def body(idx_ref, *refs):
        g0, r1, r2, outs = (refs[k * n:(k + 1) * n] for k in range(4))
        for a in range(n):
            outs[a][...] = _piece_sum(g0[a], r1[a], r2[a])

    out_specs = [pl.BlockSpec((s.shape[1] // ADAM_STEPS, s.shape[2]), lambda i, idx_ref: (i, 0)) for s in stacks]
    grid_spec = pltpu.PrefetchScalarGridSpec(num_scalar_prefetch=1, grid=(ADAM_STEPS,),
                                             in_specs=_piece_specs(stacks), out_specs=out_specs)
    return pl.pallas_call(
        body, name="shard_sum", grid_spec=grid_spec,
        out_shape=[jax.ShapeDtypeStruct(s.shape[1:], F32) for s in stacks],
        compiler_params=_params(1),
    )(idx_arr, *stacks, *r1s, *r2s)


def _adam_big(pieces, summed, ws, ms, vs, idx_arr):
    stacks, r1s, r2s = pieces
    n_p, n = len(stacks), len(ws)

    def body(idx_ref, *refs):
        it = iter(refs)
        g0, r1, r2, gs, w, m, v = ([next(it) for _ in range(k)] for k in (n_p, n_p, n_p, n - n_p, n, n, n))
        outs = list(it)
        for a in range(n):
            g = _piece_sum(g0[a], r1[a], r2[a]) if a < n_p else gs[a - n_p][...]
            delta, nm, nv = _adamw(w[a][...], g, m[a][...], v[a][...])
            outs[4 * a][...] = g
            outs[4 * a + 1][...] = delta
            outs[4 * a + 2][...] = nm
            outs[4 * a + 3][...] = nv

    def shard_spec(w):
        return pl.BlockSpec((w.shape[0] // ADAM_STEPS, w.shape[1]), lambda i, idx_ref: (i, 0))

    in_specs = _piece_specs(stacks) + [shard_spec(w) for w in ws[n_p:]] + [shard_spec(w) for w in ws] * 3
    out_specs = [shard_spec(w) for w in ws for _ in range(4)]
    grid_spec = pltpu.PrefetchScalarGridSpec(num_scalar_prefetch=1, grid=(ADAM_STEPS,), in_specs=in_specs,
                                             out_specs=out_specs)
    return pl.pallas_call(
        body, name="adam_big", grid_spec=grid_spec,
        out_shape=[jax.ShapeDtypeStruct(w.shape, F32) for w in ws for _ in range(4)],
        compiler_params=_params(1),
    )(idx_arr, *stacks, *r1s, *r2s, *summed, *ws, *ms, *vs)


def _adam_small(small_all, w, m, v):
    def body(all_ref, w_ref, m_ref, v_ref, g_ref, d_ref, nm_ref, nv_ref):
        g = all_ref[0]
        for k in range(1, N_DEV):
            g = g + all_ref[k]
        delta, nm, nv = _adamw(w_ref[...], g, m_ref[...], v_ref[...])
        g_ref[...] = g
        d_ref[...] = delta
        nm_ref[...] = nm
        nv_ref[...] = nv

    return pl.pallas_call(
        body, name="adam_small",
        in_specs=[VMEM_FULL] * 4, out_specs=[VMEM_FULL] * 4,
        out_shape=[jax.ShapeDtypeStruct(w.shape, F32)] * 4,
    )(small_all, w, m, v)


RS_PAYLOAD = BF


def _pack_rows(rows):
    rid = lax.broadcasted_iota(jnp.int32, (8, D_MODEL), 0)
    out = jnp.zeros((8, D_MODEL), F32)
    for i, r in enumerate(rows):
        out = jnp.where(rid == i, jnp.broadcast_to(r, (8, D_MODEL)), out)
    return out


def _small_rows(norm_mix_gain, b_gates, attn_sinks, ret_gn_gain, norm_mlp_gain, norm_final_gain):
    return [norm_mix_gain, b_gates[:, :D_MODEL], b_gates[:, D_MODEL:], ret_gn_gain, norm_mlp_gain,
            norm_final_gain.reshape(1, D_MODEL), jnp.pad(attn_sinks, ((0, 0), (0, D_MODEL - ATT_HEADS)))]


def _unpack_small(p):
    return dict(norm_mix_gain=p[0:1], b_gates=jnp.concatenate([p[1:2], p[2:3]], axis=1), ret_gn_gain=p[3:4],
                norm_mlp_gain=p[4:5], norm_final_gain=p[5], attn_sinks=p[6:7, :ATT_HEADS])


WEIGHTS = ("norm_mix_gain", "w_in", "b_gates", "attn_sinks", "ret_gn_gain", "w_att_up", "w_ret_up", "w_out",
           "norm_mlp_gain", "w_ff1", "w_ff2", "norm_final_gain")
BIG = ("w_in", "w_att_up", "w_ret_up", "w_out", "w_ff1", "w_ff2")


def kernel(x, norm_mix_gain, w_in, b_gates, attn_sinks, ret_gn_gain, w_att_up, w_ret_up, w_out, norm_mlp_gain, w_ff1, w_ff2, norm_final_gain, loss_target, m_norm_mix_gain, m_w_in, m_b_gates, m_attn_sinks, m_ret_gn_gain, m_w_att_up, m_w_ret_up, m_w_out, m_norm_mlp_gain, m_w_ff1, m_w_ff2, m_norm_final_gain, v_norm_mix_gain, v_w_in, v_b_gates, v_attn_sinks, v_ret_gn_gain, v_w_att_up, v_w_ret_up, v_w_out, v_norm_mlp_gain, v_w_ff1, v_w_ff2, v_norm_final_gain):
    w = dict(norm_mix_gain=norm_mix_gain, w_in=w_in, b_gates=b_gates, attn_sinks=attn_sinks, ret_gn_gain=ret_gn_gain,
             w_att_up=w_att_up, w_ret_up=w_ret_up, w_out=w_out, norm_mlp_gain=norm_mlp_gain, w_ff1=w_ff1,
             w_ff2=w_ff2, norm_final_gain=norm_final_gain)
    m = dict(norm_mix_gain=m_norm_mix_gain, w_in=m_w_in, b_gates=m_b_gates, attn_sinks=m_attn_sinks,
             ret_gn_gain=m_ret_gn_gain, w_att_up=m_w_att_up, w_ret_up=m_w_ret_up, w_out=m_w_out,
             norm_mlp_gain=m_norm_mlp_gain, w_ff1=m_w_ff1, w_ff2=m_w_ff2, norm_final_gain=m_norm_final_gain)
    v = dict(norm_mix_gain=v_norm_mix_gain, w_in=v_w_in, b_gates=v_b_gates, attn_sinks=v_attn_sinks,
             ret_gn_gain=v_ret_gn_gain, w_att_up=v_w_att_up, w_ret_up=v_w_ret_up, w_out=v_w_out,
             norm_mlp_gain=v_norm_mlp_gain, w_ff1=v_w_ff1, w_ff2=v_w_ff2, norm_final_gain=v_norm_final_gain)

    shards = {k: (w[k][0].T if k in COLUMN_SHARDED else w[k][0]).astype(BF) for k in BIG}
    loss_p, dx, small, groups = _local_step(
        x[0], loss_target[0], norm_mix_gain, b_gates, attn_sinks[0], ret_gn_gain, norm_mlp_gain,
        norm_final_gain.reshape(1, D_MODEL), shards)

    by_name = {}
    for grp in groups:
        for k, stack, r1, r2 in zip(grp.names, grp.stacks, grp.r1s, grp.r2s):
            by_name[k] = (stack, r1, r2)
    lane = lax.broadcasted_iota(jnp.int32, (1, D_MODEL), 1)
    sink_row = jnp.where(lane < ATT_HEADS, jnp.pad(small["attn_sinks"], ((0, 0), (0, D_MODEL - LANE))),
                         jnp.where(lane == ATT_HEADS, jnp.pad(loss_p, ((0, 0), (0, D_MODEL - LANE)), mode="edge"),
                                   0.0))
    small_pack = _pack_rows([small["norm_mix_gain"], small["b_gates"][:, :D_MODEL], small["b_gates"][:, D_MODEL:],
                             small["ret_gn_gain"], small["norm_mlp_gain"], small["norm_final_gain"], sink_row])
    xi, yi, ci = _coords()
    (small_all,) = _comm_alone(_small_comm(small_pack), "small_exchange")
    idx_arr = jnp.stack([4 * xi + 2 * yi + ci, 2 * xi + yi]).astype(jnp.int32)
    in_pieces = ["w_att_up", "w_ret_up", "w_out", "w_ff2", "w_in"]
    in_sum = ["w_ff1"]
    order = in_pieces + in_sum
    summed_t = _shard_sum(*[[by_name[k][i] for k in in_sum] for i in range(3)], idx_arr)

    def shard(tree, k):
        return tree[k][0].T if k == "w_in" else tree[k][0]

    adam_out = _adam_big([[by_name[k][i] for k in in_pieces] for i in range(3)], [g.T for g in summed_t],
                         [shard(w, k) for k in order], [shard(m, k) for k in order], [shard(v, k) for k in order],
                         idx_arr)
    big_out = [adam_out[4 * order.index(k) + i].T if k == "w_in" else adam_out[4 * order.index(k) + i]
               for k in BIG for i in range(4)]
    sm_g, sm_d, sm_m, sm_v = _adam_small(small_all, _pack_rows(_small_rows(*[w[k] for k in WEIGHTS if k not in BIG])),
                                         _pack_rows(_small_rows(*[m[k] for k in WEIGHTS if k not in BIG])),
                                         _pack_rows(_small_rows(*[v[k] for k in WEIGHTS if k not in BIG])))

    loss = sm_g[6, ATT_HEADS]
    kinds = []
    for idx, packed in enumerate((sm_g, sm_d, sm_m, sm_v)):
        out = _unpack_small(packed)
        for a, k in enumerate(BIG):
            out[k] = big_out[4 * a + idx][None]
        kinds.append(out)
    return (loss, dx[None], *[kinds[0][k] for k in WEIGHTS], *[kinds[1][k] for k in WEIGHTS],
            *[kinds[2][k] for k in WEIGHTS], *[kinds[3][k] for k in WEIGHTS])
```

```python
import functools
import math

import jax
import jax.numpy as jnp
from jax import lax
from jax.experimental import pallas as pl
from jax.experimental.pallas import tpu as pltpu

F32 = jnp.float32
BF = jnp.bfloat16
MESH = pl.DeviceIdType.MESH

D_MODEL = 1024
ATT_HEADS = 8
ATT_HEAD_DIM = 64
ATT_BLOCK = 128
ROPE_DIM = 16
ROPE_THETA = 500000.0
RET_HEADS = 4
RET_KEY_DIM = 128
RET_VAL_DIM = 256
RET_CHUNK = 128
RET_ROT_BASE = 10000.0
D_FF = 4096
NORM_EPS = 1e-6
GN_EPS = 1e-6
NEG_INF = -1e30
ATT_SCALE = ATT_HEAD_DIM ** -0.5
RET_SCALE = RET_KEY_DIM ** -0.5

C_QA, C_KA, C_VA, C_QR, C_KR, C_VR, C_GR, C_GATES, C_END = 0, 512, 640, 768, 1280, 1792, 2816, 3840, 5888

ADAM_LR = 0.001
ADAM_B1 = 0.9
ADAM_B2 = 0.999
ADAM_EPS = 1e-08
ADAM_WD = 0.01
ADAM_STEP = 10

N_DEV = 8
LANE = 128
VMEM_LIMIT = 56 * 1024 * 1024
TOKEN_TILE = 512
MLP_TOKEN_TILE = 256
TN_TOKEN_TILE = 4096
FF_CHUNKS = (0, 1024, 2048, 3072, 4096)
IN_CHUNKS_FWD = (C_QA, C_KA, C_QR, C_KR, C_VR, C_GR, C_GATES, C_END)
IN_CHUNKS_BWD = (C_QA, C_KA, C_QR, C_GATES, C_END)

RET_LOG_GAMMA = tuple(math.log1p(-(2.0 ** (-5.0 - h))) for h in range(RET_HEADS))
RET_CHUNK_DECAY = tuple(math.exp(RET_CHUNK * lg) for lg in RET_LOG_GAMMA)

VMEM_FULL = pl.BlockSpec(memory_space=pltpu.VMEM)
SMEM_FULL = pl.BlockSpec(memory_space=pltpu.SMEM)
ANY = pl.BlockSpec(memory_space=pl.ANY)


def _params(n_axes):
    return pltpu.CompilerParams(dimension_semantics=("arbitrary",) * n_axes, vmem_limit_bytes=VMEM_LIMIT)


def _nn(a, b):
    return jnp.dot(a, b, preferred_element_type=F32)


def _nt(a, b):
    return lax.dot_general(a, b, (((1,), (1,)), ((), ())), preferred_element_type=F32)


def _tn(a, b):
    return lax.dot_general(a, b, (((0,), (0,)), ((), ())), preferred_element_type=F32)


def _sigmoid(v):
    return 1.0 / (1.0 + jnp.exp(-v))


def _rows(tile, width):
    return pl.BlockSpec((tile, width), lambda i: (i, 0))


def _cols(height, tile):
    return pl.BlockSpec((height, tile), lambda i: (0, i))


class _Resident:
    def __init__(self, hbm_ref, vmem_ref, sems, bounds):
        self.hbm, self.vmem, self.sems, self.bounds = hbm_ref, vmem_ref, sems, bounds

    @staticmethod
    def scratch(w, bounds):
        return [pltpu.VMEM(w.shape, w.dtype), pltpu.SemaphoreType.DMA((len(bounds) - 1,))]

    @staticmethod
    def load(*weights):
        @pl.when(pl.program_id(0) == 0)
        def _():
            copies = [w._copy(c) for w in weights for c in range(len(w.bounds) - 1)]
            for cp in copies:
                cp.start()
            for cp in copies:
                cp.wait()

    def _rows(self, c):
        return pl.ds(self.bounds[c], self.bounds[c + 1] - self.bounds[c])

    def _copy(self, c):
        return pltpu.make_async_copy(self.hbm.at[self._rows(c)], self.vmem.at[self._rows(c)], self.sems.at[c])

    def chunk(self, c):
        return self.vmem.at[self._rows(c)]


class _Comm:
    def __init__(self, inputs, out_shapes, scratch, start, finish, relay=None):
        self.inputs, self.out_shapes, self.scratch = inputs, out_shapes, scratch
        self.start, self.finish, self.relay = start, finish, relay


RELAY_AT = 0.7


def _join(a, b):
    na_in, na_out, na_sem = len(a.inputs), len(a.out_shapes), len(a.scratch)

    def both(name):
        def run(ins, outs, sems):
            for part, args in ((a, (ins[:na_in], outs[:na_out], sems[:na_sem])),
                               (b, (ins[na_in:], outs[na_out:], sems[na_sem:]))):
                if getattr(part, name) is not None:
                    getattr(part, name)(*args)
        return run

    return _Comm(list(a.inputs) + list(b.inputs), list(a.out_shapes) + list(b.out_shapes),
                 list(a.scratch) + list(b.scratch), both("start"), both("finish"),
                 both("relay") if a.relay or b.relay else None)


def _call(body, args, comm=None, *, name, grid, in_specs, out_specs, out_shape, scratch_shapes=()):
    params = _params(len(grid))
    if comm is None:
        return pl.pallas_call(body, name=name, grid=grid, in_specs=in_specs, out_specs=out_specs, out_shape=out_shape,
                              scratch_shapes=scratch_shapes, compiler_params=params)(*args), ()
    single = not isinstance(out_specs, (list, tuple))
    out_specs_l = [out_specs] if single else list(out_specs)
    out_shape_l = [out_shape] if single else list(out_shape)
    n_in, n_out, n_scr = len(in_specs), len(out_specs_l), len(scratch_shapes)
    n_cin, n_cout = len(comm.inputs), len(comm.out_shapes)

    def hosted(*refs):
        it = iter(refs)
        ins, cin, outs, cout, scr = ([next(it) for _ in range(k)] for k in (n_in, n_cin, n_out, n_cout, n_scr))
        sems = list(it)
        ids = [pl.program_id(k) for k in range(len(grid))]
        first = functools.reduce(jnp.logical_and, [i == 0 for i in ids])
        last = functools.reduce(jnp.logical_and, [i == g - 1 for i, g in zip(ids, grid)])

        @pl.when(first)
        def _():
            comm.start(cin, cout, sems)

        if comm.relay is not None:
            at = [int(grid[0] * RELAY_AT)] + [0] * (len(grid) - 1)

            @pl.when(functools.reduce(jnp.logical_and, [i == v for i, v in zip(ids, at)]))
            def _():
                comm.relay(cin, cout, sems)

        body(*ins, *outs, *scr)

        @pl.when(last)
        def _():
            comm.finish(cin, cout, sems)

    res = pl.pallas_call(
        hosted, name=name, grid=grid, in_specs=list(in_specs) + [ANY] * n_cin,
        out_specs=out_specs_l + [ANY] * n_cout, out_shape=out_shape_l + list(comm.out_shapes),
        scratch_shapes=list(scratch_shapes) + list(comm.scratch), compiler_params=params)(*args, *comm.inputs)
    return (res[0] if single else res[:n_out]), res[n_out:]


def _comm_alone(comm, name):
    n_cin, n_cout = len(comm.inputs), len(comm.out_shapes)

    def body(*refs):
        cin, cout, sems = refs[:n_cin], refs[n_cin:n_cin + n_cout], refs[n_cin + n_cout:]
        comm.start(cin, cout, sems)
        if comm.relay is not None:
            comm.relay(cin, cout, sems)
        comm.finish(cin, cout, sems)

    return pl.pallas_call(body, name=name, in_specs=[ANY] * n_cin, out_specs=[ANY] * n_cout,
                          out_shape=list(comm.out_shapes), scratch_shapes=list(comm.scratch))(*comm.inputs)


def _slabs(v, fn):
    return jnp.concatenate([fn(v[:, LANE * j:LANE * (j + 1)]) for j in range(v.shape[1] // LANE)], axis=1)


def _rope_att(v, ca, sa, sb):
    return _slabs(v, lambda t: t * ca + pltpu.roll(t, LANE - 8, 1) * sa + pltpu.roll(t, 8, 1) * sb)


def _rope_att_t(v, ca, sa, sb):
    return _slabs(v, lambda t: t * ca + pltpu.roll(t * sa, 8, 1) + pltpu.roll(t * sb, LANE - 8, 1))


def _rope_att_rows(v, ct, st, sign):
    parts = []
    for h in range(v.shape[0] // ATT_HEAD_DIM):
        r0 = ATT_HEAD_DIM * h
        x1, x2 = v[r0:r0 + 8], v[r0 + 8:r0 + 16]
        parts += [x1 * ct - sign * (x2 * st), x2 * ct + sign * (x1 * st), v[r0 + 16:r0 + ATT_HEAD_DIM]]
    return jnp.concatenate(parts, axis=0)


def _rope_ret(v, cr, sr):
    return _slabs(v, lambda t: t * cr + pltpu.roll(t, 64, 1) * sr)


def _rope_ret_t(v, cr, sr):
    return _slabs(v, lambda t: t * cr + pltpu.roll(t * sr, 64, 1))


def _rope_lane_tables():
    def inv(dim, theta):
        return theta ** (-jnp.arange(0, dim, 2, dtype=F32) / dim)

    inv_a, inv_r = inv(ROPE_DIM, ROPE_THETA), inv(RET_KEY_DIM, RET_ROT_BASE)
    half = ROPE_DIM // 2
    zeros = jnp.zeros((ATT_HEAD_DIM - ROPE_DIM,), F32)
    freq64 = jnp.concatenate([inv_a, inv_a, zeros])
    lo64 = jnp.concatenate([-jnp.ones((half,), F32), jnp.zeros((half,), F32), zeros])
    hi64 = jnp.concatenate([jnp.zeros((half,), F32), jnp.ones((half,), F32), zeros])
    sign_r = jnp.concatenate([-jnp.ones((64,), F32), jnp.ones((64,), F32)])
    rows = [jnp.tile(freq64, 2), jnp.tile(lo64, 2), jnp.tile(hi64, 2), jnp.tile(inv_r, 2), sign_r]
    lanes = jnp.stack(rows + [jnp.zeros((LANE,), F32)] * (8 - len(rows)))
    return lanes, jnp.broadcast_to(inv_a[:, None], (half, LANE))


def _prologue(x, g1, comm):
    seq = x.shape[0]
    tm = min(TOKEN_TILE, seq)

    def body(x_ref, g_ref, lanes_ref, freq_rows_ref, xn_ref, ca_ref, sa_ref, sb_ref, cr_ref, sr_ref, ct_ref, st_ref):
        xf = x_ref[...]
        r = lax.rsqrt(jnp.mean(xf * xf, axis=-1, keepdims=True) + NORM_EPS)
        xn_ref[...] = (xf * r * g_ref[...]).astype(BF)
        row0 = pl.program_id(0) * tm
        pos = (row0 + lax.broadcasted_iota(jnp.int32, (tm, LANE), 0)).astype(F32)
        ang_a = pos * lanes_ref[0:1, :]
        sin_a = jnp.sin(ang_a)
        ca_ref[...] = jnp.cos(ang_a)
        sa_ref[...] = sin_a * lanes_ref[1:2, :]
        sb_ref[...] = sin_a * lanes_ref[2:3, :]
        ang_r = pos * lanes_ref[3:4, :]
        cr_ref[...] = jnp.cos(ang_r)
        sr_ref[...] = jnp.sin(ang_r) * lanes_ref[4:5, :]
        pos_t = (row0 + lax.broadcasted_iota(jnp.int32, (8, tm), 1)).astype(F32)
        ang_t = pos_t * jnp.concatenate([freq_rows_ref[...]] * (tm // LANE), axis=1)
        ct_ref[...] = jnp.cos(ang_t)
        st_ref[...] = jnp.sin(ang_t)

    tab = _rows(tm, LANE)
    tab_t = _cols(8, tm)
    return _call(
        body, (x, g1, *_rope_lane_tables()), comm, name="prologue", grid=(seq // tm,),
        in_specs=[_rows(tm, D_MODEL), VMEM_FULL, VMEM_FULL, VMEM_FULL],
        out_specs=[_rows(tm, D_MODEL), tab, tab, tab, tab, tab, tab_t, tab_t],
        out_shape=[jax.ShapeDtypeStruct((seq, D_MODEL), BF)] + [jax.ShapeDtypeStruct((seq, LANE), F32)] * 5
        + [jax.ShapeDtypeStruct((8, seq), F32)] * 2)


def _ret_tables():
    c = RET_CHUNK
    lg = jnp.asarray(RET_LOG_GAMMA, F32)
    idx = jnp.arange(c, dtype=F32)
    diff = idx[:, None] - idx[None, :]
    intra = jnp.where(diff >= 0, jnp.exp(jnp.maximum(diff, 0.0) * lg[:, None, None]), 0.0)
    qd = jnp.exp((idx + 1.0)[None, :] * lg[:, None])[..., None]
    kd = jnp.exp((c - 1.0 - idx)[None, :] * lg[:, None])[..., None]
    return intra, jnp.broadcast_to(qd, (RET_HEADS, c, RET_KEY_DIM)), jnp.broadcast_to(kd, (RET_HEADS, c, RET_KEY_DIM))


def _inproj_fwd(xn1, w_in, tabs, comm=None):
    seq = xn1.shape[0]
    tm = min(TOKEN_TILE, seq)

    def body(xn_ref, w_hbm, ca_ref, sa_ref, sb_ref, cr_ref, sr_ref, ct_ref, st_ref,
             qt_ref, kv_ref, kvt_ref, qkr_ref, vr_ref, gr_ref, gates_ref, w_vmem, w_sems):
        w = _Resident(w_hbm, w_vmem, w_sems, IN_CHUNKS_FWD)
        _Resident.load(w)
        ca, sa, sb, cr, sr, ct, st = (ref[...] for ref in (ca_ref, sa_ref, sb_ref, cr_ref, sr_ref, ct_ref, st_ref))
        xn = xn_ref[...]
        qt_ref[...] = _rope_att_rows(_nt(w.chunk(0)[...], xn), ct, st, 1.0).astype(BF)
        w_kv = w.chunk(1)
        kvt = _nt(w_kv[...], xn)
        kvt_ref[:LANE, :] = _rope_att_rows(kvt[:LANE], ct, st, 1.0).astype(BF)
        kvt_ref[LANE:, :] = kvt[LANE:].astype(BF)
        kvn = _nt(xn, w_kv[...])
        kv_ref[:, :LANE] = _rope_att(kvn[:, :LANE], ca, sa, sb).astype(BF)
        kv_ref[:, LANE:] = kvn[:, LANE:].astype(BF)
        qr = _rope_ret(_nt(xn, w.chunk(2)[...]), cr, sr)
        qkr_ref[:, :512] = qr.astype(BF)
        kr = _rope_ret(_nt(xn, w.chunk(3)[...]), cr, sr) * RET_SCALE
        qkr_ref[:, 512:] = kr.astype(BF)
        vr_ref[...] = _nt(xn, w.chunk(4)[...]).astype(BF)
        gr_ref[...] = _nt(xn, w.chunk(5)[...]).astype(BF)
        gates_ref[...] = _nt(xn, w.chunk(6)[...]).astype(BF)

    tab = _rows(tm, LANE)
    tab_t = _cols(8, tm)
    return _call(
        body, (xn1, w_in, *tabs), comm, name="inproj_fwd", grid=(seq // tm,),
        in_specs=[_rows(tm, D_MODEL), ANY, tab, tab, tab, tab, tab, tab_t, tab_t],
        out_specs=[_cols(512, tm), _rows(tm, 256), _cols(256, tm), _rows(tm, 1024),
                   _rows(tm, 1024), _rows(tm, 1024), _rows(tm, 2048)],
        out_shape=[jax.ShapeDtypeStruct(s, BF) for s in ((512, seq), (seq, 256), (256, seq),
                                                         (seq, 1024), (seq, 1024), (seq, 1024), (seq, 2048))],
        scratch_shapes=_Resident.scratch(w_in, IN_CHUNKS_FWD))


ATT_Q_PER_KV = ATT_HEADS // 2


def _att_group(ref, kv):
    d = ATT_HEAD_DIM
    g = jnp.concatenate([ref[d * (ATT_Q_PER_KV * kv + j):d * (ATT_Q_PER_KV * kv + j + 1), :]
                         for j in range(ATT_Q_PER_KV)], axis=1)
    z = jnp.zeros_like(g)
    return jnp.concatenate([g, z] if kv == 0 else [z, g], axis=0)


def _att_probs(kwin, qpad, sink_ref, kv, i):
    c = ATT_BLOCK
    kj = lax.broadcasted_iota(jnp.int32, (2 * c, c), 0)
    qi = lax.broadcasted_iota(jnp.int32, (2 * c, c), 1)
    allowed = (kj > qi) & (kj <= qi + c) & ((kj >= c) | (i > 0))
    allowed = jnp.concatenate([allowed] * ATT_Q_PER_KV, axis=1)
    s = jnp.where(allowed, _nn(kwin, qpad) * ATT_SCALE, NEG_INF)
    sk = jnp.concatenate([jnp.full((1, c), sink_ref[ATT_Q_PER_KV * kv + j], F32) for j in range(ATT_Q_PER_KV)], axis=1)
    m = jnp.maximum(jnp.max(s, axis=0, keepdims=True), sk)
    pe = jnp.exp(s - m)
    psink = jnp.exp(sk - m)
    inv = 1.0 / (jnp.sum(pe, axis=0, keepdims=True) + psink)
    return pe * inv, psink * inv


def _seq_fwd(qt, kv, kvt, sinks, qkr, vr, gr, gain, rtabs, comm=None):
    seq = kv.shape[0]
    c = ATT_BLOCK
    assert c == RET_CHUNK
    d = ATT_HEAD_DIM
    nb = seq // c
    dk, dv = RET_KEY_DIM, RET_VAL_DIM

    def body(sink_ref, qt_ref, kvc_ref, kvp_ref, kvtc_ref, kvtp_ref, qk_ref, v_ref, g_ref, gain_ref, intra_ref,
             qd_ref, kd_ref, ot_ref, ret_ref, yrp_ref, st_ref, state):
        i = pl.program_id(0)
        kwin = jnp.concatenate([kvp_ref[:, :LANE], kvc_ref[:, :LANE]], axis=0)
        vtwin = jnp.concatenate([kvtp_ref[LANE:, :], kvtc_ref[LANE:, :]], axis=1)
        for g in (0, 1):
            prob, _ = _att_probs(kwin, _att_group(qt_ref, g), sink_ref, g, i)
            out = _nn(vtwin[d * g:d * (g + 1), :], prob.astype(BF))
            for j in range(ATT_Q_PER_KV):
                h = ATT_Q_PER_KV * g + j
                ot_ref[d * h:d * (h + 1), :] = out[:, c * j:c * (j + 1)].astype(BF)

        @pl.when(i == 0)
        def _():
            state[...] = jnp.zeros_like(state)

        for h in range(RET_HEADS):
            qh = qk_ref[:, dk * h:dk * (h + 1)]
            kh = qk_ref[:, 512 + dk * h:512 + dk * (h + 1)]
            vh = v_ref[:, dv * h:dv * (h + 1)]
            sh = state[h]
            shb = sh.astype(BF)
            st_ref[0, h] = shb
            att = _nt(qh, kh) * intra_ref[h]
            inner = _nn(att.astype(BF), vh)
            cross = _nn((qh.astype(F32) * qd_ref[h]).astype(BF), shb)
            out = inner + cross
            state[h] = sh * RET_CHUNK_DECAY[h] + _tn((kh.astype(F32) * kd_ref[h]).astype(BF), vh)
            ret_ref[:, dv * h:dv * (h + 1)] = out
            mu = jnp.mean(out, axis=-1, keepdims=True)
            dev = out - mu
            var = jnp.mean(dev * dev, axis=-1, keepdims=True)
            y = dev * lax.rsqrt(var + GN_EPS) * gain_ref[:, dv * h:dv * (h + 1)]
            g = g_ref[:, dv * h:dv * (h + 1)].astype(F32)
            yrp_ref[:, dv * h:dv * (h + 1)] = (g * _sigmoid(g) * y).astype(BF)

    prev = lambda i: jnp.maximum(i - 1, 0)
    return _call(
        body, (sinks, qt, kv, kv, kvt, kvt, qkr, vr, gr, gain, *rtabs), comm, name="seq_fwd", grid=(nb,),
        in_specs=[SMEM_FULL, _cols(512, c), _rows(c, 256), pl.BlockSpec((c, 256), lambda i: (prev(i), 0)),
                  _cols(256, c), pl.BlockSpec((256, c), lambda i: (0, prev(i))),
                  _rows(c, 1024), _rows(c, 1024), _rows(c, 1024), VMEM_FULL, VMEM_FULL, VMEM_FULL, VMEM_FULL],
        out_specs=[_cols(512, c), _rows(c, 1024), _rows(c, 1024),
                   pl.BlockSpec((1, RET_HEADS, dk, dv), lambda i: (i, 0, 0, 0))],
        out_shape=[jax.ShapeDtypeStruct((512, seq), BF), jax.ShapeDtypeStruct((seq, 1024), F32),
                   jax.ShapeDtypeStruct((seq, 1024), BF), jax.ShapeDtypeStruct((nb, RET_HEADS, dk, dv), BF)],
        scratch_shapes=[pltpu.VMEM((RET_HEADS, dk, dv), F32)])


def _mix_fwd(oat, yrp, gates, bg, x, w_att, w_ret, w_out):
    seq = x.shape[0]
    tm = min(TOKEN_TILE, seq)

    def body(oat_ref, yrp_ref, gates_ref, bg_ref, x_ref, wa_ref, wr_ref, wo_ref, ya_ref, yr_ref, mg_ref, h1_ref):
        ya = _tn(oat_ref[...], wa_ref[...])
        yr = _nn(yrp_ref[...], wr_ref[...])
        gt = _sigmoid(gates_ref[...].astype(F32) + bg_ref[...])
        merged = (gt[:, :D_MODEL] * ya + gt[:, D_MODEL:] * yr).astype(BF)
        ya_ref[...] = ya.astype(BF)
        yr_ref[...] = yr.astype(BF)
        mg_ref[...] = merged
        h1_ref[...] = x_ref[...] + _nn(merged, wo_ref[...])

    return pl.pallas_call(
        body, name="mix_fwd", grid=(seq // tm,),
        in_specs=[_cols(512, tm), _rows(tm, 1024), _rows(tm, 2048), VMEM_FULL, _rows(tm, D_MODEL),
                  VMEM_FULL, VMEM_FULL, VMEM_FULL],
        out_specs=[_rows(tm, D_MODEL)] * 4,
        out_shape=[jax.ShapeDtypeStruct((seq, D_MODEL), BF)] * 3 + [jax.ShapeDtypeStruct((seq, D_MODEL), F32)],
        compiler_params=_params(1),
    )(oat, yrp, gates, bg, x, w_att, w_ret, w_out)


def _mlp_loss_step(h1, g2, g3, target, w_ff1, w_ff2):
    seq = h1.shape[0]
    tm = min(MLP_TOKEN_TILE, seq)
    n_chunks = len(FF_CHUNKS) - 1

    def body(h1_ref, g2_ref, g3_ref, t_ref, w1_hbm, w2_hbm,
             xn2_ref, hdn_ref, dh2_ref, du_ref, dh1_ref, loss_ref, dg3_ref, dg2_ref,
             w1_vmem, w1_sems, w2_vmem, w2_sems, relu_u):
        w1 = _Resident(w1_hbm, w1_vmem, w1_sems, FF_CHUNKS)
        w2 = _Resident(w2_hbm, w2_vmem, w2_sems, FF_CHUNKS)
        _Resident.load(w1, w2)

        @pl.when(pl.program_id(0) == 0)
        def _():
            loss_ref[...] = jnp.zeros_like(loss_ref)
            dg3_ref[...] = jnp.zeros_like(dg3_ref)
            dg2_ref[...] = jnp.zeros_like(dg2_ref)

        h1v = h1_ref[...]
        r2 = lax.rsqrt(jnp.mean(h1v * h1v, axis=-1, keepdims=True) + NORM_EPS)
        xn2 = (h1v * r2 * g2_ref[...]).astype(BF)
        xn2_ref[...] = xn2
        h2 = h1v
        for c in range(n_chunks):
            cols = slice(FF_CHUNKS[c], FF_CHUNKS[c + 1])
            a = jnp.maximum(_nt(xn2, w1.chunk(c)[...]), 0.0)
            relu_u[:, cols] = a.astype(BF)
            hdn = jnp.square(a).astype(BF)
            hdn_ref[:, cols] = hdn
            h2 = h2 + _nn(hdn, w2.chunk(c)[...])
        r3 = lax.rsqrt(jnp.mean(h2 * h2, axis=-1, keepdims=True) + NORM_EPS)
        hn = h2 * r3
        err = hn * g3_ref[...] - t_ref[...]
        loss_ref[...] += jnp.sum(err * err) * (0.5 / D_MODEL)
        dy = err * (1.0 / D_MODEL)
        dg3_ref[...] += jnp.sum(dy * hn, axis=0, keepdims=True)
        z = dy * g3_ref[...]
        dh2 = r3 * (z - hn * jnp.mean(z * hn, axis=-1, keepdims=True))
        dh2b = dh2.astype(BF)
        dh2_ref[...] = dh2b
        dxn2 = jnp.zeros_like(dh2)
        for c in range(n_chunks):
            cols = slice(FF_CHUNKS[c], FF_CHUNKS[c + 1])
            du = (_nt(dh2b, w2.chunk(c)[...]) * (2.0 * relu_u[:, cols].astype(F32))).astype(BF)
            du_ref[:, cols] = du
            dxn2 = dxn2 + _nn(du, w1.chunk(c)[...])
        dnorm, dg = _rms_bwd(dxn2, h1v, g2_ref[...])
        dh1_ref[...] = dh2 + dnorm
        dg2_ref[...] += dg

    gain_acc = pl.BlockSpec((1, D_MODEL), lambda i: (0, 0))
    return pl.pallas_call(
        body, name="mlp_loss_step", grid=(seq // tm,),
        in_specs=[_rows(tm, D_MODEL), VMEM_FULL, VMEM_FULL, _rows(tm, D_MODEL), ANY, ANY],
        out_specs=[_rows(tm, D_MODEL), _rows(tm, D_FF), _rows(tm, D_MODEL), _rows(tm, D_FF), _rows(tm, D_MODEL),
                   pl.BlockSpec((1, LANE), lambda i: (0, 0)), gain_acc, gain_acc],
        out_shape=[jax.ShapeDtypeStruct((seq, D_MODEL), BF), jax.ShapeDtypeStruct((seq, D_FF), BF),
                   jax.ShapeDtypeStruct((seq, D_MODEL), BF), jax.ShapeDtypeStruct((seq, D_FF), BF),
                   jax.ShapeDtypeStruct((seq, D_MODEL), F32), jax.ShapeDtypeStruct((1, LANE), F32),
                   jax.ShapeDtypeStruct((1, D_MODEL), F32), jax.ShapeDtypeStruct((1, D_MODEL), F32)],
        scratch_shapes=(_Resident.scratch(w_ff1, FF_CHUNKS) + _Resident.scratch(w_ff2, FF_CHUNKS)
                        + [pltpu.VMEM((tm, D_FF), BF)]),
        compiler_params=_params(1),
    )(h1, g2, g3, target, w_ff1, w_ff2)


def _rms_bwd(dxn, xin, gain):
    r = lax.rsqrt(jnp.mean(xin * xin, axis=-1, keepdims=True) + NORM_EPS)
    xhat = xin * r
    z = dxn * gain
    dxin = r * (z - xhat * jnp.mean(z * xhat, axis=-1, keepdims=True))
    return dxin, jnp.sum(dxn * xhat, axis=0, keepdims=True)


def _mm_tn(a, b, name, a_is_t=False, comm=None):
    kdim, seq = a.shape if a_is_t else a.shape[::-1]
    ndim = b.shape[1]
    ts = min(TN_TOKEN_TILE, seq)
    tk = min(kdim, 512)
    tn = min(ndim, 1024)
    n_steps = seq // ts

    def body(a_ref, b_ref, o_ref):
        mm = _nn if a_is_t else _tn
        part = mm(a_ref[...].astype(BF), b_ref[...].astype(BF))
        if n_steps == 1:
            o_ref[...] = part
        else:
            @pl.when(pl.program_id(2) == 0)
            def _():
                o_ref[...] = jnp.zeros_like(o_ref)

            o_ref[...] += part

    a_spec = (pl.BlockSpec((tk, ts), lambda k, n, s: (k, s)) if a_is_t
              else pl.BlockSpec((ts, tk), lambda k, n, s: (s, k)))
    res = _call(
        body, (a, b), comm, name=name, grid=(kdim // tk, ndim // tn, n_steps),
        in_specs=[a_spec, pl.BlockSpec((ts, tn), lambda k, n, s: (s, n))],
        out_specs=pl.BlockSpec((tk, tn), lambda k, n, s: (k, n)),
        out_shape=jax.ShapeDtypeStruct((kdim, ndim), F32))
    return res[0] if comm is None else res


def _dw_in_t(pieces, is_t, xn1, comm=None):
    seq = xn1.shape[0]
    tr = 256
    n_blk = [(p.shape[0] if t else p.shape[1]) // tr for p, t in zip(pieces, is_t)]
    offs = [sum(n_blk[:i]) for i in range(len(pieces) + 1)]
    n_p = len(pieces)

    def body(*refs):
        piece_refs, xn_ref, o_ref, ob_ref = refs[:n_p], refs[n_p], refs[n_p + 1], refs[n_p + 2]
        r = pl.program_id(0)
        for p in range(n_p):
            @pl.when((r >= offs[p]) & (r < offs[p + 1]))
            def _(p=p):
                res = (_nn if is_t[p] else _tn)(piece_refs[p][...], xn_ref[...])
                o_ref[...] = res
                ob_ref[...] = res.astype(BF)

    def piece_spec(p):
        def index(r):
            blk = jnp.clip(r - offs[p], 0, n_blk[p] - 1)
            return (blk, 0) if is_t[p] else (0, blk)
        return pl.BlockSpec((tr, seq) if is_t[p] else (seq, tr), index)

    return _call(
        body, (*pieces, xn1), comm, name="dw_in", grid=(offs[-1],),
        in_specs=[piece_spec(p) for p in range(n_p)] + [VMEM_FULL],
        out_specs=[_rows(tr, D_MODEL), _rows(tr, D_MODEL)],
        out_shape=[jax.ShapeDtypeStruct((offs[-1] * tr, D_MODEL), F32),
                   jax.ShapeDtypeStruct((offs[-1] * tr, D_MODEL), BF)])


def _mix_bwd(dh1, ya, yr, gates, bg, merged, oat, yrp, w_att, w_ret, w_out, comm=None):
    seq = dh1.shape[0]
    tm = min(TOKEN_TILE, seq)

    def body(dh1_ref, ya_ref, yr_ref, gates_ref, bg_ref, mg_ref, oat_ref, yrp_ref, wa_ref, wr_ref, wo_ref,
             dgates_ref, doa_ref, dyrp_ref, db_ref, dwa_ref, dwr_ref, dwo_ref):
        @pl.when(pl.program_id(0) == 0)
        def _():
            for ref in (db_ref, dwa_ref, dwr_ref, dwo_ref):
                ref[...] = jnp.zeros_like(ref)

        dh1b = dh1_ref[...].astype(BF)
        dm = _nt(dh1b, wo_ref[...])
        gt = _sigmoid(gates_ref[...].astype(F32) + bg_ref[...])
        ga, gr = gt[:, :D_MODEL], gt[:, D_MODEL:]
        dya = (dm * ga).astype(BF)
        dyr = (dm * gr).astype(BF)
        dga = dm * ya_ref[...].astype(F32) * ga * (1.0 - ga)
        dgr = dm * yr_ref[...].astype(F32) * gr * (1.0 - gr)
        dgates_ref[:, :D_MODEL] = dga.astype(BF)
        dgates_ref[:, D_MODEL:] = dgr.astype(BF)
        db_ref[:, :D_MODEL] += jnp.sum(dga, axis=0, keepdims=True)
        db_ref[:, D_MODEL:] += jnp.sum(dgr, axis=0, keepdims=True)
        doa_ref[...] = _nt(wa_ref[...], dya).astype(BF)
        dyrp_ref[...] = _nt(dyr, wr_ref[...]).astype(BF)
        dwo_ref[...] += _tn(mg_ref[...], dh1b)
        dwr_ref[...] += _tn(yrp_ref[...], dyr)
        dwa_ref[...] += _nn(oat_ref[...], dya)

    acc = lambda r, c: pl.BlockSpec((r, c), lambda i: (0, 0))
    return _call(
        body, (dh1, ya, yr, gates, bg, merged, oat, yrp, w_att, w_ret, w_out), comm, name="mix_bwd",
        grid=(seq // tm,),
        in_specs=[_rows(tm, D_MODEL), _rows(tm, D_MODEL), _rows(tm, D_MODEL), _rows(tm, 2048), VMEM_FULL,
                  _rows(tm, D_MODEL), _cols(512, tm), _rows(tm, 1024), VMEM_FULL, VMEM_FULL, VMEM_FULL],
        out_specs=[_rows(tm, 2048), _cols(512, tm), _rows(tm, 1024), acc(1, 2048), acc(512, D_MODEL),
                   acc(1024, D_MODEL), acc(D_MODEL, D_MODEL)],
        out_shape=[jax.ShapeDtypeStruct((seq, 2048), BF), jax.ShapeDtypeStruct((512, seq), BF),
                   jax.ShapeDtypeStruct((seq, 1024), BF), jax.ShapeDtypeStruct((1, 2048), F32),
                   jax.ShapeDtypeStruct((512, D_MODEL), F32), jax.ShapeDtypeStruct((1024, D_MODEL), F32),
                   jax.ShapeDtypeStruct((D_MODEL, D_MODEL), F32)])


def _seq_bwd(qt, kv, kvt, dot, sinks, qkr, vr, gr, ret, dyrp, states, gain, rtabs, tabs, comm=None):
    seq = kv.shape[0]
    c = ATT_BLOCK
    assert c == RET_CHUNK
    d = ATT_HEAD_DIM
    nb = seq // c
    dk, dv = RET_KEY_DIM, RET_VAL_DIM
    ca, sa, sb, cr, sr, ct, st = tabs

    def body(sink_ref, qt_ref, kvc_ref, kvp_ref, kvtc_ref, kvtp_ref, dot_ref, ca_ref, sa_ref, sb_ref, ct_ref, st_ref,
             qk_ref, v_ref, g_ref, ret_ref, dyp_ref, st8_ref, gain_ref, intra_ref, qd_ref, kd_ref, cr_ref, sr_ref,
             dqt_ref, dkv_ref, dsink_ref, dall_ref, dgain_ref, carry, dstate):
        step = pl.program_id(0)
        i = nb - 1 - step

        @pl.when(step == 0)
        def _():
            carry[...] = jnp.zeros_like(carry)
            dsink_ref[...] = jnp.zeros_like(dsink_ref)
            dstate[...] = jnp.zeros_like(dstate)
            dgain_ref[...] = jnp.zeros_like(dgain_ref)

        kwin = jnp.concatenate([kvp_ref[:, :LANE], kvc_ref[:, :LANE]], axis=0)
        vwin = jnp.concatenate([kvp_ref[:, LANE:], kvc_ref[:, LANE:]], axis=0)
        ktwin = jnp.concatenate([kvtp_ref[:LANE, :], kvtc_ref[:LANE, :]], axis=1)
        ctv, stv = ct_ref[...], st_ref[...]
        lane1 = lax.broadcasted_iota(jnp.int32, (1, LANE), 1)
        dkw = jnp.zeros((2 * c, LANE), F32)
        dvw = jnp.zeros((2 * c, LANE), F32)
        dsink = jnp.zeros((1, LANE), F32)
        for g in (0, 1):
            qpad, dopad = _att_group(qt_ref, g), _att_group(dot_ref, g)
            prob, psink = _att_probs(kwin, qpad, sink_ref, g, i)
            dprob = _nn(vwin, dopad)
            drow = jnp.sum(dprob * prob, axis=0, keepdims=True)
            ds = (prob * (dprob - drow) * ATT_SCALE).astype(BF)
            dqg = _nn(ktwin[d * g:d * (g + 1), :], ds)
            dkw = dkw + _nt(ds, qpad)
            dvw = dvw + _nt(prob.astype(BF), dopad)
            dsink_lanes = psink * drow
            for j in range(ATT_Q_PER_KV):
                h = ATT_Q_PER_KV * g + j
                dqt_ref[d * h:d * (h + 1), :] = _rope_att_rows(dqg[:, c * j:c * (j + 1)], ctv, stv, -1.0).astype(BF)
                dsink = dsink + jnp.where(lane1 == h, -jnp.sum(dsink_lanes[:, c * j:c * (j + 1)]), 0.0)
        dsink_ref[...] += dsink
        total = carry[...] + jnp.concatenate([dkw[c:], dvw[c:]], axis=1)
        dkv_ref[:, :LANE] = _rope_att_t(total[:, :LANE], ca_ref[...], sa_ref[...], sb_ref[...]).astype(BF)
        dkv_ref[:, LANE:] = total[:, LANE:].astype(BF)
        carry[...] = jnp.concatenate([dkw[:c], dvw[:c]], axis=1)

        crv, srv = cr_ref[...], sr_ref[...]
        for h in range(RET_HEADS):
            vs = slice(dv * h, dv * (h + 1))
            qh = qk_ref[:, dk * h:dk * (h + 1)]
            kh = qk_ref[:, 512 + dk * h:512 + dk * (h + 1)]
            vh = v_ref[:, vs]
            out = ret_ref[:, vs]
            g = g_ref[:, vs].astype(F32)
            dyp = dyp_ref[:, vs].astype(F32)
            gain_h = gain_ref[:, vs]
            mu = jnp.mean(out, axis=-1, keepdims=True)
            dev = out - mu
            rstd = lax.rsqrt(jnp.mean(dev * dev, axis=-1, keepdims=True) + GN_EPS)
            yn = dev * rstd
            sg = _sigmoid(g)
            dg = dyp * (yn * gain_h) * (sg * (1.0 + g * (1.0 - sg)))
            dy = dyp * (g * sg)
            dgain_ref[:, vs] += jnp.sum(dy * yn, axis=0, keepdims=True)
            dyn = dy * gain_h
            dout = rstd * (dyn - jnp.mean(dyn, axis=-1, keepdims=True)
                           - yn * jnp.mean(dyn * yn, axis=-1, keepdims=True))
            doutb = dout.astype(BF)
            sc = st8_ref[0, h]
            dsp = dstate[h]
            dspb = dsp.astype(BF)
            intra, qdv, kdv = intra_ref[h], qd_ref[h], kd_ref[h]
            att = _nt(qh, kh) * intra
            dab = (_nt(doutb, vh) * intra).astype(BF)
            qdec = (qh.astype(F32) * qdv).astype(BF)
            kdec = (kh.astype(F32) * kdv).astype(BF)
            dq = _nn(dab, kh) + _nt(doutb, sc) * qdv
            dkk = _tn(dab, qh) + _nt(vh, dspb) * kdv
            dvv = _tn(att.astype(BF), doutb) + _nn(kdec, dspb)
            dstate[h] = dsp * RET_CHUNK_DECAY[h] + _tn(qdec, doutb)
            dall_ref[:, dk * h:dk * (h + 1)] = _rope_ret_t(dq, crv, srv).astype(BF)
            dall_ref[:, 512 + dk * h:512 + dk * (h + 1)] = (_rope_ret_t(dkk, crv, srv) * RET_SCALE).astype(BF)
            dall_ref[:, 1024 + dv * h:1024 + dv * (h + 1)] = dvv.astype(BF)
            dall_ref[:, 2048 + dv * h:2048 + dv * (h + 1)] = dg.astype(BF)

    cur = lambda s: nb - 1 - s
    prev = lambda s: jnp.maximum(nb - 2 - s, 0)
    rows = lambda w: pl.BlockSpec((c, w), lambda s: (cur(s), 0))
    cols = lambda h: pl.BlockSpec((h, c), lambda s: (0, cur(s)))
    acc = lambda w: pl.BlockSpec((1, w), lambda s: (0, 0))
    return _call(
        body, (sinks, qt, kv, kv, kvt, kvt, dot, ca, sa, sb, ct, st, qkr, vr, gr, ret, dyrp, states, gain, *rtabs,
               cr, sr), comm, name="seq_bwd", grid=(nb,),
        in_specs=[SMEM_FULL, cols(512), rows(256), pl.BlockSpec((c, 256), lambda s: (prev(s), 0)), cols(256),
                  pl.BlockSpec((256, c), lambda s: (0, prev(s))), cols(512), rows(LANE), rows(LANE), rows(LANE),
                  cols(8), cols(8), rows(1024), rows(1024), rows(1024), rows(1024), rows(1024),
                  pl.BlockSpec((1, RET_HEADS, dk, dv), lambda s: (cur(s), 0, 0, 0)),
                  VMEM_FULL, VMEM_FULL, VMEM_FULL, VMEM_FULL, rows(LANE), rows(LANE)],
        out_specs=[cols(512), rows(256), acc(LANE), rows(3072), acc(1024)],
        out_shape=[jax.ShapeDtypeStruct((512, seq), BF), jax.ShapeDtypeStruct((seq, 256), BF),
                   jax.ShapeDtypeStruct((1, LANE), F32), jax.ShapeDtypeStruct((seq, 3072), BF),
                   jax.ShapeDtypeStruct((1, 1024), F32)],
        scratch_shapes=[pltpu.VMEM((c, 256), F32), pltpu.VMEM((RET_HEADS, dk, dv), F32)])


def _inproj_bwd(dqa, dkva, dret, dgates, dh1, x, g1, w_in, comm=None):
    seq = x.shape[0]
    tm = min(TOKEN_TILE, seq)

    def body(dqa_ref, dkva_ref, dret_ref, dgates_ref, dh1_ref, x_ref, g_ref, w_hbm, dx_ref, dg1_ref, w_vmem, w_sems):
        w = _Resident(w_hbm, w_vmem, w_sems, IN_CHUNKS_BWD)
        _Resident.load(w)

        @pl.when(pl.program_id(0) == 0)
        def _():
            dg1_ref[...] = jnp.zeros_like(dg1_ref)

        dxn = _tn(dqa_ref[...], w.chunk(0)[...])
        dxn = dxn + _nn(dkva_ref[...], w.chunk(1)[...])
        dxn = dxn + _nn(dret_ref[...], w.chunk(2)[...])
        dxn = dxn + _nn(dgates_ref[...], w.chunk(3)[...])
        dnorm, dg = _rms_bwd(dxn, x_ref[...], g_ref[...])
        dx_ref[...] = dh1_ref[...] + dnorm
        dg1_ref[...] += dg

    return _call(
        body, (dqa, dkva, dret, dgates, dh1, x, g1, w_in), comm, name="inproj_bwd", grid=(seq // tm,),
        in_specs=[_cols(512, tm), _rows(tm, 256), _rows(tm, 3072), _rows(tm, 2048), _rows(tm, D_MODEL),
                  _rows(tm, D_MODEL), VMEM_FULL, ANY],
        out_specs=[_rows(tm, D_MODEL), pl.BlockSpec((1, D_MODEL), lambda i: (0, 0))],
        out_shape=[jax.ShapeDtypeStruct((seq, D_MODEL), F32), jax.ShapeDtypeStruct((1, D_MODEL), F32)],
        scratch_shapes=_Resident.scratch(w_in, IN_CHUNKS_BWD))


def _local_step(x, target, g1, bg, sinks, gain, g2, g3, shards):
    rtabs = _ret_tables()
    scatter = _Scatter

    comm, unpack = _gather(shards, ("w_in",))
    (xn1, ca, sa, sb, cr, sr, ct, st), got = _prologue(x, g1, comm)
    (w_in,) = unpack(got)
    comm_mid, unpack_mid = _gather(shards, ("w_att_up", "w_ret_up", "w_out"))
    comm_ff1, unpack_ff1 = _gather(shards, ("w_ff1",))
    (qt, kv, kvt, qkr, vr, gr, gates), got = _inproj_fwd(xn1, w_in, (ca, sa, sb, cr, sr, ct, st),
                                                         _join(comm_mid, comm_ff1))
    w_att, w_ret, w_out = unpack_mid(got[:3])
    (w_ff1,) = unpack_ff1(got[3:])
    comm, unpack = _gather(shards, ("w_ff2",))
    (oa, ret, yrp, states), got = _seq_fwd(qt, kv, kvt, sinks, qkr, vr, gr, gain, rtabs, comm)
    (w_ff2,) = unpack(got)
    ya, yr, merged, h1 = _mix_fwd(oa, yrp, gates, bg, x, w_att, w_ret, w_out)
    xn2, hdn, dh2, du, dh1, loss, dg3, dg2 = _mlp_loss_step(h1, g2, g3, target, w_ff1, w_ff2)

    ff2 = scatter(dict(w_ff2=_mm_tn(hdn, dh2, "dw_ff2")), "ff2")
    ff1 = scatter(dict(w_ff1=_mm_tn(du, xn2, "dw_ff1")), "ff1")
    (dgates, doa, dyrp, db, d_att, d_ret, d_out), got = _mix_bwd(
        dh1, ya, yr, gates, bg, merged, oa, yrp, w_att, w_ret, w_out, _join(ff2.pair_comm(), ff1.pair_comm()))
    _Scatter.pairs_done((ff2, ff1), (got[:1], got[1:]), "ff")
    mid = scatter(dict(w_att_up=d_att, w_ret_up=d_ret, w_out=d_out), "mid")
    (dqa, dkva, dsink, dret, dgain), got = _seq_bwd(
        qt, kv, kvt, doa, sinks, qkr, vr, gr, ret, dyrp, states, gain, rtabs, (ca, sa, sb, cr, sr, ct, st),
        _join(_join(ff2.chip_comm(), ff1.chip_comm()), mid.pair_comm()))
    ff2.chip_done(got[:1])
    ff1.chip_done(got[1:2])
    mid.pair_done(got[2:])
    (d_in, d_in_bf), r2 = _dw_in_t((dqa, dkva, dret, dgates), (True, False, False, False), xn1, mid.chip_comm())
    mid.chip_done(r2)
    win = scatter(dict(w_in=d_in), "in", sent=dict(w_in=d_in_bf))
    win.pair_done(_comm_alone(win.pair_comm(), "pair_exchange_in"))
    (dx, dg1), r2 = _inproj_bwd(dqa, dkva, dret, dgates, dh1, x, g1, w_in, win.chip_comm())
    win.chip_done(r2)
    small = dict(norm_mix_gain=dg1, b_gates=db, attn_sinks=dsink, ret_gn_gain=dgain, norm_mlp_gain=dg2,
                 norm_final_gain=dg3)
    return loss, dx, small, (win, mid, ff1, ff2)


def _coords():
    return lax.axis_index("x"), lax.axis_index("y"), lax.axis_index("c")


def _flip(v, bit):
    return 1 - v if bit else v


def _xor(a, b):
    return a + b - 2 * a * b


def _gather_relay_comm(shards):
    n = len(shards)

    def parts(ins, outs, sems):
        send_sems, recv_sems, local_sems = sems
        x, y, c = _coords()
        me, sib = (x, y, c), (x, y, 1 - c)

        def ring(core):
            return ((_xor(x, core), _xor(y, 1 - core), core), (_xor(x, 1 - core), _xor(y, core), core),
                    (1 - x, 1 - y, core))

        def slot(a, blk):
            return outs[a].at[4 * blk[0] + 2 * blk[1] + blk[2]]

        def copy(a, k, blk, to, src=None):
            return pltpu.make_async_remote_copy(
                src_ref=slot(a, blk) if src is None else src, dst_ref=slot(a, blk),
                send_sem=send_sems.at[a, k], recv_sem=recv_sems.at[a, k], device_id=to, device_id_type=MESH)

        mine = [pltpu.make_async_copy(ins[a], slot(a, me), local_sems.at[a]) for a in range(n)]
        return me, sib, ring, copy, mine

    def first_copies(ins, outs, sems):
        me, sib, ring, copy, mine = parts(ins, outs, sems)
        src, dst, _ = ring(me[2])
        return mine, [copy(a, k, me, to, src=ins[a]) for a in range(n) for k, to in ((1, src), (2, dst), (0, sib))]

    def start(ins, outs, sems):
        mine, first = first_copies(ins, outs, sems)
        for cp in mine + first:
            cp.start()

    def relay(ins, outs, sems):
        me, sib, ring, copy, _ = parts(ins, outs, sems)
        src, dst, _ = ring(me[2])
        for a in range(n):
            copy(a, 1, src, me).wait_recv()
            copy(a, 3, src, dst).start()
            copy(a, 4, src, sib).start()
        for a in range(n):
            copy(a, 2, dst, me).wait_recv()
            copy(a, 5, dst, sib).start()

    def finish(ins, outs, sems):
        me, sib, ring, copy, _ = parts(ins, outs, sems)
        mine, first = first_copies(ins, outs, sems)
        src, dst, diag = ring(me[2])
        passed = [copy(a, k, blk, to) for a in range(n) for k, blk, to in ((3, src, dst), (4, src, sib), (5, dst, sib))]
        for a in range(n):
            copy(a, 3, diag, me).wait_recv()
            fwd = copy(a, 6, diag, sib)
            fwd.start()
            passed.append(fwd)
        s_src, s_dst, s_diag = ring(sib[2])
        for a in range(n):
            for k, blk in ((0, sib), (4, s_src), (5, s_dst), (6, s_diag)):
                copy(a, k, blk, me).wait_recv()
        for cp in first + passed:
            cp.wait_send()
        for cp in mine:
            cp.wait()

    return _Comm(list(shards), [jax.ShapeDtypeStruct((N_DEV,) + s.shape, s.dtype) for s in shards],
                 [pltpu.SemaphoreType.DMA((n, 7)), pltpu.SemaphoreType.DMA((n, 7)), pltpu.SemaphoreType.DMA((n,))],
                 start, finish, relay)


COLUMN_SHARDED = ("w_in", "w_ff1")
COLUMN_RELAID = ("w_att_up",)


def _gather(shards, names):
    def unpack(got):
        return [jnp.transpose(g, (1, 0, 2)).reshape(g.shape[1], N_DEV * g.shape[2]) if k in COLUMN_RELAID
                else g.reshape(N_DEV * g.shape[1], g.shape[2]) for k, g in zip(names, got)]

    return _gather_relay_comm([shards[k] for k in names]), unpack


def _pair_comm(grads):
    n = len(grads)

    def copies(g, r1, sems):
        send_sems, recv_sems = sems
        x, y, c = _coords()
        return [pltpu.make_async_remote_copy(
            src_ref=g[a].at[2 * j + (1 - c)], dst_ref=r1[a].at[j], send_sem=send_sems.at[a, j],
            recv_sem=recv_sems.at[a, j], device_id=(x, y, 1 - c), device_id_type=MESH)
            for a in range(n) for j in range(4)]

    def start(g, r1, sems):
        for cp in copies(g, r1, sems):
            cp.start()

    def finish(g, r1, sems):
        for cp in copies(g, r1, sems):
            cp.wait_recv()
        for cp in copies(g, r1, sems):
            cp.wait_send()

    return _Comm(list(grads), [jax.ShapeDtypeStruct((4,) + g.shape[1:], g.dtype) for g in grads],
                 [pltpu.SemaphoreType.DMA((n, 4)), pltpu.SemaphoreType.DMA((n, 4))], start, finish)


def _small_comm(small):
    def parts(ins, outs, sems):
        (small_ref,), (small_all,) = ins, outs
        ssend, srecv, lsem = sems
        x, y, c = _coords()
        me_idx = 4 * x + 2 * y + c
        own = pltpu.make_async_copy(small_ref, small_all.at[me_idx], lsem)
        sends, recvs = [], []
        for r in range(1, N_DEV):
            px, py, pc = _flip(x, r & 4), _flip(y, r & 2), _flip(c, r & 1)
            sends.append(pltpu.make_async_remote_copy(
                src_ref=small_ref, dst_ref=small_all.at[me_idx], send_sem=ssend.at[r - 1], recv_sem=srecv.at[r - 1],
                device_id=(px, py, pc), device_id_type=MESH))
            recvs.append(pltpu.make_async_remote_copy(
                src_ref=small_ref, dst_ref=small_all.at[4 * px + 2 * py + pc], send_sem=ssend.at[r - 1],
                recv_sem=srecv.at[r - 1], device_id=(px, py, pc), device_id_type=MESH))
        return own, sends, recvs

    def start(ins, outs, sems):
        own, sends, _ = parts(ins, outs, sems)
        own.start()
        for cp in sends:
            cp.start()

    def finish(ins, outs, sems):
        own, sends, recvs = parts(ins, outs, sems)
        for cp in recvs:
            cp.wait_recv()
        for cp in sends:
            cp.wait_send()
        own.wait()

    return _Comm([small], [jax.ShapeDtypeStruct((N_DEV,) + small.shape, small.dtype)],
                 [pltpu.SemaphoreType.DMA((N_DEV - 1,)), pltpu.SemaphoreType.DMA((N_DEV - 1,)),
                  pltpu.SemaphoreType.DMA], start, finish)


def _pair_sum(grads, r1s, c_arr, tag):
    n = len(grads)
    q = 1

    def body(c_ref, *refs):
        g, r, t = refs[:n], refs[n:2 * n], refs[2 * n:]
        for a in range(n):
            t[a][...] = (g[a][...] + r[a][...]).astype(t[a].dtype)

    def blk(arr):
        return (1, arr.shape[1] // q, arr.shape[2])

    grid_spec = pltpu.PrefetchScalarGridSpec(
        num_scalar_prefetch=1, grid=(4, q),
        in_specs=[pl.BlockSpec(blk(g), lambda j, s, c_ref: (2 * j + c_ref[0], s, 0)) for g in grads]
        + [pl.BlockSpec(blk(r), lambda j, s, c_ref: (j, s, 0)) for r in r1s],
        out_specs=[pl.BlockSpec(blk(r), lambda j, s, c_ref: (j, s, 0)) for r in r1s])
    return pl.pallas_call(
        body, name="pair_sum_" + tag, grid_spec=grid_spec,
        out_shape=[jax.ShapeDtypeStruct(r.shape, RS_PAYLOAD) for r in r1s],
        compiler_params=_params(2),
    )(c_arr, *grads, *r1s)


def _chip_comm(ts):
    n = len(ts)

    def copies(t, r2, sems):
        send_sems, recv_sems = sems
        x, y, c = _coords()
        out = []
        for a in range(n):
            for r in range(1, 4):
                tx, ty = _flip(x, r & 2), _flip(y, r & 1)
                out.append(pltpu.make_async_remote_copy(
                    src_ref=t[a].at[2 * tx + ty], dst_ref=r2[a].at[r - 1], send_sem=send_sems.at[a, r - 1],
                    recv_sem=recv_sems.at[a, r - 1], device_id=(tx, ty, c), device_id_type=MESH))
        return out

    def start(t, r2, sems):
        for cp in copies(t, r2, sems):
            cp.start()

    def finish(t, r2, sems):
        for cp in copies(t, r2, sems):
            cp.wait_recv()
        for cp in copies(t, r2, sems):
            cp.wait_send()

    return _Comm(list(ts), [jax.ShapeDtypeStruct((3,) + t.shape[1:], t.dtype) for t in ts],
                 [pltpu.SemaphoreType.DMA((n, 3)), pltpu.SemaphoreType.DMA((n, 3))], start, finish)


class _Scatter:
    def __init__(self, grads, tag, sent=None):
        def stack(k, g):
            if k in COLUMN_RELAID:
                return jnp.transpose(g.reshape(g.shape[0], N_DEV, g.shape[1] // N_DEV), (1, 0, 2))
            return g.reshape(N_DEV, g.shape[0] // N_DEV, g.shape[1])

        self.names, self.tag = tuple(grads), tag
        self.stacks = [stack(k, g) for k, g in grads.items()]
        self.sent = self.stacks if sent is None else [stack(k, sent[k]) for k in grads]

    def pair_comm(self):
        return _pair_comm(self.sent)

    def pair_done(self, r1s):
        _Scatter.pairs_done((self,), (r1s,), self.tag)

    @staticmethod
    def pairs_done(groups, r1s_of, tag):
        for grp, r1s in zip(groups, r1s_of):
            grp.r1s = list(r1s)
        c_arr = jnp.reshape(lax.axis_index("c"), (1,)).astype(jnp.int32)
        ts = _pair_sum([s for grp in groups for s in grp.stacks], [r for grp in groups for r in grp.r1s], c_arr, tag)
        for grp in groups:
            grp.ts, ts = ts[:len(grp.stacks)], ts[len(grp.stacks):]

    def chip_comm(self):
        return _chip_comm(self.ts)

    def chip_done(self, r2s):
        self.r2s = list(r2s)


def _adamw(w, g, m, v):
    m = ADAM_B1 * m + (1.0 - ADAM_B1) * g
    v = ADAM_B2 * v + (1.0 - ADAM_B2) * jnp.square(g)
    m_hat = m / (1.0 - ADAM_B1 ** ADAM_STEP)
    v_hat = v / (1.0 - ADAM_B2 ** ADAM_STEP)
    delta = -ADAM_LR * (m_hat / (jnp.sqrt(v_hat) + ADAM_EPS) + ADAM_WD * w)
    return delta, m, v


ADAM_STEPS = 4


def _piece_specs(stacks):
    def rows(s):
        return s.shape[1] // ADAM_STEPS
    return ([pl.BlockSpec((1, rows(s), s.shape[2]), lambda i, idx_ref: (idx_ref[0], i, 0)) for s in stacks]
            + [pl.BlockSpec((1, rows(s), s.shape[2]), lambda i, idx_ref: (idx_ref[1], i, 0)) for s in stacks]
            + [pl.BlockSpec((3, rows(s), s.shape[2]), lambda i, idx_ref: (0, i, 0)) for s in stacks])


def _piece_sum(g0, r1, r2):
    return (((g0[0] + r1[0]) + r2[0].astype(F32)) + r2[1].astype(F32)) + r2[2].astype(F32)


def _shard_sum(stacks, r1s, r2s, idx_arr):
    n = len(stacks)

    def body(idx_ref, *refs):
        g0, r1, r2, outs = (refs[k * n:(k + 1) * n] for k in range(4))
        for a in range(n):
            outs[a][...] = _piece_sum(g0[a], r1[a], r2[a])

    out_specs = [pl.BlockSpec((s.shape[1] // ADAM_STEPS, s.shape[2]), lambda i, idx_ref: (i, 0)) for s in stacks]
    grid_spec = pltpu.PrefetchScalarGridSpec(num_scalar_prefetch=1, grid=(ADAM_STEPS,),
                                             in_specs=_piece_specs(stacks), out_specs=out_specs)
    return pl.pallas_call(
        body, name="shard_sum", grid_spec=grid_spec,
        out_shape=[jax.ShapeDtypeStruct(s.shape[1:], F32) for s in stacks],
        compiler_params=_params(1),
    )(idx_arr, *stacks, *r1s, *r2s)


def _adam_big(pieces, summed, ws, ms, vs, idx_arr):
    stacks, r1s, r2s = pieces
    n_p, n = len(stacks), len(ws)

    def body(idx_ref, *refs):
        it = iter(refs)
        g0, r1, r2, gs, w, m, v = ([next(it) for _ in range(k)] for k in (n_p, n_p, n_p, n - n_p, n, n, n))
        outs = list(it)
        for a in range(n):
            g = _piece_sum(g0[a], r1[a], r2[a]) if a < n_p else gs[a - n_p][...]
            delta, nm, nv = _adamw(w[a][...], g, m[a][...], v[a][...])
            outs[4 * a][...] = g
            outs[4 * a + 1][...] = delta
            outs[4 * a + 2][...] = nm
            outs[4 * a + 3][...] = nv

    def shard_spec(w):
        return pl.BlockSpec((w.shape[0] // ADAM_STEPS, w.shape[1]), lambda i, idx_ref: (i, 0))

    in_specs = _piece_specs(stacks) + [shard_spec(w) for w in ws[n_p:]] + [shard_spec(w) for w in ws] * 3
    out_specs = [shard_spec(w) for w in ws for _ in range(4)]
    grid_spec = pltpu.PrefetchScalarGridSpec(num_scalar_prefetch=1, grid=(ADAM_STEPS,), in_specs=in_specs,
                                             out_specs=out_specs)
    return pl.pallas_call(
        body, name="adam_big", grid_spec=grid_spec,
        out_shape=[jax.ShapeDtypeStruct(w.shape, F32) for w in ws for _ in range(4)],
        compiler_params=_params(1),
    )(idx_arr, *stacks, *r1s, *r2s, *summed, *ws, *ms, *vs)


def _adam_small(small_all, w, m, v):
    def body(all_ref, w_ref, m_ref, v_ref, g_ref, d_ref, nm_ref, nv_ref):
        g = all_ref[0]
        for k in range(1, N_DEV):
            g = g + all_ref[k]
        delta, nm, nv = _adamw(w_ref[...], g, m_ref[...], v_ref[...])
        g_ref[...] = g
        d_ref[...] = delta
        nm_ref[...] = nm
        nv_ref[...] = nv

    return pl.pallas_call(
        body, name="adam_small",
        in_specs=[VMEM_FULL] * 4, out_specs=[VMEM_FULL] * 4,
        out_shape=[jax.ShapeDtypeStruct(w.shape, F32)] * 4,
    )(small_all, w, m, v)


RS_PAYLOAD = BF


def _pack_rows(rows):
    rid = lax.broadcasted_iota(jnp.int32, (8, D_MODEL), 0)
    out = jnp.zeros((8, D_MODEL), F32)
    for i, r in enumerate(rows):
        out = jnp.where(rid == i, jnp.broadcast_to(r, (8, D_MODEL)), out)
    return out


def _small_rows(norm_mix_gain, b_gates, attn_sinks, ret_gn_gain, norm_mlp_gain, norm_final_gain):
    return [norm_mix_gain, b_gates[:, :D_MODEL], b_gates[:, D_MODEL:], ret_gn_gain, norm_mlp_gain,
            norm_final_gain.reshape(1, D_MODEL), jnp.pad(attn_sinks, ((0, 0), (0, D_MODEL - ATT_HEADS)))]


def _unpack_small(p):
    return dict(norm_mix_gain=p[0:1], b_gates=jnp.concatenate([p[1:2], p[2:3]], axis=1), ret_gn_gain=p[3:4],
                norm_mlp_gain=p[4:5], norm_final_gain=p[5], attn_sinks=p[6:7, :ATT_HEADS])


WEIGHTS = ("norm_mix_gain", "w_in", "b_gates", "attn_sinks", "ret_gn_gain", "w_att_up", "w_ret_up", "w_out",
           "norm_mlp_gain", "w_ff1", "w_ff2", "norm_final_gain")
BIG = ("w_in", "w_att_up", "w_ret_up", "w_out", "w_ff1", "w_ff2")


def kernel(x, norm_mix_gain, w_in, b_gates, attn_sinks, ret_gn_gain, w_att_up, w_ret_up, w_out, norm_mlp_gain, w_ff1, w_ff2, norm_final_gain, loss_target, m_norm_mix_gain, m_w_in, m_b_gates, m_attn_sinks, m_ret_gn_gain, m_w_att_up, m_w_ret_up, m_w_out, m_norm_mlp_gain, m_w_ff1, m_w_ff2, m_norm_final_gain, v_norm_mix_gain, v_w_in, v_b_gates, v_attn_sinks, v_ret_gn_gain, v_w_att_up, v_w_ret_up, v_w_out, v_norm_mlp_gain, v_w_ff1, v_w_ff2, v_norm_final_gain):
    w = dict(norm_mix_gain=norm_mix_gain, w_in=w_in, b_gates=b_gates, attn_sinks=attn_sinks, ret_gn_gain=ret_gn_gain,
             w_att_up=w_att_up, w_ret_up=w_ret_up, w_out=w_out, norm_mlp_gain=norm_mlp_gain, w_ff1=w_ff1,
             w_ff2=w_ff2, norm_final_gain=norm_final_gain)
    m = dict(norm_mix_gain=m_norm_mix_gain, w_in=m_w_in, b_gates=m_b_gates, attn_sinks=m_attn_sinks,
             ret_gn_gain=m_ret_gn_gain, w_att_up=m_w_att_up, w_ret_up=m_w_ret_up, w_out=m_w_out,
             norm_mlp_gain=m_norm_mlp_gain, w_ff1=m_w_ff1, w_ff2=m_w_ff2, norm_final_gain=m_norm_final_gain)
    v = dict(norm_mix_gain=v_norm_mix_gain, w_in=v_w_in, b_gates=v_b_gates, attn_sinks=v_attn_sinks,
             ret_gn_gain=v_ret_gn_gain, w_att_up=v_w_att_up, w_ret_up=v_w_ret_up, w_out=v_w_out,
             norm_mlp_gain=v_norm_mlp_gain, w_ff1=v_w_ff1, w_ff2=v_w_ff2, norm_final_gain=v_norm_final_gain)

    shards = {k: (w[k][0].T if k in COLUMN_SHARDED else w[k][0]).astype(BF) for k in BIG}
    loss_p, dx, small, groups = _local_step(
        x[0], loss_target[0], norm_mix_gain, b_gates, attn_sinks[0], ret_gn_gain, norm_mlp_gain,
        norm_final_gain.reshape(1, D_MODEL), shards)

    by_name = {}
    for grp in groups:
        for k, stack, r1, r2 in zip(grp.names, grp.stacks, grp.r1s, grp.r2s):
            by_name[k] = (stack, r1, r2)
    lane = lax.broadcasted_iota(jnp.int32, (1, D_MODEL), 1)
    sink_row = jnp.where(lane < ATT_HEADS, jnp.pad(small["attn_sinks"], ((0, 0), (0, D_MODEL - LANE))),
                         jnp.where(lane == ATT_HEADS, jnp.pad(loss_p, ((0, 0), (0, D_MODEL - LANE)), mode="edge"),
                                   0.0))
    small_pack = _pack_rows([small["norm_mix_gain"], small["b_gates"][:, :D_MODEL], small["b_gates"][:, D_MODEL:],
                             small["ret_gn_gain"], small["norm_mlp_gain"], small["norm_final_gain"], sink_row])
    xi, yi, ci = _coords()
    (small_all,) = _comm_alone(_small_comm(small_pack), "small_exchange")
    idx_arr = jnp.stack([4 * xi + 2 * yi + ci, 2 * xi + yi]).astype(jnp.int32)
    in_pieces = ["w_att_up", "w_ret_up", "w_out", "w_ff2", "w_in"]
    in_sum = ["w_ff1"]
    order = in_pieces + in_sum
    summed_t = _shard_sum(*[[by_name[k][i] for k in in_sum] for i in range(3)], idx_arr)

    def shard(tree, k):
        return tree[k][0].T if k == "w_in" else tree[k][0]

    adam_out = _adam_big([[by_name[k][i] for k in in_pieces] for i in range(3)], [g.T for g in summed_t],
                         [shard(w, k) for k in order], [shard(m, k) for k in order], [shard(v, k) for k in order],
                         idx_arr)
    big_out = [adam_out[4 * order.index(k) + i].T if k == "w_in" else adam_out[4 * order.index(k) + i]
               for k in BIG for i in range(4)]
    sm_g, sm_d, sm_m, sm_v = _adam_small(small_all, _pack_rows(_small_rows(*[w[k] for k in WEIGHTS if k not in BIG])),
                                         _pack_rows(_small_rows(*[m[k] for k in WEIGHTS if k not in BIG])),
                                         _pack_rows(_small_rows(*[v[k] for k in WEIGHTS if k not in BIG])))

    loss = sm_g[6, ATT_HEADS]
    kinds = []
    for idx, packed in enumerate((sm_g, sm_d, sm_m, sm_v)):
        out = _unpack_small(packed)
        for a, k in enumerate(BIG):
            out[k] = big_out[4 * a + idx][None]
        kinds.append(out)
    return (loss, dx[None], *[kinds[0][k] for k in WEIGHTS], *[kinds[1][k] for k in WEIGHTS],
            *[kinds[2][k] for k in WEIGHTS], *[kinds[3][k] for k in WEIGHTS])
```

```python
import functools
import math

import jax
import jax.numpy as jnp
from jax import lax
from jax.experimental import pallas as pl
from jax.experimental.pallas import tpu as pltpu

F32 = jnp.float32
BF = jnp.bfloat16
MESH = pl.DeviceIdType.MESH

D_MODEL = 1024
ATT_HEADS = 8
ATT_HEAD_DIM = 64
ATT_BLOCK = 128
ROPE_DIM = 16
ROPE_THETA = 500000.0
RET_HEADS = 4
RET_KEY_DIM = 128
RET_VAL_DIM = 256
RET_CHUNK = 128
RET_ROT_BASE = 10000.0
D_FF = 4096
NORM_EPS = 1e-6
GN_EPS = 1e-6
NEG_INF = -1e30
ATT_SCALE = ATT_HEAD_DIM ** -0.5
RET_SCALE = RET_KEY_DIM ** -0.5

C_QA, C_KA, C_VA, C_QR, C_KR, C_VR, C_GR, C_GATES, C_END = 0, 512, 640, 768, 1280, 1792, 2816, 3840, 5888

ADAM_LR = 0.001
ADAM_B1 = 0.9
ADAM_B2 = 0.999
ADAM_EPS = 1e-08
ADAM_WD = 0.01
ADAM_STEP = 10

N_DEV = 8
LANE = 128
VMEM_LIMIT = 56 * 1024 * 1024
TOKEN_TILE = 512
MLP_TOKEN_TILE = 256
TN_TOKEN_TILE = 4096
FF_CHUNKS = (0, 1024, 2048, 3072, 4096)
IN_CHUNKS_FWD = (C_QA, C_KA, C_QR, C_KR, C_VR, C_GR, C_GATES, C_END)
IN_CHUNKS_BWD = (C_QA, C_KA, C_QR, C_GATES, C_END)

RET_LOG_GAMMA = tuple(math.log1p(-(2.0 ** (-5.0 - h))) for h in range(RET_HEADS))
RET_CHUNK_DECAY = tuple(math.exp(RET_CHUNK * lg) for lg in RET_LOG_GAMMA)

VMEM_FULL = pl.BlockSpec(memory_space=pltpu.VMEM)
SMEM_FULL = pl.BlockSpec(memory_space=pltpu.SMEM)
ANY = pl.BlockSpec(memory_space=pl.ANY)


def _params(n_axes):
    return pltpu.CompilerParams(dimension_semantics=("arbitrary",) * n_axes, vmem_limit_bytes=VMEM_LIMIT)


def _nn(a, b):
    return jnp.dot(a, b, preferred_element_type=F32)


def _nt(a, b):
    return lax.dot_general(a, b, (((1,), (1,)), ((), ())), preferred_element_type=F32)


def _tn(a, b):
    return lax.dot_general(a, b, (((0,), (0,)), ((), ())), preferred_element_type=F32)


def _sigmoid(v):
    return 1.0 / (1.0 + jnp.exp(-v))


def _rows(tile, width):
    return pl.BlockSpec((tile, width), lambda i: (i, 0))


def _cols(height, tile):
    return pl.BlockSpec((height, tile), lambda i: (0, i))


class _Resident:
    def __init__(self, hbm_ref, vmem_ref, sems, bounds):
        self.hbm, self.vmem, self.sems, self.bounds = hbm_ref, vmem_ref, sems, bounds

    @staticmethod
    def scratch(w, bounds):
        return [pltpu.VMEM(w.shape, w.dtype), pltpu.SemaphoreType.DMA((len(bounds) - 1,))]

    @staticmethod
    def load(*weights):
        @pl.when(pl.program_id(0) == 0)
        def _():
            copies = [w._copy(c) for w in weights for c in range(len(w.bounds) - 1)]
            for cp in copies:
                cp.start()
            for cp in copies:
                cp.wait()

    def _rows(self, c):
        return pl.ds(self.bounds[c], self.bounds[c + 1] - self.bounds[c])

    def _copy(self, c):
        return pltpu.make_async_copy(self.hbm.at[self._rows(c)], self.vmem.at[self._rows(c)], self.sems.at[c])

    def chunk(self, c):
        return self.vmem.at[self._rows(c)]


class _Comm:
    def __init__(self, inputs, out_shapes, scratch, start, finish, relay=None):
        self.inputs, self.out_shapes, self.scratch = inputs, out_shapes, scratch
        self.start, self.finish, self.relay = start, finish, relay


RELAY_AT = 0.7


def _join(a, b):
    na_in, na_out, na_sem = len(a.inputs), len(a.out_shapes), len(a.scratch)

    def both(name):
        def run(ins, outs, sems):
            for part, args in ((a, (ins[:na_in], outs[:na_out], sems[:na_sem])),
                               (b, (ins[na_in:], outs[na_out:], sems[na_sem:]))):
                if getattr(part, name) is not None:
                    getattr(part, name)(*args)
        return run

    return _Comm(list(a.inputs) + list(b.inputs), list(a.out_shapes) + list(b.out_shapes),
                 list(a.scratch) + list(b.scratch), both("start"), both("finish"),
                 both("relay") if a.relay or b.relay else None)


def _call(body, args, comm=None, *, name, grid, in_specs, out_specs, out_shape, scratch_shapes=()):
    params = _params(len(grid))
    if comm is None:
        return pl.pallas_call(body, name=name, grid=grid, in_specs=in_specs, out_specs=out_specs, out_shape=out_shape,
                              scratch_shapes=scratch_shapes, compiler_params=params)(*args), ()
    single = not isinstance(out_specs, (list, tuple))
    out_specs_l = [out_specs] if single else list(out_specs)
    out_shape_l = [out_shape] if single else list(out_shape)
    n_in, n_out, n_scr = len(in_specs), len(out_specs_l), len(scratch_shapes)
    n_cin, n_cout = len(comm.inputs), len(comm.out_shapes)

    def hosted(*refs):
        it = iter(refs)
        ins, cin, outs, cout, scr = ([next(it) for _ in range(k)] for k in (n_in, n_cin, n_out, n_cout, n_scr))
        sems = list(it)
        ids = [pl.program_id(k) for k in range(len(grid))]
        first = functools.reduce(jnp.logical_and, [i == 0 for i in ids])
        last = functools.reduce(jnp.logical_and, [i == g - 1 for i, g in zip(ids, grid)])

        @pl.when(first)
        def _():
            comm.start(cin, cout, sems)

        if comm.relay is not None:
            at = [int(grid[0] * RELAY_AT)] + [0] * (len(grid) - 1)

            @pl.when(functools.reduce(jnp.logical_and, [i == v for i, v in zip(ids, at)]))
            def _():
                comm.relay(cin, cout, sems)

        body(*ins, *outs, *scr)

        @pl.when(last)
        def _():
            comm.finish(cin, cout, sems)

    res = pl.pallas_call(
        hosted, name=name, grid=grid, in_specs=list(in_specs) + [ANY] * n_cin,
        out_specs=out_specs_l + [ANY] * n_cout, out_shape=out_shape_l + list(comm.out_shapes),
        scratch_shapes=list(scratch_shapes) + list(comm.scratch), compiler_params=params)(*args, *comm.inputs)
    return (res[0] if single else res[:n_out]), res[n_out:]


def _comm_alone(comm, name):
    n_cin, n_cout = len(comm.inputs), len(comm.out_shapes)

    def body(*refs):
        cin, cout, sems = refs[:n_cin], refs[n_cin:n_cin + n_cout], refs[n_cin + n_cout:]
        comm.start(cin, cout, sems)
        if comm.relay is not None:
            comm.relay(cin, cout, sems)
        comm.finish(cin, cout, sems)

    return pl.pallas_call(body, name=name, in_specs=[ANY] * n_cin, out_specs=[ANY] * n_cout,
                          out_shape=list(comm.out_shapes), scratch_shapes=list(comm.scratch))(*comm.inputs)


def _slabs(v, fn):
    return jnp.concatenate([fn(v[:, LANE * j:LANE * (j + 1)]) for j in range(v.shape[1] // LANE)], axis=1)


def _rope_att(v, ca, sa, sb):
    return _slabs(v, lambda t: t * ca + pltpu.roll(t, LANE - 8, 1) * sa + pltpu.roll(t, 8, 1) * sb)


def _rope_att_t(v, ca, sa, sb):
    return _slabs(v, lambda t: t * ca + pltpu.roll(t * sa, 8, 1) + pltpu.roll(t * sb, LANE - 8, 1))


def _rope_att_rows(v, ct, st, sign):
    parts = []
    for h in range(v.shape[0] // ATT_HEAD_DIM):
        r0 = ATT_HEAD_DIM * h
        x1, x2 = v[r0:r0 + 8], v[r0 + 8:r0 + 16]
        parts += [x1 * ct - sign * (x2 * st), x2 * ct + sign * (x1 * st), v[r0 + 16:r0 + ATT_HEAD_DIM]]
    return jnp.concatenate(parts, axis=0)


def _rope_ret(v, cr, sr):
    return _slabs(v, lambda t: t * cr + pltpu.roll(t, 64, 1) * sr)


def _rope_ret_t(v, cr, sr):
    return _slabs(v, lambda t: t * cr + pltpu.roll(t * sr, 64, 1))


def _rope_lane_tables():
    def inv(dim, theta):
        return theta ** (-jnp.arange(0, dim, 2, dtype=F32) / dim)

    inv_a, inv_r = inv(ROPE_DIM, ROPE_THETA), inv(RET_KEY_DIM, RET_ROT_BASE)
    half = ROPE_DIM // 2
    zeros = jnp.zeros((ATT_HEAD_DIM - ROPE_DIM,), F32)
    freq64 = jnp.concatenate([inv_a, inv_a, zeros])
    lo64 = jnp.concatenate([-jnp.ones((half,), F32), jnp.zeros((half,), F32), zeros])
    hi64 = jnp.concatenate([jnp.zeros((half,), F32), jnp.ones((half,), F32), zeros])
    sign_r = jnp.concatenate([-jnp.ones((64,), F32), jnp.ones((64,), F32)])
    rows = [jnp.tile(freq64, 2), jnp.tile(lo64, 2), jnp.tile(hi64, 2), jnp.tile(inv_r, 2), sign_r]
    lanes = jnp.stack(rows + [jnp.zeros((LANE,), F32)] * (8 - len(rows)))
    return lanes, jnp.broadcast_to(inv_a[:, None], (half, LANE))


def _prologue(x, g1, comm):
    seq = x.shape[0]
    tm = min(TOKEN_TILE, seq)

    def body(x_ref, g_ref, lanes_ref, freq_rows_ref, xn_ref, ca_ref, sa_ref, sb_ref, cr_ref, sr_ref, ct_ref, st_ref):
        xf = x_ref[...]
        r = lax.rsqrt(jnp.mean(xf * xf, axis=-1, keepdims=True) + NORM_EPS)
        xn_ref[...] = (xf * r * g_ref[...]).astype(BF)
        row0 = pl.program_id(0) * tm
        pos = (row0 + lax.broadcasted_iota(jnp.int32, (tm, LANE), 0)).astype(F32)
        ang_a = pos * lanes_ref[0:1, :]
        sin_a = jnp.sin(ang_a)
        ca_ref[...] = jnp.cos(ang_a)
        sa_ref[...] = sin_a * lanes_ref[1:2, :]
        sb_ref[...] = sin_a * lanes_ref[2:3, :]
        ang_r = pos * lanes_ref[3:4, :]
        cr_ref[...] = jnp.cos(ang_r)
        sr_ref[...] = jnp.sin(ang_r) * lanes_ref[4:5, :]
        pos_t = (row0 + lax.broadcasted_iota(jnp.int32, (8, tm), 1)).astype(F32)
        ang_t = pos_t * jnp.concatenate([freq_rows_ref[...]] * (tm // LANE), axis=1)
        ct_ref[...] = jnp.cos(ang_t)
        st_ref[...] = jnp.sin(ang_t)

    tab = _rows(tm, LANE)
    tab_t = _cols(8, tm)
    return _call(
        body, (x, g1, *_rope_lane_tables()), comm, name="prologue", grid=(seq // tm,),
        in_specs=[_rows(tm, D_MODEL), VMEM_FULL, VMEM_FULL, VMEM_FULL],
        out_specs=[_rows(tm, D_MODEL), tab, tab, tab, tab, tab, tab_t, tab_t],
        out_shape=[jax.ShapeDtypeStruct((seq, D_MODEL), BF)] + [jax.ShapeDtypeStruct((seq, LANE), F32)] * 5
        + [jax.ShapeDtypeStruct((8, seq), F32)] * 2)


def _ret_tables():
    c = RET_CHUNK
    lg = jnp.asarray(RET_LOG_GAMMA, F32)
    idx = jnp.arange(c, dtype=F32)
    diff = idx[:, None] - idx[None, :]
    intra = jnp.where(diff >= 0, jnp.exp(jnp.maximum(diff, 0.0) * lg[:, None, None]), 0.0)
    qd = jnp.exp((idx + 1.0)[None, :] * lg[:, None])[..., None]
    kd = jnp.exp((c - 1.0 - idx)[None, :] * lg[:, None])[..., None]
    return intra, jnp.broadcast_to(qd, (RET_HEADS, c, RET_KEY_DIM)), jnp.broadcast_to(kd, (RET_HEADS, c, RET_KEY_DIM))


def _inproj_fwd(xn1, w_in, tabs, comm=None):
    seq = xn1.shape[0]
    tm = min(TOKEN_TILE, seq)

    def body(xn_ref, w_hbm, ca_ref, sa_ref, sb_ref, cr_ref, sr_ref, ct_ref, st_ref,
             qt_ref, kv_ref, kvt_ref, qkr_ref, vr_ref, gr_ref, gates_ref, w_vmem, w_sems):
        w = _Resident(w_hbm, w_vmem, w_sems, IN_CHUNKS_FWD)
        _Resident.load(w)
        ca, sa, sb, cr, sr, ct, st = (ref[...] for ref in (ca_ref, sa_ref, sb_ref, cr_ref, sr_ref, ct_ref, st_ref))
        xn = xn_ref[...]
        qt_ref[...] = _rope_att_rows(_nt(w.chunk(0)[...], xn), ct, st, 1.0).astype(BF)
        w_kv = w.chunk(1)
        kvt = _nt(w_kv[...], xn)
        kvt_ref[:LANE, :] = _rope_att_rows(kvt[:LANE], ct, st, 1.0).astype(BF)
        kvt_ref[LANE:, :] = kvt[LANE:].astype(BF)
        kvn = _nt(xn, w_kv[...])
        kv_ref[:, :LANE] = _rope_att(kvn[:, :LANE], ca, sa, sb).astype(BF)
        kv_ref[:, LANE:] = kvn[:, LANE:].astype(BF)
        qr = _rope_ret(_nt(xn, w.chunk(2)[...]), cr, sr)
        qkr_ref[:, :512] = qr.astype(BF)
        kr = _rope_ret(_nt(xn, w.chunk(3)[...]), cr, sr) * RET_SCALE
        qkr_ref[:, 512:] = kr.astype(BF)
        vr_ref[...] = _nt(xn, w.chunk(4)[...]).astype(BF)
        gr_ref[...] = _nt(xn, w.chunk(5)[...]).astype(BF)
        gates_ref[...] = _nt(xn, w.chunk(6)[...]).astype(BF)

    tab = _rows(tm, LANE)
    tab_t = _cols(8, tm)
    return _call(
        body, (xn1, w_in, *tabs), comm, name="inproj_fwd", grid=(seq // tm,),
        in_specs=[_rows(tm, D_MODEL), ANY, tab, tab, tab, tab, tab, tab_t, tab_t],
        out_specs=[_cols(512, tm), _rows(tm, 256), _cols(256, tm), _rows(tm, 1024),
                   _rows(tm, 1024), _rows(tm, 1024), _rows(tm, 2048)],
        out_shape=[jax.ShapeDtypeStruct(s, BF) for s in ((512, seq), (seq, 256), (256, seq),
                                                         (seq, 1024), (seq, 1024), (seq, 1024), (seq, 2048))],
        scratch_shapes=_Resident.scratch(w_in, IN_CHUNKS_FWD))


ATT_Q_PER_KV = ATT_HEADS // 2


def _att_group(ref, kv):
    d = ATT_HEAD_DIM
    g = jnp.concatenate([ref[d * (ATT_Q_PER_KV * kv + j):d * (ATT_Q_PER_KV * kv + j + 1), :]
                         for j in range(ATT_Q_PER_KV)], axis=1)
    z = jnp.zeros_like(g)
    return jnp.concatenate([g, z] if kv == 0 else [z, g], axis=0)


def _att_probs(kwin, qpad, sink_ref, kv, i):
    c = ATT_BLOCK
    kj = lax.broadcasted_iota(jnp.int32, (2 * c, c), 0)
    qi = lax.broadcasted_iota(jnp.int32, (2 * c, c), 1)
    allowed = (kj > qi) & (kj <= qi + c) & ((kj >= c) | (i > 0))
    allowed = jnp.concatenate([allowed] * ATT_Q_PER_KV, axis=1)
    s = jnp.where(allowed, _nn(kwin, qpad) * ATT_SCALE, NEG_INF)
    sk = jnp.concatenate([jnp.full((1, c), sink_ref[ATT_Q_PER_KV * kv + j], F32) for j in range(ATT_Q_PER_KV)], axis=1)
    m = jnp.maximum(jnp.max(s, axis=0, keepdims=True), sk)
    pe = jnp.exp(s - m)
    psink = jnp.exp(sk - m)
    inv = 1.0 / (jnp.sum(pe, axis=0, keepdims=True) + psink)
    return pe * inv, psink * inv


def _seq_fwd(qt, kv, kvt, sinks, qkr, vr, gr, gain, rtabs, comm=None):
    seq = kv.shape[0]
    c = ATT_BLOCK
    assert c == RET_CHUNK
    d = ATT_HEAD_DIM
    nb = seq // c
    dk, dv = RET_KEY_DIM, RET_VAL_DIM

    def body(sink_ref, qt_ref, kvc_ref, kvp_ref, kvtc_ref, kvtp_ref, qk_ref, v_ref, g_ref, gain_ref, intra_ref,
             qd_ref, kd_ref, ot_ref, ret_ref, yrp_ref, st_ref, state):
        i = pl.program_id(0)
        kwin = jnp.concatenate([kvp_ref[:, :LANE], kvc_ref[:, :LANE]], axis=0)
        vtwin = jnp.concatenate([kvtp_ref[LANE:, :], kvtc_ref[LANE:, :]], axis=1)
        for g in (0, 1):
            prob, _ = _att_probs(kwin, _att_group(qt_ref, g), sink_ref, g, i)
            out = _nn(vtwin[d * g:d * (g + 1), :], prob.astype(BF))
            for j in range(ATT_Q_PER_KV):
                h = ATT_Q_PER_KV * g + j
                ot_ref[d * h:d * (h + 1), :] = out[:, c * j:c * (j + 1)].astype(BF)

        @pl.when(i == 0)
        def _():
            state[...] = jnp.zeros_like(state)

        for h in range(RET_HEADS):
            qh = qk_ref[:, dk * h:dk * (h + 1)]
            kh = qk_ref[:, 512 + dk * h:512 + dk * (h + 1)]
            vh = v_ref[:, dv * h:dv * (h + 1)]
            sh = state[h]
            shb = sh.astype(BF)
            st_ref[0, h] = shb
            att = _nt(qh, kh) * intra_ref[h]
            inner = _nn(att.astype(BF), vh)
            cross = _nn((qh.astype(F32) * qd_ref[h]).astype(BF), shb)
            out = inner + cross
            state[h] = sh * RET_CHUNK_DECAY[h] + _tn((kh.astype(F32) * kd_ref[h]).astype(BF), vh)
            ret_ref[:, dv * h:dv * (h + 1)] = out
            mu = jnp.mean(out, axis=-1, keepdims=True)
            dev = out - mu
            var = jnp.mean(dev * dev, axis=-1, keepdims=True)
            y = dev * lax.rsqrt(var + GN_EPS) * gain_ref[:, dv * h:dv * (h + 1)]
            g = g_ref[:, dv * h:dv * (h + 1)].astype(F32)
            yrp_ref[:, dv * h:dv * (h + 1)] = (g * _sigmoid(g) * y).astype(BF)

    prev = lambda i: jnp.maximum(i - 1, 0)
    return _call(
        body, (sinks, qt, kv, kv, kvt, kvt, qkr, vr, gr, gain, *rtabs), comm, name="seq_fwd", grid=(nb,),
        in_specs=[SMEM_FULL, _cols(512, c), _rows(c, 256), pl.BlockSpec((c, 256), lambda i: (prev(i), 0)),
                  _cols(256, c), pl.BlockSpec((256, c), lambda i: (0, prev(i))),
                  _rows(c, 1024), _rows(c, 1024), _rows(c, 1024), VMEM_FULL, VMEM_FULL, VMEM_FULL, VMEM_FULL],
        out_specs=[_cols(512, c), _rows(c, 1024), _rows(c, 1024),
                   pl.BlockSpec((1, RET_HEADS, dk, dv), lambda i: (i, 0, 0, 0))],
        out_shape=[jax.ShapeDtypeStruct((512, seq), BF), jax.ShapeDtypeStruct((seq, 1024), F32),
                   jax.ShapeDtypeStruct((seq, 1024), BF), jax.ShapeDtypeStruct((nb, RET_HEADS, dk, dv), BF)],
        scratch_shapes=[pltpu.VMEM((RET_HEADS, dk, dv), F32)])


def _mix_fwd(oat, yrp, gates, bg, x, w_att, w_ret, w_out):
    seq = x.shape[0]
    tm = min(TOKEN_TILE, seq)

    def body(oat_ref, yrp_ref, gates_ref, bg_ref, x_ref, wa_ref, wr_ref, wo_ref, ya_ref, yr_ref, mg_ref, h1_ref):
        ya = _tn(oat_ref[...], wa_ref[...])
        yr = _nn(yrp_ref[...], wr_ref[...])
        gt = _sigmoid(gates_ref[...].astype(F32) + bg_ref[...])
        merged = (gt[:, :D_MODEL] * ya + gt[:, D_MODEL:] * yr).astype(BF)
        ya_ref[...] = ya.astype(BF)
        yr_ref[...] = yr.astype(BF)
        mg_ref[...] = merged
        h1_ref[...] = x_ref[...] + _nn(merged, wo_ref[...])

    return pl.pallas_call(
        body, name="mix_fwd", grid=(seq // tm,),
        in_specs=[_cols(512, tm), _rows(tm, 1024), _rows(tm, 2048), VMEM_FULL, _rows(tm, D_MODEL),
                  VMEM_FULL, VMEM_FULL, VMEM_FULL],
        out_specs=[_rows(tm, D_MODEL)] * 4,
        out_shape=[jax.ShapeDtypeStruct((seq, D_MODEL), BF)] * 3 + [jax.ShapeDtypeStruct((seq, D_MODEL), F32)],
        compiler_params=_params(1),
    )(oat, yrp, gates, bg, x, w_att, w_ret, w_out)


def _mlp_loss_step(h1, g2, g3, target, w_ff1, w_ff2):
    seq = h1.shape[0]
    tm = min(MLP_TOKEN_TILE, seq)
    n_chunks = len(FF_CHUNKS) - 1

    def body(h1_ref, g2_ref, g3_ref, t_ref, w1_hbm, w2_hbm,
             xn2_ref, hdn_ref, dh2_ref, du_ref, dh1_ref, loss_ref, dg3_ref, dg2_ref,
             w1_vmem, w1_sems, w2_vmem, w2_sems, relu_u):
        w1 = _Resident(w1_hbm, w1_vmem, w1_sems, FF_CHUNKS)
        w2 = _Resident(w2_hbm, w2_vmem, w2_sems, FF_CHUNKS)
        _Resident.load(w1, w2)

        @pl.when(pl.program_id(0) == 0)
        def _():
            loss_ref[...] = jnp.zeros_like(loss_ref)
            dg3_ref[...] = jnp.zeros_like(dg3_ref)
            dg2_ref[...] = jnp.zeros_like(dg2_ref)

        h1v = h1_ref[...]
        r2 = lax.rsqrt(jnp.mean(h1v * h1v, axis=-1, keepdims=True) + NORM_EPS)
        xn2 = (h1v * r2 * g2_ref[...]).astype(BF)
        xn2_ref[...] = xn2
        h2 = h1v
        for c in range(n_chunks):
            cols = slice(FF_CHUNKS[c], FF_CHUNKS[c + 1])
            a = jnp.maximum(_nt(xn2, w1.chunk(c)[...]), 0.0)
            relu_u[:, cols] = a.astype(BF)
            hdn = jnp.square(a).astype(BF)
            hdn_ref[:, cols] = hdn
            h2 = h2 + _nn(hdn, w2.chunk(c)[...])
        r3 = lax.rsqrt(jnp.mean(h2 * h2, axis=-1, keepdims=True) + NORM_EPS)
        hn = h2 * r3
        err = hn * g3_ref[...] - t_ref[...]
        loss_ref[...] += jnp.sum(err * err) * (0.5 / D_MODEL)
        dy = err * (1.0 / D_MODEL)
        dg3_ref[...] += jnp.sum(dy * hn, axis=0, keepdims=True)
        z = dy * g3_ref[...]
        dh2 = r3 * (z - hn * jnp.mean(z * hn, axis=-1, keepdims=True))
        dh2b = dh2.astype(BF)
        dh2_ref[...] = dh2b
        dxn2 = jnp.zeros_like(dh2)
        for c in range(n_chunks):
            cols = slice(FF_CHUNKS[c], FF_CHUNKS[c + 1])
            du = (_nt(dh2b, w2.chunk(c)[...]) * (2.0 * relu_u[:, cols].astype(F32))).astype(BF)
            du_ref[:, cols] = du
            dxn2 = dxn2 + _nn(du, w1.chunk(c)[...])
        dnorm, dg = _rms_bwd(dxn2, h1v, g2_ref[...])
        dh1_ref[...] = dh2 + dnorm
        dg2_ref[...] += dg

    gain_acc = pl.BlockSpec((1, D_MODEL), lambda i: (0, 0))
    return pl.pallas_call(
        body, name="mlp_loss_step", grid=(seq // tm,),
        in_specs=[_rows(tm, D_MODEL), VMEM_FULL, VMEM_FULL, _rows(tm, D_MODEL), ANY, ANY],
        out_specs=[_rows(tm, D_MODEL), _rows(tm, D_FF), _rows(tm, D_MODEL), _rows(tm, D_FF), _rows(tm, D_MODEL),
                   pl.BlockSpec((1, LANE), lambda i: (0, 0)), gain_acc, gain_acc],
        out_shape=[jax.ShapeDtypeStruct((seq, D_MODEL), BF), jax.ShapeDtypeStruct((seq, D_FF), BF),
                   jax.ShapeDtypeStruct((seq, D_MODEL), BF), jax.ShapeDtypeStruct((seq, D_FF), BF),
                   jax.ShapeDtypeStruct((seq, D_MODEL), F32), jax.ShapeDtypeStruct((1, LANE), F32),
                   jax.ShapeDtypeStruct((1, D_MODEL), F32), jax.ShapeDtypeStruct((1, D_MODEL), F32)],
        scratch_shapes=(_Resident.scratch(w_ff1, FF_CHUNKS) + _Resident.scratch(w_ff2, FF_CHUNKS)
                        + [pltpu.VMEM((tm, D_FF), BF)]),
        compiler_params=_params(1),
    )(h1, g2, g3, target, w_ff1, w_ff2)


def _rms_bwd(dxn, xin, gain):
    r = lax.rsqrt(jnp.mean(xin * xin, axis=-1, keepdims=True) + NORM_EPS)
    xhat = xin * r
    z = dxn * gain
    dxin = r * (z - xhat * jnp.mean(z * xhat, axis=-1, keepdims=True))
    return dxin, jnp.sum(dxn * xhat, axis=0, keepdims=True)


def _mm_tn(a, b, name, a_is_t=False, comm=None):
    kdim, seq = a.shape if a_is_t else a.shape[::-1]
    ndim = b.shape[1]
    ts = min(TN_TOKEN_TILE, seq)
    tk = min(kdim, 1024)
    tn = min(ndim, 1024)
    n_steps = seq // ts

    def body(a_ref, b_ref, o_ref):
        mm = _nn if a_is_t else _tn
        part = mm(a_ref[...].astype(BF), b_ref[...].astype(BF))
        if n_steps == 1:
            o_ref[...] = part
        else:
            @pl.when(pl.program_id(2) == 0)
            def _():
                o_ref[...] = jnp.zeros_like(o_ref)

            o_ref[...] += part

    a_spec = (pl.BlockSpec((tk, ts), lambda k, n, s: (k, s)) if a_is_t
              else pl.BlockSpec((ts, tk), lambda k, n, s: (s, k)))
    res = _call(
        body, (a, b), comm, name=name, grid=(kdim // tk, ndim // tn, n_steps),
        in_specs=[a_spec, pl.BlockSpec((ts, tn), lambda k, n, s: (s, n))],
        out_specs=pl.BlockSpec((tk, tn), lambda k, n, s: (k, n)),
        out_shape=jax.ShapeDtypeStruct((kdim, ndim), F32))
    return res[0] if comm is None else res


def _dw_in_t(pieces, is_t, xn1, comm=None):
    seq = xn1.shape[0]
    tr = 256
    n_blk = [(p.shape[0] if t else p.shape[1]) // tr for p, t in zip(pieces, is_t)]
    offs = [sum(n_blk[:i]) for i in range(len(pieces) + 1)]
    n_p = len(pieces)

    def body(*refs):
        piece_refs, xn_ref, o_ref, ob_ref = refs[:n_p], refs[n_p], refs[n_p + 1], refs[n_p + 2]
        r = pl.program_id(0)
        for p in range(n_p):
            @pl.when((r >= offs[p]) & (r < offs[p + 1]))
            def _(p=p):
                res = (_nn if is_t[p] else _tn)(piece_refs[p][...], xn_ref[...])
                o_ref[...] = res
                ob_ref[...] = res.astype(BF)

    def piece_spec(p):
        def index(r):
            blk = jnp.clip(r - offs[p], 0, n_blk[p] - 1)
            return (blk, 0) if is_t[p] else (0, blk)
        return pl.BlockSpec((tr, seq) if is_t[p] else (seq, tr), index)

    return _call(
        body, (*pieces, xn1), comm, name="dw_in", grid=(offs[-1],),
        in_specs=[piece_spec(p) for p in range(n_p)] + [VMEM_FULL],
        out_specs=[_rows(tr, D_MODEL), _rows(tr, D_MODEL)],
        out_shape=[jax.ShapeDtypeStruct((offs[-1] * tr, D_MODEL), F32),
                   jax.ShapeDtypeStruct((offs[-1] * tr, D_MODEL), BF)])


def _mix_bwd(dh1, ya, yr, gates, bg, merged, oat, yrp, w_att, w_ret, w_out, comm=None):
    seq = dh1.shape[0]
    tm = min(TOKEN_TILE, seq)

    def body(dh1_ref, ya_ref, yr_ref, gates_ref, bg_ref, mg_ref, oat_ref, yrp_ref, wa_ref, wr_ref, wo_ref,
             dgates_ref, doa_ref, dyrp_ref, db_ref, dwa_ref, dwr_ref, dwo_ref):
        @pl.when(pl.program_id(0) == 0)
        def _():
            for ref in (db_ref, dwa_ref, dwr_ref, dwo_ref):
                ref[...] = jnp.zeros_like(ref)

        dh1b = dh1_ref[...].astype(BF)
        dm = _nt(dh1b, wo_ref[...])
        gt = _sigmoid(gates_ref[...].astype(F32) + bg_ref[...])
        ga, gr = gt[:, :D_MODEL], gt[:, D_MODEL:]
        dya = (dm * ga).astype(BF)
        dyr = (dm * gr).astype(BF)
        dga = dm * ya_ref[...].astype(F32) * ga * (1.0 - ga)
        dgr = dm * yr_ref[...].astype(F32) * gr * (1.0 - gr)
        dgates_ref[:, :D_MODEL] = dga.astype(BF)
        dgates_ref[:, D_MODEL:] = dgr.astype(BF)
        db_ref[:, :D_MODEL] += jnp.sum(dga, axis=0, keepdims=True)
        db_ref[:, D_MODEL:] += jnp.sum(dgr, axis=0, keepdims=True)
        doa_ref[...] = _nt(wa_ref[...], dya).astype(BF)
        dyrp_ref[...] = _nt(dyr, wr_ref[...]).astype(BF)
        dwo_ref[...] += _tn(mg_ref[...], dh1b)
        dwr_ref[...] += _tn(yrp_ref[...], dyr)
        dwa_ref[...] += _nn(oat_ref[...], dya)

    acc = lambda r, c: pl.BlockSpec((r, c), lambda i: (0, 0))
    return _call(
        body, (dh1, ya, yr, gates, bg, merged, oat, yrp, w_att, w_ret, w_out), comm, name="mix_bwd",
        grid=(seq // tm,),
        in_specs=[_rows(tm, D_MODEL), _rows(tm, D_MODEL), _rows(tm, D_MODEL), _rows(tm, 2048), VMEM_FULL,
                  _rows(tm, D_MODEL), _cols(512, tm), _rows(tm, 1024), VMEM_FULL, VMEM_FULL, VMEM_FULL],
        out_specs=[_rows(tm, 2048), _cols(512, tm), _rows(tm, 1024), acc(1, 2048), acc(512, D_MODEL),
                   acc(1024, D_MODEL), acc(D_MODEL, D_MODEL)],
        out_shape=[jax.ShapeDtypeStruct((seq, 2048), BF), jax.ShapeDtypeStruct((512, seq), BF),
                   jax.ShapeDtypeStruct((seq, 1024), BF), jax.ShapeDtypeStruct((1, 2048), F32),
                   jax.ShapeDtypeStruct((512, D_MODEL), F32), jax.ShapeDtypeStruct((1024, D_MODEL), F32),
                   jax.ShapeDtypeStruct((D_MODEL, D_MODEL), F32)])


def _seq_bwd(qt, kv, kvt, dot, sinks, qkr, vr, gr, ret, dyrp, states, gain, rtabs, tabs, comm=None):
    seq = kv.shape[0]
    c = ATT_BLOCK
    assert c == RET_CHUNK
    d = ATT_HEAD_DIM
    nb = seq // c
    dk, dv = RET_KEY_DIM, RET_VAL_DIM
    ca, sa, sb, cr, sr, ct, st = tabs

    def body(sink_ref, qt_ref, kvc_ref, kvp_ref, kvtc_ref, kvtp_ref, dot_ref, ca_ref, sa_ref, sb_ref, ct_ref, st_ref,
             qk_ref, v_ref, g_ref, ret_ref, dyp_ref, st8_ref, gain_ref, intra_ref, qd_ref, kd_ref, cr_ref, sr_ref,
             dqt_ref, dkv_ref, dsink_ref, dall_ref, dgain_ref, carry, dstate):
        step = pl.program_id(0)
        i = nb - 1 - step

        @pl.when(step == 0)
        def _():
            carry[...] = jnp.zeros_like(carry)
            dsink_ref[...] = jnp.zeros_like(dsink_ref)
            dstate[...] = jnp.zeros_like(dstate)
            dgain_ref[...] = jnp.zeros_like(dgain_ref)

        kwin = jnp.concatenate([kvp_ref[:, :LANE], kvc_ref[:, :LANE]], axis=0)
        vwin = jnp.concatenate([kvp_ref[:, LANE:], kvc_ref[:, LANE:]], axis=0)
        ktwin = jnp.concatenate([kvtp_ref[:LANE, :], kvtc_ref[:LANE, :]], axis=1)
        ctv, stv = ct_ref[...], st_ref[...]
        lane1 = lax.broadcasted_iota(jnp.int32, (1, LANE), 1)
        dkw = jnp.zeros((2 * c, LANE), F32)
        dvw = jnp.zeros((2 * c, LANE), F32)
        dsink = jnp.zeros((1, LANE), F32)
        for g in (0, 1):
            qpad, dopad = _att_group(qt_ref, g), _att_group(dot_ref, g)
            prob, psink = _att_probs(kwin, qpad, sink_ref, g, i)
            dprob = _nn(vwin, dopad)
            drow = jnp.sum(dprob * prob, axis=0, keepdims=True)
            ds = (prob * (dprob - drow) * ATT_SCALE).astype(BF)
            dqg = _nn(ktwin[d * g:d * (g + 1), :], ds)
            dkw = dkw + _nt(ds, qpad)
            dvw = dvw + _nt(prob.astype(BF), dopad)
            dsink_lanes = psink * drow
            for j in range(ATT_Q_PER_KV):
                h = ATT_Q_PER_KV * g + j
                dqt_ref[d * h:d * (h + 1), :] = _rope_att_rows(dqg[:, c * j:c * (j + 1)], ctv, stv, -1.0).astype(BF)
                dsink = dsink + jnp.where(lane1 == h, -jnp.sum(dsink_lanes[:, c * j:c * (j + 1)]), 0.0)
        dsink_ref[...] += dsink
        total = carry[...] + jnp.concatenate([dkw[c:], dvw[c:]], axis=1)
        dkv_ref[:, :LANE] = _rope_att_t(total[:, :LANE], ca_ref[...], sa_ref[...], sb_ref[...]).astype(BF)
        dkv_ref[:, LANE:] = total[:, LANE:].astype(BF)
        carry[...] = jnp.concatenate([dkw[:c], dvw[:c]], axis=1)

        crv, srv = cr_ref[...], sr_ref[...]
        for h in range(RET_HEADS):
            vs = slice(dv * h, dv * (h + 1))
            qh = qk_ref[:, dk * h:dk * (h + 1)]
            kh = qk_ref[:, 512 + dk * h:512 + dk * (h + 1)]
            vh = v_ref[:, vs]
            out = ret_ref[:, vs]
            g = g_ref[:, vs].astype(F32)
            dyp = dyp_ref[:, vs].astype(F32)
            gain_h = gain_ref[:, vs]
            mu = jnp.mean(out, axis=-1, keepdims=True)
            dev = out - mu
            rstd = lax.rsqrt(jnp.mean(dev * dev, axis=-1, keepdims=True) + GN_EPS)
            yn = dev * rstd
            sg = _sigmoid(g)
            dg = dyp * (yn * gain_h) * (sg * (1.0 + g * (1.0 - sg)))
            dy = dyp * (g * sg)
            dgain_ref[:, vs] += jnp.sum(dy * yn, axis=0, keepdims=True)
            dyn = dy * gain_h
            dout = rstd * (dyn - jnp.mean(dyn, axis=-1, keepdims=True)
                           - yn * jnp.mean(dyn * yn, axis=-1, keepdims=True))
            doutb = dout.astype(BF)
            sc = st8_ref[0, h]
            dsp = dstate[h]
            dspb = dsp.astype(BF)
            intra, qdv, kdv = intra_ref[h], qd_ref[h], kd_ref[h]
            att = _nt(qh, kh) * intra
            dab = (_nt(doutb, vh) * intra).astype(BF)
            qdec = (qh.astype(F32) * qdv).astype(BF)
            kdec = (kh.astype(F32) * kdv).astype(BF)
            dq = _nn(dab, kh) + _nt(doutb, sc) * qdv
            dkk = _tn(dab, qh) + _nt(vh, dspb) * kdv
            dvv = _tn(att.astype(BF), doutb) + _nn(kdec, dspb)
            dstate[h] = dsp * RET_CHUNK_DECAY[h] + _tn(qdec, doutb)
            dall_ref[:, dk * h:dk * (h + 1)] = _rope_ret_t(dq, crv, srv).astype(BF)
            dall_ref[:, 512 + dk * h:512 + dk * (h + 1)] = (_rope_ret_t(dkk, crv, srv) * RET_SCALE).astype(BF)
            dall_ref[:, 1024 + dv * h:1024 + dv * (h + 1)] = dvv.astype(BF)
            dall_ref[:, 2048 + dv * h:2048 + dv * (h + 1)] = dg.astype(BF)

    cur = lambda s: nb - 1 - s
    prev = lambda s: jnp.maximum(nb - 2 - s, 0)
    rows = lambda w: pl.BlockSpec((c, w), lambda s: (cur(s), 0))
    cols = lambda h: pl.BlockSpec((h, c), lambda s: (0, cur(s)))
    acc = lambda w: pl.BlockSpec((1, w), lambda s: (0, 0))
    return _call(
        body, (sinks, qt, kv, kv, kvt, kvt, dot, ca, sa, sb, ct, st, qkr, vr, gr, ret, dyrp, states, gain, *rtabs,
               cr, sr), comm, name="seq_bwd", grid=(nb,),
        in_specs=[SMEM_FULL, cols(512), rows(256), pl.BlockSpec((c, 256), lambda s: (prev(s), 0)), cols(256),
                  pl.BlockSpec((256, c), lambda s: (0, prev(s))), cols(512), rows(LANE), rows(LANE), rows(LANE),
                  cols(8), cols(8), rows(1024), rows(1024), rows(1024), rows(1024), rows(1024),
                  pl.BlockSpec((1, RET_HEADS, dk, dv), lambda s: (cur(s), 0, 0, 0)),
                  VMEM_FULL, VMEM_FULL, VMEM_FULL, VMEM_FULL, rows(LANE), rows(LANE)],
        out_specs=[cols(512), rows(256), acc(LANE), rows(3072), acc(1024)],
        out_shape=[jax.ShapeDtypeStruct((512, seq), BF), jax.ShapeDtypeStruct((seq, 256), BF),
                   jax.ShapeDtypeStruct((1, LANE), F32), jax.ShapeDtypeStruct((seq, 3072), BF),
                   jax.ShapeDtypeStruct((1, 1024), F32)],
        scratch_shapes=[pltpu.VMEM((c, 256), F32), pltpu.VMEM((RET_HEADS, dk, dv), F32)])


def _inproj_bwd(dqa, dkva, dret, dgates, dh1, x, g1, w_in, comm=None):
    seq = x.shape[0]
    tm = min(TOKEN_TILE, seq)

    def body(dqa_ref, dkva_ref, dret_ref, dgates_ref, dh1_ref, x_ref, g_ref, w_hbm, dx_ref, dg1_ref, w_vmem, w_sems):
        w = _Resident(w_hbm, w_vmem, w_sems, IN_CHUNKS_BWD)
        _Resident.load(w)

        @pl.when(pl.program_id(0) == 0)
        def _():
            dg1_ref[...] = jnp.zeros_like(dg1_ref)

        dxn = _tn(dqa_ref[...], w.chunk(0)[...])
        dxn = dxn + _nn(dkva_ref[...], w.chunk(1)[...])
        dxn = dxn + _nn(dret_ref[...], w.chunk(2)[...])
        dxn = dxn + _nn(dgates_ref[...], w.chunk(3)[...])
        dnorm, dg = _rms_bwd(dxn, x_ref[...], g_ref[...])
        dx_ref[...] = dh1_ref[...] + dnorm
        dg1_ref[...] += dg

    return _call(
        body, (dqa, dkva, dret, dgates, dh1, x, g1, w_in), comm, name="inproj_bwd", grid=(seq // tm,),
        in_specs=[_cols(512, tm), _rows(tm, 256), _rows(tm, 3072), _rows(tm, 2048), _rows(tm, D_MODEL),
                  _rows(tm, D_MODEL), VMEM_FULL, ANY],
        out_specs=[_rows(tm, D_MODEL), pl.BlockSpec((1, D_MODEL), lambda i: (0, 0))],
        out_shape=[jax.ShapeDtypeStruct((seq, D_MODEL), F32), jax.ShapeDtypeStruct((1, D_MODEL), F32)],
        scratch_shapes=_Resident.scratch(w_in, IN_CHUNKS_BWD))


def _local_step(x, target, g1, bg, sinks, gain, g2, g3, shards):
    rtabs = _ret_tables()
    scatter = _Scatter

    comm, unpack = _gather(shards, ("w_in",))
    (xn1, ca, sa, sb, cr, sr, ct, st), got = _prologue(x, g1, comm)
    (w_in,) = unpack(got)
    comm_mid, unpack_mid = _gather(shards, ("w_att_up", "w_ret_up", "w_out"))
    comm_ff1, unpack_ff1 = _gather(shards, ("w_ff1",))
    (qt, kv, kvt, qkr, vr, gr, gates), got = _inproj_fwd(xn1, w_in, (ca, sa, sb, cr, sr, ct, st),
                                                         _join(comm_mid, comm_ff1))
    w_att, w_ret, w_out = unpack_mid(got[:3])
    (w_ff1,) = unpack_ff1(got[3:])
    comm, unpack = _gather(shards, ("w_ff2",))
    (oa, ret, yrp, states), got = _seq_fwd(qt, kv, kvt, sinks, qkr, vr, gr, gain, rtabs, comm)
    (w_ff2,) = unpack(got)
    ya, yr, merged, h1 = _mix_fwd(oa, yrp, gates, bg, x, w_att, w_ret, w_out)
    xn2, hdn, dh2, du, dh1, loss, dg3, dg2 = _mlp_loss_step(h1, g2, g3, target, w_ff1, w_ff2)

    ff2 = scatter(dict(w_ff2=_mm_tn(hdn, dh2, "dw_ff2")), "ff2")
    ff1 = scatter(dict(w_ff1=_mm_tn(du, xn2, "dw_ff1")), "ff1")
    (dgates, doa, dyrp, db, d_att, d_ret, d_out), got = _mix_bwd(
        dh1, ya, yr, gates, bg, merged, oa, yrp, w_att, w_ret, w_out, _join(ff2.pair_comm(), ff1.pair_comm()))
    _Scatter.pairs_done((ff2, ff1), (got[:1], got[1:]), "ff")
    mid = scatter(dict(w_att_up=d_att, w_ret_up=d_ret, w_out=d_out), "mid")
    (dqa, dkva, dsink, dret, dgain), got = _seq_bwd(
        qt, kv, kvt, doa, sinks, qkr, vr, gr, ret, dyrp, states, gain, rtabs, (ca, sa, sb, cr, sr, ct, st),
        _join(_join(ff2.chip_comm(), ff1.chip_comm()), mid.pair_comm()))
    ff2.chip_done(got[:1])
    ff1.chip_done(got[1:2])
    mid.pair_done(got[2:])
    (d_in, d_in_bf), r2 = _dw_in_t((dqa, dkva, dret, dgates), (True, False, False, False), xn1, mid.chip_comm())
    mid.chip_done(r2)
    win = scatter(dict(w_in=d_in), "in", sent=dict(w_in=d_in_bf))
    win.pair_done(_comm_alone(win.pair_comm(), "pair_exchange_in"))
    (dx, dg1), r2 = _inproj_bwd(dqa, dkva, dret, dgates, dh1, x, g1, w_in, win.chip_comm())
    win.chip_done(r2)
    small = dict(norm_mix_gain=dg1, b_gates=db, attn_sinks=dsink, ret_gn_gain=dgain, norm_mlp_gain=dg2,
                 norm_final_gain=dg3)
    return loss, dx, small, (win, mid, ff1, ff2)


def _coords():
    return lax.axis_index("x"), lax.axis_index("y"), lax.axis_index("c")


def _flip(v, bit):
    return 1 - v if bit else v


def _xor(a, b):
    return a + b - 2 * a * b


def _gather_relay_comm(shards):
    n = len(shards)

    def parts(ins, outs, sems):
        send_sems, recv_sems, local_sems = sems
        x, y, c = _coords()
        me, sib = (x, y, c), (x, y, 1 - c)

        def ring(core):
            return ((_xor(x, core), _xor(y, 1 - core), core), (_xor(x, 1 - core), _xor(y, core), core),
                    (1 - x, 1 - y, core))

        def slot(a, blk):
            return outs[a].at[4 * blk[0] + 2 * blk[1] + blk[2]]

        def copy(a, k, blk, to, src=None):
            return pltpu.make_async_remote_copy(
                src_ref=slot(a, blk) if src is None else src, dst_ref=slot(a, blk),
                send_sem=send_sems.at[a, k], recv_sem=recv_sems.at[a, k], device_id=to, device_id_type=MESH)

        mine = [pltpu.make_async_copy(ins[a], slot(a, me), local_sems.at[a]) for a in range(n)]
        return me, sib, ring, copy, mine

    def first_copies(ins, outs, sems):
        me, sib, ring, copy, mine = parts(ins, outs, sems)
        src, dst, _ = ring(me[2])
        return mine, [copy(a, k, me, to, src=ins[a]) for a in range(n) for k, to in ((1, src), (2, dst), (0, sib))]

    def start(ins, outs, sems):
        mine, first = first_copies(ins, outs, sems)
        for cp in mine + first:
            cp.start()

    def relay(ins, outs, sems):
        me, sib, ring, copy, _ = parts(ins, outs, sems)
        src, dst, _ = ring(me[2])
        for a in range(n):
            copy(a, 1, src, me).wait_recv()
            copy(a, 3, src, dst).start()
            copy(a, 4, src, sib).start()
        for a in range(n):
            copy(a, 2, dst, me).wait_recv()
            copy(a, 5, dst, sib).start()

    def finish(ins, outs, sems):
        me, sib, ring, copy, _ = parts(ins, outs, sems)
        mine, first = first_copies(ins, outs, sems)
        src, dst, diag = ring(me[2])
        passed = [copy(a, k, blk, to) for a in range(n) for k, blk, to in ((3, src, dst), (4, src, sib), (5, dst, sib))]
        for a in range(n):
            copy(a, 3, diag, me).wait_recv()
            fwd = copy(a, 6, diag, sib)
            fwd.start()
            passed.append(fwd)
        s_src, s_dst, s_diag = ring(sib[2])
        for a in range(n):
            for k, blk in ((0, sib), (4, s_src), (5, s_dst), (6, s_diag)):
                copy(a, k, blk, me).wait_recv()
        for cp in first + passed:
            cp.wait_send()
        for cp in mine:
            cp.wait()

    return _Comm(list(shards), [jax.ShapeDtypeStruct((N_DEV,) + s.shape, s.dtype) for s in shards],
                 [pltpu.SemaphoreType.DMA((n, 7)), pltpu.SemaphoreType.DMA((n, 7)), pltpu.SemaphoreType.DMA((n,))],
                 start, finish, relay)


COLUMN_SHARDED = ("w_in", "w_ff1")
COLUMN_RELAID = ("w_att_up",)


def _gather(shards, names):
    def unpack(got):
        return [jnp.transpose(g, (1, 0, 2)).reshape(g.shape[1], N_DEV * g.shape[2]) if k in COLUMN_RELAID
                else g.reshape(N_DEV * g.shape[1], g.shape[2]) for k, g in zip(names, got)]

    return _gather_relay_comm([shards[k] for k in names]), unpack


def _pair_comm(grads):
    n = len(grads)

    def copies(g, r1, sems):
        send_sems, recv_sems = sems
        x, y, c = _coords()
        return [pltpu.make_async_remote_copy(
            src_ref=g[a].at[2 * j + (1 - c)], dst_ref=r1[a].at[j], send_sem=send_sems.at[a, j],
            recv_sem=recv_sems.at[a, j], device_id=(x, y, 1 - c), device_id_type=MESH)
            for a in range(n) for j in range(4)]

    def start(g, r1, sems):
        for cp in copies(g, r1, sems):
            cp.start()

    def finish(g, r1, sems):
        for cp in copies(g, r1, sems):
            cp.wait_recv()
        for cp in copies(g, r1, sems):
            cp.wait_send()

    return _Comm(list(grads), [jax.ShapeDtypeStruct((4,) + g.shape[1:], g.dtype) for g in grads],
                 [pltpu.SemaphoreType.DMA((n, 4)), pltpu.SemaphoreType.DMA((n, 4))], start, finish)


def _small_comm(small):
    def parts(ins, outs, sems):
        (small_ref,), (small_all,) = ins, outs
        ssend, srecv, lsem = sems
        x, y, c = _coords()
        me_idx = 4 * x + 2 * y + c
        own = pltpu.make_async_copy(small_ref, small_all.at[me_idx], lsem)
        sends, recvs = [], []
        for r in range(1, N_DEV):
            px, py, pc = _flip(x, r & 4), _flip(y, r & 2), _flip(c, r & 1)
            sends.append(pltpu.make_async_remote_copy(
                src_ref=small_ref, dst_ref=small_all.at[me_idx], send_sem=ssend.at[r - 1], recv_sem=srecv.at[r - 1],
                device_id=(px, py, pc), device_id_type=MESH))
            recvs.append(pltpu.make_async_remote_copy(
                src_ref=small_ref, dst_ref=small_all.at[4 * px + 2 * py + pc], send_sem=ssend.at[r - 1],
                recv_sem=srecv.at[r - 1], device_id=(px, py, pc), device_id_type=MESH))
        return own, sends, recvs

    def start(ins, outs, sems):
        own, sends, _ = parts(ins, outs, sems)
        own.start()
        for cp in sends:
            cp.start()

    def finish(ins, outs, sems):
        own, sends, recvs = parts(ins, outs, sems)
        for cp in recvs:
            cp.wait_recv()
        for cp in sends:
            cp.wait_send()
        own.wait()

    return _Comm([small], [jax.ShapeDtypeStruct((N_DEV,) + small.shape, small.dtype)],
                 [pltpu.SemaphoreType.DMA((N_DEV - 1,)), pltpu.SemaphoreType.DMA((N_DEV - 1,)),
                  pltpu.SemaphoreType.DMA], start, finish)


def _pair_sum(grads, r1s, c_arr, tag):
    n = len(grads)
    q = 1

    def body(c_ref, *refs):
        g, r, t = refs[:n], refs[n:2 * n], refs[2 * n:]
        for a in range(n):
            t[a][...] = (g[a][...] + r[a][...]).astype(t[a].dtype)

    def blk(arr):
        return (1, arr.shape[1] // q, arr.shape[2])

    grid_spec = pltpu.PrefetchScalarGridSpec(
        num_scalar_prefetch=1, grid=(4, q),
        in_specs=[pl.BlockSpec(blk(g), lambda j, s, c_ref: (2 * j + c_ref[0], s, 0)) for g in grads]
        + [pl.BlockSpec(blk(r), lambda j, s, c_ref: (j, s, 0)) for r in r1s],
        out_specs=[pl.BlockSpec(blk(r), lambda j, s, c_ref: (j, s, 0)) for r in r1s])
    return pl.pallas_call(
        body, name="pair_sum_" + tag, grid_spec=grid_spec,
        out_shape=[jax.ShapeDtypeStruct(r.shape, RS_PAYLOAD) for r in r1s],
        compiler_params=_params(2),
    )(c_arr, *grads, *r1s)


def _chip_comm(ts):
    n = len(ts)

    def copies(t, r2, sems):
        send_sems, recv_sems = sems
        x, y, c = _coords()
        out = []
        for a in range(n):
            for r in range(1, 4):
                tx, ty = _flip(x, r & 2), _flip(y, r & 1)
                out.append(pltpu.make_async_remote_copy(
                    src_ref=t[a].at[2 * tx + ty], dst_ref=r2[a].at[r - 1], send_sem=send_sems.at[a, r - 1],
                    recv_sem=recv_sems.at[a, r - 1], device_id=(tx, ty, c), device_id_type=MESH))
        return out

    def start(t, r2, sems):
        for cp in copies(t, r2, sems):
            cp.start()

    def finish(t, r2, sems):
        for cp in copies(t, r2, sems):
            cp.wait_recv()
        for cp in copies(t, r2, sems):
            cp.wait_send()

    return _Comm(list(ts), [jax.ShapeDtypeStruct((3,) + t.shape[1:], t.dtype) for t in ts],
                 [pltpu.SemaphoreType.DMA((n, 3)), pltpu.SemaphoreType.DMA((n, 3))], start, finish)


class _Scatter:
    def __init__(self, grads, tag, sent=None):
        def stack(k, g):
            if k in COLUMN_RELAID:
                return jnp.transpose(g.reshape(g.shape[0], N_DEV, g.shape[1] // N_DEV), (1, 0, 2))
            return g.reshape(N_DEV, g.shape[0] // N_DEV, g.shape[1])

        self.names, self.tag = tuple(grads), tag
        self.stacks = [stack(k, g) for k, g in grads.items()]
        self.sent = self.stacks if sent is None else [stack(k, sent[k]) for k in grads]

    def pair_comm(self):
        return _pair_comm(self.sent)

    def pair_done(self, r1s):
        _Scatter.pairs_done((self,), (r1s,), self.tag)

    @staticmethod
    def pairs_done(groups, r1s_of, tag):
        for grp, r1s in zip(groups, r1s_of):
            grp.r1s = list(r1s)
        c_arr = jnp.reshape(lax.axis_index("c"), (1,)).astype(jnp.int32)
        ts = _pair_sum([s for grp in groups for s in grp.stacks], [r for grp in groups for r in grp.r1s], c_arr, tag)
        for grp in groups:
            grp.ts, ts = ts[:len(grp.stacks)], ts[len(grp.stacks):]

    def chip_comm(self):
        return _chip_comm(self.ts)

    def chip_done(self, r2s):
        self.r2s = list(r2s)


def _adamw(w, g, m, v):
    m = ADAM_B1 * m + (1.0 - ADAM_B1) * g
    v = ADAM_B2 * v + (1.0 - ADAM_B2) * jnp.square(g)
    m_hat = m / (1.0 - ADAM_B1 ** ADAM_STEP)
    v_hat = v / (1.0 - ADAM_B2 ** ADAM_STEP)
    delta = -ADAM_LR * (m_hat / (jnp.sqrt(v_hat) + ADAM_EPS) + ADAM_WD * w)
    return delta, m, v


ADAM_STEPS = 4


def _piece_specs(stacks):
    def rows(s):
        return s.shape[1] // ADAM_STEPS
    return ([pl.BlockSpec((1, rows(s), s.shape[2]), lambda i, idx_ref: (idx_ref[0], i, 0)) for s in stacks]
            + [pl.BlockSpec((1, rows(s), s.shape[2]), lambda i, idx_ref: (idx_ref[1], i, 0)) for s in stacks]
            + [pl.BlockSpec((3, rows(s), s.shape[2]), lambda i, idx_ref: (0, i, 0)) for s in stacks])


def _piece_sum(g0, r1, r2):
    return (((g0[0] + r1[0]) + r2[0].astype(F32)) + r2[1].astype(F32)) + r2[2].astype(F32)


def _shard_sum(stacks, r1s, r2s, idx_arr, small):
    n = len(stacks)
    comm = _small_comm(small)

    def body(idx_ref, *refs):
        g0, r1, r2 = (refs[k * n:(k + 1) * n] for k in range(3))
        small_ref, outs, small_all, sems = refs[3 * n], refs[3 * n + 1:4 * n + 1], refs[4 * n + 1], refs[4 * n + 2:]
        step = pl.program_id(0)

        @pl.when(step == 0)
        def _():
            comm.start([small_ref], [small_all], sems)

        for a in range(n):
            outs[a][...] = _piece_sum(g0[a], r1[a], r2[a])

        @pl.when(step == ADAM_STEPS - 1)
        def _():
            comm.finish([small_ref], [small_all], sems)

    out_specs = [pl.BlockSpec((s.shape[1] // ADAM_STEPS, s.shape[2]), lambda i, idx_ref: (i, 0)) for s in stacks]
    grid_spec = pltpu.PrefetchScalarGridSpec(num_scalar_prefetch=1, grid=(ADAM_STEPS,),
                                             in_specs=_piece_specs(stacks) + [ANY], out_specs=out_specs + [ANY],
                                             scratch_shapes=list(comm.scratch))
    res = pl.pallas_call(
        body, name="shard_sum", grid_spec=grid_spec,
        out_shape=[jax.ShapeDtypeStruct(s.shape[1:], F32) for s in stacks] + list(comm.out_shapes),
        compiler_params=_params(1),
    )(idx_arr, *stacks, *r1s, *r2s, small)
    return res[:n], res[n]


def _adam_big(pieces, summed, ws, ms, vs, idx_arr):
    stacks, r1s, r2s = pieces
    n_p, n = len(stacks), len(ws)

    def body(idx_ref, *refs):
        it = iter(refs)
        g0, r1, r2, gs, w, m, v = ([next(it) for _ in range(k)] for k in (n_p, n_p, n_p, n - n_p, n, n, n))
        outs = list(it)
        for a in range(n):
            g = _piece_sum(g0[a], r1[a], r2[a]) if a < n_p else gs[a - n_p][...]
            delta, nm, nv = _adamw(w[a][...], g, m[a][...], v[a][...])
            outs[4 * a][...] = g
            outs[4 * a + 1][...] = delta
            outs[4 * a + 2][...] = nm
            outs[4 * a + 3][...] = nv

    def shard_spec(w):
        return pl.BlockSpec((w.shape[0] // ADAM_STEPS, w.shape[1]), lambda i, idx_ref: (i, 0))

    in_specs = _piece_specs(stacks) + [shard_spec(w) for w in ws[n_p:]] + [shard_spec(w) for w in ws] * 3
    out_specs = [shard_spec(w) for w in ws for _ in range(4)]
    grid_spec = pltpu.PrefetchScalarGridSpec(num_scalar_prefetch=1, grid=(ADAM_STEPS,), in_specs=in_specs,
                                             out_specs=out_specs)
    return pl.pallas_call(
        body, name="adam_big", grid_spec=grid_spec,
        out_shape=[jax.ShapeDtypeStruct(w.shape, F32) for w in ws for _ in range(4)],
        compiler_params=_params(1),
    )(idx_arr, *stacks, *r1s, *r2s, *summed, *ws, *ms, *vs)


def _adam_small(small_all, w, m, v):
    def body(all_ref, w_ref, m_ref, v_ref, g_ref, d_ref, nm_ref, nv_ref):
        g = all_ref[0]
        for k in range(1, N_DEV):
            g = g + all_ref[k]
        delta, nm, nv = _adamw(w_ref[...], g, m_ref[...], v_ref[...])
        g_ref[...] = g
        d_ref[...] = delta
        nm_ref[...] = nm
        nv_ref[...] = nv

    return pl.pallas_call(
        body, name="adam_small",
        in_specs=[VMEM_FULL] * 4, out_specs=[VMEM_FULL] * 4,
        out_shape=[jax.ShapeDtypeStruct(w.shape, F32)] * 4,
    )(small_all, w, m, v)


RS_PAYLOAD = BF


def _pack_rows(rows):
    rid = lax.broadcasted_iota(jnp.int32, (8, D_MODEL), 0)
    out = jnp.zeros((8, D_MODEL), F32)
    for i, r in enumerate(rows):
        out = jnp.where(rid == i, jnp.broadcast_to(r, (8, D_MODEL)), out)
    return out


def _small_rows(norm_mix_gain, b_gates, attn_sinks, ret_gn_gain, norm_mlp_gain, norm_final_gain):
    return [norm_mix_gain, b_gates[:, :D_MODEL], b_gates[:, D_MODEL:], ret_gn_gain, norm_mlp_gain,
            norm_final_gain.reshape(1, D_MODEL), jnp.pad(attn_sinks, ((0, 0), (0, D_MODEL - ATT_HEADS)))]


def _unpack_small(p):
    return dict(norm_mix_gain=p[0:1], b_gates=jnp.concatenate([p[1:2], p[2:3]], axis=1), ret_gn_gain=p[3:4],
                norm_mlp_gain=p[4:5], norm_final_gain=p[5], attn_sinks=p[6:7, :ATT_HEADS])


WEIGHTS = ("norm_mix_gain", "w_in", "b_gates", "attn_sinks", "ret_gn_gain", "w_att_up", "w_ret_up", "w_out",
           "norm_mlp_gain", "w_ff1", "w_ff2", "norm_final_gain")
BIG = ("w_in", "w_att_up", "w_ret_up", "w_out", "w_ff1", "w_ff2")


def kernel(x, norm_mix_gain, w_in, b_gates, attn_sinks, ret_gn_gain, w_att_up, w_ret_up, w_out, norm_mlp_gain, w_ff1, w_ff2, norm_final_gain, loss_target, m_norm_mix_gain, m_w_in, m_b_gates, m_attn_sinks, m_ret_gn_gain, m_w_att_up, m_w_ret_up, m_w_out, m_norm_mlp_gain, m_w_ff1, m_w_ff2, m_norm_final_gain, v_norm_mix_gain, v_w_in, v_b_gates, v_attn_sinks, v_ret_gn_gain, v_w_att_up, v_w_ret_up, v_w_out, v_norm_mlp_gain, v_w_ff1, v_w_ff2, v_norm_final_gain):
    w = dict(norm_mix_gain=norm_mix_gain, w_in=w_in, b_gates=b_gates, attn_sinks=attn_sinks, ret_gn_gain=ret_gn_gain,
             w_att_up=w_att_up, w_ret_up=w_ret_up, w_out=w_out, norm_mlp_gain=norm_mlp_gain, w_ff1=w_ff1,
             w_ff2=w_ff2, norm_final_gain=norm_final_gain)
    m = dict(norm_mix_gain=m_norm_mix_gain, w_in=m_w_in, b_gates=m_b_gates, attn_sinks=m_attn_sinks,
             ret_gn_gain=m_ret_gn_gain, w_att_up=m_w_att_up, w_ret_up=m_w_ret_up, w_out=m_w_out,
             norm_mlp_gain=m_norm_mlp_gain, w_ff1=m_w_ff1, w_ff2=m_w_ff2, norm_final_gain=m_norm_final_gain)
    v = dict(norm_mix_gain=v_norm_mix_gain, w_in=v_w_in, b_gates=v_b_gates, attn_sinks=v_attn_sinks,
             ret_gn_gain=v_ret_gn_gain, w_att_up=v_w_att_up, w_ret_up=v_w_ret_up, w_out=v_w_out,
             norm_mlp_gain=v_norm_mlp_gain, w_ff1=v_w_ff1, w_ff2=v_w_ff2, norm_final_gain=v_norm_final_gain)

    shards = {k: (w[k][0].T if k in COLUMN_SHARDED else w[k][0]).astype(BF) for k in BIG}
    loss_p, dx, small, groups = _local_step(
        x[0], loss_target[0], norm_mix_gain, b_gates, attn_sinks[0], ret_gn_gain, norm_mlp_gain,
        norm_final_gain.reshape(1, D_MODEL), shards)

    by_name = {}
    for grp in groups:
        for k, stack, r1, r2 in zip(grp.names, grp.stacks, grp.r1s, grp.r2s):
            by_name[k] = (stack, r1, r2)
    lane = lax.broadcasted_iota(jnp.int32, (1, D_MODEL), 1)
    sink_row = jnp.where(lane < ATT_HEADS, jnp.pad(small["attn_sinks"], ((0, 0), (0, D_MODEL - LANE))),
                         jnp.where(lane == ATT_HEADS, jnp.pad(loss_p, ((0, 0), (0, D_MODEL - LANE)), mode="edge"),
                                   0.0))
    small_pack = _pack_rows([small["norm_mix_gain"], small["b_gates"][:, :D_MODEL], small["b_gates"][:, D_MODEL:],
                             small["ret_gn_gain"], small["norm_mlp_gain"], small["norm_final_gain"], sink_row])
    xi, yi, ci = _coords()
    idx_arr = jnp.stack([4 * xi + 2 * yi + ci, 2 * xi + yi]).astype(jnp.int32)
    in_pieces = ["w_att_up", "w_ret_up", "w_out", "w_ff2", "w_in"]
    in_sum = ["w_ff1"]
    order = in_pieces + in_sum
    summed_t, small_all = _shard_sum(*[[by_name[k][i] for k in in_sum] for i in range(3)], idx_arr, small_pack)

    def shard(tree, k):
        return tree[k][0].T if k == "w_in" else tree[k][0]

    adam_out = _adam_big([[by_name[k][i] for k in in_pieces] for i in range(3)], [g.T for g in summed_t],
                         [shard(w, k) for k in order], [shard(m, k) for k in order], [shard(v, k) for k in order],
                         idx_arr)
    big_out = [adam_out[4 * order.index(k) + i].T if k == "w_in" else adam_out[4 * order.index(k) + i]
               for k in BIG for i in range(4)]
    sm_g, sm_d, sm_m, sm_v = _adam_small(small_all, _pack_rows(_small_rows(*[w[k] for k in WEIGHTS if k not in BIG])),
                                         _pack_rows(_small_rows(*[m[k] for k in WEIGHTS if k not in BIG])),
                                         _pack_rows(_small_rows(*[v[k] for k in WEIGHTS if k not in BIG])))

    loss = sm_g[6, ATT_HEADS]
    kinds = []
    for idx, packed in enumerate((sm_g, sm_d, sm_m, sm_v)):
        out = _unpack_small(packed)
        for a, k in enumerate(BIG):
            out[k] = big_out[4 * a + idx][None]
        kinds.append(out)
    return (loss, dx[None], *[kinds[0][k] for k in WEIGHTS], *[kinds[1][k] for k in WEIGHTS],
            *[kinds[2][k] for k in WEIGHTS], *[kinds[3][k] for k in WEIGHTS])
```
